```python
import jax, jax.numpy as jnp
from jax import lax
import numpy as np

D_MODEL = 2048
BATCH = 8
SEQ = 4096
DEPTH = 2

N_MEM = 256
EPS = 1e-6
N_EVEN = (DEPTH + 1) // 2
N_ODD = DEPTH // 2
MIX_A = D_MODEL // 2
POOL_WINDOWS = (2, 4, 8, 16)
N_POOL_GROUPS = len(POOL_WINDOWS)
POOL_GROUP = MIX_A // N_POOL_GROUPS
MIX_B = D_MODEL - MIX_A
HG_HEAD = 128
HG_HEADS = MIX_B // HG_HEAD
HG_CHUNK = 64
IN_EVEN = MIX_A + 4 * MIX_B
FOX_HEAD = 128
FOX_HEADS = D_MODEL // FOX_HEAD
FOX_BLOCK = 128
IN_ODD = 3 * D_MODEL + FOX_HEADS
XA_HEADS = 4
XA_HEAD = D_MODEL // XA_HEADS
D_FF = -(-8 * D_MODEL // (3 * 256)) * 256

kernel_name = "hybrid_pool_hgrn2_fox_trunk"


def rmsnorm(x, g):
    xf = x.astype(jnp.float32)
    y = xf * lax.rsqrt(jnp.mean(xf * xf, axis=-1, keepdims=True) + EPS)
    return (y * g.astype(jnp.float32)).astype(x.dtype)


def pool_mixer(u, w_pool, pool_scale):
    B, T, _ = u.shape
    uf = u.astype(jnp.float32)
    c = jnp.pad(jnp.cumsum(uf, axis=1), ((0, 0), (1, 0), (0, 0)))
    t = jnp.arange(T)
    outs = []
    for gi, w in enumerate(POOL_WINDOWS):
        cg = c[:, :, gi * POOL_GROUP:(gi + 1) * POOL_GROUP]
        c_lag = jnp.pad(cg, ((0, 0), (w - 1, 0), (0, 0)))[:, :T]
        cnt = jnp.minimum(t + 1, w).astype(jnp.float32)[None, :, None]
        mean = (cg[:, 1:] - c_lag) / cnt
        outs.append(mean - uf[:, :, gi * POOL_GROUP:(gi + 1) * POOL_GROUP])
    p = jnp.stack(outs, axis=2)
    y = jnp.einsum('btgc,gcd->btgd', p, w_pool.astype(jnp.float32)).reshape(B, T, MIX_A)
    return (y * pool_scale.astype(jnp.float32)).astype(u.dtype)


def hgrn2_mixer(q, fl, i, g, lb, norm_g):
    B, T, _ = q.shape
    H, Dh, C = HG_HEADS, HG_HEAD, HG_CHUNK
    N = T // C
    f = lb + (1.0 - lb) * jax.nn.sigmoid(fl.astype(jnp.float32))
    logf = jnp.log(f)
    k = 1.0 - f
    qf = jax.nn.silu(q.astype(jnp.float32)) * (Dh ** -0.5)

    def to_chunks(a):
        return a.reshape(B, N, C, H, Dh).transpose(1, 0, 3, 2, 4)

    qc, kc, vc = to_chunks(qf), to_chunks(k), to_chunks(i.astype(jnp.float32))
    bc = jnp.cumsum(to_chunks(logf), axis=3)
    causal = jnp.tril(jnp.ones((C, C), dtype=bool))[:, :, None]

    def step(S, inp):
        qh, kh, vh, bh = inp
        diff = bh[:, :, :, None, :] - bh[:, :, None, :, :]
        decay = jnp.exp(jnp.where(causal, diff, -jnp.inf))
        A = jnp.einsum('bhtk,bhsk,bhtsk->bhts', qh, kh, decay)
        o = (jnp.einsum('bhts,bhsv->bhtv', A, vh)
             + jnp.einsum('bhtk,bhkv->bhtv', qh * jnp.exp(bh), S))
        b_last = bh[:, :, -1:, :]
        S = (jnp.exp(b_last[:, :, 0, :])[..., None] * S
             + jnp.einsum('bhsk,bhsv->bhkv', kh * jnp.exp(b_last - bh), vh))
        return S, o

    S0 = jnp.zeros((B, H, Dh, Dh), jnp.float32)
    _, o = lax.scan(step, S0, (qc, kc, vc, bc))
    o = o.transpose(1, 0, 3, 2, 4).reshape(B, T, H, Dh)
    o = rmsnorm(o, norm_g).reshape(B, T, MIX_B)
    return (o * jax.nn.silu(g.astype(jnp.float32))).astype(q.dtype)


def fox_attention(q, k, v, fl):
    B, T, H, Dh = q.shape
    Fc = jnp.cumsum(jax.nn.log_sigmoid(fl.astype(jnp.float32)), axis=1).transpose(0, 2, 1)
    scale = Dh ** -0.5
    outs = []
    for blk in range(T // FOX_BLOCK):
        q0, q1 = blk * FOX_BLOCK, (blk + 1) * FOX_BLOCK
        s = jnp.einsum('bqhd,bkhd->bhqk', q[:, q0:q1], k[:, :q1]).astype(jnp.float32) * scale
        s = s + (Fc[:, :, q0:q1, None] - Fc[:, :, None, :q1])
        mask = (q0 + jnp.arange(FOX_BLOCK))[:, None] >= jnp.arange(q1)[None, :]
        p = jax.nn.softmax(jnp.where(mask, s, -jnp.inf), axis=-1)
        outs.append(jnp.einsum('bhqk,bkhd->bqhd', p.astype(v.dtype), v[:, :q1]))
    return jnp.concatenate(outs, axis=1).reshape(B, T, H * Dh)


def cross_attention(h, mem_n, wq, wkv, wo):
    B, T, _ = h.shape
    M = mem_n.shape[1]
    q = (h @ wq).reshape(B, T, XA_HEADS, XA_HEAD)
    kv = mem_n @ wkv
    k = kv[..., :D_MODEL].reshape(B, M, XA_HEADS, XA_HEAD)
    v = kv[..., D_MODEL:].reshape(B, M, XA_HEADS, XA_HEAD)
    s = jnp.einsum('bthd,bmhd->bhtm', q, k).astype(jnp.float32) * (XA_HEAD ** -0.5)
    p = jax.nn.softmax(s, axis=-1)
    o = jnp.einsum('bhtm,bmhd->bthd', p.astype(v.dtype), v).reshape(B, T, D_MODEL)
    return o @ wo


def _fwd_setup_inputs(seed: int = 0) -> dict:
    key = jax.random.key(seed)
    ks = jax.random.split(key, 24)
    D = D_MODEL

    def nrm(k, shape, s):
        return jax.random.normal(k, shape, jnp.float32) * s

    def gain(k, shape):
        return 1.0 + 0.05 * jax.random.normal(k, shape, jnp.float32)

    return {
        "x": nrm(ks[0], (BATCH, SEQ, D), 1.0),
        "mem": nrm(ks[1], (BATCH, N_MEM, D), 1.0),
        "lb_table": nrm(ks[2], (DEPTH + 1, MIX_B), 0.5),
        "ev_norm": gain(ks[3], (N_EVEN, D)),
        "ev_w_in": nrm(ks[4], (N_EVEN, D, IN_EVEN), D ** -0.5),
        "ev_w_pool": nrm(ks[5], (N_EVEN, N_POOL_GROUPS, POOL_GROUP, POOL_GROUP), POOL_GROUP ** -0.5),
        "ev_pool_scale": gain(ks[6], (N_EVEN, MIX_A)),
        "ev_hg_norm": gain(ks[7], (N_EVEN, HG_HEAD)),
        "ev_w_out": nrm(ks[8], (N_EVEN, D, D), D ** -0.5),
        "od_norm": gain(ks[9], (N_ODD, D)),
        "od_w_in": nrm(ks[10], (N_ODD, D, IN_ODD), D ** -0.5),
        "od_b_f": 2.0 + nrm(ks[11], (N_ODD, FOX_HEADS), 0.1),
        "od_w_out": nrm(ks[12], (N_ODD, D, D), D ** -0.5),
        "xa_norm": gain(ks[13], (DEPTH, D)),
        "xa_mem_norm": gain(ks[14], (DEPTH, D)),
        "xa_wq": nrm(ks[15], (DEPTH, D, D), D ** -0.5),
        "xa_wkv": nrm(ks[16], (DEPTH, D, 2 * D), D ** -0.5),
        "xa_wo": nrm(ks[17], (DEPTH, D, D), D ** -0.5),
        "ffn_norm": gain(ks[18], (DEPTH, D)),
        "ffn_w_gate": nrm(ks[19], (DEPTH, D, D_FF), D ** -0.5),
        "ffn_w_up": nrm(ks[20], (DEPTH, D, D_FF), D ** -0.5),
        "ffn_w_down": nrm(ks[21], (DEPTH, D_FF, D), D_FF ** -0.5),
        "final_norm": gain(ks[22], (D,)),
    }


def _fwd_reference(x, mem, lb_table, ev_norm, ev_w_in, ev_w_pool, ev_pool_scale, ev_hg_norm, ev_w_out,
              od_norm, od_w_in, od_b_f, od_w_out, xa_norm, xa_mem_norm, xa_wq, xa_wkv, xa_wo,
              ffn_norm, ffn_w_gate, ffn_w_up, ffn_w_down, final_norm):
    B, T, D = x.shape
    lb_cum = jnp.cumsum(jax.nn.softmax(lb_table.astype(jnp.float32), axis=0), axis=0)
    for l in range(DEPTH):
        if l % 2 == 0:
            e = l // 2
            h = rmsnorm(x, ev_norm[e])
            z = h @ ev_w_in[e]
            u = z[..., :MIX_A]
            q, fl, i, g = jnp.split(z[..., MIX_A:], 4, axis=-1)
            ya = pool_mixer(u, ev_w_pool[e], ev_pool_scale[e])
            yb = hgrn2_mixer(q, fl, i, g, lb_cum[l + 1] - lb_cum[0], ev_hg_norm[e])
            x = x + jnp.concatenate([ya, yb], axis=-1) @ ev_w_out[e]
        else:
            o = l // 2
            h = rmsnorm(x, od_norm[o])
            z = h @ od_w_in[o]
            q = z[..., :D].reshape(B, T, FOX_HEADS, FOX_HEAD)
            k = z[..., D:2 * D].reshape(B, T, FOX_HEADS, FOX_HEAD)
            v = z[..., 2 * D:3 * D].reshape(B, T, FOX_HEADS, FOX_HEAD)
            fl = z[..., 3 * D:] + od_b_f[o]
            x = x + fox_attention(q, k, v, fl) @ od_w_out[o]
        h = rmsnorm(x, xa_norm[l])
        x = x + cross_attention(h, rmsnorm(mem, xa_mem_norm[l]), xa_wq[l], xa_wkv[l], xa_wo[l])
        h = rmsnorm(x, ffn_norm[l])
        x = x + (jax.nn.silu(h @ ffn_w_gate[l]) * (h @ ffn_w_up[l])) @ ffn_w_down[l]
    return rmsnorm(x, final_norm)


import jax as _jax
import jax.numpy as _jnp

TWIN_FORMAT = 'train_step'
FWD_PARAMS = ['x', 'mem', 'lb_table', 'ev_norm', 'ev_w_in', 'ev_w_pool', 'ev_pool_scale', 'ev_hg_norm', 'ev_w_out', 'od_norm', 'od_w_in', 'od_b_f', 'od_w_out', 'xa_norm', 'xa_mem_norm', 'xa_wq', 'xa_wkv', 'xa_wo', 'ffn_norm', 'ffn_w_gate', 'ffn_w_up', 'ffn_w_down', 'final_norm']
TWIN_WEIGHTS = ['lb_table', 'ev_norm', 'ev_w_in', 'ev_w_pool', 'ev_pool_scale', 'ev_hg_norm', 'ev_w_out', 'od_norm', 'od_w_in', 'od_b_f', 'od_w_out', 'xa_norm', 'xa_mem_norm', 'xa_wq', 'xa_wkv', 'xa_wo', 'ffn_norm', 'ffn_w_gate', 'ffn_w_up', 'ffn_w_down', 'final_norm']
TWIN_DIFF_INPUT = 'x'
TWIN_INPUTS = ['x', 'mem', 'lb_table', 'ev_norm', 'ev_w_in', 'ev_w_pool', 'ev_pool_scale', 'ev_hg_norm', 'ev_w_out', 'od_norm', 'od_w_in', 'od_b_f', 'od_w_out', 'xa_norm', 'xa_mem_norm', 'xa_wq', 'xa_wkv', 'xa_wo', 'ffn_norm', 'ffn_w_gate', 'ffn_w_up', 'ffn_w_down', 'final_norm', 'loss_target', 'm_lb_table', 'm_ev_norm', 'm_ev_w_in', 'm_ev_w_pool', 'm_ev_pool_scale', 'm_ev_hg_norm', 'm_ev_w_out', 'm_od_norm', 'm_od_w_in', 'm_od_b_f', 'm_od_w_out', 'm_xa_norm', 'm_xa_mem_norm', 'm_xa_wq', 'm_xa_wkv', 'm_xa_wo', 'm_ffn_norm', 'm_ffn_w_gate', 'm_ffn_w_up', 'm_ffn_w_down', 'm_final_norm', 'v_lb_table', 'v_ev_norm', 'v_ev_w_in', 'v_ev_w_pool', 'v_ev_pool_scale', 'v_ev_hg_norm', 'v_ev_w_out', 'v_od_norm', 'v_od_w_in', 'v_od_b_f', 'v_od_w_out', 'v_xa_norm', 'v_xa_mem_norm', 'v_xa_wq', 'v_xa_wkv', 'v_xa_wo', 'v_ffn_norm', 'v_ffn_w_gate', 'v_ffn_w_up', 'v_ffn_w_down', 'v_final_norm']
TWIN_OUTPUTS = ['loss', 'grad_x', 'grad_lb_table', 'grad_ev_norm', 'grad_ev_w_in', 'grad_ev_w_pool', 'grad_ev_pool_scale', 'grad_ev_hg_norm', 'grad_ev_w_out', 'grad_od_norm', 'grad_od_w_in', 'grad_od_b_f', 'grad_od_w_out', 'grad_xa_norm', 'grad_xa_mem_norm', 'grad_xa_wq', 'grad_xa_wkv', 'grad_xa_wo', 'grad_ffn_norm', 'grad_ffn_w_gate', 'grad_ffn_w_up', 'grad_ffn_w_down', 'grad_final_norm', 'delta_lb_table', 'delta_ev_norm', 'delta_ev_w_in', 'delta_ev_w_pool', 'delta_ev_pool_scale', 'delta_ev_hg_norm', 'delta_ev_w_out', 'delta_od_norm', 'delta_od_w_in', 'delta_od_b_f', 'delta_od_w_out', 'delta_xa_norm', 'delta_xa_mem_norm', 'delta_xa_wq', 'delta_xa_wkv', 'delta_xa_wo', 'delta_ffn_norm', 'delta_ffn_w_gate', 'delta_ffn_w_up', 'delta_ffn_w_down', 'delta_final_norm', 'new_m_lb_table', 'new_m_ev_norm', 'new_m_ev_w_in', 'new_m_ev_w_pool', 'new_m_ev_pool_scale', 'new_m_ev_hg_norm', 'new_m_ev_w_out', 'new_m_od_norm', 'new_m_od_w_in', 'new_m_od_b_f', 'new_m_od_w_out', 'new_m_xa_norm', 'new_m_xa_mem_norm', 'new_m_xa_wq', 'new_m_xa_wkv', 'new_m_xa_wo', 'new_m_ffn_norm', 'new_m_ffn_w_gate', 'new_m_ffn_w_up', 'new_m_ffn_w_down', 'new_m_final_norm', 'new_v_lb_table', 'new_v_ev_norm', 'new_v_ev_w_in', 'new_v_ev_w_pool', 'new_v_ev_pool_scale', 'new_v_ev_hg_norm', 'new_v_ev_w_out', 'new_v_od_norm', 'new_v_od_w_in', 'new_v_od_b_f', 'new_v_od_w_out', 'new_v_xa_norm', 'new_v_xa_mem_norm', 'new_v_xa_wq', 'new_v_xa_wkv', 'new_v_xa_wo', 'new_v_ffn_norm', 'new_v_ffn_w_gate', 'new_v_ffn_w_up', 'new_v_ffn_w_down', 'new_v_final_norm']
TWIN_LEAF_KINDS = {'loss': 'loss', 'grad_x': 'grad_x', 'grad_lb_table': 'grad_w', 'grad_ev_norm': 'grad_w', 'grad_ev_w_in': 'grad_w', 'grad_ev_w_pool': 'grad_w', 'grad_ev_pool_scale': 'grad_w', 'grad_ev_hg_norm': 'grad_w', 'grad_ev_w_out': 'grad_w', 'grad_od_norm': 'grad_w', 'grad_od_w_in': 'grad_w', 'grad_od_b_f': 'grad_w', 'grad_od_w_out': 'grad_w', 'grad_xa_norm': 'grad_w', 'grad_xa_mem_norm': 'grad_w', 'grad_xa_wq': 'grad_w', 'grad_xa_wkv': 'grad_w', 'grad_xa_wo': 'grad_w', 'grad_ffn_norm': 'grad_w', 'grad_ffn_w_gate': 'grad_w', 'grad_ffn_w_up': 'grad_w', 'grad_ffn_w_down': 'grad_w', 'grad_final_norm': 'grad_w', 'delta_lb_table': 'delta_w', 'delta_ev_norm': 'delta_w', 'delta_ev_w_in': 'delta_w', 'delta_ev_w_pool': 'delta_w', 'delta_ev_pool_scale': 'delta_w', 'delta_ev_hg_norm': 'delta_w', 'delta_ev_w_out': 'delta_w', 'delta_od_norm': 'delta_w', 'delta_od_w_in': 'delta_w', 'delta_od_b_f': 'delta_w', 'delta_od_w_out': 'delta_w', 'delta_xa_norm': 'delta_w', 'delta_xa_mem_norm': 'delta_w', 'delta_xa_wq': 'delta_w', 'delta_xa_wkv': 'delta_w', 'delta_xa_wo': 'delta_w', 'delta_ffn_norm': 'delta_w', 'delta_ffn_w_gate': 'delta_w', 'delta_ffn_w_up': 'delta_w', 'delta_ffn_w_down': 'delta_w', 'delta_final_norm': 'delta_w', 'new_m_lb_table': 'new_m', 'new_m_ev_norm': 'new_m', 'new_m_ev_w_in': 'new_m', 'new_m_ev_w_pool': 'new_m', 'new_m_ev_pool_scale': 'new_m', 'new_m_ev_hg_norm': 'new_m', 'new_m_ev_w_out': 'new_m', 'new_m_od_norm': 'new_m', 'new_m_od_w_in': 'new_m', 'new_m_od_b_f': 'new_m', 'new_m_od_w_out': 'new_m', 'new_m_xa_norm': 'new_m', 'new_m_xa_mem_norm': 'new_m', 'new_m_xa_wq': 'new_m', 'new_m_xa_wkv': 'new_m', 'new_m_xa_wo': 'new_m', 'new_m_ffn_norm': 'new_m', 'new_m_ffn_w_gate': 'new_m', 'new_m_ffn_w_up': 'new_m', 'new_m_ffn_w_down': 'new_m', 'new_m_final_norm': 'new_m', 'new_v_lb_table': 'new_v', 'new_v_ev_norm': 'new_v', 'new_v_ev_w_in': 'new_v', 'new_v_ev_w_pool': 'new_v', 'new_v_ev_pool_scale': 'new_v', 'new_v_ev_hg_norm': 'new_v', 'new_v_ev_w_out': 'new_v', 'new_v_od_norm': 'new_v', 'new_v_od_w_in': 'new_v', 'new_v_od_b_f': 'new_v', 'new_v_od_w_out': 'new_v', 'new_v_xa_norm': 'new_v', 'new_v_xa_mem_norm': 'new_v', 'new_v_xa_wq': 'new_v', 'new_v_xa_wkv': 'new_v', 'new_v_xa_wo': 'new_v', 'new_v_ffn_norm': 'new_v', 'new_v_ffn_w_gate': 'new_v', 'new_v_ffn_w_up': 'new_v', 'new_v_ffn_w_down': 'new_v', 'new_v_final_norm': 'new_v'}


def _forward(args):
    return _fwd_reference(*[args[k] for k in FWD_PARAMS])


def _output_shape():
    def fwd():
        inp = _fwd_setup_inputs(0)
        return _fwd_reference(*[inp[k] for k in FWD_PARAMS])
    out = _jax.eval_shape(fwd)
    return out.shape, out.dtype

N_MICROBATCH = 1
ADAM_LR = 0.001
ADAM_B1 = 0.9
ADAM_B2 = 0.999
ADAM_EPS = 1e-08
ADAM_WD = 0.01
ADAM_STEP = 10
PER_EXAMPLE_BATCH_AXIS = {'x': 0, 'mem': 0, 'loss_target': 0}
SHARED_INPUTS = []
_WEIGHT_DTYPES = {'lb_table': _jnp.float32, 'ev_norm': _jnp.float32, 'ev_w_in': _jnp.float32, 'ev_w_pool': _jnp.float32, 'ev_pool_scale': _jnp.float32, 'ev_hg_norm': _jnp.float32, 'ev_w_out': _jnp.float32, 'od_norm': _jnp.float32, 'od_w_in': _jnp.float32, 'od_b_f': _jnp.float32, 'od_w_out': _jnp.float32, 'xa_norm': _jnp.float32, 'xa_mem_norm': _jnp.float32, 'xa_wq': _jnp.float32, 'xa_wkv': _jnp.float32, 'xa_wo': _jnp.float32, 'ffn_norm': _jnp.float32, 'ffn_w_gate': _jnp.float32, 'ffn_w_up': _jnp.float32, 'ffn_w_down': _jnp.float32, 'final_norm': _jnp.float32}
MOMENT_SCALE = {'lb_table': 2.459078e-03, 'ev_norm': 7.792799e-02, 'ev_w_in': 4.836474e-02, 'ev_w_pool': 7.752797e-02, 'ev_pool_scale': 7.861300e-02, 'ev_hg_norm': 1.565847e-01, 'ev_w_out': 6.651850e-02, 'od_norm': 4.167290e-02, 'od_w_in': 2.349142e-02, 'od_b_f': 2.430975e-01, 'od_w_out': 2.725661e-02, 'xa_norm': 8.031961e-03, 'xa_mem_norm': 1.194683e-02, 'xa_wq': 7.983470e-03, 'xa_wkv': 8.207335e-03, 'xa_wo': 8.425558e-03, 'ffn_norm': 5.605874e-02, 'ffn_w_gate': 2.426214e-02, 'ffn_w_up': 2.356508e-02, 'ffn_w_down': 3.911742e-02, 'final_norm': 1.600271e+01}


def _to_microbatches(a, axis):
    t = _jnp.moveaxis(a, axis, 0)
    t = t.reshape((N_MICROBATCH, t.shape[0] // N_MICROBATCH) + t.shape[1:])
    return _jnp.moveaxis(t, 1, axis + 1)


def setup_inputs(seed: int = 0) -> dict:
    inp = _fwd_setup_inputs(seed)
    key = _jax.random.fold_in(_jax.random.key(seed), 7919)
    shape, _ = _output_shape()
    out = dict(inp)
    out["loss_target"] = _jax.random.normal(_jax.random.fold_in(key, 0), shape, _jnp.float32)
    for i, name in enumerate(TWIN_WEIGHTS):
        w = inp[name].astype(_jnp.float32)
        if MOMENT_SCALE is None:
            s = _jnp.sqrt(_jnp.mean(_jnp.square(w)) + 1e-30)
        else:
            s = MOMENT_SCALE[name]
        km, kv = _jax.random.split(_jax.random.fold_in(key, i + 1))
        out[name] = w
        out["m_" + name] = s * _jax.random.normal(km, w.shape, _jnp.float32)
        out["v_" + name] = (s * s) * _jax.random.uniform(kv, w.shape, _jnp.float32, 0.5, 1.5)
    if N_MICROBATCH > 1:
        for name, axis in PER_EXAMPLE_BATCH_AXIS.items():
            out[name] = _to_microbatches(out[name], axis)
    return {'x': out['x'], 'mem': out['mem'], 'lb_table': out['lb_table'], 'ev_norm': out['ev_norm'], 'ev_w_in': out['ev_w_in'], 'ev_w_pool': out['ev_w_pool'], 'ev_pool_scale': out['ev_pool_scale'], 'ev_hg_norm': out['ev_hg_norm'], 'ev_w_out': out['ev_w_out'], 'od_norm': out['od_norm'], 'od_w_in': out['od_w_in'], 'od_b_f': out['od_b_f'], 'od_w_out': out['od_w_out'], 'xa_norm': out['xa_norm'], 'xa_mem_norm': out['xa_mem_norm'], 'xa_wq': out['xa_wq'], 'xa_wkv': out['xa_wkv'], 'xa_wo': out['xa_wo'], 'ffn_norm': out['ffn_norm'], 'ffn_w_gate': out['ffn_w_gate'], 'ffn_w_up': out['ffn_w_up'], 'ffn_w_down': out['ffn_w_down'], 'final_norm': out['final_norm'], 'loss_target': out['loss_target'], 'm_lb_table': out['m_lb_table'], 'm_ev_norm': out['m_ev_norm'], 'm_ev_w_in': out['m_ev_w_in'], 'm_ev_w_pool': out['m_ev_w_pool'], 'm_ev_pool_scale': out['m_ev_pool_scale'], 'm_ev_hg_norm': out['m_ev_hg_norm'], 'm_ev_w_out': out['m_ev_w_out'], 'm_od_norm': out['m_od_norm'], 'm_od_w_in': out['m_od_w_in'], 'm_od_b_f': out['m_od_b_f'], 'm_od_w_out': out['m_od_w_out'], 'm_xa_norm': out['m_xa_norm'], 'm_xa_mem_norm': out['m_xa_mem_norm'], 'm_xa_wq': out['m_xa_wq'], 'm_xa_wkv': out['m_xa_wkv'], 'm_xa_wo': out['m_xa_wo'], 'm_ffn_norm': out['m_ffn_norm'], 'm_ffn_w_gate': out['m_ffn_w_gate'], 'm_ffn_w_up': out['m_ffn_w_up'], 'm_ffn_w_down': out['m_ffn_w_down'], 'm_final_norm': out['m_final_norm'], 'v_lb_table': out['v_lb_table'], 'v_ev_norm': out['v_ev_norm'], 'v_ev_w_in': out['v_ev_w_in'], 'v_ev_w_pool': out['v_ev_w_pool'], 'v_ev_pool_scale': out['v_ev_pool_scale'], 'v_ev_hg_norm': out['v_ev_hg_norm'], 'v_ev_w_out': out['v_ev_w_out'], 'v_od_norm': out['v_od_norm'], 'v_od_w_in': out['v_od_w_in'], 'v_od_b_f': out['v_od_b_f'], 'v_od_w_out': out['v_od_w_out'], 'v_xa_norm': out['v_xa_norm'], 'v_xa_mem_norm': out['v_xa_mem_norm'], 'v_xa_wq': out['v_xa_wq'], 'v_xa_wkv': out['v_xa_wkv'], 'v_xa_wo': out['v_xa_wo'], 'v_ffn_norm': out['v_ffn_norm'], 'v_ffn_w_gate': out['v_ffn_w_gate'], 'v_ffn_w_up': out['v_ffn_w_up'], 'v_ffn_w_down': out['v_ffn_w_down'], 'v_final_norm': out['v_final_norm']}


def _loss(weights, diff, rest, loss_target):
    with _jax.named_scope("forward"):
        args = {**rest, TWIN_DIFF_INPUT: diff, **{k: w.astype(_WEIGHT_DTYPES[k]) for k, w in weights.items()}}
        y = _forward(args)
    with _jax.named_scope("loss_head"):
        err = _jnp.square(y.astype(_jnp.float32) - loss_target)
        return 0.5 * _jnp.sum(_jnp.mean(err, axis=-1)) if err.ndim else 0.5 * err


def _adamw(w, g, m, v):
    m = ADAM_B1 * m + (1.0 - ADAM_B1) * g
    v = ADAM_B2 * v + (1.0 - ADAM_B2) * _jnp.square(g)
    m_hat = m / (1.0 - ADAM_B1 ** ADAM_STEP)
    v_hat = v / (1.0 - ADAM_B2 ** ADAM_STEP)
    delta = -ADAM_LR * (m_hat / (_jnp.sqrt(v_hat) + ADAM_EPS) + ADAM_WD * w)
    return delta, m, v


def reference(x, mem, lb_table, ev_norm, ev_w_in, ev_w_pool, ev_pool_scale, ev_hg_norm, ev_w_out, od_norm, od_w_in, od_b_f, od_w_out, xa_norm, xa_mem_norm, xa_wq, xa_wkv, xa_wo, ffn_norm, ffn_w_gate, ffn_w_up, ffn_w_down, final_norm, loss_target, m_lb_table, m_ev_norm, m_ev_w_in, m_ev_w_pool, m_ev_pool_scale, m_ev_hg_norm, m_ev_w_out, m_od_norm, m_od_w_in, m_od_b_f, m_od_w_out, m_xa_norm, m_xa_mem_norm, m_xa_wq, m_xa_wkv, m_xa_wo, m_ffn_norm, m_ffn_w_gate, m_ffn_w_up, m_ffn_w_down, m_final_norm, v_lb_table, v_ev_norm, v_ev_w_in, v_ev_w_pool, v_ev_pool_scale, v_ev_hg_norm, v_ev_w_out, v_od_norm, v_od_w_in, v_od_b_f, v_od_w_out, v_xa_norm, v_xa_mem_norm, v_xa_wq, v_xa_wkv, v_xa_wo, v_ffn_norm, v_ffn_w_gate, v_ffn_w_up, v_ffn_w_down, v_final_norm):
    given = dict(x=x, mem=mem, lb_table=lb_table, ev_norm=ev_norm, ev_w_in=ev_w_in, ev_w_pool=ev_w_pool, ev_pool_scale=ev_pool_scale, ev_hg_norm=ev_hg_norm, ev_w_out=ev_w_out, od_norm=od_norm, od_w_in=od_w_in, od_b_f=od_b_f, od_w_out=od_w_out, xa_norm=xa_norm, xa_mem_norm=xa_mem_norm, xa_wq=xa_wq, xa_wkv=xa_wkv, xa_wo=xa_wo, ffn_norm=ffn_norm, ffn_w_gate=ffn_w_gate, ffn_w_up=ffn_w_up, ffn_w_down=ffn_w_down, final_norm=final_norm, loss_target=loss_target, m_lb_table=m_lb_table, m_ev_norm=m_ev_norm, m_ev_w_in=m_ev_w_in, m_ev_w_pool=m_ev_w_pool, m_ev_pool_scale=m_ev_pool_scale, m_ev_hg_norm=m_ev_hg_norm, m_ev_w_out=m_ev_w_out, m_od_norm=m_od_norm, m_od_w_in=m_od_w_in, m_od_b_f=m_od_b_f, m_od_w_out=m_od_w_out, m_xa_norm=m_xa_norm, m_xa_mem_norm=m_xa_mem_norm, m_xa_wq=m_xa_wq, m_xa_wkv=m_xa_wkv, m_xa_wo=m_xa_wo, m_ffn_norm=m_ffn_norm, m_ffn_w_gate=m_ffn_w_gate, m_ffn_w_up=m_ffn_w_up, m_ffn_w_down=m_ffn_w_down, m_final_norm=m_final_norm, v_lb_table=v_lb_table, v_ev_norm=v_ev_norm, v_ev_w_in=v_ev_w_in, v_ev_w_pool=v_ev_w_pool, v_ev_pool_scale=v_ev_pool_scale, v_ev_hg_norm=v_ev_hg_norm, v_ev_w_out=v_ev_w_out, v_od_norm=v_od_norm, v_od_w_in=v_od_w_in, v_od_b_f=v_od_b_f, v_od_w_out=v_od_w_out, v_xa_norm=v_xa_norm, v_xa_mem_norm=v_xa_mem_norm, v_xa_wq=v_xa_wq, v_xa_wkv=v_xa_wkv, v_xa_wo=v_xa_wo, v_ffn_norm=v_ffn_norm, v_ffn_w_gate=v_ffn_w_gate, v_ffn_w_up=v_ffn_w_up, v_ffn_w_down=v_ffn_w_down, v_final_norm=v_final_norm)
    weights = {n: given[n] for n in TWIN_WEIGHTS}
    shared = {n: given[n] for n in SHARED_INPUTS}
    per_example = {n: given[n] for n in ['x', 'mem']}
    grad_fn = _jax.value_and_grad(_loss, argnums=(0, 1))

    def one_microbatch(ex, loss_target):
        ex = dict(ex)
        diff = ex.pop(TWIN_DIFF_INPUT)
        return grad_fn(weights, diff, {**shared, **ex}, loss_target)

    if N_MICROBATCH == 1:
        loss, (grad_w, grad_x) = one_microbatch(per_example, given["loss_target"])
    else:
        def body(carry, xs):
            loss_sum, grad_sum = carry
            l_k, (gw_k, gx_k) = one_microbatch(xs[0], xs[1])
            with _jax.named_scope("update"):
                return (loss_sum + l_k, _jax.tree.map(_jnp.add, grad_sum, gw_k)), gx_k

        init = (_jnp.zeros((), _jnp.float32), _jax.tree.map(_jnp.zeros_like, weights))
        (loss, grad_w), grad_x = _jax.lax.scan(body, init, (per_example, given["loss_target"]))
    with _jax.named_scope("update"):
        delta_w, new_m, new_v = {}, {}, {}
        for n in TWIN_WEIGHTS:
            delta_w[n], new_m[n], new_v[n] = _adamw(weights[n], grad_w[n], given["m_" + n], given["v_" + n])
    return (loss, grad_x, *[grad_w[n] for n in TWIN_WEIGHTS], *[delta_w[n] for n in TWIN_WEIGHTS],
            *[new_m[n] for n in TWIN_WEIGHTS], *[new_v[n] for n in TWIN_WEIGHTS])
```

```python
import math

import jax
import jax.numpy as jnp
from jax import lax
from jax.experimental import pallas as pl
from jax.experimental.pallas import tpu as pltpu

F32 = jnp.float32
BF16 = jnp.bfloat16
MESH = pl.DeviceIdType.MESH

V7X_VMEM_LIMIT_BYTES = 56 * 1024 * 1024
N_CHIPS = 4

EPS = 1e-6
POOL_WINDOWS = (2, 4, 8, 16)
POOL_HALO = 16
HG_HEAD = 128
HG_CHUNK = 64
FOX_HEAD = 128
FOX_BLK = 512
XA_HEADS = 4
ADAM_LR, ADAM_B1, ADAM_B2, ADAM_EPS, ADAM_WD, ADAM_STEP = 0.001, 0.9, 0.999, 1e-08, 0.01, 10
EXP_CLAMP = 80.0


def _params(sem=None):
    return pltpu.CompilerParams(dimension_semantics=sem, vmem_limit_bytes=V7X_VMEM_LIMIT_BYTES)


def _blk(pref, dim):
    b = min(pref, dim)
    assert dim % b == 0, (pref, dim)
    return b


class VM:
    def __init__(self, arr, kind="cs", lead=(), inner=(), pfn=None):
        self.arr, self.kind, self.lead, self.inner = arr, kind, tuple(lead), tuple(inner)
        self.pfn = pfn or (lambda p: p)
        p = arr.shape[len(self.lead)]
        r, c = arr.shape[-2:]
        assert arr.ndim == len(self.lead) + 1 + len(self.inner) + 2, (arr.shape, lead, inner)
        self.P = p
        self.shape = (r, c * p) if kind == "cs" else (r * p, c)
        self.dtype = arr.dtype

    def spec(self, br, bc, rfn, cfn):
        p = self.P
        r, c = self.arr.shape[-2:]
        assert c % bc == 0 and r % br == 0, (self.arr.shape, br, bc)
        if p == 1:
            def imap(*g):
                return (*self.lead, self.pfn(0), *self.inner, rfn(*g), cfn(*g))
        elif self.kind == "cs":
            per = c // bc

            def imap(*g):
                cb = cfn(*g)
                return (*self.lead, self.pfn(lax.div(cb, per)), *self.inner, rfn(*g), lax.rem(cb, per))
        else:
            per = r // br

            def imap(*g):
                rb = rfn(*g)
                return (*self.lead, self.pfn(lax.div(rb, per)), *self.inner, lax.rem(rb, per), cfn(*g))
        return pl.BlockSpec((None,) * (self.arr.ndim - 2) + (br, bc), imap)


def vm2(arr):
    return VM(arr.reshape((1,) + arr.shape))


def _out_struct(shape, kind, p, dtype):
    r, c = shape
    return jax.ShapeDtypeStruct((p, r, c // p) if kind == "cs" else (p, r // p, c), dtype)


def _best(g, cap):
    if g <= cap:
        return g
    cands = [d for d in range(128, cap + 1, 128) if g % d == 0]
    assert cands, (g, cap)
    return cands[-1]


def _row_blk(n, cap):
    cands = [d for d in range(16, min(n, cap) + 1, 16) if n % d == 0]
    assert cands, (n, cap)
    return cands[-1]


def _tiles(a, b, mode, out_kind, out_p, bm, bn, bk):
    def cpiece(v):
        return v.arr.shape[-1] if v.kind == "cs" else v.shape[1]

    def rpiece(v):
        return v.arr.shape[-2] if v.kind == "rs" else v.shape[0]

    if mode == "nn":
        m, n = a.shape[0], b.shape[1]
        gm, gn, gk = rpiece(a), cpiece(b), math.gcd(cpiece(a), rpiece(b))
    elif mode == "nt":
        m, n = a.shape[0], b.shape[0]
        gm, gn, gk = rpiece(a), rpiece(b), math.gcd(cpiece(a), cpiece(b))
    else:
        m, n = a.shape[1], b.shape[1]
        gm, gn, gk = cpiece(a), cpiece(b), math.gcd(rpiece(a), rpiece(b))
    if out_kind == "cs":
        gn = math.gcd(gn, n // out_p)
    else:
        gm = math.gcd(gm, m // out_p)
    caps = {"nn": (512, 1536, 2048), "nt": (512, 2048, 2048), "tn": (1024, 1024, 1024)}[mode]
    return (bm or _best(gm, caps[0])), (bn or _best(gn, caps[1])), (bk or _best(gk, caps[2]))


def matmul(a, b, mode, *, out_dtype, bm=None, bn=None, bk=None, out_kind="cs", out_p=1, out_pfn=None, res=None, name):
    bm, bn, bk = _tiles(a, b, mode, out_kind, out_p, bm, bn, bk)
    if mode == "nn":
        (m, k), (k2, n) = a.shape, b.shape
        a_spec = a.spec(bm, bk, lambda i, j, kk: i, lambda i, j, kk: kk)
        b_spec = b.spec(bk, bn, lambda i, j, kk: kk, lambda i, j, kk: j)
        dims = (((1,), (0,)), ((), ()))
    elif mode == "nt":
        (m, k), (n, k2) = a.shape, b.shape
        a_spec = a.spec(bm, bk, lambda i, j, kk: i, lambda i, j, kk: kk)
        b_spec = b.spec(bn, bk, lambda i, j, kk: j, lambda i, j, kk: kk)
        dims = (((1,), (1,)), ((), ()))
    else:
        (k, m), (k2, n) = a.shape, b.shape
        a_spec = a.spec(bk, bm, lambda i, j, kk: kk, lambda i, j, kk: i)
        b_spec = b.spec(bk, bn, lambda i, j, kk: kk, lambda i, j, kk: j)
        dims = (((0,), (0,)), ((), ()))
    assert k == k2, (a.shape, b.shape, mode)
    assert m % bm == 0 and n % bn == 0 and k % bk == 0, (m, n, k, bm, bn, bk)
    nk = k // bk
    out_sds = _out_struct((m, n), out_kind, out_p, out_dtype)
    out_vm = VM(out_sds, out_kind, pfn=out_pfn)
    o_spec = out_vm.spec(bm, bn, lambda i, j, kk: i, lambda i, j, kk: j)
    in_specs, args = [a_spec, b_spec], [a.arr, b.arr]
    if res is not None:
        assert res.shape == (m, n)
        in_specs.append(res.spec(bm, bn, lambda i, j, kk: i, lambda i, j, kk: j))
        args.append(res.arr)

    def body(a_ref, b_ref, *rest):
        if res is not None:
            r_ref, o_ref = rest[0], rest[1]
        else:
            r_ref, o_ref = None, rest[0]
        part = lax.dot_general(a_ref[...], b_ref[...], dims, preferred_element_type=F32)
        if nk == 1:
            if r_ref is not None:
                part = part + r_ref[...].astype(F32)
            o_ref[...] = part.astype(o_ref.dtype)
            return
        acc = rest[-1]
        kk = pl.program_id(2)

        @pl.when(kk == 0)
        def _():
            acc[...] = part

        @pl.when(kk > 0)
        def _():
            acc[...] += part

        @pl.when(kk == nk - 1)
        def _():
            tot = acc[...]
            if r_ref is not None:
                tot = tot + r_ref[...].astype(F32)
            o_ref[...] = tot.astype(o_ref.dtype)

    return pl.pallas_call(
        body, name=name, grid=(m // bm, n // bn, nk), in_specs=in_specs, out_specs=o_spec, out_shape=out_sds,
        scratch_shapes=[pltpu.VMEM((bm, bn), F32)] if nk > 1 else [],
        compiler_params=_params(("parallel", "parallel", "arbitrary")),
    )(*args)


def rmsnorm_fwd(x, g, *, name):
    t, d = x.shape
    bt = _blk(512, t)

    def body(x_ref, g_ref, o_ref):
        xv = x_ref[...]
        r = lax.rsqrt(jnp.mean(xv * xv, axis=-1, keepdims=True) + EPS)
        o_ref[...] = (xv * r * g_ref[...]).astype(o_ref.dtype)

    return pl.pallas_call(
        body, name=name, grid=(t // bt,),
        in_specs=[pl.BlockSpec((bt, d), lambda i: (i, 0)), pl.BlockSpec((1, d), lambda i: (0, 0))],
        out_specs=pl.BlockSpec((bt, d), lambda i: (i, 0)), out_shape=jax.ShapeDtypeStruct((t, d), BF16),
        compiler_params=_params(("parallel",)),
    )(x, g)


def rmsnorm_bwd(x, g, dh, dres, *, name):
    t, d = x.shape
    bt = _blk(256, t)
    want_dx = dres is not None

    def body(x_ref, g_ref, dh_ref, *rest):
        if want_dx:
            dres_ref, dx_ref, dxb_ref, dg_ref = rest
        else:
            (dg_ref,) = rest
        xv = x_ref[...]
        dhv = dh_ref[...].astype(F32)
        r = lax.rsqrt(jnp.mean(xv * xv, axis=-1, keepdims=True) + EPS)
        xh = xv * r
        part = jnp.sum(dhv * xh, axis=0, keepdims=True)

        @pl.when(pl.program_id(0) == 0)
        def _():
            dg_ref[...] = part

        @pl.when(pl.program_id(0) > 0)
        def _():
            dg_ref[...] += part

        if want_dx:
            dy = dhv * g_ref[...]
            dxn = r * (dy - xh * jnp.mean(dy * xh, axis=-1, keepdims=True))
            dx = dres_ref[...] + dxn
            dx_ref[...] = dx
            dxb_ref[...] = dx.astype(BF16)

    row = pl.BlockSpec((bt, d), lambda i: (i, 0))
    vec = pl.BlockSpec((1, d), lambda i: (0, 0))
    in_specs, args = [row, vec, row], [x, g, dh]
    out_specs, out_shape = [vec], [jax.ShapeDtypeStruct((1, d), F32)]
    if want_dx:
        in_specs.append(row)
        args.append(dres)
        out_specs = [row, row] + out_specs
        out_shape = [jax.ShapeDtypeStruct((t, d), F32), jax.ShapeDtypeStruct((t, d), BF16)] + out_shape
    return pl.pallas_call(
        body, name=name, grid=(t // bt,), in_specs=in_specs, out_specs=out_specs, out_shape=out_shape,
        compiler_params=_params(("arbitrary",)),
    )(*args)


def loss_head(x, g, tgt, *, name):
    t, d = x.shape
    bt = _blk(256, t)

    def body(x_ref, g_ref, t_ref, loss_ref, dx_ref, dxb_ref, dg_ref):
        xv = x_ref[...]
        gv = g_ref[...]
        r = lax.rsqrt(jnp.mean(xv * xv, axis=-1, keepdims=True) + EPS)
        xh = xv * r
        e = xh * gv - t_ref[...]
        lpart = jnp.zeros((1, 128), F32) + jnp.sum(e * e) * (0.5 / d)
        dyv = e * (1.0 / d)
        gpart = jnp.sum(dyv * xh, axis=0, keepdims=True)

        @pl.when(pl.program_id(0) == 0)
        def _():
            loss_ref[...] = lpart
            dg_ref[...] = gpart

        @pl.when(pl.program_id(0) > 0)
        def _():
            loss_ref[...] += lpart
            dg_ref[...] += gpart

        dy = dyv * gv
        dx = r * (dy - xh * jnp.mean(dy * xh, axis=-1, keepdims=True))
        dx_ref[...] = dx
        dxb_ref[...] = dx.astype(BF16)

    row = pl.BlockSpec((bt, d), lambda i: (i, 0))
    vec = pl.BlockSpec((1, d), lambda i: (0, 0))
    return pl.pallas_call(
        body, name=name, grid=(t // bt,), in_specs=[row, vec, row],
        out_specs=[pl.BlockSpec((1, 128), lambda i: (0, 0)), row, row, vec],
        out_shape=[jax.ShapeDtypeStruct((1, 128), F32), jax.ShapeDtypeStruct((t, d), F32),
                   jax.ShapeDtypeStruct((t, d), BF16), jax.ShapeDtypeStruct((1, d), F32)],
        compiler_params=_params(("arbitrary",)),
    )(x, g, tgt)


def _sigmoid(x):
    return 1.0 / (1.0 + jnp.exp(-x))


def swiglu_fwd(a, b, *, name):
    t, f = a.shape
    bt, bf = _blk(512, t), _best(f, 1536)

    def body(a_ref, b_ref, o_ref):
        av = a_ref[...].astype(F32)
        o_ref[...] = (av * _sigmoid(av) * b_ref[...].astype(F32)).astype(o_ref.dtype)

    spec = pl.BlockSpec((bt, bf), lambda i, j: (i, j))
    return pl.pallas_call(
        body, name=name, grid=(t // bt, f // bf), in_specs=[spec, spec], out_specs=spec,
        out_shape=jax.ShapeDtypeStruct((t, f), BF16), compiler_params=_params(("parallel", "parallel")),
    )(a, b)


def swiglu_bwd(ds, a, b, *, name):
    t, f = a.shape
    bt, bf = _blk(512, t), _best(f, 1536)

    def body(ds_ref, a_ref, b_ref, da_ref, db_ref):
        av = a_ref[...].astype(F32)
        dsv = ds_ref[...].astype(F32)
        sg = _sigmoid(av)
        da_ref[...] = (dsv * b_ref[...].astype(F32) * sg * (1.0 + av * (1.0 - sg))).astype(BF16)
        db_ref[...] = (dsv * av * sg).astype(BF16)

    spec = pl.BlockSpec((bt, bf), lambda i, j: (i, j))
    return pl.pallas_call(
        body, name=name, grid=(t // bt, f // bf), in_specs=[spec, spec, spec], out_specs=[spec, spec],
        out_shape=[jax.ShapeDtypeStruct((t, f), BF16)] * 2, compiler_params=_params(("parallel", "parallel")),
    )(ds, a, b)


def _xa_probs(qh, kh, scale):
    s = lax.dot_general(qh, kh, (((1,), (1,)), ((), ())), preferred_element_type=F32) * scale
    s = s - jnp.max(s, axis=-1, keepdims=True)
    p = jnp.exp(s)
    return p / jnp.sum(p, axis=-1, keepdims=True)


def xattn_fwd(q, kv, *, name):
    t, d = q.shape
    m = kv.shape[0]
    hd = d // XA_HEADS
    bt = _blk(512, t)
    scale = hd ** -0.5

    def body(q_ref, kv_ref, o_ref):
        for h in range(XA_HEADS):
            qh = q_ref[:, h * hd:(h + 1) * hd]
            kh = kv_ref[:, h * hd:(h + 1) * hd]
            vh = kv_ref[:, d + h * hd:d + (h + 1) * hd]
            p = _xa_probs(qh, kh, scale)
            o_ref[:, h * hd:(h + 1) * hd] = jnp.dot(p.astype(BF16), vh, preferred_element_type=F32).astype(BF16)

    return pl.pallas_call(
        body, name=name, grid=(t // bt,),
        in_specs=[pl.BlockSpec((bt, d), lambda i: (i, 0)), pl.BlockSpec((m, 2 * d), lambda i: (0, 0))],
        out_specs=pl.BlockSpec((bt, d), lambda i: (i, 0)), out_shape=jax.ShapeDtypeStruct((t, d), BF16),
        compiler_params=_params(("parallel",)),
    )(q, kv)


def xattn_bwd(q, kv, do, *, name):
    t, d = q.shape
    m = kv.shape[0]
    hd = d // XA_HEADS
    bt = _blk(512, t)
    scale = hd ** -0.5

    def body(q_ref, kv_ref, do_ref, dq_ref, dkv_ref):
        first = pl.program_id(0) == 0
        for h in range(XA_HEADS):
            qs, ks, vs = slice(h * hd, (h + 1) * hd), slice(h * hd, (h + 1) * hd), slice(d + h * hd, d + (h + 1) * hd)
            qh, kh, vh, doh = q_ref[:, qs], kv_ref[:, ks], kv_ref[:, vs], do_ref[:, qs]
            p = _xa_probs(qh, kh, scale)
            dp = lax.dot_general(doh, vh, (((1,), (1,)), ((), ())), preferred_element_type=F32)
            dsv = p * (dp - jnp.sum(p * dp, axis=-1, keepdims=True)) * scale
            dsb = dsv.astype(BF16)
            dq_ref[:, qs] = jnp.dot(dsb, kh, preferred_element_type=F32).astype(BF16)
            dk = lax.dot_general(dsb, qh, (((0,), (0,)), ((), ())), preferred_element_type=F32)
            dv = lax.dot_general(p.astype(BF16), doh, (((0,), (0,)), ((), ())), preferred_element_type=F32)

            @pl.when(first)
            def _():
                dkv_ref[:, ks] = dk
                dkv_ref[:, vs] = dv

            @pl.when(jnp.logical_not(first))
            def _():
                dkv_ref[:, ks] += dk
                dkv_ref[:, vs] += dv

    row = pl.BlockSpec((bt, d), lambda i: (i, 0))
    full = pl.BlockSpec((m, 2 * d), lambda i: (0, 0))
    return pl.pallas_call(
        body, name=name, grid=(t // bt,), in_specs=[row, full, row], out_specs=[row, full],
        out_shape=[jax.ShapeDtypeStruct((t, d), BF16), jax.ShapeDtypeStruct((m, 2 * d), F32)],
        compiler_params=_params(("arbitrary",)),
    )(q, kv, do)


def _pool_p(buf, uv, rows, w, bt):
    acc = uv
    for dd in range(1, w):
        acc = acc + buf[pl.ds(POOL_HALO - dd, bt), :]
    cnt = jnp.minimum(rows + 1, w).astype(F32)
    return acc / cnt - uv


def pool_fwd(z, w_pool, scale, *, name):
    t = z.shape[0]
    ng, gsz = w_pool.shape[0], w_pool.shape[1]
    mix = ng * gsz
    bt = _blk(512, t)

    def body(u_ref, uh_ref, w_ref, sc_ref, o_ref, buf):
        r = pl.program_id(0)
        rows = r * bt + lax.broadcasted_iota(jnp.int32, (bt, 1), 0)
        for g in range(ng):
            gs = slice(g * gsz, (g + 1) * gsz)
            uv = u_ref[:, gs]
            buf[0:POOL_HALO, :] = jnp.where(r > 0, uh_ref[:, gs], 0.0)
            buf[POOL_HALO:POOL_HALO + bt, :] = uv
            p = _pool_p(buf, uv, rows, POOL_WINDOWS[g], bt)
            y = jnp.dot(p.astype(BF16), w_ref[g], preferred_element_type=F32) * sc_ref[:, gs]
            o_ref[:, gs] = y.astype(BF16)

    hb = bt // POOL_HALO
    return pl.pallas_call(
        body, name=name, grid=(t // bt,),
        in_specs=[pl.BlockSpec((bt, mix), lambda i: (i, 0)),
                  pl.BlockSpec((POOL_HALO, mix), lambda i: (jnp.maximum(i * hb - 1, 0), 0)),
                  pl.BlockSpec((ng, gsz, gsz), lambda i: (0, 0, 0)), pl.BlockSpec((1, mix), lambda i: (0, 0))],
        out_specs=pl.BlockSpec((None, bt, mix), lambda i: (0, i, 0)),
        out_shape=jax.ShapeDtypeStruct((2, t, mix), BF16),
        scratch_shapes=[pltpu.VMEM((POOL_HALO + bt, gsz), F32)],
        compiler_params=_params(("parallel",)),
    )(z, z, w_pool, scale)


def pool_bwd(z, dcat, w_pool, scale, *, name):
    t = z.shape[0]
    ng, gsz = w_pool.shape[0], w_pool.shape[1]
    mix = ng * gsz
    bt = _blk(512, t)
    nb = t // bt
    nt_dims = (((1,), (1,)), ((), ()))
    tn_dims = (((0,), (0,)), ((), ()))

    def body(u_ref, uh_ref, dy_ref, dyh_ref, w_ref, sc_ref, du_ref, dw_ref, dsc_ref, buf, buf2):
        r = pl.program_id(0)
        first = r == 0
        rows = r * bt + lax.broadcasted_iota(jnp.int32, (bt, 1), 0)
        rows_h = (r + 1) * bt + lax.broadcasted_iota(jnp.int32, (POOL_HALO, 1), 0)
        for g in range(ng):
            w = POOL_WINDOWS[g]
            gs = slice(g * gsz, (g + 1) * gsz)
            uv = u_ref[:, gs]
            buf[0:POOL_HALO, :] = jnp.where(r > 0, uh_ref[:, gs], 0.0)
            buf[POOL_HALO:POOL_HALO + bt, :] = uv
            pb = _pool_p(buf, uv, rows, w, bt).astype(BF16)
            wg = w_ref[g]
            sc = sc_ref[:, gs]
            y0 = jnp.dot(pb, wg, preferred_element_type=F32)
            dyv = dy_ref[:, gs].astype(F32)
            dsc = jnp.sum(dyv * y0, axis=0, keepdims=True)
            dyw = (dyv * sc).astype(BF16)
            dw = lax.dot_general(pb, dyw, tn_dims, preferred_element_type=F32)

            @pl.when(first)
            def _():
                dw_ref[g] = dw
                dsc_ref[:, gs] = dsc

            @pl.when(jnp.logical_not(first))
            def _():
                dw_ref[g] += dw
                dsc_ref[:, gs] += dsc

            dp = lax.dot_general(dyw, wg, nt_dims, preferred_element_type=F32)
            dyh = (dyh_ref[:, gs].astype(F32) * sc).astype(BF16)
            dph = lax.dot_general(dyh, wg, nt_dims, preferred_element_type=F32)
            dph = jnp.where(r < nb - 1, dph, 0.0)
            buf2[0:bt, :] = dp / jnp.minimum(rows + 1, w).astype(F32)
            buf2[bt:bt + POOL_HALO, :] = dph / jnp.minimum(rows_h + 1, w).astype(F32)
            acc = buf2[pl.ds(0, bt), :]
            for dd in range(1, w):
                acc = acc + buf2[pl.ds(dd, bt), :]
            du_ref[:, gs] = (acc - dp).astype(BF16)

    hb = bt // POOL_HALO
    nhb = t // POOL_HALO
    return pl.pallas_call(
        body, name=name, grid=(nb,),
        in_specs=[pl.BlockSpec((bt, mix), lambda i: (i, 0)),
                  pl.BlockSpec((POOL_HALO, mix), lambda i: (jnp.maximum(i * hb - 1, 0), 0)),
                  pl.BlockSpec((None, bt, mix), lambda i: (0, i, 0)),
                  pl.BlockSpec((None, POOL_HALO, mix), lambda i: (0, jnp.minimum((i + 1) * hb, nhb - 1), 0)),
                  pl.BlockSpec((ng, gsz, gsz), lambda i: (0, 0, 0)), pl.BlockSpec((1, mix), lambda i: (0, 0))],
        out_specs=[pl.BlockSpec((None, bt, mix), lambda i: (4, i, 0)),
                   pl.BlockSpec((ng, gsz, gsz), lambda i: (0, 0, 0)), pl.BlockSpec((1, mix), lambda i: (0, 0))],
        out_shape=[jax.ShapeDtypeStruct((5, t, mix), BF16), jax.ShapeDtypeStruct((ng, gsz, gsz), F32),
                   jax.ShapeDtypeStruct((1, mix), F32)],
        scratch_shapes=[pltpu.VMEM((POOL_HALO + bt, gsz), F32), pltpu.VMEM((bt + POOL_HALO, gsz), F32)],
        compiler_params=_params(("arbitrary",)),
    )(z, z, dcat, dcat, w_pool, scale)


HG_LEVELS = ((64, 31), (32, 15), (16, 7))
HG_DIAG = (8, 3)
_NT = (((1,), (1,)), ((), ()))
_TN = (((0,), (0,)), ((), ()))
_HI = lax.Precision.HIGHEST


def _hg_masks():
    c = HG_CHUNK
    t = lax.broadcasted_iota(jnp.int32, (c, c), 0)
    s = lax.broadcasted_iota(jnp.int32, (c, c), 1)
    masks = []
    for blk, row in HG_LEVELS:
        sh = blk.bit_length() - 1
        same = (t >> sh) == (s >> sh)
        masks.append(same & ((t & (blk - 1)) > row) & ((s & (blk - 1)) <= row))
    sh = HG_DIAG[0].bit_length() - 1
    masks.append(((t >> sh) == (s >> sh)) & (s <= t))
    return t, s, masks


def _row_of_block(x, blk, row):
    c, n = x.shape
    x3 = x.reshape(c // blk, blk, n)
    return jnp.broadcast_to(x3[:, row:row + 1, :], x3.shape).reshape(c, n)


def _hg_parts(qv, flv, lb, masks, tri):
    sgf = _sigmoid(flv)
    f = lb + (1.0 - lb) * sgf
    logf = jnp.log(f)
    kk = 1.0 - f
    sgq = _sigmoid(qv)
    qf = qv * sgq * (HG_HEAD ** -0.5)
    bc = jnp.dot(tri, logf, preferred_element_type=F32, precision=_HI)
    levels = []
    a = None
    for li, (blk, row) in enumerate(HG_LEVELS + (HG_DIAG,)):
        e = bc - _row_of_block(bc, blk, row)
        if li < len(HG_LEVELS):
            eq, ek = jnp.exp(jnp.minimum(e, 0.0)), jnp.exp(jnp.minimum(-e, 0.0))
        else:
            eq, ek = jnp.exp(jnp.clip(e, -EXP_CLAMP, EXP_CLAMP)), jnp.exp(jnp.clip(-e, -EXP_CLAMP, EXP_CLAMP))
        qt, kt = qf * eq, kk * ek
        part = jnp.where(masks[li], lax.dot_general(qt.astype(BF16), kt.astype(BF16), _NT, preferred_element_type=F32), 0.0)
        a = part if a is None else a + part
        levels.append((eq, ek, qt, kt))
    return dict(sgf=sgf, f=f, kk=kk, sgq=sgq, qf=qf, bc=bc, levels=levels, a=a)


def hgrn_fwd(z, cat, lb, gain, mix_a, *, name):
    t = z.shape[0]
    mix_b = lb.shape[1]
    nh = mix_b // HG_HEAD
    bt = _blk(256, t)
    ncb = bt // HG_CHUNK
    dh = HG_HEAD

    def body(q_ref, fl_ref, i_ref, g_ref, lb_ref, gain_ref, cat_in, o_ref, st_ref, st):
        del cat_in

        @pl.when(pl.program_id(1) == 0)
        def _():
            st[...] = jnp.zeros_like(st)

        t_i, s_i, masks = _hg_masks()
        tri = (s_i <= t_i).astype(F32)
        lbv, gn = lb_ref[...], gain_ref[...]
        for c in range(ncb):
            rs = slice(c * HG_CHUNK, (c + 1) * HG_CHUNK)
            pr = _hg_parts(q_ref[rs, :], fl_ref[rs, :], lbv, masks, tri)
            vb = i_ref[rs, :].astype(BF16)
            stv = st[...]
            st_ref[c] = stv
            bc = pr["bc"]
            qt = pr["qf"] * jnp.exp(bc)
            o = (jnp.dot(pr["a"].astype(BF16), vb, preferred_element_type=F32)
                 + lax.dot_general(qt.astype(BF16), stv.astype(BF16), _NT, preferred_element_type=F32))
            bl = bc[HG_CHUNK - 1:HG_CHUNK, :]
            khat = pr["kk"] * jnp.exp(bl - bc)
            st[...] = stv * jnp.exp(bl) + lax.dot_general(vb, khat.astype(BF16), _TN, preferred_element_type=F32)
            r = lax.rsqrt(jnp.mean(o * o, axis=-1, keepdims=True) + EPS)
            gv = g_ref[rs, :]
            o_ref[rs, :] = (o * r * gn * (gv * _sigmoid(gv))).astype(BF16)

    def col(which):
        base = (mix_a + which * mix_b) // dh
        return pl.BlockSpec((bt, dh), lambda h, i: (i, base + h))

    return pl.pallas_call(
        body, name=name, grid=(nh, t // bt),
        in_specs=[col(0), col(1), col(2), col(3), pl.BlockSpec((1, dh), lambda h, i: (0, h)),
                  pl.BlockSpec((1, dh), lambda h, i: (0, 0)), pl.BlockSpec(memory_space=pl.ANY)],
        out_specs=[pl.BlockSpec((None, bt, dh), lambda h, i: (1, i, h)),
                   pl.BlockSpec((None, ncb, dh, dh), lambda h, i: (h, i, 0, 0))],
        out_shape=[jax.ShapeDtypeStruct(cat.shape, BF16), jax.ShapeDtypeStruct((nh, t // HG_CHUNK, dh, dh), F32)],
        scratch_shapes=[pltpu.VMEM((dh, dh), F32)],
        input_output_aliases={6: 0},
        compiler_params=_params(("parallel", "arbitrary")),
    )(z, z, z, z, lb, gain, cat)


def hgrn_bwd(z, dcat, dz5, states, lb, gain, mix_a, *, name):
    t = z.shape[0]
    mix_b = lb.shape[1]
    nh = mix_b // HG_HEAD
    bt = _blk(256, t)
    nb = t // bt
    ncb = bt // HG_CHUNK
    dh = HG_HEAD

    def body(q_ref, fl_ref, i_ref, g_ref, dy_ref, st_ref, lb_ref, gain_ref, dz_in, dz_ref, dlb_ref, dgn_ref, dst):
        del dz_in
        first = pl.program_id(1) == 0

        @pl.when(first)
        def _():
            dst[...] = jnp.zeros_like(dst)

        t_i, s_i, masks = _hg_masks()
        tri = (s_i <= t_i).astype(F32)
        triu = (s_i >= t_i).astype(F32)
        last_row = lax.broadcasted_iota(jnp.int32, (HG_CHUNK, 1), 0) == HG_CHUNK - 1
        lbv, gn = lb_ref[...], gain_ref[...]
        dlb_acc = jnp.zeros((1, dh), F32)
        dgn_acc = jnp.zeros((1, dh), F32)
        for c in reversed(range(ncb)):
            rs = slice(c * HG_CHUNK, (c + 1) * HG_CHUNK)
            qv, flv, gv = q_ref[rs, :], fl_ref[rs, :], g_ref[rs, :]
            pr = _hg_parts(qv, flv, lbv, masks, tri)
            vb = i_ref[rs, :].astype(BF16)
            stv = st_ref[c]
            stb = stv.astype(BF16)
            dsv = dst[...]
            dsb = dsv.astype(BF16)
            bc, kk, qf, ab = pr["bc"], pr["kk"], pr["qf"], pr["a"].astype(BF16)
            ebc = jnp.exp(bc)
            qt = qf * ebc
            qtb = qt.astype(BF16)
            o = jnp.dot(ab, vb, preferred_element_type=F32) + lax.dot_general(qtb, stb, _NT, preferred_element_type=F32)
            r = lax.rsqrt(jnp.mean(o * o, axis=-1, keepdims=True) + EPS)
            oh = o * r
            sgg = _sigmoid(gv)
            dyv = dy_ref[rs, :].astype(F32)
            don = dyv * (gv * sgg)
            dgate = dyv * (oh * gn) * (sgg * (1.0 + gv * (1.0 - sgg)))
            dgn_acc = dgn_acc + jnp.sum(don * oh, axis=0, keepdims=True)
            doh = don * gn
            do = r * (doh - oh * jnp.mean(doh * oh, axis=-1, keepdims=True))
            dob = do.astype(BF16)
            bl = bc[HG_CHUNK - 1:HG_CHUNK, :]
            ebl = jnp.exp(bl)
            ekh = jnp.exp(bl - bc)
            khat = kk * ekh
            dv = (lax.dot_general(ab, dob, _TN, preferred_element_type=F32)
                  + lax.dot_general(khat.astype(BF16), dsb, _NT, preferred_element_type=F32))
            da = lax.dot_general(dob, vb, _NT, preferred_element_type=F32)
            dqt = jnp.dot(dob, stb, preferred_element_type=F32)
            dkh = jnp.dot(vb, dsb, preferred_element_type=F32)
            dst[...] = dsv * ebl + lax.dot_general(dob, qtb, _TN, preferred_element_type=F32)
            dbl = jnp.sum(dsv * stv, axis=0, keepdims=True) * ebl + jnp.sum(dkh * khat, axis=0, keepdims=True)
            dqf = dqt * ebc
            dkk = dkh * ekh
            dbc = dqt * qt - dkh * khat
            for li, (eq, ek, qtl, ktl) in enumerate(pr["levels"]):
                gm = jnp.where(masks[li], da, 0.0).astype(BF16)
                qtr, ktr = qtl.astype(BF16), ktl.astype(BF16)
                dql = jnp.dot(gm, ktr, preferred_element_type=F32)
                dkl = lax.dot_general(gm, qtr, _TN, preferred_element_type=F32)
                dqf = dqf + dql * eq
                dkk = dkk + dkl * ek
                dbc = dbc + qtr.astype(F32) * dql - ktr.astype(F32) * dkl
            dbc = dbc + jnp.where(last_row, dbl, 0.0)
            dlogf = jnp.dot(triu, dbc, preferred_element_type=F32, precision=_HI)
            df = dlogf / pr["f"] - dkk
            sgf = pr["sgf"]
            dfl = df * (1.0 - lbv) * sgf * (1.0 - sgf)
            dlb_acc = dlb_acc + jnp.sum(df * (1.0 - sgf), axis=0, keepdims=True)
            sgq = pr["sgq"]
            dq = dqf * (HG_HEAD ** -0.5) * (sgq * (1.0 + qv * (1.0 - sgq)))
            dz_ref[0, rs, :] = dq.astype(BF16)
            dz_ref[1, rs, :] = dfl.astype(BF16)
            dz_ref[2, rs, :] = dv.astype(BF16)
            dz_ref[3, rs, :] = dgate.astype(BF16)

        @pl.when(first)
        def _():
            dlb_ref[...] = dlb_acc
            dgn_ref[...] = dgn_acc

        @pl.when(jnp.logical_not(first))
        def _():
            dlb_ref[...] += dlb_acc
            dgn_ref[...] += dgn_acc

    def col(which):
        base = (mix_a + which * mix_b) // dh
        return pl.BlockSpec((bt, dh), lambda h, i: (nb - 1 - i, base + h))

    return pl.pallas_call(
        body, name=name, grid=(nh, nb),
        in_specs=[col(0), col(1), col(2), col(3),
                  pl.BlockSpec((None, bt, dh), lambda h, i: (1, nb - 1 - i, h)),
                  pl.BlockSpec((None, ncb, dh, dh), lambda h, i: (h, nb - 1 - i, 0, 0)),
                  pl.BlockSpec((1, dh), lambda h, i: (0, h)), pl.BlockSpec((1, dh), lambda h, i: (0, 0)),
                  pl.BlockSpec(memory_space=pl.ANY)],
        out_specs=[pl.BlockSpec((4, bt, dh), lambda h, i: (0, nb - 1 - i, h)),
                   pl.BlockSpec((1, dh), lambda h, i: (0, h)),
                   pl.BlockSpec((None, 1, dh), lambda h, i: (h, 0, 0))],
        out_shape=[jax.ShapeDtypeStruct(dz5.shape, BF16), jax.ShapeDtypeStruct((1, mix_b), F32),
                   jax.ShapeDtypeStruct((nh, 1, dh), F32)],
        scratch_shapes=[pltpu.VMEM((dh, dh), F32)],
        input_output_aliases={8: 0},
        compiler_params=_params(("parallel", "arbitrary")),
    )(z, z, z, z, dcat, states, lb, gain, dz5)


def _fox_scores(qb, kb, fk, scale, masked):
    s = lax.dot_general(qb, kb, _NT, preferred_element_type=F32) * scale - fk
    if masked:
        n = s.shape[0]
        row = lax.broadcasted_iota(jnp.int32, (n, n), 0)
        col = lax.broadcasted_iota(jnp.int32, (n, n), 1)
        s = jnp.where(col <= row, s, -jnp.inf)
    return s


def fox_fwd(qkv, fk, *, name):
    _, t, d = qkv.shape
    nh = d // FOX_HEAD
    b = _blk(FOX_BLK, t)
    nb = t // b
    dh = FOX_HEAD
    scale = dh ** -0.5

    def body(q_ref, k_ref, v_ref, f_ref, o_ref, lse_ref):
        qi = pl.program_id(1)
        qb = q_ref[...]

        def step(kj, carry, masked):
            m, l, acc = carry
            off = pl.multiple_of(kj * b, b)
            s = _fox_scores(qb, k_ref[pl.ds(off, b), :], f_ref[kj], scale, masked)
            m_new = jnp.maximum(m, jnp.max(s, axis=-1, keepdims=True))
            alpha = jnp.exp(m - m_new)
            p = jnp.exp(s - m_new)
            l = alpha * l + jnp.sum(p, axis=-1, keepdims=True)
            acc = alpha * acc + jnp.dot(p.astype(BF16), v_ref[pl.ds(off, b), :], preferred_element_type=F32)
            return m_new, l, acc

        init = (jnp.full((b, 1), -jnp.inf, F32), jnp.zeros((b, 1), F32), jnp.zeros((b, dh), F32))
        carry = lax.fori_loop(0, qi, lambda kj, c: step(kj, c, False), init)
        m, l, acc = step(qi, carry, True)
        o_ref[...] = (acc / l).astype(BF16)
        lse_ref[...] = m + jnp.log(l)

    return pl.pallas_call(
        body, name=name, grid=(nh, nb),
        in_specs=[pl.BlockSpec((None, b, dh), lambda h, i: (0, i, h)),
                  pl.BlockSpec((None, t, dh), lambda h, i: (1, 0, h)),
                  pl.BlockSpec((None, t, dh), lambda h, i: (2, 0, h)),
                  pl.BlockSpec((None, nb, 1, b), lambda h, i: (h, 0, 0, 0))],
        out_specs=[pl.BlockSpec((b, dh), lambda h, i: (i, h)), pl.BlockSpec((None, b, 1), lambda h, i: (h, i, 0))],
        out_shape=[jax.ShapeDtypeStruct((t, d), BF16), jax.ShapeDtypeStruct((nh, t, 1), F32)],
        compiler_params=_params(("parallel", "parallel")),
    )(qkv, qkv, qkv, fk)


def fox_bwd_dq(qkv, fk, do, lse, *, name):
    _, t, d = qkv.shape
    nh = d // FOX_HEAD
    b = _blk(FOX_BLK, t)
    nb = t // b
    dh = FOX_HEAD
    scale = dh ** -0.5

    def body(q_ref, k_ref, v_ref, f_ref, do_ref, lse_ref, dq_ref, dl_ref, p_buf, dp_buf):
        qi = pl.program_id(1)
        qb, dob, lse_v = q_ref[...], do_ref[...], lse_ref[...]

        def first(kj, dl, masked):
            off = pl.multiple_of(kj * b, b)
            p = jnp.exp(_fox_scores(qb, k_ref[pl.ds(off, b), :], f_ref[kj], scale, masked) - lse_v)
            dp = lax.dot_general(dob, v_ref[pl.ds(off, b), :], _NT, preferred_element_type=F32)
            p_buf[kj] = p
            dp_buf[kj] = dp
            return dl + jnp.sum(p * dp, axis=-1, keepdims=True)

        dl = lax.fori_loop(0, qi, lambda kj, c: first(kj, c, False), jnp.zeros((b, 1), F32))
        dl = first(qi, dl, True)
        dl_ref[...] = dl

        def second(kj, dq):
            off = pl.multiple_of(kj * b, b)
            dsv = p_buf[kj] * (dp_buf[kj] - dl)
            return dq + jnp.dot(dsv.astype(BF16), k_ref[pl.ds(off, b), :], preferred_element_type=F32)

        dq = lax.fori_loop(0, qi + 1, second, jnp.zeros((b, dh), F32))
        dq_ref[...] = (dq * scale).astype(BF16)

    col = pl.BlockSpec((None, b, 1), lambda h, i: (h, i, 0))
    return pl.pallas_call(
        body, name=name, grid=(nh, nb),
        in_specs=[pl.BlockSpec((None, b, dh), lambda h, i: (0, i, h)),
                  pl.BlockSpec((None, t, dh), lambda h, i: (1, 0, h)),
                  pl.BlockSpec((None, t, dh), lambda h, i: (2, 0, h)),
                  pl.BlockSpec((None, nb, 1, b), lambda h, i: (h, 0, 0, 0)),
                  pl.BlockSpec((b, dh), lambda h, i: (i, h)), col],
        out_specs=[pl.BlockSpec((None, b, dh), lambda h, i: (2, i, h)), col],
        out_shape=[jax.ShapeDtypeStruct((3, t, d), BF16), jax.ShapeDtypeStruct((nh, t, 1), F32)],
        scratch_shapes=[pltpu.VMEM((nb, b, b), F32), pltpu.VMEM((nb, b, b), F32)],
        compiler_params=_params(("parallel", "parallel")),
    )(qkv, qkv, qkv, fk, do, lse)


def fox_bwd_dkv(qkv, fk, do, lse, delta, dqkv, *, name):
    _, t, d = qkv.shape
    nh = d // FOX_HEAD
    b = _blk(FOX_BLK, t)
    nb = t // b
    dh = FOX_HEAD
    scale = dh ** -0.5

    def body(q_ref, k_ref, v_ref, f_ref, do_ref, lse_ref, dl_ref, dz_in, dkv_ref, df_ref):
        del dz_in
        kj = pl.program_id(1)
        kb, vb, fkv = k_ref[...], v_ref[...], f_ref[...]

        def step(qi, carry, masked):
            dk, dv, df = carry
            off = pl.multiple_of(qi * b, b)
            qb, dob = q_ref[pl.ds(off, b), :], do_ref[pl.ds(off, b), :]
            p = jnp.exp(_fox_scores(qb, kb, fkv, scale, masked) - lse_ref[pl.ds(off, b), :])
            dv = dv + lax.dot_general(p.astype(BF16), dob, _TN, preferred_element_type=F32)
            dp = lax.dot_general(dob, vb, _NT, preferred_element_type=F32)
            dsv = p * (dp - dl_ref[pl.ds(off, b), :])
            dk = dk + lax.dot_general(dsv.astype(BF16), qb, _TN, preferred_element_type=F32)
            return dk, dv, df - jnp.sum(dsv, axis=0, keepdims=True)

        init = (jnp.zeros((b, dh), F32), jnp.zeros((b, dh), F32), jnp.zeros((1, b), F32))
        carry = step(kj, init, True)
        dk, dv, df = lax.fori_loop(kj + 1, nb, lambda qi, c: step(qi, c, False), carry)
        dkv_ref[0] = (dk * scale).astype(BF16)
        dkv_ref[1] = dv.astype(BF16)
        df_ref[...] = df

    col = pl.BlockSpec((None, t, 1), lambda h, j: (h, 0, 0))
    return pl.pallas_call(
        body, name=name, grid=(nh, nb),
        in_specs=[pl.BlockSpec((None, t, dh), lambda h, j: (0, 0, h)),
                  pl.BlockSpec((None, b, dh), lambda h, j: (1, j, h)),
                  pl.BlockSpec((None, b, dh), lambda h, j: (2, j, h)),
                  pl.BlockSpec((None, None, 1, b), lambda h, j: (h, j, 0, 0)),
                  pl.BlockSpec((t, dh), lambda h, j: (0, h)), col, col, pl.BlockSpec(memory_space=pl.ANY)],
        out_specs=[pl.BlockSpec((2, b, dh), lambda h, j: (0, j, h)),
                   pl.BlockSpec((None, None, 1, b), lambda h, j: (h, j, 0, 0))],
        out_shape=[jax.ShapeDtypeStruct((3, t, d), BF16), jax.ShapeDtypeStruct((nh, nb, 1, b), F32)],
        input_output_aliases={7: 0},
        compiler_params=_params(("parallel", "parallel")),
    )(qkv, qkv, qkv, fk, do, lse, delta, dqkv)


FL_BLK = 256


def _log_sigmoid(x):
    return jnp.minimum(x, 0.0) - jnp.log(1.0 + jnp.exp(-jnp.abs(x)))


def fl_fwd(zf, bf, *, name):
    t, n = zf.shape
    bt = _blk(FL_BLK, t)

    def body(z_ref, b_ref, o_ref, carry):
        @pl.when(pl.program_id(0) == 0)
        def _():
            carry[...] = jnp.zeros_like(carry)

        ls = _log_sigmoid(z_ref[...] + b_ref[...])
        r = lax.broadcasted_iota(jnp.int32, (bt, bt), 0)
        c = lax.broadcasted_iota(jnp.int32, (bt, bt), 1)
        cs = jnp.dot((c <= r).astype(F32), ls, preferred_element_type=F32, precision=_HI) + carry[...]
        o_ref[...] = cs
        carry[...] = cs[bt - 1:bt, :]

    return pl.pallas_call(
        body, name=name, grid=(t // bt,),
        in_specs=[pl.BlockSpec((bt, n), lambda i: (i, 0)), pl.BlockSpec((1, n), lambda i: (0, 0))],
        out_specs=pl.BlockSpec((bt, n), lambda i: (i, 0)), out_shape=jax.ShapeDtypeStruct((t, n), F32),
        scratch_shapes=[pltpu.VMEM((1, n), F32)], compiler_params=_params(("arbitrary",)),
    )(zf, bf)


def fl_bwd(df, zf, bf, *, name):
    t, n = zf.shape
    bt = _blk(FL_BLK, t)
    nb = t // bt

    def body(df_ref, z_ref, b_ref, dz_ref, db_ref, carry):
        first = pl.program_id(0) == 0

        @pl.when(first)
        def _():
            carry[...] = jnp.zeros_like(carry)

        r = lax.broadcasted_iota(jnp.int32, (bt, bt), 0)
        c = lax.broadcasted_iota(jnp.int32, (bt, bt), 1)
        dls = jnp.dot((c >= r).astype(F32), df_ref[...], preferred_element_type=F32, precision=_HI) + carry[...]
        carry[...] = dls[0:1, :]
        dz = dls * (1.0 - _sigmoid(z_ref[...] + b_ref[...]))
        dz_ref[...] = dz.astype(BF16)
        part = jnp.sum(dz, axis=0, keepdims=True)

        @pl.when(first)
        def _():
            db_ref[...] = part

        @pl.when(jnp.logical_not(first))
        def _():
            db_ref[...] += part

    row = pl.BlockSpec((bt, n), lambda i: (nb - 1 - i, 0))
    vec = pl.BlockSpec((1, n), lambda i: (0, 0))
    return pl.pallas_call(
        body, name=name, grid=(nb,), in_specs=[row, row, vec], out_specs=[row, vec],
        out_shape=[jax.ShapeDtypeStruct((t, n), BF16), jax.ShapeDtypeStruct((1, n), F32)],
        scratch_shapes=[pltpu.VMEM((1, n), F32)], compiler_params=_params(("arbitrary",)),
    )(df, zf, bf)


def _adamw_math(w, g, m, v):
    m = ADAM_B1 * m + (1.0 - ADAM_B1) * g
    v = ADAM_B2 * v + (1.0 - ADAM_B2) * (g * g)
    m_hat = m / (1.0 - ADAM_B1 ** ADAM_STEP)
    v_hat = v / (1.0 - ADAM_B2 ** ADAM_STEP)
    delta = -ADAM_LR * (m_hat / (jnp.sqrt(v_hat) + ADAM_EPS) + ADAM_WD * w)
    return delta, m, v


def adamw(w, g, m, v, *, name):
    r, c = w.shape
    br = _blk(256, r)

    def body(w_ref, g_ref, m_ref, v_ref, go_ref, d_ref, mo_ref, vo_ref):
        gv = g_ref[...]
        go_ref[...] = gv
        d_ref[...], mo_ref[...], vo_ref[...] = _adamw_math(w_ref[...], gv, m_ref[...], v_ref[...])

    spec = pl.BlockSpec((br, c), lambda i: (i, 0))
    return pl.pallas_call(
        body, name=name, grid=(r // br,), in_specs=[spec] * 4, out_specs=[spec] * 4,
        out_shape=[jax.ShapeDtypeStruct((r, c), F32)] * 4, compiler_params=_params(("parallel",)),
    )(w, g, m, v)


W_EV_OUT, W_OD_OUT, W_XQ, W_XO = 0, 1, 2, 4


def _f2(a):
    return a.reshape(a.shape[-2:])


def _local_step(x0, mem, tgt, sp, G):
    t, d = x0.shape
    mix_a = sp["pool_scale"].shape[1]
    w2048 = lambda i: VM(G["w2048"], "rs", inner=(i,))
    down = lambda l: VM(G["down"], "rs", inner=(l,))
    wkv = lambda l: VM(G["wkv"], "cs", inner=(l,))
    gate = lambda l: VM(G["gu"], "cs", inner=(l,))
    up = lambda l: VM(G["gu"], "cs", inner=(2 + l,))
    ev_in = VM(G["ev_in"], "cs")
    wqkv, wf = vm2(G["wqkv"]), vm2(G["wf"])
    grads, small = {}, {}

    def row(a, l):
        return a[l:l + 1]

    def xattn_f(l, xin):
        hx = rmsnorm_fwd(xin, row(sp["xa_norm"], l), name=f"xa_norm_f{l}")
        q = _f2(matmul(vm2(hx), w2048(W_XQ + l), "nn", out_dtype=BF16, name=f"xa_q_f{l}"))
        mn = rmsnorm_fwd(mem, row(sp["xa_mem_norm"], l), name=f"xa_memnorm_f{l}")
        kv = _f2(matmul(vm2(mn), wkv(l), "nn", out_dtype=BF16, name=f"xa_kv_f{l}"))
        o = xattn_fwd(q, kv, name=f"xa_attn_f{l}")
        xout = _f2(matmul(vm2(o), w2048(W_XO + l), "nn", out_dtype=F32, res=vm2(xin), name=f"xa_o_f{l}"))
        return xout, (xin, hx, q, mn, kv, o)

    def ffn_f(l, xin):
        hf = rmsnorm_fwd(xin, row(sp["ffn_norm"], l), name=f"ffn_norm_f{l}")
        a = _f2(matmul(vm2(hf), gate(l), "nn", out_dtype=BF16, name=f"ffn_gate_f{l}"))
        b = _f2(matmul(vm2(hf), up(l), "nn", out_dtype=BF16, name=f"ffn_up_f{l}"))
        s = swiglu_fwd(a, b, name=f"ffn_act_f{l}")
        xout = _f2(matmul(vm2(s), down(l), "nn", out_dtype=F32, res=vm2(xin), name=f"ffn_down_f{l}"))
        return xout, (xin, hf, a, b, s)

    h0 = rmsnorm_fwd(x0, sp["ev_norm"], name="ev_norm_f")
    z = _f2(matmul(vm2(h0), ev_in, "nn", out_dtype=F32, name="ev_in_f"))
    cat = pool_fwd(z, G["pool"], sp["pool_scale"], name="pool_f")
    cat, states = hgrn_fwd(z, cat, sp["lb"], sp["hg_gain"], mix_a, name="hgrn_f")
    x1 = _f2(matmul(VM(cat), w2048(W_EV_OUT), "nn", out_dtype=F32, res=vm2(x0), name="ev_out_f"))
    x2, xa0 = xattn_f(0, x1)
    x3, ff0 = ffn_f(0, x2)

    ho = rmsnorm_fwd(x3, sp["od_norm"], name="od_norm_f")
    qkv = matmul(vm2(ho), wqkv, "nn", out_dtype=BF16, out_p=3, name="od_qkv_f")
    zf = _f2(matmul(vm2(ho), wf, "nn", out_dtype=F32, name="od_fl_f"))
    fcum = fl_fwd(zf, sp["bf"], name="od_forget_f")
    nh = d // FOX_HEAD
    nfb = t // _blk(FOX_BLK, t)
    fk = fcum[:, :nh].T.reshape(nh, nfb, 1, t // nfb)
    of, lse = fox_fwd(qkv, fk, name="fox_f")
    x4 = _f2(matmul(vm2(of), w2048(W_OD_OUT), "nn", out_dtype=F32, res=vm2(x3), name="od_out_f"))
    x5, xa1 = xattn_f(1, x4)
    x6, ff1 = ffn_f(1, x5)
    loss, dx, dxb, small["final_norm"] = loss_head(x6, sp["final_norm"], tgt, name="loss_head")

    def ffn_b(l, saved, dx, dxb):
        xin, hf, a, b, s = saved
        dsv = _f2(matmul(vm2(dxb), down(l), "nt", out_dtype=BF16, name=f"ffn_down_bx{l}"))
        grads[f"down{l}"] = matmul(vm2(s), vm2(dxb), "tn", out_dtype=BF16, out_kind="rs", out_p=N_CHIPS, name=f"ffn_down_bw{l}")
        da, db = swiglu_bwd(dsv, a, b, name=f"ffn_act_b{l}")
        grads[f"gate{l}"] = matmul(vm2(hf), vm2(da), "tn", out_dtype=BF16, out_p=N_CHIPS, name=f"ffn_gate_bw{l}")
        grads[f"up{l}"] = matmul(vm2(hf), vm2(db), "tn", out_dtype=BF16, out_p=N_CHIPS, name=f"ffn_up_bw{l}")
        dh = matmul(vm2(da), gate(l), "nt", out_dtype=F32, name=f"ffn_gate_bx{l}")
        dh = _f2(matmul(vm2(db), up(l), "nt", out_dtype=F32, res=VM(dh), name=f"ffn_up_bx{l}"))
        dx, dxb, dg = rmsnorm_bwd(xin, row(sp["ffn_norm"], l), dh, dx, name=f"ffn_norm_b{l}")
        return dx, dxb, dg

    def xattn_b(l, saved, dx, dxb):
        xin, hx, q, mn, kv, o = saved
        do = _f2(matmul(vm2(dxb), w2048(W_XO + l), "nt", out_dtype=BF16, name=f"xa_o_bx{l}"))
        grads[f"wo{l}"] = matmul(vm2(o), vm2(dxb), "tn", out_dtype=BF16, out_kind="rs", out_p=N_CHIPS, name=f"xa_o_bw{l}")
        dq, dkv = xattn_bwd(q, kv, do, name=f"xa_attn_b{l}")
        grads[f"wq{l}"] = matmul(vm2(hx), vm2(dq), "tn", out_dtype=BF16, out_kind="rs", out_p=N_CHIPS, name=f"xa_q_bw{l}")
        dh = _f2(matmul(vm2(dq), w2048(W_XQ + l), "nt", out_dtype=F32, name=f"xa_q_bx{l}"))
        dkvb = dkv.astype(BF16)
        grads[f"wkv{l}"] = matmul(vm2(mn), vm2(dkvb), "tn", out_dtype=BF16, out_p=N_CHIPS, name=f"xa_kv_bw{l}")
        dmn = _f2(matmul(vm2(dkvb), wkv(l), "nt", out_dtype=F32, name=f"xa_kv_bx{l}"))
        (dgm,) = rmsnorm_bwd(mem, row(sp["xa_mem_norm"], l), dmn, None, name=f"xa_memnorm_b{l}")
        dx, dxb, dg = rmsnorm_bwd(xin, row(sp["xa_norm"], l), dh, dx, name=f"xa_norm_b{l}")
        return dx, dxb, dg, dgm

    dg_ffn, dg_xa, dg_mem = [None, None], [None, None], [None, None]
    dx, dxb, dg_ffn[1] = ffn_b(1, ff1, dx, dxb)
    dx, dxb, dg_xa[1], dg_mem[1] = xattn_b(1, xa1, dx, dxb)

    do = _f2(matmul(vm2(dxb), w2048(W_OD_OUT), "nt", out_dtype=BF16, name="od_out_bx"))
    grads["od_out"] = matmul(vm2(of), vm2(dxb), "tn", out_dtype=BF16, out_kind="rs", out_p=N_CHIPS, name="od_out_bw")
    dz3, delta = fox_bwd_dq(qkv, fk, do, lse, name="fox_bq")
    dz3, dfk = fox_bwd_dkv(qkv, fk, do, lse, delta, dz3, name="fox_bkv")
    dfc = jnp.pad(dfk.reshape(nh, t).T, ((0, 0), (0, zf.shape[1] - nh)))
    dzf, dbf = fl_bwd(dfc, zf, sp["bf"], name="od_forget_b")
    dqkv = VM(dz3, "cs", pfn=lambda p: lax.rem(p + 2, 3))
    dwqkv = _f2(matmul(vm2(ho), dqkv, "tn", out_dtype=BF16, name="od_qkv_bw"))
    dwf = _f2(matmul(vm2(ho), vm2(dzf), "tn", out_dtype=BF16, name="od_fl_bw"))
    grads["od_in_full"] = jnp.concatenate([dwqkv, dwf[:, :nh]], axis=1)
    dh = matmul(dqkv, wqkv, "nt", out_dtype=F32, name="od_qkv_bx")
    dh = _f2(matmul(vm2(dzf), wf, "nt", out_dtype=F32, res=VM(dh), name="od_fl_bx"))
    dx, dxb, small["od_norm"] = rmsnorm_bwd(x3, sp["od_norm"], dh, dx, name="od_norm_b")
    small["bf"] = dbf

    dx, dxb, dg_ffn[0] = ffn_b(0, ff0, dx, dxb)
    dx, dxb, dg_xa[0], dg_mem[0] = xattn_b(0, xa0, dx, dxb)

    dcat = matmul(vm2(dxb), w2048(W_EV_OUT), "nt", out_dtype=BF16, out_p=2, name="ev_out_bx")
    grads["ev_out"] = matmul(VM(cat), vm2(dxb), "tn", out_dtype=BF16, out_kind="rs", out_p=N_CHIPS, name="ev_out_bw")
    dz5, grads["pool_full"], small["pool_scale"] = pool_bwd(z, dcat, G["pool"], sp["pool_scale"], name="pool_b")
    dz5, small["lb"], dgn = hgrn_bwd(z, dcat, dz5, states, sp["lb"], sp["hg_gain"], mix_a, name="hgrn_b")
    small["hg_gain"] = jnp.sum(dgn, axis=0)
    dzv = VM(dz5, "cs", pfn=lambda p: lax.rem(p + 4, 5))
    grads["ev_in"] = matmul(vm2(h0), dzv, "tn", out_dtype=BF16, out_p=N_CHIPS, name="ev_in_bw")
    dh = _f2(matmul(dzv, ev_in, "nt", out_dtype=F32, name="ev_in_bx"))
    dx, _, small["ev_norm"] = rmsnorm_bwd(x0, sp["ev_norm"], dh, dx, name="ev_norm_b")

    small["xa_norm"] = jnp.concatenate(dg_xa, axis=0)
    small["xa_mem_norm"] = jnp.concatenate(dg_mem, axis=0)
    small["ffn_norm"] = jnp.concatenate(dg_ffn, axis=0)
    return loss, dx, small, grads


def _me():
    x, y, c = lax.axis_index("x"), lax.axis_index("y"), lax.axis_index("c")
    chips = [(1 - x, y), (x, 1 - y), (1 - x, 1 - y)]
    return x, y, c, chips


def _chip_id(xy):
    return 2 * xy[0] + xy[1]


def _rcopy(src, dst, ssem, rsem, dev):
    return pltpu.make_async_remote_copy(src_ref=src, dst_ref=dst, send_sem=ssem, recv_sem=rsem, device_id=dev,
                                        device_id_type=MESH)


_ANY = pl.BlockSpec(memory_space=pl.ANY)


def gather_shards(shards, *, name):
    n = len(shards)

    def body(*refs):
        ins, outs = refs[:n], refs[n:2 * n]
        ssem, rsem, lsem = refs[2 * n:]
        x, y, c, chips = _me()
        mine = _chip_id((x, y))
        sibling = (x, y, 1 - c)

        def rows(w, chip_id, which):
            h = shards[w].shape[0] // 2
            return outs[w].at[chip_id, pl.ds(which * h, h)]

        def local(w):
            return pltpu.make_async_copy(ins[w], outs[w].at[mine], lsem.at[w])

        def to_chip(w, j):
            h = shards[w].shape[0] // 2
            return _rcopy(ins[w].at[pl.ds(c * h, h)], rows(w, mine, c), ssem.at[w, j], rsem.at[w, j], (*chips[j], c))

        def from_chip(w, j):
            r = rows(w, _chip_id(chips[j]), c)
            return _rcopy(r, r, ssem.at[w, j], rsem.at[w, j], (*chips[j], c))

        def to_sibling(w, j):
            r = rows(w, _chip_id(chips[j]), c)
            return _rcopy(r, r, ssem.at[w, 3 + j], rsem.at[w, 3 + j], sibling)

        def from_sibling(w, j):
            r = rows(w, _chip_id(chips[j]), 1 - c)
            return _rcopy(r, r, ssem.at[w, 3 + j], rsem.at[w, 3 + j], sibling)

        for w in range(n):
            local(w).start()
            for j in range(3):
                to_chip(w, j).start()
        for w in range(n):
            for j in range(3):
                from_chip(w, j).wait_recv()
                to_sibling(w, j).start()
        for w in range(n):
            for j in range(3):
                from_sibling(w, j).wait_recv()
        for w in range(n):
            local(w).wait()
            for j in range(3):
                to_chip(w, j).wait_send()
                to_sibling(w, j).wait_send()

    return pl.pallas_call(
        body, name=name, in_specs=[_ANY] * n, out_specs=[_ANY] * n,
        out_shape=[jax.ShapeDtypeStruct((N_CHIPS,) + s.shape, s.dtype) for s in shards],
        scratch_shapes=[pltpu.SemaphoreType.DMA((n, 6)), pltpu.SemaphoreType.DMA((n, 6)), pltpu.SemaphoreType.DMA((n,))],
    )(*shards)


def rs_pair(parts, *, name):
    n = len(parts)

    def body(*refs):
        ins, own, recv = refs[:n], refs[n:2 * n], refs[2 * n:3 * n]
        ssem, rsem, lsem = refs[3 * n:]
        x, y, c, _ = _me()
        sibling = (x, y, 1 - c)
        for w in range(n):
            h = parts[w].shape[1] // 2
            pltpu.make_async_copy(ins[w].at[:, pl.ds(c * h, h), :], own[w], lsem.at[w]).start()
            _rcopy(ins[w].at[:, pl.ds((1 - c) * h, h), :], recv[w], ssem.at[w], rsem.at[w], sibling).start()
        for w in range(n):
            h = parts[w].shape[1] // 2
            pltpu.make_async_copy(ins[w].at[:, pl.ds(c * h, h), :], own[w], lsem.at[w]).wait()
            _rcopy(ins[w].at[:, pl.ds((1 - c) * h, h), :], recv[w], ssem.at[w], rsem.at[w], sibling).wait()

    half = [jax.ShapeDtypeStruct((p.shape[0], p.shape[1] // 2, p.shape[2]), p.dtype) for p in parts]
    res = pl.pallas_call(
        body, name=name, in_specs=[_ANY] * n, out_specs=[_ANY] * (2 * n), out_shape=half + half,
        scratch_shapes=[pltpu.SemaphoreType.DMA((n,)), pltpu.SemaphoreType.DMA((n,)), pltpu.SemaphoreType.DMA((n,))],
    )(*parts)
    return res[:n], res[n:]


def rs_chip(sums, *, name):
    n = len(sums)

    def body(*refs):
        ins, outs = refs[:n], refs[n:2 * n]
        ssem, rsem, lsem = refs[2 * n:]
        x, y, c, chips = _me()
        mine = _chip_id((x, y))
        for w in range(n):
            pltpu.make_async_copy(ins[w].at[mine], outs[w].at[mine], lsem.at[w]).start()
            for j, chip in enumerate(chips):
                _rcopy(ins[w].at[_chip_id(chip)], outs[w].at[mine], ssem.at[w, j], rsem.at[w, j], (*chip, c)).start()
        for w in range(n):
            pltpu.make_async_copy(ins[w].at[mine], outs[w].at[mine], lsem.at[w]).wait()
            for j, chip in enumerate(chips):
                cp = _rcopy(ins[w].at[_chip_id(chip)], outs[w].at[_chip_id(chip)], ssem.at[w, j], rsem.at[w, j], (*chip, c))
                cp.wait_send()
                cp.wait_recv()

    return pl.pallas_call(
        body, name=name, in_specs=[_ANY] * n, out_specs=[_ANY] * n,
        out_shape=[jax.ShapeDtypeStruct(s.shape, s.dtype) for s in sums],
        scratch_shapes=[pltpu.SemaphoreType.DMA((n, 3)), pltpu.SemaphoreType.DMA((n, 3)), pltpu.SemaphoreType.DMA((n,))],
    )(*sums)


def rs_share(halves, slots, group_shapes, *, name):
    n = len(halves)
    ng = len(group_shapes)

    def body(*refs):
        ins, outs = refs[:n], refs[n:n + ng]
        ssem, rsem, lsem = refs[n + ng:]
        x, y, c, _ = _me()
        sibling = (x, y, 1 - c)

        def copies(w):
            g, l = slots[w]
            h = halves[w].shape[0]
            mine_rows = outs[g].at[l, pl.ds(c * h, h), :]
            return (pltpu.make_async_copy(ins[w], mine_rows, lsem.at[w]),
                    _rcopy(ins[w], mine_rows, ssem.at[w], rsem.at[w], sibling))

        for w in range(n):
            loc, rem = copies(w)
            loc.start()
            rem.start()
        for w in range(n):
            g, l = slots[w]
            h = halves[w].shape[0]
            loc, rem = copies(w)
            loc.wait()
            rem.wait_send()
            theirs = outs[g].at[l, pl.ds((1 - c) * h, h), :]
            _rcopy(ins[w], theirs, ssem.at[w], rsem.at[w], sibling).wait_recv()

    return pl.pallas_call(
        body, name=name, in_specs=[_ANY] * n, out_specs=[_ANY] * ng,
        out_shape=[jax.ShapeDtypeStruct(s, F32) for s in group_shapes],
        scratch_shapes=[pltpu.SemaphoreType.DMA((n,)), pltpu.SemaphoreType.DMA((n,)), pltpu.SemaphoreType.DMA((n,))],
    )(*halves)


def add_pair(a, b, *, name):
    p, h, c = a.shape
    a2, b2 = a.reshape(p * h, c), b.reshape(p * h, c)
    br = _row_blk(p * h, 512)

    def body(a_ref, b_ref, o_ref):
        o_ref[...] = (a_ref[...].astype(F32) + b_ref[...].astype(F32)).astype(o_ref.dtype)

    spec = pl.BlockSpec((br, c), lambda i: (i, 0))
    return pl.pallas_call(
        body, name=name, grid=(p * h // br,), in_specs=[spec, spec], out_specs=spec,
        out_shape=jax.ShapeDtypeStruct((p * h, c), a.dtype), compiler_params=_params(("parallel",)),
    )(a2, b2).reshape(p, h, c)


def add_chips(a, *, name):
    p, h, c = a.shape
    br = _row_blk(h, 256)

    def body(a_ref, o_ref):
        tot = a_ref[0].astype(F32)
        for k in range(1, p):
            tot = tot + a_ref[k].astype(F32)
        o_ref[...] = tot

    return pl.pallas_call(
        body, name=name, grid=(h // br,), in_specs=[pl.BlockSpec((p, br, c), lambda i: (0, i, 0))],
        out_specs=pl.BlockSpec((br, c), lambda i: (i, 0)), out_shape=jax.ShapeDtypeStruct((h, c), F32),
        compiler_params=_params(("parallel",)),
    )(a)


def allreduce_small(v, *, name):
    r, c = v.shape
    ndev = 2 * N_CHIPS

    def body(v_ref, o_ref, buf, ssem, rsem):
        x, y, cc, _ = _me()
        me = 4 * x + 2 * y + cc
        flips = [(a, b, d) for a in (0, 1) for b in (0, 1) for d in (0, 1)][1:]
        buf[me] = v_ref[...]
        cps = []
        for k, (a, b, d) in enumerate(flips):
            peer = (jnp.bitwise_xor(x, a), jnp.bitwise_xor(y, b), jnp.bitwise_xor(cc, d))
            cp = _rcopy(v_ref, buf.at[me], ssem.at[k], rsem.at[k], peer)
            cp.start()
            cps.append(cp)
        for k, (a, b, d) in enumerate(flips):
            peer = (jnp.bitwise_xor(x, a), jnp.bitwise_xor(y, b), jnp.bitwise_xor(cc, d))
            src = 4 * peer[0] + 2 * peer[1] + peer[2]
            _rcopy(v_ref, buf.at[src], ssem.at[k], rsem.at[k], peer).wait_recv()
        for cp in cps:
            cp.wait_send()
        tot = buf[0]
        for k in range(1, ndev):
            tot = tot + buf[k]
        o_ref[...] = tot

    vm = pl.BlockSpec(memory_space=pltpu.VMEM)
    return pl.pallas_call(
        body, name=name, in_specs=[vm], out_specs=vm, out_shape=jax.ShapeDtypeStruct((r, c), F32),
        scratch_shapes=[pltpu.VMEM((ndev, r, c), F32), pltpu.SemaphoreType.DMA((ndev - 1,)), pltpu.SemaphoreType.DMA((ndev - 1,))],
    )(v)


WEIGHTS = ["lb_table", "ev_norm", "ev_w_in", "ev_w_pool", "ev_pool_scale", "ev_hg_norm", "ev_w_out", "od_norm", "od_w_in",
           "od_b_f", "od_w_out", "xa_norm", "xa_mem_norm", "xa_wq", "xa_wkv", "xa_wo", "ffn_norm", "ffn_w_gate", "ffn_w_up",
           "ffn_w_down", "final_norm"]
BIG = ["ev_w_in", "ev_w_pool", "ev_w_out", "od_w_in", "od_w_out", "xa_wq", "xa_wkv", "xa_wo", "ffn_w_gate", "ffn_w_up", "ffn_w_down"]
SMALL_ROWS = 16


def _rows(parts, width):
    rows = [jnp.pad(p.reshape(-1, p.shape[-1]).astype(F32), ((0, 0), (0, width - p.shape[-1]))) for p in parts]
    out = jnp.concatenate(rows, axis=0)
    return jnp.pad(out, ((0, SMALL_ROWS - out.shape[0]), (0, 0)))


def _unrows(packed, like):
    out, r = [], 0
    for p in like:
        n = p.size // p.shape[-1]
        out.append(packed[r:r + n, :p.shape[-1]].reshape(p.shape))
        r += n
    return out


def _m3(a):
    return a.reshape(a.shape[0], -1, a.shape[-1])


def kernel(x, mem, lb_table, ev_norm, ev_w_in, ev_w_pool, ev_pool_scale, ev_hg_norm, ev_w_out, od_norm, od_w_in, od_b_f, od_w_out, xa_norm, xa_mem_norm, xa_wq, xa_wkv, xa_wo, ffn_norm, ffn_w_gate, ffn_w_up, ffn_w_down, final_norm, loss_target, m_lb_table, m_ev_norm, m_ev_w_in, m_ev_w_pool, m_ev_pool_scale, m_ev_hg_norm, m_ev_w_out, m_od_norm, m_od_w_in, m_od_b_f, m_od_w_out, m_xa_norm, m_xa_mem_norm, m_xa_wq, m_xa_wkv, m_xa_wo, m_ffn_norm, m_ffn_w_gate, m_ffn_w_up, m_ffn_w_down, m_final_norm, v_lb_table, v_ev_norm, v_ev_w_in, v_ev_w_pool, v_ev_pool_scale, v_ev_hg_norm, v_ev_w_out, v_od_norm, v_od_w_in, v_od_b_f, v_od_w_out, v_xa_norm, v_xa_mem_norm, v_xa_wq, v_xa_wkv, v_xa_wo, v_ffn_norm, v_ffn_w_gate, v_ffn_w_up, v_ffn_w_down, v_final_norm):
    a = dict(locals())
    w = {n: a[n] for n in WEIGHTS}
    mom = {n: a["m_" + n] for n in WEIGHTS}
    var = {n: a["v_" + n] for n in WEIGHTS}
    _, t, d = x.shape
    nh = d // FOX_HEAD
    lanes = 128
    cx, cy = lax.axis_index("x"), lax.axis_index("y")
    chip = 2 * cx + cy

    bf = lambda v: v.astype(BF16)
    w3 = {n: _m3(w[n]) for n in BIG}
    flat = lambda v: v.reshape(-1, v.shape[-1])
    shards = [
        bf(jnp.concatenate([flat(w3["ev_w_out"]), flat(w3["od_w_out"]), flat(w3["xa_wq"]), flat(w3["xa_wo"])], axis=0)),
        bf(flat(w3["ffn_w_down"])), bf(flat(w3["ev_w_in"])), bf(flat(w3["od_w_in"])), bf(flat(w3["xa_wkv"])),
        bf(jnp.concatenate([flat(w3["ffn_w_gate"]), flat(w3["ffn_w_up"])], axis=0)),
        bf(flat(w3["ev_w_pool"])), jnp.broadcast_to(od_norm, (16, od_norm.shape[1])),
    ]
    g2048, gdown, gev_in, god_in, gwkv, ggu, gpool, godn = gather_shards(shards, name="gather_weights")
    rs_rows = d // N_CHIPS
    fs = ffn_w_down.shape[1]
    od_full = god_in.transpose(1, 0, 2).reshape(d, -1)
    ng, gsz = ev_w_pool.shape[1], ev_w_pool.shape[3]
    G = {
        "w2048": g2048.reshape(N_CHIPS, 6, rs_rows, d), "down": gdown.reshape(N_CHIPS, 2, fs, d), "ev_in": gev_in,
        "wkv": gwkv.reshape(N_CHIPS, 2, d, -1), "gu": ggu.reshape(N_CHIPS, 4, d, -1),
        "wqkv": od_full[:, :3 * d], "wf": jnp.pad(od_full[:, 3 * d:], ((0, 0), (0, lanes - nh))),
        "pool": gpool.reshape(N_CHIPS, ng, gsz // N_CHIPS, gsz).transpose(1, 0, 2, 3).reshape(ng, gsz, gsz),
    }
    od_norm_full = godn[:, 0, :].reshape(1, d)

    sm = jax.nn.softmax(lb_table, axis=0)
    sp = {
        "lb": sm[1:2], "ev_norm": ev_norm, "pool_scale": ev_pool_scale, "hg_gain": ev_hg_norm, "od_norm": od_norm_full,
        "bf": jnp.pad(od_b_f, ((0, 0), (0, lanes - nh))), "xa_norm": xa_norm, "xa_mem_norm": xa_mem_norm, "ffn_norm": ffn_norm,
        "final_norm": final_norm.reshape(1, d),
    }
    loss_l, gx, small, grads = _local_step(x[0], mem[0], loss_target[0], sp, G)
    loss = lax.psum(loss_l[0, 0], ("x", "y", "c"))

    od_in_parts = grads["od_in_full"].reshape(d, N_CHIPS, -1).transpose(1, 0, 2)
    pool_parts = bf(grads["pool_full"].reshape(ng, N_CHIPS, gsz // N_CHIPS, gsz).transpose(1, 0, 2, 3).reshape(N_CHIPS, gsz, gsz))
    ops = [("ev_w_in", 0, grads["ev_in"]), ("ev_w_pool", 0, pool_parts), ("ev_w_out", 0, grads["ev_out"]),
           ("od_w_in", 0, od_in_parts), ("od_w_out", 0, grads["od_out"])]
    for l in range(2):
        ops += [("xa_wq", l, grads[f"wq{l}"]), ("xa_wkv", l, grads[f"wkv{l}"]), ("xa_wo", l, grads[f"wo{l}"]),
                ("ffn_w_gate", l, grads[f"gate{l}"]), ("ffn_w_up", l, grads[f"up{l}"]), ("ffn_w_down", l, grads[f"down{l}"])]
    own, got = rs_pair([o[2] for o in ops], name="reduce_pair")
    chip_sums = [add_pair(o, g, name=f"reduce_add2_{k}") for k, (o, g) in enumerate(zip(own, got))]
    landed = rs_chip(chip_sums, name="reduce_chips")
    halves = [add_chips(v, name=f"reduce_add4_{k}") for k, v in enumerate(landed)]
    full = rs_share(halves, [(BIG.index(o[0]), o[1]) for o in ops], [w3[n].shape for n in BIG], name="reduce_share")
    gbig = dict(zip(BIG, full))

    raw_like = [small["lb"], small["ev_norm"], small["pool_scale"], small["hg_gain"], small["od_norm"], small["bf"],
                small["xa_norm"], small["xa_mem_norm"], small["ffn_norm"], small["final_norm"]]
    summed = _unrows(allreduce_small(_rows(raw_like, d), name="reduce_small"), raw_like)
    dlb, g_ev_norm, g_pool_scale, g_hg, g_od_norm_full, g_bf, g_xa, g_xam, g_ffn, g_final = summed
    dsm = jnp.zeros_like(sm).at[1:2].set(dlb)
    gsmall = {
        "lb_table": sm * (dsm - jnp.sum(sm * dsm, axis=0, keepdims=True)), "ev_norm": g_ev_norm, "ev_pool_scale": g_pool_scale,
        "ev_hg_norm": g_hg, "od_norm": lax.dynamic_slice_in_dim(g_od_norm_full, chip * od_norm.shape[1], od_norm.shape[1], axis=1),
        "od_b_f": g_bf[:, :nh], "xa_norm": g_xa, "xa_mem_norm": g_xam, "ffn_norm": g_ffn, "final_norm": g_final.reshape(d),
    }

    grad, delta, new_m, new_v = {}, {}, {}, {}
    for n in BIG:
        shp = w[n].shape
        res = adamw(flat(w3[n]), flat(gbig[n]), flat(_m3(mom[n])), flat(_m3(var[n])), name=f"adamw_{n}")
        grad[n], delta[n], new_m[n], new_v[n] = [r.reshape(shp) for r in res]
    snames = [n for n in WEIGHTS if n not in BIG]
    like = [w[n] for n in snames]
    res = adamw(_rows(like, d), _rows([gsmall[n] for n in snames], d), _rows([mom[n] for n in snames], d),
                _rows([var[n] for n in snames], d), name="adamw_small")
    for vals, dst in zip(res, (grad, delta, new_m, new_v)):
        dst.update(zip(snames, _unrows(vals, like)))
    return (loss, gx.reshape(x.shape), *[grad[n] for n in WEIGHTS], *[delta[n] for n in WEIGHTS],
            *[new_m[n] for n in WEIGHTS], *[new_v[n] for n in WEIGHTS])
```

```python
import math

import jax
import jax.numpy as jnp
from jax import lax
from jax.experimental import pallas as pl
from jax.experimental.pallas import tpu as pltpu

F32 = jnp.float32
BF16 = jnp.bfloat16
MESH = pl.DeviceIdType.MESH

V7X_VMEM_LIMIT_BYTES = 56 * 1024 * 1024
N_CHIPS = 4

EPS = 1e-6
POOL_WINDOWS = (2, 4, 8, 16)
POOL_HALO = 16
HG_HEAD = 128
HG_CHUNK = 64
FOX_HEAD = 128
FOX_BLK = 512
XA_HEADS = 4
ADAM_LR, ADAM_B1, ADAM_B2, ADAM_EPS, ADAM_WD, ADAM_STEP = 0.001, 0.9, 0.999, 1e-08, 0.01, 10
EXP_CLAMP = 80.0


def _params(sem=None):
    return pltpu.CompilerParams(dimension_semantics=sem, vmem_limit_bytes=V7X_VMEM_LIMIT_BYTES)


def _blk(pref, dim):
    b = min(pref, dim)
    assert dim % b == 0, (pref, dim)
    return b


class VM:
    def __init__(self, arr, kind="cs", lead=(), inner=(), pfn=None):
        self.arr, self.kind, self.lead, self.inner = arr, kind, tuple(lead), tuple(inner)
        self.pfn = pfn or (lambda p: p)
        p = arr.shape[len(self.lead)]
        r, c = arr.shape[-2:]
        assert arr.ndim == len(self.lead) + 1 + len(self.inner) + 2, (arr.shape, lead, inner)
        self.P = p
        self.shape = (r, c * p) if kind == "cs" else (r * p, c)
        self.dtype = arr.dtype

    def spec(self, br, bc, rfn, cfn):
        p = self.P
        r, c = self.arr.shape[-2:]
        assert c % bc == 0 and r % br == 0, (self.arr.shape, br, bc)
        if p == 1:
            def imap(*g):
                return (*self.lead, self.pfn(0), *self.inner, rfn(*g), cfn(*g))
        elif self.kind == "cs":
            per = c // bc

            def imap(*g):
                cb = cfn(*g)
                return (*self.lead, self.pfn(lax.div(cb, per)), *self.inner, rfn(*g), lax.rem(cb, per))
        else:
            per = r // br

            def imap(*g):
                rb = rfn(*g)
                return (*self.lead, self.pfn(lax.div(rb, per)), *self.inner, lax.rem(rb, per), cfn(*g))
        return pl.BlockSpec((None,) * (self.arr.ndim - 2) + (br, bc), imap)


def vm2(arr):
    return VM(arr.reshape((1,) + arr.shape))


def _out_struct(shape, kind, p, dtype):
    r, c = shape
    return jax.ShapeDtypeStruct((p, r, c // p) if kind == "cs" else (p, r // p, c), dtype)


def _best(g, cap):
    if g <= cap:
        return g
    cands = [d for d in range(128, cap + 1, 128) if g % d == 0]
    assert cands, (g, cap)
    return cands[-1]


def _row_blk(n, cap):
    cands = [d for d in range(16, min(n, cap) + 1, 16) if n % d == 0]
    assert cands, (n, cap)
    return cands[-1]


def _tiles(a, b, mode, out_kind, out_p, bm, bn, bk):
    def cpiece(v):
        return v.arr.shape[-1] if v.kind == "cs" else v.shape[1]

    def rpiece(v):
        return v.arr.shape[-2] if v.kind == "rs" else v.shape[0]

    if mode == "nn":
        m, n = a.shape[0], b.shape[1]
        gm, gn, gk = rpiece(a), cpiece(b), math.gcd(cpiece(a), rpiece(b))
    elif mode == "nt":
        m, n = a.shape[0], b.shape[0]
        gm, gn, gk = rpiece(a), rpiece(b), math.gcd(cpiece(a), cpiece(b))
    else:
        m, n = a.shape[1], b.shape[1]
        gm, gn, gk = cpiece(a), cpiece(b), math.gcd(rpiece(a), rpiece(b))
    if out_kind == "cs":
        gn = math.gcd(gn, n // out_p)
    else:
        gm = math.gcd(gm, m // out_p)
    caps = {"nn": (512, 1536, 2048), "nt": (512, 2048, 2048), "tn": (1536, 1536, 1024)}[mode]
    return (bm or _best(gm, caps[0])), (bn or _best(gn, caps[1])), (bk or _best(gk, caps[2]))


def matmul(a, b, mode, *, out_dtype, bm=None, bn=None, bk=None, out_kind="cs", out_p=1, out_pfn=None, res=None, name):
    bm, bn, bk = _tiles(a, b, mode, out_kind, out_p, bm, bn, bk)
    if mode == "nn":
        (m, k), (k2, n) = a.shape, b.shape
        a_spec = a.spec(bm, bk, lambda i, j, kk: i, lambda i, j, kk: kk)
        b_spec = b.spec(bk, bn, lambda i, j, kk: kk, lambda i, j, kk: j)
        dims = (((1,), (0,)), ((), ()))
    elif mode == "nt":
        (m, k), (n, k2) = a.shape, b.shape
        a_spec = a.spec(bm, bk, lambda i, j, kk: i, lambda i, j, kk: kk)
        b_spec = b.spec(bn, bk, lambda i, j, kk: j, lambda i, j, kk: kk)
        dims = (((1,), (1,)), ((), ()))
    else:
        (k, m), (k2, n) = a.shape, b.shape
        a_spec = a.spec(bk, bm, lambda i, j, kk: kk, lambda i, j, kk: i)
        b_spec = b.spec(bk, bn, lambda i, j, kk: kk, lambda i, j, kk: j)
        dims = (((0,), (0,)), ((), ()))
    assert k == k2, (a.shape, b.shape, mode)
    assert m % bm == 0 and n % bn == 0 and k % bk == 0, (m, n, k, bm, bn, bk)
    nk = k // bk
    out_sds = _out_struct((m, n), out_kind, out_p, out_dtype)
    out_vm = VM(out_sds, out_kind, pfn=out_pfn)
    o_spec = out_vm.spec(bm, bn, lambda i, j, kk: i, lambda i, j, kk: j)
    in_specs, args = [a_spec, b_spec], [a.arr, b.arr]
    if res is not None:
        assert res.shape == (m, n)
        in_specs.append(res.spec(bm, bn, lambda i, j, kk: i, lambda i, j, kk: j))
        args.append(res.arr)

    def body(a_ref, b_ref, *rest):
        if res is not None:
            r_ref, o_ref = rest[0], rest[1]
        else:
            r_ref, o_ref = None, rest[0]
        part = lax.dot_general(a_ref[...], b_ref[...], dims, preferred_element_type=F32)
        if nk == 1:
            if r_ref is not None:
                part = part + r_ref[...].astype(F32)
            o_ref[...] = part.astype(o_ref.dtype)
            return
        acc = rest[-1]
        kk = pl.program_id(2)

        @pl.when(kk == 0)
        def _():
            acc[...] = part

        @pl.when(kk > 0)
        def _():
            acc[...] += part

        @pl.when(kk == nk - 1)
        def _():
            tot = acc[...]
            if r_ref is not None:
                tot = tot + r_ref[...].astype(F32)
            o_ref[...] = tot.astype(o_ref.dtype)

    return pl.pallas_call(
        body, name=name, grid=(m // bm, n // bn, nk), in_specs=in_specs, out_specs=o_spec, out_shape=out_sds,
        scratch_shapes=[pltpu.VMEM((bm, bn), F32)] if nk > 1 else [],
        compiler_params=_params(("parallel", "parallel", "arbitrary")),
    )(*args)


def rmsnorm_fwd(x, g, *, name):
    t, d = x.shape
    bt = _blk(512, t)

    def body(x_ref, g_ref, o_ref):
        xv = x_ref[...]
        r = lax.rsqrt(jnp.mean(xv * xv, axis=-1, keepdims=True) + EPS)
        o_ref[...] = (xv * r * g_ref[...]).astype(o_ref.dtype)

    return pl.pallas_call(
        body, name=name, grid=(t // bt,),
        in_specs=[pl.BlockSpec((bt, d), lambda i: (i, 0)), pl.BlockSpec((1, d), lambda i: (0, 0))],
        out_specs=pl.BlockSpec((bt, d), lambda i: (i, 0)), out_shape=jax.ShapeDtypeStruct((t, d), BF16),
        compiler_params=_params(("parallel",)),
    )(x, g)


def rmsnorm_bwd(x, g, dh, dres, *, name):
    t, d = x.shape
    bt = _blk(256, t)
    want_dx = dres is not None

    def body(x_ref, g_ref, dh_ref, *rest):
        if want_dx:
            dres_ref, dx_ref, dxb_ref, dg_ref = rest
        else:
            (dg_ref,) = rest
        xv = x_ref[...]
        dhv = dh_ref[...].astype(F32)
        r = lax.rsqrt(jnp.mean(xv * xv, axis=-1, keepdims=True) + EPS)
        xh = xv * r
        part = jnp.sum(dhv * xh, axis=0, keepdims=True)

        @pl.when(pl.program_id(0) == 0)
        def _():
            dg_ref[...] = part

        @pl.when(pl.program_id(0) > 0)
        def _():
            dg_ref[...] += part

        if want_dx:
            dy = dhv * g_ref[...]
            dxn = r * (dy - xh * jnp.mean(dy * xh, axis=-1, keepdims=True))
            dx = dres_ref[...] + dxn
            dx_ref[...] = dx
            dxb_ref[...] = dx.astype(BF16)

    row = pl.BlockSpec((bt, d), lambda i: (i, 0))
    vec = pl.BlockSpec((1, d), lambda i: (0, 0))
    in_specs, args = [row, vec, row], [x, g, dh]
    out_specs, out_shape = [vec], [jax.ShapeDtypeStruct((1, d), F32)]
    if want_dx:
        in_specs.append(row)
        args.append(dres)
        out_specs = [row, row] + out_specs
        out_shape = [jax.ShapeDtypeStruct((t, d), F32), jax.ShapeDtypeStruct((t, d), BF16)] + out_shape
    return pl.pallas_call(
        body, name=name, grid=(t // bt,), in_specs=in_specs, out_specs=out_specs, out_shape=out_shape,
        compiler_params=_params(("arbitrary",)),
    )(*args)


def loss_head(x, g, tgt, *, name):
    t, d = x.shape
    bt = _blk(256, t)

    def body(x_ref, g_ref, t_ref, loss_ref, dx_ref, dxb_ref, dg_ref):
        xv = x_ref[...]
        gv = g_ref[...]
        r = lax.rsqrt(jnp.mean(xv * xv, axis=-1, keepdims=True) + EPS)
        xh = xv * r
        e = xh * gv - t_ref[...]
        lpart = jnp.zeros((1, 128), F32) + jnp.sum(e * e) * (0.5 / d)
        dyv = e * (1.0 / d)
        gpart = jnp.sum(dyv * xh, axis=0, keepdims=True)

        @pl.when(pl.program_id(0) == 0)
        def _():
            loss_ref[...] = lpart
            dg_ref[...] = gpart

        @pl.when(pl.program_id(0) > 0)
        def _():
            loss_ref[...] += lpart
            dg_ref[...] += gpart

        dy = dyv * gv
        dx = r * (dy - xh * jnp.mean(dy * xh, axis=-1, keepdims=True))
        dx_ref[...] = dx
        dxb_ref[...] = dx.astype(BF16)

    row = pl.BlockSpec((bt, d), lambda i: (i, 0))
    vec = pl.BlockSpec((1, d), lambda i: (0, 0))
    return pl.pallas_call(
        body, name=name, grid=(t // bt,), in_specs=[row, vec, row],
        out_specs=[pl.BlockSpec((1, 128), lambda i: (0, 0)), row, row, vec],
        out_shape=[jax.ShapeDtypeStruct((1, 128), F32), jax.ShapeDtypeStruct((t, d), F32),
                   jax.ShapeDtypeStruct((t, d), BF16), jax.ShapeDtypeStruct((1, d), F32)],
        compiler_params=_params(("arbitrary",)),
    )(x, g, tgt)


def _sigmoid(x):
    return 1.0 / (1.0 + jnp.exp(-x))


def swiglu_fwd(a, b, *, name):
    t, f = a.shape
    bt, bf = _blk(512, t), _best(f, 1536)

    def body(a_ref, b_ref, o_ref):
        av = a_ref[...].astype(F32)
        o_ref[...] = (av * _sigmoid(av) * b_ref[...].astype(F32)).astype(o_ref.dtype)

    spec = pl.BlockSpec((bt, bf), lambda i, j: (i, j))
    return pl.pallas_call(
        body, name=name, grid=(t // bt, f // bf), in_specs=[spec, spec], out_specs=spec,
        out_shape=jax.ShapeDtypeStruct((t, f), BF16), compiler_params=_params(("parallel", "parallel")),
    )(a, b)


def swiglu_bwd(ds, a, b, *, name):
    t, f = a.shape
    bt, bf = _blk(512, t), _best(f, 1536)

    def body(ds_ref, a_ref, b_ref, da_ref, db_ref):
        av = a_ref[...].astype(F32)
        dsv = ds_ref[...].astype(F32)
        sg = _sigmoid(av)
        da_ref[...] = (dsv * b_ref[...].astype(F32) * sg * (1.0 + av * (1.0 - sg))).astype(BF16)
        db_ref[...] = (dsv * av * sg).astype(BF16)

    spec = pl.BlockSpec((bt, bf), lambda i, j: (i, j))
    return pl.pallas_call(
        body, name=name, grid=(t // bt, f // bf), in_specs=[spec, spec, spec], out_specs=[spec, spec],
        out_shape=[jax.ShapeDtypeStruct((t, f), BF16)] * 2, compiler_params=_params(("parallel", "parallel")),
    )(ds, a, b)


def _xa_probs(qh, kh, scale):
    s = lax.dot_general(qh, kh, (((1,), (1,)), ((), ())), preferred_element_type=F32) * scale
    s = s - jnp.max(s, axis=-1, keepdims=True)
    p = jnp.exp(s)
    return p / jnp.sum(p, axis=-1, keepdims=True)


def xattn_fwd(q, kv, *, name):
    t, d = q.shape
    m = kv.shape[0]
    hd = d // XA_HEADS
    bt = _blk(512, t)
    scale = hd ** -0.5

    def body(q_ref, kv_ref, o_ref):
        for h in range(XA_HEADS):
            qh = q_ref[:, h * hd:(h + 1) * hd]
            kh = kv_ref[:, h * hd:(h + 1) * hd]
            vh = kv_ref[:, d + h * hd:d + (h + 1) * hd]
            p = _xa_probs(qh, kh, scale)
            o_ref[:, h * hd:(h + 1) * hd] = jnp.dot(p.astype(BF16), vh, preferred_element_type=F32).astype(BF16)

    return pl.pallas_call(
        body, name=name, grid=(t // bt,),
        in_specs=[pl.BlockSpec((bt, d), lambda i: (i, 0)), pl.BlockSpec((m, 2 * d), lambda i: (0, 0))],
        out_specs=pl.BlockSpec((bt, d), lambda i: (i, 0)), out_shape=jax.ShapeDtypeStruct((t, d), BF16),
        compiler_params=_params(("parallel",)),
    )(q, kv)


def xattn_bwd(q, kv, do, *, name):
    t, d = q.shape
    m = kv.shape[0]
    hd = d // XA_HEADS
    bt = _blk(512, t)
    scale = hd ** -0.5

    def body(q_ref, kv_ref, do_ref, dq_ref, dkv_ref):
        first = pl.program_id(0) == 0
        for h in range(XA_HEADS):
            qs, ks, vs = slice(h * hd, (h + 1) * hd), slice(h * hd, (h + 1) * hd), slice(d + h * hd, d + (h + 1) * hd)
            qh, kh, vh, doh = q_ref[:, qs], kv_ref[:, ks], kv_ref[:, vs], do_ref[:, qs]
            p = _xa_probs(qh, kh, scale)
            dp = lax.dot_general(doh, vh, (((1,), (1,)), ((), ())), preferred_element_type=F32)
            dsv = p * (dp - jnp.sum(p * dp, axis=-1, keepdims=True)) * scale
            dsb = dsv.astype(BF16)
            dq_ref[:, qs] = jnp.dot(dsb, kh, preferred_element_type=F32).astype(BF16)
            dk = lax.dot_general(dsb, qh, (((0,), (0,)), ((), ())), preferred_element_type=F32)
            dv = lax.dot_general(p.astype(BF16), doh, (((0,), (0,)), ((), ())), preferred_element_type=F32)

            @pl.when(first)
            def _():
                dkv_ref[:, ks] = dk
                dkv_ref[:, vs] = dv

            @pl.when(jnp.logical_not(first))
            def _():
                dkv_ref[:, ks] += dk
                dkv_ref[:, vs] += dv

    row = pl.BlockSpec((bt, d), lambda i: (i, 0))
    full = pl.BlockSpec((m, 2 * d), lambda i: (0, 0))
    return pl.pallas_call(
        body, name=name, grid=(t // bt,), in_specs=[row, full, row], out_specs=[row, full],
        out_shape=[jax.ShapeDtypeStruct((t, d), BF16), jax.ShapeDtypeStruct((m, 2 * d), F32)],
        compiler_params=_params(("arbitrary",)),
    )(q, kv, do)


def _pool_p(buf, uv, rows, w, bt):
    acc = uv
    for dd in range(1, w):
        acc = acc + buf[pl.ds(POOL_HALO - dd, bt), :]
    cnt = jnp.minimum(rows + 1, w).astype(F32)
    return acc / cnt - uv


def pool_fwd(z, w_pool, scale, *, name):
    t = z.shape[0]
    ng, gsz = w_pool.shape[0], w_pool.shape[1]
    mix = ng * gsz
    bt = _blk(512, t)

    def body(u_ref, uh_ref, w_ref, sc_ref, o_ref, buf):
        r = pl.program_id(0)
        rows = r * bt + lax.broadcasted_iota(jnp.int32, (bt, 1), 0)
        for g in range(ng):
            gs = slice(g * gsz, (g + 1) * gsz)
            uv = u_ref[:, gs]
            buf[0:POOL_HALO, :] = jnp.where(r > 0, uh_ref[:, gs], 0.0)
            buf[POOL_HALO:POOL_HALO + bt, :] = uv
            p = _pool_p(buf, uv, rows, POOL_WINDOWS[g], bt)
            y = jnp.dot(p.astype(BF16), w_ref[g], preferred_element_type=F32) * sc_ref[:, gs]
            o_ref[:, gs] = y.astype(BF16)

    hb = bt // POOL_HALO
    return pl.pallas_call(
        body, name=name, grid=(t // bt,),
        in_specs=[pl.BlockSpec((bt, mix), lambda i: (i, 0)),
                  pl.BlockSpec((POOL_HALO, mix), lambda i: (jnp.maximum(i * hb - 1, 0), 0)),
                  pl.BlockSpec((ng, gsz, gsz), lambda i: (0, 0, 0)), pl.BlockSpec((1, mix), lambda i: (0, 0))],
        out_specs=pl.BlockSpec((None, bt, mix), lambda i: (0, i, 0)),
        out_shape=jax.ShapeDtypeStruct((2, t, mix), BF16),
        scratch_shapes=[pltpu.VMEM((POOL_HALO + bt, gsz), F32)],
        compiler_params=_params(("parallel",)),
    )(z, z, w_pool, scale)


def pool_bwd(z, dcat, w_pool, scale, *, name):
    t = z.shape[0]
    ng, gsz = w_pool.shape[0], w_pool.shape[1]
    mix = ng * gsz
    bt = _blk(512, t)
    nb = t // bt
    nt_dims = (((1,), (1,)), ((), ()))
    tn_dims = (((0,), (0,)), ((), ()))

    def body(u_ref, uh_ref, dy_ref, dyh_ref, w_ref, sc_ref, du_ref, dw_ref, dsc_ref, buf, buf2):
        r = pl.program_id(0)
        first = r == 0
        rows = r * bt + lax.broadcasted_iota(jnp.int32, (bt, 1), 0)
        rows_h = (r + 1) * bt + lax.broadcasted_iota(jnp.int32, (POOL_HALO, 1), 0)
        for g in range(ng):
            w = POOL_WINDOWS[g]
            gs = slice(g * gsz, (g + 1) * gsz)
            uv = u_ref[:, gs]
            buf[0:POOL_HALO, :] = jnp.where(r > 0, uh_ref[:, gs], 0.0)
            buf[POOL_HALO:POOL_HALO + bt, :] = uv
            pb = _pool_p(buf, uv, rows, w, bt).astype(BF16)
            wg = w_ref[g]
            sc = sc_ref[:, gs]
            y0 = jnp.dot(pb, wg, preferred_element_type=F32)
            dyv = dy_ref[:, gs].astype(F32)
            dsc = jnp.sum(dyv * y0, axis=0, keepdims=True)
            dyw = (dyv * sc).astype(BF16)
            dw = lax.dot_general(pb, dyw, tn_dims, preferred_element_type=F32)

            @pl.when(first)
            def _():
                dw_ref[g] = dw
                dsc_ref[:, gs] = dsc

            @pl.when(jnp.logical_not(first))
            def _():
                dw_ref[g] += dw
                dsc_ref[:, gs] += dsc

            dp = lax.dot_general(dyw, wg, nt_dims, preferred_element_type=F32)
            dyh = (dyh_ref[:, gs].astype(F32) * sc).astype(BF16)
            dph = lax.dot_general(dyh, wg, nt_dims, preferred_element_type=F32)
            dph = jnp.where(r < nb - 1, dph, 0.0)
            buf2[0:bt, :] = dp / jnp.minimum(rows + 1, w).astype(F32)
            buf2[bt:bt + POOL_HALO, :] = dph / jnp.minimum(rows_h + 1, w).astype(F32)
            acc = buf2[pl.ds(0, bt), :]
            for dd in range(1, w):
                acc = acc + buf2[pl.ds(dd, bt), :]
            du_ref[:, gs] = (acc - dp).astype(BF16)

    hb = bt // POOL_HALO
    nhb = t // POOL_HALO
    return pl.pallas_call(
        body, name=name, grid=(nb,),
        in_specs=[pl.BlockSpec((bt, mix), lambda i: (i, 0)),
                  pl.BlockSpec((POOL_HALO, mix), lambda i: (jnp.maximum(i * hb - 1, 0), 0)),
                  pl.BlockSpec((None, bt, mix), lambda i: (0, i, 0)),
                  pl.BlockSpec((None, POOL_HALO, mix), lambda i: (0, jnp.minimum((i + 1) * hb, nhb - 1), 0)),
                  pl.BlockSpec((ng, gsz, gsz), lambda i: (0, 0, 0)), pl.BlockSpec((1, mix), lambda i: (0, 0))],
        out_specs=[pl.BlockSpec((None, bt, mix), lambda i: (4, i, 0)),
                   pl.BlockSpec((ng, gsz, gsz), lambda i: (0, 0, 0)), pl.BlockSpec((1, mix), lambda i: (0, 0))],
        out_shape=[jax.ShapeDtypeStruct((5, t, mix), BF16), jax.ShapeDtypeStruct((ng, gsz, gsz), F32),
                   jax.ShapeDtypeStruct((1, mix), F32)],
        scratch_shapes=[pltpu.VMEM((POOL_HALO + bt, gsz), F32), pltpu.VMEM((bt + POOL_HALO, gsz), F32)],
        compiler_params=_params(("arbitrary",)),
    )(z, z, dcat, dcat, w_pool, scale)


HG_LEVELS = ((64, 31), (32, 15), (16, 7))
HG_DIAG = (8, 3)
_NT = (((1,), (1,)), ((), ()))
_TN = (((0,), (0,)), ((), ()))
_HI = lax.Precision.HIGHEST


def _hg_masks():
    c = HG_CHUNK
    t = lax.broadcasted_iota(jnp.int32, (c, c), 0)
    s = lax.broadcasted_iota(jnp.int32, (c, c), 1)
    masks = []
    for blk, row in HG_LEVELS:
        sh = blk.bit_length() - 1
        same = (t >> sh) == (s >> sh)
        masks.append(same & ((t & (blk - 1)) > row) & ((s & (blk - 1)) <= row))
    sh = HG_DIAG[0].bit_length() - 1
    masks.append(((t >> sh) == (s >> sh)) & (s <= t))
    return t, s, masks


def _row_of_block(x, blk, row):
    c, n = x.shape
    x3 = x.reshape(c // blk, blk, n)
    return jnp.broadcast_to(x3[:, row:row + 1, :], x3.shape).reshape(c, n)


def _hg_parts(qv, flv, lb, masks, tri):
    sgf = _sigmoid(flv)
    f = lb + (1.0 - lb) * sgf
    logf = jnp.log(f)
    kk = 1.0 - f
    sgq = _sigmoid(qv)
    qf = qv * sgq * (HG_HEAD ** -0.5)
    bc = jnp.dot(tri, logf, preferred_element_type=F32, precision=_HI)
    levels = []
    a = None
    for li, (blk, row) in enumerate(HG_LEVELS + (HG_DIAG,)):
        e = bc - _row_of_block(bc, blk, row)
        if li < len(HG_LEVELS):
            eq, ek = jnp.exp(jnp.minimum(e, 0.0)), jnp.exp(jnp.minimum(-e, 0.0))
        else:
            eq, ek = jnp.exp(jnp.clip(e, -EXP_CLAMP, EXP_CLAMP)), jnp.exp(jnp.clip(-e, -EXP_CLAMP, EXP_CLAMP))
        qt, kt = qf * eq, kk * ek
        part = jnp.where(masks[li], lax.dot_general(qt.astype(BF16), kt.astype(BF16), _NT, preferred_element_type=F32), 0.0)
        a = part if a is None else a + part
        levels.append((eq, ek, qt, kt))
    return dict(sgf=sgf, f=f, kk=kk, sgq=sgq, qf=qf, bc=bc, levels=levels, a=a)


def hgrn_fwd(z, cat, lb, gain, mix_a, *, name):
    t = z.shape[0]
    mix_b = lb.shape[1]
    nh = mix_b // HG_HEAD
    bt = _blk(256, t)
    ncb = bt // HG_CHUNK
    dh = HG_HEAD

    def body(q_ref, fl_ref, i_ref, g_ref, lb_ref, gain_ref, cat_in, o_ref, st_ref, st):
        del cat_in

        @pl.when(pl.program_id(1) == 0)
        def _():
            st[...] = jnp.zeros_like(st)

        t_i, s_i, masks = _hg_masks()
        tri = (s_i <= t_i).astype(F32)
        lbv, gn = lb_ref[...], gain_ref[...]
        for c in range(ncb):
            rs = slice(c * HG_CHUNK, (c + 1) * HG_CHUNK)
            pr = _hg_parts(q_ref[rs, :], fl_ref[rs, :], lbv, masks, tri)
            vb = i_ref[rs, :].astype(BF16)
            stv = st[...]
            st_ref[c] = stv
            bc = pr["bc"]
            qt = pr["qf"] * jnp.exp(bc)
            o = (jnp.dot(pr["a"].astype(BF16), vb, preferred_element_type=F32)
                 + lax.dot_general(qt.astype(BF16), stv.astype(BF16), _NT, preferred_element_type=F32))
            bl = bc[HG_CHUNK - 1:HG_CHUNK, :]
            khat = pr["kk"] * jnp.exp(bl - bc)
            st[...] = stv * jnp.exp(bl) + lax.dot_general(vb, khat.astype(BF16), _TN, preferred_element_type=F32)
            r = lax.rsqrt(jnp.mean(o * o, axis=-1, keepdims=True) + EPS)
            gv = g_ref[rs, :]
            o_ref[rs, :] = (o * r * gn * (gv * _sigmoid(gv))).astype(BF16)

    def col(which):
        base = (mix_a + which * mix_b) // dh
        return pl.BlockSpec((bt, dh), lambda h, i: (i, base + h))

    return pl.pallas_call(
        body, name=name, grid=(nh, t // bt),
        in_specs=[col(0), col(1), col(2), col(3), pl.BlockSpec((1, dh), lambda h, i: (0, h)),
                  pl.BlockSpec((1, dh), lambda h, i: (0, 0)), pl.BlockSpec(memory_space=pl.ANY)],
        out_specs=[pl.BlockSpec((None, bt, dh), lambda h, i: (1, i, h)),
                   pl.BlockSpec((None, ncb, dh, dh), lambda h, i: (h, i, 0, 0))],
        out_shape=[jax.ShapeDtypeStruct(cat.shape, BF16), jax.ShapeDtypeStruct((nh, t // HG_CHUNK, dh, dh), F32)],
        scratch_shapes=[pltpu.VMEM((dh, dh), F32)],
        input_output_aliases={6: 0},
        compiler_params=_params(("parallel", "arbitrary")),
    )(z, z, z, z, lb, gain, cat)


def hgrn_bwd(z, dcat, dz5, states, lb, gain, mix_a, *, name):
    t = z.shape[0]
    mix_b = lb.shape[1]
    nh = mix_b // HG_HEAD
    bt = _blk(256, t)
    nb = t // bt
    ncb = bt // HG_CHUNK
    dh = HG_HEAD

    def body(q_ref, fl_ref, i_ref, g_ref, dy_ref, st_ref, lb_ref, gain_ref, dz_in, dz_ref, dlb_ref, dgn_ref, dst):
        del dz_in
        first = pl.program_id(1) == 0

        @pl.when(first)
        def _():
            dst[...] = jnp.zeros_like(dst)

        t_i, s_i, masks = _hg_masks()
        tri = (s_i <= t_i).astype(F32)
        triu = (s_i >= t_i).astype(F32)
        last_row = lax.broadcasted_iota(jnp.int32, (HG_CHUNK, 1), 0) == HG_CHUNK - 1
        lbv, gn = lb_ref[...], gain_ref[...]
        dlb_acc = jnp.zeros((1, dh), F32)
        dgn_acc = jnp.zeros((1, dh), F32)
        for c in reversed(range(ncb)):
            rs = slice(c * HG_CHUNK, (c + 1) * HG_CHUNK)
            qv, flv, gv = q_ref[rs, :], fl_ref[rs, :], g_ref[rs, :]
            pr = _hg_parts(qv, flv, lbv, masks, tri)
            vb = i_ref[rs, :].astype(BF16)
            stv = st_ref[c]
            stb = stv.astype(BF16)
            dsv = dst[...]
            dsb = dsv.astype(BF16)
            bc, kk, qf, ab = pr["bc"], pr["kk"], pr["qf"], pr["a"].astype(BF16)
            ebc = jnp.exp(bc)
            qt = qf * ebc
            qtb = qt.astype(BF16)
            o = jnp.dot(ab, vb, preferred_element_type=F32) + lax.dot_general(qtb, stb, _NT, preferred_element_type=F32)
            r = lax.rsqrt(jnp.mean(o * o, axis=-1, keepdims=True) + EPS)
            oh = o * r
            sgg = _sigmoid(gv)
            dyv = dy_ref[rs, :].astype(F32)
            don = dyv * (gv * sgg)
            dgate = dyv * (oh * gn) * (sgg * (1.0 + gv * (1.0 - sgg)))
            dgn_acc = dgn_acc + jnp.sum(don * oh, axis=0, keepdims=True)
            doh = don * gn
            do = r * (doh - oh * jnp.mean(doh * oh, axis=-1, keepdims=True))
            dob = do.astype(BF16)
            bl = bc[HG_CHUNK - 1:HG_CHUNK, :]
            ebl = jnp.exp(bl)
            ekh = jnp.exp(bl - bc)
            khat = kk * ekh
            dv = (lax.dot_general(ab, dob, _TN, preferred_element_type=F32)
                  + lax.dot_general(khat.astype(BF16), dsb, _NT, preferred_element_type=F32))
            da = lax.dot_general(dob, vb, _NT, preferred_element_type=F32)
            dqt = jnp.dot(dob, stb, preferred_element_type=F32)
            dkh = jnp.dot(vb, dsb, preferred_element_type=F32)
            dst[...] = dsv * ebl + lax.dot_general(dob, qtb, _TN, preferred_element_type=F32)
            dbl = jnp.sum(dsv * stv, axis=0, keepdims=True) * ebl + jnp.sum(dkh * khat, axis=0, keepdims=True)
            dqf = dqt * ebc
            dkk = dkh * ekh
            dbc = dqt * qt - dkh * khat
            for li, (eq, ek, qtl, ktl) in enumerate(pr["levels"]):
                gm = jnp.where(masks[li], da, 0.0).astype(BF16)
                qtr, ktr = qtl.astype(BF16), ktl.astype(BF16)
                dql = jnp.dot(gm, ktr, preferred_element_type=F32)
                dkl = lax.dot_general(gm, qtr, _TN, preferred_element_type=F32)
                dqf = dqf + dql * eq
                dkk = dkk + dkl * ek
                dbc = dbc + qtr.astype(F32) * dql - ktr.astype(F32) * dkl
            dbc = dbc + jnp.where(last_row, dbl, 0.0)
            dlogf = jnp.dot(triu, dbc, preferred_element_type=F32, precision=_HI)
            df = dlogf / pr["f"] - dkk
            sgf = pr["sgf"]
            dfl = df * (1.0 - lbv) * sgf * (1.0 - sgf)
            dlb_acc = dlb_acc + jnp.sum(df * (1.0 - sgf), axis=0, keepdims=True)
            sgq = pr["sgq"]
            dq = dqf * (HG_HEAD ** -0.5) * (sgq * (1.0 + qv * (1.0 - sgq)))
            dz_ref[0, rs, :] = dq.astype(BF16)
            dz_ref[1, rs, :] = dfl.astype(BF16)
            dz_ref[2, rs, :] = dv.astype(BF16)
            dz_ref[3, rs, :] = dgate.astype(BF16)

        @pl.when(first)
        def _():
            dlb_ref[...] = dlb_acc
            dgn_ref[...] = dgn_acc

        @pl.when(jnp.logical_not(first))
        def _():
            dlb_ref[...] += dlb_acc
            dgn_ref[...] += dgn_acc

    def col(which):
        base = (mix_a + which * mix_b) // dh
        return pl.BlockSpec((bt, dh), lambda h, i: (nb - 1 - i, base + h))

    return pl.pallas_call(
        body, name=name, grid=(nh, nb),
        in_specs=[col(0), col(1), col(2), col(3),
                  pl.BlockSpec((None, bt, dh), lambda h, i: (1, nb - 1 - i, h)),
                  pl.BlockSpec((None, ncb, dh, dh), lambda h, i: (h, nb - 1 - i, 0, 0)),
                  pl.BlockSpec((1, dh), lambda h, i: (0, h)), pl.BlockSpec((1, dh), lambda h, i: (0, 0)),
                  pl.BlockSpec(memory_space=pl.ANY)],
        out_specs=[pl.BlockSpec((4, bt, dh), lambda h, i: (0, nb - 1 - i, h)),
                   pl.BlockSpec((1, dh), lambda h, i: (0, h)),
                   pl.BlockSpec((None, 1, dh), lambda h, i: (h, 0, 0))],
        out_shape=[jax.ShapeDtypeStruct(dz5.shape, BF16), jax.ShapeDtypeStruct((1, mix_b), F32),
                   jax.ShapeDtypeStruct((nh, 1, dh), F32)],
        scratch_shapes=[pltpu.VMEM((dh, dh), F32)],
        input_output_aliases={8: 0},
        compiler_params=_params(("parallel", "arbitrary")),
    )(z, z, z, z, dcat, states, lb, gain, dz5)


def _fox_scores(qb, kb, fk, scale, masked):
    s = lax.dot_general(qb, kb, _NT, preferred_element_type=F32) * scale - fk
    if masked:
        n = s.shape[0]
        row = lax.broadcasted_iota(jnp.int32, (n, n), 0)
        col = lax.broadcasted_iota(jnp.int32, (n, n), 1)
        s = jnp.where(col <= row, s, -jnp.inf)
    return s


def fox_fwd(qkv, fk, *, name):
    _, t, d = qkv.shape
    nh = d // FOX_HEAD
    b = _blk(FOX_BLK, t)
    nb = t // b
    dh = FOX_HEAD
    scale = dh ** -0.5

    def body(q_ref, k_ref, v_ref, f_ref, o_ref, lse_ref):
        qi = pl.program_id(1)
        qb = q_ref[...]

        def step(kj, carry, masked):
            m, l, acc = carry
            off = pl.multiple_of(kj * b, b)
            s = _fox_scores(qb, k_ref[pl.ds(off, b), :], f_ref[kj], scale, masked)
            m_new = jnp.maximum(m, jnp.max(s, axis=-1, keepdims=True))
            alpha = jnp.exp(m - m_new)
            p = jnp.exp(s - m_new)
            l = alpha * l + jnp.sum(p, axis=-1, keepdims=True)
            acc = alpha * acc + jnp.dot(p.astype(BF16), v_ref[pl.ds(off, b), :], preferred_element_type=F32)
            return m_new, l, acc

        init = (jnp.full((b, 1), -jnp.inf, F32), jnp.zeros((b, 1), F32), jnp.zeros((b, dh), F32))
        carry = lax.fori_loop(0, qi, lambda kj, c: step(kj, c, False), init)
        m, l, acc = step(qi, carry, True)
        o_ref[...] = (acc / l).astype(BF16)
        lse_ref[...] = m + jnp.log(l)

    return pl.pallas_call(
        body, name=name, grid=(nh, nb),
        in_specs=[pl.BlockSpec((None, b, dh), lambda h, i: (0, i, h)),
                  pl.BlockSpec((None, t, dh), lambda h, i: (1, 0, h)),
                  pl.BlockSpec((None, t, dh), lambda h, i: (2, 0, h)),
                  pl.BlockSpec((None, nb, 1, b), lambda h, i: (h, 0, 0, 0))],
        out_specs=[pl.BlockSpec((b, dh), lambda h, i: (i, h)), pl.BlockSpec((None, b, 1), lambda h, i: (h, i, 0))],
        out_shape=[jax.ShapeDtypeStruct((t, d), BF16), jax.ShapeDtypeStruct((nh, t, 1), F32)],
        compiler_params=_params(("parallel", "parallel")),
    )(qkv, qkv, qkv, fk)


def fox_bwd_dq(qkv, fk, do, lse, *, name):
    _, t, d = qkv.shape
    nh = d // FOX_HEAD
    b = _blk(FOX_BLK, t)
    nb = t // b
    dh = FOX_HEAD
    scale = dh ** -0.5

    def body(q_ref, k_ref, v_ref, f_ref, do_ref, lse_ref, dq_ref, dl_ref, p_buf, dp_buf):
        qi = pl.program_id(1)
        qb, dob, lse_v = q_ref[...], do_ref[...], lse_ref[...]

        def first(kj, dl, masked):
            off = pl.multiple_of(kj * b, b)
            p = jnp.exp(_fox_scores(qb, k_ref[pl.ds(off, b), :], f_ref[kj], scale, masked) - lse_v)
            dp = lax.dot_general(dob, v_ref[pl.ds(off, b), :], _NT, preferred_element_type=F32)
            p_buf[kj] = p
            dp_buf[kj] = dp
            return dl + jnp.sum(p * dp, axis=-1, keepdims=True)

        dl = lax.fori_loop(0, qi, lambda kj, c: first(kj, c, False), jnp.zeros((b, 1), F32))
        dl = first(qi, dl, True)
        dl_ref[...] = dl

        def second(kj, dq):
            off = pl.multiple_of(kj * b, b)
            dsv = p_buf[kj] * (dp_buf[kj] - dl)
            return dq + jnp.dot(dsv.astype(BF16), k_ref[pl.ds(off, b), :], preferred_element_type=F32)

        dq = lax.fori_loop(0, qi + 1, second, jnp.zeros((b, dh), F32))
        dq_ref[...] = (dq * scale).astype(BF16)

    col = pl.BlockSpec((None, b, 1), lambda h, i: (h, i, 0))
    return pl.pallas_call(
        body, name=name, grid=(nh, nb),
        in_specs=[pl.BlockSpec((None, b, dh), lambda h, i: (0, i, h)),
                  pl.BlockSpec((None, t, dh), lambda h, i: (1, 0, h)),
                  pl.BlockSpec((None, t, dh), lambda h, i: (2, 0, h)),
                  pl.BlockSpec((None, nb, 1, b), lambda h, i: (h, 0, 0, 0)),
                  pl.BlockSpec((b, dh), lambda h, i: (i, h)), col],
        out_specs=[pl.BlockSpec((None, b, dh), lambda h, i: (2, i, h)), col],
        out_shape=[jax.ShapeDtypeStruct((3, t, d), BF16), jax.ShapeDtypeStruct((nh, t, 1), F32)],
        scratch_shapes=[pltpu.VMEM((nb, b, b), F32), pltpu.VMEM((nb, b, b), F32)],
        compiler_params=_params(("parallel", "parallel")),
    )(qkv, qkv, qkv, fk, do, lse)


def fox_bwd_dkv(qkv, fk, do, lse, delta, dqkv, *, name):
    _, t, d = qkv.shape
    nh = d // FOX_HEAD
    b = _blk(FOX_BLK, t)
    nb = t // b
    dh = FOX_HEAD
    scale = dh ** -0.5

    def body(q_ref, k_ref, v_ref, f_ref, do_ref, lse_ref, dl_ref, dz_in, dkv_ref, df_ref):
        del dz_in
        kj = pl.program_id(1)
        kb, vb, fkv = k_ref[...], v_ref[...], f_ref[...]

        def step(qi, carry, masked):
            dk, dv, df = carry
            off = pl.multiple_of(qi * b, b)
            qb, dob = q_ref[pl.ds(off, b), :], do_ref[pl.ds(off, b), :]
            p = jnp.exp(_fox_scores(qb, kb, fkv, scale, masked) - lse_ref[pl.ds(off, b), :])
            dv = dv + lax.dot_general(p.astype(BF16), dob, _TN, preferred_element_type=F32)
            dp = lax.dot_general(dob, vb, _NT, preferred_element_type=F32)
            dsv = p * (dp - dl_ref[pl.ds(off, b), :])
            dk = dk + lax.dot_general(dsv.astype(BF16), qb, _TN, preferred_element_type=F32)
            return dk, dv, df - jnp.sum(dsv, axis=0, keepdims=True)

        init = (jnp.zeros((b, dh), F32), jnp.zeros((b, dh), F32), jnp.zeros((1, b), F32))
        carry = step(kj, init, True)
        dk, dv, df = lax.fori_loop(kj + 1, nb, lambda qi, c: step(qi, c, False), carry)
        dkv_ref[0] = (dk * scale).astype(BF16)
        dkv_ref[1] = dv.astype(BF16)
        df_ref[...] = df

    col = pl.BlockSpec((None, t, 1), lambda h, j: (h, 0, 0))
    return pl.pallas_call(
        body, name=name, grid=(nh, nb),
        in_specs=[pl.BlockSpec((None, t, dh), lambda h, j: (0, 0, h)),
                  pl.BlockSpec((None, b, dh), lambda h, j: (1, j, h)),
                  pl.BlockSpec((None, b, dh), lambda h, j: (2, j, h)),
                  pl.BlockSpec((None, None, 1, b), lambda h, j: (h, j, 0, 0)),
                  pl.BlockSpec((t, dh), lambda h, j: (0, h)), col, col, pl.BlockSpec(memory_space=pl.ANY)],
        out_specs=[pl.BlockSpec((2, b, dh), lambda h, j: (0, j, h)),
                   pl.BlockSpec((None, None, 1, b), lambda h, j: (h, j, 0, 0))],
        out_shape=[jax.ShapeDtypeStruct((3, t, d), BF16), jax.ShapeDtypeStruct((nh, nb, 1, b), F32)],
        input_output_aliases={7: 0},
        compiler_params=_params(("parallel", "parallel")),
    )(qkv, qkv, qkv, fk, do, lse, delta, dqkv)


FL_BLK = 256


def _log_sigmoid(x):
    return jnp.minimum(x, 0.0) - jnp.log(1.0 + jnp.exp(-jnp.abs(x)))


def fl_fwd(zf, bf, *, name):
    t, n = zf.shape
    bt = _blk(FL_BLK, t)

    def body(z_ref, b_ref, o_ref, carry):
        @pl.when(pl.program_id(0) == 0)
        def _():
            carry[...] = jnp.zeros_like(carry)

        ls = _log_sigmoid(z_ref[...] + b_ref[...])
        r = lax.broadcasted_iota(jnp.int32, (bt, bt), 0)
        c = lax.broadcasted_iota(jnp.int32, (bt, bt), 1)
        cs = jnp.dot((c <= r).astype(F32), ls, preferred_element_type=F32, precision=_HI) + carry[...]
        o_ref[...] = cs
        carry[...] = cs[bt - 1:bt, :]

    return pl.pallas_call(
        body, name=name, grid=(t // bt,),
        in_specs=[pl.BlockSpec((bt, n), lambda i: (i, 0)), pl.BlockSpec((1, n), lambda i: (0, 0))],
        out_specs=pl.BlockSpec((bt, n), lambda i: (i, 0)), out_shape=jax.ShapeDtypeStruct((t, n), F32),
        scratch_shapes=[pltpu.VMEM((1, n), F32)], compiler_params=_params(("arbitrary",)),
    )(zf, bf)


def fl_bwd(df, zf, bf, *, name):
    t, n = zf.shape
    bt = _blk(FL_BLK, t)
    nb = t // bt

    def body(df_ref, z_ref, b_ref, dz_ref, db_ref, carry):
        first = pl.program_id(0) == 0

        @pl.when(first)
        def _():
            carry[...] = jnp.zeros_like(carry)

        r = lax.broadcasted_iota(jnp.int32, (bt, bt), 0)
        c = lax.broadcasted_iota(jnp.int32, (bt, bt), 1)
        dls = jnp.dot((c >= r).astype(F32), df_ref[...], preferred_element_type=F32, precision=_HI) + carry[...]
        carry[...] = dls[0:1, :]
        dz = dls * (1.0 - _sigmoid(z_ref[...] + b_ref[...]))
        dz_ref[...] = dz.astype(BF16)
        part = jnp.sum(dz, axis=0, keepdims=True)

        @pl.when(first)
        def _():
            db_ref[...] = part

        @pl.when(jnp.logical_not(first))
        def _():
            db_ref[...] += part

    row = pl.BlockSpec((bt, n), lambda i: (nb - 1 - i, 0))
    vec = pl.BlockSpec((1, n), lambda i: (0, 0))
    return pl.pallas_call(
        body, name=name, grid=(nb,), in_specs=[row, row, vec], out_specs=[row, vec],
        out_shape=[jax.ShapeDtypeStruct((t, n), BF16), jax.ShapeDtypeStruct((1, n), F32)],
        scratch_shapes=[pltpu.VMEM((1, n), F32)], compiler_params=_params(("arbitrary",)),
    )(df, zf, bf)


def _adamw_math(w, g, m, v):
    m = ADAM_B1 * m + (1.0 - ADAM_B1) * g
    v = ADAM_B2 * v + (1.0 - ADAM_B2) * (g * g)
    m_hat = m / (1.0 - ADAM_B1 ** ADAM_STEP)
    v_hat = v / (1.0 - ADAM_B2 ** ADAM_STEP)
    delta = -ADAM_LR * (m_hat / (jnp.sqrt(v_hat) + ADAM_EPS) + ADAM_WD * w)
    return delta, m, v


def adamw(w, g, m, v, *, name):
    r, c = w.shape
    br = _blk(256, r)

    def body(w_ref, g_ref, m_ref, v_ref, go_ref, d_ref, mo_ref, vo_ref):
        gv = g_ref[...]
        go_ref[...] = gv
        d_ref[...], mo_ref[...], vo_ref[...] = _adamw_math(w_ref[...], gv, m_ref[...], v_ref[...])

    spec = pl.BlockSpec((br, c), lambda i: (i, 0))
    return pl.pallas_call(
        body, name=name, grid=(r // br,), in_specs=[spec] * 4, out_specs=[spec] * 4,
        out_shape=[jax.ShapeDtypeStruct((r, c), F32)] * 4, compiler_params=_params(("parallel",)),
    )(w, g, m, v)


W_EV_OUT, W_OD_OUT, W_XQ, W_XO = 0, 1, 2, 4


def _f2(a):
    return a.reshape(a.shape[-2:])


def _local_step(x0, mem, tgt, sp, G):
    t, d = x0.shape
    mix_a = sp["pool_scale"].shape[1]
    w2048 = lambda i: VM(G["w2048"], "rs", inner=(i,))
    down = lambda l: VM(G["down"], "rs", inner=(l,))
    wkv = lambda l: VM(G["wkv"], "cs", inner=(l,))
    gate = lambda l: VM(G["gu"], "cs", inner=(l,))
    up = lambda l: VM(G["gu"], "cs", inner=(2 + l,))
    ev_in = VM(G["ev_in"], "cs")
    wqkv, wf = vm2(G["wqkv"]), vm2(G["wf"])
    grads, small = {}, {}

    def row(a, l):
        return a[l:l + 1]

    def xattn_f(l, xin):
        hx = rmsnorm_fwd(xin, row(sp["xa_norm"], l), name=f"xa_norm_f{l}")
        q = _f2(matmul(vm2(hx), w2048(W_XQ + l), "nn", out_dtype=BF16, name=f"xa_q_f{l}"))
        mn = rmsnorm_fwd(mem, row(sp["xa_mem_norm"], l), name=f"xa_memnorm_f{l}")
        kv = _f2(matmul(vm2(mn), wkv(l), "nn", out_dtype=BF16, name=f"xa_kv_f{l}"))
        o = xattn_fwd(q, kv, name=f"xa_attn_f{l}")
        xout = _f2(matmul(vm2(o), w2048(W_XO + l), "nn", out_dtype=F32, res=vm2(xin), name=f"xa_o_f{l}"))
        return xout, (xin, hx, q, mn, kv, o)

    def ffn_f(l, xin):
        hf = rmsnorm_fwd(xin, row(sp["ffn_norm"], l), name=f"ffn_norm_f{l}")
        a = _f2(matmul(vm2(hf), gate(l), "nn", out_dtype=BF16, name=f"ffn_gate_f{l}"))
        b = _f2(matmul(vm2(hf), up(l), "nn", out_dtype=BF16, name=f"ffn_up_f{l}"))
        s = swiglu_fwd(a, b, name=f"ffn_act_f{l}")
        xout = _f2(matmul(vm2(s), down(l), "nn", out_dtype=F32, res=vm2(xin), name=f"ffn_down_f{l}"))
        return xout, (xin, hf, a, b, s)

    h0 = rmsnorm_fwd(x0, sp["ev_norm"], name="ev_norm_f")
    z = _f2(matmul(vm2(h0), ev_in, "nn", out_dtype=F32, name="ev_in_f"))
    cat = pool_fwd(z, G["pool"], sp["pool_scale"], name="pool_f")
    cat, states = hgrn_fwd(z, cat, sp["lb"], sp["hg_gain"], mix_a, name="hgrn_f")
    x1 = _f2(matmul(VM(cat), w2048(W_EV_OUT), "nn", out_dtype=F32, res=vm2(x0), name="ev_out_f"))
    x2, xa0 = xattn_f(0, x1)
    x3, ff0 = ffn_f(0, x2)

    ho = rmsnorm_fwd(x3, sp["od_norm"], name="od_norm_f")
    qkv = matmul(vm2(ho), wqkv, "nn", out_dtype=BF16, out_p=3, name="od_qkv_f")
    zf = _f2(matmul(vm2(ho), wf, "nn", out_dtype=F32, name="od_fl_f"))
    fcum = fl_fwd(zf, sp["bf"], name="od_forget_f")
    nh = d // FOX_HEAD
    nfb = t // _blk(FOX_BLK, t)
    fk = fcum[:, :nh].T.reshape(nh, nfb, 1, t // nfb)
    of, lse = fox_fwd(qkv, fk, name="fox_f")
    x4 = _f2(matmul(vm2(of), w2048(W_OD_OUT), "nn", out_dtype=F32, res=vm2(x3), name="od_out_f"))
    x5, xa1 = xattn_f(1, x4)
    x6, ff1 = ffn_f(1, x5)
    loss, dx, dxb, small["final_norm"] = loss_head(x6, sp["final_norm"], tgt, name="loss_head")

    def ffn_b(l, saved, dx, dxb):
        xin, hf, a, b, s = saved
        dsv = _f2(matmul(vm2(dxb), down(l), "nt", out_dtype=BF16, name=f"ffn_down_bx{l}"))
        grads[f"down{l}"] = matmul(vm2(s), vm2(dxb), "tn", out_dtype=BF16, out_kind="rs", out_p=N_CHIPS, name=f"ffn_down_bw{l}")
        da, db = swiglu_bwd(dsv, a, b, name=f"ffn_act_b{l}")
        grads[f"gate{l}"] = matmul(vm2(hf), vm2(da), "tn", out_dtype=BF16, out_p=N_CHIPS, name=f"ffn_gate_bw{l}")
        grads[f"up{l}"] = matmul(vm2(hf), vm2(db), "tn", out_dtype=BF16, out_p=N_CHIPS, name=f"ffn_up_bw{l}")
        dh = matmul(vm2(da), gate(l), "nt", out_dtype=F32, name=f"ffn_gate_bx{l}")
        dh = _f2(matmul(vm2(db), up(l), "nt", out_dtype=F32, res=VM(dh), name=f"ffn_up_bx{l}"))
        dx, dxb, dg = rmsnorm_bwd(xin, row(sp["ffn_norm"], l), dh, dx, name=f"ffn_norm_b{l}")
        return dx, dxb, dg

    def xattn_b(l, saved, dx, dxb):
        xin, hx, q, mn, kv, o = saved
        do = _f2(matmul(vm2(dxb), w2048(W_XO + l), "nt", out_dtype=BF16, name=f"xa_o_bx{l}"))
        grads[f"wo{l}"] = matmul(vm2(o), vm2(dxb), "tn", out_dtype=BF16, out_kind="rs", out_p=N_CHIPS, name=f"xa_o_bw{l}")
        dq, dkv = xattn_bwd(q, kv, do, name=f"xa_attn_b{l}")
        grads[f"wq{l}"] = matmul(vm2(hx), vm2(dq), "tn", out_dtype=BF16, out_kind="rs", out_p=N_CHIPS, name=f"xa_q_bw{l}")
        dh = _f2(matmul(vm2(dq), w2048(W_XQ + l), "nt", out_dtype=F32, name=f"xa_q_bx{l}"))
        dkvb = dkv.astype(BF16)
        grads[f"wkv{l}"] = matmul(vm2(mn), vm2(dkvb), "tn", out_dtype=BF16, out_p=N_CHIPS, name=f"xa_kv_bw{l}")
        dmn = _f2(matmul(vm2(dkvb), wkv(l), "nt", out_dtype=F32, name=f"xa_kv_bx{l}"))
        (dgm,) = rmsnorm_bwd(mem, row(sp["xa_mem_norm"], l), dmn, None, name=f"xa_memnorm_b{l}")
        dx, dxb, dg = rmsnorm_bwd(xin, row(sp["xa_norm"], l), dh, dx, name=f"xa_norm_b{l}")
        return dx, dxb, dg, dgm

    dg_ffn, dg_xa, dg_mem = [None, None], [None, None], [None, None]
    dx, dxb, dg_ffn[1] = ffn_b(1, ff1, dx, dxb)
    dx, dxb, dg_xa[1], dg_mem[1] = xattn_b(1, xa1, dx, dxb)

    do = _f2(matmul(vm2(dxb), w2048(W_OD_OUT), "nt", out_dtype=BF16, name="od_out_bx"))
    grads["od_out"] = matmul(vm2(of), vm2(dxb), "tn", out_dtype=BF16, out_kind="rs", out_p=N_CHIPS, name="od_out_bw")
    dz3, delta = fox_bwd_dq(qkv, fk, do, lse, name="fox_bq")
    dz3, dfk = fox_bwd_dkv(qkv, fk, do, lse, delta, dz3, name="fox_bkv")
    dfc = jnp.pad(dfk.reshape(nh, t).T, ((0, 0), (0, zf.shape[1] - nh)))
    dzf, dbf = fl_bwd(dfc, zf, sp["bf"], name="od_forget_b")
    dqkv = VM(dz3, "cs", pfn=lambda p: lax.rem(p + 2, 3))
    dwqkv = _f2(matmul(vm2(ho), dqkv, "tn", out_dtype=BF16, name="od_qkv_bw"))
    dwf = _f2(matmul(vm2(ho), vm2(dzf), "tn", out_dtype=BF16, name="od_fl_bw"))
    grads["od_in_full"] = jnp.concatenate([dwqkv, dwf[:, :nh]], axis=1)
    dh = matmul(dqkv, wqkv, "nt", out_dtype=F32, name="od_qkv_bx")
    dh = _f2(matmul(vm2(dzf), wf, "nt", out_dtype=F32, res=VM(dh), name="od_fl_bx"))
    dx, dxb, small["od_norm"] = rmsnorm_bwd(x3, sp["od_norm"], dh, dx, name="od_norm_b")
    small["bf"] = dbf

    dx, dxb, dg_ffn[0] = ffn_b(0, ff0, dx, dxb)
    dx, dxb, dg_xa[0], dg_mem[0] = xattn_b(0, xa0, dx, dxb)

    dcat = matmul(vm2(dxb), w2048(W_EV_OUT), "nt", out_dtype=BF16, out_p=2, name="ev_out_bx")
    grads["ev_out"] = matmul(VM(cat), vm2(dxb), "tn", out_dtype=BF16, out_kind="rs", out_p=N_CHIPS, name="ev_out_bw")
    dz5, grads["pool_full"], small["pool_scale"] = pool_bwd(z, dcat, G["pool"], sp["pool_scale"], name="pool_b")
    dz5, small["lb"], dgn = hgrn_bwd(z, dcat, dz5, states, sp["lb"], sp["hg_gain"], mix_a, name="hgrn_b")
    small["hg_gain"] = jnp.sum(dgn, axis=0)
    dzv = VM(dz5, "cs", pfn=lambda p: lax.rem(p + 4, 5))
    grads["ev_in"] = matmul(vm2(h0), dzv, "tn", out_dtype=BF16, out_p=N_CHIPS, name="ev_in_bw")
    dh = _f2(matmul(dzv, ev_in, "nt", out_dtype=F32, name="ev_in_bx"))
    dx, _, small["ev_norm"] = rmsnorm_bwd(x0, sp["ev_norm"], dh, dx, name="ev_norm_b")

    small["xa_norm"] = jnp.concatenate(dg_xa, axis=0)
    small["xa_mem_norm"] = jnp.concatenate(dg_mem, axis=0)
    small["ffn_norm"] = jnp.concatenate(dg_ffn, axis=0)
    return loss, dx, small, grads


def _me():
    x, y, c = lax.axis_index("x"), lax.axis_index("y"), lax.axis_index("c")
    chips = [(1 - x, y), (x, 1 - y), (1 - x, 1 - y)]
    return x, y, c, chips


def _chip_id(xy):
    return 2 * xy[0] + xy[1]


def _rcopy(src, dst, ssem, rsem, dev):
    return pltpu.make_async_remote_copy(src_ref=src, dst_ref=dst, send_sem=ssem, recv_sem=rsem, device_id=dev,
                                        device_id_type=MESH)


_ANY = pl.BlockSpec(memory_space=pl.ANY)


def gather_shards(shards, *, name):
    n = len(shards)

    def body(*refs):
        ins, outs = refs[:n], refs[n:2 * n]
        ssem, rsem = refs[2 * n:]
        x, y, c, chips = _me()
        mine = _chip_id((x, y))
        sibling = (x, y, 1 - c)

        def rows(w, chip_id, which):
            h = shards[w].shape[0] // 2
            return outs[w].at[chip_id, pl.ds(which * h, h)]

        def to_chip(w, j):
            h = shards[w].shape[0] // 2
            return _rcopy(ins[w].at[pl.ds(c * h, h)], rows(w, mine, c), ssem.at[w, j], rsem.at[w, j], (*chips[j], c))

        def from_chip(w, j):
            r = rows(w, _chip_id(chips[j]), c)
            return _rcopy(r, r, ssem.at[w, j], rsem.at[w, j], (*chips[j], c))

        def to_sibling(w, j):
            r = rows(w, _chip_id(chips[j]), c)
            return _rcopy(r, r, ssem.at[w, 3 + j], rsem.at[w, 3 + j], sibling)

        def from_sibling(w, j):
            r = rows(w, _chip_id(chips[j]), 1 - c)
            return _rcopy(r, r, ssem.at[w, 3 + j], rsem.at[w, 3 + j], sibling)

        for w in range(n):
            for j in range(3):
                to_chip(w, j).start()
        for w in range(n):
            for j in range(3):
                from_chip(w, j).wait_recv()
                to_sibling(w, j).start()
        for w in range(n):
            for j in range(3):
                from_sibling(w, j).wait_recv()
        for w in range(n):
            for j in range(3):
                to_chip(w, j).wait_send()
                to_sibling(w, j).wait_send()

    return pl.pallas_call(
        body, name=name, in_specs=[_ANY] * n, out_specs=[_ANY] * n,
        out_shape=[jax.ShapeDtypeStruct((N_CHIPS,) + s.shape, s.dtype) for s in shards],
        scratch_shapes=[pltpu.SemaphoreType.DMA((n, 6)), pltpu.SemaphoreType.DMA((n, 6))],
    )(*shards)


def _ids_spec(grid, in_specs, out_specs):
    return pltpu.PrefetchScalarGridSpec(num_scalar_prefetch=1, grid=grid, in_specs=in_specs, out_specs=out_specs)


def fill_own(full, shard, ids, *, name):
    r, c = shard.shape
    br = _row_blk(r, 512)

    def body(ids_ref, s_ref, f_in, o_ref):
        del ids_ref, f_in
        o_ref[...] = s_ref[...]

    return pl.pallas_call(
        body, name=name, out_shape=jax.ShapeDtypeStruct(full.shape, full.dtype), input_output_aliases={2: 0},
        grid_spec=_ids_spec((r // br,), [pl.BlockSpec((br, c), lambda i, ids: (i, 0)), _ANY],
                            pl.BlockSpec((None, br, c), lambda i, ids: (ids[0], i, 0))),
        compiler_params=_params(("parallel",)),
    )(ids, shard, full)


def rs_pair(parts, *, name):
    n = len(parts)

    def body(*refs):
        ins, recv = refs[:n], refs[n:2 * n]
        ssem, rsem = refs[2 * n:]
        x, y, c, _ = _me()
        sibling = (x, y, 1 - c)

        def swap(w):
            h = parts[w].shape[1] // 2
            return _rcopy(ins[w].at[:, pl.ds((1 - c) * h, h), :], recv[w], ssem.at[w], rsem.at[w], sibling)

        for w in range(n):
            swap(w).start()
        for w in range(n):
            swap(w).wait()

    return pl.pallas_call(
        body, name=name, in_specs=[_ANY] * n, out_specs=[_ANY] * n,
        out_shape=[jax.ShapeDtypeStruct((p.shape[0], p.shape[1] // 2, p.shape[2]), p.dtype) for p in parts],
        scratch_shapes=[pltpu.SemaphoreType.DMA((n,)), pltpu.SemaphoreType.DMA((n,))],
    )(*parts)


def add_pair(part, recv, ids, *, name):
    p, h, c = recv.shape
    br = _row_blk(h, 512)
    nb = h // br

    def body(ids_ref, a_ref, b_ref, o_ref):
        del ids_ref
        o_ref[...] = (a_ref[...].astype(F32) + b_ref[...].astype(F32)).astype(o_ref.dtype)

    half = pl.BlockSpec((None, br, c), lambda k, i, ids: (k, i, 0))
    return pl.pallas_call(
        body, name=name, out_shape=jax.ShapeDtypeStruct(recv.shape, recv.dtype),
        grid_spec=_ids_spec((p, nb), [pl.BlockSpec((None, br, c), lambda k, i, ids: (k, ids[1] * nb + i, 0)), half], half),
        compiler_params=_params(("parallel", "parallel")),
    )(ids, part, recv)


def rs_chip(sums, *, name):
    n = len(sums)

    def body(*refs):
        ins, outs = refs[:n], refs[n:2 * n]
        ssem, rsem = refs[2 * n:]
        x, y, c, chips = _me()

        def swap(w, j):
            return _rcopy(ins[w].at[_chip_id(chips[j])], outs[w].at[j], ssem.at[w, j], rsem.at[w, j], (*chips[j], c))

        for w in range(n):
            for j in range(3):
                swap(w, j).start()
        for w in range(n):
            for j in range(3):
                swap(w, j).wait()

    return pl.pallas_call(
        body, name=name, in_specs=[_ANY] * n, out_specs=[_ANY] * n,
        out_shape=[jax.ShapeDtypeStruct((3,) + s.shape[1:], s.dtype) for s in sums],
        scratch_shapes=[pltpu.SemaphoreType.DMA((n, 3)), pltpu.SemaphoreType.DMA((n, 3))],
    )(*sums)


def add_chips(sums, landed, ids, group, layer, group_shape, *, name):
    _, h, c = sums.shape
    br = _row_blk(h, 256)
    nb = h // br

    def body(ids_ref, a_ref, b_ref, *rest):
        o_ref = rest[-1]
        tot = a_ref[...].astype(F32)
        for k in range(3):
            tot = tot + b_ref[k].astype(F32)
        o_ref[...] = tot

    in_specs = [pl.BlockSpec((None, br, c), lambda i, ids: (ids[0], i, 0)), pl.BlockSpec((3, br, c), lambda i, ids: (0, i, 0))]
    args = [ids, sums, landed]
    if group is not None:
        in_specs.append(_ANY)
        args.append(group)
    return pl.pallas_call(
        body, name=name, out_shape=jax.ShapeDtypeStruct(group_shape, F32),
        input_output_aliases={3: 0} if group is not None else {},
        grid_spec=_ids_spec((nb,), in_specs, pl.BlockSpec((None, br, c), lambda i, ids: (layer, ids[1] * nb + i, 0))),
        compiler_params=_params(("parallel",)),
    )(*args)


def rs_share(groups, slots, *, name):
    ng = len(groups)
    n = len(slots)

    def body(*refs):
        outs = refs[ng:2 * ng]
        ssem, rsem = refs[2 * ng:]
        x, y, c, _ = _me()
        sibling = (x, y, 1 - c)

        def rows(w, which):
            g, l = slots[w]
            h = groups[g].shape[1] // 2
            return outs[g].at[l, pl.ds(which * h, h), :]

        def swap(w):
            return _rcopy(rows(w, c), rows(w, c), ssem.at[w], rsem.at[w], sibling)

        for w in range(n):
            swap(w).start()
        for w in range(n):
            swap(w).wait_send()
            _rcopy(rows(w, 1 - c), rows(w, 1 - c), ssem.at[w], rsem.at[w], sibling).wait_recv()

    return pl.pallas_call(
        body, name=name, in_specs=[_ANY] * ng, out_specs=[_ANY] * ng,
        out_shape=[jax.ShapeDtypeStruct(g.shape, g.dtype) for g in groups],
        input_output_aliases={g: g for g in range(ng)},
        scratch_shapes=[pltpu.SemaphoreType.DMA((n,)), pltpu.SemaphoreType.DMA((n,))],
    )(*groups)


def allreduce_small(v, *, name):
    r, c = v.shape
    ndev = 2 * N_CHIPS

    def body(v_ref, o_ref, buf, ssem, rsem):
        x, y, cc, _ = _me()
        me = 4 * x + 2 * y + cc
        flips = [(a, b, d) for a in (0, 1) for b in (0, 1) for d in (0, 1)][1:]
        buf[me] = v_ref[...]
        cps = []
        for k, (a, b, d) in enumerate(flips):
            peer = (jnp.bitwise_xor(x, a), jnp.bitwise_xor(y, b), jnp.bitwise_xor(cc, d))
            cp = _rcopy(v_ref, buf.at[me], ssem.at[k], rsem.at[k], peer)
            cp.start()
            cps.append(cp)
        for k, (a, b, d) in enumerate(flips):
            peer = (jnp.bitwise_xor(x, a), jnp.bitwise_xor(y, b), jnp.bitwise_xor(cc, d))
            src = 4 * peer[0] + 2 * peer[1] + peer[2]
            _rcopy(v_ref, buf.at[src], ssem.at[k], rsem.at[k], peer).wait_recv()
        for cp in cps:
            cp.wait_send()
        tot = buf[0]
        for k in range(1, ndev):
            tot = tot + buf[k]
        o_ref[...] = tot

    vm = pl.BlockSpec(memory_space=pltpu.VMEM)
    return pl.pallas_call(
        body, name=name, in_specs=[vm], out_specs=vm, out_shape=jax.ShapeDtypeStruct((r, c), F32),
        scratch_shapes=[pltpu.VMEM((ndev, r, c), F32), pltpu.SemaphoreType.DMA((ndev - 1,)), pltpu.SemaphoreType.DMA((ndev - 1,))],
    )(v)


WEIGHTS = ["lb_table", "ev_norm", "ev_w_in", "ev_w_pool", "ev_pool_scale", "ev_hg_norm", "ev_w_out", "od_norm", "od_w_in",
           "od_b_f", "od_w_out", "xa_norm", "xa_mem_norm", "xa_wq", "xa_wkv", "xa_wo", "ffn_norm", "ffn_w_gate", "ffn_w_up",
           "ffn_w_down", "final_norm"]
BIG = ["ev_w_in", "ev_w_pool", "ev_w_out", "od_w_in", "od_w_out", "xa_wq", "xa_wkv", "xa_wo", "ffn_w_gate", "ffn_w_up", "ffn_w_down"]
SMALL_ROWS = 16


def _rows(parts, width):
    rows = [jnp.pad(p.reshape(-1, p.shape[-1]).astype(F32), ((0, 0), (0, width - p.shape[-1]))) for p in parts]
    out = jnp.concatenate(rows, axis=0)
    return jnp.pad(out, ((0, SMALL_ROWS - out.shape[0]), (0, 0)))


def _unrows(packed, like):
    out, r = [], 0
    for p in like:
        n = p.size // p.shape[-1]
        out.append(packed[r:r + n, :p.shape[-1]].reshape(p.shape))
        r += n
    return out


def _m3(a):
    return a.reshape(a.shape[0], -1, a.shape[-1])


def kernel(x, mem, lb_table, ev_norm, ev_w_in, ev_w_pool, ev_pool_scale, ev_hg_norm, ev_w_out, od_norm, od_w_in, od_b_f, od_w_out, xa_norm, xa_mem_norm, xa_wq, xa_wkv, xa_wo, ffn_norm, ffn_w_gate, ffn_w_up, ffn_w_down, final_norm, loss_target, m_lb_table, m_ev_norm, m_ev_w_in, m_ev_w_pool, m_ev_pool_scale, m_ev_hg_norm, m_ev_w_out, m_od_norm, m_od_w_in, m_od_b_f, m_od_w_out, m_xa_norm, m_xa_mem_norm, m_xa_wq, m_xa_wkv, m_xa_wo, m_ffn_norm, m_ffn_w_gate, m_ffn_w_up, m_ffn_w_down, m_final_norm, v_lb_table, v_ev_norm, v_ev_w_in, v_ev_w_pool, v_ev_pool_scale, v_ev_hg_norm, v_ev_w_out, v_od_norm, v_od_w_in, v_od_b_f, v_od_w_out, v_xa_norm, v_xa_mem_norm, v_xa_wq, v_xa_wkv, v_xa_wo, v_ffn_norm, v_ffn_w_gate, v_ffn_w_up, v_ffn_w_down, v_final_norm):
    a = dict(locals())
    w = {n: a[n] for n in WEIGHTS}
    mom = {n: a["m_" + n] for n in WEIGHTS}
    var = {n: a["v_" + n] for n in WEIGHTS}
    _, t, d = x.shape
    nh = d // FOX_HEAD
    lanes = 128
    cx, cy = lax.axis_index("x"), lax.axis_index("y")
    chip = 2 * cx + cy

    bf = lambda v: v.astype(BF16)
    w3 = {n: _m3(w[n]) for n in BIG}
    flat = lambda v: v.reshape(-1, v.shape[-1])
    shards = [
        bf(jnp.concatenate([flat(w3["ev_w_out"]), flat(w3["od_w_out"]), flat(w3["xa_wq"]), flat(w3["xa_wo"])], axis=0)),
        bf(flat(w3["ffn_w_down"])), bf(flat(w3["ev_w_in"])), bf(flat(w3["od_w_in"])), bf(flat(w3["xa_wkv"])),
        bf(jnp.concatenate([flat(w3["ffn_w_gate"]), flat(w3["ffn_w_up"])], axis=0)),
        bf(flat(w3["ev_w_pool"])), jnp.broadcast_to(od_norm, (16, od_norm.shape[1])),
    ]
    ids = jnp.stack([chip, lax.axis_index("c")]).astype(jnp.int32)
    gathered = gather_shards(shards, name="gather_weights")
    g2048, gdown, gev_in, god_in, gwkv, ggu, gpool, godn = [
        fill_own(f, s, ids, name=f"gather_own_{k}") for k, (f, s) in enumerate(zip(gathered, shards))]
    rs_rows = d // N_CHIPS
    fs = ffn_w_down.shape[1]
    od_full = god_in.transpose(1, 0, 2).reshape(d, -1)
    ng, gsz = ev_w_pool.shape[1], ev_w_pool.shape[3]
    G = {
        "w2048": g2048.reshape(N_CHIPS, 6, rs_rows, d), "down": gdown.reshape(N_CHIPS, 2, fs, d), "ev_in": gev_in,
        "wkv": gwkv.reshape(N_CHIPS, 2, d, -1), "gu": ggu.reshape(N_CHIPS, 4, d, -1),
        "wqkv": od_full[:, :3 * d], "wf": jnp.pad(od_full[:, 3 * d:], ((0, 0), (0, lanes - nh))),
        "pool": gpool.reshape(N_CHIPS, ng, gsz // N_CHIPS, gsz).transpose(1, 0, 2, 3).reshape(ng, gsz, gsz),
    }
    od_norm_full = godn[:, 0, :].reshape(1, d)

    sm = jax.nn.softmax(lb_table, axis=0)
    sp = {
        "lb": sm[1:2], "ev_norm": ev_norm, "pool_scale": ev_pool_scale, "hg_gain": ev_hg_norm, "od_norm": od_norm_full,
        "bf": jnp.pad(od_b_f, ((0, 0), (0, lanes - nh))), "xa_norm": xa_norm, "xa_mem_norm": xa_mem_norm, "ffn_norm": ffn_norm,
        "final_norm": final_norm.reshape(1, d),
    }
    loss_l, gx, small, grads = _local_step(x[0], mem[0], loss_target[0], sp, G)
    loss = lax.psum(loss_l[0, 0], ("x", "y", "c"))

    od_in_parts = grads["od_in_full"].reshape(d, N_CHIPS, -1).transpose(1, 0, 2)
    pool_parts = bf(grads["pool_full"].reshape(ng, N_CHIPS, gsz // N_CHIPS, gsz).transpose(1, 0, 2, 3).reshape(N_CHIPS, gsz, gsz))
    ops = [("ev_w_in", 0, grads["ev_in"]), ("ev_w_pool", 0, pool_parts), ("ev_w_out", 0, grads["ev_out"]),
           ("od_w_in", 0, od_in_parts), ("od_w_out", 0, grads["od_out"])]
    for l in range(2):
        ops += [("xa_wq", l, grads[f"wq{l}"]), ("xa_wkv", l, grads[f"wkv{l}"]), ("xa_wo", l, grads[f"wo{l}"]),
                ("ffn_w_gate", l, grads[f"gate{l}"]), ("ffn_w_up", l, grads[f"up{l}"]), ("ffn_w_down", l, grads[f"down{l}"])]
    got = rs_pair([o[2] for o in ops], name="reduce_pair")
    chip_sums = [add_pair(o[2], g, ids, name=f"reduce_add2_{k}") for k, (o, g) in enumerate(zip(ops, got))]
    landed = rs_chip(chip_sums, name="reduce_chips")
    gbig = {n: None for n in BIG}
    for k, (n, l, _) in enumerate(ops):
        gbig[n] = add_chips(chip_sums[k], landed[k], ids, gbig[n], l, w3[n].shape, name=f"reduce_add4_{k}")
    full = rs_share([gbig[n] for n in BIG], [(BIG.index(o[0]), o[1]) for o in ops], name="reduce_share")
    gbig = dict(zip(BIG, full))

    raw_like = [small["lb"], small["ev_norm"], small["pool_scale"], small["hg_gain"], small["od_norm"], small["bf"],
                small["xa_norm"], small["xa_mem_norm"], small["ffn_norm"], small["final_norm"]]
    summed = _unrows(allreduce_small(_rows(raw_like, d), name="reduce_small"), raw_like)
    dlb, g_ev_norm, g_pool_scale, g_hg, g_od_norm_full, g_bf, g_xa, g_xam, g_ffn, g_final = summed
    dsm = jnp.zeros_like(sm).at[1:2].set(dlb)
    gsmall = {
        "lb_table": sm * (dsm - jnp.sum(sm * dsm, axis=0, keepdims=True)), "ev_norm": g_ev_norm, "ev_pool_scale": g_pool_scale,
        "ev_hg_norm": g_hg, "od_norm": lax.dynamic_slice_in_dim(g_od_norm_full, chip * od_norm.shape[1], od_norm.shape[1], axis=1),
        "od_b_f": g_bf[:, :nh], "xa_norm": g_xa, "xa_mem_norm": g_xam, "ffn_norm": g_ffn, "final_norm": g_final.reshape(d),
    }

    grad, delta, new_m, new_v = {}, {}, {}, {}
    for n in BIG:
        shp = w[n].shape
        res = adamw(flat(w3[n]), flat(gbig[n]), flat(_m3(mom[n])), flat(_m3(var[n])), name=f"adamw_{n}")
        grad[n], delta[n], new_m[n], new_v[n] = [r.reshape(shp) for r in res]
    snames = [n for n in WEIGHTS if n not in BIG]
    like = [w[n] for n in snames]
    res = adamw(_rows(like, d), _rows([gsmall[n] for n in snames], d), _rows([mom[n] for n in snames], d),
                _rows([var[n] for n in snames], d), name="adamw_small")
    for vals, dst in zip(res, (grad, delta, new_m, new_v)):
        dst.update(zip(snames, _unrows(vals, like)))
    return (loss, gx.reshape(x.shape), *[grad[n] for n in WEIGHTS], *[delta[n] for n in WEIGHTS],
            *[new_m[n] for n in WEIGHTS], *[new_v[n] for n in WEIGHTS])
```

```python
import functools
import math

import jax
import jax.numpy as jnp
from jax import lax
from jax.experimental import pallas as pl
from jax.experimental.pallas import tpu as pltpu

F32 = jnp.float32
BF16 = jnp.bfloat16
MESH = pl.DeviceIdType.MESH

V7X_VMEM_LIMIT_BYTES = 56 * 1024 * 1024
N_CHIPS = 4

EPS = 1e-6
POOL_WINDOWS = (2, 4, 8, 16)
POOL_HALO = 16
HG_HEAD = 128
HG_CHUNK = 64
FOX_HEAD = 128
FOX_BLK = 512
XA_HEADS = 4
ADAM_LR, ADAM_B1, ADAM_B2, ADAM_EPS, ADAM_WD, ADAM_STEP = 0.001, 0.9, 0.999, 1e-08, 0.01, 10
EXP_CLAMP = 80.0


def _params(sem=None):
    return pltpu.CompilerParams(dimension_semantics=sem, vmem_limit_bytes=V7X_VMEM_LIMIT_BYTES)


def _blk(pref, dim):
    b = min(pref, dim)
    assert dim % b == 0, (pref, dim)
    return b


class VM:
    def __init__(self, arr, kind="cs", lead=(), inner=(), pfn=None):
        self.arr, self.kind, self.lead, self.inner = arr, kind, tuple(lead), tuple(inner)
        self.pfn = pfn or (lambda p: p)
        p = arr.shape[len(self.lead)]
        r, c = arr.shape[-2:]
        assert arr.ndim == len(self.lead) + 1 + len(self.inner) + 2, (arr.shape, lead, inner)
        self.P = p
        self.shape = (r, c * p) if kind == "cs" else (r * p, c)
        self.dtype = arr.dtype

    def spec(self, br, bc, rfn, cfn):
        p = self.P
        r, c = self.arr.shape[-2:]
        assert c % bc == 0 and r % br == 0, (self.arr.shape, br, bc)
        if p == 1:
            def imap(*g):
                return (*self.lead, self.pfn(0), *self.inner, rfn(*g), cfn(*g))
        elif self.kind == "cs":
            per = c // bc

            def imap(*g):
                cb = cfn(*g)
                return (*self.lead, self.pfn(lax.div(cb, per)), *self.inner, rfn(*g), lax.rem(cb, per))
        else:
            per = r // br

            def imap(*g):
                rb = rfn(*g)
                return (*self.lead, self.pfn(lax.div(rb, per)), *self.inner, lax.rem(rb, per), cfn(*g))
        return pl.BlockSpec((None,) * (self.arr.ndim - 2) + (br, bc), imap)


def vm2(arr):
    return VM(arr.reshape((1,) + arr.shape))


def _out_struct(shape, kind, p, dtype):
    r, c = shape
    return jax.ShapeDtypeStruct((p, r, c // p) if kind == "cs" else (p, r // p, c), dtype)


_ANY = pl.BlockSpec(memory_space=pl.ANY)


def _me():
    x, y, c = lax.axis_index("x"), lax.axis_index("y"), lax.axis_index("c")
    chips = [(1 - x, y), (x, 1 - y), (1 - x, 1 - y)]
    return x, y, c, chips


def _chip_id(xy):
    return 2 * xy[0] + xy[1]


def _rcopy(src, dst, ssem, rsem, dev):
    return pltpu.make_async_remote_copy(src_ref=src, dst_ref=dst, send_sem=ssem, recv_sem=rsem, device_id=dev,
                                        device_id_type=MESH)


class GatherSide:
    def __init__(self, shards):
        self.inputs = list(shards)
        self.out_shape = [jax.ShapeDtypeStruct((N_CHIPS,) + s.shape, s.dtype) for s in shards]
        self.sem_shape = (len(shards), 3)
        self.outs = None

    def _copy(self, ins, outs, ssem, rsem, w, j, receive):
        x, y, c, chips = _me()
        h = self.inputs[w].shape[0] // 2
        half = pl.ds(c * h, h)
        if receive:
            r = outs[w].at[_chip_id(chips[j]), half]
            return _rcopy(r, r, ssem.at[w, j], rsem.at[w, j], (*chips[j], c))
        return _rcopy(ins[w].at[half], outs[w].at[_chip_id((x, y)), half], ssem.at[w, j], rsem.at[w, j], (*chips[j], c))

    def start(self, ins, outs, ssem, rsem):
        for w in range(len(self.inputs)):
            for j in range(3):
                self._copy(ins, outs, ssem, rsem, w, j, False).start()

    def finish(self, ins, outs, ssem, rsem):
        for w in range(len(self.inputs)):
            for j in range(3):
                self._copy(ins, outs, ssem, rsem, w, j, True).wait_recv()
                self._copy(ins, outs, ssem, rsem, w, j, False).wait_send()


class ReduceSide:
    def __init__(self, sums):
        self.inputs = list(sums)
        self.out_shape = [jax.ShapeDtypeStruct((3,) + s.shape[1:], s.dtype) for s in sums]
        self.sem_shape = (len(sums), 3)
        self.outs = None

    def _copy(self, ins, outs, ssem, rsem, w, j):
        _, _, c, chips = _me()
        return _rcopy(ins[w].at[_chip_id(chips[j])], outs[w].at[j], ssem.at[w, j], rsem.at[w, j], (*chips[j], c))

    def start(self, ins, outs, ssem, rsem):
        for w in range(len(self.inputs)):
            for j in range(3):
                self._copy(ins, outs, ssem, rsem, w, j).start()

    def finish(self, ins, outs, ssem, rsem):
        for w in range(len(self.inputs)):
            for j in range(3):
                self._copy(ins, outs, ssem, rsem, w, j).wait()


def _call(body, side, *, name, grid, in_specs, out_specs, out_shape, scratch_shapes=(), sem, aliases=None, args):
    if side is None:
        return pl.pallas_call(body, name=name, grid=grid, in_specs=in_specs, out_specs=out_specs, out_shape=out_shape,
                              scratch_shapes=list(scratch_shapes), input_output_aliases=aliases or {},
                              compiler_params=_params(sem))(*args)
    single = not isinstance(out_shape, (list, tuple))
    oshape, ospecs = ([out_shape], [out_specs]) if single else (list(out_shape), list(out_specs))
    n_in, n_out, s_in, s_out = len(in_specs), len(oshape), len(side.inputs), len(side.out_shape)

    def wrapped(*refs):
        ins, sin = refs[:n_in], refs[n_in:n_in + s_in]
        outs = refs[n_in + s_in:n_in + s_in + n_out]
        souts = refs[n_in + s_in + n_out:n_in + s_in + n_out + s_out]
        rest = refs[n_in + s_in + n_out + s_out:]
        scratch, (ssem, rsem) = rest[:-2], rest[-2:]
        first = functools.reduce(jnp.logical_and, [pl.program_id(a) == 0 for a in range(len(grid))])
        last = functools.reduce(jnp.logical_and, [pl.program_id(a) == grid[a] - 1 for a in range(len(grid))])

        @pl.when(first)
        def _():
            side.start(sin, souts, ssem, rsem)

        body(*ins, *outs, *scratch)

        @pl.when(last)
        def _():
            side.finish(sin, souts, ssem, rsem)

    res = pl.pallas_call(
        wrapped, name=name, grid=grid, in_specs=list(in_specs) + [_ANY] * s_in, out_specs=ospecs + [_ANY] * s_out,
        out_shape=oshape + side.out_shape,
        scratch_shapes=list(scratch_shapes) + [pltpu.SemaphoreType.DMA(side.sem_shape), pltpu.SemaphoreType.DMA(side.sem_shape)],
        input_output_aliases=aliases or {}, compiler_params=_params(("arbitrary",) * len(grid)),
    )(*args, *side.inputs)
    side.outs = list(res[n_out:])
    return res[0] if single else list(res[:n_out])


def _best(g, cap):
    if g <= cap:
        return g
    cands = [d for d in range(128, cap + 1, 128) if g % d == 0]
    assert cands, (g, cap)
    return cands[-1]


def _row_blk(n, cap):
    cands = [d for d in range(16, min(n, cap) + 1, 16) if n % d == 0]
    assert cands, (n, cap)
    return cands[-1]


def _tiles(a, b, mode, out_kind, out_p, bm, bn, bk):
    def cpiece(v):
        return v.arr.shape[-1] if v.kind == "cs" else v.shape[1]

    def rpiece(v):
        return v.arr.shape[-2] if v.kind == "rs" else v.shape[0]

    if mode == "nn":
        m, n = a.shape[0], b.shape[1]
        gm, gn, gk = rpiece(a), cpiece(b), math.gcd(cpiece(a), rpiece(b))
    elif mode == "nt":
        m, n = a.shape[0], b.shape[0]
        gm, gn, gk = rpiece(a), rpiece(b), math.gcd(cpiece(a), cpiece(b))
    else:
        m, n = a.shape[1], b.shape[1]
        gm, gn, gk = cpiece(a), cpiece(b), math.gcd(rpiece(a), rpiece(b))
    if out_kind == "cs":
        gn = math.gcd(gn, n // out_p)
    else:
        gm = math.gcd(gm, m // out_p)
    caps = {"nn": (512, 1536, 2048), "nt": (512, 2048, 2048), "tn": (1536, 1536, 1024)}[mode]
    return (bm or _best(gm, caps[0])), (bn or _best(gn, caps[1])), (bk or _best(gk, caps[2]))


def matmul(a, b, mode, *, out_dtype, bm=None, bn=None, bk=None, out_kind="cs", out_p=1, out_pfn=None, res=None, side=None, name):
    bm, bn, bk = _tiles(a, b, mode, out_kind, out_p, bm, bn, bk)
    if mode == "nn":
        (m, k), (k2, n) = a.shape, b.shape
        a_spec = a.spec(bm, bk, lambda i, j, kk: i, lambda i, j, kk: kk)
        b_spec = b.spec(bk, bn, lambda i, j, kk: kk, lambda i, j, kk: j)
        dims = (((1,), (0,)), ((), ()))
    elif mode == "nt":
        (m, k), (n, k2) = a.shape, b.shape
        a_spec = a.spec(bm, bk, lambda i, j, kk: i, lambda i, j, kk: kk)
        b_spec = b.spec(bn, bk, lambda i, j, kk: j, lambda i, j, kk: kk)
        dims = (((1,), (1,)), ((), ()))
    else:
        (k, m), (k2, n) = a.shape, b.shape
        a_spec = a.spec(bk, bm, lambda i, j, kk: kk, lambda i, j, kk: i)
        b_spec = b.spec(bk, bn, lambda i, j, kk: kk, lambda i, j, kk: j)
        dims = (((0,), (0,)), ((), ()))
    assert k == k2, (a.shape, b.shape, mode)
    assert m % bm == 0 and n % bn == 0 and k % bk == 0, (m, n, k, bm, bn, bk)
    nk = k // bk
    out_sds = _out_struct((m, n), out_kind, out_p, out_dtype)
    out_vm = VM(out_sds, out_kind, pfn=out_pfn)
    o_spec = out_vm.spec(bm, bn, lambda i, j, kk: i, lambda i, j, kk: j)
    in_specs, args = [a_spec, b_spec], [a.arr, b.arr]
    if res is not None:
        assert res.shape == (m, n)
        in_specs.append(res.spec(bm, bn, lambda i, j, kk: i, lambda i, j, kk: j))
        args.append(res.arr)

    def body(a_ref, b_ref, *rest):
        if res is not None:
            r_ref, o_ref = rest[0], rest[1]
        else:
            r_ref, o_ref = None, rest[0]
        part = lax.dot_general(a_ref[...], b_ref[...], dims, preferred_element_type=F32)
        if nk == 1:
            if r_ref is not None:
                part = part + r_ref[...].astype(F32)
            o_ref[...] = part.astype(o_ref.dtype)
            return
        acc = rest[-1]
        kk = pl.program_id(2)

        @pl.when(kk == 0)
        def _():
            acc[...] = part

        @pl.when(kk > 0)
        def _():
            acc[...] += part

        @pl.when(kk == nk - 1)
        def _():
            tot = acc[...]
            if r_ref is not None:
                tot = tot + r_ref[...].astype(F32)
            o_ref[...] = tot.astype(o_ref.dtype)

    return _call(body, side, name=name, grid=(m // bm, n // bn, nk), in_specs=in_specs, out_specs=o_spec, out_shape=out_sds,
                 scratch_shapes=[pltpu.VMEM((bm, bn), F32)] if nk > 1 else [],
                 sem=("parallel", "parallel", "arbitrary"), args=args)


def rmsnorm_fwd(x, g, *, name):
    t, d = x.shape
    bt = _blk(512, t)

    def body(x_ref, g_ref, o_ref):
        xv = x_ref[...]
        r = lax.rsqrt(jnp.mean(xv * xv, axis=-1, keepdims=True) + EPS)
        o_ref[...] = (xv * r * g_ref[...]).astype(o_ref.dtype)

    return pl.pallas_call(
        body, name=name, grid=(t // bt,),
        in_specs=[pl.BlockSpec((bt, d), lambda i: (i, 0)), pl.BlockSpec((1, d), lambda i: (0, 0))],
        out_specs=pl.BlockSpec((bt, d), lambda i: (i, 0)), out_shape=jax.ShapeDtypeStruct((t, d), BF16),
        compiler_params=_params(("parallel",)),
    )(x, g)


def rmsnorm_bwd(x, g, dh, dres, *, name):
    t, d = x.shape
    bt = _blk(256, t)
    want_dx = dres is not None

    def body(x_ref, g_ref, dh_ref, *rest):
        if want_dx:
            dres_ref, dx_ref, dxb_ref, dg_ref = rest
        else:
            (dg_ref,) = rest
        xv = x_ref[...]
        dhv = dh_ref[...].astype(F32)
        r = lax.rsqrt(jnp.mean(xv * xv, axis=-1, keepdims=True) + EPS)
        xh = xv * r
        part = jnp.sum(dhv * xh, axis=0, keepdims=True)

        @pl.when(pl.program_id(0) == 0)
        def _():
            dg_ref[...] = part

        @pl.when(pl.program_id(0) > 0)
        def _():
            dg_ref[...] += part

        if want_dx:
            dy = dhv * g_ref[...]
            dxn = r * (dy - xh * jnp.mean(dy * xh, axis=-1, keepdims=True))
            dx = dres_ref[...] + dxn
            dx_ref[...] = dx
            dxb_ref[...] = dx.astype(BF16)

    row = pl.BlockSpec((bt, d), lambda i: (i, 0))
    vec = pl.BlockSpec((1, d), lambda i: (0, 0))
    in_specs, args = [row, vec, row], [x, g, dh]
    out_specs, out_shape = [vec], [jax.ShapeDtypeStruct((1, d), F32)]
    if want_dx:
        in_specs.append(row)
        args.append(dres)
        out_specs = [row, row] + out_specs
        out_shape = [jax.ShapeDtypeStruct((t, d), F32), jax.ShapeDtypeStruct((t, d), BF16)] + out_shape
    return pl.pallas_call(
        body, name=name, grid=(t // bt,), in_specs=in_specs, out_specs=out_specs, out_shape=out_shape,
        compiler_params=_params(("arbitrary",)),
    )(*args)


def loss_head(x, g, tgt, *, name):
    t, d = x.shape
    bt = _blk(256, t)

    def body(x_ref, g_ref, t_ref, loss_ref, dx_ref, dxb_ref, dg_ref):
        xv = x_ref[...]
        gv = g_ref[...]
        r = lax.rsqrt(jnp.mean(xv * xv, axis=-1, keepdims=True) + EPS)
        xh = xv * r
        e = xh * gv - t_ref[...]
        lpart = jnp.zeros((1, 128), F32) + jnp.sum(e * e) * (0.5 / d)
        dyv = e * (1.0 / d)
        gpart = jnp.sum(dyv * xh, axis=0, keepdims=True)

        @pl.when(pl.program_id(0) == 0)
        def _():
            loss_ref[...] = lpart
            dg_ref[...] = gpart

        @pl.when(pl.program_id(0) > 0)
        def _():
            loss_ref[...] += lpart
            dg_ref[...] += gpart

        dy = dyv * gv
        dx = r * (dy - xh * jnp.mean(dy * xh, axis=-1, keepdims=True))
        dx_ref[...] = dx
        dxb_ref[...] = dx.astype(BF16)

    row = pl.BlockSpec((bt, d), lambda i: (i, 0))
    vec = pl.BlockSpec((1, d), lambda i: (0, 0))
    return pl.pallas_call(
        body, name=name, grid=(t // bt,), in_specs=[row, vec, row],
        out_specs=[pl.BlockSpec((1, 128), lambda i: (0, 0)), row, row, vec],
        out_shape=[jax.ShapeDtypeStruct((1, 128), F32), jax.ShapeDtypeStruct((t, d), F32),
                   jax.ShapeDtypeStruct((t, d), BF16), jax.ShapeDtypeStruct((1, d), F32)],
        compiler_params=_params(("arbitrary",)),
    )(x, g, tgt)


def _sigmoid(x):
    return 1.0 / (1.0 + jnp.exp(-x))


def swiglu_fwd(a, b, *, name):
    t, f = a.shape
    bt, bf = _blk(512, t), _best(f, 1536)

    def body(a_ref, b_ref, o_ref):
        av = a_ref[...].astype(F32)
        o_ref[...] = (av * _sigmoid(av) * b_ref[...].astype(F32)).astype(o_ref.dtype)

    spec = pl.BlockSpec((bt, bf), lambda i, j: (i, j))
    return pl.pallas_call(
        body, name=name, grid=(t // bt, f // bf), in_specs=[spec, spec], out_specs=spec,
        out_shape=jax.ShapeDtypeStruct((t, f), BF16), compiler_params=_params(("parallel", "parallel")),
    )(a, b)


def swiglu_bwd(ds, a, b, *, name):
    t, f = a.shape
    bt, bf = _blk(512, t), _best(f, 1536)

    def body(ds_ref, a_ref, b_ref, da_ref, db_ref):
        av = a_ref[...].astype(F32)
        dsv = ds_ref[...].astype(F32)
        sg = _sigmoid(av)
        da_ref[...] = (dsv * b_ref[...].astype(F32) * sg * (1.0 + av * (1.0 - sg))).astype(BF16)
        db_ref[...] = (dsv * av * sg).astype(BF16)

    spec = pl.BlockSpec((bt, bf), lambda i, j: (i, j))
    return pl.pallas_call(
        body, name=name, grid=(t // bt, f // bf), in_specs=[spec, spec, spec], out_specs=[spec, spec],
        out_shape=[jax.ShapeDtypeStruct((t, f), BF16)] * 2, compiler_params=_params(("parallel", "parallel")),
    )(ds, a, b)


def _xa_probs(qh, kh, scale):
    s = lax.dot_general(qh, kh, (((1,), (1,)), ((), ())), preferred_element_type=F32) * scale
    s = s - jnp.max(s, axis=-1, keepdims=True)
    p = jnp.exp(s)
    return p / jnp.sum(p, axis=-1, keepdims=True)


def xattn_fwd(q, kv, *, name):
    t, d = q.shape
    m = kv.shape[0]
    hd = d // XA_HEADS
    bt = _blk(512, t)
    scale = hd ** -0.5

    def body(q_ref, kv_ref, o_ref):
        for h in range(XA_HEADS):
            qh = q_ref[:, h * hd:(h + 1) * hd]
            kh = kv_ref[:, h * hd:(h + 1) * hd]
            vh = kv_ref[:, d + h * hd:d + (h + 1) * hd]
            p = _xa_probs(qh, kh, scale)
            o_ref[:, h * hd:(h + 1) * hd] = jnp.dot(p.astype(BF16), vh, preferred_element_type=F32).astype(BF16)

    return pl.pallas_call(
        body, name=name, grid=(t // bt,),
        in_specs=[pl.BlockSpec((bt, d), lambda i: (i, 0)), pl.BlockSpec((m, 2 * d), lambda i: (0, 0))],
        out_specs=pl.BlockSpec((bt, d), lambda i: (i, 0)), out_shape=jax.ShapeDtypeStruct((t, d), BF16),
        compiler_params=_params(("parallel",)),
    )(q, kv)


def xattn_bwd(q, kv, do, *, name):
    t, d = q.shape
    m = kv.shape[0]
    hd = d // XA_HEADS
    bt = _blk(512, t)
    scale = hd ** -0.5

    def body(q_ref, kv_ref, do_ref, dq_ref, dkv_ref):
        first = pl.program_id(0) == 0
        for h in range(XA_HEADS):
            qs, ks, vs = slice(h * hd, (h + 1) * hd), slice(h * hd, (h + 1) * hd), slice(d + h * hd, d + (h + 1) * hd)
            qh, kh, vh, doh = q_ref[:, qs], kv_ref[:, ks], kv_ref[:, vs], do_ref[:, qs]
            p = _xa_probs(qh, kh, scale)
            dp = lax.dot_general(doh, vh, (((1,), (1,)), ((), ())), preferred_element_type=F32)
            dsv = p * (dp - jnp.sum(p * dp, axis=-1, keepdims=True)) * scale
            dsb = dsv.astype(BF16)
            dq_ref[:, qs] = jnp.dot(dsb, kh, preferred_element_type=F32).astype(BF16)
            dk = lax.dot_general(dsb, qh, (((0,), (0,)), ((), ())), preferred_element_type=F32)
            dv = lax.dot_general(p.astype(BF16), doh, (((0,), (0,)), ((), ())), preferred_element_type=F32)

            @pl.when(first)
            def _():
                dkv_ref[:, ks] = dk
                dkv_ref[:, vs] = dv

            @pl.when(jnp.logical_not(first))
            def _():
                dkv_ref[:, ks] += dk
                dkv_ref[:, vs] += dv

    row = pl.BlockSpec((bt, d), lambda i: (i, 0))
    full = pl.BlockSpec((m, 2 * d), lambda i: (0, 0))
    return pl.pallas_call(
        body, name=name, grid=(t // bt,), in_specs=[row, full, row], out_specs=[row, full],
        out_shape=[jax.ShapeDtypeStruct((t, d), BF16), jax.ShapeDtypeStruct((m, 2 * d), F32)],
        compiler_params=_params(("arbitrary",)),
    )(q, kv, do)


def _pool_p(buf, uv, rows, w, bt):
    acc = uv
    for dd in range(1, w):
        acc = acc + buf[pl.ds(POOL_HALO - dd, bt), :]
    cnt = jnp.minimum(rows + 1, w).astype(F32)
    return acc / cnt - uv


def pool_fwd(z, w_pool, scale, *, name):
    t = z.shape[0]
    ng, gsz = w_pool.shape[0], w_pool.shape[1]
    mix = ng * gsz
    bt = _blk(512, t)

    def body(u_ref, uh_ref, w_ref, sc_ref, o_ref, buf):
        r = pl.program_id(0)
        rows = r * bt + lax.broadcasted_iota(jnp.int32, (bt, 1), 0)
        for g in range(ng):
            gs = slice(g * gsz, (g + 1) * gsz)
            uv = u_ref[:, gs]
            buf[0:POOL_HALO, :] = jnp.where(r > 0, uh_ref[:, gs], 0.0)
            buf[POOL_HALO:POOL_HALO + bt, :] = uv
            p = _pool_p(buf, uv, rows, POOL_WINDOWS[g], bt)
            y = jnp.dot(p.astype(BF16), w_ref[g], preferred_element_type=F32) * sc_ref[:, gs]
            o_ref[:, gs] = y.astype(BF16)

    hb = bt // POOL_HALO
    return pl.pallas_call(
        body, name=name, grid=(t // bt,),
        in_specs=[pl.BlockSpec((bt, mix), lambda i: (i, 0)),
                  pl.BlockSpec((POOL_HALO, mix), lambda i: (jnp.maximum(i * hb - 1, 0), 0)),
                  pl.BlockSpec((ng, gsz, gsz), lambda i: (0, 0, 0)), pl.BlockSpec((1, mix), lambda i: (0, 0))],
        out_specs=pl.BlockSpec((None, bt, mix), lambda i: (0, i, 0)),
        out_shape=jax.ShapeDtypeStruct((2, t, mix), BF16),
        scratch_shapes=[pltpu.VMEM((POOL_HALO + bt, gsz), F32)],
        compiler_params=_params(("parallel",)),
    )(z, z, w_pool, scale)


def pool_bwd(z, dcat, w_pool, scale, *, name):
    t = z.shape[0]
    ng, gsz = w_pool.shape[0], w_pool.shape[1]
    mix = ng * gsz
    bt = _blk(512, t)
    nb = t // bt
    nt_dims = (((1,), (1,)), ((), ()))
    tn_dims = (((0,), (0,)), ((), ()))

    def body(u_ref, uh_ref, dy_ref, dyh_ref, w_ref, sc_ref, du_ref, dw_ref, dsc_ref, buf, buf2):
        r = pl.program_id(0)
        first = r == 0
        rows = r * bt + lax.broadcasted_iota(jnp.int32, (bt, 1), 0)
        rows_h = (r + 1) * bt + lax.broadcasted_iota(jnp.int32, (POOL_HALO, 1), 0)
        for g in range(ng):
            w = POOL_WINDOWS[g]
            gs = slice(g * gsz, (g + 1) * gsz)
            uv = u_ref[:, gs]
            buf[0:POOL_HALO, :] = jnp.where(r > 0, uh_ref[:, gs], 0.0)
            buf[POOL_HALO:POOL_HALO + bt, :] = uv
            pb = _pool_p(buf, uv, rows, w, bt).astype(BF16)
            wg = w_ref[g]
            sc = sc_ref[:, gs]
            y0 = jnp.dot(pb, wg, preferred_element_type=F32)
            dyv = dy_ref[:, gs].astype(F32)
            dsc = jnp.sum(dyv * y0, axis=0, keepdims=True)
            dyw = (dyv * sc).astype(BF16)
            dw = lax.dot_general(pb, dyw, tn_dims, preferred_element_type=F32)

            @pl.when(first)
            def _():
                dw_ref[g] = dw
                dsc_ref[:, gs] = dsc

            @pl.when(jnp.logical_not(first))
            def _():
                dw_ref[g] += dw
                dsc_ref[:, gs] += dsc

            dp = lax.dot_general(dyw, wg, nt_dims, preferred_element_type=F32)
            dyh = (dyh_ref[:, gs].astype(F32) * sc).astype(BF16)
            dph = lax.dot_general(dyh, wg, nt_dims, preferred_element_type=F32)
            dph = jnp.where(r < nb - 1, dph, 0.0)
            buf2[0:bt, :] = dp / jnp.minimum(rows + 1, w).astype(F32)
            buf2[bt:bt + POOL_HALO, :] = dph / jnp.minimum(rows_h + 1, w).astype(F32)
            acc = buf2[pl.ds(0, bt), :]
            for dd in range(1, w):
                acc = acc + buf2[pl.ds(dd, bt), :]
            du_ref[:, gs] = (acc - dp).astype(BF16)

    hb = bt // POOL_HALO
    nhb = t // POOL_HALO
    return pl.pallas_call(
        body, name=name, grid=(nb,),
        in_specs=[pl.BlockSpec((bt, mix), lambda i: (i, 0)),
                  pl.BlockSpec((POOL_HALO, mix), lambda i: (jnp.maximum(i * hb - 1, 0), 0)),
                  pl.BlockSpec((None, bt, mix), lambda i: (0, i, 0)),
                  pl.BlockSpec((None, POOL_HALO, mix), lambda i: (0, jnp.minimum((i + 1) * hb, nhb - 1), 0)),
                  pl.BlockSpec((ng, gsz, gsz), lambda i: (0, 0, 0)), pl.BlockSpec((1, mix), lambda i: (0, 0))],
        out_specs=[pl.BlockSpec((None, bt, mix), lambda i: (4, i, 0)),
                   pl.BlockSpec((ng, gsz, gsz), lambda i: (0, 0, 0)), pl.BlockSpec((1, mix), lambda i: (0, 0))],
        out_shape=[jax.ShapeDtypeStruct((5, t, mix), BF16), jax.ShapeDtypeStruct((ng, gsz, gsz), F32),
                   jax.ShapeDtypeStruct((1, mix), F32)],
        scratch_shapes=[pltpu.VMEM((POOL_HALO + bt, gsz), F32), pltpu.VMEM((bt + POOL_HALO, gsz), F32)],
        compiler_params=_params(("arbitrary",)),
    )(z, z, dcat, dcat, w_pool, scale)


HG_LEVELS = ((64, 31), (32, 15), (16, 7))
HG_DIAG = (8, 3)
_NT = (((1,), (1,)), ((), ()))
_TN = (((0,), (0,)), ((), ()))
_HI = lax.Precision.HIGHEST


def _hg_masks():
    c = HG_CHUNK
    t = lax.broadcasted_iota(jnp.int32, (c, c), 0)
    s = lax.broadcasted_iota(jnp.int32, (c, c), 1)
    masks = []
    for blk, row in HG_LEVELS:
        sh = blk.bit_length() - 1
        same = (t >> sh) == (s >> sh)
        masks.append(same & ((t & (blk - 1)) > row) & ((s & (blk - 1)) <= row))
    sh = HG_DIAG[0].bit_length() - 1
    masks.append(((t >> sh) == (s >> sh)) & (s <= t))
    return t, s, masks


def _row_of_block(x, blk, row):
    c, n = x.shape
    x3 = x.reshape(c // blk, blk, n)
    return jnp.broadcast_to(x3[:, row:row + 1, :], x3.shape).reshape(c, n)


def _hg_parts(qv, flv, lb, masks, tri):
    sgf = _sigmoid(flv)
    f = lb + (1.0 - lb) * sgf
    logf = jnp.log(f)
    kk = 1.0 - f
    sgq = _sigmoid(qv)
    qf = qv * sgq * (HG_HEAD ** -0.5)
    bc = jnp.dot(tri, logf, preferred_element_type=F32, precision=_HI)
    levels = []
    a = None
    for li, (blk, row) in enumerate(HG_LEVELS + (HG_DIAG,)):
        e = bc - _row_of_block(bc, blk, row)
        if li < len(HG_LEVELS):
            eq, ek = jnp.exp(jnp.minimum(e, 0.0)), jnp.exp(jnp.minimum(-e, 0.0))
        else:
            eq, ek = jnp.exp(jnp.clip(e, -EXP_CLAMP, EXP_CLAMP)), jnp.exp(jnp.clip(-e, -EXP_CLAMP, EXP_CLAMP))
        qt, kt = qf * eq, kk * ek
        part = jnp.where(masks[li], lax.dot_general(qt.astype(BF16), kt.astype(BF16), _NT, preferred_element_type=F32), 0.0)
        a = part if a is None else a + part
        levels.append((eq, ek, qt, kt))
    return dict(sgf=sgf, f=f, kk=kk, sgq=sgq, qf=qf, bc=bc, levels=levels, a=a)


def hgrn_fwd(z, cat, lb, gain, mix_a, *, side=None, name):
    t = z.shape[0]
    mix_b = lb.shape[1]
    nh = mix_b // HG_HEAD
    bt = _blk(256, t)
    ncb = bt // HG_CHUNK
    dh = HG_HEAD

    def body(q_ref, fl_ref, i_ref, g_ref, lb_ref, gain_ref, cat_in, o_ref, st_ref, st):
        del cat_in

        @pl.when(pl.program_id(1) == 0)
        def _():
            st[...] = jnp.zeros_like(st)

        t_i, s_i, masks = _hg_masks()
        tri = (s_i <= t_i).astype(F32)
        lbv, gn = lb_ref[...], gain_ref[...]
        for c in range(ncb):
            rs = slice(c * HG_CHUNK, (c + 1) * HG_CHUNK)
            pr = _hg_parts(q_ref[rs, :], fl_ref[rs, :], lbv, masks, tri)
            vb = i_ref[rs, :].astype(BF16)
            stv = st[...]
            st_ref[c] = stv
            bc = pr["bc"]
            qt = pr["qf"] * jnp.exp(bc)
            o = (jnp.dot(pr["a"].astype(BF16), vb, preferred_element_type=F32)
                 + lax.dot_general(qt.astype(BF16), stv.astype(BF16), _NT, preferred_element_type=F32))
            bl = bc[HG_CHUNK - 1:HG_CHUNK, :]
            khat = pr["kk"] * jnp.exp(bl - bc)
            st[...] = stv * jnp.exp(bl) + lax.dot_general(vb, khat.astype(BF16), _TN, preferred_element_type=F32)
            r = lax.rsqrt(jnp.mean(o * o, axis=-1, keepdims=True) + EPS)
            gv = g_ref[rs, :]
            o_ref[rs, :] = (o * r * gn * (gv * _sigmoid(gv))).astype(BF16)

    def col(which):
        base = (mix_a + which * mix_b) // dh
        return pl.BlockSpec((bt, dh), lambda h, i: (i, base + h))

    return _call(
        body, side, name=name, grid=(nh, t // bt),
        in_specs=[col(0), col(1), col(2), col(3), pl.BlockSpec((1, dh), lambda h, i: (0, h)),
                  pl.BlockSpec((1, dh), lambda h, i: (0, 0)), _ANY],
        out_specs=[pl.BlockSpec((None, bt, dh), lambda h, i: (1, i, h)),
                   pl.BlockSpec((None, ncb, dh, dh), lambda h, i: (h, i, 0, 0))],
        out_shape=[jax.ShapeDtypeStruct(cat.shape, BF16), jax.ShapeDtypeStruct((nh, t // HG_CHUNK, dh, dh), F32)],
        scratch_shapes=[pltpu.VMEM((dh, dh), F32)], aliases={6: 0}, sem=("parallel", "arbitrary"),
        args=(z, z, z, z, lb, gain, cat))


def hgrn_bwd(z, dcat, dz5, states, lb, gain, mix_a, *, side=None, name):
    t = z.shape[0]
    mix_b = lb.shape[1]
    nh = mix_b // HG_HEAD
    bt = _blk(256, t)
    nb = t // bt
    ncb = bt // HG_CHUNK
    dh = HG_HEAD

    def body(q_ref, fl_ref, i_ref, g_ref, dy_ref, st_ref, lb_ref, gain_ref, dz_in, dz_ref, dlb_ref, dgn_ref, dst):
        del dz_in
        first = pl.program_id(1) == 0

        @pl.when(first)
        def _():
            dst[...] = jnp.zeros_like(dst)

        t_i, s_i, masks = _hg_masks()
        tri = (s_i <= t_i).astype(F32)
        triu = (s_i >= t_i).astype(F32)
        last_row = lax.broadcasted_iota(jnp.int32, (HG_CHUNK, 1), 0) == HG_CHUNK - 1
        lbv, gn = lb_ref[...], gain_ref[...]
        dlb_acc = jnp.zeros((1, dh), F32)
        dgn_acc = jnp.zeros((1, dh), F32)
        for c in reversed(range(ncb)):
            rs = slice(c * HG_CHUNK, (c + 1) * HG_CHUNK)
            qv, flv, gv = q_ref[rs, :], fl_ref[rs, :], g_ref[rs, :]
            pr = _hg_parts(qv, flv, lbv, masks, tri)
            vb = i_ref[rs, :].astype(BF16)
            stv = st_ref[c]
            stb = stv.astype(BF16)
            dsv = dst[...]
            dsb = dsv.astype(BF16)
            bc, kk, qf, ab = pr["bc"], pr["kk"], pr["qf"], pr["a"].astype(BF16)
            ebc = jnp.exp(bc)
            qt = qf * ebc
            qtb = qt.astype(BF16)
            o = jnp.dot(ab, vb, preferred_element_type=F32) + lax.dot_general(qtb, stb, _NT, preferred_element_type=F32)
            r = lax.rsqrt(jnp.mean(o * o, axis=-1, keepdims=True) + EPS)
            oh = o * r
            sgg = _sigmoid(gv)
            dyv = dy_ref[rs, :].astype(F32)
            don = dyv * (gv * sgg)
            dgate = dyv * (oh * gn) * (sgg * (1.0 + gv * (1.0 - sgg)))
            dgn_acc = dgn_acc + jnp.sum(don * oh, axis=0, keepdims=True)
            doh = don * gn
            do = r * (doh - oh * jnp.mean(doh * oh, axis=-1, keepdims=True))
            dob = do.astype(BF16)
            bl = bc[HG_CHUNK - 1:HG_CHUNK, :]
            ebl = jnp.exp(bl)
            ekh = jnp.exp(bl - bc)
            khat = kk * ekh
            dv = (lax.dot_general(ab, dob, _TN, preferred_element_type=F32)
                  + lax.dot_general(khat.astype(BF16), dsb, _NT, preferred_element_type=F32))
            da = lax.dot_general(dob, vb, _NT, preferred_element_type=F32)
            dqt = jnp.dot(dob, stb, preferred_element_type=F32)
            dkh = jnp.dot(vb, dsb, preferred_element_type=F32)
            dst[...] = dsv * ebl + lax.dot_general(dob, qtb, _TN, preferred_element_type=F32)
            dbl = jnp.sum(dsv * stv, axis=0, keepdims=True) * ebl + jnp.sum(dkh * khat, axis=0, keepdims=True)
            dqf = dqt * ebc
            dkk = dkh * ekh
            dbc = dqt * qt - dkh * khat
            for li, (eq, ek, qtl, ktl) in enumerate(pr["levels"]):
                gm = jnp.where(masks[li], da, 0.0).astype(BF16)
                qtr, ktr = qtl.astype(BF16), ktl.astype(BF16)
                dql = jnp.dot(gm, ktr, preferred_element_type=F32)
                dkl = lax.dot_general(gm, qtr, _TN, preferred_element_type=F32)
                dqf = dqf + dql * eq
                dkk = dkk + dkl * ek
                dbc = dbc + qtr.astype(F32) * dql - ktr.astype(F32) * dkl
            dbc = dbc + jnp.where(last_row, dbl, 0.0)
            dlogf = jnp.dot(triu, dbc, preferred_element_type=F32, precision=_HI)
            df = dlogf / pr["f"] - dkk
            sgf = pr["sgf"]
            dfl = df * (1.0 - lbv) * sgf * (1.0 - sgf)
            dlb_acc = dlb_acc + jnp.sum(df * (1.0 - sgf), axis=0, keepdims=True)
            sgq = pr["sgq"]
            dq = dqf * (HG_HEAD ** -0.5) * (sgq * (1.0 + qv * (1.0 - sgq)))
            dz_ref[0, rs, :] = dq.astype(BF16)
            dz_ref[1, rs, :] = dfl.astype(BF16)
            dz_ref[2, rs, :] = dv.astype(BF16)
            dz_ref[3, rs, :] = dgate.astype(BF16)

        @pl.when(first)
        def _():
            dlb_ref[...] = dlb_acc
            dgn_ref[...] = dgn_acc

        @pl.when(jnp.logical_not(first))
        def _():
            dlb_ref[...] += dlb_acc
            dgn_ref[...] += dgn_acc

    def col(which):
        base = (mix_a + which * mix_b) // dh
        return pl.BlockSpec((bt, dh), lambda h, i: (nb - 1 - i, base + h))

    return _call(
        body, side, name=name, grid=(nh, nb),
        in_specs=[col(0), col(1), col(2), col(3),
                  pl.BlockSpec((None, bt, dh), lambda h, i: (1, nb - 1 - i, h)),
                  pl.BlockSpec((None, ncb, dh, dh), lambda h, i: (h, nb - 1 - i, 0, 0)),
                  pl.BlockSpec((1, dh), lambda h, i: (0, h)), pl.BlockSpec((1, dh), lambda h, i: (0, 0)), _ANY],
        out_specs=[pl.BlockSpec((4, bt, dh), lambda h, i: (0, nb - 1 - i, h)),
                   pl.BlockSpec((1, dh), lambda h, i: (0, h)),
                   pl.BlockSpec((None, 1, dh), lambda h, i: (h, 0, 0))],
        out_shape=[jax.ShapeDtypeStruct(dz5.shape, BF16), jax.ShapeDtypeStruct((1, mix_b), F32),
                   jax.ShapeDtypeStruct((nh, 1, dh), F32)],
        scratch_shapes=[pltpu.VMEM((dh, dh), F32)], aliases={8: 0}, sem=("parallel", "arbitrary"),
        args=(z, z, z, z, dcat, states, lb, gain, dz5))


def _fox_scores(qb, kb, fk, scale, masked):
    s = lax.dot_general(qb, kb, _NT, preferred_element_type=F32) * scale - fk
    if masked:
        n = s.shape[0]
        row = lax.broadcasted_iota(jnp.int32, (n, n), 0)
        col = lax.broadcasted_iota(jnp.int32, (n, n), 1)
        s = jnp.where(col <= row, s, -jnp.inf)
    return s


def fox_fwd(qkv, fk, *, side=None, name):
    _, t, d = qkv.shape
    nh = d // FOX_HEAD
    b = _blk(FOX_BLK, t)
    nb = t // b
    dh = FOX_HEAD
    scale = dh ** -0.5

    def body(q_ref, k_ref, v_ref, f_ref, o_ref, lse_ref):
        qi = pl.program_id(1)
        qb = q_ref[...]

        def step(kj, carry, masked):
            m, l, acc = carry
            off = pl.multiple_of(kj * b, b)
            s = _fox_scores(qb, k_ref[pl.ds(off, b), :], f_ref[kj], scale, masked)
            m_new = jnp.maximum(m, jnp.max(s, axis=-1, keepdims=True))
            alpha = jnp.exp(m - m_new)
            p = jnp.exp(s - m_new)
            l = alpha * l + jnp.sum(p, axis=-1, keepdims=True)
            acc = alpha * acc + jnp.dot(p.astype(BF16), v_ref[pl.ds(off, b), :], preferred_element_type=F32)
            return m_new, l, acc

        init = (jnp.full((b, 1), -jnp.inf, F32), jnp.zeros((b, 1), F32), jnp.zeros((b, dh), F32))
        carry = lax.fori_loop(0, qi, lambda kj, c: step(kj, c, False), init)
        m, l, acc = step(qi, carry, True)
        o_ref[...] = (acc / l).astype(BF16)
        lse_ref[...] = m + jnp.log(l)

    return _call(
        body, side, name=name, grid=(nh, nb),
        in_specs=[pl.BlockSpec((None, b, dh), lambda h, i: (0, i, h)),
                  pl.BlockSpec((None, t, dh), lambda h, i: (1, 0, h)),
                  pl.BlockSpec((None, t, dh), lambda h, i: (2, 0, h)),
                  pl.BlockSpec((None, nb, 1, b), lambda h, i: (h, 0, 0, 0))],
        out_specs=[pl.BlockSpec((b, dh), lambda h, i: (i, h)), pl.BlockSpec((None, b, 1), lambda h, i: (h, i, 0))],
        out_shape=[jax.ShapeDtypeStruct((t, d), BF16), jax.ShapeDtypeStruct((nh, t, 1), F32)],
        sem=("parallel", "parallel"), args=(qkv, qkv, qkv, fk))


def fox_bwd_dq(qkv, fk, do, lse, *, side=None, name):
    _, t, d = qkv.shape
    nh = d // FOX_HEAD
    b = _blk(FOX_BLK, t)
    nb = t // b
    dh = FOX_HEAD
    scale = dh ** -0.5

    def body(q_ref, k_ref, v_ref, f_ref, do_ref, lse_ref, dq_ref, dl_ref, p_buf, dp_buf):
        qi = pl.program_id(1)
        qb, dob, lse_v = q_ref[...], do_ref[...], lse_ref[...]

        def first(kj, dl, masked):
            off = pl.multiple_of(kj * b, b)
            p = jnp.exp(_fox_scores(qb, k_ref[pl.ds(off, b), :], f_ref[kj], scale, masked) - lse_v)
            dp = lax.dot_general(dob, v_ref[pl.ds(off, b), :], _NT, preferred_element_type=F32)
            p_buf[kj] = p
            dp_buf[kj] = dp
            return dl + jnp.sum(p * dp, axis=-1, keepdims=True)

        dl = lax.fori_loop(0, qi, lambda kj, c: first(kj, c, False), jnp.zeros((b, 1), F32))
        dl = first(qi, dl, True)
        dl_ref[...] = dl

        def second(kj, dq):
            off = pl.multiple_of(kj * b, b)
            dsv = p_buf[kj] * (dp_buf[kj] - dl)
            return dq + jnp.dot(dsv.astype(BF16), k_ref[pl.ds(off, b), :], preferred_element_type=F32)

        dq = lax.fori_loop(0, qi + 1, second, jnp.zeros((b, dh), F32))
        dq_ref[...] = (dq * scale).astype(BF16)

    col = pl.BlockSpec((None, b, 1), lambda h, i: (h, i, 0))
    return _call(
        body, side, name=name, grid=(nh, nb),
        in_specs=[pl.BlockSpec((None, b, dh), lambda h, i: (0, i, h)),
                  pl.BlockSpec((None, t, dh), lambda h, i: (1, 0, h)),
                  pl.BlockSpec((None, t, dh), lambda h, i: (2, 0, h)),
                  pl.BlockSpec((None, nb, 1, b), lambda h, i: (h, 0, 0, 0)),
                  pl.BlockSpec((b, dh), lambda h, i: (i, h)), col],
        out_specs=[pl.BlockSpec((None, b, dh), lambda h, i: (2, i, h)), col],
        out_shape=[jax.ShapeDtypeStruct((3, t, d), BF16), jax.ShapeDtypeStruct((nh, t, 1), F32)],
        scratch_shapes=[pltpu.VMEM((nb, b, b), F32), pltpu.VMEM((nb, b, b), F32)],
        sem=("parallel", "parallel"), args=(qkv, qkv, qkv, fk, do, lse))


def fox_bwd_dkv(qkv, fk, do, lse, delta, dqkv, *, side=None, name):
    _, t, d = qkv.shape
    nh = d // FOX_HEAD
    b = _blk(FOX_BLK, t)
    nb = t // b
    dh = FOX_HEAD
    scale = dh ** -0.5

    def body(q_ref, k_ref, v_ref, f_ref, do_ref, lse_ref, dl_ref, dz_in, dkv_ref, df_ref):
        del dz_in
        kj = pl.program_id(1)
        kb, vb, fkv = k_ref[...], v_ref[...], f_ref[...]

        def step(qi, carry, masked):
            dk, dv, df = carry
            off = pl.multiple_of(qi * b, b)
            qb, dob = q_ref[pl.ds(off, b), :], do_ref[pl.ds(off, b), :]
            p = jnp.exp(_fox_scores(qb, kb, fkv, scale, masked) - lse_ref[pl.ds(off, b), :])
            dv = dv + lax.dot_general(p.astype(BF16), dob, _TN, preferred_element_type=F32)
            dp = lax.dot_general(dob, vb, _NT, preferred_element_type=F32)
            dsv = p * (dp - dl_ref[pl.ds(off, b), :])
            dk = dk + lax.dot_general(dsv.astype(BF16), qb, _TN, preferred_element_type=F32)
            return dk, dv, df - jnp.sum(dsv, axis=0, keepdims=True)

        init = (jnp.zeros((b, dh), F32), jnp.zeros((b, dh), F32), jnp.zeros((1, b), F32))
        carry = step(kj, init, True)
        dk, dv, df = lax.fori_loop(kj + 1, nb, lambda qi, c: step(qi, c, False), carry)
        dkv_ref[0] = (dk * scale).astype(BF16)
        dkv_ref[1] = dv.astype(BF16)
        df_ref[...] = df

    col = pl.BlockSpec((None, t, 1), lambda h, j: (h, 0, 0))
    return _call(
        body, side, name=name, grid=(nh, nb),
        in_specs=[pl.BlockSpec((None, t, dh), lambda h, j: (0, 0, h)),
                  pl.BlockSpec((None, b, dh), lambda h, j: (1, j, h)),
                  pl.BlockSpec((None, b, dh), lambda h, j: (2, j, h)),
                  pl.BlockSpec((None, None, 1, b), lambda h, j: (h, j, 0, 0)),
                  pl.BlockSpec((t, dh), lambda h, j: (0, h)), col, col, pl.BlockSpec(memory_space=pl.ANY)],
        out_specs=[pl.BlockSpec((2, b, dh), lambda h, j: (0, j, h)),
                   pl.BlockSpec((None, None, 1, b), lambda h, j: (h, j, 0, 0))],
        out_shape=[jax.ShapeDtypeStruct((3, t, d), BF16), jax.ShapeDtypeStruct((nh, nb, 1, b), F32)],
        aliases={7: 0}, sem=("parallel", "parallel"), args=(qkv, qkv, qkv, fk, do, lse, delta, dqkv))


FL_BLK = 256


def _log_sigmoid(x):
    return jnp.minimum(x, 0.0) - jnp.log(1.0 + jnp.exp(-jnp.abs(x)))


def fl_fwd(zf, bf, *, name):
    t, n = zf.shape
    bt = _blk(FL_BLK, t)

    def body(z_ref, b_ref, o_ref, carry):
        @pl.when(pl.program_id(0) == 0)
        def _():
            carry[...] = jnp.zeros_like(carry)

        ls = _log_sigmoid(z_ref[...] + b_ref[...])
        r = lax.broadcasted_iota(jnp.int32, (bt, bt), 0)
        c = lax.broadcasted_iota(jnp.int32, (bt, bt), 1)
        cs = jnp.dot((c <= r).astype(F32), ls, preferred_element_type=F32, precision=_HI) + carry[...]
        o_ref[...] = cs
        carry[...] = cs[bt - 1:bt, :]

    return pl.pallas_call(
        body, name=name, grid=(t // bt,),
        in_specs=[pl.BlockSpec((bt, n), lambda i: (i, 0)), pl.BlockSpec((1, n), lambda i: (0, 0))],
        out_specs=pl.BlockSpec((bt, n), lambda i: (i, 0)), out_shape=jax.ShapeDtypeStruct((t, n), F32),
        scratch_shapes=[pltpu.VMEM((1, n), F32)], compiler_params=_params(("arbitrary",)),
    )(zf, bf)


def fl_bwd(df, zf, bf, *, name):
    t, n = zf.shape
    bt = _blk(FL_BLK, t)
    nb = t // bt

    def body(df_ref, z_ref, b_ref, dz_ref, db_ref, carry):
        first = pl.program_id(0) == 0

        @pl.when(first)
        def _():
            carry[...] = jnp.zeros_like(carry)

        r = lax.broadcasted_iota(jnp.int32, (bt, bt), 0)
        c = lax.broadcasted_iota(jnp.int32, (bt, bt), 1)
        dls = jnp.dot((c >= r).astype(F32), df_ref[...], preferred_element_type=F32, precision=_HI) + carry[...]
        carry[...] = dls[0:1, :]
        dz = dls * (1.0 - _sigmoid(z_ref[...] + b_ref[...]))
        dz_ref[...] = dz.astype(BF16)
        part = jnp.sum(dz, axis=0, keepdims=True)

        @pl.when(first)
        def _():
            db_ref[...] = part

        @pl.when(jnp.logical_not(first))
        def _():
            db_ref[...] += part

    row = pl.BlockSpec((bt, n), lambda i: (nb - 1 - i, 0))
    vec = pl.BlockSpec((1, n), lambda i: (0, 0))
    return pl.pallas_call(
        body, name=name, grid=(nb,), in_specs=[row, row, vec], out_specs=[row, vec],
        out_shape=[jax.ShapeDtypeStruct((t, n), BF16), jax.ShapeDtypeStruct((1, n), F32)],
        scratch_shapes=[pltpu.VMEM((1, n), F32)], compiler_params=_params(("arbitrary",)),
    )(df, zf, bf)


def _adamw_math(w, g, m, v):
    m = ADAM_B1 * m + (1.0 - ADAM_B1) * g
    v = ADAM_B2 * v + (1.0 - ADAM_B2) * (g * g)
    m_hat = m / (1.0 - ADAM_B1 ** ADAM_STEP)
    v_hat = v / (1.0 - ADAM_B2 ** ADAM_STEP)
    delta = -ADAM_LR * (m_hat / (jnp.sqrt(v_hat) + ADAM_EPS) + ADAM_WD * w)
    return delta, m, v


def adamw(w, g, m, v, *, name):
    r, c = w.shape
    br = _blk(256, r)

    def body(w_ref, g_ref, m_ref, v_ref, go_ref, d_ref, mo_ref, vo_ref):
        gv = g_ref[...]
        go_ref[...] = gv
        d_ref[...], mo_ref[...], vo_ref[...] = _adamw_math(w_ref[...], gv, m_ref[...], v_ref[...])

    spec = pl.BlockSpec((br, c), lambda i: (i, 0))
    return pl.pallas_call(
        body, name=name, grid=(r // br,), in_specs=[spec] * 4, out_specs=[spec] * 4,
        out_shape=[jax.ShapeDtypeStruct((r, c), F32)] * 4, compiler_params=_params(("parallel",)),
    )(w, g, m, v)


def _f2(a):
    return a.reshape(a.shape[-2:])


def _local_step(x0, mem, tgt, sp, plan):
    t, d = x0.shape
    mix_a = sp["pool_scale"].shape[1]
    small = {}

    def row(a, l):
        return a[l:l + 1]

    def xattn_f(l, xin):
        w = plan.weights(f"xa{l}")
        hx = rmsnorm_fwd(xin, row(sp["xa_norm"], l), name=f"xa_norm_f{l}")
        q = _f2(matmul(vm2(hx), w["wq"], "nn", out_dtype=BF16, side=plan.take_fwd(), name=f"xa_q_f{l}"))
        mn = rmsnorm_fwd(mem, row(sp["xa_mem_norm"], l), name=f"xa_memnorm_f{l}")
        kv = _f2(matmul(vm2(mn), w["wkv"], "nn", out_dtype=BF16, name=f"xa_kv_f{l}"))
        o = xattn_fwd(q, kv, name=f"xa_attn_f{l}")
        xout = _f2(matmul(vm2(o), w["wo"], "nn", out_dtype=F32, res=vm2(xin), side=plan.take_fwd(), name=f"xa_o_f{l}"))
        return xout, (xin, hx, q, mn, kv, o)

    def ffn_f(l, xin):
        w = plan.weights(f"ffn{l}")
        hf = rmsnorm_fwd(xin, row(sp["ffn_norm"], l), name=f"ffn_norm_f{l}")
        a = _f2(matmul(vm2(hf), w["gate"], "nn", out_dtype=BF16, side=plan.take_fwd(), name=f"ffn_gate_f{l}"))
        b = _f2(matmul(vm2(hf), w["up"], "nn", out_dtype=BF16, side=plan.take_fwd(), name=f"ffn_up_f{l}"))
        s = swiglu_fwd(a, b, name=f"ffn_act_f{l}")
        xout = _f2(matmul(vm2(s), w["down"], "nn", out_dtype=F32, res=vm2(xin), side=plan.take_fwd(), name=f"ffn_down_f{l}"))
        return xout, (xin, hf, a, b, s)

    ev = plan.weights("ev")
    h0 = rmsnorm_fwd(x0, sp["ev_norm"], name="ev_norm_f")
    z = _f2(matmul(vm2(h0), ev["ev_in"], "nn", out_dtype=F32, name="ev_in_f"))
    cat = pool_fwd(z, ev["pool"], sp["pool_scale"], name="pool_f")
    cat, states = hgrn_fwd(z, cat, sp["lb"], sp["hg_gain"], mix_a, side=plan.take_fwd(), name="hgrn_f")
    x1 = _f2(matmul(VM(cat), ev["ev_out"], "nn", out_dtype=F32, res=vm2(x0), side=plan.take_fwd(), name="ev_out_f"))
    x2, xa0 = xattn_f(0, x1)
    x3, ff0 = ffn_f(0, x2)

    od = plan.weights("od")
    ho = rmsnorm_fwd(x3, sp["od_norm"], name="od_norm_f")
    qkv = matmul(vm2(ho), od["wqkv"], "nn", out_dtype=BF16, out_p=3, side=plan.take_fwd(), name="od_qkv_f")
    zf = _f2(matmul(vm2(ho), od["wf"], "nn", out_dtype=F32, name="od_fl_f"))
    fcum = fl_fwd(zf, sp["bf"], name="od_forget_f")
    nh = d // FOX_HEAD
    nfb = t // _blk(FOX_BLK, t)
    fk = fcum[:, :nh].T.reshape(nh, nfb, 1, t // nfb)
    of, lse = fox_fwd(qkv, fk, side=plan.take_fwd(), name="fox_f")
    x4 = _f2(matmul(vm2(of), od["od_out"], "nn", out_dtype=F32, res=vm2(x3), name="od_out_f"))
    x5, xa1 = xattn_f(1, x4)
    x6, ff1 = ffn_f(1, x5)
    loss, dx, dxb, small["final_norm"] = loss_head(x6, sp["final_norm"], tgt, name="loss_head")

    def ffn_b(l, saved, dx, dxb):
        xin, hf, a, b, s = saved
        w = plan.weights(f"ffn{l}")
        dsv = _f2(matmul(vm2(dxb), w["down"], "nt", out_dtype=BF16, name=f"ffn_down_bx{l}"))
        g_down = matmul(vm2(s), vm2(dxb), "tn", out_dtype=BF16, out_kind="rs", out_p=N_CHIPS, name=f"ffn_down_bw{l}")
        da, db = swiglu_bwd(dsv, a, b, name=f"ffn_act_b{l}")
        g_gate = matmul(vm2(hf), vm2(da), "tn", out_dtype=BF16, out_p=N_CHIPS, name=f"ffn_gate_bw{l}")
        g_up = matmul(vm2(hf), vm2(db), "tn", out_dtype=BF16, out_p=N_CHIPS, name=f"ffn_up_bw{l}")
        plan.grads_done({f"down{l}": g_down, f"gate{l}": g_gate, f"up{l}": g_up})
        dh = matmul(vm2(da), w["gate"], "nt", out_dtype=F32, name=f"ffn_gate_bx{l}")
        dh = _f2(matmul(vm2(db), w["up"], "nt", out_dtype=F32, res=VM(dh), name=f"ffn_up_bx{l}"))
        dx, dxb, dg = rmsnorm_bwd(xin, row(sp["ffn_norm"], l), dh, dx, name=f"ffn_norm_b{l}")
        return dx, dxb, dg

    def xattn_b(l, saved, dx, dxb):
        xin, hx, q, mn, kv, o = saved
        w = plan.weights(f"xa{l}")
        do = _f2(matmul(vm2(dxb), w["wo"], "nt", out_dtype=BF16, name=f"xa_o_bx{l}"))
        g_wo = matmul(vm2(o), vm2(dxb), "tn", out_dtype=BF16, out_kind="rs", out_p=N_CHIPS, name=f"xa_o_bw{l}")
        dq, dkv = xattn_bwd(q, kv, do, name=f"xa_attn_b{l}")
        g_wq = matmul(vm2(hx), vm2(dq), "tn", out_dtype=BF16, out_kind="rs", out_p=N_CHIPS, name=f"xa_q_bw{l}")
        dh = _f2(matmul(vm2(dq), w["wq"], "nt", out_dtype=F32, name=f"xa_q_bx{l}"))
        dkvb = dkv.astype(BF16)
        g_wkv = matmul(vm2(mn), vm2(dkvb), "tn", out_dtype=BF16, out_p=N_CHIPS, name=f"xa_kv_bw{l}")
        plan.grads_done({f"wo{l}": g_wo, f"wq{l}": g_wq, f"wkv{l}": g_wkv})
        dmn = _f2(matmul(vm2(dkvb), w["wkv"], "nt", out_dtype=F32, name=f"xa_kv_bx{l}"))
        (dgm,) = rmsnorm_bwd(mem, row(sp["xa_mem_norm"], l), dmn, None, name=f"xa_memnorm_b{l}")
        dx, dxb, dg = rmsnorm_bwd(xin, row(sp["xa_norm"], l), dh, dx, name=f"xa_norm_b{l}")
        return dx, dxb, dg, dgm

    dg_ffn, dg_xa, dg_mem = [None, None], [None, None], [None, None]
    dx, dxb, dg_ffn[1] = ffn_b(1, ff1, dx, dxb)
    dx, dxb, dg_xa[1], dg_mem[1] = xattn_b(1, xa1, dx, dxb)

    do = _f2(matmul(vm2(dxb), od["od_out"], "nt", out_dtype=BF16, name="od_out_bx"))
    g_od_out = matmul(vm2(of), vm2(dxb), "tn", out_dtype=BF16, out_kind="rs", out_p=N_CHIPS, name="od_out_bw")
    dz3, delta = fox_bwd_dq(qkv, fk, do, lse, side=plan.take_bwd(), name="fox_bq")
    dz3, dfk = fox_bwd_dkv(qkv, fk, do, lse, delta, dz3, side=plan.take_bwd(), name="fox_bkv")
    dfc = jnp.pad(dfk.reshape(nh, t).T, ((0, 0), (0, zf.shape[1] - nh)))
    dzf, dbf = fl_bwd(dfc, zf, sp["bf"], name="od_forget_b")
    dqkv = VM(dz3, "cs", pfn=lambda p: lax.rem(p + 2, 3))
    dwqkv = _f2(matmul(vm2(ho), dqkv, "tn", out_dtype=BF16, name="od_qkv_bw"))
    dwf = _f2(matmul(vm2(ho), vm2(dzf), "tn", out_dtype=BF16, name="od_fl_bw"))
    od_in_full = jnp.concatenate([dwqkv, dwf[:, :nh]], axis=1)
    plan.grads_done({"od_out": g_od_out, "od_in": od_in_full.reshape(d, N_CHIPS, -1).transpose(1, 0, 2)})
    dh = matmul(dqkv, od["wqkv"], "nt", out_dtype=F32, name="od_qkv_bx")
    dh = _f2(matmul(vm2(dzf), od["wf"], "nt", out_dtype=F32, res=VM(dh), name="od_fl_bx"))
    dx, dxb, small["od_norm"] = rmsnorm_bwd(x3, sp["od_norm"], dh, dx, name="od_norm_b")
    small["bf"] = dbf

    dx, dxb, dg_ffn[0] = ffn_b(0, ff0, dx, dxb)
    dx, dxb, dg_xa[0], dg_mem[0] = xattn_b(0, xa0, dx, dxb)

    dcat = matmul(vm2(dxb), ev["ev_out"], "nt", out_dtype=BF16, out_p=2, name="ev_out_bx")
    g_ev_out = matmul(VM(cat), vm2(dxb), "tn", out_dtype=BF16, out_kind="rs", out_p=N_CHIPS, name="ev_out_bw")
    dz5, g_pool, small["pool_scale"] = pool_bwd(z, dcat, ev["pool"], sp["pool_scale"], name="pool_b")
    dz5, small["lb"], dgn = hgrn_bwd(z, dcat, dz5, states, sp["lb"], sp["hg_gain"], mix_a, side=plan.take_bwd(2), name="hgrn_b")
    small["hg_gain"] = jnp.sum(dgn, axis=0)
    dzv = VM(dz5, "cs", pfn=lambda p: lax.rem(p + 4, 5))
    g_ev_in = matmul(vm2(h0), dzv, "tn", out_dtype=BF16, out_p=N_CHIPS, side=plan.take_bwd(), name="ev_in_bw")
    ng, gsz = g_pool.shape[0], g_pool.shape[1]
    pool_parts = g_pool.reshape(ng, N_CHIPS, gsz // N_CHIPS, gsz).transpose(1, 0, 2, 3).reshape(N_CHIPS, gsz, gsz).astype(BF16)
    plan.grads_done({"ev_out": g_ev_out, "pool": pool_parts, "ev_in": g_ev_in})
    dh = _f2(matmul(dzv, ev["ev_in"], "nt", out_dtype=F32, side=plan.take_bwd(), name="ev_in_bx"))
    dx, _, small["ev_norm"] = rmsnorm_bwd(x0, sp["ev_norm"], dh, dx, name="ev_norm_b")

    small["xa_norm"] = jnp.concatenate(dg_xa, axis=0)
    small["xa_mem_norm"] = jnp.concatenate(dg_mem, axis=0)
    small["ffn_norm"] = jnp.concatenate(dg_ffn, axis=0)
    return loss, dx, small


def gather_forward(fulls, *, name):
    n = len(fulls)

    def body(*refs):
        outs = refs[n:2 * n]
        ssem, rsem = refs[2 * n:]
        x, y, c, chips = _me()
        sibling = (x, y, 1 - c)

        def rows(w, j, which):
            h = fulls[w].shape[1] // 2
            return outs[w].at[_chip_id(chips[j]), pl.ds(which * h, h)]

        def swap(w, j):
            return _rcopy(rows(w, j, c), rows(w, j, c), ssem.at[w, j], rsem.at[w, j], sibling)

        for w in range(n):
            for j in range(3):
                swap(w, j).start()
        for w in range(n):
            for j in range(3):
                swap(w, j).wait_send()
                _rcopy(rows(w, j, 1 - c), rows(w, j, 1 - c), ssem.at[w, j], rsem.at[w, j], sibling).wait_recv()

    return pl.pallas_call(
        body, name=name, in_specs=[_ANY] * n, out_specs=[_ANY] * n,
        out_shape=[jax.ShapeDtypeStruct(f.shape, f.dtype) for f in fulls], input_output_aliases={w: w for w in range(n)},
        scratch_shapes=[pltpu.SemaphoreType.DMA((n, 3)), pltpu.SemaphoreType.DMA((n, 3))],
    )(*fulls)


def gather_shards(shards, *, name):
    n = len(shards)

    def body(*refs):
        ins, outs = refs[:n], refs[n:2 * n]
        ssem, rsem = refs[2 * n:]
        x, y, c, chips = _me()
        mine = _chip_id((x, y))
        sibling = (x, y, 1 - c)

        def rows(w, chip_id, which):
            h = shards[w].shape[0] // 2
            return outs[w].at[chip_id, pl.ds(which * h, h)]

        def to_chip(w, j):
            h = shards[w].shape[0] // 2
            return _rcopy(ins[w].at[pl.ds(c * h, h)], rows(w, mine, c), ssem.at[w, j], rsem.at[w, j], (*chips[j], c))

        def from_chip(w, j):
            r = rows(w, _chip_id(chips[j]), c)
            return _rcopy(r, r, ssem.at[w, j], rsem.at[w, j], (*chips[j], c))

        def to_sibling(w, j):
            r = rows(w, _chip_id(chips[j]), c)
            return _rcopy(r, r, ssem.at[w, 3 + j], rsem.at[w, 3 + j], sibling)

        def from_sibling(w, j):
            r = rows(w, _chip_id(chips[j]), 1 - c)
            return _rcopy(r, r, ssem.at[w, 3 + j], rsem.at[w, 3 + j], sibling)

        for w in range(n):
            for j in range(3):
                to_chip(w, j).start()
        for w in range(n):
            for j in range(3):
                from_chip(w, j).wait_recv()
                to_sibling(w, j).start()
        for w in range(n):
            for j in range(3):
                from_sibling(w, j).wait_recv()
        for w in range(n):
            for j in range(3):
                to_chip(w, j).wait_send()
                to_sibling(w, j).wait_send()

    return pl.pallas_call(
        body, name=name, in_specs=[_ANY] * n, out_specs=[_ANY] * n,
        out_shape=[jax.ShapeDtypeStruct((N_CHIPS,) + s.shape, s.dtype) for s in shards],
        scratch_shapes=[pltpu.SemaphoreType.DMA((n, 6)), pltpu.SemaphoreType.DMA((n, 6))],
    )(*shards)


def _ids_spec(grid, in_specs, out_specs):
    return pltpu.PrefetchScalarGridSpec(num_scalar_prefetch=1, grid=grid, in_specs=in_specs, out_specs=out_specs)


def fill_own(full, shard, ids, *, name):
    r, c = shard.shape
    br = _row_blk(r, 512)

    def body(ids_ref, s_ref, f_in, o_ref):
        del ids_ref, f_in
        o_ref[...] = s_ref[...]

    return pl.pallas_call(
        body, name=name, out_shape=jax.ShapeDtypeStruct(full.shape, full.dtype), input_output_aliases={2: 0},
        grid_spec=_ids_spec((r // br,), [pl.BlockSpec((br, c), lambda i, ids: (i, 0)), _ANY],
                            pl.BlockSpec((None, br, c), lambda i, ids: (ids[0], i, 0))),
        compiler_params=_params(("parallel",)),
    )(ids, shard, full)


def rs_pair(parts, *, name):
    n = len(parts)

    def body(*refs):
        ins, recv = refs[:n], refs[n:2 * n]
        ssem, rsem = refs[2 * n:]
        x, y, c, _ = _me()
        sibling = (x, y, 1 - c)

        def swap(w):
            h = parts[w].shape[1] // 2
            return _rcopy(ins[w].at[:, pl.ds((1 - c) * h, h), :], recv[w], ssem.at[w], rsem.at[w], sibling)

        for w in range(n):
            swap(w).start()
        for w in range(n):
            swap(w).wait()

    return pl.pallas_call(
        body, name=name, in_specs=[_ANY] * n, out_specs=[_ANY] * n,
        out_shape=[jax.ShapeDtypeStruct((p.shape[0], p.shape[1] // 2, p.shape[2]), p.dtype) for p in parts],
        scratch_shapes=[pltpu.SemaphoreType.DMA((n,)), pltpu.SemaphoreType.DMA((n,))],
    )(*parts)


def add_pair(part, recv, ids, *, name):
    p, h, c = recv.shape
    br = _row_blk(h, 512)
    nb = h // br

    def body(ids_ref, a_ref, b_ref, o_ref):
        del ids_ref
        o_ref[...] = (a_ref[...].astype(F32) + b_ref[...].astype(F32)).astype(o_ref.dtype)

    half = pl.BlockSpec((None, br, c), lambda k, i, ids: (k, i, 0))
    return pl.pallas_call(
        body, name=name, out_shape=jax.ShapeDtypeStruct(recv.shape, recv.dtype),
        grid_spec=_ids_spec((p, nb), [pl.BlockSpec((None, br, c), lambda k, i, ids: (k, ids[1] * nb + i, 0)), half], half),
        compiler_params=_params(("parallel", "parallel")),
    )(ids, part, recv)


def rs_chip(sums, *, name):
    n = len(sums)

    def body(*refs):
        ins, outs = refs[:n], refs[n:2 * n]
        ssem, rsem = refs[2 * n:]
        x, y, c, chips = _me()

        def swap(w, j):
            return _rcopy(ins[w].at[_chip_id(chips[j])], outs[w].at[j], ssem.at[w, j], rsem.at[w, j], (*chips[j], c))

        for w in range(n):
            for j in range(3):
                swap(w, j).start()
        for w in range(n):
            for j in range(3):
                swap(w, j).wait()

    return pl.pallas_call(
        body, name=name, in_specs=[_ANY] * n, out_specs=[_ANY] * n,
        out_shape=[jax.ShapeDtypeStruct((3,) + s.shape[1:], s.dtype) for s in sums],
        scratch_shapes=[pltpu.SemaphoreType.DMA((n, 3)), pltpu.SemaphoreType.DMA((n, 3))],
    )(*sums)


def add_chips(sums, landed, ids, group, layer, group_shape, *, name):
    _, h, c = sums.shape
    br = _row_blk(h, 256)
    nb = h // br

    def body(ids_ref, a_ref, b_ref, *rest):
        o_ref = rest[-1]
        tot = a_ref[...].astype(F32)
        for k in range(3):
            tot = tot + b_ref[k].astype(F32)
        o_ref[...] = tot

    in_specs = [pl.BlockSpec((None, br, c), lambda i, ids: (ids[0], i, 0)), pl.BlockSpec((3, br, c), lambda i, ids: (0, i, 0))]
    args = [ids, sums, landed]
    if group is not None:
        in_specs.append(_ANY)
        args.append(group)
    return pl.pallas_call(
        body, name=name, out_shape=jax.ShapeDtypeStruct(group_shape, F32),
        input_output_aliases={3: 0} if group is not None else {},
        grid_spec=_ids_spec((nb,), in_specs, pl.BlockSpec((None, br, c), lambda i, ids: (layer, ids[1] * nb + i, 0))),
        compiler_params=_params(("parallel",)),
    )(*args)


def rs_share(groups, slots, *, name):
    ng = len(groups)
    n = len(slots)

    def body(*refs):
        outs = refs[ng:2 * ng]
        ssem, rsem = refs[2 * ng:]
        x, y, c, _ = _me()
        sibling = (x, y, 1 - c)

        def rows(w, which):
            g, l = slots[w]
            h = groups[g].shape[1] // 2
            return outs[g].at[l, pl.ds(which * h, h), :]

        def swap(w):
            return _rcopy(rows(w, c), rows(w, c), ssem.at[w], rsem.at[w], sibling)

        for w in range(n):
            swap(w).start()
        for w in range(n):
            swap(w).wait_send()
            _rcopy(rows(w, 1 - c), rows(w, 1 - c), ssem.at[w], rsem.at[w], sibling).wait_recv()

    return pl.pallas_call(
        body, name=name, in_specs=[_ANY] * ng, out_specs=[_ANY] * ng,
        out_shape=[jax.ShapeDtypeStruct(g.shape, g.dtype) for g in groups],
        input_output_aliases={g: g for g in range(ng)},
        scratch_shapes=[pltpu.SemaphoreType.DMA((n,)), pltpu.SemaphoreType.DMA((n,))],
    )(*groups)


def allreduce_small(v, *, name):
    r, c = v.shape
    ndev = 2 * N_CHIPS

    def body(v_ref, o_ref, buf, ssem, rsem):
        x, y, cc, _ = _me()
        me = 4 * x + 2 * y + cc
        flips = [(a, b, d) for a in (0, 1) for b in (0, 1) for d in (0, 1)][1:]
        buf[me] = v_ref[...]
        cps = []
        for k, (a, b, d) in enumerate(flips):
            peer = (jnp.bitwise_xor(x, a), jnp.bitwise_xor(y, b), jnp.bitwise_xor(cc, d))
            cp = _rcopy(v_ref, buf.at[me], ssem.at[k], rsem.at[k], peer)
            cp.start()
            cps.append(cp)
        for k, (a, b, d) in enumerate(flips):
            peer = (jnp.bitwise_xor(x, a), jnp.bitwise_xor(y, b), jnp.bitwise_xor(cc, d))
            src = 4 * peer[0] + 2 * peer[1] + peer[2]
            _rcopy(v_ref, buf.at[src], ssem.at[k], rsem.at[k], peer).wait_recv()
        for cp in cps:
            cp.wait_send()
        tot = buf[0]
        for k in range(1, ndev):
            tot = tot + buf[k]
        o_ref[...] = tot

    vm = pl.BlockSpec(memory_space=pltpu.VMEM)
    return pl.pallas_call(
        body, name=name, in_specs=[vm], out_specs=vm, out_shape=jax.ShapeDtypeStruct((r, c), F32),
        scratch_shapes=[pltpu.VMEM((ndev, r, c), F32), pltpu.SemaphoreType.DMA((ndev - 1,)), pltpu.SemaphoreType.DMA((ndev - 1,))],
    )(v)


WEIGHTS = ["lb_table", "ev_norm", "ev_w_in", "ev_w_pool", "ev_pool_scale", "ev_hg_norm", "ev_w_out", "od_norm", "od_w_in",
           "od_b_f", "od_w_out", "xa_norm", "xa_mem_norm", "xa_wq", "xa_wkv", "xa_wo", "ffn_norm", "ffn_w_gate", "ffn_w_up",
           "ffn_w_down", "final_norm"]
BIG = ["ev_w_in", "ev_w_pool", "ev_w_out", "od_w_in", "od_w_out", "xa_wq", "xa_wkv", "xa_wo", "ffn_w_gate", "ffn_w_up", "ffn_w_down"]
SMALL_ROWS = 16


def _rows(parts, width):
    rows = [jnp.pad(p.reshape(-1, p.shape[-1]).astype(F32), ((0, 0), (0, width - p.shape[-1]))) for p in parts]
    out = jnp.concatenate(rows, axis=0)
    return jnp.pad(out, ((0, SMALL_ROWS - out.shape[0]), (0, 0)))


def _unrows(packed, like):
    out, r = [], 0
    for p in like:
        n = p.size // p.shape[-1]
        out.append(packed[r:r + n, :p.shape[-1]].reshape(p.shape))
        r += n
    return out


def _m3(a):
    return a.reshape(a.shape[0], -1, a.shape[-1])


SLOT = {"ev_in": ("ev_w_in", 0), "pool": ("ev_w_pool", 0), "ev_out": ("ev_w_out", 0), "od_in": ("od_w_in", 0),
        "od_out": ("od_w_out", 0)}
for _l in range(2):
    SLOT.update({f"wq{_l}": ("xa_wq", _l), f"wkv{_l}": ("xa_wkv", _l), f"wo{_l}": ("xa_wo", _l),
                 f"gate{_l}": ("ffn_w_gate", _l), f"up{_l}": ("ffn_w_up", _l), f"down{_l}": ("ffn_w_down", _l)})
GATHER_FIRST = ["ev_in", "ev_out", "pool", "od_norm"]
GATHER_CARRIED = [["wq0", "wo0", "wkv0"], ["gate0"], ["up0"], ["down0"], ["od_in"], ["od_out"], ["wq1", "wo1", "wkv1"],
                  ["gate1"], ["up1", "down1"]]


class _Plan:
    def __init__(self, shards, ids, group_shapes, d, nh):
        self.shards, self.ids, self.group_shapes, self.d, self.nh = shards, ids, group_shapes, d, nh
        self.full, self.cache = {}, {}
        self.queue, self.sides = [list(u) for u in GATHER_CARRIED], []
        self.sums, self.rqueue, self.rsides = {}, [], []
        got = gather_shards([shards[n] for n in GATHER_FIRST], name="gather_first")
        for n, f in zip(GATHER_FIRST, got):
            self.full[n] = fill_own(f, shards[n], ids, name=f"gather_own_{n}")

    def take_fwd(self):
        if not self.queue:
            return None
        names = self.queue.pop(0)
        side = GatherSide([self.shards[n] for n in names])
        self.sides.append((names, side))
        return side

    def _need(self, names):
        missing = [n for n in names if n not in self.full]
        if not missing:
            return
        landed = {}
        for ns, side in self.sides:
            if side.outs is not None:
                landed.update(zip(ns, side.outs))
        have = [n for n in missing if n in landed]
        late = [n for n in missing if n not in landed]
        pre = {}
        if have:
            pre.update(zip(have, gather_forward([landed[n] for n in have], name=f"gather_forward_{have[0]}")))
        if late:
            self.queue = [u for u in ([n for n in u if n not in late] for u in self.queue) if u]
            pre.update(zip(late, gather_shards([self.shards[n] for n in late], name=f"gather_late_{late[0]}")))
        for n in missing:
            self.full[n] = fill_own(pre[n], self.shards[n], self.ids, name=f"gather_own_{n}")

    def weights(self, group):
        if group in self.cache:
            return self.cache[group]
        f = self.full
        if group == "ev":
            self._need(["ev_in", "ev_out", "pool"])
            rows, gsz = f["pool"].shape[1:]
            ng = rows * N_CHIPS // gsz
            pool = f["pool"].reshape(N_CHIPS, ng, gsz // N_CHIPS, gsz).transpose(1, 0, 2, 3).reshape(ng, gsz, gsz)
            out = {"ev_in": VM(f["ev_in"], "cs"), "ev_out": VM(f["ev_out"], "rs"), "pool": pool}
        elif group == "od":
            self._need(["od_in", "od_out"])
            od_full = f["od_in"].transpose(1, 0, 2).reshape(self.d, -1)
            wf = jnp.pad(od_full[:, 3 * self.d:], ((0, 0), (0, 128 - self.nh)))
            out = {"wqkv": vm2(od_full[:, :3 * self.d]), "wf": vm2(wf), "od_out": VM(f["od_out"], "rs")}
        elif group.startswith("xa"):
            l = group[2:]
            self._need([f"wq{l}", f"wo{l}", f"wkv{l}"])
            out = {"wq": VM(f[f"wq{l}"], "rs"), "wo": VM(f[f"wo{l}"], "rs"), "wkv": VM(f[f"wkv{l}"], "cs")}
        else:
            l = group[3:]
            self._need([f"gate{l}", f"up{l}", f"down{l}"])
            out = {"gate": VM(f[f"gate{l}"], "cs"), "up": VM(f[f"up{l}"], "cs"), "down": VM(f[f"down{l}"], "rs")}
        self.cache[group] = out
        return out

    def grads_done(self, parts):
        names = list(parts)
        got = rs_pair([parts[n] for n in names], name=f"reduce_pair_{names[0]}")
        for n, g in zip(names, got):
            self.sums[n] = add_pair(parts[n], g, self.ids, name=f"reduce_add2_{n}")
        self.rqueue.append(names)

    def take_bwd(self, units=1):
        names = [n for u in self.rqueue[:units] for n in u]
        self.rqueue = self.rqueue[units:]
        if not names:
            return None
        side = ReduceSide([self.sums[n] for n in names])
        self.rsides.append((names, side))
        return side

    def finish(self):
        landed = {}
        for ns, side in self.rsides:
            landed.update(zip(ns, side.outs))
        rest = [n for u in self.rqueue for n in u]
        if rest:
            landed.update(zip(rest, rs_chip([self.sums[n] for n in rest], name="reduce_chips_rest")))
        gbig = {n: None for n in BIG}
        for n, (big, l) in SLOT.items():
            gbig[big] = add_chips(self.sums[n], landed[n], self.ids, gbig[big], l, self.group_shapes[big], name=f"reduce_add4_{n}")
        full = rs_share([gbig[n] for n in BIG], [(BIG.index(big), l) for big, l in SLOT.values()], name="reduce_share")
        return dict(zip(BIG, full))


def kernel(x, mem, lb_table, ev_norm, ev_w_in, ev_w_pool, ev_pool_scale, ev_hg_norm, ev_w_out, od_norm, od_w_in, od_b_f, od_w_out, xa_norm, xa_mem_norm, xa_wq, xa_wkv, xa_wo, ffn_norm, ffn_w_gate, ffn_w_up, ffn_w_down, final_norm, loss_target, m_lb_table, m_ev_norm, m_ev_w_in, m_ev_w_pool, m_ev_pool_scale, m_ev_hg_norm, m_ev_w_out, m_od_norm, m_od_w_in, m_od_b_f, m_od_w_out, m_xa_norm, m_xa_mem_norm, m_xa_wq, m_xa_wkv, m_xa_wo, m_ffn_norm, m_ffn_w_gate, m_ffn_w_up, m_ffn_w_down, m_final_norm, v_lb_table, v_ev_norm, v_ev_w_in, v_ev_w_pool, v_ev_pool_scale, v_ev_hg_norm, v_ev_w_out, v_od_norm, v_od_w_in, v_od_b_f, v_od_w_out, v_xa_norm, v_xa_mem_norm, v_xa_wq, v_xa_wkv, v_xa_wo, v_ffn_norm, v_ffn_w_gate, v_ffn_w_up, v_ffn_w_down, v_final_norm):
    a = dict(locals())
    w = {n: a[n] for n in WEIGHTS}
    mom = {n: a["m_" + n] for n in WEIGHTS}
    var = {n: a["v_" + n] for n in WEIGHTS}
    _, t, d = x.shape
    nh = d // FOX_HEAD
    lanes = 128
    cx, cy = lax.axis_index("x"), lax.axis_index("y")
    chip = 2 * cx + cy

    w3 = {n: _m3(w[n]) for n in BIG}
    flat = lambda v: v.reshape(-1, v.shape[-1])
    shards = {"od_norm": jnp.broadcast_to(od_norm, (16, od_norm.shape[1]))}
    for name, (big, l) in SLOT.items():
        shards[name] = w3[big][l].astype(BF16)
    ids = jnp.stack([chip, lax.axis_index("c")]).astype(jnp.int32)
    plan = _Plan(shards, ids, {n: w3[n].shape for n in BIG}, d, nh)
    od_norm_full = plan.full["od_norm"][:, 0, :].reshape(1, d)

    sm = jax.nn.softmax(lb_table, axis=0)
    sp = {
        "lb": sm[1:2], "ev_norm": ev_norm, "pool_scale": ev_pool_scale, "hg_gain": ev_hg_norm, "od_norm": od_norm_full,
        "bf": jnp.pad(od_b_f, ((0, 0), (0, lanes - nh))), "xa_norm": xa_norm, "xa_mem_norm": xa_mem_norm, "ffn_norm": ffn_norm,
        "final_norm": final_norm.reshape(1, d),
    }
    loss_l, gx, small = _local_step(x[0], mem[0], loss_target[0], sp, plan)
    loss = lax.psum(loss_l[0, 0], ("x", "y", "c"))
    gbig = plan.finish()

    raw_like = [small["lb"], small["ev_norm"], small["pool_scale"], small["hg_gain"], small["od_norm"], small["bf"],
                small["xa_norm"], small["xa_mem_norm"], small["ffn_norm"], small["final_norm"]]
    summed = _unrows(allreduce_small(_rows(raw_like, d), name="reduce_small"), raw_like)
    dlb, g_ev_norm, g_pool_scale, g_hg, g_od_norm_full, g_bf, g_xa, g_xam, g_ffn, g_final = summed
    dsm = jnp.zeros_like(sm).at[1:2].set(dlb)
    gsmall = {
        "lb_table": sm * (dsm - jnp.sum(sm * dsm, axis=0, keepdims=True)), "ev_norm": g_ev_norm, "ev_pool_scale": g_pool_scale,
        "ev_hg_norm": g_hg, "od_norm": lax.dynamic_slice_in_dim(g_od_norm_full, chip * od_norm.shape[1], od_norm.shape[1], axis=1),
        "od_b_f": g_bf[:, :nh], "xa_norm": g_xa, "xa_mem_norm": g_xam, "ffn_norm": g_ffn, "final_norm": g_final.reshape(d),
    }

    grad, delta, new_m, new_v = {}, {}, {}, {}
    for n in BIG:
        shp = w[n].shape
        res = adamw(flat(w3[n]), flat(gbig[n]), flat(_m3(mom[n])), flat(_m3(var[n])), name=f"adamw_{n}")
        grad[n], delta[n], new_m[n], new_v[n] = [r.reshape(shp) for r in res]
    snames = [n for n in WEIGHTS if n not in BIG]
    like = [w[n] for n in snames]
    res = adamw(_rows(like, d), _rows([gsmall[n] for n in snames], d), _rows([mom[n] for n in snames], d),
                _rows([var[n] for n in snames], d), name="adamw_small")
    for vals, dst in zip(res, (grad, delta, new_m, new_v)):
        dst.update(zip(snames, _unrows(vals, like)))
    return (loss, gx.reshape(x.shape), *[grad[n] for n in WEIGHTS], *[delta[n] for n in WEIGHTS],
            *[new_m[n] for n in WEIGHTS], *[new_v[n] for n in WEIGHTS])
```

```python
import functools
import math

import jax
import jax.numpy as jnp
from jax import lax
from jax.experimental import pallas as pl
from jax.experimental.pallas import tpu as pltpu

F32 = jnp.float32
BF16 = jnp.bfloat16
MESH = pl.DeviceIdType.MESH

V7X_VMEM_LIMIT_BYTES = 56 * 1024 * 1024
N_CHIPS = 4

EPS = 1e-6
POOL_WINDOWS = (2, 4, 8, 16)
POOL_HALO = 16
HG_HEAD = 128
HG_CHUNK = 64
FOX_HEAD = 128
FOX_BLK = 512
XA_HEADS = 4
ADAM_LR, ADAM_B1, ADAM_B2, ADAM_EPS, ADAM_WD, ADAM_STEP = 0.001, 0.9, 0.999, 1e-08, 0.01, 10
EXP_CLAMP = 80.0


def _params(sem=None):
    return pltpu.CompilerParams(dimension_semantics=sem, vmem_limit_bytes=V7X_VMEM_LIMIT_BYTES)


def _blk(pref, dim):
    b = min(pref, dim)
    assert dim % b == 0, (pref, dim)
    return b


class VM:
    def __init__(self, arr, kind="cs", lead=(), inner=(), pfn=None):
        self.arr, self.kind, self.lead, self.inner = arr, kind, tuple(lead), tuple(inner)
        self.pfn = pfn or (lambda p: p)
        p = arr.shape[len(self.lead)]
        r, c = arr.shape[-2:]
        assert arr.ndim == len(self.lead) + 1 + len(self.inner) + 2, (arr.shape, lead, inner)
        self.P = p
        self.shape = (r, c * p) if kind == "cs" else (r * p, c)
        self.dtype = arr.dtype

    def spec(self, br, bc, rfn, cfn):
        p = self.P
        r, c = self.arr.shape[-2:]
        assert c % bc == 0 and r % br == 0, (self.arr.shape, br, bc)
        if p == 1:
            def imap(*g):
                return (*self.lead, self.pfn(0), *self.inner, rfn(*g), cfn(*g))
        elif self.kind == "cs":
            per = c // bc

            def imap(*g):
                cb = cfn(*g)
                return (*self.lead, self.pfn(lax.div(cb, per)), *self.inner, rfn(*g), lax.rem(cb, per))
        else:
            per = r // br

            def imap(*g):
                rb = rfn(*g)
                return (*self.lead, self.pfn(lax.div(rb, per)), *self.inner, lax.rem(rb, per), cfn(*g))
        return pl.BlockSpec((None,) * (self.arr.ndim - 2) + (br, bc), imap)


def vm2(arr):
    return VM(arr.reshape((1,) + arr.shape))


def _out_struct(shape, kind, p, dtype):
    r, c = shape
    return jax.ShapeDtypeStruct((p, r, c // p) if kind == "cs" else (p, r // p, c), dtype)


_ANY = pl.BlockSpec(memory_space=pl.ANY)


def _me():
    x, y, c = lax.axis_index("x"), lax.axis_index("y"), lax.axis_index("c")
    chips = [(1 - x, y), (x, 1 - y), (1 - x, 1 - y)]
    return x, y, c, chips


def _chip_id(xy):
    return 2 * xy[0] + xy[1]


def _rcopy(src, dst, ssem, rsem, dev):
    return pltpu.make_async_remote_copy(src_ref=src, dst_ref=dst, send_sem=ssem, recv_sem=rsem, device_id=dev,
                                        device_id_type=MESH)


class GatherSide:
    def __init__(self, shards):
        self.inputs = list(shards)
        self.out_shape = [jax.ShapeDtypeStruct((N_CHIPS,) + s.shape, s.dtype) for s in shards]
        self.aliases = {}
        self.rows = len(shards)
        self.outs = None

    def _copy(self, ins, outs, ssem, rsem, w, j, receive):
        x, y, c, chips = _me()
        h = self.inputs[w].shape[0] // 2
        half = pl.ds(c * h, h)
        if receive:
            r = outs[w].at[_chip_id(chips[j]), half]
            return _rcopy(r, r, ssem.at[w, j], rsem.at[w, j], (*chips[j], c))
        return _rcopy(ins[w].at[half], outs[w].at[_chip_id((x, y)), half], ssem.at[w, j], rsem.at[w, j], (*chips[j], c))

    def start(self, ins, outs, ssem, rsem):
        for w in range(len(self.inputs)):
            for j in range(3):
                self._copy(ins, outs, ssem, rsem, w, j, False).start()

    def finish(self, ins, outs, ssem, rsem):
        for w in range(len(self.inputs)):
            for j in range(3):
                self._copy(ins, outs, ssem, rsem, w, j, True).wait_recv()
                self._copy(ins, outs, ssem, rsem, w, j, False).wait_send()


class ForwardSide:
    def __init__(self, fulls):
        self.inputs = list(fulls)
        self.out_shape = [jax.ShapeDtypeStruct(f.shape, f.dtype) for f in fulls]
        self.aliases = {w: w for w in range(len(fulls))}
        self.rows = len(fulls)
        self.outs = None

    def _copy(self, outs, ssem, rsem, w, j, receive):
        x, y, c, chips = _me()
        h = self.inputs[w].shape[1] // 2
        r = outs[w].at[_chip_id(chips[j]), pl.ds(((1 - c) if receive else c) * h, h)]
        return _rcopy(r, r, ssem.at[w, j], rsem.at[w, j], (x, y, 1 - c))

    def start(self, ins, outs, ssem, rsem):
        for w in range(self.rows):
            for j in range(3):
                self._copy(outs, ssem, rsem, w, j, False).start()

    def finish(self, ins, outs, ssem, rsem):
        for w in range(self.rows):
            for j in range(3):
                self._copy(outs, ssem, rsem, w, j, False).wait_send()
                self._copy(outs, ssem, rsem, w, j, True).wait_recv()


class PairSide:
    def __init__(self, parts):
        self.inputs = list(parts)
        self.out_shape = [jax.ShapeDtypeStruct((p.shape[0], p.shape[1] // 2, p.shape[2]), p.dtype) for p in parts]
        self.aliases = {}
        self.rows = len(parts)
        self.outs = None

    def _copy(self, ins, outs, ssem, rsem, w):
        x, y, c, _ = _me()
        h = self.inputs[w].shape[1] // 2
        return _rcopy(ins[w].at[:, pl.ds((1 - c) * h, h), :], outs[w], ssem.at[w, 0], rsem.at[w, 0], (x, y, 1 - c))

    def start(self, ins, outs, ssem, rsem):
        for w in range(self.rows):
            self._copy(ins, outs, ssem, rsem, w).start()

    def finish(self, ins, outs, ssem, rsem):
        for w in range(self.rows):
            self._copy(ins, outs, ssem, rsem, w).wait()


class _SemRows:
    def __init__(self, sem, off):
        self.sem, self.off = sem, off

    @property
    def at(self):
        return self

    def __getitem__(self, idx):
        return self.sem.at[self.off + idx[0], idx[1]]


class Sides:
    def __init__(self, sides):
        self.sides = list(sides)
        self.inputs = [a for s in self.sides for a in s.inputs]
        self.out_shape = [o for s in self.sides for o in s.out_shape]
        self.rows = sum(s.rows for s in self.sides)
        self.aliases, i0, o0 = {}, 0, 0
        for s in self.sides:
            self.aliases.update({i0 + i: o0 + o for i, o in s.aliases.items()})
            i0, o0 = i0 + len(s.inputs), o0 + len(s.out_shape)

    def _each(self, method, ins, outs, ssem, rsem):
        i0 = o0 = r0 = 0
        for s in self.sides:
            getattr(s, method)(ins[i0:i0 + len(s.inputs)], outs[o0:o0 + len(s.out_shape)], _SemRows(ssem, r0), _SemRows(rsem, r0))
            i0, o0, r0 = i0 + len(s.inputs), o0 + len(s.out_shape), r0 + s.rows

    def start(self, ins, outs, ssem, rsem):
        self._each("start", ins, outs, ssem, rsem)

    def finish(self, ins, outs, ssem, rsem):
        self._each("finish", ins, outs, ssem, rsem)

    @property
    def outs(self):
        return None

    @outs.setter
    def outs(self, vals):
        o0 = 0
        for s in self.sides:
            s.outs = list(vals[o0:o0 + len(s.out_shape)])
            o0 += len(s.out_shape)


class ReduceSide:
    def __init__(self, sums):
        self.inputs = list(sums)
        self.out_shape = [jax.ShapeDtypeStruct((3,) + s.shape[1:], s.dtype) for s in sums]
        self.aliases = {}
        self.rows = len(sums)
        self.outs = None

    def _copy(self, ins, outs, ssem, rsem, w, j):
        _, _, c, chips = _me()
        return _rcopy(ins[w].at[_chip_id(chips[j])], outs[w].at[j], ssem.at[w, j], rsem.at[w, j], (*chips[j], c))

    def start(self, ins, outs, ssem, rsem):
        for w in range(len(self.inputs)):
            for j in range(3):
                self._copy(ins, outs, ssem, rsem, w, j).start()

    def finish(self, ins, outs, ssem, rsem):
        for w in range(len(self.inputs)):
            for j in range(3):
                self._copy(ins, outs, ssem, rsem, w, j).wait()


def _call(body, side, *, name, grid, in_specs, out_specs, out_shape, scratch_shapes=(), sem, aliases=None, args):
    if side is None:
        return pl.pallas_call(body, name=name, grid=grid, in_specs=in_specs, out_specs=out_specs, out_shape=out_shape,
                              scratch_shapes=list(scratch_shapes), input_output_aliases=aliases or {},
                              compiler_params=_params(sem))(*args)
    single = not isinstance(out_shape, (list, tuple))
    oshape, ospecs = ([out_shape], [out_specs]) if single else (list(out_shape), list(out_specs))
    n_in, n_out, s_in, s_out = len(in_specs), len(oshape), len(side.inputs), len(side.out_shape)

    def wrapped(*refs):
        ins, sin = refs[:n_in], refs[n_in:n_in + s_in]
        outs = refs[n_in + s_in:n_in + s_in + n_out]
        souts = refs[n_in + s_in + n_out:n_in + s_in + n_out + s_out]
        rest = refs[n_in + s_in + n_out + s_out:]
        scratch, (ssem, rsem) = rest[:-2], rest[-2:]
        first = functools.reduce(jnp.logical_and, [pl.program_id(a) == 0 for a in range(len(grid))])
        last = functools.reduce(jnp.logical_and, [pl.program_id(a) == grid[a] - 1 for a in range(len(grid))])

        @pl.when(first)
        def _():
            side.start(sin, souts, ssem, rsem)

        body(*ins, *outs, *scratch)

        @pl.when(last)
        def _():
            side.finish(sin, souts, ssem, rsem)

    sems = pltpu.SemaphoreType.DMA((side.rows, 3))
    res = pl.pallas_call(
        wrapped, name=name, grid=grid, in_specs=list(in_specs) + [_ANY] * s_in, out_specs=ospecs + [_ANY] * s_out,
        out_shape=oshape + side.out_shape, scratch_shapes=list(scratch_shapes) + [sems, sems],
        input_output_aliases={**(aliases or {}), **{n_in + i: n_out + o for i, o in side.aliases.items()}},
        compiler_params=_params(("arbitrary",) * len(grid)),
    )(*args, *side.inputs)
    side.outs = list(res[n_out:])
    return res[0] if single else list(res[:n_out])


def _best(g, cap):
    if g <= cap:
        return g
    cands = [d for d in range(128, cap + 1, 128) if g % d == 0]
    assert cands, (g, cap)
    return cands[-1]


def _row_blk(n, cap):
    cands = [d for d in range(16, min(n, cap) + 1, 16) if n % d == 0]
    assert cands, (n, cap)
    return cands[-1]


def _tiles(a, b, mode, out_kind, out_p, bm, bn, bk):
    def cpiece(v):
        return v.arr.shape[-1] if v.kind == "cs" else v.shape[1]

    def rpiece(v):
        return v.arr.shape[-2] if v.kind == "rs" else v.shape[0]

    if mode == "nn":
        m, n = a.shape[0], b.shape[1]
        gm, gn, gk = rpiece(a), cpiece(b), math.gcd(cpiece(a), rpiece(b))
    elif mode == "nt":
        m, n = a.shape[0], b.shape[0]
        gm, gn, gk = rpiece(a), rpiece(b), math.gcd(cpiece(a), cpiece(b))
    else:
        m, n = a.shape[1], b.shape[1]
        gm, gn, gk = cpiece(a), cpiece(b), math.gcd(rpiece(a), rpiece(b))
    if out_kind == "cs":
        gn = math.gcd(gn, n // out_p)
    else:
        gm = math.gcd(gm, m // out_p)
    caps = {"nn": (512, 1536, 2048), "nt": (512, 2048, 2048), "tn": (1536, 1536, 1024)}[mode]
    return (bm or _best(gm, caps[0])), (bn or _best(gn, caps[1])), (bk or _best(gk, caps[2]))


def matmul(a, b, mode, *, out_dtype, bm=None, bn=None, bk=None, out_kind="cs", out_p=1, out_pfn=None, res=None, epi=None,
           side=None, name):
    bm, bn, bk = _tiles(a, b, mode, out_kind, out_p, bm, bn, bk)
    if mode == "nn":
        (m, k), (k2, n) = a.shape, b.shape
        a_spec = a.spec(bm, bk, lambda i, j, kk: i, lambda i, j, kk: kk)
        b_spec = b.spec(bk, bn, lambda i, j, kk: kk, lambda i, j, kk: j)
        dims = (((1,), (0,)), ((), ()))
    elif mode == "nt":
        (m, k), (n, k2) = a.shape, b.shape
        a_spec = a.spec(bm, bk, lambda i, j, kk: i, lambda i, j, kk: kk)
        b_spec = b.spec(bn, bk, lambda i, j, kk: j, lambda i, j, kk: kk)
        dims = (((1,), (1,)), ((), ()))
    else:
        (k, m), (k2, n) = a.shape, b.shape
        a_spec = a.spec(bk, bm, lambda i, j, kk: kk, lambda i, j, kk: i)
        b_spec = b.spec(bk, bn, lambda i, j, kk: kk, lambda i, j, kk: j)
        dims = (((0,), (0,)), ((), ()))
    assert k == k2, (a.shape, b.shape, mode)
    assert m % bm == 0 and n % bn == 0 and k % bk == 0, (m, n, k, bm, bn, bk)
    nk = k // bk
    out_sds = _out_struct((m, n), out_kind, out_p, out_dtype)
    out_vm = VM(out_sds, out_kind, pfn=out_pfn)
    o_spec = out_vm.spec(bm, bn, lambda i, j, kk: i, lambda i, j, kk: j)
    in_specs, args = [a_spec, b_spec], [a.arr, b.arr]
    tiles = ([res] if res is not None else []) + (list(epi[1]) if epi else [])
    for v in tiles:
        assert v.shape == (m, n)
        in_specs.append(v.spec(bm, bn, lambda i, j, kk: i, lambda i, j, kk: j))
        args.append(v.arr)
    n_out = epi[2] if epi else 1

    def body(a_ref, b_ref, *rest):
        t_refs, o_refs = rest[:len(tiles)], rest[len(tiles):len(tiles) + n_out]
        part = lax.dot_general(a_ref[...], b_ref[...], dims, preferred_element_type=F32)

        def write(tot):
            if res is not None:
                tot = tot + t_refs[0][...].astype(F32)
            outs = epi[0](tot, *[r[...].astype(F32) for r in t_refs[len(tiles) - len(epi[1]):]]) if epi else (tot,)
            for o_ref, val in zip(o_refs, outs):
                o_ref[...] = val.astype(o_ref.dtype)

        if nk == 1:
            write(part)
            return
        acc = rest[-1]
        kk = pl.program_id(2)

        @pl.when(kk == 0)
        def _():
            acc[...] = part

        @pl.when(kk > 0)
        def _():
            acc[...] += part

        @pl.when(kk == nk - 1)
        def _():
            write(acc[...])

    return _call(body, side, name=name, grid=(m // bm, n // bn, nk), in_specs=in_specs,
                 out_specs=o_spec if n_out == 1 else [o_spec] * n_out, out_shape=out_sds if n_out == 1 else [out_sds] * n_out,
                 scratch_shapes=[pltpu.VMEM((bm, bn), F32)] if nk > 1 else [],
                 sem=("parallel", "parallel", "arbitrary"), args=args)


def rmsnorm_fwd(x, g, *, name):
    t, d = x.shape
    bt = _blk(512, t)

    def body(x_ref, g_ref, o_ref):
        xv = x_ref[...]
        r = lax.rsqrt(jnp.mean(xv * xv, axis=-1, keepdims=True) + EPS)
        o_ref[...] = (xv * r * g_ref[...]).astype(o_ref.dtype)

    return pl.pallas_call(
        body, name=name, grid=(t // bt,),
        in_specs=[pl.BlockSpec((bt, d), lambda i: (i, 0)), pl.BlockSpec((1, d), lambda i: (0, 0))],
        out_specs=pl.BlockSpec((bt, d), lambda i: (i, 0)), out_shape=jax.ShapeDtypeStruct((t, d), BF16),
        compiler_params=_params(("parallel",)),
    )(x, g)


def rmsnorm_bwd(x, g, dh, dres, *, name):
    t, d = x.shape
    bt = _blk(256, t)
    want_dx = dres is not None

    def body(x_ref, g_ref, dh_ref, *rest):
        if want_dx:
            dres_ref, dx_ref, dxb_ref, dg_ref = rest
        else:
            (dg_ref,) = rest
        xv = x_ref[...]
        dhv = dh_ref[...].astype(F32)
        r = lax.rsqrt(jnp.mean(xv * xv, axis=-1, keepdims=True) + EPS)
        xh = xv * r
        part = jnp.sum(dhv * xh, axis=0, keepdims=True)

        @pl.when(pl.program_id(0) == 0)
        def _():
            dg_ref[...] = part

        @pl.when(pl.program_id(0) > 0)
        def _():
            dg_ref[...] += part

        if want_dx:
            dy = dhv * g_ref[...]
            dxn = r * (dy - xh * jnp.mean(dy * xh, axis=-1, keepdims=True))
            dx = dres_ref[...] + dxn
            dx_ref[...] = dx
            dxb_ref[...] = dx.astype(BF16)

    row = pl.BlockSpec((bt, d), lambda i: (i, 0))
    vec = pl.BlockSpec((1, d), lambda i: (0, 0))
    in_specs, args = [row, vec, row], [x, g, dh]
    out_specs, out_shape = [vec], [jax.ShapeDtypeStruct((1, d), F32)]
    if want_dx:
        in_specs.append(row)
        args.append(dres)
        out_specs = [row, row] + out_specs
        out_shape = [jax.ShapeDtypeStruct((t, d), F32), jax.ShapeDtypeStruct((t, d), BF16)] + out_shape
    return pl.pallas_call(
        body, name=name, grid=(t // bt,), in_specs=in_specs, out_specs=out_specs, out_shape=out_shape,
        compiler_params=_params(("arbitrary",)),
    )(*args)


def loss_head(x, g, tgt, *, name):
    t, d = x.shape
    bt = _blk(256, t)

    def body(x_ref, g_ref, t_ref, loss_ref, dx_ref, dxb_ref, dg_ref):
        xv = x_ref[...]
        gv = g_ref[...]
        r = lax.rsqrt(jnp.mean(xv * xv, axis=-1, keepdims=True) + EPS)
        xh = xv * r
        e = xh * gv - t_ref[...]
        lpart = jnp.zeros((1, 128), F32) + jnp.sum(e * e) * (0.5 / d)
        dyv = e * (1.0 / d)
        gpart = jnp.sum(dyv * xh, axis=0, keepdims=True)

        @pl.when(pl.program_id(0) == 0)
        def _():
            loss_ref[...] = lpart
            dg_ref[...] = gpart

        @pl.when(pl.program_id(0) > 0)
        def _():
            loss_ref[...] += lpart
            dg_ref[...] += gpart

        dy = dyv * gv
        dx = r * (dy - xh * jnp.mean(dy * xh, axis=-1, keepdims=True))
        dx_ref[...] = dx
        dxb_ref[...] = dx.astype(BF16)

    row = pl.BlockSpec((bt, d), lambda i: (i, 0))
    vec = pl.BlockSpec((1, d), lambda i: (0, 0))
    return pl.pallas_call(
        body, name=name, grid=(t // bt,), in_specs=[row, vec, row],
        out_specs=[pl.BlockSpec((1, 128), lambda i: (0, 0)), row, row, vec],
        out_shape=[jax.ShapeDtypeStruct((1, 128), F32), jax.ShapeDtypeStruct((t, d), F32),
                   jax.ShapeDtypeStruct((t, d), BF16), jax.ShapeDtypeStruct((1, d), F32)],
        compiler_params=_params(("arbitrary",)),
    )(x, g, tgt)


def _sigmoid(x):
    return 1.0 / (1.0 + jnp.exp(-x))


def _swiglu_epi(b, a):
    return b, a * _sigmoid(a) * b


def _swiglu_bwd_epi(ds, a, b):
    sg = _sigmoid(a)
    return ds * b * sg * (1.0 + a * (1.0 - sg)), ds * a * sg


def _xa_probs(qh, kh, scale):
    s = lax.dot_general(qh, kh, (((1,), (1,)), ((), ())), preferred_element_type=F32) * scale
    s = s - jnp.max(s, axis=-1, keepdims=True)
    p = jnp.exp(s)
    return p / jnp.sum(p, axis=-1, keepdims=True)


def xattn_fwd(q, kv, *, name):
    t, d = q.shape
    m = kv.shape[0]
    hd = d // XA_HEADS
    bt = _blk(512, t)
    scale = hd ** -0.5

    def body(q_ref, kv_ref, o_ref):
        for h in range(XA_HEADS):
            qh = q_ref[:, h * hd:(h + 1) * hd]
            kh = kv_ref[:, h * hd:(h + 1) * hd]
            vh = kv_ref[:, d + h * hd:d + (h + 1) * hd]
            p = _xa_probs(qh, kh, scale)
            o_ref[:, h * hd:(h + 1) * hd] = jnp.dot(p.astype(BF16), vh, preferred_element_type=F32).astype(BF16)

    return pl.pallas_call(
        body, name=name, grid=(t // bt,),
        in_specs=[pl.BlockSpec((bt, d), lambda i: (i, 0)), pl.BlockSpec((m, 2 * d), lambda i: (0, 0))],
        out_specs=pl.BlockSpec((bt, d), lambda i: (i, 0)), out_shape=jax.ShapeDtypeStruct((t, d), BF16),
        compiler_params=_params(("parallel",)),
    )(q, kv)


def xattn_bwd(q, kv, do, *, name):
    t, d = q.shape
    m = kv.shape[0]
    hd = d // XA_HEADS
    bt = _blk(512, t)
    scale = hd ** -0.5

    def body(q_ref, kv_ref, do_ref, dq_ref, dkv_ref):
        first = pl.program_id(0) == 0
        for h in range(XA_HEADS):
            qs, ks, vs = slice(h * hd, (h + 1) * hd), slice(h * hd, (h + 1) * hd), slice(d + h * hd, d + (h + 1) * hd)
            qh, kh, vh, doh = q_ref[:, qs], kv_ref[:, ks], kv_ref[:, vs], do_ref[:, qs]
            p = _xa_probs(qh, kh, scale)
            dp = lax.dot_general(doh, vh, (((1,), (1,)), ((), ())), preferred_element_type=F32)
            dsv = p * (dp - jnp.sum(p * dp, axis=-1, keepdims=True)) * scale
            dsb = dsv.astype(BF16)
            dq_ref[:, qs] = jnp.dot(dsb, kh, preferred_element_type=F32).astype(BF16)
            dk = lax.dot_general(dsb, qh, (((0,), (0,)), ((), ())), preferred_element_type=F32)
            dv = lax.dot_general(p.astype(BF16), doh, (((0,), (0,)), ((), ())), preferred_element_type=F32)

            @pl.when(first)
            def _():
                dkv_ref[:, ks] = dk
                dkv_ref[:, vs] = dv

            @pl.when(jnp.logical_not(first))
            def _():
                dkv_ref[:, ks] += dk
                dkv_ref[:, vs] += dv

    row = pl.BlockSpec((bt, d), lambda i: (i, 0))
    full = pl.BlockSpec((m, 2 * d), lambda i: (0, 0))
    return pl.pallas_call(
        body, name=name, grid=(t // bt,), in_specs=[row, full, row], out_specs=[row, full],
        out_shape=[jax.ShapeDtypeStruct((t, d), BF16), jax.ShapeDtypeStruct((m, 2 * d), F32)],
        compiler_params=_params(("arbitrary",)),
    )(q, kv, do)


def _pool_p(buf, uv, rows, w, bt):
    acc = uv
    for dd in range(1, w):
        acc = acc + buf[pl.ds(POOL_HALO - dd, bt), :]
    cnt = jnp.minimum(rows + 1, w).astype(F32)
    return acc / cnt - uv


def pool_fwd(z, w_pool, scale, *, name):
    t = z.shape[0]
    ng, gsz = w_pool.shape[0], w_pool.shape[1]
    mix = ng * gsz
    bt = _blk(512, t)

    def body(u_ref, uh_ref, w_ref, sc_ref, o_ref, buf):
        r = pl.program_id(0)
        rows = r * bt + lax.broadcasted_iota(jnp.int32, (bt, 1), 0)
        for g in range(ng):
            gs = slice(g * gsz, (g + 1) * gsz)
            uv = u_ref[:, gs]
            buf[0:POOL_HALO, :] = jnp.where(r > 0, uh_ref[:, gs], 0.0)
            buf[POOL_HALO:POOL_HALO + bt, :] = uv
            p = _pool_p(buf, uv, rows, POOL_WINDOWS[g], bt)
            y = jnp.dot(p.astype(BF16), w_ref[g], preferred_element_type=F32) * sc_ref[:, gs]
            o_ref[:, gs] = y.astype(BF16)

    hb = bt // POOL_HALO
    return pl.pallas_call(
        body, name=name, grid=(t // bt,),
        in_specs=[pl.BlockSpec((bt, mix), lambda i: (i, 0)),
                  pl.BlockSpec((POOL_HALO, mix), lambda i: (jnp.maximum(i * hb - 1, 0), 0)),
                  pl.BlockSpec((ng, gsz, gsz), lambda i: (0, 0, 0)), pl.BlockSpec((1, mix), lambda i: (0, 0))],
        out_specs=pl.BlockSpec((None, bt, mix), lambda i: (0, i, 0)),
        out_shape=jax.ShapeDtypeStruct((2, t, mix), BF16),
        scratch_shapes=[pltpu.VMEM((POOL_HALO + bt, gsz), F32)],
        compiler_params=_params(("parallel",)),
    )(z, z, w_pool, scale)


def pool_bwd(z, dcat, w_pool, scale, *, name):
    t = z.shape[0]
    ng, gsz = w_pool.shape[0], w_pool.shape[1]
    mix = ng * gsz
    bt = _blk(512, t)
    nb = t // bt
    nt_dims = (((1,), (1,)), ((), ()))
    tn_dims = (((0,), (0,)), ((), ()))

    def body(u_ref, uh_ref, dy_ref, dyh_ref, w_ref, sc_ref, du_ref, dw_ref, dsc_ref, buf, buf2):
        r = pl.program_id(0)
        first = r == 0
        rows = r * bt + lax.broadcasted_iota(jnp.int32, (bt, 1), 0)
        rows_h = (r + 1) * bt + lax.broadcasted_iota(jnp.int32, (POOL_HALO, 1), 0)
        for g in range(ng):
            w = POOL_WINDOWS[g]
            gs = slice(g * gsz, (g + 1) * gsz)
            uv = u_ref[:, gs]
            buf[0:POOL_HALO, :] = jnp.where(r > 0, uh_ref[:, gs], 0.0)
            buf[POOL_HALO:POOL_HALO + bt, :] = uv
            pb = _pool_p(buf, uv, rows, w, bt).astype(BF16)
            wg = w_ref[g]
            sc = sc_ref[:, gs]
            y0 = jnp.dot(pb, wg, preferred_element_type=F32)
            dyv = dy_ref[:, gs].astype(F32)
            dsc = jnp.sum(dyv * y0, axis=0, keepdims=True)
            dyw = (dyv * sc).astype(BF16)
            dw = lax.dot_general(pb, dyw, tn_dims, preferred_element_type=F32)

            @pl.when(first)
            def _():
                dw_ref[g] = dw
                dsc_ref[:, gs] = dsc

            @pl.when(jnp.logical_not(first))
            def _():
                dw_ref[g] += dw
                dsc_ref[:, gs] += dsc

            dp = lax.dot_general(dyw, wg, nt_dims, preferred_element_type=F32)
            dyh = (dyh_ref[:, gs].astype(F32) * sc).astype(BF16)
            dph = lax.dot_general(dyh, wg, nt_dims, preferred_element_type=F32)
            dph = jnp.where(r < nb - 1, dph, 0.0)
            buf2[0:bt, :] = dp / jnp.minimum(rows + 1, w).astype(F32)
            buf2[bt:bt + POOL_HALO, :] = dph / jnp.minimum(rows_h + 1, w).astype(F32)
            acc = buf2[pl.ds(0, bt), :]
            for dd in range(1, w):
                acc = acc + buf2[pl.ds(dd, bt), :]
            du_ref[:, gs] = (acc - dp).astype(BF16)

    hb = bt // POOL_HALO
    nhb = t // POOL_HALO
    return pl.pallas_call(
        body, name=name, grid=(nb,),
        in_specs=[pl.BlockSpec((bt, mix), lambda i: (i, 0)),
                  pl.BlockSpec((POOL_HALO, mix), lambda i: (jnp.maximum(i * hb - 1, 0), 0)),
                  pl.BlockSpec((None, bt, mix), lambda i: (0, i, 0)),
                  pl.BlockSpec((None, POOL_HALO, mix), lambda i: (0, jnp.minimum((i + 1) * hb, nhb - 1), 0)),
                  pl.BlockSpec((ng, gsz, gsz), lambda i: (0, 0, 0)), pl.BlockSpec((1, mix), lambda i: (0, 0))],
        out_specs=[pl.BlockSpec((None, bt, mix), lambda i: (4, i, 0)),
                   pl.BlockSpec((ng, gsz, gsz), lambda i: (0, 0, 0)), pl.BlockSpec((1, mix), lambda i: (0, 0))],
        out_shape=[jax.ShapeDtypeStruct((5, t, mix), BF16), jax.ShapeDtypeStruct((ng, gsz, gsz), F32),
                   jax.ShapeDtypeStruct((1, mix), F32)],
        scratch_shapes=[pltpu.VMEM((POOL_HALO + bt, gsz), F32), pltpu.VMEM((bt + POOL_HALO, gsz), F32)],
        compiler_params=_params(("arbitrary",)),
    )(z, z, dcat, dcat, w_pool, scale)


HG_LEVELS = ((64, 31), (32, 15), (16, 7))
HG_DIAG = (8, 3)
_NT = (((1,), (1,)), ((), ()))
_TN = (((0,), (0,)), ((), ()))
_HI = lax.Precision.HIGHEST


def _hg_masks():
    c = HG_CHUNK
    t = lax.broadcasted_iota(jnp.int32, (c, c), 0)
    s = lax.broadcasted_iota(jnp.int32, (c, c), 1)
    masks = []
    for blk, row in HG_LEVELS:
        sh = blk.bit_length() - 1
        same = (t >> sh) == (s >> sh)
        masks.append(same & ((t & (blk - 1)) > row) & ((s & (blk - 1)) <= row))
    sh = HG_DIAG[0].bit_length() - 1
    masks.append(((t >> sh) == (s >> sh)) & (s <= t))
    return t, s, masks


def _row_of_block(x, blk, row):
    c, n = x.shape
    x3 = x.reshape(c // blk, blk, n)
    return jnp.broadcast_to(x3[:, row:row + 1, :], x3.shape).reshape(c, n)


def _hg_parts(qv, flv, lb, masks, tri):
    sgf = _sigmoid(flv)
    f = lb + (1.0 - lb) * sgf
    logf = jnp.log(f)
    kk = 1.0 - f
    sgq = _sigmoid(qv)
    qf = qv * sgq * (HG_HEAD ** -0.5)
    bc = jnp.dot(tri, logf, preferred_element_type=F32, precision=_HI)
    levels = []
    a = None
    for li, (blk, row) in enumerate(HG_LEVELS + (HG_DIAG,)):
        e = bc - _row_of_block(bc, blk, row)
        if li < len(HG_LEVELS):
            eq, ek = jnp.exp(jnp.minimum(e, 0.0)), jnp.exp(jnp.minimum(-e, 0.0))
        else:
            eq, ek = jnp.exp(jnp.clip(e, -EXP_CLAMP, EXP_CLAMP)), jnp.exp(jnp.clip(-e, -EXP_CLAMP, EXP_CLAMP))
        qt, kt = qf * eq, kk * ek
        part = jnp.where(masks[li], lax.dot_general(qt.astype(BF16), kt.astype(BF16), _NT, preferred_element_type=F32), 0.0)
        a = part if a is None else a + part
        levels.append((eq, ek, qt, kt))
    return dict(sgf=sgf, f=f, kk=kk, sgq=sgq, qf=qf, bc=bc, levels=levels, a=a)


def hgrn_fwd(z, cat, lb, gain, mix_a, *, side=None, name):
    t = z.shape[0]
    mix_b = lb.shape[1]
    nh = mix_b // HG_HEAD
    bt = _blk(256, t)
    ncb = bt // HG_CHUNK
    dh = HG_HEAD

    def body(q_ref, fl_ref, i_ref, g_ref, lb_ref, gain_ref, cat_in, o_ref, st_ref, st):
        del cat_in

        @pl.when(pl.program_id(1) == 0)
        def _():
            st[...] = jnp.zeros_like(st)

        t_i, s_i, masks = _hg_masks()
        tri = (s_i <= t_i).astype(F32)
        lbv, gn = lb_ref[...], gain_ref[...]
        for c in range(ncb):
            rs = slice(c * HG_CHUNK, (c + 1) * HG_CHUNK)
            pr = _hg_parts(q_ref[rs, :], fl_ref[rs, :], lbv, masks, tri)
            vb = i_ref[rs, :].astype(BF16)
            stv = st[...]
            st_ref[c] = stv
            bc = pr["bc"]
            qt = pr["qf"] * jnp.exp(bc)
            o = (jnp.dot(pr["a"].astype(BF16), vb, preferred_element_type=F32)
                 + lax.dot_general(qt.astype(BF16), stv.astype(BF16), _NT, preferred_element_type=F32))
            bl = bc[HG_CHUNK - 1:HG_CHUNK, :]
            khat = pr["kk"] * jnp.exp(bl - bc)
            st[...] = stv * jnp.exp(bl) + lax.dot_general(vb, khat.astype(BF16), _TN, preferred_element_type=F32)
            r = lax.rsqrt(jnp.mean(o * o, axis=-1, keepdims=True) + EPS)
            gv = g_ref[rs, :]
            o_ref[rs, :] = (o * r * gn * (gv * _sigmoid(gv))).astype(BF16)

    def col(which):
        base = (mix_a + which * mix_b) // dh
        return pl.BlockSpec((bt, dh), lambda h, i: (i, base + h))

    return _call(
        body, side, name=name, grid=(nh, t // bt),
        in_specs=[col(0), col(1), col(2), col(3), pl.BlockSpec((1, dh), lambda h, i: (0, h)),
                  pl.BlockSpec((1, dh), lambda h, i: (0, 0)), _ANY],
        out_specs=[pl.BlockSpec((None, bt, dh), lambda h, i: (1, i, h)),
                   pl.BlockSpec((None, ncb, dh, dh), lambda h, i: (h, i, 0, 0))],
        out_shape=[jax.ShapeDtypeStruct(cat.shape, BF16), jax.ShapeDtypeStruct((nh, t // HG_CHUNK, dh, dh), F32)],
        scratch_shapes=[pltpu.VMEM((dh, dh), F32)], aliases={6: 0}, sem=("parallel", "arbitrary"),
        args=(z, z, z, z, lb, gain, cat))


def hgrn_bwd(z, dcat, dz5, states, lb, gain, mix_a, *, side=None, name):
    t = z.shape[0]
    mix_b = lb.shape[1]
    nh = mix_b // HG_HEAD
    bt = _blk(256, t)
    nb = t // bt
    ncb = bt // HG_CHUNK
    dh = HG_HEAD

    def body(q_ref, fl_ref, i_ref, g_ref, dy_ref, st_ref, lb_ref, gain_ref, dz_in, dz_ref, dlb_ref, dgn_ref, dst):
        del dz_in
        first = pl.program_id(1) == 0

        @pl.when(first)
        def _():
            dst[...] = jnp.zeros_like(dst)

        t_i, s_i, masks = _hg_masks()
        tri = (s_i <= t_i).astype(F32)
        triu = (s_i >= t_i).astype(F32)
        last_row = lax.broadcasted_iota(jnp.int32, (HG_CHUNK, 1), 0) == HG_CHUNK - 1
        lbv, gn = lb_ref[...], gain_ref[...]
        dlb_acc = jnp.zeros((1, dh), F32)
        dgn_acc = jnp.zeros((1, dh), F32)
        for c in reversed(range(ncb)):
            rs = slice(c * HG_CHUNK, (c + 1) * HG_CHUNK)
            qv, flv, gv = q_ref[rs, :], fl_ref[rs, :], g_ref[rs, :]
            pr = _hg_parts(qv, flv, lbv, masks, tri)
            vb = i_ref[rs, :].astype(BF16)
            stv = st_ref[c]
            stb = stv.astype(BF16)
            dsv = dst[...]
            dsb = dsv.astype(BF16)
            bc, kk, qf, ab = pr["bc"], pr["kk"], pr["qf"], pr["a"].astype(BF16)
            ebc = jnp.exp(bc)
            qt = qf * ebc
            qtb = qt.astype(BF16)
            o = jnp.dot(ab, vb, preferred_element_type=F32) + lax.dot_general(qtb, stb, _NT, preferred_element_type=F32)
            r = lax.rsqrt(jnp.mean(o * o, axis=-1, keepdims=True) + EPS)
            oh = o * r
            sgg = _sigmoid(gv)
            dyv = dy_ref[rs, :].astype(F32)
            don = dyv * (gv * sgg)
            dgate = dyv * (oh * gn) * (sgg * (1.0 + gv * (1.0 - sgg)))
            dgn_acc = dgn_acc + jnp.sum(don * oh, axis=0, keepdims=True)
            doh = don * gn
            do = r * (doh - oh * jnp.mean(doh * oh, axis=-1, keepdims=True))
            dob = do.astype(BF16)
            bl = bc[HG_CHUNK - 1:HG_CHUNK, :]
            ebl = jnp.exp(bl)
            ekh = jnp.exp(bl - bc)
            khat = kk * ekh
            dv = (lax.dot_general(ab, dob, _TN, preferred_element_type=F32)
                  + lax.dot_general(khat.astype(BF16), dsb, _NT, preferred_element_type=F32))
            da = lax.dot_general(dob, vb, _NT, preferred_element_type=F32)
            dqt = jnp.dot(dob, stb, preferred_element_type=F32)
            dkh = jnp.dot(vb, dsb, preferred_element_type=F32)
            dst[...] = dsv * ebl + lax.dot_general(dob, qtb, _TN, preferred_element_type=F32)
            dbl = jnp.sum(dsv * stv, axis=0, keepdims=True) * ebl + jnp.sum(dkh * khat, axis=0, keepdims=True)
            dqf = dqt * ebc
            dkk = dkh * ekh
            dbc = dqt * qt - dkh * khat
            for li, (eq, ek, qtl, ktl) in enumerate(pr["levels"]):
                gm = jnp.where(masks[li], da, 0.0).astype(BF16)
                qtr, ktr = qtl.astype(BF16), ktl.astype(BF16)
                dql = jnp.dot(gm, ktr, preferred_element_type=F32)
                dkl = lax.dot_general(gm, qtr, _TN, preferred_element_type=F32)
                dqf = dqf + dql * eq
                dkk = dkk + dkl * ek
                dbc = dbc + qtr.astype(F32) * dql - ktr.astype(F32) * dkl
            dbc = dbc + jnp.where(last_row, dbl, 0.0)
            dlogf = jnp.dot(triu, dbc, preferred_element_type=F32, precision=_HI)
            df = dlogf / pr["f"] - dkk
            sgf = pr["sgf"]
            dfl = df * (1.0 - lbv) * sgf * (1.0 - sgf)
            dlb_acc = dlb_acc + jnp.sum(df * (1.0 - sgf), axis=0, keepdims=True)
            sgq = pr["sgq"]
            dq = dqf * (HG_HEAD ** -0.5) * (sgq * (1.0 + qv * (1.0 - sgq)))
            dz_ref[0, rs, :] = dq.astype(BF16)
            dz_ref[1, rs, :] = dfl.astype(BF16)
            dz_ref[2, rs, :] = dv.astype(BF16)
            dz_ref[3, rs, :] = dgate.astype(BF16)

        @pl.when(first)
        def _():
            dlb_ref[...] = dlb_acc
            dgn_ref[...] = dgn_acc

        @pl.when(jnp.logical_not(first))
        def _():
            dlb_ref[...] += dlb_acc
            dgn_ref[...] += dgn_acc

    def col(which):
        base = (mix_a + which * mix_b) // dh
        return pl.BlockSpec((bt, dh), lambda h, i: (nb - 1 - i, base + h))

    return _call(
        body, side, name=name, grid=(nh, nb),
        in_specs=[col(0), col(1), col(2), col(3),
                  pl.BlockSpec((None, bt, dh), lambda h, i: (1, nb - 1 - i, h)),
                  pl.BlockSpec((None, ncb, dh, dh), lambda h, i: (h, nb - 1 - i, 0, 0)),
                  pl.BlockSpec((1, dh), lambda h, i: (0, h)), pl.BlockSpec((1, dh), lambda h, i: (0, 0)), _ANY],
        out_specs=[pl.BlockSpec((4, bt, dh), lambda h, i: (0, nb - 1 - i, h)),
                   pl.BlockSpec((1, dh), lambda h, i: (0, h)),
                   pl.BlockSpec((None, 1, dh), lambda h, i: (h, 0, 0))],
        out_shape=[jax.ShapeDtypeStruct(dz5.shape, BF16), jax.ShapeDtypeStruct((1, mix_b), F32),
                   jax.ShapeDtypeStruct((nh, 1, dh), F32)],
        scratch_shapes=[pltpu.VMEM((dh, dh), F32)], aliases={8: 0}, sem=("parallel", "arbitrary"),
        args=(z, z, z, z, dcat, states, lb, gain, dz5))


def _fox_scores(qb, kb, fk, scale, masked):
    s = lax.dot_general(qb, kb, _NT, preferred_element_type=F32) * scale - fk
    if masked:
        n = s.shape[0]
        row = lax.broadcasted_iota(jnp.int32, (n, n), 0)
        col = lax.broadcasted_iota(jnp.int32, (n, n), 1)
        s = jnp.where(col <= row, s, -jnp.inf)
    return s


def fox_fwd(qkv, fk, *, side=None, name):
    _, t, d = qkv.shape
    nh = d // FOX_HEAD
    b = _blk(FOX_BLK, t)
    nb = t // b
    dh = FOX_HEAD
    scale = dh ** -0.5

    def body(q_ref, k_ref, v_ref, f_ref, o_ref, lse_ref):
        qi = pl.program_id(1)
        qb = q_ref[...]

        def step(kj, carry, masked):
            m, l, acc = carry
            off = pl.multiple_of(kj * b, b)
            s = _fox_scores(qb, k_ref[pl.ds(off, b), :], f_ref[kj], scale, masked)
            m_new = jnp.maximum(m, jnp.max(s, axis=-1, keepdims=True))
            alpha = jnp.exp(m - m_new)
            p = jnp.exp(s - m_new)
            l = alpha * l + jnp.sum(p, axis=-1, keepdims=True)
            acc = alpha * acc + jnp.dot(p.astype(BF16), v_ref[pl.ds(off, b), :], preferred_element_type=F32)
            return m_new, l, acc

        init = (jnp.full((b, 1), -jnp.inf, F32), jnp.zeros((b, 1), F32), jnp.zeros((b, dh), F32))
        carry = lax.fori_loop(0, qi, lambda kj, c: step(kj, c, False), init)
        m, l, acc = step(qi, carry, True)
        o_ref[...] = (acc / l).astype(BF16)
        lse_ref[...] = m + jnp.log(l)

    return _call(
        body, side, name=name, grid=(nh, nb),
        in_specs=[pl.BlockSpec((None, b, dh), lambda h, i: (0, i, h)),
                  pl.BlockSpec((None, t, dh), lambda h, i: (1, 0, h)),
                  pl.BlockSpec((None, t, dh), lambda h, i: (2, 0, h)),
                  pl.BlockSpec((None, nb, 1, b), lambda h, i: (h, 0, 0, 0))],
        out_specs=[pl.BlockSpec((b, dh), lambda h, i: (i, h)), pl.BlockSpec((None, b, 1), lambda h, i: (h, i, 0))],
        out_shape=[jax.ShapeDtypeStruct((t, d), BF16), jax.ShapeDtypeStruct((nh, t, 1), F32)],
        sem=("parallel", "parallel"), args=(qkv, qkv, qkv, fk))


def fox_bwd_dq(qkv, fk, do, lse, *, side=None, name):
    _, t, d = qkv.shape
    nh = d // FOX_HEAD
    b = _blk(FOX_BLK, t)
    nb = t // b
    dh = FOX_HEAD
    scale = dh ** -0.5

    def body(q_ref, k_ref, v_ref, f_ref, do_ref, lse_ref, dq_ref, dl_ref, p_buf, dp_buf):
        qi = pl.program_id(1)
        qb, dob, lse_v = q_ref[...], do_ref[...], lse_ref[...]

        def first(kj, dl, masked):
            off = pl.multiple_of(kj * b, b)
            p = jnp.exp(_fox_scores(qb, k_ref[pl.ds(off, b), :], f_ref[kj], scale, masked) - lse_v)
            dp = lax.dot_general(dob, v_ref[pl.ds(off, b), :], _NT, preferred_element_type=F32)
            p_buf[kj] = p
            dp_buf[kj] = dp
            return dl + jnp.sum(p * dp, axis=-1, keepdims=True)

        dl = lax.fori_loop(0, qi, lambda kj, c: first(kj, c, False), jnp.zeros((b, 1), F32))
        dl = first(qi, dl, True)
        dl_ref[...] = dl

        def second(kj, dq):
            off = pl.multiple_of(kj * b, b)
            dsv = p_buf[kj] * (dp_buf[kj] - dl)
            return dq + jnp.dot(dsv.astype(BF16), k_ref[pl.ds(off, b), :], preferred_element_type=F32)

        dq = lax.fori_loop(0, qi + 1, second, jnp.zeros((b, dh), F32))
        dq_ref[...] = (dq * scale).astype(BF16)

    col = pl.BlockSpec((None, b, 1), lambda h, i: (h, i, 0))
    return _call(
        body, side, name=name, grid=(nh, nb),
        in_specs=[pl.BlockSpec((None, b, dh), lambda h, i: (0, i, h)),
                  pl.BlockSpec((None, t, dh), lambda h, i: (1, 0, h)),
                  pl.BlockSpec((None, t, dh), lambda h, i: (2, 0, h)),
                  pl.BlockSpec((None, nb, 1, b), lambda h, i: (h, 0, 0, 0)),
                  pl.BlockSpec((b, dh), lambda h, i: (i, h)), col],
        out_specs=[pl.BlockSpec((None, b, dh), lambda h, i: (2, i, h)), col],
        out_shape=[jax.ShapeDtypeStruct((3, t, d), BF16), jax.ShapeDtypeStruct((nh, t, 1), F32)],
        scratch_shapes=[pltpu.VMEM((nb, b, b), F32), pltpu.VMEM((nb, b, b), F32)],
        sem=("parallel", "parallel"), args=(qkv, qkv, qkv, fk, do, lse))


def fox_bwd_dkv(qkv, fk, do, lse, delta, dqkv, *, side=None, name):
    _, t, d = qkv.shape
    nh = d // FOX_HEAD
    b = _blk(FOX_BLK, t)
    nb = t // b
    dh = FOX_HEAD
    scale = dh ** -0.5

    def body(q_ref, k_ref, v_ref, f_ref, do_ref, lse_ref, dl_ref, dz_in, dkv_ref, df_ref):
        del dz_in
        kj = pl.program_id(1)
        kb, vb, fkv = k_ref[...], v_ref[...], f_ref[...]

        def step(qi, carry, masked):
            dk, dv, df = carry
            off = pl.multiple_of(qi * b, b)
            qb, dob = q_ref[pl.ds(off, b), :], do_ref[pl.ds(off, b), :]
            p = jnp.exp(_fox_scores(qb, kb, fkv, scale, masked) - lse_ref[pl.ds(off, b), :])
            dv = dv + lax.dot_general(p.astype(BF16), dob, _TN, preferred_element_type=F32)
            dp = lax.dot_general(dob, vb, _NT, preferred_element_type=F32)
            dsv = p * (dp - dl_ref[pl.ds(off, b), :])
            dk = dk + lax.dot_general(dsv.astype(BF16), qb, _TN, preferred_element_type=F32)
            return dk, dv, df - jnp.sum(dsv, axis=0, keepdims=True)

        init = (jnp.zeros((b, dh), F32), jnp.zeros((b, dh), F32), jnp.zeros((1, b), F32))
        carry = step(kj, init, True)
        dk, dv, df = lax.fori_loop(kj + 1, nb, lambda qi, c: step(qi, c, False), carry)
        dkv_ref[0] = (dk * scale).astype(BF16)
        dkv_ref[1] = dv.astype(BF16)
        df_ref[...] = df

    col = pl.BlockSpec((None, t, 1), lambda h, j: (h, 0, 0))
    return _call(
        body, side, name=name, grid=(nh, nb),
        in_specs=[pl.BlockSpec((None, t, dh), lambda h, j: (0, 0, h)),
                  pl.BlockSpec((None, b, dh), lambda h, j: (1, j, h)),
                  pl.BlockSpec((None, b, dh), lambda h, j: (2, j, h)),
                  pl.BlockSpec((None, None, 1, b), lambda h, j: (h, j, 0, 0)),
                  pl.BlockSpec((t, dh), lambda h, j: (0, h)), col, col, pl.BlockSpec(memory_space=pl.ANY)],
        out_specs=[pl.BlockSpec((2, b, dh), lambda h, j: (0, j, h)),
                   pl.BlockSpec((None, None, 1, b), lambda h, j: (h, j, 0, 0))],
        out_shape=[jax.ShapeDtypeStruct((3, t, d), BF16), jax.ShapeDtypeStruct((nh, nb, 1, b), F32)],
        aliases={7: 0}, sem=("parallel", "parallel"), args=(qkv, qkv, qkv, fk, do, lse, delta, dqkv))


FL_BLK = 256


def _log_sigmoid(x):
    return jnp.minimum(x, 0.0) - jnp.log(1.0 + jnp.exp(-jnp.abs(x)))


def fl_fwd(zf, bf, *, name):
    t, n = zf.shape
    bt = _blk(FL_BLK, t)

    def body(z_ref, b_ref, o_ref, carry):
        @pl.when(pl.program_id(0) == 0)
        def _():
            carry[...] = jnp.zeros_like(carry)

        ls = _log_sigmoid(z_ref[...] + b_ref[...])
        r = lax.broadcasted_iota(jnp.int32, (bt, bt), 0)
        c = lax.broadcasted_iota(jnp.int32, (bt, bt), 1)
        cs = jnp.dot((c <= r).astype(F32), ls, preferred_element_type=F32, precision=_HI) + carry[...]
        o_ref[...] = cs
        carry[...] = cs[bt - 1:bt, :]

    return pl.pallas_call(
        body, name=name, grid=(t // bt,),
        in_specs=[pl.BlockSpec((bt, n), lambda i: (i, 0)), pl.BlockSpec((1, n), lambda i: (0, 0))],
        out_specs=pl.BlockSpec((bt, n), lambda i: (i, 0)), out_shape=jax.ShapeDtypeStruct((t, n), F32),
        scratch_shapes=[pltpu.VMEM((1, n), F32)], compiler_params=_params(("arbitrary",)),
    )(zf, bf)


def fl_bwd(df, zf, bf, *, name):
    t, n = zf.shape
    bt = _blk(FL_BLK, t)
    nb = t // bt

    def body(df_ref, z_ref, b_ref, dz_ref, db_ref, carry):
        first = pl.program_id(0) == 0

        @pl.when(first)
        def _():
            carry[...] = jnp.zeros_like(carry)

        r = lax.broadcasted_iota(jnp.int32, (bt, bt), 0)
        c = lax.broadcasted_iota(jnp.int32, (bt, bt), 1)
        dls = jnp.dot((c >= r).astype(F32), df_ref[...], preferred_element_type=F32, precision=_HI) + carry[...]
        carry[...] = dls[0:1, :]
        dz = dls * (1.0 - _sigmoid(z_ref[...] + b_ref[...]))
        dz_ref[...] = dz.astype(BF16)
        part = jnp.sum(dz, axis=0, keepdims=True)

        @pl.when(first)
        def _():
            db_ref[...] = part

        @pl.when(jnp.logical_not(first))
        def _():
            db_ref[...] += part

    row = pl.BlockSpec((bt, n), lambda i: (nb - 1 - i, 0))
    vec = pl.BlockSpec((1, n), lambda i: (0, 0))
    return pl.pallas_call(
        body, name=name, grid=(nb,), in_specs=[row, row, vec], out_specs=[row, vec],
        out_shape=[jax.ShapeDtypeStruct((t, n), BF16), jax.ShapeDtypeStruct((1, n), F32)],
        scratch_shapes=[pltpu.VMEM((1, n), F32)], compiler_params=_params(("arbitrary",)),
    )(df, zf, bf)


def _adamw_math(w, g, m, v):
    m = ADAM_B1 * m + (1.0 - ADAM_B1) * g
    v = ADAM_B2 * v + (1.0 - ADAM_B2) * (g * g)
    m_hat = m / (1.0 - ADAM_B1 ** ADAM_STEP)
    v_hat = v / (1.0 - ADAM_B2 ** ADAM_STEP)
    delta = -ADAM_LR * (m_hat / (jnp.sqrt(v_hat) + ADAM_EPS) + ADAM_WD * w)
    return delta, m, v


def adamw(w, g, m, v, *, name):
    r, c = w.shape
    br = _blk(256, r)

    def body(w_ref, g_ref, m_ref, v_ref, go_ref, d_ref, mo_ref, vo_ref):
        gv = g_ref[...]
        go_ref[...] = gv
        d_ref[...], mo_ref[...], vo_ref[...] = _adamw_math(w_ref[...], gv, m_ref[...], v_ref[...])

    spec = pl.BlockSpec((br, c), lambda i: (i, 0))
    return pl.pallas_call(
        body, name=name, grid=(r // br,), in_specs=[spec] * 4, out_specs=[spec] * 4,
        out_shape=[jax.ShapeDtypeStruct((r, c), F32)] * 4, compiler_params=_params(("parallel",)),
    )(w, g, m, v)


def _f2(a):
    return a.reshape(a.shape[-2:])


def _local_step(x0, mem, tgt, sp, plan):
    t, d = x0.shape
    mix_a = sp["pool_scale"].shape[1]
    small = {}

    def row(a, l):
        return a[l:l + 1]

    def xattn_f(l, xin):
        w = plan.weights(f"xa{l}")
        hx = rmsnorm_fwd(xin, row(sp["xa_norm"], l), name=f"xa_norm_f{l}")
        q = _f2(matmul(vm2(hx), w["wq"], "nn", out_dtype=BF16, side=plan.take_fwd(), name=f"xa_q_f{l}"))
        mn = rmsnorm_fwd(mem, row(sp["xa_mem_norm"], l), name=f"xa_memnorm_f{l}")
        kv = _f2(matmul(vm2(mn), w["wkv"], "nn", out_dtype=BF16, name=f"xa_kv_f{l}"))
        o = xattn_fwd(q, kv, name=f"xa_attn_f{l}")
        xout = _f2(matmul(vm2(o), w["wo"], "nn", out_dtype=F32, res=vm2(xin), side=plan.take_fwd(), name=f"xa_o_f{l}"))
        return xout, (xin, hx, q, mn, kv, o)

    def ffn_f(l, xin):
        w = plan.weights(f"ffn{l}")
        hf = rmsnorm_fwd(xin, row(sp["ffn_norm"], l), name=f"ffn_norm_f{l}")
        a = _f2(matmul(vm2(hf), w["gate"], "nn", out_dtype=BF16, side=plan.take_fwd(), name=f"ffn_gate_f{l}"))
        b, s = matmul(vm2(hf), w["up"], "nn", out_dtype=BF16, epi=(_swiglu_epi, [vm2(a)], 2), side=plan.take_fwd(), name=f"ffn_up_f{l}")
        b, s = _f2(b), _f2(s)
        xout = _f2(matmul(vm2(s), w["down"], "nn", out_dtype=F32, res=vm2(xin), side=plan.take_fwd(), name=f"ffn_down_f{l}"))
        return xout, (xin, hf, a, b, s)

    ev = plan.weights("ev")
    h0 = rmsnorm_fwd(x0, sp["ev_norm"], name="ev_norm_f")
    z = _f2(matmul(vm2(h0), ev["ev_in"], "nn", out_dtype=F32, name="ev_in_f"))
    cat = pool_fwd(z, ev["pool"], sp["pool_scale"], name="pool_f")
    cat, states = hgrn_fwd(z, cat, sp["lb"], sp["hg_gain"], mix_a, side=plan.take_fwd(), name="hgrn_f")
    x1 = _f2(matmul(VM(cat), ev["ev_out"], "nn", out_dtype=F32, res=vm2(x0), side=plan.take_fwd(), name="ev_out_f"))
    x2, xa0 = xattn_f(0, x1)
    x3, ff0 = ffn_f(0, x2)

    od = plan.weights("od")
    ho = rmsnorm_fwd(x3, sp["od_norm"], name="od_norm_f")
    qkv = matmul(vm2(ho), od["wqkv"], "nn", out_dtype=BF16, out_p=3, side=plan.take_fwd(), name="od_qkv_f")
    zf = _f2(matmul(vm2(ho), od["wf"], "nn", out_dtype=F32, name="od_fl_f"))
    fcum = fl_fwd(zf, sp["bf"], name="od_forget_f")
    nh = d // FOX_HEAD
    nfb = t // _blk(FOX_BLK, t)
    fk = fcum[:, :nh].T.reshape(nh, nfb, 1, t // nfb)
    of, lse = fox_fwd(qkv, fk, side=plan.take_fwd(), name="fox_f")
    x4 = _f2(matmul(vm2(of), od["od_out"], "nn", out_dtype=F32, res=vm2(x3), name="od_out_f"))
    x5, xa1 = xattn_f(1, x4)
    x6, ff1 = ffn_f(1, x5)
    loss, dx, dxb, small["final_norm"] = loss_head(x6, sp["final_norm"], tgt, name="loss_head")

    def ffn_b(l, saved, dx, dxb):
        xin, hf, a, b, s = saved
        w = plan.weights(f"ffn{l}")
        da, db = matmul(vm2(dxb), w["down"], "nt", out_dtype=BF16, epi=(_swiglu_bwd_epi, [vm2(a), vm2(b)], 2), side=plan.take_bwd(), name=f"ffn_down_bx{l}")
        da, db = _f2(da), _f2(db)
        g_down = matmul(vm2(s), vm2(dxb), "tn", out_dtype=BF16, out_kind="rs", out_p=N_CHIPS, name=f"ffn_down_bw{l}")
        g_gate = matmul(vm2(hf), vm2(da), "tn", out_dtype=BF16, out_p=N_CHIPS, name=f"ffn_gate_bw{l}")
        g_up = matmul(vm2(hf), vm2(db), "tn", out_dtype=BF16, out_p=N_CHIPS, name=f"ffn_up_bw{l}")
        plan.grads_done({f"down{l}": g_down, f"gate{l}": g_gate, f"up{l}": g_up})
        dh = matmul(vm2(da), w["gate"], "nt", out_dtype=F32, side=plan.take_bwd(), name=f"ffn_gate_bx{l}")
        dh = _f2(matmul(vm2(db), w["up"], "nt", out_dtype=F32, res=VM(dh), side=plan.take_bwd(), name=f"ffn_up_bx{l}"))
        dx, dxb, dg = rmsnorm_bwd(xin, row(sp["ffn_norm"], l), dh, dx, name=f"ffn_norm_b{l}")
        return dx, dxb, dg

    def xattn_b(l, saved, dx, dxb):
        xin, hx, q, mn, kv, o = saved
        w = plan.weights(f"xa{l}")
        do = _f2(matmul(vm2(dxb), w["wo"], "nt", out_dtype=BF16, side=plan.take_bwd(), name=f"xa_o_bx{l}"))
        g_wo = matmul(vm2(o), vm2(dxb), "tn", out_dtype=BF16, out_kind="rs", out_p=N_CHIPS, name=f"xa_o_bw{l}")
        dq, dkv = xattn_bwd(q, kv, do, name=f"xa_attn_b{l}")
        g_wq = matmul(vm2(hx), vm2(dq), "tn", out_dtype=BF16, out_kind="rs", out_p=N_CHIPS, name=f"xa_q_bw{l}")
        dh = _f2(matmul(vm2(dq), w["wq"], "nt", out_dtype=F32, name=f"xa_q_bx{l}"))
        dkvb = dkv.astype(BF16)
        g_wkv = matmul(vm2(mn), vm2(dkvb), "tn", out_dtype=BF16, out_p=N_CHIPS, name=f"xa_kv_bw{l}")
        plan.grads_done({f"wo{l}": g_wo, f"wq{l}": g_wq, f"wkv{l}": g_wkv})
        dmn = _f2(matmul(vm2(dkvb), w["wkv"], "nt", out_dtype=F32, side=plan.take_bwd(), name=f"xa_kv_bx{l}"))
        (dgm,) = rmsnorm_bwd(mem, row(sp["xa_mem_norm"], l), dmn, None, name=f"xa_memnorm_b{l}")
        dx, dxb, dg = rmsnorm_bwd(xin, row(sp["xa_norm"], l), dh, dx, name=f"xa_norm_b{l}")
        return dx, dxb, dg, dgm

    dg_ffn, dg_xa, dg_mem = [None, None], [None, None], [None, None]
    dx, dxb, dg_ffn[1] = ffn_b(1, ff1, dx, dxb)
    dx, dxb, dg_xa[1], dg_mem[1] = xattn_b(1, xa1, dx, dxb)

    do = _f2(matmul(vm2(dxb), od["od_out"], "nt", out_dtype=BF16, side=plan.take_bwd(), name="od_out_bx"))
    g_od_out = matmul(vm2(of), vm2(dxb), "tn", out_dtype=BF16, out_kind="rs", out_p=N_CHIPS, name="od_out_bw")
    dz3, delta = fox_bwd_dq(qkv, fk, do, lse, side=plan.take_bwd(1), name="fox_bq")
    dz3, dfk = fox_bwd_dkv(qkv, fk, do, lse, delta, dz3, side=plan.take_bwd(1), name="fox_bkv")
    dfc = jnp.pad(dfk.reshape(nh, t).T, ((0, 0), (0, zf.shape[1] - nh)))
    dzf, dbf = fl_bwd(dfc, zf, sp["bf"], name="od_forget_b")
    dqkv = VM(dz3, "cs", pfn=lambda p: lax.rem(p + 2, 3))
    dwqkv = _f2(matmul(vm2(ho), dqkv, "tn", out_dtype=BF16, name="od_qkv_bw"))
    dwf = _f2(matmul(vm2(ho), vm2(dzf), "tn", out_dtype=BF16, name="od_fl_bw"))
    od_in_full = jnp.concatenate([dwqkv, dwf[:, :nh]], axis=1)
    plan.grads_done({"od_out": g_od_out, "od_in": od_in_full.reshape(d, N_CHIPS, -1).transpose(1, 0, 2)})
    dh = matmul(dqkv, od["wqkv"], "nt", out_dtype=F32, side=plan.take_bwd(), name="od_qkv_bx")
    dh = _f2(matmul(vm2(dzf), od["wf"], "nt", out_dtype=F32, res=VM(dh), name="od_fl_bx"))
    dx, dxb, small["od_norm"] = rmsnorm_bwd(x3, sp["od_norm"], dh, dx, name="od_norm_b")
    small["bf"] = dbf

    dx, dxb, dg_ffn[0] = ffn_b(0, ff0, dx, dxb)
    dx, dxb, dg_xa[0], dg_mem[0] = xattn_b(0, xa0, dx, dxb)

    dcat = matmul(vm2(dxb), ev["ev_out"], "nt", out_dtype=BF16, out_p=2, side=plan.take_bwd(), name="ev_out_bx")
    g_ev_out = matmul(VM(cat), vm2(dxb), "tn", out_dtype=BF16, out_kind="rs", out_p=N_CHIPS, name="ev_out_bw")
    dz5, g_pool, small["pool_scale"] = pool_bwd(z, dcat, ev["pool"], sp["pool_scale"], name="pool_b")
    dz5, small["lb"], dgn = hgrn_bwd(z, dcat, dz5, states, sp["lb"], sp["hg_gain"], mix_a, side=plan.take_bwd(2), name="hgrn_b")
    small["hg_gain"] = jnp.sum(dgn, axis=0)
    dzv = VM(dz5, "cs", pfn=lambda p: lax.rem(p + 4, 5))
    g_ev_in = _f2(matmul(vm2(h0), dzv, "tn", out_dtype=BF16, side=plan.take_bwd(1), name="ev_in_bw"))
    g_ev_in = g_ev_in.reshape(d, N_CHIPS, -1).transpose(1, 0, 2)
    ng, gsz = g_pool.shape[0], g_pool.shape[1]
    pool_parts = g_pool.reshape(ng, N_CHIPS, gsz // N_CHIPS, gsz).transpose(1, 0, 2, 3).reshape(N_CHIPS, gsz, gsz).astype(BF16)
    plan.grads_done({"ev_out": g_ev_out, "pool": pool_parts, "ev_in": g_ev_in}, now=True)
    dh = _f2(matmul(dzv, ev["ev_in"], "nt", out_dtype=F32, side=plan.take_bwd(1), name="ev_in_bx"))
    dx, _, small["ev_norm"] = rmsnorm_bwd(x0, sp["ev_norm"], dh, dx, name="ev_norm_b")

    small["xa_norm"] = jnp.concatenate(dg_xa, axis=0)
    small["xa_mem_norm"] = jnp.concatenate(dg_mem, axis=0)
    small["ffn_norm"] = jnp.concatenate(dg_ffn, axis=0)
    return loss, dx, small


def gather_forward(fulls, *, name):
    n = len(fulls)

    def body(*refs):
        outs = refs[n:2 * n]
        ssem, rsem = refs[2 * n:]
        x, y, c, chips = _me()
        sibling = (x, y, 1 - c)

        def rows(w, j, which):
            h = fulls[w].shape[1] // 2
            return outs[w].at[_chip_id(chips[j]), pl.ds(which * h, h)]

        def swap(w, j):
            return _rcopy(rows(w, j, c), rows(w, j, c), ssem.at[w, j], rsem.at[w, j], sibling)

        for w in range(n):
            for j in range(3):
                swap(w, j).start()
        for w in range(n):
            for j in range(3):
                swap(w, j).wait_send()
                _rcopy(rows(w, j, 1 - c), rows(w, j, 1 - c), ssem.at[w, j], rsem.at[w, j], sibling).wait_recv()

    return pl.pallas_call(
        body, name=name, in_specs=[_ANY] * n, out_specs=[_ANY] * n,
        out_shape=[jax.ShapeDtypeStruct(f.shape, f.dtype) for f in fulls], input_output_aliases={w: w for w in range(n)},
        scratch_shapes=[pltpu.SemaphoreType.DMA((n, 3)), pltpu.SemaphoreType.DMA((n, 3))],
    )(*fulls)


def gather_shards(shards, *, name):
    n = len(shards)

    def body(*refs):
        ins, outs = refs[:n], refs[n:2 * n]
        ssem, rsem = refs[2 * n:]
        x, y, c, chips = _me()
        mine = _chip_id((x, y))
        sibling = (x, y, 1 - c)

        def rows(w, chip_id, which):
            h = shards[w].shape[0] // 2
            return outs[w].at[chip_id, pl.ds(which * h, h)]

        def to_chip(w, j):
            h = shards[w].shape[0] // 2
            return _rcopy(ins[w].at[pl.ds(c * h, h)], rows(w, mine, c), ssem.at[w, j], rsem.at[w, j], (*chips[j], c))

        def from_chip(w, j):
            r = rows(w, _chip_id(chips[j]), c)
            return _rcopy(r, r, ssem.at[w, j], rsem.at[w, j], (*chips[j], c))

        def to_sibling(w, j):
            r = rows(w, _chip_id(chips[j]), c)
            return _rcopy(r, r, ssem.at[w, 3 + j], rsem.at[w, 3 + j], sibling)

        def from_sibling(w, j):
            r = rows(w, _chip_id(chips[j]), 1 - c)
            return _rcopy(r, r, ssem.at[w, 3 + j], rsem.at[w, 3 + j], sibling)

        for w in range(n):
            for j in range(3):
                to_chip(w, j).start()
        for w in range(n):
            for j in range(3):
                from_chip(w, j).wait_recv()
                to_sibling(w, j).start()
        for w in range(n):
            for j in range(3):
                from_sibling(w, j).wait_recv()
        for w in range(n):
            for j in range(3):
                to_chip(w, j).wait_send()
                to_sibling(w, j).wait_send()

    return pl.pallas_call(
        body, name=name, in_specs=[_ANY] * n, out_specs=[_ANY] * n,
        out_shape=[jax.ShapeDtypeStruct((N_CHIPS,) + s.shape, s.dtype) for s in shards],
        scratch_shapes=[pltpu.SemaphoreType.DMA((n, 6)), pltpu.SemaphoreType.DMA((n, 6))],
    )(*shards)


def _ids_spec(grid, in_specs, out_specs):
    return pltpu.PrefetchScalarGridSpec(num_scalar_prefetch=1, grid=grid, in_specs=in_specs, out_specs=out_specs)


def fill_own(full, shard, ids, *, name):
    r, c = shard.shape
    br = _row_blk(r, 512)

    def body(ids_ref, s_ref, f_in, o_ref):
        del ids_ref, f_in
        o_ref[...] = s_ref[...]

    return pl.pallas_call(
        body, name=name, out_shape=jax.ShapeDtypeStruct(full.shape, full.dtype), input_output_aliases={2: 0},
        grid_spec=_ids_spec((r // br,), [pl.BlockSpec((br, c), lambda i, ids: (i, 0)), _ANY],
                            pl.BlockSpec((None, br, c), lambda i, ids: (ids[0], i, 0))),
        compiler_params=_params(("parallel",)),
    )(ids, shard, full)


def rs_pair(parts, *, name):
    n = len(parts)

    def body(*refs):
        ins, recv = refs[:n], refs[n:2 * n]
        ssem, rsem = refs[2 * n:]
        x, y, c, _ = _me()
        sibling = (x, y, 1 - c)

        def swap(w):
            h = parts[w].shape[1] // 2
            return _rcopy(ins[w].at[:, pl.ds((1 - c) * h, h), :], recv[w], ssem.at[w], rsem.at[w], sibling)

        for w in range(n):
            swap(w).start()
        for w in range(n):
            swap(w).wait()

    return pl.pallas_call(
        body, name=name, in_specs=[_ANY] * n, out_specs=[_ANY] * n,
        out_shape=[jax.ShapeDtypeStruct((p.shape[0], p.shape[1] // 2, p.shape[2]), p.dtype) for p in parts],
        scratch_shapes=[pltpu.SemaphoreType.DMA((n,)), pltpu.SemaphoreType.DMA((n,))],
    )(*parts)


def add_pair(part, recv, ids, *, name):
    p, h, c = recv.shape
    br = _row_blk(h, 512)
    nb = h // br

    def body(ids_ref, a_ref, b_ref, o_ref):
        del ids_ref
        o_ref[...] = (a_ref[...].astype(F32) + b_ref[...].astype(F32)).astype(o_ref.dtype)

    half = pl.BlockSpec((None, br, c), lambda k, i, ids: (k, i, 0))
    return pl.pallas_call(
        body, name=name, out_shape=jax.ShapeDtypeStruct(recv.shape, recv.dtype),
        grid_spec=_ids_spec((p, nb), [pl.BlockSpec((None, br, c), lambda k, i, ids: (k, ids[1] * nb + i, 0)), half], half),
        compiler_params=_params(("parallel", "parallel")),
    )(ids, part, recv)


def rs_chip(sums, *, name):
    n = len(sums)

    def body(*refs):
        ins, outs = refs[:n], refs[n:2 * n]
        ssem, rsem = refs[2 * n:]
        x, y, c, chips = _me()

        def swap(w, j):
            return _rcopy(ins[w].at[_chip_id(chips[j])], outs[w].at[j], ssem.at[w, j], rsem.at[w, j], (*chips[j], c))

        for w in range(n):
            for j in range(3):
                swap(w, j).start()
        for w in range(n):
            for j in range(3):
                swap(w, j).wait()

    return pl.pallas_call(
        body, name=name, in_specs=[_ANY] * n, out_specs=[_ANY] * n,
        out_shape=[jax.ShapeDtypeStruct((3,) + s.shape[1:], s.dtype) for s in sums],
        scratch_shapes=[pltpu.SemaphoreType.DMA((n, 3)), pltpu.SemaphoreType.DMA((n, 3))],
    )(*sums)


def add_chips(sums, landed, ids, group, layer, group_shape, *, name):
    _, h, c = sums.shape
    br = _row_blk(h, 256)
    nb = h // br

    def body(ids_ref, a_ref, b_ref, *rest):
        o_ref = rest[-1]
        tot = a_ref[...].astype(F32)
        for k in range(3):
            tot = tot + b_ref[k].astype(F32)
        o_ref[...] = tot

    in_specs = [pl.BlockSpec((None, br, c), lambda i, ids: (ids[0], i, 0)), pl.BlockSpec((3, br, c), lambda i, ids: (0, i, 0))]
    args = [ids, sums, landed]
    if group is not None:
        in_specs.append(_ANY)
        args.append(group)
    return pl.pallas_call(
        body, name=name, out_shape=jax.ShapeDtypeStruct(group_shape, F32),
        input_output_aliases={3: 0} if group is not None else {},
        grid_spec=_ids_spec((nb,), in_specs, pl.BlockSpec((None, br, c), lambda i, ids: (layer, ids[1] * nb + i, 0))),
        compiler_params=_params(("parallel",)),
    )(*args)


def rs_share(groups, slots, *, name):
    ng = len(groups)
    n = len(slots)

    def body(*refs):
        outs = refs[ng:2 * ng]
        ssem, rsem = refs[2 * ng:]
        x, y, c, _ = _me()
        sibling = (x, y, 1 - c)

        def rows(w, which):
            g, l = slots[w]
            h = groups[g].shape[1] // 2
            return outs[g].at[l, pl.ds(which * h, h), :]

        def swap(w):
            return _rcopy(rows(w, c), rows(w, c), ssem.at[w], rsem.at[w], sibling)

        for w in range(n):
            swap(w).start()
        for w in range(n):
            swap(w).wait_send()
            _rcopy(rows(w, 1 - c), rows(w, 1 - c), ssem.at[w], rsem.at[w], sibling).wait_recv()

    return pl.pallas_call(
        body, name=name, in_specs=[_ANY] * ng, out_specs=[_ANY] * ng,
        out_shape=[jax.ShapeDtypeStruct(g.shape, g.dtype) for g in groups],
        input_output_aliases={g: g for g in range(ng)},
        scratch_shapes=[pltpu.SemaphoreType.DMA((n,)), pltpu.SemaphoreType.DMA((n,))],
    )(*groups)


def allreduce_small(v, *, name):
    r, c = v.shape
    ndev = 2 * N_CHIPS

    def body(v_ref, o_ref, buf, ssem, rsem):
        x, y, cc, _ = _me()
        me = 4 * x + 2 * y + cc
        flips = [(a, b, d) for a in (0, 1) for b in (0, 1) for d in (0, 1)][1:]
        buf[me] = v_ref[...]
        cps = []
        for k, (a, b, d) in enumerate(flips):
            peer = (jnp.bitwise_xor(x, a), jnp.bitwise_xor(y, b), jnp.bitwise_xor(cc, d))
            cp = _rcopy(v_ref, buf.at[me], ssem.at[k], rsem.at[k], peer)
            cp.start()
            cps.append(cp)
        for k, (a, b, d) in enumerate(flips):
            peer = (jnp.bitwise_xor(x, a), jnp.bitwise_xor(y, b), jnp.bitwise_xor(cc, d))
            src = 4 * peer[0] + 2 * peer[1] + peer[2]
            _rcopy(v_ref, buf.at[src], ssem.at[k], rsem.at[k], peer).wait_recv()
        for cp in cps:
            cp.wait_send()
        tot = buf[0]
        for k in range(1, ndev):
            tot = tot + buf[k]
        o_ref[...] = tot

    vm = pl.BlockSpec(memory_space=pltpu.VMEM)
    return pl.pallas_call(
        body, name=name, in_specs=[vm], out_specs=vm, out_shape=jax.ShapeDtypeStruct((r, c), F32),
        scratch_shapes=[pltpu.VMEM((ndev, r, c), F32), pltpu.SemaphoreType.DMA((ndev - 1,)), pltpu.SemaphoreType.DMA((ndev - 1,))],
    )(v)


WEIGHTS = ["lb_table", "ev_norm", "ev_w_in", "ev_w_pool", "ev_pool_scale", "ev_hg_norm", "ev_w_out", "od_norm", "od_w_in",
           "od_b_f", "od_w_out", "xa_norm", "xa_mem_norm", "xa_wq", "xa_wkv", "xa_wo", "ffn_norm", "ffn_w_gate", "ffn_w_up",
           "ffn_w_down", "final_norm"]
BIG = ["ev_w_in", "ev_w_pool", "ev_w_out", "od_w_in", "od_w_out", "xa_wq", "xa_wkv", "xa_wo", "ffn_w_gate", "ffn_w_up", "ffn_w_down"]
SMALL_ROWS = 16


def _rows(parts, width):
    rows = [jnp.pad(p.reshape(-1, p.shape[-1]).astype(F32), ((0, 0), (0, width - p.shape[-1]))) for p in parts]
    out = jnp.concatenate(rows, axis=0)
    return jnp.pad(out, ((0, SMALL_ROWS - out.shape[0]), (0, 0)))


def _unrows(packed, like):
    out, r = [], 0
    for p in like:
        n = p.size // p.shape[-1]
        out.append(packed[r:r + n, :p.shape[-1]].reshape(p.shape))
        r += n
    return out


def _m3(a):
    return a.reshape(a.shape[0], -1, a.shape[-1])


SLOT = {"ev_in": ("ev_w_in", 0), "pool": ("ev_w_pool", 0), "ev_out": ("ev_w_out", 0), "od_in": ("od_w_in", 0),
        "od_out": ("od_w_out", 0)}
for _l in range(2):
    SLOT.update({f"wq{_l}": ("xa_wq", _l), f"wkv{_l}": ("xa_wkv", _l), f"wo{_l}": ("xa_wo", _l),
                 f"gate{_l}": ("ffn_w_gate", _l), f"up{_l}": ("ffn_w_up", _l), f"down{_l}": ("ffn_w_down", _l)})
GATHER_FIRST = ["ev_in", "ev_out", "pool", "od_norm"]
GATHER_CARRIED = [["wq0", "wo0", "wkv0"], ["gate0"], ["up0"], ["down0"], ["od_in"], ["od_out"], ["wq1", "wo1", "wkv1"],
                  ["gate1"], ["up1", "down1"]]


class _Lazy:
    def __init__(self, plan, group):
        self.plan, self.layer = plan, group[-1] if group[-1] in "01" else ""

    def __getitem__(self, key):
        return self.plan.w(key + self.layer if key in ("wq", "wo", "wkv", "gate", "up", "down") else key)


class _Plan:
    def __init__(self, shards, ids, group_shapes, d, nh):
        self.shards, self.ids, self.group_shapes, self.d, self.nh = shards, ids, group_shapes, d, nh
        self.full, self.cache = {}, {}
        self.queue, self.sides, self.fsides, self.forwarded = [list(u) for u in GATHER_CARRIED], [], [], set()
        self.parts, self.psides, self.sums, self.rqueue, self.rsides = [], [], {}, [], []
        got = gather_shards([shards[n] for n in GATHER_FIRST], name="gather_first")
        for n, f in zip(GATHER_FIRST, got):
            self.full[n] = fill_own(f, shards[n], ids, name=f"gather_own_{n}")

    def take_fwd(self):
        parts = []
        ready = [(ns, s) for ns, s in self.sides if s.outs is not None and ns[0] not in self.forwarded]
        for ns, s in ready:
            fs = ForwardSide(s.outs)
            self.fsides.append((ns, fs))
            self.forwarded.update(ns)
            parts.append(fs)
        if self.queue:
            names = self.queue.pop(0)
            side = GatherSide([self.shards[n] for n in names])
            self.sides.append((names, side))
            parts.append(side)
        return Sides(parts) if parts else None

    def _need(self, names):
        missing = [n for n in names if n not in self.full]
        if not missing:
            return
        done = {n: a for ns, s in self.fsides if s.outs is not None for n, a in zip(ns, s.outs)}
        landed = {n: a for ns, s in self.sides if s.outs is not None for n, a in zip(ns, s.outs)}
        pre = {n: done[n] for n in missing if n in done}
        half = [n for n in missing if n not in done and n in landed]
        late = [n for n in missing if n not in done and n not in landed]
        if half:
            self.forwarded.update(half)
            pre.update(zip(half, gather_forward([landed[n] for n in half], name=f"gather_forward_{half[0]}")))
        if late:
            self.queue = [u for u in ([n for n in u if n not in late] for u in self.queue) if u]
            pre.update(zip(late, gather_shards([self.shards[n] for n in late], name=f"gather_late_{late[0]}")))
        for n in missing:
            self.full[n] = fill_own(pre[n], self.shards[n], self.ids, name=f"gather_own_{n}")

    def w(self, name):
        if name in self.cache:
            return self.cache[name]
        if name in ("wqkv", "wf"):
            self._need(["od_in"])
            od_full = self.full["od_in"].transpose(1, 0, 2).reshape(self.d, -1)
            self.cache["wqkv"] = vm2(od_full[:, :3 * self.d])
            self.cache["wf"] = vm2(jnp.pad(od_full[:, 3 * self.d:], ((0, 0), (0, 128 - self.nh))))
            return self.cache[name]
        self._need([name])
        f = self.full[name]
        if name == "pool":
            rows, gsz = f.shape[1:]
            ng = rows * N_CHIPS // gsz
            out = f.reshape(N_CHIPS, ng, gsz // N_CHIPS, gsz).transpose(1, 0, 2, 3).reshape(ng, gsz, gsz)
        elif name == "ev_in":
            out = vm2(f.transpose(1, 0, 2).reshape(self.d, -1))
        else:
            out = VM(f, "cs" if name.rstrip("01") in ("wkv", "gate", "up") else "rs")
        self.cache[name] = out
        return out

    def weights(self, group):
        return _Lazy(self, group)

    def grads_done(self, parts, now=False):
        names = list(parts)
        if now:
            got = rs_pair([parts[n] for n in names], name=f"reduce_pair_{names[0]}")
            for n, g in zip(names, got):
                self.sums[n] = add_pair(parts[n], g, self.ids, name=f"reduce_add2_{n}")
            self.rqueue.append(names)
        else:
            self.parts.append((names, [parts[n] for n in names]))

    def _add_swapped(self):
        for names, parts, side in self.psides:
            if side.outs is not None and names[0] not in self.sums:
                for n, p, g in zip(names, parts, side.outs):
                    self.sums[n] = add_pair(p, g, self.ids, name=f"reduce_add2_{n}")
                self.rqueue.append(names)

    def take_bwd(self, units=0):
        self._add_swapped()
        sides = []
        for names, parts in self.parts:
            ps = PairSide(parts)
            self.psides.append((names, parts, ps))
            sides.append(ps)
        self.parts = []
        names = [n for u in self.rqueue[:units] for n in u]
        self.rqueue = self.rqueue[units:]
        if names:
            rs = ReduceSide([self.sums[n] for n in names])
            self.rsides.append((names, rs))
            sides.append(rs)
        return Sides(sides) if sides else None

    def finish(self):
        for names, parts in self.parts:
            self.grads_done(dict(zip(names, parts)), now=True)
        self._add_swapped()
        landed = {}
        for ns, side in self.rsides:
            landed.update(zip(ns, side.outs))
        rest = [n for u in self.rqueue for n in u]
        if rest:
            landed.update(zip(rest, rs_chip([self.sums[n] for n in rest], name="reduce_chips_rest")))
        gbig = {n: None for n in BIG}
        for n, (big, l) in SLOT.items():
            gbig[big] = add_chips(self.sums[n], landed[n], self.ids, gbig[big], l, self.group_shapes[big], name=f"reduce_add4_{n}")
        full = rs_share([gbig[n] for n in BIG], [(BIG.index(big), l) for big, l in SLOT.values()], name="reduce_share")
        return dict(zip(BIG, full))


def kernel(x, mem, lb_table, ev_norm, ev_w_in, ev_w_pool, ev_pool_scale, ev_hg_norm, ev_w_out, od_norm, od_w_in, od_b_f, od_w_out, xa_norm, xa_mem_norm, xa_wq, xa_wkv, xa_wo, ffn_norm, ffn_w_gate, ffn_w_up, ffn_w_down, final_norm, loss_target, m_lb_table, m_ev_norm, m_ev_w_in, m_ev_w_pool, m_ev_pool_scale, m_ev_hg_norm, m_ev_w_out, m_od_norm, m_od_w_in, m_od_b_f, m_od_w_out, m_xa_norm, m_xa_mem_norm, m_xa_wq, m_xa_wkv, m_xa_wo, m_ffn_norm, m_ffn_w_gate, m_ffn_w_up, m_ffn_w_down, m_final_norm, v_lb_table, v_ev_norm, v_ev_w_in, v_ev_w_pool, v_ev_pool_scale, v_ev_hg_norm, v_ev_w_out, v_od_norm, v_od_w_in, v_od_b_f, v_od_w_out, v_xa_norm, v_xa_mem_norm, v_xa_wq, v_xa_wkv, v_xa_wo, v_ffn_norm, v_ffn_w_gate, v_ffn_w_up, v_ffn_w_down, v_final_norm):
    a = dict(locals())
    w = {n: a[n] for n in WEIGHTS}
    mom = {n: a["m_" + n] for n in WEIGHTS}
    var = {n: a["v_" + n] for n in WEIGHTS}
    _, t, d = x.shape
    nh = d // FOX_HEAD
    lanes = 128
    cx, cy = lax.axis_index("x"), lax.axis_index("y")
    chip = 2 * cx + cy

    w3 = {n: _m3(w[n]) for n in BIG}
    flat = lambda v: v.reshape(-1, v.shape[-1])
    shards = {"od_norm": jnp.broadcast_to(od_norm, (16, od_norm.shape[1]))}
    for name, (big, l) in SLOT.items():
        shards[name] = w3[big][l].astype(BF16)
    ids = jnp.stack([chip, lax.axis_index("c")]).astype(jnp.int32)
    plan = _Plan(shards, ids, {n: w3[n].shape for n in BIG}, d, nh)
    od_norm_full = plan.full["od_norm"][:, 0, :].reshape(1, d)

    sm = jax.nn.softmax(lb_table, axis=0)
    sp = {
        "lb": sm[1:2], "ev_norm": ev_norm, "pool_scale": ev_pool_scale, "hg_gain": ev_hg_norm, "od_norm": od_norm_full,
        "bf": jnp.pad(od_b_f, ((0, 0), (0, lanes - nh))), "xa_norm": xa_norm, "xa_mem_norm": xa_mem_norm, "ffn_norm": ffn_norm,
        "final_norm": final_norm.reshape(1, d),
    }
    loss_l, gx, small = _local_step(x[0], mem[0], loss_target[0], sp, plan)
    loss = lax.psum(loss_l[0, 0], ("x", "y", "c"))
    gbig = plan.finish()

    raw_like = [small["lb"], small["ev_norm"], small["pool_scale"], small["hg_gain"], small["od_norm"], small["bf"],
                small["xa_norm"], small["xa_mem_norm"], small["ffn_norm"], small["final_norm"]]
    summed = _unrows(allreduce_small(_rows(raw_like, d), name="reduce_small"), raw_like)
    dlb, g_ev_norm, g_pool_scale, g_hg, g_od_norm_full, g_bf, g_xa, g_xam, g_ffn, g_final = summed
    dsm = jnp.zeros_like(sm).at[1:2].set(dlb)
    gsmall = {
        "lb_table": sm * (dsm - jnp.sum(sm * dsm, axis=0, keepdims=True)), "ev_norm": g_ev_norm, "ev_pool_scale": g_pool_scale,
        "ev_hg_norm": g_hg, "od_norm": lax.dynamic_slice_in_dim(g_od_norm_full, chip * od_norm.shape[1], od_norm.shape[1], axis=1),
        "od_b_f": g_bf[:, :nh], "xa_norm": g_xa, "xa_mem_norm": g_xam, "ffn_norm": g_ffn, "final_norm": g_final.reshape(d),
    }

    grad, delta, new_m, new_v = {}, {}, {}, {}
    for n in BIG:
        shp = w[n].shape
        res = adamw(flat(w3[n]), flat(gbig[n]), flat(_m3(mom[n])), flat(_m3(var[n])), name=f"adamw_{n}")
        grad[n], delta[n], new_m[n], new_v[n] = [r.reshape(shp) for r in res]
    snames = [n for n in WEIGHTS if n not in BIG]
    like = [w[n] for n in snames]
    res = adamw(_rows(like, d), _rows([gsmall[n] for n in snames], d), _rows([mom[n] for n in snames], d),
                _rows([var[n] for n in snames], d), name="adamw_small")
    for vals, dst in zip(res, (grad, delta, new_m, new_v)):
        dst.update(zip(snames, _unrows(vals, like)))
    return (loss, gx.reshape(x.shape), *[grad[n] for n in WEIGHTS], *[delta[n] for n in WEIGHTS],
            *[new_m[n] for n in WEIGHTS], *[new_v[n] for n in WEIGHTS])
```

```python
import functools
import math

import jax
import jax.numpy as jnp
from jax import lax
from jax.experimental import pallas as pl
from jax.experimental.pallas import tpu as pltpu

F32 = jnp.float32
BF16 = jnp.bfloat16
MESH = pl.DeviceIdType.MESH

V7X_VMEM_LIMIT_BYTES = 56 * 1024 * 1024
N_CHIPS = 4

EPS = 1e-6
POOL_WINDOWS = (2, 4, 8, 16)
POOL_HALO = 16
HG_HEAD = 128
HG_CHUNK = 64
FOX_HEAD = 128
FOX_BLK = 512
XA_HEADS = 4
ADAM_LR, ADAM_B1, ADAM_B2, ADAM_EPS, ADAM_WD, ADAM_STEP = 0.001, 0.9, 0.999, 1e-08, 0.01, 10
EXP_CLAMP = 80.0


def _params(sem=None):
    return pltpu.CompilerParams(dimension_semantics=sem, vmem_limit_bytes=V7X_VMEM_LIMIT_BYTES)


def _blk(pref, dim):
    b = min(pref, dim)
    assert dim % b == 0, (pref, dim)
    return b


class VM:
    def __init__(self, arr, kind="cs", lead=(), inner=(), pfn=None):
        self.arr, self.kind, self.lead, self.inner = arr, kind, tuple(lead), tuple(inner)
        self.pfn = pfn or (lambda p: p)
        p = arr.shape[len(self.lead)]
        r, c = arr.shape[-2:]
        assert arr.ndim == len(self.lead) + 1 + len(self.inner) + 2, (arr.shape, lead, inner)
        self.P = p
        self.shape = (r, c * p) if kind == "cs" else (r * p, c)
        self.dtype = arr.dtype

    def spec(self, br, bc, rfn, cfn):
        p = self.P
        r, c = self.arr.shape[-2:]
        assert c % bc == 0 and r % br == 0, (self.arr.shape, br, bc)
        if p == 1:
            def imap(*g):
                return (*self.lead, self.pfn(0), *self.inner, rfn(*g), cfn(*g))
        elif self.kind == "cs":
            per = c // bc

            def imap(*g):
                cb = cfn(*g)
                return (*self.lead, self.pfn(lax.div(cb, per)), *self.inner, rfn(*g), lax.rem(cb, per))
        else:
            per = r // br

            def imap(*g):
                rb = rfn(*g)
                return (*self.lead, self.pfn(lax.div(rb, per)), *self.inner, lax.rem(rb, per), cfn(*g))
        return pl.BlockSpec((None,) * (self.arr.ndim - 2) + (br, bc), imap)


def vm2(arr):
    return VM(arr.reshape((1,) + arr.shape))


def _out_struct(shape, kind, p, dtype):
    r, c = shape
    return jax.ShapeDtypeStruct((p, r, c // p) if kind == "cs" else (p, r // p, c), dtype)


_ANY = pl.BlockSpec(memory_space=pl.ANY)


def _me():
    x, y, c = lax.axis_index("x"), lax.axis_index("y"), lax.axis_index("c")
    chips = [(1 - x, y), (x, 1 - y), (1 - x, 1 - y)]
    return x, y, c, chips


def _chip_id(xy):
    return 2 * xy[0] + xy[1]


def _rcopy(src, dst, ssem, rsem, dev):
    return pltpu.make_async_remote_copy(src_ref=src, dst_ref=dst, send_sem=ssem, recv_sem=rsem, device_id=dev,
                                        device_id_type=MESH)


class GatherSide:
    def __init__(self, shards):
        self.inputs = list(shards)
        self.out_shape = [jax.ShapeDtypeStruct((N_CHIPS,) + s.shape, s.dtype) for s in shards]
        self.aliases = {}
        self.rows = len(shards)
        self.outs = None

    def _copy(self, ins, outs, ssem, rsem, w, j, receive):
        x, y, c, chips = _me()
        h = self.inputs[w].shape[0] // 2
        half = pl.ds(c * h, h)
        if receive:
            r = outs[w].at[_chip_id(chips[j]), half]
            return _rcopy(r, r, ssem.at[w, j], rsem.at[w, j], (*chips[j], c))
        return _rcopy(ins[w].at[half], outs[w].at[_chip_id((x, y)), half], ssem.at[w, j], rsem.at[w, j], (*chips[j], c))

    def start(self, ins, outs, ssem, rsem):
        for w in range(len(self.inputs)):
            for j in range(3):
                self._copy(ins, outs, ssem, rsem, w, j, False).start()

    def finish(self, ins, outs, ssem, rsem):
        for w in range(len(self.inputs)):
            for j in range(3):
                self._copy(ins, outs, ssem, rsem, w, j, True).wait_recv()
                self._copy(ins, outs, ssem, rsem, w, j, False).wait_send()


class ForwardSide:
    def __init__(self, fulls):
        self.inputs = list(fulls)
        self.out_shape = [jax.ShapeDtypeStruct(f.shape, f.dtype) for f in fulls]
        self.aliases = {w: w for w in range(len(fulls))}
        self.rows = len(fulls)
        self.outs = None

    def _copy(self, outs, ssem, rsem, w, j, receive):
        x, y, c, chips = _me()
        h = self.inputs[w].shape[1] // 2
        r = outs[w].at[_chip_id(chips[j]), pl.ds(((1 - c) if receive else c) * h, h)]
        return _rcopy(r, r, ssem.at[w, j], rsem.at[w, j], (x, y, 1 - c))

    def start(self, ins, outs, ssem, rsem):
        for w in range(self.rows):
            for j in range(3):
                self._copy(outs, ssem, rsem, w, j, False).start()

    def finish(self, ins, outs, ssem, rsem):
        for w in range(self.rows):
            for j in range(3):
                self._copy(outs, ssem, rsem, w, j, False).wait_send()
                self._copy(outs, ssem, rsem, w, j, True).wait_recv()


class PairSide:
    def __init__(self, parts):
        self.inputs = list(parts)
        self.out_shape = [jax.ShapeDtypeStruct((p.shape[0], p.shape[1] // 2, p.shape[2]), p.dtype) for p in parts]
        self.aliases = {}
        self.rows = len(parts)
        self.outs = None

    def _copy(self, ins, outs, ssem, rsem, w):
        x, y, c, _ = _me()
        h = self.inputs[w].shape[1] // 2
        return _rcopy(ins[w].at[:, pl.ds((1 - c) * h, h), :], outs[w], ssem.at[w, 0], rsem.at[w, 0], (x, y, 1 - c))

    def start(self, ins, outs, ssem, rsem):
        for w in range(self.rows):
            self._copy(ins, outs, ssem, rsem, w).start()

    def finish(self, ins, outs, ssem, rsem):
        for w in range(self.rows):
            self._copy(ins, outs, ssem, rsem, w).wait()


class _SemRows:
    def __init__(self, sem, off):
        self.sem, self.off = sem, off

    @property
    def at(self):
        return self

    def __getitem__(self, idx):
        return self.sem.at[self.off + idx[0], idx[1]]


class Sides:
    def __init__(self, sides):
        self.sides = list(sides)
        self.inputs = [a for s in self.sides for a in s.inputs]
        self.out_shape = [o for s in self.sides for o in s.out_shape]
        self.rows = sum(s.rows for s in self.sides)
        self.aliases, i0, o0 = {}, 0, 0
        for s in self.sides:
            self.aliases.update({i0 + i: o0 + o for i, o in s.aliases.items()})
            i0, o0 = i0 + len(s.inputs), o0 + len(s.out_shape)

    def _each(self, method, ins, outs, ssem, rsem):
        i0 = o0 = r0 = 0
        for s in self.sides:
            getattr(s, method)(ins[i0:i0 + len(s.inputs)], outs[o0:o0 + len(s.out_shape)], _SemRows(ssem, r0), _SemRows(rsem, r0))
            i0, o0, r0 = i0 + len(s.inputs), o0 + len(s.out_shape), r0 + s.rows

    def start(self, ins, outs, ssem, rsem):
        self._each("start", ins, outs, ssem, rsem)

    def finish(self, ins, outs, ssem, rsem):
        self._each("finish", ins, outs, ssem, rsem)

    @property
    def outs(self):
        return None

    @outs.setter
    def outs(self, vals):
        o0 = 0
        for s in self.sides:
            s.outs = list(vals[o0:o0 + len(s.out_shape)])
            o0 += len(s.out_shape)


class ReduceSide:
    def __init__(self, sums):
        self.inputs = list(sums)
        self.out_shape = [jax.ShapeDtypeStruct((3,) + s.shape[1:], s.dtype) for s in sums]
        self.aliases = {}
        self.rows = len(sums)
        self.outs = None

    def _copy(self, ins, outs, ssem, rsem, w, j):
        _, _, c, chips = _me()
        return _rcopy(ins[w].at[_chip_id(chips[j])], outs[w].at[j], ssem.at[w, j], rsem.at[w, j], (*chips[j], c))

    def start(self, ins, outs, ssem, rsem):
        for w in range(len(self.inputs)):
            for j in range(3):
                self._copy(ins, outs, ssem, rsem, w, j).start()

    def finish(self, ins, outs, ssem, rsem):
        for w in range(len(self.inputs)):
            for j in range(3):
                self._copy(ins, outs, ssem, rsem, w, j).wait()


def _call(body, side, *, name, grid, in_specs, out_specs, out_shape, scratch_shapes=(), sem, aliases=None, args):
    if side is None:
        return pl.pallas_call(body, name=name, grid=grid, in_specs=in_specs, out_specs=out_specs, out_shape=out_shape,
                              scratch_shapes=list(scratch_shapes), input_output_aliases=aliases or {},
                              compiler_params=_params(sem))(*args)
    single = not isinstance(out_shape, (list, tuple))
    oshape, ospecs = ([out_shape], [out_specs]) if single else (list(out_shape), list(out_specs))
    n_in, n_out, s_in, s_out = len(in_specs), len(oshape), len(side.inputs), len(side.out_shape)

    def wrapped(*refs):
        ins, sin = refs[:n_in], refs[n_in:n_in + s_in]
        outs = refs[n_in + s_in:n_in + s_in + n_out]
        souts = refs[n_in + s_in + n_out:n_in + s_in + n_out + s_out]
        rest = refs[n_in + s_in + n_out + s_out:]
        scratch, (ssem, rsem) = rest[:-2], rest[-2:]
        first = functools.reduce(jnp.logical_and, [pl.program_id(a) == 0 for a in range(len(grid))])
        last = functools.reduce(jnp.logical_and, [pl.program_id(a) == grid[a] - 1 for a in range(len(grid))])

        @pl.when(first)
        def _():
            side.start(sin, souts, ssem, rsem)

        body(*ins, *outs, *scratch)

        @pl.when(last)
        def _():
            side.finish(sin, souts, ssem, rsem)

    sems = pltpu.SemaphoreType.DMA((side.rows, 3))
    res = pl.pallas_call(
        wrapped, name=name, grid=grid, in_specs=list(in_specs) + [_ANY] * s_in, out_specs=ospecs + [_ANY] * s_out,
        out_shape=oshape + side.out_shape, scratch_shapes=list(scratch_shapes) + [sems, sems],
        input_output_aliases={**(aliases or {}), **{n_in + i: n_out + o for i, o in side.aliases.items()}},
        compiler_params=_params(("arbitrary",) * len(grid)),
    )(*args, *side.inputs)
    side.outs = list(res[n_out:])
    return res[0] if single else list(res[:n_out])


def _best(g, cap):
    if g <= cap:
        return g
    cands = [d for d in range(128, cap + 1, 128) if g % d == 0]
    assert cands, (g, cap)
    return cands[-1]


def _row_blk(n, cap):
    cands = [d for d in range(16, min(n, cap) + 1, 16) if n % d == 0]
    assert cands, (n, cap)
    return cands[-1]


def _tiles(a, b, mode, out_kind, out_p, bm, bn, bk):
    def cpiece(v):
        return v.arr.shape[-1] if v.kind == "cs" else v.shape[1]

    def rpiece(v):
        return v.arr.shape[-2] if v.kind == "rs" else v.shape[0]

    if mode == "nn":
        m, n = a.shape[0], b.shape[1]
        gm, gn, gk = rpiece(a), cpiece(b), math.gcd(cpiece(a), rpiece(b))
    elif mode == "nt":
        m, n = a.shape[0], b.shape[0]
        gm, gn, gk = rpiece(a), rpiece(b), math.gcd(cpiece(a), cpiece(b))
    else:
        m, n = a.shape[1], b.shape[1]
        gm, gn, gk = cpiece(a), cpiece(b), math.gcd(rpiece(a), rpiece(b))
    if out_kind == "cs":
        gn = math.gcd(gn, n // out_p)
    else:
        gm = math.gcd(gm, m // out_p)
    caps = {"nn": (1024, 1536, 2048), "nt": (512, 2048, 2048), "tn": (1536, 1536, 1024)}[mode]
    return (bm or _best(gm, caps[0])), (bn or _best(gn, caps[1])), (bk or _best(gk, caps[2]))


def matmul(a, b, mode, *, out_dtype, bm=None, bn=None, bk=None, out_kind="cs", out_p=1, out_pfn=None, res=None, epi=None,
           side=None, name):
    bm, bn, bk = _tiles(a, b, mode, out_kind, out_p, bm, bn, bk)
    if mode == "nn":
        (m, k), (k2, n) = a.shape, b.shape
        a_spec = a.spec(bm, bk, lambda i, j, kk: i, lambda i, j, kk: kk)
        b_spec = b.spec(bk, bn, lambda i, j, kk: kk, lambda i, j, kk: j)
        dims = (((1,), (0,)), ((), ()))
    elif mode == "nt":
        (m, k), (n, k2) = a.shape, b.shape
        a_spec = a.spec(bm, bk, lambda i, j, kk: i, lambda i, j, kk: kk)
        b_spec = b.spec(bn, bk, lambda i, j, kk: j, lambda i, j, kk: kk)
        dims = (((1,), (1,)), ((), ()))
    else:
        (k, m), (k2, n) = a.shape, b.shape
        a_spec = a.spec(bk, bm, lambda i, j, kk: kk, lambda i, j, kk: i)
        b_spec = b.spec(bk, bn, lambda i, j, kk: kk, lambda i, j, kk: j)
        dims = (((0,), (0,)), ((), ()))
    assert k == k2, (a.shape, b.shape, mode)
    assert m % bm == 0 and n % bn == 0 and k % bk == 0, (m, n, k, bm, bn, bk)
    nk = k // bk
    out_sds = _out_struct((m, n), out_kind, out_p, out_dtype)
    out_vm = VM(out_sds, out_kind, pfn=out_pfn)
    o_spec = out_vm.spec(bm, bn, lambda i, j, kk: i, lambda i, j, kk: j)
    in_specs, args = [a_spec, b_spec], [a.arr, b.arr]
    tiles = ([res] if res is not None else []) + (list(epi[1]) if epi else [])
    for v in tiles:
        assert v.shape == (m, n)
        in_specs.append(v.spec(bm, bn, lambda i, j, kk: i, lambda i, j, kk: j))
        args.append(v.arr)
    n_out = epi[2] if epi else 1

    def body(a_ref, b_ref, *rest):
        t_refs, o_refs = rest[:len(tiles)], rest[len(tiles):len(tiles) + n_out]
        part = lax.dot_general(a_ref[...], b_ref[...], dims, preferred_element_type=F32)

        def write(tot):
            if res is not None:
                tot = tot + t_refs[0][...].astype(F32)
            outs = epi[0](tot, *[r[...].astype(F32) for r in t_refs[len(tiles) - len(epi[1]):]]) if epi else (tot,)
            for o_ref, val in zip(o_refs, outs):
                o_ref[...] = val.astype(o_ref.dtype)

        if nk == 1:
            write(part)
            return
        acc = rest[-1]
        kk = pl.program_id(2)

        @pl.when(kk == 0)
        def _():
            acc[...] = part

        @pl.when(kk > 0)
        def _():
            acc[...] += part

        @pl.when(kk == nk - 1)
        def _():
            write(acc[...])

    return _call(body, side, name=name, grid=(m // bm, n // bn, nk), in_specs=in_specs,
                 out_specs=o_spec if n_out == 1 else [o_spec] * n_out, out_shape=out_sds if n_out == 1 else [out_sds] * n_out,
                 scratch_shapes=[pltpu.VMEM((bm, bn), F32)] if nk > 1 else [],
                 sem=("parallel", "parallel", "arbitrary"), args=args)


def rmsnorm_fwd(x, g, *, name):
    t, d = x.shape
    bt = _blk(512, t)

    def body(x_ref, g_ref, o_ref):
        xv = x_ref[...]
        r = lax.rsqrt(jnp.mean(xv * xv, axis=-1, keepdims=True) + EPS)
        o_ref[...] = (xv * r * g_ref[...]).astype(o_ref.dtype)

    return pl.pallas_call(
        body, name=name, grid=(t // bt,),
        in_specs=[pl.BlockSpec((bt, d), lambda i: (i, 0)), pl.BlockSpec((1, d), lambda i: (0, 0))],
        out_specs=pl.BlockSpec((bt, d), lambda i: (i, 0)), out_shape=jax.ShapeDtypeStruct((t, d), BF16),
        compiler_params=_params(("parallel",)),
    )(x, g)


def rmsnorm_bwd(x, g, dh, dres, *, name):
    t, d = x.shape
    bt = _blk(256, t)
    want_dx = dres is not None

    def body(x_ref, g_ref, dh_ref, *rest):
        if want_dx:
            dres_ref, dx_ref, dxb_ref, dg_ref = rest
        else:
            (dg_ref,) = rest
        xv = x_ref[...]
        dhv = dh_ref[...].astype(F32)
        r = lax.rsqrt(jnp.mean(xv * xv, axis=-1, keepdims=True) + EPS)
        xh = xv * r
        part = jnp.sum(dhv * xh, axis=0, keepdims=True)

        @pl.when(pl.program_id(0) == 0)
        def _():
            dg_ref[...] = part

        @pl.when(pl.program_id(0) > 0)
        def _():
            dg_ref[...] += part

        if want_dx:
            dy = dhv * g_ref[...]
            dxn = r * (dy - xh * jnp.mean(dy * xh, axis=-1, keepdims=True))
            dx = dres_ref[...] + dxn
            dx_ref[...] = dx
            dxb_ref[...] = dx.astype(BF16)

    row = pl.BlockSpec((bt, d), lambda i: (i, 0))
    vec = pl.BlockSpec((1, d), lambda i: (0, 0))
    in_specs, args = [row, vec, row], [x, g, dh]
    out_specs, out_shape = [vec], [jax.ShapeDtypeStruct((1, d), F32)]
    if want_dx:
        in_specs.append(row)
        args.append(dres)
        out_specs = [row, row] + out_specs
        out_shape = [jax.ShapeDtypeStruct((t, d), F32), jax.ShapeDtypeStruct((t, d), BF16)] + out_shape
    return pl.pallas_call(
        body, name=name, grid=(t // bt,), in_specs=in_specs, out_specs=out_specs, out_shape=out_shape,
        compiler_params=_params(("arbitrary",)),
    )(*args)


def loss_head(x, g, tgt, *, name):
    t, d = x.shape
    bt = _blk(256, t)

    def body(x_ref, g_ref, t_ref, loss_ref, dx_ref, dxb_ref, dg_ref):
        xv = x_ref[...]
        gv = g_ref[...]
        r = lax.rsqrt(jnp.mean(xv * xv, axis=-1, keepdims=True) + EPS)
        xh = xv * r
        e = xh * gv - t_ref[...]
        lpart = jnp.zeros((1, 128), F32) + jnp.sum(e * e) * (0.5 / d)
        dyv = e * (1.0 / d)
        gpart = jnp.sum(dyv * xh, axis=0, keepdims=True)

        @pl.when(pl.program_id(0) == 0)
        def _():
            loss_ref[...] = lpart
            dg_ref[...] = gpart

        @pl.when(pl.program_id(0) > 0)
        def _():
            loss_ref[...] += lpart
            dg_ref[...] += gpart

        dy = dyv * gv
        dx = r * (dy - xh * jnp.mean(dy * xh, axis=-1, keepdims=True))
        dx_ref[...] = dx
        dxb_ref[...] = dx.astype(BF16)

    row = pl.BlockSpec((bt, d), lambda i: (i, 0))
    vec = pl.BlockSpec((1, d), lambda i: (0, 0))
    return pl.pallas_call(
        body, name=name, grid=(t // bt,), in_specs=[row, vec, row],
        out_specs=[pl.BlockSpec((1, 128), lambda i: (0, 0)), row, row, vec],
        out_shape=[jax.ShapeDtypeStruct((1, 128), F32), jax.ShapeDtypeStruct((t, d), F32),
                   jax.ShapeDtypeStruct((t, d), BF16), jax.ShapeDtypeStruct((1, d), F32)],
        compiler_params=_params(("arbitrary",)),
    )(x, g, tgt)


def _sigmoid(x):
    return 1.0 / (1.0 + jnp.exp(-x))


def _swiglu_epi(b, a):
    return b, a * _sigmoid(a) * b


def _swiglu_bwd_epi(ds, a, b):
    sg = _sigmoid(a)
    return ds * b * sg * (1.0 + a * (1.0 - sg)), ds * a * sg


def _xa_probs(qh, kh, scale):
    s = lax.dot_general(qh, kh, (((1,), (1,)), ((), ())), preferred_element_type=F32) * scale
    s = s - jnp.max(s, axis=-1, keepdims=True)
    p = jnp.exp(s)
    return p / jnp.sum(p, axis=-1, keepdims=True)


def xattn_fwd(q, kv, *, name):
    t, d = q.shape
    m = kv.shape[0]
    hd = d // XA_HEADS
    bt = _blk(512, t)
    scale = hd ** -0.5

    def body(q_ref, kv_ref, o_ref):
        for h in range(XA_HEADS):
            qh = q_ref[:, h * hd:(h + 1) * hd]
            kh = kv_ref[:, h * hd:(h + 1) * hd]
            vh = kv_ref[:, d + h * hd:d + (h + 1) * hd]
            p = _xa_probs(qh, kh, scale)
            o_ref[:, h * hd:(h + 1) * hd] = jnp.dot(p.astype(BF16), vh, preferred_element_type=F32).astype(BF16)

    return pl.pallas_call(
        body, name=name, grid=(t // bt,),
        in_specs=[pl.BlockSpec((bt, d), lambda i: (i, 0)), pl.BlockSpec((m, 2 * d), lambda i: (0, 0))],
        out_specs=pl.BlockSpec((bt, d), lambda i: (i, 0)), out_shape=jax.ShapeDtypeStruct((t, d), BF16),
        compiler_params=_params(("parallel",)),
    )(q, kv)


def xattn_bwd(q, kv, do, *, name):
    t, d = q.shape
    m = kv.shape[0]
    hd = d // XA_HEADS
    bt = _blk(512, t)
    scale = hd ** -0.5

    def body(q_ref, kv_ref, do_ref, dq_ref, dkv_ref):
        first = pl.program_id(0) == 0
        for h in range(XA_HEADS):
            qs, ks, vs = slice(h * hd, (h + 1) * hd), slice(h * hd, (h + 1) * hd), slice(d + h * hd, d + (h + 1) * hd)
            qh, kh, vh, doh = q_ref[:, qs], kv_ref[:, ks], kv_ref[:, vs], do_ref[:, qs]
            p = _xa_probs(qh, kh, scale)
            dp = lax.dot_general(doh, vh, (((1,), (1,)), ((), ())), preferred_element_type=F32)
            dsv = p * (dp - jnp.sum(p * dp, axis=-1, keepdims=True)) * scale
            dsb = dsv.astype(BF16)
            dq_ref[:, qs] = jnp.dot(dsb, kh, preferred_element_type=F32).astype(BF16)
            dk = lax.dot_general(dsb, qh, (((0,), (0,)), ((), ())), preferred_element_type=F32)
            dv = lax.dot_general(p.astype(BF16), doh, (((0,), (0,)), ((), ())), preferred_element_type=F32)

            @pl.when(first)
            def _():
                dkv_ref[:, ks] = dk
                dkv_ref[:, vs] = dv

            @pl.when(jnp.logical_not(first))
            def _():
                dkv_ref[:, ks] += dk
                dkv_ref[:, vs] += dv

    row = pl.BlockSpec((bt, d), lambda i: (i, 0))
    full = pl.BlockSpec((m, 2 * d), lambda i: (0, 0))
    return pl.pallas_call(
        body, name=name, grid=(t // bt,), in_specs=[row, full, row], out_specs=[row, full],
        out_shape=[jax.ShapeDtypeStruct((t, d), BF16), jax.ShapeDtypeStruct((m, 2 * d), F32)],
        compiler_params=_params(("arbitrary",)),
    )(q, kv, do)


def _pool_p(buf, uv, rows, w, bt):
    acc = uv
    for dd in range(1, w):
        acc = acc + buf[pl.ds(POOL_HALO - dd, bt), :]
    cnt = jnp.minimum(rows + 1, w).astype(F32)
    return acc / cnt - uv


def pool_fwd(z, w_pool, scale, *, name):
    t = z.shape[0]
    ng, gsz = w_pool.shape[0], w_pool.shape[1]
    mix = ng * gsz
    bt = _blk(512, t)

    def body(u_ref, uh_ref, w_ref, sc_ref, o_ref, buf):
        r = pl.program_id(0)
        rows = r * bt + lax.broadcasted_iota(jnp.int32, (bt, 1), 0)
        for g in range(ng):
            gs = slice(g * gsz, (g + 1) * gsz)
            uv = u_ref[:, gs]
            buf[0:POOL_HALO, :] = jnp.where(r > 0, uh_ref[:, gs], 0.0)
            buf[POOL_HALO:POOL_HALO + bt, :] = uv
            p = _pool_p(buf, uv, rows, POOL_WINDOWS[g], bt)
            y = jnp.dot(p.astype(BF16), w_ref[g], preferred_element_type=F32) * sc_ref[:, gs]
            o_ref[:, gs] = y.astype(BF16)

    hb = bt // POOL_HALO
    return pl.pallas_call(
        body, name=name, grid=(t // bt,),
        in_specs=[pl.BlockSpec((bt, mix), lambda i: (i, 0)),
                  pl.BlockSpec((POOL_HALO, mix), lambda i: (jnp.maximum(i * hb - 1, 0), 0)),
                  pl.BlockSpec((ng, gsz, gsz), lambda i: (0, 0, 0)), pl.BlockSpec((1, mix), lambda i: (0, 0))],
        out_specs=pl.BlockSpec((None, bt, mix), lambda i: (0, i, 0)),
        out_shape=jax.ShapeDtypeStruct((2, t, mix), BF16),
        scratch_shapes=[pltpu.VMEM((POOL_HALO + bt, gsz), F32)],
        compiler_params=_params(("parallel",)),
    )(z, z, w_pool, scale)


def pool_bwd(z, dcat, w_pool, scale, *, name):
    t = z.shape[0]
    ng, gsz = w_pool.shape[0], w_pool.shape[1]
    mix = ng * gsz
    bt = _blk(512, t)
    nb = t // bt
    nt_dims = (((1,), (1,)), ((), ()))
    tn_dims = (((0,), (0,)), ((), ()))

    def body(u_ref, uh_ref, dy_ref, dyh_ref, w_ref, sc_ref, du_ref, dw_ref, dsc_ref, buf, buf2):
        r = pl.program_id(0)
        first = r == 0
        rows = r * bt + lax.broadcasted_iota(jnp.int32, (bt, 1), 0)
        rows_h = (r + 1) * bt + lax.broadcasted_iota(jnp.int32, (POOL_HALO, 1), 0)
        for g in range(ng):
            w = POOL_WINDOWS[g]
            gs = slice(g * gsz, (g + 1) * gsz)
            uv = u_ref[:, gs]
            buf[0:POOL_HALO, :] = jnp.where(r > 0, uh_ref[:, gs], 0.0)
            buf[POOL_HALO:POOL_HALO + bt, :] = uv
            pb = _pool_p(buf, uv, rows, w, bt).astype(BF16)
            wg = w_ref[g]
            sc = sc_ref[:, gs]
            y0 = jnp.dot(pb, wg, preferred_element_type=F32)
            dyv = dy_ref[:, gs].astype(F32)
            dsc = jnp.sum(dyv * y0, axis=0, keepdims=True)
            dyw = (dyv * sc).astype(BF16)
            dw = lax.dot_general(pb, dyw, tn_dims, preferred_element_type=F32)

            @pl.when(first)
            def _():
                dw_ref[g] = dw
                dsc_ref[:, gs] = dsc

            @pl.when(jnp.logical_not(first))
            def _():
                dw_ref[g] += dw
                dsc_ref[:, gs] += dsc

            dp = lax.dot_general(dyw, wg, nt_dims, preferred_element_type=F32)
            dyh = (dyh_ref[:, gs].astype(F32) * sc).astype(BF16)
            dph = lax.dot_general(dyh, wg, nt_dims, preferred_element_type=F32)
            dph = jnp.where(r < nb - 1, dph, 0.0)
            buf2[0:bt, :] = dp / jnp.minimum(rows + 1, w).astype(F32)
            buf2[bt:bt + POOL_HALO, :] = dph / jnp.minimum(rows_h + 1, w).astype(F32)
            acc = buf2[pl.ds(0, bt), :]
            for dd in range(1, w):
                acc = acc + buf2[pl.ds(dd, bt), :]
            du_ref[:, gs] = (acc - dp).astype(BF16)

    hb = bt // POOL_HALO
    nhb = t // POOL_HALO
    return pl.pallas_call(
        body, name=name, grid=(nb,),
        in_specs=[pl.BlockSpec((bt, mix), lambda i: (i, 0)),
                  pl.BlockSpec((POOL_HALO, mix), lambda i: (jnp.maximum(i * hb - 1, 0), 0)),
                  pl.BlockSpec((None, bt, mix), lambda i: (0, i, 0)),
                  pl.BlockSpec((None, POOL_HALO, mix), lambda i: (0, jnp.minimum((i + 1) * hb, nhb - 1), 0)),
                  pl.BlockSpec((ng, gsz, gsz), lambda i: (0, 0, 0)), pl.BlockSpec((1, mix), lambda i: (0, 0))],
        out_specs=[pl.BlockSpec((None, bt, mix), lambda i: (4, i, 0)),
                   pl.BlockSpec((ng, gsz, gsz), lambda i: (0, 0, 0)), pl.BlockSpec((1, mix), lambda i: (0, 0))],
        out_shape=[jax.ShapeDtypeStruct((5, t, mix), BF16), jax.ShapeDtypeStruct((ng, gsz, gsz), F32),
                   jax.ShapeDtypeStruct((1, mix), F32)],
        scratch_shapes=[pltpu.VMEM((POOL_HALO + bt, gsz), F32), pltpu.VMEM((bt + POOL_HALO, gsz), F32)],
        compiler_params=_params(("arbitrary",)),
    )(z, z, dcat, dcat, w_pool, scale)


HG_LEVELS = ((64, 31), (32, 15), (16, 7))
HG_DIAG = (8, 3)
_NT = (((1,), (1,)), ((), ()))
_TN = (((0,), (0,)), ((), ()))
_HI = lax.Precision.HIGHEST


def _hg_masks():
    c = HG_CHUNK
    t = lax.broadcasted_iota(jnp.int32, (c, c), 0)
    s = lax.broadcasted_iota(jnp.int32, (c, c), 1)
    masks = []
    for blk, row in HG_LEVELS:
        sh = blk.bit_length() - 1
        same = (t >> sh) == (s >> sh)
        masks.append(same & ((t & (blk - 1)) > row) & ((s & (blk - 1)) <= row))
    sh = HG_DIAG[0].bit_length() - 1
    masks.append(((t >> sh) == (s >> sh)) & (s <= t))
    return t, s, masks


def _row_of_block(x, blk, row):
    c, n = x.shape
    x3 = x.reshape(c // blk, blk, n)
    return jnp.broadcast_to(x3[:, row:row + 1, :], x3.shape).reshape(c, n)


def _hg_parts(qv, flv, lb, masks, tri):
    sgf = _sigmoid(flv)
    f = lb + (1.0 - lb) * sgf
    logf = jnp.log(f)
    kk = 1.0 - f
    sgq = _sigmoid(qv)
    qf = qv * sgq * (HG_HEAD ** -0.5)
    bc = jnp.dot(tri, logf, preferred_element_type=F32, precision=_HI)
    levels = []
    a = None
    for li, (blk, row) in enumerate(HG_LEVELS + (HG_DIAG,)):
        e = bc - _row_of_block(bc, blk, row)
        if li < len(HG_LEVELS):
            eq, ek = jnp.exp(jnp.minimum(e, 0.0)), jnp.exp(jnp.minimum(-e, 0.0))
        else:
            eq, ek = jnp.exp(jnp.clip(e, -EXP_CLAMP, EXP_CLAMP)), jnp.exp(jnp.clip(-e, -EXP_CLAMP, EXP_CLAMP))
        qt, kt = qf * eq, kk * ek
        part = jnp.where(masks[li], lax.dot_general(qt.astype(BF16), kt.astype(BF16), _NT, preferred_element_type=F32), 0.0)
        a = part if a is None else a + part
        levels.append((eq, ek, qt, kt))
    return dict(sgf=sgf, f=f, kk=kk, sgq=sgq, qf=qf, bc=bc, levels=levels, a=a)


def hgrn_fwd(z, cat, lb, gain, mix_a, *, side=None, name):
    t = z.shape[0]
    mix_b = lb.shape[1]
    nh = mix_b // HG_HEAD
    bt = _blk(256, t)
    ncb = bt // HG_CHUNK
    dh = HG_HEAD

    def body(q_ref, fl_ref, i_ref, g_ref, lb_ref, gain_ref, cat_in, o_ref, st_ref, st):
        del cat_in

        @pl.when(pl.program_id(1) == 0)
        def _():
            st[...] = jnp.zeros_like(st)

        t_i, s_i, masks = _hg_masks()
        tri = (s_i <= t_i).astype(F32)
        lbv, gn = lb_ref[...], gain_ref[...]
        for c in range(ncb):
            rs = slice(c * HG_CHUNK, (c + 1) * HG_CHUNK)
            pr = _hg_parts(q_ref[rs, :], fl_ref[rs, :], lbv, masks, tri)
            vb = i_ref[rs, :].astype(BF16)
            stv = st[...]
            st_ref[c] = stv
            bc = pr["bc"]
            qt = pr["qf"] * jnp.exp(bc)
            o = (jnp.dot(pr["a"].astype(BF16), vb, preferred_element_type=F32)
                 + lax.dot_general(qt.astype(BF16), stv.astype(BF16), _NT, preferred_element_type=F32))
            bl = bc[HG_CHUNK - 1:HG_CHUNK, :]
            khat = pr["kk"] * jnp.exp(bl - bc)
            st[...] = stv * jnp.exp(bl) + lax.dot_general(vb, khat.astype(BF16), _TN, preferred_element_type=F32)
            r = lax.rsqrt(jnp.mean(o * o, axis=-1, keepdims=True) + EPS)
            gv = g_ref[rs, :]
            o_ref[rs, :] = (o * r * gn * (gv * _sigmoid(gv))).astype(BF16)

    def col(which):
        base = (mix_a + which * mix_b) // dh
        return pl.BlockSpec((bt, dh), lambda h, i: (i, base + h))

    return _call(
        body, side, name=name, grid=(nh, t // bt),
        in_specs=[col(0), col(1), col(2), col(3), pl.BlockSpec((1, dh), lambda h, i: (0, h)),
                  pl.BlockSpec((1, dh), lambda h, i: (0, 0)), _ANY],
        out_specs=[pl.BlockSpec((None, bt, dh), lambda h, i: (1, i, h)),
                   pl.BlockSpec((None, ncb, dh, dh), lambda h, i: (h, i, 0, 0))],
        out_shape=[jax.ShapeDtypeStruct(cat.shape, BF16), jax.ShapeDtypeStruct((nh, t // HG_CHUNK, dh, dh), F32)],
        scratch_shapes=[pltpu.VMEM((dh, dh), F32)], aliases={6: 0}, sem=("parallel", "arbitrary"),
        args=(z, z, z, z, lb, gain, cat))


def hgrn_bwd(z, dcat, dz5, states, lb, gain, mix_a, *, side=None, name):
    t = z.shape[0]
    mix_b = lb.shape[1]
    nh = mix_b // HG_HEAD
    bt = _blk(256, t)
    nb = t // bt
    ncb = bt // HG_CHUNK
    dh = HG_HEAD

    def body(q_ref, fl_ref, i_ref, g_ref, dy_ref, st_ref, lb_ref, gain_ref, dz_in, dz_ref, dlb_ref, dgn_ref, dst):
        del dz_in
        first = pl.program_id(1) == 0

        @pl.when(first)
        def _():
            dst[...] = jnp.zeros_like(dst)

        t_i, s_i, masks = _hg_masks()
        tri = (s_i <= t_i).astype(F32)
        triu = (s_i >= t_i).astype(F32)
        last_row = lax.broadcasted_iota(jnp.int32, (HG_CHUNK, 1), 0) == HG_CHUNK - 1
        lbv, gn = lb_ref[...], gain_ref[...]
        dlb_acc = jnp.zeros((1, dh), F32)
        dgn_acc = jnp.zeros((1, dh), F32)
        for c in reversed(range(ncb)):
            rs = slice(c * HG_CHUNK, (c + 1) * HG_CHUNK)
            qv, flv, gv = q_ref[rs, :], fl_ref[rs, :], g_ref[rs, :]
            pr = _hg_parts(qv, flv, lbv, masks, tri)
            vb = i_ref[rs, :].astype(BF16)
            stv = st_ref[c]
            stb = stv.astype(BF16)
            dsv = dst[...]
            dsb = dsv.astype(BF16)
            bc, kk, qf, ab = pr["bc"], pr["kk"], pr["qf"], pr["a"].astype(BF16)
            ebc = jnp.exp(bc)
            qt = qf * ebc
            qtb = qt.astype(BF16)
            o = jnp.dot(ab, vb, preferred_element_type=F32) + lax.dot_general(qtb, stb, _NT, preferred_element_type=F32)
            r = lax.rsqrt(jnp.mean(o * o, axis=-1, keepdims=True) + EPS)
            oh = o * r
            sgg = _sigmoid(gv)
            dyv = dy_ref[rs, :].astype(F32)
            don = dyv * (gv * sgg)
            dgate = dyv * (oh * gn) * (sgg * (1.0 + gv * (1.0 - sgg)))
            dgn_acc = dgn_acc + jnp.sum(don * oh, axis=0, keepdims=True)
            doh = don * gn
            do = r * (doh - oh * jnp.mean(doh * oh, axis=-1, keepdims=True))
            dob = do.astype(BF16)
            bl = bc[HG_CHUNK - 1:HG_CHUNK, :]
            ebl = jnp.exp(bl)
            ekh = jnp.exp(bl - bc)
            khat = kk * ekh
            dv = (lax.dot_general(ab, dob, _TN, preferred_element_type=F32)
                  + lax.dot_general(khat.astype(BF16), dsb, _NT, preferred_element_type=F32))
            da = lax.dot_general(dob, vb, _NT, preferred_element_type=F32)
            dqt = jnp.dot(dob, stb, preferred_element_type=F32)
            dkh = jnp.dot(vb, dsb, preferred_element_type=F32)
            dst[...] = dsv * ebl + lax.dot_general(dob, qtb, _TN, preferred_element_type=F32)
            dbl = jnp.sum(dsv * stv, axis=0, keepdims=True) * ebl + jnp.sum(dkh * khat, axis=0, keepdims=True)
            dqf = dqt * ebc
            dkk = dkh * ekh
            dbc = dqt * qt - dkh * khat
            for li, (eq, ek, qtl, ktl) in enumerate(pr["levels"]):
                gm = jnp.where(masks[li], da, 0.0).astype(BF16)
                qtr, ktr = qtl.astype(BF16), ktl.astype(BF16)
                dql = jnp.dot(gm, ktr, preferred_element_type=F32)
                dkl = lax.dot_general(gm, qtr, _TN, preferred_element_type=F32)
                dqf = dqf + dql * eq
                dkk = dkk + dkl * ek
                dbc = dbc + qtr.astype(F32) * dql - ktr.astype(F32) * dkl
            dbc = dbc + jnp.where(last_row, dbl, 0.0)
            dlogf = jnp.dot(triu, dbc, preferred_element_type=F32, precision=_HI)
            df = dlogf / pr["f"] - dkk
            sgf = pr["sgf"]
            dfl = df * (1.0 - lbv) * sgf * (1.0 - sgf)
            dlb_acc = dlb_acc + jnp.sum(df * (1.0 - sgf), axis=0, keepdims=True)
            sgq = pr["sgq"]
            dq = dqf * (HG_HEAD ** -0.5) * (sgq * (1.0 + qv * (1.0 - sgq)))
            dz_ref[0, rs, :] = dq.astype(BF16)
            dz_ref[1, rs, :] = dfl.astype(BF16)
            dz_ref[2, rs, :] = dv.astype(BF16)
            dz_ref[3, rs, :] = dgate.astype(BF16)

        @pl.when(first)
        def _():
            dlb_ref[...] = dlb_acc
            dgn_ref[...] = dgn_acc

        @pl.when(jnp.logical_not(first))
        def _():
            dlb_ref[...] += dlb_acc
            dgn_ref[...] += dgn_acc

    def col(which):
        base = (mix_a + which * mix_b) // dh
        return pl.BlockSpec((bt, dh), lambda h, i: (nb - 1 - i, base + h))

    return _call(
        body, side, name=name, grid=(nh, nb),
        in_specs=[col(0), col(1), col(2), col(3),
                  pl.BlockSpec((None, bt, dh), lambda h, i: (1, nb - 1 - i, h)),
                  pl.BlockSpec((None, ncb, dh, dh), lambda h, i: (h, nb - 1 - i, 0, 0)),
                  pl.BlockSpec((1, dh), lambda h, i: (0, h)), pl.BlockSpec((1, dh), lambda h, i: (0, 0)), _ANY],
        out_specs=[pl.BlockSpec((4, bt, dh), lambda h, i: (0, nb - 1 - i, h)),
                   pl.BlockSpec((1, dh), lambda h, i: (0, h)),
                   pl.BlockSpec((None, 1, dh), lambda h, i: (h, 0, 0))],
        out_shape=[jax.ShapeDtypeStruct(dz5.shape, BF16), jax.ShapeDtypeStruct((1, mix_b), F32),
                   jax.ShapeDtypeStruct((nh, 1, dh), F32)],
        scratch_shapes=[pltpu.VMEM((dh, dh), F32)], aliases={8: 0}, sem=("parallel", "arbitrary"),
        args=(z, z, z, z, dcat, states, lb, gain, dz5))


def _fox_scores(qb, kb, fk, scale, masked):
    s = lax.dot_general(qb, kb, _NT, preferred_element_type=F32) * scale - fk
    if masked:
        n = s.shape[0]
        row = lax.broadcasted_iota(jnp.int32, (n, n), 0)
        col = lax.broadcasted_iota(jnp.int32, (n, n), 1)
        s = jnp.where(col <= row, s, -jnp.inf)
    return s


def fox_fwd(qkv, fk, *, side=None, name):
    _, t, d = qkv.shape
    nh = d // FOX_HEAD
    b = _blk(FOX_BLK, t)
    nb = t // b
    dh = FOX_HEAD
    scale = dh ** -0.5

    def body(q_ref, k_ref, v_ref, f_ref, o_ref, lse_ref):
        qi = pl.program_id(1)
        qb = q_ref[...]

        def step(kj, carry, masked):
            m, l, acc = carry
            off = pl.multiple_of(kj * b, b)
            s = _fox_scores(qb, k_ref[pl.ds(off, b), :], f_ref[kj], scale, masked)
            m_new = jnp.maximum(m, jnp.max(s, axis=-1, keepdims=True))
            alpha = jnp.exp(m - m_new)
            p = jnp.exp(s - m_new)
            l = alpha * l + jnp.sum(p, axis=-1, keepdims=True)
            acc = alpha * acc + jnp.dot(p.astype(BF16), v_ref[pl.ds(off, b), :], preferred_element_type=F32)
            return m_new, l, acc

        init = (jnp.full((b, 1), -jnp.inf, F32), jnp.zeros((b, 1), F32), jnp.zeros((b, dh), F32))
        carry = lax.fori_loop(0, qi, lambda kj, c: step(kj, c, False), init)
        m, l, acc = step(qi, carry, True)
        o_ref[...] = (acc / l).astype(BF16)
        lse_ref[...] = m + jnp.log(l)

    return _call(
        body, side, name=name, grid=(nh, nb),
        in_specs=[pl.BlockSpec((None, b, dh), lambda h, i: (0, i, h)),
                  pl.BlockSpec((None, t, dh), lambda h, i: (1, 0, h)),
                  pl.BlockSpec((None, t, dh), lambda h, i: (2, 0, h)),
                  pl.BlockSpec((None, nb, 1, b), lambda h, i: (h, 0, 0, 0))],
        out_specs=[pl.BlockSpec((b, dh), lambda h, i: (i, h)), pl.BlockSpec((None, b, 1), lambda h, i: (h, i, 0))],
        out_shape=[jax.ShapeDtypeStruct((t, d), BF16), jax.ShapeDtypeStruct((nh, t, 1), F32)],
        sem=("parallel", "parallel"), args=(qkv, qkv, qkv, fk))


def fox_bwd_dq(qkv, fk, do, lse, *, side=None, name):
    _, t, d = qkv.shape
    nh = d // FOX_HEAD
    b = _blk(FOX_BLK, t)
    nb = t // b
    dh = FOX_HEAD
    scale = dh ** -0.5

    def body(q_ref, k_ref, v_ref, f_ref, do_ref, lse_ref, dq_ref, dl_ref, p_buf, dp_buf):
        qi = pl.program_id(1)
        qb, dob, lse_v = q_ref[...], do_ref[...], lse_ref[...]

        def first(kj, dl, masked):
            off = pl.multiple_of(kj * b, b)
            p = jnp.exp(_fox_scores(qb, k_ref[pl.ds(off, b), :], f_ref[kj], scale, masked) - lse_v)
            dp = lax.dot_general(dob, v_ref[pl.ds(off, b), :], _NT, preferred_element_type=F32)
            p_buf[kj] = p
            dp_buf[kj] = dp
            return dl + jnp.sum(p * dp, axis=-1, keepdims=True)

        dl = lax.fori_loop(0, qi, lambda kj, c: first(kj, c, False), jnp.zeros((b, 1), F32))
        dl = first(qi, dl, True)
        dl_ref[...] = dl

        def second(kj, dq):
            off = pl.multiple_of(kj * b, b)
            dsv = p_buf[kj] * (dp_buf[kj] - dl)
            return dq + jnp.dot(dsv.astype(BF16), k_ref[pl.ds(off, b), :], preferred_element_type=F32)

        dq = lax.fori_loop(0, qi + 1, second, jnp.zeros((b, dh), F32))
        dq_ref[...] = (dq * scale).astype(BF16)

    col = pl.BlockSpec((None, b, 1), lambda h, i: (h, i, 0))
    return _call(
        body, side, name=name, grid=(nh, nb),
        in_specs=[pl.BlockSpec((None, b, dh), lambda h, i: (0, i, h)),
                  pl.BlockSpec((None, t, dh), lambda h, i: (1, 0, h)),
                  pl.BlockSpec((None, t, dh), lambda h, i: (2, 0, h)),
                  pl.BlockSpec((None, nb, 1, b), lambda h, i: (h, 0, 0, 0)),
                  pl.BlockSpec((b, dh), lambda h, i: (i, h)), col],
        out_specs=[pl.BlockSpec((None, b, dh), lambda h, i: (2, i, h)), col],
        out_shape=[jax.ShapeDtypeStruct((3, t, d), BF16), jax.ShapeDtypeStruct((nh, t, 1), F32)],
        scratch_shapes=[pltpu.VMEM((nb, b, b), F32), pltpu.VMEM((nb, b, b), F32)],
        sem=("parallel", "parallel"), args=(qkv, qkv, qkv, fk, do, lse))


def fox_bwd_dkv(qkv, fk, do, lse, delta, dqkv, *, side=None, name):
    _, t, d = qkv.shape
    nh = d // FOX_HEAD
    b = _blk(FOX_BLK, t)
    nb = t // b
    dh = FOX_HEAD
    scale = dh ** -0.5

    def body(q_ref, k_ref, v_ref, f_ref, do_ref, lse_ref, dl_ref, dz_in, dkv_ref, df_ref):
        del dz_in
        kj = pl.program_id(1)
        kb, vb, fkv = k_ref[...], v_ref[...], f_ref[...]

        def step(qi, carry, masked):
            dk, dv, df = carry
            off = pl.multiple_of(qi * b, b)
            qb, dob = q_ref[pl.ds(off, b), :], do_ref[pl.ds(off, b), :]
            p = jnp.exp(_fox_scores(qb, kb, fkv, scale, masked) - lse_ref[pl.ds(off, b), :])
            dv = dv + lax.dot_general(p.astype(BF16), dob, _TN, preferred_element_type=F32)
            dp = lax.dot_general(dob, vb, _NT, preferred_element_type=F32)
            dsv = p * (dp - dl_ref[pl.ds(off, b), :])
            dk = dk + lax.dot_general(dsv.astype(BF16), qb, _TN, preferred_element_type=F32)
            return dk, dv, df - jnp.sum(dsv, axis=0, keepdims=True)

        init = (jnp.zeros((b, dh), F32), jnp.zeros((b, dh), F32), jnp.zeros((1, b), F32))
        carry = step(kj, init, True)
        dk, dv, df = lax.fori_loop(kj + 1, nb, lambda qi, c: step(qi, c, False), carry)
        dkv_ref[0] = (dk * scale).astype(BF16)
        dkv_ref[1] = dv.astype(BF16)
        df_ref[...] = df

    col = pl.BlockSpec((None, t, 1), lambda h, j: (h, 0, 0))
    return _call(
        body, side, name=name, grid=(nh, nb),
        in_specs=[pl.BlockSpec((None, t, dh), lambda h, j: (0, 0, h)),
                  pl.BlockSpec((None, b, dh), lambda h, j: (1, j, h)),
                  pl.BlockSpec((None, b, dh), lambda h, j: (2, j, h)),
                  pl.BlockSpec((None, None, 1, b), lambda h, j: (h, j, 0, 0)),
                  pl.BlockSpec((t, dh), lambda h, j: (0, h)), col, col, pl.BlockSpec(memory_space=pl.ANY)],
        out_specs=[pl.BlockSpec((2, b, dh), lambda h, j: (0, j, h)),
                   pl.BlockSpec((None, None, 1, b), lambda h, j: (h, j, 0, 0))],
        out_shape=[jax.ShapeDtypeStruct((3, t, d), BF16), jax.ShapeDtypeStruct((nh, nb, 1, b), F32)],
        aliases={7: 0}, sem=("parallel", "parallel"), args=(qkv, qkv, qkv, fk, do, lse, delta, dqkv))


FL_BLK = 256


def _log_sigmoid(x):
    return jnp.minimum(x, 0.0) - jnp.log(1.0 + jnp.exp(-jnp.abs(x)))


def fl_fwd(zf, bf, *, name):
    t, n = zf.shape
    bt = _blk(FL_BLK, t)

    def body(z_ref, b_ref, o_ref, carry):
        @pl.when(pl.program_id(0) == 0)
        def _():
            carry[...] = jnp.zeros_like(carry)

        ls = _log_sigmoid(z_ref[...] + b_ref[...])
        r = lax.broadcasted_iota(jnp.int32, (bt, bt), 0)
        c = lax.broadcasted_iota(jnp.int32, (bt, bt), 1)
        cs = jnp.dot((c <= r).astype(F32), ls, preferred_element_type=F32, precision=_HI) + carry[...]
        o_ref[...] = cs
        carry[...] = cs[bt - 1:bt, :]

    return pl.pallas_call(
        body, name=name, grid=(t // bt,),
        in_specs=[pl.BlockSpec((bt, n), lambda i: (i, 0)), pl.BlockSpec((1, n), lambda i: (0, 0))],
        out_specs=pl.BlockSpec((bt, n), lambda i: (i, 0)), out_shape=jax.ShapeDtypeStruct((t, n), F32),
        scratch_shapes=[pltpu.VMEM((1, n), F32)], compiler_params=_params(("arbitrary",)),
    )(zf, bf)


def fl_bwd(df, zf, bf, *, name):
    t, n = zf.shape
    bt = _blk(FL_BLK, t)
    nb = t // bt

    def body(df_ref, z_ref, b_ref, dz_ref, db_ref, carry):
        first = pl.program_id(0) == 0

        @pl.when(first)
        def _():
            carry[...] = jnp.zeros_like(carry)

        r = lax.broadcasted_iota(jnp.int32, (bt, bt), 0)
        c = lax.broadcasted_iota(jnp.int32, (bt, bt), 1)
        dls = jnp.dot((c >= r).astype(F32), df_ref[...], preferred_element_type=F32, precision=_HI) + carry[...]
        carry[...] = dls[0:1, :]
        dz = dls * (1.0 - _sigmoid(z_ref[...] + b_ref[...]))
        dz_ref[...] = dz.astype(BF16)
        part = jnp.sum(dz, axis=0, keepdims=True)

        @pl.when(first)
        def _():
            db_ref[...] = part

        @pl.when(jnp.logical_not(first))
        def _():
            db_ref[...] += part

    row = pl.BlockSpec((bt, n), lambda i: (nb - 1 - i, 0))
    vec = pl.BlockSpec((1, n), lambda i: (0, 0))
    return pl.pallas_call(
        body, name=name, grid=(nb,), in_specs=[row, row, vec], out_specs=[row, vec],
        out_shape=[jax.ShapeDtypeStruct((t, n), BF16), jax.ShapeDtypeStruct((1, n), F32)],
        scratch_shapes=[pltpu.VMEM((1, n), F32)], compiler_params=_params(("arbitrary",)),
    )(df, zf, bf)


def _adamw_math(w, g, m, v):
    m = ADAM_B1 * m + (1.0 - ADAM_B1) * g
    v = ADAM_B2 * v + (1.0 - ADAM_B2) * (g * g)
    m_hat = m / (1.0 - ADAM_B1 ** ADAM_STEP)
    v_hat = v / (1.0 - ADAM_B2 ** ADAM_STEP)
    delta = -ADAM_LR * (m_hat / (jnp.sqrt(v_hat) + ADAM_EPS) + ADAM_WD * w)
    return delta, m, v


def adamw(w, g, m, v, *, name):
    r, c = w.shape
    br = _blk(256, r)

    def body(w_ref, g_ref, m_ref, v_ref, go_ref, d_ref, mo_ref, vo_ref):
        gv = g_ref[...]
        go_ref[...] = gv
        d_ref[...], mo_ref[...], vo_ref[...] = _adamw_math(w_ref[...], gv, m_ref[...], v_ref[...])

    spec = pl.BlockSpec((br, c), lambda i: (i, 0))
    return pl.pallas_call(
        body, name=name, grid=(r // br,), in_specs=[spec] * 4, out_specs=[spec] * 4,
        out_shape=[jax.ShapeDtypeStruct((r, c), F32)] * 4, compiler_params=_params(("parallel",)),
    )(w, g, m, v)


def _f2(a):
    return a.reshape(a.shape[-2:])


def _local_step(x0, mem, tgt, sp, plan):
    t, d = x0.shape
    mix_a = sp["pool_scale"].shape[1]
    small = {}

    def row(a, l):
        return a[l:l + 1]

    def rows4(g):
        return g.reshape(N_CHIPS, -1, g.shape[-1])

    def xattn_f(l, xin):
        w = plan.weights(f"xa{l}")
        hx = rmsnorm_fwd(xin, row(sp["xa_norm"], l), name=f"xa_norm_f{l}")
        q = _f2(matmul(vm2(hx), w["wq"], "nn", out_dtype=BF16, side=plan.take_fwd(), name=f"xa_q_f{l}"))
        mn = rmsnorm_fwd(mem, row(sp["xa_mem_norm"], l), name=f"xa_memnorm_f{l}")
        kv = _f2(matmul(vm2(mn), w["wkv"], "nn", out_dtype=BF16, name=f"xa_kv_f{l}"))
        o = xattn_fwd(q, kv, name=f"xa_attn_f{l}")
        xout = _f2(matmul(vm2(o), w["wo"], "nn", out_dtype=F32, res=vm2(xin), side=plan.take_fwd(), name=f"xa_o_f{l}"))
        return xout, (xin, hx, q, mn, kv, o)

    def ffn_f(l, xin):
        w = plan.weights(f"ffn{l}")
        hf = rmsnorm_fwd(xin, row(sp["ffn_norm"], l), name=f"ffn_norm_f{l}")
        a = _f2(matmul(vm2(hf), w["gate"], "nn", out_dtype=BF16, side=plan.take_fwd(), name=f"ffn_gate_f{l}"))
        b, s = matmul(vm2(hf), w["up"], "nn", out_dtype=BF16, epi=(_swiglu_epi, [vm2(a)], 2), side=plan.take_fwd(), name=f"ffn_up_f{l}")
        b, s = _f2(b), _f2(s)
        xout = _f2(matmul(vm2(s), w["down"], "nn", out_dtype=F32, res=vm2(xin), side=plan.take_fwd(), name=f"ffn_down_f{l}"))
        return xout, (xin, hf, a, b, s)

    ev = plan.weights("ev")
    h0 = rmsnorm_fwd(x0, sp["ev_norm"], name="ev_norm_f")
    z = _f2(matmul(vm2(h0), ev["ev_in"], "nn", out_dtype=F32, name="ev_in_f"))
    cat = pool_fwd(z, ev["pool"], sp["pool_scale"], name="pool_f")
    cat, states = hgrn_fwd(z, cat, sp["lb"], sp["hg_gain"], mix_a, side=plan.take_fwd(), name="hgrn_f")
    x1 = _f2(matmul(VM(cat), ev["ev_out"], "nn", out_dtype=F32, res=vm2(x0), side=plan.take_fwd(), name="ev_out_f"))
    x2, xa0 = xattn_f(0, x1)
    x3, ff0 = ffn_f(0, x2)

    od = plan.weights("od")
    ho = rmsnorm_fwd(x3, sp["od_norm"], name="od_norm_f")
    qkv = matmul(vm2(ho), od["wqkv"], "nn", out_dtype=BF16, out_p=3, side=plan.take_fwd(), name="od_qkv_f")
    zf = _f2(matmul(vm2(ho), od["wf"], "nn", out_dtype=F32, name="od_fl_f"))
    fcum = fl_fwd(zf, sp["bf"], name="od_forget_f")
    nh = d // FOX_HEAD
    nfb = t // _blk(FOX_BLK, t)
    fk = fcum[:, :nh].T.reshape(nh, nfb, 1, t // nfb)
    of, lse = fox_fwd(qkv, fk, side=plan.take_fwd(), name="fox_f")
    x4 = _f2(matmul(vm2(of), od["od_out"], "nn", out_dtype=F32, res=vm2(x3), name="od_out_f"))
    x5, xa1 = xattn_f(1, x4)
    x6, ff1 = ffn_f(1, x5)
    loss, dx, dxb, small["final_norm"] = loss_head(x6, sp["final_norm"], tgt, name="loss_head")

    def ffn_b(l, saved, dx, dxb):
        xin, hf, a, b, s = saved
        w = plan.weights(f"ffn{l}")
        da, db = matmul(vm2(dxb), w["down"], "nt", out_dtype=BF16, epi=(_swiglu_bwd_epi, [vm2(a), vm2(b)], 2), side=plan.take_bwd(), name=f"ffn_down_bx{l}")
        da, db = _f2(da), _f2(db)
        g_down = rows4(matmul(vm2(s), vm2(dxb), "tn", out_dtype=BF16, name=f"ffn_down_bw{l}"))
        g_gate = matmul(vm2(hf), vm2(da), "tn", out_dtype=BF16, out_p=N_CHIPS, name=f"ffn_gate_bw{l}")
        g_up = matmul(vm2(hf), vm2(db), "tn", out_dtype=BF16, out_p=N_CHIPS, name=f"ffn_up_bw{l}")
        plan.grads_done({f"down{l}": g_down, f"gate{l}": g_gate, f"up{l}": g_up})
        dh = matmul(vm2(da), w["gate"], "nt", out_dtype=F32, side=plan.take_bwd(), name=f"ffn_gate_bx{l}")
        dh = _f2(matmul(vm2(db), w["up"], "nt", out_dtype=F32, res=VM(dh), side=plan.take_bwd(), name=f"ffn_up_bx{l}"))
        dx, dxb, dg = rmsnorm_bwd(xin, row(sp["ffn_norm"], l), dh, dx, name=f"ffn_norm_b{l}")
        return dx, dxb, dg

    def xattn_b(l, saved, dx, dxb):
        xin, hx, q, mn, kv, o = saved
        w = plan.weights(f"xa{l}")
        do = _f2(matmul(vm2(dxb), w["wo"], "nt", out_dtype=BF16, side=plan.take_bwd(), name=f"xa_o_bx{l}"))
        g_wo = rows4(matmul(vm2(o), vm2(dxb), "tn", out_dtype=BF16, name=f"xa_o_bw{l}"))
        dq, dkv = xattn_bwd(q, kv, do, name=f"xa_attn_b{l}")
        g_wq = rows4(matmul(vm2(hx), vm2(dq), "tn", out_dtype=BF16, name=f"xa_q_bw{l}"))
        dh = _f2(matmul(vm2(dq), w["wq"], "nt", out_dtype=F32, name=f"xa_q_bx{l}"))
        dkvb = dkv.astype(BF16)
        g_wkv = matmul(vm2(mn), vm2(dkvb), "tn", out_dtype=BF16, out_p=N_CHIPS, name=f"xa_kv_bw{l}")
        plan.grads_done({f"wo{l}": g_wo, f"wq{l}": g_wq, f"wkv{l}": g_wkv})
        dmn = _f2(matmul(vm2(dkvb), w["wkv"], "nt", out_dtype=F32, side=plan.take_bwd(), name=f"xa_kv_bx{l}"))
        (dgm,) = rmsnorm_bwd(mem, row(sp["xa_mem_norm"], l), dmn, None, name=f"xa_memnorm_b{l}")
        dx, dxb, dg = rmsnorm_bwd(xin, row(sp["xa_norm"], l), dh, dx, name=f"xa_norm_b{l}")
        return dx, dxb, dg, dgm

    dg_ffn, dg_xa, dg_mem = [None, None], [None, None], [None, None]
    dx, dxb, dg_ffn[1] = ffn_b(1, ff1, dx, dxb)
    dx, dxb, dg_xa[1], dg_mem[1] = xattn_b(1, xa1, dx, dxb)

    do = _f2(matmul(vm2(dxb), od["od_out"], "nt", out_dtype=BF16, side=plan.take_bwd(), name="od_out_bx"))
    g_od_out = rows4(matmul(vm2(of), vm2(dxb), "tn", out_dtype=BF16, name="od_out_bw"))
    dz3, delta = fox_bwd_dq(qkv, fk, do, lse, side=plan.take_bwd(1), name="fox_bq")
    dz3, dfk = fox_bwd_dkv(qkv, fk, do, lse, delta, dz3, side=plan.take_bwd(1), name="fox_bkv")
    dfc = jnp.pad(dfk.reshape(nh, t).T, ((0, 0), (0, zf.shape[1] - nh)))
    dzf, dbf = fl_bwd(dfc, zf, sp["bf"], name="od_forget_b")
    dqkv = VM(dz3, "cs", pfn=lambda p: lax.rem(p + 2, 3))
    dwqkv = _f2(matmul(vm2(ho), dqkv, "tn", out_dtype=BF16, name="od_qkv_bw"))
    dwf = _f2(matmul(vm2(ho), vm2(dzf), "tn", out_dtype=BF16, name="od_fl_bw"))
    od_in_full = jnp.concatenate([dwqkv, dwf[:, :nh]], axis=1)
    plan.grads_done({"od_out": g_od_out, "od_in": od_in_full.reshape(d, N_CHIPS, -1).transpose(1, 0, 2)})
    dh = matmul(dqkv, od["wqkv"], "nt", out_dtype=F32, side=plan.take_bwd(), name="od_qkv_bx")
    dh = _f2(matmul(vm2(dzf), od["wf"], "nt", out_dtype=F32, res=VM(dh), name="od_fl_bx"))
    dx, dxb, small["od_norm"] = rmsnorm_bwd(x3, sp["od_norm"], dh, dx, name="od_norm_b")
    small["bf"] = dbf

    dx, dxb, dg_ffn[0] = ffn_b(0, ff0, dx, dxb)
    dx, dxb, dg_xa[0], dg_mem[0] = xattn_b(0, xa0, dx, dxb)

    dcat = matmul(vm2(dxb), ev["ev_out"], "nt", out_dtype=BF16, out_p=2, side=plan.take_bwd(), name="ev_out_bx")
    g_ev_out = rows4(matmul(VM(cat), vm2(dxb), "tn", out_dtype=BF16, name="ev_out_bw"))
    dz5, g_pool, small["pool_scale"] = pool_bwd(z, dcat, ev["pool"], sp["pool_scale"], name="pool_b")
    dz5, small["lb"], dgn = hgrn_bwd(z, dcat, dz5, states, sp["lb"], sp["hg_gain"], mix_a, side=plan.take_bwd(2), name="hgrn_b")
    small["hg_gain"] = jnp.sum(dgn, axis=0)
    dzv = VM(dz5, "cs", pfn=lambda p: lax.rem(p + 4, 5))
    g_ev_in = _f2(matmul(vm2(h0), dzv, "tn", out_dtype=BF16, side=plan.take_bwd(1), name="ev_in_bw"))
    g_ev_in = g_ev_in.reshape(d, N_CHIPS, -1).transpose(1, 0, 2)
    ng, gsz = g_pool.shape[0], g_pool.shape[1]
    pool_parts = g_pool.reshape(ng, N_CHIPS, gsz // N_CHIPS, gsz).transpose(1, 0, 2, 3).reshape(N_CHIPS, gsz, gsz).astype(BF16)
    plan.grads_done({"ev_out": g_ev_out, "pool": pool_parts, "ev_in": g_ev_in}, now=True)
    dh = _f2(matmul(dzv, ev["ev_in"], "nt", out_dtype=F32, side=plan.take_bwd(1), name="ev_in_bx"))
    dx, _, small["ev_norm"] = rmsnorm_bwd(x0, sp["ev_norm"], dh, dx, name="ev_norm_b")

    small["xa_norm"] = jnp.concatenate(dg_xa, axis=0)
    small["xa_mem_norm"] = jnp.concatenate(dg_mem, axis=0)
    small["ffn_norm"] = jnp.concatenate(dg_ffn, axis=0)
    return loss, dx, small


def gather_forward(fulls, *, name):
    n = len(fulls)

    def body(*refs):
        outs = refs[n:2 * n]
        ssem, rsem = refs[2 * n:]
        x, y, c, chips = _me()
        sibling = (x, y, 1 - c)

        def rows(w, j, which):
            h = fulls[w].shape[1] // 2
            return outs[w].at[_chip_id(chips[j]), pl.ds(which * h, h)]

        def swap(w, j):
            return _rcopy(rows(w, j, c), rows(w, j, c), ssem.at[w, j], rsem.at[w, j], sibling)

        for w in range(n):
            for j in range(3):
                swap(w, j).start()
        for w in range(n):
            for j in range(3):
                swap(w, j).wait_send()
                _rcopy(rows(w, j, 1 - c), rows(w, j, 1 - c), ssem.at[w, j], rsem.at[w, j], sibling).wait_recv()

    return pl.pallas_call(
        body, name=name, in_specs=[_ANY] * n, out_specs=[_ANY] * n,
        out_shape=[jax.ShapeDtypeStruct(f.shape, f.dtype) for f in fulls], input_output_aliases={w: w for w in range(n)},
        scratch_shapes=[pltpu.SemaphoreType.DMA((n, 3)), pltpu.SemaphoreType.DMA((n, 3))],
    )(*fulls)


def gather_shards(shards, *, name):
    n = len(shards)

    def body(*refs):
        ins, outs = refs[:n], refs[n:2 * n]
        ssem, rsem = refs[2 * n:]
        x, y, c, chips = _me()
        mine = _chip_id((x, y))
        sibling = (x, y, 1 - c)

        def rows(w, chip_id, which):
            h = shards[w].shape[0] // 2
            return outs[w].at[chip_id, pl.ds(which * h, h)]

        def to_chip(w, j):
            h = shards[w].shape[0] // 2
            return _rcopy(ins[w].at[pl.ds(c * h, h)], rows(w, mine, c), ssem.at[w, j], rsem.at[w, j], (*chips[j], c))

        def from_chip(w, j):
            r = rows(w, _chip_id(chips[j]), c)
            return _rcopy(r, r, ssem.at[w, j], rsem.at[w, j], (*chips[j], c))

        def to_sibling(w, j):
            r = rows(w, _chip_id(chips[j]), c)
            return _rcopy(r, r, ssem.at[w, 3 + j], rsem.at[w, 3 + j], sibling)

        def from_sibling(w, j):
            r = rows(w, _chip_id(chips[j]), 1 - c)
            return _rcopy(r, r, ssem.at[w, 3 + j], rsem.at[w, 3 + j], sibling)

        for w in range(n):
            for j in range(3):
                to_chip(w, j).start()
        for w in range(n):
            for j in range(3):
                from_chip(w, j).wait_recv()
                to_sibling(w, j).start()
        for w in range(n):
            for j in range(3):
                from_sibling(w, j).wait_recv()
        for w in range(n):
            for j in range(3):
                to_chip(w, j).wait_send()
                to_sibling(w, j).wait_send()

    return pl.pallas_call(
        body, name=name, in_specs=[_ANY] * n, out_specs=[_ANY] * n,
        out_shape=[jax.ShapeDtypeStruct((N_CHIPS,) + s.shape, s.dtype) for s in shards],
        scratch_shapes=[pltpu.SemaphoreType.DMA((n, 6)), pltpu.SemaphoreType.DMA((n, 6))],
    )(*shards)


def _ids_spec(grid, in_specs, out_specs):
    return pltpu.PrefetchScalarGridSpec(num_scalar_prefetch=1, grid=grid, in_specs=in_specs, out_specs=out_specs)


def fill_own(full, shard, ids, *, name):
    r, c = shard.shape
    br = _row_blk(r, 512)

    def body(ids_ref, s_ref, f_in, o_ref):
        del ids_ref, f_in
        o_ref[...] = s_ref[...]

    return pl.pallas_call(
        body, name=name, out_shape=jax.ShapeDtypeStruct(full.shape, full.dtype), input_output_aliases={2: 0},
        grid_spec=_ids_spec((r // br,), [pl.BlockSpec((br, c), lambda i, ids: (i, 0)), _ANY],
                            pl.BlockSpec((None, br, c), lambda i, ids: (ids[0], i, 0))),
        compiler_params=_params(("parallel",)),
    )(ids, shard, full)


def rs_pair(parts, *, name):
    n = len(parts)

    def body(*refs):
        ins, recv = refs[:n], refs[n:2 * n]
        ssem, rsem = refs[2 * n:]
        x, y, c, _ = _me()
        sibling = (x, y, 1 - c)

        def swap(w):
            h = parts[w].shape[1] // 2
            return _rcopy(ins[w].at[:, pl.ds((1 - c) * h, h), :], recv[w], ssem.at[w], rsem.at[w], sibling)

        for w in range(n):
            swap(w).start()
        for w in range(n):
            swap(w).wait()

    return pl.pallas_call(
        body, name=name, in_specs=[_ANY] * n, out_specs=[_ANY] * n,
        out_shape=[jax.ShapeDtypeStruct((p.shape[0], p.shape[1] // 2, p.shape[2]), p.dtype) for p in parts],
        scratch_shapes=[pltpu.SemaphoreType.DMA((n,)), pltpu.SemaphoreType.DMA((n,))],
    )(*parts)


def add_pair(part, recv, ids, *, name):
    p, h, c = recv.shape
    br = _row_blk(h, 512)
    nb = h // br

    def body(ids_ref, a_ref, b_ref, o_ref):
        del ids_ref
        o_ref[...] = (a_ref[...].astype(F32) + b_ref[...].astype(F32)).astype(o_ref.dtype)

    half = pl.BlockSpec((None, br, c), lambda k, i, ids: (k, i, 0))
    return pl.pallas_call(
        body, name=name, out_shape=jax.ShapeDtypeStruct(recv.shape, recv.dtype),
        grid_spec=_ids_spec((p, nb), [pl.BlockSpec((None, br, c), lambda k, i, ids: (k, ids[1] * nb + i, 0)), half], half),
        compiler_params=_params(("parallel", "parallel")),
    )(ids, part, recv)


def rs_chip(sums, *, name):
    n = len(sums)

    def body(*refs):
        ins, outs = refs[:n], refs[n:2 * n]
        ssem, rsem = refs[2 * n:]
        x, y, c, chips = _me()

        def swap(w, j):
            return _rcopy(ins[w].at[_chip_id(chips[j])], outs[w].at[j], ssem.at[w, j], rsem.at[w, j], (*chips[j], c))

        for w in range(n):
            for j in range(3):
                swap(w, j).start()
        for w in range(n):
            for j in range(3):
                swap(w, j).wait()

    return pl.pallas_call(
        body, name=name, in_specs=[_ANY] * n, out_specs=[_ANY] * n,
        out_shape=[jax.ShapeDtypeStruct((3,) + s.shape[1:], s.dtype) for s in sums],
        scratch_shapes=[pltpu.SemaphoreType.DMA((n, 3)), pltpu.SemaphoreType.DMA((n, 3))],
    )(*sums)


def add_chips(sums, landed, ids, group, layer, group_shape, *, name):
    _, h, c = sums.shape
    br = _row_blk(h, 256)
    nb = h // br

    def body(ids_ref, a_ref, b_ref, *rest):
        o_ref = rest[-1]
        tot = a_ref[...].astype(F32)
        for k in range(3):
            tot = tot + b_ref[k].astype(F32)
        o_ref[...] = tot

    in_specs = [pl.BlockSpec((None, br, c), lambda i, ids: (ids[0], i, 0)), pl.BlockSpec((3, br, c), lambda i, ids: (0, i, 0))]
    args = [ids, sums, landed]
    if group is not None:
        in_specs.append(_ANY)
        args.append(group)
    return pl.pallas_call(
        body, name=name, out_shape=jax.ShapeDtypeStruct(group_shape, F32),
        input_output_aliases={3: 0} if group is not None else {},
        grid_spec=_ids_spec((nb,), in_specs, pl.BlockSpec((None, br, c), lambda i, ids: (layer, ids[1] * nb + i, 0))),
        compiler_params=_params(("parallel",)),
    )(*args)


def rs_share(groups, slots, *, name):
    ng = len(groups)
    n = len(slots)

    def body(*refs):
        outs = refs[ng:2 * ng]
        ssem, rsem = refs[2 * ng:]
        x, y, c, _ = _me()
        sibling = (x, y, 1 - c)

        def rows(w, which):
            g, l = slots[w]
            h = groups[g].shape[1] // 2
            return outs[g].at[l, pl.ds(which * h, h), :]

        def swap(w):
            return _rcopy(rows(w, c), rows(w, c), ssem.at[w], rsem.at[w], sibling)

        for w in range(n):
            swap(w).start()
        for w in range(n):
            swap(w).wait_send()
            _rcopy(rows(w, 1 - c), rows(w, 1 - c), ssem.at[w], rsem.at[w], sibling).wait_recv()

    return pl.pallas_call(
        body, name=name, in_specs=[_ANY] * ng, out_specs=[_ANY] * ng,
        out_shape=[jax.ShapeDtypeStruct(g.shape, g.dtype) for g in groups],
        input_output_aliases={g: g for g in range(ng)},
        scratch_shapes=[pltpu.SemaphoreType.DMA((n,)), pltpu.SemaphoreType.DMA((n,))],
    )(*groups)


def allreduce_small(v, *, name):
    r, c = v.shape
    ndev = 2 * N_CHIPS

    def body(v_ref, o_ref, buf, ssem, rsem):
        x, y, cc, _ = _me()
        me = 4 * x + 2 * y + cc
        flips = [(a, b, d) for a in (0, 1) for b in (0, 1) for d in (0, 1)][1:]
        buf[me] = v_ref[...]
        cps = []
        for k, (a, b, d) in enumerate(flips):
            peer = (jnp.bitwise_xor(x, a), jnp.bitwise_xor(y, b), jnp.bitwise_xor(cc, d))
            cp = _rcopy(v_ref, buf.at[me], ssem.at[k], rsem.at[k], peer)
            cp.start()
            cps.append(cp)
        for k, (a, b, d) in enumerate(flips):
            peer = (jnp.bitwise_xor(x, a), jnp.bitwise_xor(y, b), jnp.bitwise_xor(cc, d))
            src = 4 * peer[0] + 2 * peer[1] + peer[2]
            _rcopy(v_ref, buf.at[src], ssem.at[k], rsem.at[k], peer).wait_recv()
        for cp in cps:
            cp.wait_send()
        tot = buf[0]
        for k in range(1, ndev):
            tot = tot + buf[k]
        o_ref[...] = tot

    vm = pl.BlockSpec(memory_space=pltpu.VMEM)
    return pl.pallas_call(
        body, name=name, in_specs=[vm], out_specs=vm, out_shape=jax.ShapeDtypeStruct((r, c), F32),
        scratch_shapes=[pltpu.VMEM((ndev, r, c), F32), pltpu.SemaphoreType.DMA((ndev - 1,)), pltpu.SemaphoreType.DMA((ndev - 1,))],
    )(v)


WEIGHTS = ["lb_table", "ev_norm", "ev_w_in", "ev_w_pool", "ev_pool_scale", "ev_hg_norm", "ev_w_out", "od_norm", "od_w_in",
           "od_b_f", "od_w_out", "xa_norm", "xa_mem_norm", "xa_wq", "xa_wkv", "xa_wo", "ffn_norm", "ffn_w_gate", "ffn_w_up",
           "ffn_w_down", "final_norm"]
BIG = ["ev_w_in", "ev_w_pool", "ev_w_out", "od_w_in", "od_w_out", "xa_wq", "xa_wkv", "xa_wo", "ffn_w_gate", "ffn_w_up", "ffn_w_down"]
SMALL_ROWS = 16


def _rows(parts, width):
    rows = [jnp.pad(p.reshape(-1, p.shape[-1]).astype(F32), ((0, 0), (0, width - p.shape[-1]))) for p in parts]
    out = jnp.concatenate(rows, axis=0)
    return jnp.pad(out, ((0, SMALL_ROWS - out.shape[0]), (0, 0)))


def _unrows(packed, like):
    out, r = [], 0
    for p in like:
        n = p.size // p.shape[-1]
        out.append(packed[r:r + n, :p.shape[-1]].reshape(p.shape))
        r += n
    return out


def _m3(a):
    return a.reshape(a.shape[0], -1, a.shape[-1])


SLOT = {"ev_in": ("ev_w_in", 0), "pool": ("ev_w_pool", 0), "ev_out": ("ev_w_out", 0), "od_in": ("od_w_in", 0),
        "od_out": ("od_w_out", 0)}
for _l in range(2):
    SLOT.update({f"wq{_l}": ("xa_wq", _l), f"wkv{_l}": ("xa_wkv", _l), f"wo{_l}": ("xa_wo", _l),
                 f"gate{_l}": ("ffn_w_gate", _l), f"up{_l}": ("ffn_w_up", _l), f"down{_l}": ("ffn_w_down", _l)})
GATHER_FIRST = ["ev_in", "ev_out", "pool", "od_norm"]
GATHER_CARRIED = [["wq0", "wo0", "wkv0"], ["gate0"], ["up0"], ["down0"], ["od_in"], ["od_out"], ["wq1", "wo1", "wkv1"],
                  ["gate1"], ["up1", "down1"]]


class _Lazy:
    def __init__(self, plan, group):
        self.plan, self.layer = plan, group[-1] if group[-1] in "01" else ""

    def __getitem__(self, key):
        return self.plan.w(key + self.layer if key in ("wq", "wo", "wkv", "gate", "up", "down") else key)


class _Plan:
    def __init__(self, shards, ids, group_shapes, d, nh):
        self.shards, self.ids, self.group_shapes, self.d, self.nh = shards, ids, group_shapes, d, nh
        self.full, self.cache = {}, {}
        self.queue, self.sides, self.fsides, self.forwarded = [list(u) for u in GATHER_CARRIED], [], [], set()
        self.parts, self.psides, self.sums, self.rqueue, self.rsides = [], [], {}, [], []
        got = gather_shards([shards[n] for n in GATHER_FIRST], name="gather_first")
        for n, f in zip(GATHER_FIRST, got):
            self.full[n] = fill_own(f, shards[n], ids, name=f"gather_own_{n}")

    def take_fwd(self):
        parts = []
        ready = [(ns, s) for ns, s in self.sides if s.outs is not None and ns[0] not in self.forwarded]
        for ns, s in ready:
            fs = ForwardSide(s.outs)
            self.fsides.append((ns, fs))
            self.forwarded.update(ns)
            parts.append(fs)
        if self.queue:
            names = self.queue.pop(0)
            side = GatherSide([self.shards[n] for n in names])
            self.sides.append((names, side))
            parts.append(side)
        return Sides(parts) if parts else None

    def _need(self, names):
        missing = [n for n in names if n not in self.full]
        if not missing:
            return
        done = {n: a for ns, s in self.fsides if s.outs is not None for n, a in zip(ns, s.outs)}
        landed = {n: a for ns, s in self.sides if s.outs is not None for n, a in zip(ns, s.outs)}
        pre = {n: done[n] for n in missing if n in done}
        half = [n for n in missing if n not in done and n in landed]
        late = [n for n in missing if n not in done and n not in landed]
        if half:
            self.forwarded.update(half)
            pre.update(zip(half, gather_forward([landed[n] for n in half], name=f"gather_forward_{half[0]}")))
        if late:
            self.queue = [u for u in ([n for n in u if n not in late] for u in self.queue) if u]
            pre.update(zip(late, gather_shards([self.shards[n] for n in late], name=f"gather_late_{late[0]}")))
        for n in missing:
            self.full[n] = fill_own(pre[n], self.shards[n], self.ids, name=f"gather_own_{n}")

    def w(self, name):
        if name in self.cache:
            return self.cache[name]
        if name in ("wqkv", "wf"):
            self._need(["od_in"])
            od_full = self.full["od_in"].transpose(1, 0, 2).reshape(self.d, -1)
            self.cache["wqkv"] = vm2(od_full[:, :3 * self.d])
            self.cache["wf"] = vm2(jnp.pad(od_full[:, 3 * self.d:], ((0, 0), (0, 128 - self.nh))))
            return self.cache[name]
        self._need([name])
        f = self.full[name]
        if name == "pool":
            rows, gsz = f.shape[1:]
            ng = rows * N_CHIPS // gsz
            out = f.reshape(N_CHIPS, ng, gsz // N_CHIPS, gsz).transpose(1, 0, 2, 3).reshape(ng, gsz, gsz)
        elif name == "ev_in":
            out = vm2(f.transpose(1, 0, 2).reshape(self.d, -1))
        else:
            out = VM(f, "cs") if name.rstrip("01") in ("wkv", "gate", "up") else vm2(f.reshape(-1, f.shape[-1]))
        self.cache[name] = out
        return out

    def weights(self, group):
        return _Lazy(self, group)

    def grads_done(self, parts, now=False):
        names = list(parts)
        if now:
            got = rs_pair([parts[n] for n in names], name=f"reduce_pair_{names[0]}")
            for n, g in zip(names, got):
                self.sums[n] = add_pair(parts[n], g, self.ids, name=f"reduce_add2_{n}")
            self.rqueue.append(names)
        else:
            self.parts.append((names, [parts[n] for n in names]))

    def _add_swapped(self):
        for names, parts, side in self.psides:
            if side.outs is not None and names[0] not in self.sums:
                for n, p, g in zip(names, parts, side.outs):
                    self.sums[n] = add_pair(p, g, self.ids, name=f"reduce_add2_{n}")
                self.rqueue.append(names)

    def take_bwd(self, units=0):
        self._add_swapped()
        sides = []
        for names, parts in self.parts:
            ps = PairSide(parts)
            self.psides.append((names, parts, ps))
            sides.append(ps)
        self.parts = []
        names = [n for u in self.rqueue[:units] for n in u]
        self.rqueue = self.rqueue[units:]
        if names:
            rs = ReduceSide([self.sums[n] for n in names])
            self.rsides.append((names, rs))
            sides.append(rs)
        return Sides(sides) if sides else None

    def finish(self):
        for names, parts in self.parts:
            self.grads_done(dict(zip(names, parts)), now=True)
        self._add_swapped()
        landed = {}
        for ns, side in self.rsides:
            landed.update(zip(ns, side.outs))
        rest = [n for u in self.rqueue for n in u]
        if rest:
            landed.update(zip(rest, rs_chip([self.sums[n] for n in rest], name="reduce_chips_rest")))
        gbig = {n: None for n in BIG}
        for n, (big, l) in SLOT.items():
            gbig[big] = add_chips(self.sums[n], landed[n], self.ids, gbig[big], l, self.group_shapes[big], name=f"reduce_add4_{n}")
        full = rs_share([gbig[n] for n in BIG], [(BIG.index(big), l) for big, l in SLOT.values()], name="reduce_share")
        return dict(zip(BIG, full))


def kernel(x, mem, lb_table, ev_norm, ev_w_in, ev_w_pool, ev_pool_scale, ev_hg_norm, ev_w_out, od_norm, od_w_in, od_b_f, od_w_out, xa_norm, xa_mem_norm, xa_wq, xa_wkv, xa_wo, ffn_norm, ffn_w_gate, ffn_w_up, ffn_w_down, final_norm, loss_target, m_lb_table, m_ev_norm, m_ev_w_in, m_ev_w_pool, m_ev_pool_scale, m_ev_hg_norm, m_ev_w_out, m_od_norm, m_od_w_in, m_od_b_f, m_od_w_out, m_xa_norm, m_xa_mem_norm, m_xa_wq, m_xa_wkv, m_xa_wo, m_ffn_norm, m_ffn_w_gate, m_ffn_w_up, m_ffn_w_down, m_final_norm, v_lb_table, v_ev_norm, v_ev_w_in, v_ev_w_pool, v_ev_pool_scale, v_ev_hg_norm, v_ev_w_out, v_od_norm, v_od_w_in, v_od_b_f, v_od_w_out, v_xa_norm, v_xa_mem_norm, v_xa_wq, v_xa_wkv, v_xa_wo, v_ffn_norm, v_ffn_w_gate, v_ffn_w_up, v_ffn_w_down, v_final_norm):
    a = dict(locals())
    w = {n: a[n] for n in WEIGHTS}
    mom = {n: a["m_" + n] for n in WEIGHTS}
    var = {n: a["v_" + n] for n in WEIGHTS}
    _, t, d = x.shape
    nh = d // FOX_HEAD
    lanes = 128
    cx, cy = lax.axis_index("x"), lax.axis_index("y")
    chip = 2 * cx + cy

    w3 = {n: _m3(w[n]) for n in BIG}
    flat = lambda v: v.reshape(-1, v.shape[-1])
    shards = {"od_norm": jnp.broadcast_to(od_norm, (16, od_norm.shape[1]))}
    for name, (big, l) in SLOT.items():
        shards[name] = w3[big][l].astype(BF16)
    ids = jnp.stack([chip, lax.axis_index("c")]).astype(jnp.int32)
    plan = _Plan(shards, ids, {n: w3[n].shape for n in BIG}, d, nh)
    od_norm_full = plan.full["od_norm"][:, 0, :].reshape(1, d)

    sm = jax.nn.softmax(lb_table, axis=0)
    sp = {
        "lb": sm[1:2], "ev_norm": ev_norm, "pool_scale": ev_pool_scale, "hg_gain": ev_hg_norm, "od_norm": od_norm_full,
        "bf": jnp.pad(od_b_f, ((0, 0), (0, lanes - nh))), "xa_norm": xa_norm, "xa_mem_norm": xa_mem_norm, "ffn_norm": ffn_norm,
        "final_norm": final_norm.reshape(1, d),
    }
    loss_l, gx, small = _local_step(x[0], mem[0], loss_target[0], sp, plan)
    loss = lax.psum(loss_l[0, 0], ("x", "y", "c"))
    gbig = plan.finish()

    raw_like = [small["lb"], small["ev_norm"], small["pool_scale"], small["hg_gain"], small["od_norm"], small["bf"],
                small["xa_norm"], small["xa_mem_norm"], small["ffn_norm"], small["final_norm"]]
    summed = _unrows(allreduce_small(_rows(raw_like, d), name="reduce_small"), raw_like)
    dlb, g_ev_norm, g_pool_scale, g_hg, g_od_norm_full, g_bf, g_xa, g_xam, g_ffn, g_final = summed
    dsm = jnp.zeros_like(sm).at[1:2].set(dlb)
    gsmall = {
        "lb_table": sm * (dsm - jnp.sum(sm * dsm, axis=0, keepdims=True)), "ev_norm": g_ev_norm, "ev_pool_scale": g_pool_scale,
        "ev_hg_norm": g_hg, "od_norm": lax.dynamic_slice_in_dim(g_od_norm_full, chip * od_norm.shape[1], od_norm.shape[1], axis=1),
        "od_b_f": g_bf[:, :nh], "xa_norm": g_xa, "xa_mem_norm": g_xam, "ffn_norm": g_ffn, "final_norm": g_final.reshape(d),
    }

    grad, delta, new_m, new_v = {}, {}, {}, {}
    for n in BIG:
        shp = w[n].shape
        res = adamw(flat(w3[n]), flat(gbig[n]), flat(_m3(mom[n])), flat(_m3(var[n])), name=f"adamw_{n}")
        grad[n], delta[n], new_m[n], new_v[n] = [r.reshape(shp) for r in res]
    snames = [n for n in WEIGHTS if n not in BIG]
    like = [w[n] for n in snames]
    res = adamw(_rows(like, d), _rows([gsmall[n] for n in snames], d), _rows([mom[n] for n in snames], d),
                _rows([var[n] for n in snames], d), name="adamw_small")
    for vals, dst in zip(res, (grad, delta, new_m, new_v)):
        dst.update(zip(snames, _unrows(vals, like)))
    return (loss, gx.reshape(x.shape), *[grad[n] for n in WEIGHTS], *[delta[n] for n in WEIGHTS],
            *[new_m[n] for n in WEIGHTS], *[new_v[n] for n in WEIGHTS])
```

```python
import functools
import math

import jax
import jax.numpy as jnp
from jax import lax
from jax.experimental import pallas as pl
from jax.experimental.pallas import tpu as pltpu

F32 = jnp.float32
BF16 = jnp.bfloat16
MESH = pl.DeviceIdType.MESH

V7X_VMEM_LIMIT_BYTES = 56 * 1024 * 1024
N_CHIPS = 4

EPS = 1e-6
POOL_WINDOWS = (2, 4, 8, 16)
POOL_HALO = 16
HG_HEAD = 128
HG_CHUNK = 64
FOX_HEAD = 128
FOX_BLK = 512
XA_HEADS = 4
ADAM_LR, ADAM_B1, ADAM_B2, ADAM_EPS, ADAM_WD, ADAM_STEP = 0.001, 0.9, 0.999, 1e-08, 0.01, 10
EXP_CLAMP = 80.0


def _params(sem=None):
    return pltpu.CompilerParams(dimension_semantics=sem, vmem_limit_bytes=V7X_VMEM_LIMIT_BYTES)


def _blk(pref, dim):
    b = min(pref, dim)
    assert dim % b == 0, (pref, dim)
    return b


class VM:
    def __init__(self, arr, kind="cs", lead=(), inner=(), pfn=None):
        self.arr, self.kind, self.lead, self.inner = arr, kind, tuple(lead), tuple(inner)
        self.pfn = pfn or (lambda p: p)
        p = arr.shape[len(self.lead)]
        r, c = arr.shape[-2:]
        assert arr.ndim == len(self.lead) + 1 + len(self.inner) + 2, (arr.shape, lead, inner)
        self.P = p
        self.shape = (r, c * p) if kind == "cs" else (r * p, c)
        self.dtype = arr.dtype

    def spec(self, br, bc, rfn, cfn):
        p = self.P
        r, c = self.arr.shape[-2:]
        assert c % bc == 0 and r % br == 0, (self.arr.shape, br, bc)
        if p == 1:
            def imap(*g):
                return (*self.lead, self.pfn(0), *self.inner, rfn(*g), cfn(*g))
        elif self.kind == "cs":
            per = c // bc

            def imap(*g):
                cb = cfn(*g)
                return (*self.lead, self.pfn(lax.div(cb, per)), *self.inner, rfn(*g), lax.rem(cb, per))
        else:
            per = r // br

            def imap(*g):
                rb = rfn(*g)
                return (*self.lead, self.pfn(lax.div(rb, per)), *self.inner, lax.rem(rb, per), cfn(*g))
        return pl.BlockSpec((None,) * (self.arr.ndim - 2) + (br, bc), imap)


def vm2(arr):
    return VM(arr.reshape((1,) + arr.shape))


def _out_struct(shape, kind, p, dtype):
    r, c = shape
    return jax.ShapeDtypeStruct((p, r, c // p) if kind == "cs" else (p, r // p, c), dtype)


_ANY = pl.BlockSpec(memory_space=pl.ANY)


def _me():
    x, y, c = lax.axis_index("x"), lax.axis_index("y"), lax.axis_index("c")
    chips = [(1 - x, y), (x, 1 - y), (1 - x, 1 - y)]
    return x, y, c, chips


def _chip_id(xy):
    return 2 * xy[0] + xy[1]


def _rcopy(src, dst, ssem, rsem, dev):
    return pltpu.make_async_remote_copy(src_ref=src, dst_ref=dst, send_sem=ssem, recv_sem=rsem, device_id=dev,
                                        device_id_type=MESH)


class GatherSide:
    def __init__(self, shards):
        self.inputs = list(shards)
        self.out_shape = [jax.ShapeDtypeStruct((N_CHIPS,) + s.shape, s.dtype) for s in shards]
        self.aliases = {}
        self.rows = len(shards)
        self.outs = None

    def _copy(self, ins, outs, ssem, rsem, w, j, receive):
        x, y, c, chips = _me()
        h = self.inputs[w].shape[0] // 2
        half = pl.ds(c * h, h)
        if receive:
            r = outs[w].at[_chip_id(chips[j]), half]
            return _rcopy(r, r, ssem.at[w, j], rsem.at[w, j], (*chips[j], c))
        return _rcopy(ins[w].at[half], outs[w].at[_chip_id((x, y)), half], ssem.at[w, j], rsem.at[w, j], (*chips[j], c))

    def start(self, ins, outs, ssem, rsem):
        for w in range(len(self.inputs)):
            for j in range(3):
                self._copy(ins, outs, ssem, rsem, w, j, False).start()

    def finish(self, ins, outs, ssem, rsem):
        for w in range(len(self.inputs)):
            for j in range(3):
                self._copy(ins, outs, ssem, rsem, w, j, True).wait_recv()
                self._copy(ins, outs, ssem, rsem, w, j, False).wait_send()


class ForwardSide:
    def __init__(self, fulls):
        self.inputs = list(fulls)
        self.out_shape = [jax.ShapeDtypeStruct(f.shape, f.dtype) for f in fulls]
        self.aliases = {w: w for w in range(len(fulls))}
        self.rows = len(fulls)
        self.outs = None

    def _copy(self, outs, ssem, rsem, w, j, receive):
        x, y, c, chips = _me()
        h = self.inputs[w].shape[1] // 2
        r = outs[w].at[_chip_id(chips[j]), pl.ds(((1 - c) if receive else c) * h, h)]
        return _rcopy(r, r, ssem.at[w, j], rsem.at[w, j], (x, y, 1 - c))

    def start(self, ins, outs, ssem, rsem):
        for w in range(self.rows):
            for j in range(3):
                self._copy(outs, ssem, rsem, w, j, False).start()

    def finish(self, ins, outs, ssem, rsem):
        for w in range(self.rows):
            for j in range(3):
                self._copy(outs, ssem, rsem, w, j, False).wait_send()
                self._copy(outs, ssem, rsem, w, j, True).wait_recv()


class PairSide:
    def __init__(self, parts):
        self.inputs = list(parts)
        self.out_shape = [jax.ShapeDtypeStruct((p.shape[0], p.shape[1] // 2, p.shape[2]), p.dtype) for p in parts]
        self.aliases = {}
        self.rows = len(parts)
        self.outs = None

    def _copy(self, ins, outs, ssem, rsem, w):
        x, y, c, _ = _me()
        h = self.inputs[w].shape[1] // 2
        return _rcopy(ins[w].at[:, pl.ds((1 - c) * h, h), :], outs[w], ssem.at[w, 0], rsem.at[w, 0], (x, y, 1 - c))

    def start(self, ins, outs, ssem, rsem):
        for w in range(self.rows):
            self._copy(ins, outs, ssem, rsem, w).start()

    def finish(self, ins, outs, ssem, rsem):
        for w in range(self.rows):
            self._copy(ins, outs, ssem, rsem, w).wait()


class _SemRows:
    def __init__(self, sem, off):
        self.sem, self.off = sem, off

    @property
    def at(self):
        return self

    def __getitem__(self, idx):
        return self.sem.at[self.off + idx[0], idx[1]]


class Sides:
    def __init__(self, sides):
        self.sides = list(sides)
        self.inputs = [a for s in self.sides for a in s.inputs]
        self.out_shape = [o for s in self.sides for o in s.out_shape]
        self.rows = sum(s.rows for s in self.sides)
        self.aliases, i0, o0 = {}, 0, 0
        for s in self.sides:
            self.aliases.update({i0 + i: o0 + o for i, o in s.aliases.items()})
            i0, o0 = i0 + len(s.inputs), o0 + len(s.out_shape)

    def _each(self, method, ins, outs, ssem, rsem):
        i0 = o0 = r0 = 0
        for s in self.sides:
            getattr(s, method)(ins[i0:i0 + len(s.inputs)], outs[o0:o0 + len(s.out_shape)], _SemRows(ssem, r0), _SemRows(rsem, r0))
            i0, o0, r0 = i0 + len(s.inputs), o0 + len(s.out_shape), r0 + s.rows

    def start(self, ins, outs, ssem, rsem):
        self._each("start", ins, outs, ssem, rsem)

    def finish(self, ins, outs, ssem, rsem):
        self._each("finish", ins, outs, ssem, rsem)

    @property
    def outs(self):
        return None

    @outs.setter
    def outs(self, vals):
        o0 = 0
        for s in self.sides:
            s.outs = list(vals[o0:o0 + len(s.out_shape)])
            o0 += len(s.out_shape)


class ReduceSide:
    def __init__(self, sums):
        self.inputs = list(sums)
        self.out_shape = [jax.ShapeDtypeStruct((3,) + s.shape[1:], s.dtype) for s in sums]
        self.aliases = {}
        self.rows = len(sums)
        self.outs = None

    def _copy(self, ins, outs, ssem, rsem, w, j):
        _, _, c, chips = _me()
        return _rcopy(ins[w].at[_chip_id(chips[j])], outs[w].at[j], ssem.at[w, j], rsem.at[w, j], (*chips[j], c))

    def start(self, ins, outs, ssem, rsem):
        for w in range(len(self.inputs)):
            for j in range(3):
                self._copy(ins, outs, ssem, rsem, w, j).start()

    def finish(self, ins, outs, ssem, rsem):
        for w in range(len(self.inputs)):
            for j in range(3):
                self._copy(ins, outs, ssem, rsem, w, j).wait()


def _call(body, side, *, name, grid, in_specs, out_specs, out_shape, scratch_shapes=(), sem, aliases=None, args):
    if side is None:
        return pl.pallas_call(body, name=name, grid=grid, in_specs=in_specs, out_specs=out_specs, out_shape=out_shape,
                              scratch_shapes=list(scratch_shapes), input_output_aliases=aliases or {},
                              compiler_params=_params(sem))(*args)
    single = not isinstance(out_shape, (list, tuple))
    oshape, ospecs = ([out_shape], [out_specs]) if single else (list(out_shape), list(out_specs))
    n_in, n_out, s_in, s_out = len(in_specs), len(oshape), len(side.inputs), len(side.out_shape)

    def wrapped(*refs):
        ins, sin = refs[:n_in], refs[n_in:n_in + s_in]
        outs = refs[n_in + s_in:n_in + s_in + n_out]
        souts = refs[n_in + s_in + n_out:n_in + s_in + n_out + s_out]
        rest = refs[n_in + s_in + n_out + s_out:]
        scratch, (ssem, rsem) = rest[:-2], rest[-2:]
        first = functools.reduce(jnp.logical_and, [pl.program_id(a) == 0 for a in range(len(grid))])
        last = functools.reduce(jnp.logical_and, [pl.program_id(a) == grid[a] - 1 for a in range(len(grid))])

        @pl.when(first)
        def _():
            side.start(sin, souts, ssem, rsem)

        body(*ins, *outs, *scratch)

        @pl.when(last)
        def _():
            side.finish(sin, souts, ssem, rsem)

    sems = pltpu.SemaphoreType.DMA((side.rows, 3))
    res = pl.pallas_call(
        wrapped, name=name, grid=grid, in_specs=list(in_specs) + [_ANY] * s_in, out_specs=ospecs + [_ANY] * s_out,
        out_shape=oshape + side.out_shape, scratch_shapes=list(scratch_shapes) + [sems, sems],
        input_output_aliases={**(aliases or {}), **{n_in + i: n_out + o for i, o in side.aliases.items()}},
        compiler_params=_params(("arbitrary",) * len(grid)),
    )(*args, *side.inputs)
    side.outs = list(res[n_out:])
    return res[0] if single else list(res[:n_out])


def _best(g, cap):
    if g <= cap:
        return g
    cands = [d for d in range(128, cap + 1, 128) if g % d == 0]
    assert cands, (g, cap)
    return cands[-1]


def _row_blk(n, cap):
    cands = [d for d in range(16, min(n, cap) + 1, 16) if n % d == 0]
    assert cands, (n, cap)
    return cands[-1]


def _tiles(a, b, mode, out_kind, out_p, bm, bn, bk):
    def cpiece(v):
        return v.arr.shape[-1] if v.kind == "cs" else v.shape[1]

    def rpiece(v):
        return v.arr.shape[-2] if v.kind == "rs" else v.shape[0]

    if mode == "nn":
        m, n = a.shape[0], b.shape[1]
        gm, gn, gk = rpiece(a), cpiece(b), math.gcd(cpiece(a), rpiece(b))
    elif mode == "nt":
        m, n = a.shape[0], b.shape[0]
        gm, gn, gk = rpiece(a), rpiece(b), math.gcd(cpiece(a), cpiece(b))
    else:
        m, n = a.shape[1], b.shape[1]
        gm, gn, gk = cpiece(a), cpiece(b), math.gcd(rpiece(a), rpiece(b))
    if out_kind == "cs":
        gn = math.gcd(gn, n // out_p)
    else:
        gm = math.gcd(gm, m // out_p)
    caps = {"nn": (1024, 1536, 2048), "nt": (512, 2048, 2048), "tn": (1536, 1536, 1024)}[mode]
    return (bm or _best(gm, caps[0])), (bn or _best(gn, caps[1])), (bk or _best(gk, caps[2]))


def matmul(a, b, mode, *, out_dtype, bm=None, bn=None, bk=None, out_kind="cs", out_p=1, out_pfn=None, res=None, epi=None,
           side=None, name):
    bm, bn, bk = _tiles(a, b, mode, out_kind, out_p, bm, bn, bk)
    if mode == "nn":
        (m, k), (k2, n) = a.shape, b.shape
        a_spec = a.spec(bm, bk, lambda i, j, kk: i, lambda i, j, kk: kk)
        b_spec = b.spec(bk, bn, lambda i, j, kk: kk, lambda i, j, kk: j)
        dims = (((1,), (0,)), ((), ()))
    elif mode == "nt":
        (m, k), (n, k2) = a.shape, b.shape
        a_spec = a.spec(bm, bk, lambda i, j, kk: i, lambda i, j, kk: kk)
        b_spec = b.spec(bn, bk, lambda i, j, kk: j, lambda i, j, kk: kk)
        dims = (((1,), (1,)), ((), ()))
    else:
        (k, m), (k2, n) = a.shape, b.shape
        a_spec = a.spec(bk, bm, lambda i, j, kk: kk, lambda i, j, kk: i)
        b_spec = b.spec(bk, bn, lambda i, j, kk: kk, lambda i, j, kk: j)
        dims = (((0,), (0,)), ((), ()))
    assert k == k2, (a.shape, b.shape, mode)
    assert m % bm == 0 and n % bn == 0 and k % bk == 0, (m, n, k, bm, bn, bk)
    nk = k // bk
    out_sds = _out_struct((m, n), out_kind, out_p, out_dtype)
    out_vm = VM(out_sds, out_kind, pfn=out_pfn)
    o_spec = out_vm.spec(bm, bn, lambda i, j, kk: i, lambda i, j, kk: j)
    in_specs, args = [a_spec, b_spec], [a.arr, b.arr]
    tiles = ([res] if res is not None else []) + (list(epi[1]) if epi else [])
    for v in tiles:
        assert v.shape == (m, n)
        in_specs.append(v.spec(bm, bn, lambda i, j, kk: i, lambda i, j, kk: j))
        args.append(v.arr)
    n_out = epi[2] if epi else 1

    def body(a_ref, b_ref, *rest):
        t_refs, o_refs = rest[:len(tiles)], rest[len(tiles):len(tiles) + n_out]
        part = lax.dot_general(a_ref[...], b_ref[...], dims, preferred_element_type=F32)

        def write(tot):
            if res is not None:
                tot = tot + t_refs[0][...].astype(F32)
            outs = epi[0](tot, *[r[...].astype(F32) for r in t_refs[len(tiles) - len(epi[1]):]]) if epi else (tot,)
            for o_ref, val in zip(o_refs, outs):
                o_ref[...] = val.astype(o_ref.dtype)

        if nk == 1:
            write(part)
            return
        acc = rest[-1]
        kk = pl.program_id(2)

        @pl.when(kk == 0)
        def _():
            acc[...] = part

        @pl.when(kk > 0)
        def _():
            acc[...] += part

        @pl.when(kk == nk - 1)
        def _():
            write(acc[...])

    return _call(body, side, name=name, grid=(m // bm, n // bn, nk), in_specs=in_specs,
                 out_specs=o_spec if n_out == 1 else [o_spec] * n_out, out_shape=out_sds if n_out == 1 else [out_sds] * n_out,
                 scratch_shapes=[pltpu.VMEM((bm, bn), F32)] if nk > 1 else [],
                 sem=("parallel", "parallel", "arbitrary"), args=args)


def rmsnorm_fwd(x, g, *, name):
    t, d = x.shape
    bt = _blk(512, t)

    def body(x_ref, g_ref, o_ref):
        xv = x_ref[...]
        r = lax.rsqrt(jnp.mean(xv * xv, axis=-1, keepdims=True) + EPS)
        o_ref[...] = (xv * r * g_ref[...]).astype(o_ref.dtype)

    return pl.pallas_call(
        body, name=name, grid=(t // bt,),
        in_specs=[pl.BlockSpec((bt, d), lambda i: (i, 0)), pl.BlockSpec((1, d), lambda i: (0, 0))],
        out_specs=pl.BlockSpec((bt, d), lambda i: (i, 0)), out_shape=jax.ShapeDtypeStruct((t, d), BF16),
        compiler_params=_params(("parallel",)),
    )(x, g)


def rmsnorm_bwd(x, g, dh, dres, *, name):
    t, d = x.shape
    bt = _blk(256, t)
    want_dx = dres is not None

    def body(x_ref, g_ref, dh_ref, *rest):
        if want_dx:
            dres_ref, dx_ref, dxb_ref, dg_ref = rest
        else:
            (dg_ref,) = rest
        xv = x_ref[...]
        dhv = dh_ref[...].astype(F32)
        r = lax.rsqrt(jnp.mean(xv * xv, axis=-1, keepdims=True) + EPS)
        xh = xv * r
        part = jnp.sum(dhv * xh, axis=0, keepdims=True)

        @pl.when(pl.program_id(0) == 0)
        def _():
            dg_ref[...] = part

        @pl.when(pl.program_id(0) > 0)
        def _():
            dg_ref[...] += part

        if want_dx:
            dy = dhv * g_ref[...]
            dxn = r * (dy - xh * jnp.mean(dy * xh, axis=-1, keepdims=True))
            dx = dres_ref[...] + dxn
            dx_ref[...] = dx
            dxb_ref[...] = dx.astype(BF16)

    row = pl.BlockSpec((bt, d), lambda i: (i, 0))
    vec = pl.BlockSpec((1, d), lambda i: (0, 0))
    in_specs, args = [row, vec, row], [x, g, dh]
    out_specs, out_shape = [vec], [jax.ShapeDtypeStruct((1, d), F32)]
    if want_dx:
        in_specs.append(row)
        args.append(dres)
        out_specs = [row, row] + out_specs
        out_shape = [jax.ShapeDtypeStruct((t, d), F32), jax.ShapeDtypeStruct((t, d), BF16)] + out_shape
    return pl.pallas_call(
        body, name=name, grid=(t // bt,), in_specs=in_specs, out_specs=out_specs, out_shape=out_shape,
        compiler_params=_params(("arbitrary",)),
    )(*args)


def loss_head(x, g, tgt, *, name):
    t, d = x.shape
    bt = _blk(256, t)

    def body(x_ref, g_ref, t_ref, loss_ref, dx_ref, dxb_ref, dg_ref):
        xv = x_ref[...]
        gv = g_ref[...]
        r = lax.rsqrt(jnp.mean(xv * xv, axis=-1, keepdims=True) + EPS)
        xh = xv * r
        e = xh * gv - t_ref[...]
        lpart = jnp.zeros((1, 128), F32) + jnp.sum(e * e) * (0.5 / d)
        dyv = e * (1.0 / d)
        gpart = jnp.sum(dyv * xh, axis=0, keepdims=True)

        @pl.when(pl.program_id(0) == 0)
        def _():
            loss_ref[...] = lpart
            dg_ref[...] = gpart

        @pl.when(pl.program_id(0) > 0)
        def _():
            loss_ref[...] += lpart
            dg_ref[...] += gpart

        dy = dyv * gv
        dx = r * (dy - xh * jnp.mean(dy * xh, axis=-1, keepdims=True))
        dx_ref[...] = dx
        dxb_ref[...] = dx.astype(BF16)

    row = pl.BlockSpec((bt, d), lambda i: (i, 0))
    vec = pl.BlockSpec((1, d), lambda i: (0, 0))
    return pl.pallas_call(
        body, name=name, grid=(t // bt,), in_specs=[row, vec, row],
        out_specs=[pl.BlockSpec((1, 128), lambda i: (0, 0)), row, row, vec],
        out_shape=[jax.ShapeDtypeStruct((1, 128), F32), jax.ShapeDtypeStruct((t, d), F32),
                   jax.ShapeDtypeStruct((t, d), BF16), jax.ShapeDtypeStruct((1, d), F32)],
        compiler_params=_params(("arbitrary",)),
    )(x, g, tgt)


def _sigmoid(x):
    return 1.0 / (1.0 + jnp.exp(-x))


def _swiglu_epi(b, a):
    return b, a * _sigmoid(a) * b


def _swiglu_bwd_epi(ds, a, b):
    sg = _sigmoid(a)
    return ds * b * sg * (1.0 + a * (1.0 - sg)), ds * a * sg


def _xa_probs(qh, kh, scale):
    s = lax.dot_general(qh, kh, (((1,), (1,)), ((), ())), preferred_element_type=F32) * scale
    s = s - jnp.max(s, axis=-1, keepdims=True)
    p = jnp.exp(s)
    return p / jnp.sum(p, axis=-1, keepdims=True)


def xattn_fwd(q, kv, *, name):
    t, d = q.shape
    m = kv.shape[0]
    hd = d // XA_HEADS
    bt = _blk(512, t)
    scale = hd ** -0.5

    def body(q_ref, kv_ref, o_ref):
        for h in range(XA_HEADS):
            qh = q_ref[:, h * hd:(h + 1) * hd]
            kh = kv_ref[:, h * hd:(h + 1) * hd]
            vh = kv_ref[:, d + h * hd:d + (h + 1) * hd]
            p = _xa_probs(qh, kh, scale)
            o_ref[:, h * hd:(h + 1) * hd] = jnp.dot(p.astype(BF16), vh, preferred_element_type=F32).astype(BF16)

    return pl.pallas_call(
        body, name=name, grid=(t // bt,),
        in_specs=[pl.BlockSpec((bt, d), lambda i: (i, 0)), pl.BlockSpec((m, 2 * d), lambda i: (0, 0))],
        out_specs=pl.BlockSpec((bt, d), lambda i: (i, 0)), out_shape=jax.ShapeDtypeStruct((t, d), BF16),
        compiler_params=_params(("parallel",)),
    )(q, kv)


def xattn_bwd(q, kv, do, *, name):
    t, d = q.shape
    m = kv.shape[0]
    hd = d // XA_HEADS
    bt = _blk(512, t)
    scale = hd ** -0.5

    def body(q_ref, kv_ref, do_ref, dq_ref, dkv_ref):
        first = pl.program_id(0) == 0
        for h in range(XA_HEADS):
            qs, ks, vs = slice(h * hd, (h + 1) * hd), slice(h * hd, (h + 1) * hd), slice(d + h * hd, d + (h + 1) * hd)
            qh, kh, vh, doh = q_ref[:, qs], kv_ref[:, ks], kv_ref[:, vs], do_ref[:, qs]
            p = _xa_probs(qh, kh, scale)
            dp = lax.dot_general(doh, vh, (((1,), (1,)), ((), ())), preferred_element_type=F32)
            dsv = p * (dp - jnp.sum(p * dp, axis=-1, keepdims=True)) * scale
            dsb = dsv.astype(BF16)
            dq_ref[:, qs] = jnp.dot(dsb, kh, preferred_element_type=F32).astype(BF16)
            dk = lax.dot_general(dsb, qh, (((0,), (0,)), ((), ())), preferred_element_type=F32)
            dv = lax.dot_general(p.astype(BF16), doh, (((0,), (0,)), ((), ())), preferred_element_type=F32)

            @pl.when(first)
            def _():
                dkv_ref[:, ks] = dk
                dkv_ref[:, vs] = dv

            @pl.when(jnp.logical_not(first))
            def _():
                dkv_ref[:, ks] += dk
                dkv_ref[:, vs] += dv

    row = pl.BlockSpec((bt, d), lambda i: (i, 0))
    full = pl.BlockSpec((m, 2 * d), lambda i: (0, 0))
    return pl.pallas_call(
        body, name=name, grid=(t // bt,), in_specs=[row, full, row], out_specs=[row, full],
        out_shape=[jax.ShapeDtypeStruct((t, d), BF16), jax.ShapeDtypeStruct((m, 2 * d), F32)],
        compiler_params=_params(("arbitrary",)),
    )(q, kv, do)


def _pool_p(buf, uv, rows, w, bt):
    acc = uv
    for dd in range(1, w):
        acc = acc + buf[pl.ds(POOL_HALO - dd, bt), :]
    cnt = jnp.minimum(rows + 1, w).astype(F32)
    return acc / cnt - uv


def pool_fwd(z, w_pool, scale, *, name):
    t = z.shape[0]
    ng, gsz = w_pool.shape[0], w_pool.shape[1]
    mix = ng * gsz
    bt = _blk(512, t)

    def body(u_ref, uh_ref, w_ref, sc_ref, o_ref, buf):
        r = pl.program_id(0)
        rows = r * bt + lax.broadcasted_iota(jnp.int32, (bt, 1), 0)
        for g in range(ng):
            gs = slice(g * gsz, (g + 1) * gsz)
            uv = u_ref[:, gs]
            buf[0:POOL_HALO, :] = jnp.where(r > 0, uh_ref[:, gs], 0.0)
            buf[POOL_HALO:POOL_HALO + bt, :] = uv
            p = _pool_p(buf, uv, rows, POOL_WINDOWS[g], bt)
            y = jnp.dot(p.astype(BF16), w_ref[g], preferred_element_type=F32) * sc_ref[:, gs]
            o_ref[:, gs] = y.astype(BF16)

    hb = bt // POOL_HALO
    return pl.pallas_call(
        body, name=name, grid=(t // bt,),
        in_specs=[pl.BlockSpec((bt, mix), lambda i: (i, 0)),
                  pl.BlockSpec((POOL_HALO, mix), lambda i: (jnp.maximum(i * hb - 1, 0), 0)),
                  pl.BlockSpec((ng, gsz, gsz), lambda i: (0, 0, 0)), pl.BlockSpec((1, mix), lambda i: (0, 0))],
        out_specs=pl.BlockSpec((None, bt, mix), lambda i: (0, i, 0)),
        out_shape=jax.ShapeDtypeStruct((2, t, mix), BF16),
        scratch_shapes=[pltpu.VMEM((POOL_HALO + bt, gsz), F32)],
        compiler_params=_params(("parallel",)),
    )(z, z, w_pool, scale)


def pool_bwd(z, dcat, w_pool, scale, *, name):
    t = z.shape[0]
    ng, gsz = w_pool.shape[0], w_pool.shape[1]
    mix = ng * gsz
    bt = _blk(512, t)
    nb = t // bt
    nt_dims = (((1,), (1,)), ((), ()))
    tn_dims = (((0,), (0,)), ((), ()))

    def body(u_ref, uh_ref, dy_ref, dyh_ref, w_ref, sc_ref, du_ref, dw_ref, dsc_ref, buf, buf2):
        r = pl.program_id(0)
        first = r == 0
        rows = r * bt + lax.broadcasted_iota(jnp.int32, (bt, 1), 0)
        rows_h = (r + 1) * bt + lax.broadcasted_iota(jnp.int32, (POOL_HALO, 1), 0)
        for g in range(ng):
            w = POOL_WINDOWS[g]
            gs = slice(g * gsz, (g + 1) * gsz)
            uv = u_ref[:, gs]
            buf[0:POOL_HALO, :] = jnp.where(r > 0, uh_ref[:, gs], 0.0)
            buf[POOL_HALO:POOL_HALO + bt, :] = uv
            pb = _pool_p(buf, uv, rows, w, bt).astype(BF16)
            wg = w_ref[g]
            sc = sc_ref[:, gs]
            y0 = jnp.dot(pb, wg, preferred_element_type=F32)
            dyv = dy_ref[:, gs].astype(F32)
            dsc = jnp.sum(dyv * y0, axis=0, keepdims=True)
            dyw = (dyv * sc).astype(BF16)
            dw = lax.dot_general(pb, dyw, tn_dims, preferred_element_type=F32)

            @pl.when(first)
            def _():
                dw_ref[g] = dw
                dsc_ref[:, gs] = dsc

            @pl.when(jnp.logical_not(first))
            def _():
                dw_ref[g] += dw
                dsc_ref[:, gs] += dsc

            dp = lax.dot_general(dyw, wg, nt_dims, preferred_element_type=F32)
            dyh = (dyh_ref[:, gs].astype(F32) * sc).astype(BF16)
            dph = lax.dot_general(dyh, wg, nt_dims, preferred_element_type=F32)
            dph = jnp.where(r < nb - 1, dph, 0.0)
            buf2[0:bt, :] = dp / jnp.minimum(rows + 1, w).astype(F32)
            buf2[bt:bt + POOL_HALO, :] = dph / jnp.minimum(rows_h + 1, w).astype(F32)
            acc = buf2[pl.ds(0, bt), :]
            for dd in range(1, w):
                acc = acc + buf2[pl.ds(dd, bt), :]
            du_ref[:, gs] = (acc - dp).astype(BF16)

    hb = bt // POOL_HALO
    nhb = t // POOL_HALO
    return pl.pallas_call(
        body, name=name, grid=(nb,),
        in_specs=[pl.BlockSpec((bt, mix), lambda i: (i, 0)),
                  pl.BlockSpec((POOL_HALO, mix), lambda i: (jnp.maximum(i * hb - 1, 0), 0)),
                  pl.BlockSpec((None, bt, mix), lambda i: (0, i, 0)),
                  pl.BlockSpec((None, POOL_HALO, mix), lambda i: (0, jnp.minimum((i + 1) * hb, nhb - 1), 0)),
                  pl.BlockSpec((ng, gsz, gsz), lambda i: (0, 0, 0)), pl.BlockSpec((1, mix), lambda i: (0, 0))],
        out_specs=[pl.BlockSpec((None, bt, mix), lambda i: (4, i, 0)),
                   pl.BlockSpec((ng, gsz, gsz), lambda i: (0, 0, 0)), pl.BlockSpec((1, mix), lambda i: (0, 0))],
        out_shape=[jax.ShapeDtypeStruct((5, t, mix), BF16), jax.ShapeDtypeStruct((ng, gsz, gsz), F32),
                   jax.ShapeDtypeStruct((1, mix), F32)],
        scratch_shapes=[pltpu.VMEM((POOL_HALO + bt, gsz), F32), pltpu.VMEM((bt + POOL_HALO, gsz), F32)],
        compiler_params=_params(("arbitrary",)),
    )(z, z, dcat, dcat, w_pool, scale)


HG_HEADS_PER_STEP = 2
HG_LEVELS = ((64, 31), (32, 15), (16, 7))
HG_DIAG = (8, 3)
_NT = (((1,), (1,)), ((), ()))
_TN = (((0,), (0,)), ((), ()))
_HI = lax.Precision.HIGHEST


def _hg_masks():
    c = HG_CHUNK
    t = lax.broadcasted_iota(jnp.int32, (c, c), 0)
    s = lax.broadcasted_iota(jnp.int32, (c, c), 1)
    masks = []
    for blk, row in HG_LEVELS:
        sh = blk.bit_length() - 1
        same = (t >> sh) == (s >> sh)
        masks.append(same & ((t & (blk - 1)) > row) & ((s & (blk - 1)) <= row))
    sh = HG_DIAG[0].bit_length() - 1
    masks.append(((t >> sh) == (s >> sh)) & (s <= t))
    return t, s, masks


def _row_of_block(x, blk, row):
    c, n = x.shape
    x3 = x.reshape(c // blk, blk, n)
    return jnp.broadcast_to(x3[:, row:row + 1, :], x3.shape).reshape(c, n)


def _hg_parts(qv, flv, lb, masks, tri):
    sgf = _sigmoid(flv)
    f = lb + (1.0 - lb) * sgf
    logf = jnp.log(f)
    kk = 1.0 - f
    sgq = _sigmoid(qv)
    qf = qv * sgq * (HG_HEAD ** -0.5)
    bc = jnp.dot(tri, logf, preferred_element_type=F32, precision=_HI)
    levels = []
    a = None
    for li, (blk, row) in enumerate(HG_LEVELS + (HG_DIAG,)):
        e = bc - _row_of_block(bc, blk, row)
        if li < len(HG_LEVELS):
            eq, ek = jnp.exp(jnp.minimum(e, 0.0)), jnp.exp(jnp.minimum(-e, 0.0))
        else:
            eq, ek = jnp.exp(jnp.clip(e, -EXP_CLAMP, EXP_CLAMP)), jnp.exp(jnp.clip(-e, -EXP_CLAMP, EXP_CLAMP))
        qt, kt = qf * eq, kk * ek
        part = jnp.where(masks[li], lax.dot_general(qt.astype(BF16), kt.astype(BF16), _NT, preferred_element_type=F32), 0.0)
        a = part if a is None else a + part
        levels.append((eq, ek, qt, kt))
    return dict(sgf=sgf, f=f, kk=kk, sgq=sgq, qf=qf, bc=bc, levels=levels, a=a)


def hgrn_fwd(z, cat, lb, gain, mix_a, *, side=None, name):
    t = z.shape[0]
    mix_b = lb.shape[1]
    nh = mix_b // HG_HEAD
    bt = _blk(256, t)
    ncb = bt // HG_CHUNK
    dh = HG_HEAD

    def body(q_ref, fl_ref, i_ref, g_ref, lb_ref, gain_ref, cat_in, o_ref, st_ref, st):
        del cat_in

        @pl.when(pl.program_id(1) == 0)
        def _():
            st[...] = jnp.zeros_like(st)

        t_i, s_i, masks = _hg_masks()
        tri = (s_i <= t_i).astype(F32)
        lbv, gn = lb_ref[...], gain_ref[...]
        for c in range(ncb):
            rs = slice(c * HG_CHUNK, (c + 1) * HG_CHUNK)
            pr = _hg_parts(q_ref[rs, :], fl_ref[rs, :], lbv, masks, tri)
            vb = i_ref[rs, :].astype(BF16)
            stv = st[...]
            st_ref[c] = stv
            bc = pr["bc"]
            qt = pr["qf"] * jnp.exp(bc)
            o = (jnp.dot(pr["a"].astype(BF16), vb, preferred_element_type=F32)
                 + lax.dot_general(qt.astype(BF16), stv.astype(BF16), _NT, preferred_element_type=F32))
            bl = bc[HG_CHUNK - 1:HG_CHUNK, :]
            khat = pr["kk"] * jnp.exp(bl - bc)
            st[...] = stv * jnp.exp(bl) + lax.dot_general(vb, khat.astype(BF16), _TN, preferred_element_type=F32)
            r = lax.rsqrt(jnp.mean(o * o, axis=-1, keepdims=True) + EPS)
            gv = g_ref[rs, :]
            o_ref[rs, :] = (o * r * gn * (gv * _sigmoid(gv))).astype(BF16)

    def col(which):
        base = (mix_a + which * mix_b) // dh
        return pl.BlockSpec((bt, dh), lambda h, i: (i, base + h))

    return _call(
        body, side, name=name, grid=(nh, t // bt),
        in_specs=[col(0), col(1), col(2), col(3), pl.BlockSpec((1, dh), lambda h, i: (0, h)),
                  pl.BlockSpec((1, dh), lambda h, i: (0, 0)), _ANY],
        out_specs=[pl.BlockSpec((None, bt, dh), lambda h, i: (1, i, h)),
                   pl.BlockSpec((None, ncb, dh, dh), lambda h, i: (h, i, 0, 0))],
        out_shape=[jax.ShapeDtypeStruct(cat.shape, BF16), jax.ShapeDtypeStruct((nh, t // HG_CHUNK, dh, dh), F32)],
        scratch_shapes=[pltpu.VMEM((dh, dh), F32)], aliases={6: 0}, sem=("parallel", "arbitrary"),
        args=(z, z, z, z, lb, gain, cat))


def hgrn_bwd(z, dcat, dz5, states, lb, gain, mix_a, *, side=None, name):
    t = z.shape[0]
    mix_b = lb.shape[1]
    nh = mix_b // HG_HEAD
    bt = _blk(256, t)
    nb = t // bt
    ncb = bt // HG_CHUNK
    dh = HG_HEAD
    hp = HG_HEADS_PER_STEP if nh % HG_HEADS_PER_STEP == 0 else 1

    def body(q_ref, fl_ref, i_ref, g_ref, dy_ref, st_ref, lb_ref, gain_ref, dz_in, dz_ref, dlb_ref, dgn_ref, dst):
        del dz_in
        first = pl.program_id(1) == 0

        @pl.when(first)
        def _():
            dst[...] = jnp.zeros_like(dst)

        t_i, s_i, masks = _hg_masks()
        tri = (s_i <= t_i).astype(F32)
        triu = (s_i >= t_i).astype(F32)
        last_row = lax.broadcasted_iota(jnp.int32, (HG_CHUNK, 1), 0) == HG_CHUNK - 1
        gn = gain_ref[...]
        dlb_acc = [jnp.zeros((1, dh), F32) for _ in range(hp)]
        dgn_acc = [jnp.zeros((1, dh), F32) for _ in range(hp)]
        for c, hh in [(c, hh) for c in reversed(range(ncb)) for hh in range(hp)]:
            rs, cs = slice(c * HG_CHUNK, (c + 1) * HG_CHUNK), slice(hh * dh, (hh + 1) * dh)
            lbv = lb_ref[:, cs]
            qv, flv, gv = q_ref[rs, cs], fl_ref[rs, cs], g_ref[rs, cs]
            pr = _hg_parts(qv, flv, lbv, masks, tri)
            vb = i_ref[rs, cs].astype(BF16)
            stv = st_ref[hh, c]
            stb = stv.astype(BF16)
            dsv = dst[hh]
            dsb = dsv.astype(BF16)
            bc, kk, qf, ab = pr["bc"], pr["kk"], pr["qf"], pr["a"].astype(BF16)
            ebc = jnp.exp(bc)
            qt = qf * ebc
            qtb = qt.astype(BF16)
            o = jnp.dot(ab, vb, preferred_element_type=F32) + lax.dot_general(qtb, stb, _NT, preferred_element_type=F32)
            r = lax.rsqrt(jnp.mean(o * o, axis=-1, keepdims=True) + EPS)
            oh = o * r
            sgg = _sigmoid(gv)
            dyv = dy_ref[rs, cs].astype(F32)
            don = dyv * (gv * sgg)
            dgate = dyv * (oh * gn) * (sgg * (1.0 + gv * (1.0 - sgg)))
            dgn_acc[hh] = dgn_acc[hh] + jnp.sum(don * oh, axis=0, keepdims=True)
            doh = don * gn
            do = r * (doh - oh * jnp.mean(doh * oh, axis=-1, keepdims=True))
            dob = do.astype(BF16)
            bl = bc[HG_CHUNK - 1:HG_CHUNK, :]
            ebl = jnp.exp(bl)
            ekh = jnp.exp(bl - bc)
            khat = kk * ekh
            dv = (lax.dot_general(ab, dob, _TN, preferred_element_type=F32)
                  + lax.dot_general(khat.astype(BF16), dsb, _NT, preferred_element_type=F32))
            da = lax.dot_general(dob, vb, _NT, preferred_element_type=F32)
            dqt = jnp.dot(dob, stb, preferred_element_type=F32)
            dkh = jnp.dot(vb, dsb, preferred_element_type=F32)
            dst[hh] = dsv * ebl + lax.dot_general(dob, qtb, _TN, preferred_element_type=F32)
            dbl = jnp.sum(dsv * stv, axis=0, keepdims=True) * ebl + jnp.sum(dkh * khat, axis=0, keepdims=True)
            dqf = dqt * ebc
            dkk = dkh * ekh
            dbc = dqt * qt - dkh * khat
            for li, (eq, ek, qtl, ktl) in enumerate(pr["levels"]):
                gm = jnp.where(masks[li], da, 0.0).astype(BF16)
                qtr, ktr = qtl.astype(BF16), ktl.astype(BF16)
                dql = jnp.dot(gm, ktr, preferred_element_type=F32)
                dkl = lax.dot_general(gm, qtr, _TN, preferred_element_type=F32)
                dqf = dqf + dql * eq
                dkk = dkk + dkl * ek
                dbc = dbc + qtr.astype(F32) * dql - ktr.astype(F32) * dkl
            dbc = dbc + jnp.where(last_row, dbl, 0.0)
            dlogf = jnp.dot(triu, dbc, preferred_element_type=F32, precision=_HI)
            df = dlogf / pr["f"] - dkk
            sgf = pr["sgf"]
            dfl = df * (1.0 - lbv) * sgf * (1.0 - sgf)
            dlb_acc[hh] = dlb_acc[hh] + jnp.sum(df * (1.0 - sgf), axis=0, keepdims=True)
            sgq = pr["sgq"]
            dq = dqf * (HG_HEAD ** -0.5) * (sgq * (1.0 + qv * (1.0 - sgq)))
            dz_ref[0, rs, cs] = dq.astype(BF16)
            dz_ref[1, rs, cs] = dfl.astype(BF16)
            dz_ref[2, rs, cs] = dv.astype(BF16)
            dz_ref[3, rs, cs] = dgate.astype(BF16)

        @pl.when(first)
        def _():
            for hh in range(hp):
                dlb_ref[:, hh * dh:(hh + 1) * dh] = dlb_acc[hh]
                dgn_ref[hh] = dgn_acc[hh]

        @pl.when(jnp.logical_not(first))
        def _():
            for hh in range(hp):
                dlb_ref[:, hh * dh:(hh + 1) * dh] += dlb_acc[hh]
                dgn_ref[hh] += dgn_acc[hh]

    wd = hp * dh

    def col(which):
        base = (mix_a + which * mix_b) // wd
        return pl.BlockSpec((bt, wd), lambda h, i: (nb - 1 - i, base + h))

    return _call(
        body, side, name=name, grid=(nh // hp, nb),
        in_specs=[col(0), col(1), col(2), col(3),
                  pl.BlockSpec((None, bt, wd), lambda h, i: (1, nb - 1 - i, h)),
                  pl.BlockSpec((hp, ncb, dh, dh), lambda h, i: (h, nb - 1 - i, 0, 0)),
                  pl.BlockSpec((1, wd), lambda h, i: (0, h)), pl.BlockSpec((1, dh), lambda h, i: (0, 0)), _ANY],
        out_specs=[pl.BlockSpec((4, bt, wd), lambda h, i: (0, nb - 1 - i, h)),
                   pl.BlockSpec((1, wd), lambda h, i: (0, h)),
                   pl.BlockSpec((hp, 1, dh), lambda h, i: (h, 0, 0))],
        out_shape=[jax.ShapeDtypeStruct(dz5.shape, BF16), jax.ShapeDtypeStruct((1, mix_b), F32),
                   jax.ShapeDtypeStruct((nh, 1, dh), F32)],
        scratch_shapes=[pltpu.VMEM((hp, dh, dh), F32)], aliases={8: 0}, sem=("parallel", "arbitrary"),
        args=(z, z, z, z, dcat, states, lb, gain, dz5))


def _fox_scores(qb, kb, fk, scale, masked):
    s = lax.dot_general(qb, kb, _NT, preferred_element_type=F32) * scale - fk
    if masked:
        n = s.shape[0]
        row = lax.broadcasted_iota(jnp.int32, (n, n), 0)
        col = lax.broadcasted_iota(jnp.int32, (n, n), 1)
        s = jnp.where(col <= row, s, -jnp.inf)
    return s


def fox_fwd(qkv, fk, *, side=None, name):
    _, t, d = qkv.shape
    nh = d // FOX_HEAD
    b = _blk(FOX_BLK, t)
    nb = t // b
    dh = FOX_HEAD
    scale = dh ** -0.5

    def body(q_ref, k_ref, v_ref, f_ref, o_ref, lse_ref):
        qi = pl.program_id(1)
        qb = q_ref[...]

        def step(kj, carry, masked):
            m, l, acc = carry
            off = pl.multiple_of(kj * b, b)
            s = _fox_scores(qb, k_ref[pl.ds(off, b), :], f_ref[kj], scale, masked)
            m_new = jnp.maximum(m, jnp.max(s, axis=-1, keepdims=True))
            alpha = jnp.exp(m - m_new)
            p = jnp.exp(s - m_new)
            l = alpha * l + jnp.sum(p, axis=-1, keepdims=True)
            acc = alpha * acc + jnp.dot(p.astype(BF16), v_ref[pl.ds(off, b), :], preferred_element_type=F32)
            return m_new, l, acc

        init = (jnp.full((b, 1), -jnp.inf, F32), jnp.zeros((b, 1), F32), jnp.zeros((b, dh), F32))
        carry = lax.fori_loop(0, qi, lambda kj, c: step(kj, c, False), init)
        m, l, acc = step(qi, carry, True)
        o_ref[...] = (acc / l).astype(BF16)
        lse_ref[...] = m + jnp.log(l)

    return _call(
        body, side, name=name, grid=(nh, nb),
        in_specs=[pl.BlockSpec((None, b, dh), lambda h, i: (0, i, h)),
                  pl.BlockSpec((None, t, dh), lambda h, i: (1, 0, h)),
                  pl.BlockSpec((None, t, dh), lambda h, i: (2, 0, h)),
                  pl.BlockSpec((None, nb, 1, b), lambda h, i: (h, 0, 0, 0))],
        out_specs=[pl.BlockSpec((b, dh), lambda h, i: (i, h)), pl.BlockSpec((None, b, 1), lambda h, i: (h, i, 0))],
        out_shape=[jax.ShapeDtypeStruct((t, d), BF16), jax.ShapeDtypeStruct((nh, t, 1), F32)],
        sem=("parallel", "parallel"), args=(qkv, qkv, qkv, fk))


def fox_bwd_dq(qkv, fk, do, lse, *, side=None, name):
    _, t, d = qkv.shape
    nh = d // FOX_HEAD
    b = _blk(FOX_BLK, t)
    nb = t // b
    dh = FOX_HEAD
    scale = dh ** -0.5

    def body(q_ref, k_ref, v_ref, f_ref, do_ref, lse_ref, dq_ref, dl_ref, p_buf, dp_buf):
        qi = pl.program_id(1)
        qb, dob, lse_v = q_ref[...], do_ref[...], lse_ref[...]

        def first(kj, dl, masked):
            off = pl.multiple_of(kj * b, b)
            p = jnp.exp(_fox_scores(qb, k_ref[pl.ds(off, b), :], f_ref[kj], scale, masked) - lse_v)
            dp = lax.dot_general(dob, v_ref[pl.ds(off, b), :], _NT, preferred_element_type=F32)
            p_buf[kj] = p
            dp_buf[kj] = dp
            return dl + jnp.sum(p * dp, axis=-1, keepdims=True)

        dl = lax.fori_loop(0, qi, lambda kj, c: first(kj, c, False), jnp.zeros((b, 1), F32))
        dl = first(qi, dl, True)
        dl_ref[...] = dl

        def second(kj, dq):
            off = pl.multiple_of(kj * b, b)
            dsv = p_buf[kj] * (dp_buf[kj] - dl)
            return dq + jnp.dot(dsv.astype(BF16), k_ref[pl.ds(off, b), :], preferred_element_type=F32)

        dq = lax.fori_loop(0, qi + 1, second, jnp.zeros((b, dh), F32))
        dq_ref[...] = (dq * scale).astype(BF16)

    col = pl.BlockSpec((None, b, 1), lambda h, i: (h, i, 0))
    return _call(
        body, side, name=name, grid=(nh, nb),
        in_specs=[pl.BlockSpec((None, b, dh), lambda h, i: (0, i, h)),
                  pl.BlockSpec((None, t, dh), lambda h, i: (1, 0, h)),
                  pl.BlockSpec((None, t, dh), lambda h, i: (2, 0, h)),
                  pl.BlockSpec((None, nb, 1, b), lambda h, i: (h, 0, 0, 0)),
                  pl.BlockSpec((b, dh), lambda h, i: (i, h)), col],
        out_specs=[pl.BlockSpec((None, b, dh), lambda h, i: (2, i, h)), col],
        out_shape=[jax.ShapeDtypeStruct((3, t, d), BF16), jax.ShapeDtypeStruct((nh, t, 1), F32)],
        scratch_shapes=[pltpu.VMEM((nb, b, b), F32), pltpu.VMEM((nb, b, b), F32)],
        sem=("parallel", "parallel"), args=(qkv, qkv, qkv, fk, do, lse))


def fox_bwd_dkv(qkv, fk, do, lse, delta, dqkv, *, side=None, name):
    _, t, d = qkv.shape
    nh = d // FOX_HEAD
    b = _blk(FOX_BLK, t)
    nb = t // b
    dh = FOX_HEAD
    scale = dh ** -0.5

    def body(q_ref, k_ref, v_ref, f_ref, do_ref, lse_ref, dl_ref, dz_in, dkv_ref, df_ref):
        del dz_in
        kj = pl.program_id(1)
        kb, vb, fkv = k_ref[...], v_ref[...], f_ref[...]

        def step(qi, carry, masked):
            dk, dv, df = carry
            off = pl.multiple_of(qi * b, b)
            qb, dob = q_ref[pl.ds(off, b), :], do_ref[pl.ds(off, b), :]
            p = jnp.exp(_fox_scores(qb, kb, fkv, scale, masked) - lse_ref[pl.ds(off, b), :])
            dv = dv + lax.dot_general(p.astype(BF16), dob, _TN, preferred_element_type=F32)
            dp = lax.dot_general(dob, vb, _NT, preferred_element_type=F32)
            dsv = p * (dp - dl_ref[pl.ds(off, b), :])
            dk = dk + lax.dot_general(dsv.astype(BF16), qb, _TN, preferred_element_type=F32)
            return dk, dv, df - jnp.sum(dsv, axis=0, keepdims=True)

        init = (jnp.zeros((b, dh), F32), jnp.zeros((b, dh), F32), jnp.zeros((1, b), F32))
        carry = step(kj, init, True)
        dk, dv, df = lax.fori_loop(kj + 1, nb, lambda qi, c: step(qi, c, False), carry)
        dkv_ref[0] = (dk * scale).astype(BF16)
        dkv_ref[1] = dv.astype(BF16)
        df_ref[...] = df

    col = pl.BlockSpec((None, t, 1), lambda h, j: (h, 0, 0))
    return _call(
        body, side, name=name, grid=(nh, nb),
        in_specs=[pl.BlockSpec((None, t, dh), lambda h, j: (0, 0, h)),
                  pl.BlockSpec((None, b, dh), lambda h, j: (1, j, h)),
                  pl.BlockSpec((None, b, dh), lambda h, j: (2, j, h)),
                  pl.BlockSpec((None, None, 1, b), lambda h, j: (h, j, 0, 0)),
                  pl.BlockSpec((t, dh), lambda h, j: (0, h)), col, col, pl.BlockSpec(memory_space=pl.ANY)],
        out_specs=[pl.BlockSpec((2, b, dh), lambda h, j: (0, j, h)),
                   pl.BlockSpec((None, None, 1, b), lambda h, j: (h, j, 0, 0))],
        out_shape=[jax.ShapeDtypeStruct((3, t, d), BF16), jax.ShapeDtypeStruct((nh, nb, 1, b), F32)],
        aliases={7: 0}, sem=("parallel", "parallel"), args=(qkv, qkv, qkv, fk, do, lse, delta, dqkv))


FL_BLK = 256


def _log_sigmoid(x):
    return jnp.minimum(x, 0.0) - jnp.log(1.0 + jnp.exp(-jnp.abs(x)))


def fl_fwd(zf, bf, *, name):
    t, n = zf.shape
    bt = _blk(FL_BLK, t)

    def body(z_ref, b_ref, o_ref, carry):
        @pl.when(pl.program_id(0) == 0)
        def _():
            carry[...] = jnp.zeros_like(carry)

        ls = _log_sigmoid(z_ref[...] + b_ref[...])
        r = lax.broadcasted_iota(jnp.int32, (bt, bt), 0)
        c = lax.broadcasted_iota(jnp.int32, (bt, bt), 1)
        cs = jnp.dot((c <= r).astype(F32), ls, preferred_element_type=F32, precision=_HI) + carry[...]
        o_ref[...] = cs
        carry[...] = cs[bt - 1:bt, :]

    return pl.pallas_call(
        body, name=name, grid=(t // bt,),
        in_specs=[pl.BlockSpec((bt, n), lambda i: (i, 0)), pl.BlockSpec((1, n), lambda i: (0, 0))],
        out_specs=pl.BlockSpec((bt, n), lambda i: (i, 0)), out_shape=jax.ShapeDtypeStruct((t, n), F32),
        scratch_shapes=[pltpu.VMEM((1, n), F32)], compiler_params=_params(("arbitrary",)),
    )(zf, bf)


def fl_bwd(df, zf, bf, *, name):
    t, n = zf.shape
    bt = _blk(FL_BLK, t)
    nb = t // bt

    def body(df_ref, z_ref, b_ref, dz_ref, db_ref, carry):
        first = pl.program_id(0) == 0

        @pl.when(first)
        def _():
            carry[...] = jnp.zeros_like(carry)

        r = lax.broadcasted_iota(jnp.int32, (bt, bt), 0)
        c = lax.broadcasted_iota(jnp.int32, (bt, bt), 1)
        dls = jnp.dot((c >= r).astype(F32), df_ref[...], preferred_element_type=F32, precision=_HI) + carry[...]
        carry[...] = dls[0:1, :]
        dz = dls * (1.0 - _sigmoid(z_ref[...] + b_ref[...]))
        dz_ref[...] = dz.astype(BF16)
        part = jnp.sum(dz, axis=0, keepdims=True)

        @pl.when(first)
        def _():
            db_ref[...] = part

        @pl.when(jnp.logical_not(first))
        def _():
            db_ref[...] += part

    row = pl.BlockSpec((bt, n), lambda i: (nb - 1 - i, 0))
    vec = pl.BlockSpec((1, n), lambda i: (0, 0))
    return pl.pallas_call(
        body, name=name, grid=(nb,), in_specs=[row, row, vec], out_specs=[row, vec],
        out_shape=[jax.ShapeDtypeStruct((t, n), BF16), jax.ShapeDtypeStruct((1, n), F32)],
        scratch_shapes=[pltpu.VMEM((1, n), F32)], compiler_params=_params(("arbitrary",)),
    )(df, zf, bf)


def _adamw_math(w, g, m, v):
    m = ADAM_B1 * m + (1.0 - ADAM_B1) * g
    v = ADAM_B2 * v + (1.0 - ADAM_B2) * (g * g)
    m_hat = m / (1.0 - ADAM_B1 ** ADAM_STEP)
    v_hat = v / (1.0 - ADAM_B2 ** ADAM_STEP)
    delta = -ADAM_LR * (m_hat / (jnp.sqrt(v_hat) + ADAM_EPS) + ADAM_WD * w)
    return delta, m, v


def adamw(w, g, m, v, *, name):
    r, c = w.shape
    br = _blk(256, r)

    def body(w_ref, g_ref, m_ref, v_ref, go_ref, d_ref, mo_ref, vo_ref):
        gv = g_ref[...]
        go_ref[...] = gv
        d_ref[...], mo_ref[...], vo_ref[...] = _adamw_math(w_ref[...], gv, m_ref[...], v_ref[...])

    spec = pl.BlockSpec((br, c), lambda i: (i, 0))
    return pl.pallas_call(
        body, name=name, grid=(r // br,), in_specs=[spec] * 4, out_specs=[spec] * 4,
        out_shape=[jax.ShapeDtypeStruct((r, c), F32)] * 4, compiler_params=_params(("parallel",)),
    )(w, g, m, v)


def _f2(a):
    return a.reshape(a.shape[-2:])


def _local_step(x0, mem, tgt, sp, plan):
    t, d = x0.shape
    mix_a = sp["pool_scale"].shape[1]
    small = {}

    def row(a, l):
        return a[l:l + 1]

    def rows4(g):
        return g.reshape(N_CHIPS, -1, g.shape[-1])

    def xattn_f(l, xin):
        w = plan.weights(f"xa{l}")
        hx = rmsnorm_fwd(xin, row(sp["xa_norm"], l), name=f"xa_norm_f{l}")
        q = _f2(matmul(vm2(hx), w["wq"], "nn", out_dtype=BF16, side=plan.take_fwd(), name=f"xa_q_f{l}"))
        mn = rmsnorm_fwd(mem, row(sp["xa_mem_norm"], l), name=f"xa_memnorm_f{l}")
        kv = _f2(matmul(vm2(mn), w["wkv"], "nn", out_dtype=BF16, name=f"xa_kv_f{l}"))
        o = xattn_fwd(q, kv, name=f"xa_attn_f{l}")
        xout = _f2(matmul(vm2(o), w["wo"], "nn", out_dtype=F32, res=vm2(xin), side=plan.take_fwd(), name=f"xa_o_f{l}"))
        return xout, (xin, hx, q, mn, kv, o)

    def ffn_f(l, xin):
        w = plan.weights(f"ffn{l}")
        hf = rmsnorm_fwd(xin, row(sp["ffn_norm"], l), name=f"ffn_norm_f{l}")
        a = _f2(matmul(vm2(hf), w["gate"], "nn", out_dtype=BF16, side=plan.take_fwd(), name=f"ffn_gate_f{l}"))
        b, s = matmul(vm2(hf), w["up"], "nn", out_dtype=BF16, epi=(_swiglu_epi, [vm2(a)], 2), side=plan.take_fwd(), name=f"ffn_up_f{l}")
        b, s = _f2(b), _f2(s)
        xout = _f2(matmul(vm2(s), w["down"], "nn", out_dtype=F32, res=vm2(xin), side=plan.take_fwd(), name=f"ffn_down_f{l}"))
        return xout, (xin, hf, a, b, s)

    ev = plan.weights("ev")
    h0 = rmsnorm_fwd(x0, sp["ev_norm"], name="ev_norm_f")
    z = _f2(matmul(vm2(h0), ev["ev_in"], "nn", out_dtype=F32, side=plan.take_fwd(), name="ev_in_f"))
    cat = pool_fwd(z, ev["pool"], sp["pool_scale"], name="pool_f")
    cat, states = hgrn_fwd(z, cat, sp["lb"], sp["hg_gain"], mix_a, side=plan.take_fwd(), name="hgrn_f")
    x1 = _f2(matmul(VM(cat), ev["ev_out"], "nn", out_dtype=F32, res=vm2(x0), side=plan.take_fwd(), name="ev_out_f"))
    x2, xa0 = xattn_f(0, x1)
    x3, ff0 = ffn_f(0, x2)

    od = plan.weights("od")
    ho = rmsnorm_fwd(x3, sp["od_norm"], name="od_norm_f")
    qkv = matmul(vm2(ho), od["wqkv"], "nn", out_dtype=BF16, out_p=3, side=plan.take_fwd(), name="od_qkv_f")
    zf = _f2(matmul(vm2(ho), od["wf"], "nn", out_dtype=F32, name="od_fl_f"))
    fcum = fl_fwd(zf, sp["bf"], name="od_forget_f")
    nh = d // FOX_HEAD
    nfb = t // _blk(FOX_BLK, t)
    fk = fcum[:, :nh].T.reshape(nh, nfb, 1, t // nfb)
    of, lse = fox_fwd(qkv, fk, side=plan.take_fwd(), name="fox_f")
    x4 = _f2(matmul(vm2(of), od["od_out"], "nn", out_dtype=F32, res=vm2(x3), name="od_out_f"))
    x5, xa1 = xattn_f(1, x4)
    x6, ff1 = ffn_f(1, x5)
    loss, dx, dxb, small["final_norm"] = loss_head(x6, sp["final_norm"], tgt, name="loss_head")

    def ffn_b(l, saved, dx, dxb):
        xin, hf, a, b, s = saved
        w = plan.weights(f"ffn{l}")
        da, db = matmul(vm2(dxb), w["down"], "nt", out_dtype=BF16, epi=(_swiglu_bwd_epi, [vm2(a), vm2(b)], 2), side=plan.take_bwd(1), name=f"ffn_down_bx{l}")
        da, db = _f2(da), _f2(db)
        g_down = rows4(matmul(vm2(s), vm2(dxb), "tn", out_dtype=BF16, name=f"ffn_down_bw{l}"))
        g_gate = matmul(vm2(hf), vm2(da), "tn", out_dtype=BF16, out_p=N_CHIPS, name=f"ffn_gate_bw{l}")
        g_up = matmul(vm2(hf), vm2(db), "tn", out_dtype=BF16, out_p=N_CHIPS, name=f"ffn_up_bw{l}")
        plan.grads_done({f"down{l}": g_down, f"gate{l}": g_gate, f"up{l}": g_up})
        dh = matmul(vm2(da), w["gate"], "nt", out_dtype=F32, side=plan.take_bwd(), name=f"ffn_gate_bx{l}")
        dh = _f2(matmul(vm2(db), w["up"], "nt", out_dtype=BF16, res=VM(dh), side=plan.take_bwd(), name=f"ffn_up_bx{l}"))
        dx, dxb, dg = rmsnorm_bwd(xin, row(sp["ffn_norm"], l), dh, dx, name=f"ffn_norm_b{l}")
        return dx, dxb, dg

    def xattn_b(l, saved, dx, dxb):
        xin, hx, q, mn, kv, o = saved
        w = plan.weights(f"xa{l}")
        do = _f2(matmul(vm2(dxb), w["wo"], "nt", out_dtype=BF16, side=plan.take_bwd(), name=f"xa_o_bx{l}"))
        g_wo = rows4(matmul(vm2(o), vm2(dxb), "tn", out_dtype=BF16, name=f"xa_o_bw{l}"))
        dq, dkv = xattn_bwd(q, kv, do, name=f"xa_attn_b{l}")
        g_wq = rows4(matmul(vm2(hx), vm2(dq), "tn", out_dtype=BF16, name=f"xa_q_bw{l}"))
        dh = _f2(matmul(vm2(dq), w["wq"], "nt", out_dtype=BF16, name=f"xa_q_bx{l}"))
        dkvb = dkv.astype(BF16)
        g_wkv = matmul(vm2(mn), vm2(dkvb), "tn", out_dtype=BF16, out_p=N_CHIPS, name=f"xa_kv_bw{l}")
        plan.grads_done({f"wo{l}": g_wo, f"wq{l}": g_wq, f"wkv{l}": g_wkv})
        dmn = _f2(matmul(vm2(dkvb), w["wkv"], "nt", out_dtype=F32, side=plan.take_bwd(), name=f"xa_kv_bx{l}"))
        (dgm,) = rmsnorm_bwd(mem, row(sp["xa_mem_norm"], l), dmn, None, name=f"xa_memnorm_b{l}")
        dx, dxb, dg = rmsnorm_bwd(xin, row(sp["xa_norm"], l), dh, dx, name=f"xa_norm_b{l}")
        return dx, dxb, dg, dgm

    dg_ffn, dg_xa, dg_mem = [None, None], [None, None], [None, None]
    dx, dxb, dg_ffn[1] = ffn_b(1, ff1, dx, dxb)
    dx, dxb, dg_xa[1], dg_mem[1] = xattn_b(1, xa1, dx, dxb)

    do = _f2(matmul(vm2(dxb), od["od_out"], "nt", out_dtype=BF16, side=plan.take_bwd(), name="od_out_bx"))
    g_od_out = rows4(matmul(vm2(of), vm2(dxb), "tn", out_dtype=BF16, name="od_out_bw"))
    dz3, delta = fox_bwd_dq(qkv, fk, do, lse, side=plan.take_bwd(1), name="fox_bq")
    dz3, dfk = fox_bwd_dkv(qkv, fk, do, lse, delta, dz3, side=plan.take_bwd(1), name="fox_bkv")
    dfc = jnp.pad(dfk.reshape(nh, t).T, ((0, 0), (0, zf.shape[1] - nh)))
    dzf, dbf = fl_bwd(dfc, zf, sp["bf"], name="od_forget_b")
    dqkv = VM(dz3, "cs", pfn=lambda p: lax.rem(p + 2, 3))
    dwqkv = _f2(matmul(vm2(ho), dqkv, "tn", out_dtype=BF16, name="od_qkv_bw"))
    dwf = _f2(matmul(vm2(ho), vm2(dzf), "tn", out_dtype=BF16, name="od_fl_bw"))
    od_in_full = jnp.concatenate([dwqkv, dwf[:, :nh]], axis=1)
    plan.grads_done({"od_out": g_od_out, "od_in": od_in_full.reshape(d, N_CHIPS, -1).transpose(1, 0, 2)})
    dh = matmul(dqkv, od["wqkv"], "nt", out_dtype=F32, side=plan.take_bwd(), name="od_qkv_bx")
    dh = _f2(matmul(vm2(dzf), od["wf"], "nt", out_dtype=BF16, res=VM(dh), name="od_fl_bx"))
    dx, dxb, small["od_norm"] = rmsnorm_bwd(x3, sp["od_norm"], dh, dx, name="od_norm_b")
    small["bf"] = dbf

    dx, dxb, dg_ffn[0] = ffn_b(0, ff0, dx, dxb)
    dx, dxb, dg_xa[0], dg_mem[0] = xattn_b(0, xa0, dx, dxb)

    dcat = matmul(vm2(dxb), ev["ev_out"], "nt", out_dtype=BF16, out_p=2, side=plan.take_bwd(), name="ev_out_bx")
    g_ev_out = rows4(matmul(VM(cat), vm2(dxb), "tn", out_dtype=BF16, name="ev_out_bw"))
    dz5, g_pool, small["pool_scale"] = pool_bwd(z, dcat, ev["pool"], sp["pool_scale"], name="pool_b")
    dz5, small["lb"], dgn = hgrn_bwd(z, dcat, dz5, states, sp["lb"], sp["hg_gain"], mix_a, side=plan.take_bwd(1), name="hgrn_b")
    small["hg_gain"] = jnp.sum(dgn, axis=0)
    dzv = VM(dz5, "cs", pfn=lambda p: lax.rem(p + 4, 5))
    g_ev_in = _f2(matmul(vm2(h0), dzv, "tn", out_dtype=BF16, side=plan.take_bwd(1), name="ev_in_bw"))
    g_ev_in = g_ev_in.reshape(d, N_CHIPS, -1).transpose(1, 0, 2)
    ng, gsz = g_pool.shape[0], g_pool.shape[1]
    pool_parts = g_pool.reshape(ng, N_CHIPS, gsz // N_CHIPS, gsz).transpose(1, 0, 2, 3).reshape(N_CHIPS, gsz, gsz).astype(BF16)
    plan.grads_done({"ev_out": g_ev_out, "pool": pool_parts, "ev_in": g_ev_in}, now=True)
    dh = _f2(matmul(dzv, ev["ev_in"], "nt", out_dtype=BF16, side=plan.take_bwd(1), name="ev_in_bx"))
    dx, _, small["ev_norm"] = rmsnorm_bwd(x0, sp["ev_norm"], dh, dx, name="ev_norm_b")

    small["xa_norm"] = jnp.concatenate(dg_xa, axis=0)
    small["xa_mem_norm"] = jnp.concatenate(dg_mem, axis=0)
    small["ffn_norm"] = jnp.concatenate(dg_ffn, axis=0)
    return loss, dx, small


def gather_forward(fulls, *, name):
    n = len(fulls)

    def body(*refs):
        outs = refs[n:2 * n]
        ssem, rsem = refs[2 * n:]
        x, y, c, chips = _me()
        sibling = (x, y, 1 - c)

        def rows(w, j, which):
            h = fulls[w].shape[1] // 2
            return outs[w].at[_chip_id(chips[j]), pl.ds(which * h, h)]

        def swap(w, j):
            return _rcopy(rows(w, j, c), rows(w, j, c), ssem.at[w, j], rsem.at[w, j], sibling)

        for w in range(n):
            for j in range(3):
                swap(w, j).start()
        for w in range(n):
            for j in range(3):
                swap(w, j).wait_send()
                _rcopy(rows(w, j, 1 - c), rows(w, j, 1 - c), ssem.at[w, j], rsem.at[w, j], sibling).wait_recv()

    return pl.pallas_call(
        body, name=name, in_specs=[_ANY] * n, out_specs=[_ANY] * n,
        out_shape=[jax.ShapeDtypeStruct(f.shape, f.dtype) for f in fulls], input_output_aliases={w: w for w in range(n)},
        scratch_shapes=[pltpu.SemaphoreType.DMA((n, 3)), pltpu.SemaphoreType.DMA((n, 3))],
    )(*fulls)


def gather_shards(shards, *, name):
    n = len(shards)

    def body(*refs):
        ins, outs = refs[:n], refs[n:2 * n]
        ssem, rsem = refs[2 * n:]
        x, y, c, chips = _me()
        mine = _chip_id((x, y))
        sibling = (x, y, 1 - c)

        def rows(w, chip_id, which):
            h = shards[w].shape[0] // 2
            return outs[w].at[chip_id, pl.ds(which * h, h)]

        def to_chip(w, j):
            h = shards[w].shape[0] // 2
            return _rcopy(ins[w].at[pl.ds(c * h, h)], rows(w, mine, c), ssem.at[w, j], rsem.at[w, j], (*chips[j], c))

        def from_chip(w, j):
            r = rows(w, _chip_id(chips[j]), c)
            return _rcopy(r, r, ssem.at[w, j], rsem.at[w, j], (*chips[j], c))

        def to_sibling(w, j):
            r = rows(w, _chip_id(chips[j]), c)
            return _rcopy(r, r, ssem.at[w, 3 + j], rsem.at[w, 3 + j], sibling)

        def from_sibling(w, j):
            r = rows(w, _chip_id(chips[j]), 1 - c)
            return _rcopy(r, r, ssem.at[w, 3 + j], rsem.at[w, 3 + j], sibling)

        for w in range(n):
            for j in range(3):
                to_chip(w, j).start()
        for w in range(n):
            for j in range(3):
                from_chip(w, j).wait_recv()
                to_sibling(w, j).start()
        for w in range(n):
            for j in range(3):
                from_sibling(w, j).wait_recv()
        for w in range(n):
            for j in range(3):
                to_chip(w, j).wait_send()
                to_sibling(w, j).wait_send()

    return pl.pallas_call(
        body, name=name, in_specs=[_ANY] * n, out_specs=[_ANY] * n,
        out_shape=[jax.ShapeDtypeStruct((N_CHIPS,) + s.shape, s.dtype) for s in shards],
        scratch_shapes=[pltpu.SemaphoreType.DMA((n, 6)), pltpu.SemaphoreType.DMA((n, 6))],
    )(*shards)


def _ids_spec(grid, in_specs, out_specs):
    return pltpu.PrefetchScalarGridSpec(num_scalar_prefetch=1, grid=grid, in_specs=in_specs, out_specs=out_specs)


def fill_own(full, shard, ids, *, name):
    r, c = shard.shape
    br = _row_blk(r, 512)

    def body(ids_ref, s_ref, f_in, o_ref):
        del ids_ref, f_in
        o_ref[...] = s_ref[...]

    return pl.pallas_call(
        body, name=name, out_shape=jax.ShapeDtypeStruct(full.shape, full.dtype), input_output_aliases={2: 0},
        grid_spec=_ids_spec((r // br,), [pl.BlockSpec((br, c), lambda i, ids: (i, 0)), _ANY],
                            pl.BlockSpec((None, br, c), lambda i, ids: (ids[0], i, 0))),
        compiler_params=_params(("parallel",)),
    )(ids, shard, full)


def rs_pair(parts, *, name):
    n = len(parts)

    def body(*refs):
        ins, recv = refs[:n], refs[n:2 * n]
        ssem, rsem = refs[2 * n:]
        x, y, c, _ = _me()
        sibling = (x, y, 1 - c)

        def swap(w):
            h = parts[w].shape[1] // 2
            return _rcopy(ins[w].at[:, pl.ds((1 - c) * h, h), :], recv[w], ssem.at[w], rsem.at[w], sibling)

        for w in range(n):
            swap(w).start()
        for w in range(n):
            swap(w).wait()

    return pl.pallas_call(
        body, name=name, in_specs=[_ANY] * n, out_specs=[_ANY] * n,
        out_shape=[jax.ShapeDtypeStruct((p.shape[0], p.shape[1] // 2, p.shape[2]), p.dtype) for p in parts],
        scratch_shapes=[pltpu.SemaphoreType.DMA((n,)), pltpu.SemaphoreType.DMA((n,))],
    )(*parts)


def add_pair(part, recv, ids, *, name):
    p, h, c = recv.shape
    br = _row_blk(h, 512)
    nb = h // br

    def body(ids_ref, a_ref, b_ref, o_ref):
        del ids_ref
        o_ref[...] = (a_ref[...].astype(F32) + b_ref[...].astype(F32)).astype(o_ref.dtype)

    half = pl.BlockSpec((None, br, c), lambda k, i, ids: (k, i, 0))
    return pl.pallas_call(
        body, name=name, out_shape=jax.ShapeDtypeStruct(recv.shape, recv.dtype),
        grid_spec=_ids_spec((p, nb), [pl.BlockSpec((None, br, c), lambda k, i, ids: (k, ids[1] * nb + i, 0)), half], half),
        compiler_params=_params(("parallel", "parallel")),
    )(ids, part, recv)


def rs_chip(sums, *, name):
    n = len(sums)

    def body(*refs):
        ins, outs = refs[:n], refs[n:2 * n]
        ssem, rsem = refs[2 * n:]
        x, y, c, chips = _me()

        def swap(w, j):
            return _rcopy(ins[w].at[_chip_id(chips[j])], outs[w].at[j], ssem.at[w, j], rsem.at[w, j], (*chips[j], c))

        for w in range(n):
            for j in range(3):
                swap(w, j).start()
        for w in range(n):
            for j in range(3):
                swap(w, j).wait()

    return pl.pallas_call(
        body, name=name, in_specs=[_ANY] * n, out_specs=[_ANY] * n,
        out_shape=[jax.ShapeDtypeStruct((3,) + s.shape[1:], s.dtype) for s in sums],
        scratch_shapes=[pltpu.SemaphoreType.DMA((n, 3)), pltpu.SemaphoreType.DMA((n, 3))],
    )(*sums)


def add_chips(sums, landed, ids, group, layer, group_shape, *, name):
    _, h, c = sums.shape
    br = _row_blk(h, 256)
    nb = h // br

    def body(ids_ref, a_ref, b_ref, *rest):
        o_ref = rest[-1]
        tot = a_ref[...].astype(F32)
        for k in range(3):
            tot = tot + b_ref[k].astype(F32)
        o_ref[...] = tot

    in_specs = [pl.BlockSpec((None, br, c), lambda i, ids: (ids[0], i, 0)), pl.BlockSpec((3, br, c), lambda i, ids: (0, i, 0))]
    args = [ids, sums, landed]
    if group is not None:
        in_specs.append(_ANY)
        args.append(group)
    return pl.pallas_call(
        body, name=name, out_shape=jax.ShapeDtypeStruct(group_shape, F32),
        input_output_aliases={3: 0} if group is not None else {},
        grid_spec=_ids_spec((nb,), in_specs, pl.BlockSpec((None, br, c), lambda i, ids: (layer, ids[1] * nb + i, 0))),
        compiler_params=_params(("parallel",)),
    )(*args)


def rs_share(groups, slots, *, name):
    ng = len(groups)
    n = len(slots)

    def body(*refs):
        outs = refs[ng:2 * ng]
        ssem, rsem = refs[2 * ng:]
        x, y, c, _ = _me()
        sibling = (x, y, 1 - c)

        def rows(w, which):
            g, l = slots[w]
            h = groups[g].shape[1] // 2
            return outs[g].at[l, pl.ds(which * h, h), :]

        def swap(w):
            return _rcopy(rows(w, c), rows(w, c), ssem.at[w], rsem.at[w], sibling)

        for w in range(n):
            swap(w).start()
        for w in range(n):
            swap(w).wait_send()
            _rcopy(rows(w, 1 - c), rows(w, 1 - c), ssem.at[w], rsem.at[w], sibling).wait_recv()

    return pl.pallas_call(
        body, name=name, in_specs=[_ANY] * ng, out_specs=[_ANY] * ng,
        out_shape=[jax.ShapeDtypeStruct(g.shape, g.dtype) for g in groups],
        input_output_aliases={g: g for g in range(ng)},
        scratch_shapes=[pltpu.SemaphoreType.DMA((n,)), pltpu.SemaphoreType.DMA((n,))],
    )(*groups)


def allreduce_small(v, *, name):
    r, c = v.shape
    ndev = 2 * N_CHIPS

    def body(v_ref, o_ref, buf, ssem, rsem):
        x, y, cc, _ = _me()
        me = 4 * x + 2 * y + cc
        flips = [(a, b, d) for a in (0, 1) for b in (0, 1) for d in (0, 1)][1:]
        buf[me] = v_ref[...]
        cps = []
        for k, (a, b, d) in enumerate(flips):
            peer = (jnp.bitwise_xor(x, a), jnp.bitwise_xor(y, b), jnp.bitwise_xor(cc, d))
            cp = _rcopy(v_ref, buf.at[me], ssem.at[k], rsem.at[k], peer)
            cp.start()
            cps.append(cp)
        for k, (a, b, d) in enumerate(flips):
            peer = (jnp.bitwise_xor(x, a), jnp.bitwise_xor(y, b), jnp.bitwise_xor(cc, d))
            src = 4 * peer[0] + 2 * peer[1] + peer[2]
            _rcopy(v_ref, buf.at[src], ssem.at[k], rsem.at[k], peer).wait_recv()
        for cp in cps:
            cp.wait_send()
        tot = buf[0]
        for k in range(1, ndev):
            tot = tot + buf[k]
        o_ref[...] = tot

    vm = pl.BlockSpec(memory_space=pltpu.VMEM)
    return pl.pallas_call(
        body, name=name, in_specs=[vm], out_specs=vm, out_shape=jax.ShapeDtypeStruct((r, c), F32),
        scratch_shapes=[pltpu.VMEM((ndev, r, c), F32), pltpu.SemaphoreType.DMA((ndev - 1,)), pltpu.SemaphoreType.DMA((ndev - 1,))],
    )(v)


WEIGHTS = ["lb_table", "ev_norm", "ev_w_in", "ev_w_pool", "ev_pool_scale", "ev_hg_norm", "ev_w_out", "od_norm", "od_w_in",
           "od_b_f", "od_w_out", "xa_norm", "xa_mem_norm", "xa_wq", "xa_wkv", "xa_wo", "ffn_norm", "ffn_w_gate", "ffn_w_up",
           "ffn_w_down", "final_norm"]
BIG = ["ev_w_in", "ev_w_pool", "ev_w_out", "od_w_in", "od_w_out", "xa_wq", "xa_wkv", "xa_wo", "ffn_w_gate", "ffn_w_up", "ffn_w_down"]
SMALL_ROWS = 16


def _rows(parts, width):
    rows = [jnp.pad(p.reshape(-1, p.shape[-1]).astype(F32), ((0, 0), (0, width - p.shape[-1]))) for p in parts]
    out = jnp.concatenate(rows, axis=0)
    return jnp.pad(out, ((0, SMALL_ROWS - out.shape[0]), (0, 0)))


def _unrows(packed, like):
    out, r = [], 0
    for p in like:
        n = p.size // p.shape[-1]
        out.append(packed[r:r + n, :p.shape[-1]].reshape(p.shape))
        r += n
    return out


def _m3(a):
    return a.reshape(a.shape[0], -1, a.shape[-1])


SLOT = {"ev_in": ("ev_w_in", 0), "pool": ("ev_w_pool", 0), "ev_out": ("ev_w_out", 0), "od_in": ("od_w_in", 0),
        "od_out": ("od_w_out", 0)}
for _l in range(2):
    SLOT.update({f"wq{_l}": ("xa_wq", _l), f"wkv{_l}": ("xa_wkv", _l), f"wo{_l}": ("xa_wo", _l),
                 f"gate{_l}": ("ffn_w_gate", _l), f"up{_l}": ("ffn_w_up", _l), f"down{_l}": ("ffn_w_down", _l)})
GATHER_FIRST = ["ev_in", "ev_out", "pool", "od_norm"]
GATHER_CARRIED = [["wq0", "wo0"], ["wkv0", "gate0"], ["od_out"], ["wq1"], ["wo1"], ["up0"], ["down0"], ["od_in"], ["wkv1"],
                  ["gate1", "up1", "down1"]]


class _Lazy:
    def __init__(self, plan, group):
        self.plan, self.layer = plan, group[-1] if group[-1] in "01" else ""

    def __getitem__(self, key):
        return self.plan.w(key + self.layer if key in ("wq", "wo", "wkv", "gate", "up", "down") else key)


class _Plan:
    def __init__(self, shards, ids, group_shapes, d, nh):
        self.shards, self.ids, self.group_shapes, self.d, self.nh = shards, ids, group_shapes, d, nh
        self.full, self.cache = {}, {}
        self.queue, self.sides, self.fsides, self.forwarded = [list(u) for u in GATHER_CARRIED], [], [], set()
        self.parts, self.psides, self.sums, self.rqueue, self.rsides = [], [], {}, [], []
        got = gather_shards([shards[n] for n in GATHER_FIRST], name="gather_first")
        for n, f in zip(GATHER_FIRST, got):
            self.full[n] = fill_own(f, shards[n], ids, name=f"gather_own_{n}")

    def take_fwd(self):
        parts = []
        ready = [(ns, s) for ns, s in self.sides if s.outs is not None and ns[0] not in self.forwarded]
        for ns, s in ready:
            fs = ForwardSide(s.outs)
            self.fsides.append((ns, fs))
            self.forwarded.update(ns)
            parts.append(fs)
        if self.queue:
            names = self.queue.pop(0)
            side = GatherSide([self.shards[n] for n in names])
            self.sides.append((names, side))
            parts.append(side)
        return Sides(parts) if parts else None

    def _need(self, names):
        missing = [n for n in names if n not in self.full]
        if not missing:
            return
        done = {n: a for ns, s in self.fsides if s.outs is not None for n, a in zip(ns, s.outs)}
        landed = {n: a for ns, s in self.sides if s.outs is not None for n, a in zip(ns, s.outs)}
        pre = {n: done[n] for n in missing if n in done}
        half = [n for n in missing if n not in done and n in landed]
        late = [n for n in missing if n not in done and n not in landed]
        if half:
            self.forwarded.update(half)
            pre.update(zip(half, gather_forward([landed[n] for n in half], name=f"gather_forward_{half[0]}")))
        if late:
            self.queue = [u for u in ([n for n in u if n not in late] for u in self.queue) if u]
            pre.update(zip(late, gather_shards([self.shards[n] for n in late], name=f"gather_late_{late[0]}")))
        for n in missing:
            self.full[n] = fill_own(pre[n], self.shards[n], self.ids, name=f"gather_own_{n}")

    def w(self, name):
        if name in self.cache:
            return self.cache[name]
        if name in ("wqkv", "wf"):
            self._need(["od_in"])
            od_full = self.full["od_in"].transpose(1, 0, 2).reshape(self.d, -1)
            self.cache["wqkv"] = vm2(od_full[:, :3 * self.d])
            self.cache["wf"] = vm2(jnp.pad(od_full[:, 3 * self.d:], ((0, 0), (0, 128 - self.nh))))
            return self.cache[name]
        self._need([name])
        f = self.full[name]
        if name == "pool":
            rows, gsz = f.shape[1:]
            ng = rows * N_CHIPS // gsz
            out = f.reshape(N_CHIPS, ng, gsz // N_CHIPS, gsz).transpose(1, 0, 2, 3).reshape(ng, gsz, gsz)
        elif name == "ev_in":
            out = vm2(f.transpose(1, 0, 2).reshape(self.d, -1))
        else:
            out = VM(f, "cs") if name.rstrip("01") in ("wkv", "gate", "up") else vm2(f.reshape(-1, f.shape[-1]))
        self.cache[name] = out
        return out

    def weights(self, group):
        return _Lazy(self, group)

    def grads_done(self, parts, now=False):
        names = list(parts)
        if now:
            got = rs_pair([parts[n] for n in names], name=f"reduce_pair_{names[0]}")
            for n, g in zip(names, got):
                self.sums[n] = add_pair(parts[n], g, self.ids, name=f"reduce_add2_{n}")
            self.rqueue.append(names)
        else:
            self.parts.append((names, [parts[n] for n in names]))

    def _add_swapped(self):
        for names, parts, side in self.psides:
            if side.outs is not None and names[0] not in self.sums:
                for n, p, g in zip(names, parts, side.outs):
                    self.sums[n] = add_pair(p, g, self.ids, name=f"reduce_add2_{n}")
                self.rqueue.append(names)

    def take_bwd(self, units=0):
        self._add_swapped()
        sides = []
        for names, parts in self.parts:
            ps = PairSide(parts)
            self.psides.append((names, parts, ps))
            sides.append(ps)
        self.parts = []
        names = [n for u in self.rqueue[:units] for n in u]
        self.rqueue = self.rqueue[units:]
        if names:
            rs = ReduceSide([self.sums[n] for n in names])
            self.rsides.append((names, rs))
            sides.append(rs)
        return Sides(sides) if sides else None

    def finish(self):
        for names, parts in self.parts:
            self.grads_done(dict(zip(names, parts)), now=True)
        self._add_swapped()
        landed = {}
        for ns, side in self.rsides:
            landed.update(zip(ns, side.outs))
        rest = [n for u in self.rqueue for n in u]
        if rest:
            landed.update(zip(rest, rs_chip([self.sums[n] for n in rest], name="reduce_chips_rest")))
        gbig = {n: None for n in BIG}
        for n, (big, l) in SLOT.items():
            gbig[big] = add_chips(self.sums[n], landed[n], self.ids, gbig[big], l, self.group_shapes[big], name=f"reduce_add4_{n}")
        full = rs_share([gbig[n] for n in BIG], [(BIG.index(big), l) for big, l in SLOT.values()], name="reduce_share")
        return dict(zip(BIG, full))


def kernel(x, mem, lb_table, ev_norm, ev_w_in, ev_w_pool, ev_pool_scale, ev_hg_norm, ev_w_out, od_norm, od_w_in, od_b_f, od_w_out, xa_norm, xa_mem_norm, xa_wq, xa_wkv, xa_wo, ffn_norm, ffn_w_gate, ffn_w_up, ffn_w_down, final_norm, loss_target, m_lb_table, m_ev_norm, m_ev_w_in, m_ev_w_pool, m_ev_pool_scale, m_ev_hg_norm, m_ev_w_out, m_od_norm, m_od_w_in, m_od_b_f, m_od_w_out, m_xa_norm, m_xa_mem_norm, m_xa_wq, m_xa_wkv, m_xa_wo, m_ffn_norm, m_ffn_w_gate, m_ffn_w_up, m_ffn_w_down, m_final_norm, v_lb_table, v_ev_norm, v_ev_w_in, v_ev_w_pool, v_ev_pool_scale, v_ev_hg_norm, v_ev_w_out, v_od_norm, v_od_w_in, v_od_b_f, v_od_w_out, v_xa_norm, v_xa_mem_norm, v_xa_wq, v_xa_wkv, v_xa_wo, v_ffn_norm, v_ffn_w_gate, v_ffn_w_up, v_ffn_w_down, v_final_norm):
    a = dict(locals())
    w = {n: a[n] for n in WEIGHTS}
    mom = {n: a["m_" + n] for n in WEIGHTS}
    var = {n: a["v_" + n] for n in WEIGHTS}
    _, t, d = x.shape
    nh = d // FOX_HEAD
    lanes = 128
    cx, cy = lax.axis_index("x"), lax.axis_index("y")
    chip = 2 * cx + cy

    w3 = {n: _m3(w[n]) for n in BIG}
    flat = lambda v: v.reshape(-1, v.shape[-1])
    shards = {"od_norm": jnp.broadcast_to(od_norm, (16, od_norm.shape[1]))}
    for name, (big, l) in SLOT.items():
        shards[name] = w3[big][l].astype(BF16)
    ids = jnp.stack([chip, lax.axis_index("c")]).astype(jnp.int32)
    plan = _Plan(shards, ids, {n: w3[n].shape for n in BIG}, d, nh)
    od_norm_full = plan.full["od_norm"][:, 0, :].reshape(1, d)

    sm = jax.nn.softmax(lb_table, axis=0)
    sp = {
        "lb": sm[1:2], "ev_norm": ev_norm, "pool_scale": ev_pool_scale, "hg_gain": ev_hg_norm, "od_norm": od_norm_full,
        "bf": jnp.pad(od_b_f, ((0, 0), (0, lanes - nh))), "xa_norm": xa_norm, "xa_mem_norm": xa_mem_norm, "ffn_norm": ffn_norm,
        "final_norm": final_norm.reshape(1, d),
    }
    loss_l, gx, small = _local_step(x[0], mem[0], loss_target[0], sp, plan)
    loss = lax.psum(loss_l[0, 0], ("x", "y", "c"))
    gbig = plan.finish()

    raw_like = [small["lb"], small["ev_norm"], small["pool_scale"], small["hg_gain"], small["od_norm"], small["bf"],
                small["xa_norm"], small["xa_mem_norm"], small["ffn_norm"], small["final_norm"]]
    summed = _unrows(allreduce_small(_rows(raw_like, d), name="reduce_small"), raw_like)
    dlb, g_ev_norm, g_pool_scale, g_hg, g_od_norm_full, g_bf, g_xa, g_xam, g_ffn, g_final = summed
    dsm = jnp.zeros_like(sm).at[1:2].set(dlb)
    gsmall = {
        "lb_table": sm * (dsm - jnp.sum(sm * dsm, axis=0, keepdims=True)), "ev_norm": g_ev_norm, "ev_pool_scale": g_pool_scale,
        "ev_hg_norm": g_hg, "od_norm": lax.dynamic_slice_in_dim(g_od_norm_full, chip * od_norm.shape[1], od_norm.shape[1], axis=1),
        "od_b_f": g_bf[:, :nh], "xa_norm": g_xa, "xa_mem_norm": g_xam, "ffn_norm": g_ffn, "final_norm": g_final.reshape(d),
    }

    grad, delta, new_m, new_v = {}, {}, {}, {}
    for n in BIG:
        shp = w[n].shape
        res = adamw(flat(w3[n]), flat(gbig[n]), flat(_m3(mom[n])), flat(_m3(var[n])), name=f"adamw_{n}")
        grad[n], delta[n], new_m[n], new_v[n] = [r.reshape(shp) for r in res]
    snames = [n for n in WEIGHTS if n not in BIG]
    like = [w[n] for n in snames]
    res = adamw(_rows(like, d), _rows([gsmall[n] for n in snames], d), _rows([mom[n] for n in snames], d),
                _rows([var[n] for n in snames], d), name="adamw_small")
    for vals, dst in zip(res, (grad, delta, new_m, new_v)):
        dst.update(zip(snames, _unrows(vals, like)))
    return (loss, gx.reshape(x.shape), *[grad[n] for n in WEIGHTS], *[delta[n] for n in WEIGHTS],
            *[new_m[n] for n in WEIGHTS], *[new_v[n] for n in WEIGHTS])
```

```python
import functools
import math

import jax
import jax.numpy as jnp
from jax import lax
from jax.experimental import pallas as pl
from jax.experimental.pallas import tpu as pltpu

F32 = jnp.float32
BF16 = jnp.bfloat16
MESH = pl.DeviceIdType.MESH

V7X_VMEM_LIMIT_BYTES = 56 * 1024 * 1024
N_CHIPS = 4

EPS = 1e-6
POOL_WINDOWS = (2, 4, 8, 16)
POOL_HALO = 16
HG_HEAD = 128
HG_CHUNK = 64
FOX_HEAD = 128
FOX_BLK = 512
XA_HEADS = 4
ADAM_LR, ADAM_B1, ADAM_B2, ADAM_EPS, ADAM_WD, ADAM_STEP = 0.001, 0.9, 0.999, 1e-08, 0.01, 10
EXP_CLAMP = 80.0


def _params(sem=None):
    return pltpu.CompilerParams(dimension_semantics=sem, vmem_limit_bytes=V7X_VMEM_LIMIT_BYTES)


def _blk(pref, dim):
    b = min(pref, dim)
    assert dim % b == 0, (pref, dim)
    return b


class VM:
    def __init__(self, arr, kind="cs", lead=(), inner=(), pfn=None):
        self.arr, self.kind, self.lead, self.inner = arr, kind, tuple(lead), tuple(inner)
        self.pfn = pfn or (lambda p: p)
        p = arr.shape[len(self.lead)]
        r, c = arr.shape[-2:]
        assert arr.ndim == len(self.lead) + 1 + len(self.inner) + 2, (arr.shape, lead, inner)
        self.P = p
        self.shape = (r, c * p) if kind == "cs" else (r * p, c)
        self.dtype = arr.dtype

    def spec(self, br, bc, rfn, cfn):
        p = self.P
        r, c = self.arr.shape[-2:]
        assert c % bc == 0 and r % br == 0, (self.arr.shape, br, bc)
        if p == 1:
            def imap(*g):
                return (*self.lead, self.pfn(0), *self.inner, rfn(*g), cfn(*g))
        elif self.kind == "cs":
            per = c // bc

            def imap(*g):
                cb = cfn(*g)
                return (*self.lead, self.pfn(lax.div(cb, per)), *self.inner, rfn(*g), lax.rem(cb, per))
        else:
            per = r // br

            def imap(*g):
                rb = rfn(*g)
                return (*self.lead, self.pfn(lax.div(rb, per)), *self.inner, lax.rem(rb, per), cfn(*g))
        return pl.BlockSpec((None,) * (self.arr.ndim - 2) + (br, bc), imap)


def vm2(arr):
    return VM(arr.reshape((1,) + arr.shape))


def _out_struct(shape, kind, p, dtype):
    r, c = shape
    return jax.ShapeDtypeStruct((p, r, c // p) if kind == "cs" else (p, r // p, c), dtype)


_ANY = pl.BlockSpec(memory_space=pl.ANY)


def _me():
    x, y, c = lax.axis_index("x"), lax.axis_index("y"), lax.axis_index("c")
    chips = [(1 - x, y), (x, 1 - y), (1 - x, 1 - y)]
    return x, y, c, chips


def _chip_id(xy):
    return 2 * xy[0] + xy[1]


def _rcopy(src, dst, ssem, rsem, dev):
    return pltpu.make_async_remote_copy(src_ref=src, dst_ref=dst, send_sem=ssem, recv_sem=rsem, device_id=dev,
                                        device_id_type=MESH)


class GatherSide:
    def __init__(self, shards):
        self.inputs = list(shards)
        self.out_shape = [jax.ShapeDtypeStruct((N_CHIPS,) + s.shape, s.dtype) for s in shards]
        self.aliases = {}
        self.rows = len(shards)
        self.outs = None

    def _copy(self, ins, outs, ssem, rsem, w, j, receive):
        x, y, c, chips = _me()
        h = self.inputs[w].shape[0] // 2
        half = pl.ds(c * h, h)
        if receive:
            r = outs[w].at[_chip_id(chips[j]), half]
            return _rcopy(r, r, ssem.at[w, j], rsem.at[w, j], (*chips[j], c))
        return _rcopy(ins[w].at[half], outs[w].at[_chip_id((x, y)), half], ssem.at[w, j], rsem.at[w, j], (*chips[j], c))

    def start(self, ins, outs, ssem, rsem):
        for w in range(len(self.inputs)):
            for j in range(3):
                self._copy(ins, outs, ssem, rsem, w, j, False).start()

    def finish(self, ins, outs, ssem, rsem):
        for w in range(len(self.inputs)):
            for j in range(3):
                self._copy(ins, outs, ssem, rsem, w, j, True).wait_recv()
                self._copy(ins, outs, ssem, rsem, w, j, False).wait_send()


class ForwardSide:
    def __init__(self, fulls, shards):
        self.inputs = list(fulls) + list(shards)
        self.out_shape = [jax.ShapeDtypeStruct(f.shape, f.dtype) for f in fulls]
        self.aliases = {w: w for w in range(len(fulls))}
        self.rows = len(fulls)
        self.outs = None

    def _copy(self, outs, ssem, rsem, w, j, receive):
        x, y, c, chips = _me()
        h = self.inputs[w].shape[1] // 2
        r = outs[w].at[_chip_id(chips[j]), pl.ds(((1 - c) if receive else c) * h, h)]
        return _rcopy(r, r, ssem.at[w, j], rsem.at[w, j], (x, y, 1 - c))

    def _own(self, ins, outs, ssem, rsem, w):
        x, y, c, _ = _me()
        return _rcopy(ins[self.rows + w], outs[w].at[_chip_id((x, y))], ssem.at[w, 3], rsem.at[w, 3], (x, y, 1 - c))

    def start(self, ins, outs, ssem, rsem):
        for w in range(self.rows):
            self._own(ins, outs, ssem, rsem, w).start()
            for j in range(3):
                self._copy(outs, ssem, rsem, w, j, False).start()

    def finish(self, ins, outs, ssem, rsem):
        for w in range(self.rows):
            self._own(ins, outs, ssem, rsem, w).wait()
            for j in range(3):
                self._copy(outs, ssem, rsem, w, j, False).wait_send()
                self._copy(outs, ssem, rsem, w, j, True).wait_recv()


class PairSide:
    def __init__(self, parts):
        self.inputs = list(parts)
        self.out_shape = [jax.ShapeDtypeStruct((p.shape[0], p.shape[1] // 2, p.shape[2]), p.dtype) for p in parts]
        self.aliases = {}
        self.rows = len(parts)
        self.outs = None

    def _copy(self, ins, outs, ssem, rsem, w):
        x, y, c, _ = _me()
        h = self.inputs[w].shape[1] // 2
        return _rcopy(ins[w].at[:, pl.ds((1 - c) * h, h), :], outs[w], ssem.at[w, 0], rsem.at[w, 0], (x, y, 1 - c))

    def start(self, ins, outs, ssem, rsem):
        for w in range(self.rows):
            self._copy(ins, outs, ssem, rsem, w).start()

    def finish(self, ins, outs, ssem, rsem):
        for w in range(self.rows):
            self._copy(ins, outs, ssem, rsem, w).wait()


class _SemRows:
    def __init__(self, sem, off):
        self.sem, self.off = sem, off

    @property
    def at(self):
        return self

    def __getitem__(self, idx):
        return self.sem.at[self.off + idx[0], idx[1]]


class Sides:
    def __init__(self, sides):
        self.sides = list(sides)
        self.inputs = [a for s in self.sides for a in s.inputs]
        self.out_shape = [o for s in self.sides for o in s.out_shape]
        self.rows = sum(s.rows for s in self.sides)
        self.aliases, i0, o0 = {}, 0, 0
        for s in self.sides:
            self.aliases.update({i0 + i: o0 + o for i, o in s.aliases.items()})
            i0, o0 = i0 + len(s.inputs), o0 + len(s.out_shape)

    def _each(self, method, ins, outs, ssem, rsem):
        i0 = o0 = r0 = 0
        for s in self.sides:
            getattr(s, method)(ins[i0:i0 + len(s.inputs)], outs[o0:o0 + len(s.out_shape)], _SemRows(ssem, r0), _SemRows(rsem, r0))
            i0, o0, r0 = i0 + len(s.inputs), o0 + len(s.out_shape), r0 + s.rows

    def start(self, ins, outs, ssem, rsem):
        self._each("start", ins, outs, ssem, rsem)

    def finish(self, ins, outs, ssem, rsem):
        self._each("finish", ins, outs, ssem, rsem)

    @property
    def outs(self):
        return None

    @outs.setter
    def outs(self, vals):
        o0 = 0
        for s in self.sides:
            s.outs = list(vals[o0:o0 + len(s.out_shape)])
            o0 += len(s.out_shape)


class ReduceSide:
    def __init__(self, sums):
        self.inputs = list(sums)
        self.out_shape = [jax.ShapeDtypeStruct((3,) + s.shape[1:], s.dtype) for s in sums]
        self.aliases = {}
        self.rows = len(sums)
        self.outs = None

    def _copy(self, ins, outs, ssem, rsem, w, j):
        _, _, c, chips = _me()
        return _rcopy(ins[w].at[_chip_id(chips[j])], outs[w].at[j], ssem.at[w, j], rsem.at[w, j], (*chips[j], c))

    def start(self, ins, outs, ssem, rsem):
        for w in range(len(self.inputs)):
            for j in range(3):
                self._copy(ins, outs, ssem, rsem, w, j).start()

    def finish(self, ins, outs, ssem, rsem):
        for w in range(len(self.inputs)):
            for j in range(3):
                self._copy(ins, outs, ssem, rsem, w, j).wait()


def _call(body, side, *, name, grid, in_specs, out_specs, out_shape, scratch_shapes=(), sem, aliases=None, args):
    if side is None:
        return pl.pallas_call(body, name=name, grid=grid, in_specs=in_specs, out_specs=out_specs, out_shape=out_shape,
                              scratch_shapes=list(scratch_shapes), input_output_aliases=aliases or {},
                              compiler_params=_params(sem))(*args)
    single = not isinstance(out_shape, (list, tuple))
    oshape, ospecs = ([out_shape], [out_specs]) if single else (list(out_shape), list(out_specs))
    n_in, n_out, s_in, s_out = len(in_specs), len(oshape), len(side.inputs), len(side.out_shape)

    def wrapped(*refs):
        ins, sin = refs[:n_in], refs[n_in:n_in + s_in]
        outs = refs[n_in + s_in:n_in + s_in + n_out]
        souts = refs[n_in + s_in + n_out:n_in + s_in + n_out + s_out]
        rest = refs[n_in + s_in + n_out + s_out:]
        scratch, (ssem, rsem) = rest[:-2], rest[-2:]
        first = functools.reduce(jnp.logical_and, [pl.program_id(a) == 0 for a in range(len(grid))])
        last = functools.reduce(jnp.logical_and, [pl.program_id(a) == grid[a] - 1 for a in range(len(grid))])

        @pl.when(first)
        def _():
            side.start(sin, souts, ssem, rsem)

        body(*ins, *outs, *scratch)

        @pl.when(last)
        def _():
            side.finish(sin, souts, ssem, rsem)

    sems = pltpu.SemaphoreType.DMA((side.rows, 4))
    res = pl.pallas_call(
        wrapped, name=name, grid=grid, in_specs=list(in_specs) + [_ANY] * s_in, out_specs=ospecs + [_ANY] * s_out,
        out_shape=oshape + side.out_shape, scratch_shapes=list(scratch_shapes) + [sems, sems],
        input_output_aliases={**(aliases or {}), **{n_in + i: n_out + o for i, o in side.aliases.items()}},
        compiler_params=_params(("arbitrary",) * len(grid)),
    )(*args, *side.inputs)
    side.outs = list(res[n_out:])
    return res[0] if single else list(res[:n_out])


def _best(g, cap):
    if g <= cap:
        return g
    cands = [d for d in range(128, cap + 1, 128) if g % d == 0]
    assert cands, (g, cap)
    return cands[-1]


def _row_blk(n, cap):
    cands = [d for d in range(16, min(n, cap) + 1, 16) if n % d == 0]
    assert cands, (n, cap)
    return cands[-1]


def _tiles(a, b, mode, out_kind, out_p, bm, bn, bk):
    def cpiece(v):
        return v.arr.shape[-1] if v.kind == "cs" else v.shape[1]

    def rpiece(v):
        return v.arr.shape[-2] if v.kind == "rs" else v.shape[0]

    if mode == "nn":
        m, n = a.shape[0], b.shape[1]
        gm, gn, gk = rpiece(a), cpiece(b), math.gcd(cpiece(a), rpiece(b))
    elif mode == "nt":
        m, n = a.shape[0], b.shape[0]
        gm, gn, gk = rpiece(a), rpiece(b), math.gcd(cpiece(a), cpiece(b))
    else:
        m, n = a.shape[1], b.shape[1]
        gm, gn, gk = cpiece(a), cpiece(b), math.gcd(rpiece(a), rpiece(b))
    if out_kind == "cs":
        gn = math.gcd(gn, n // out_p)
    else:
        gm = math.gcd(gm, m // out_p)
    caps = {"nn": (1024, 1536, 2048), "nt": (512, 2048, 2048), "tn": (1536, 1536, 2048)}[mode]
    return (bm or _best(gm, caps[0])), (bn or _best(gn, caps[1])), (bk or _best(gk, caps[2]))


def matmul(a, b, mode, *, out_dtype, bm=None, bn=None, bk=None, out_kind="cs", out_p=1, out_pfn=None, res=None, epi=None,
           side=None, name):
    bm, bn, bk = _tiles(a, b, mode, out_kind, out_p, bm, bn, bk)
    if mode == "nn":
        (m, k), (k2, n) = a.shape, b.shape
        a_spec = a.spec(bm, bk, lambda i, j, kk: i, lambda i, j, kk: kk)
        b_spec = b.spec(bk, bn, lambda i, j, kk: kk, lambda i, j, kk: j)
        dims = (((1,), (0,)), ((), ()))
    elif mode == "nt":
        (m, k), (n, k2) = a.shape, b.shape
        a_spec = a.spec(bm, bk, lambda i, j, kk: i, lambda i, j, kk: kk)
        b_spec = b.spec(bn, bk, lambda i, j, kk: j, lambda i, j, kk: kk)
        dims = (((1,), (1,)), ((), ()))
    else:
        (k, m), (k2, n) = a.shape, b.shape
        a_spec = a.spec(bk, bm, lambda i, j, kk: kk, lambda i, j, kk: i)
        b_spec = b.spec(bk, bn, lambda i, j, kk: kk, lambda i, j, kk: j)
        dims = (((0,), (0,)), ((), ()))
    assert k == k2, (a.shape, b.shape, mode)
    assert m % bm == 0 and n % bn == 0 and k % bk == 0, (m, n, k, bm, bn, bk)
    nk = k // bk
    out_sds = _out_struct((m, n), out_kind, out_p, out_dtype)
    out_vm = VM(out_sds, out_kind, pfn=out_pfn)
    o_spec = out_vm.spec(bm, bn, lambda i, j, kk: i, lambda i, j, kk: j)
    in_specs, args = [a_spec, b_spec], [a.arr, b.arr]
    tiles = ([res] if res is not None else []) + (list(epi[1]) if epi else [])
    for v in tiles:
        assert v.shape == (m, n)
        in_specs.append(v.spec(bm, bn, lambda i, j, kk: i, lambda i, j, kk: j))
        args.append(v.arr)
    n_out = epi[2] if epi else 1

    def body(a_ref, b_ref, *rest):
        t_refs, o_refs = rest[:len(tiles)], rest[len(tiles):len(tiles) + n_out]
        part = lax.dot_general(a_ref[...], b_ref[...], dims, preferred_element_type=F32)

        def write(tot):
            if res is not None:
                tot = tot + t_refs[0][...].astype(F32)
            outs = epi[0](tot, *[r[...].astype(F32) for r in t_refs[len(tiles) - len(epi[1]):]]) if epi else (tot,)
            for o_ref, val in zip(o_refs, outs):
                o_ref[...] = val.astype(o_ref.dtype)

        if nk == 1:
            write(part)
            return
        acc = rest[-1]
        kk = pl.program_id(2)

        @pl.when(kk == 0)
        def _():
            acc[...] = part

        @pl.when(kk > 0)
        def _():
            acc[...] += part

        @pl.when(kk == nk - 1)
        def _():
            write(acc[...])

    return _call(body, side, name=name, grid=(m // bm, n // bn, nk), in_specs=in_specs,
                 out_specs=o_spec if n_out == 1 else [o_spec] * n_out, out_shape=out_sds if n_out == 1 else [out_sds] * n_out,
                 scratch_shapes=[pltpu.VMEM((bm, bn), F32)] if nk > 1 else [],
                 sem=("parallel", "parallel", "arbitrary"), args=args)


def rmsnorm_fwd(x, g, *, name):
    t, d = x.shape
    bt = _blk(512, t)

    def body(x_ref, g_ref, o_ref):
        xv = x_ref[...]
        r = lax.rsqrt(jnp.mean(xv * xv, axis=-1, keepdims=True) + EPS)
        o_ref[...] = (xv * r * g_ref[...]).astype(o_ref.dtype)

    return pl.pallas_call(
        body, name=name, grid=(t // bt,),
        in_specs=[pl.BlockSpec((bt, d), lambda i: (i, 0)), pl.BlockSpec((1, d), lambda i: (0, 0))],
        out_specs=pl.BlockSpec((bt, d), lambda i: (i, 0)), out_shape=jax.ShapeDtypeStruct((t, d), BF16),
        compiler_params=_params(("parallel",)),
    )(x, g)


def rmsnorm_bwd(x, g, dh, dres, *, name):
    t, d = x.shape
    bt = _blk(256, t)
    want_dx = dres is not None

    def body(x_ref, g_ref, dh_ref, *rest):
        if want_dx:
            dres_ref, dx_ref, dxb_ref, dg_ref = rest
        else:
            (dg_ref,) = rest
        xv = x_ref[...]
        dhv = dh_ref[...].astype(F32)
        r = lax.rsqrt(jnp.mean(xv * xv, axis=-1, keepdims=True) + EPS)
        xh = xv * r
        part = jnp.sum(dhv * xh, axis=0, keepdims=True)

        @pl.when(pl.program_id(0) == 0)
        def _():
            dg_ref[...] = part

        @pl.when(pl.program_id(0) > 0)
        def _():
            dg_ref[...] += part

        if want_dx:
            dy = dhv * g_ref[...]
            dxn = r * (dy - xh * jnp.mean(dy * xh, axis=-1, keepdims=True))
            dx = dres_ref[...] + dxn
            dx_ref[...] = dx
            dxb_ref[...] = dx.astype(BF16)

    row = pl.BlockSpec((bt, d), lambda i: (i, 0))
    vec = pl.BlockSpec((1, d), lambda i: (0, 0))
    in_specs, args = [row, vec, row], [x, g, dh]
    out_specs, out_shape = [vec], [jax.ShapeDtypeStruct((1, d), F32)]
    if want_dx:
        in_specs.append(row)
        args.append(dres)
        out_specs = [row, row] + out_specs
        out_shape = [jax.ShapeDtypeStruct((t, d), F32), jax.ShapeDtypeStruct((t, d), BF16)] + out_shape
    return pl.pallas_call(
        body, name=name, grid=(t // bt,), in_specs=in_specs, out_specs=out_specs, out_shape=out_shape,
        compiler_params=_params(("arbitrary",)),
    )(*args)


def loss_head(x, g, tgt, *, name):
    t, d = x.shape
    bt = _blk(256, t)

    def body(x_ref, g_ref, t_ref, loss_ref, dx_ref, dxb_ref, dg_ref):
        xv = x_ref[...]
        gv = g_ref[...]
        r = lax.rsqrt(jnp.mean(xv * xv, axis=-1, keepdims=True) + EPS)
        xh = xv * r
        e = xh * gv - t_ref[...]
        lpart = jnp.zeros((1, 128), F32) + jnp.sum(e * e) * (0.5 / d)
        dyv = e * (1.0 / d)
        gpart = jnp.sum(dyv * xh, axis=0, keepdims=True)

        @pl.when(pl.program_id(0) == 0)
        def _():
            loss_ref[...] = lpart
            dg_ref[...] = gpart

        @pl.when(pl.program_id(0) > 0)
        def _():
            loss_ref[...] += lpart
            dg_ref[...] += gpart

        dy = dyv * gv
        dx = r * (dy - xh * jnp.mean(dy * xh, axis=-1, keepdims=True))
        dx_ref[...] = dx
        dxb_ref[...] = dx.astype(BF16)

    row = pl.BlockSpec((bt, d), lambda i: (i, 0))
    vec = pl.BlockSpec((1, d), lambda i: (0, 0))
    return pl.pallas_call(
        body, name=name, grid=(t // bt,), in_specs=[row, vec, row],
        out_specs=[pl.BlockSpec((1, 128), lambda i: (0, 0)), row, row, vec],
        out_shape=[jax.ShapeDtypeStruct((1, 128), F32), jax.ShapeDtypeStruct((t, d), F32),
                   jax.ShapeDtypeStruct((t, d), BF16), jax.ShapeDtypeStruct((1, d), F32)],
        compiler_params=_params(("arbitrary",)),
    )(x, g, tgt)


def _sigmoid(x):
    return 1.0 / (1.0 + jnp.exp(-x))


def _swiglu_epi(b, a):
    return b, a * _sigmoid(a) * b


def _swiglu_bwd_epi(ds, a, b):
    sg = _sigmoid(a)
    return ds * b * sg * (1.0 + a * (1.0 - sg)), ds * a * sg


def _xa_probs(qh, kh, scale):
    s = lax.dot_general(qh, kh, (((1,), (1,)), ((), ())), preferred_element_type=F32) * scale
    s = s - jnp.max(s, axis=-1, keepdims=True)
    p = jnp.exp(s)
    return p / jnp.sum(p, axis=-1, keepdims=True)


def xattn_fwd(q, kv, *, name):
    t, d = q.shape
    m = kv.shape[0]
    hd = d // XA_HEADS
    bt = _blk(512, t)
    scale = hd ** -0.5

    def body(q_ref, kv_ref, o_ref):
        for h in range(XA_HEADS):
            qh = q_ref[:, h * hd:(h + 1) * hd]
            kh = kv_ref[:, h * hd:(h + 1) * hd]
            vh = kv_ref[:, d + h * hd:d + (h + 1) * hd]
            p = _xa_probs(qh, kh, scale)
            o_ref[:, h * hd:(h + 1) * hd] = jnp.dot(p.astype(BF16), vh, preferred_element_type=F32).astype(BF16)

    return pl.pallas_call(
        body, name=name, grid=(t // bt,),
        in_specs=[pl.BlockSpec((bt, d), lambda i: (i, 0)), pl.BlockSpec((m, 2 * d), lambda i: (0, 0))],
        out_specs=pl.BlockSpec((bt, d), lambda i: (i, 0)), out_shape=jax.ShapeDtypeStruct((t, d), BF16),
        compiler_params=_params(("parallel",)),
    )(q, kv)


def xattn_bwd(q, kv, do, *, name):
    t, d = q.shape
    m = kv.shape[0]
    hd = d // XA_HEADS
    bt = _blk(512, t)
    scale = hd ** -0.5

    def body(q_ref, kv_ref, do_ref, dq_ref, dkv_ref):
        first = pl.program_id(0) == 0
        for h in range(XA_HEADS):
            qs, ks, vs = slice(h * hd, (h + 1) * hd), slice(h * hd, (h + 1) * hd), slice(d + h * hd, d + (h + 1) * hd)
            qh, kh, vh, doh = q_ref[:, qs], kv_ref[:, ks], kv_ref[:, vs], do_ref[:, qs]
            p = _xa_probs(qh, kh, scale)
            dp = lax.dot_general(doh, vh, (((1,), (1,)), ((), ())), preferred_element_type=F32)
            dsv = p * (dp - jnp.sum(p * dp, axis=-1, keepdims=True)) * scale
            dsb = dsv.astype(BF16)
            dq_ref[:, qs] = jnp.dot(dsb, kh, preferred_element_type=F32).astype(BF16)
            dk = lax.dot_general(dsb, qh, (((0,), (0,)), ((), ())), preferred_element_type=F32)
            dv = lax.dot_general(p.astype(BF16), doh, (((0,), (0,)), ((), ())), preferred_element_type=F32)

            @pl.when(first)
            def _():
                dkv_ref[:, ks] = dk
                dkv_ref[:, vs] = dv

            @pl.when(jnp.logical_not(first))
            def _():
                dkv_ref[:, ks] += dk
                dkv_ref[:, vs] += dv

    row = pl.BlockSpec((bt, d), lambda i: (i, 0))
    full = pl.BlockSpec((m, 2 * d), lambda i: (0, 0))
    return pl.pallas_call(
        body, name=name, grid=(t // bt,), in_specs=[row, full, row], out_specs=[row, full],
        out_shape=[jax.ShapeDtypeStruct((t, d), BF16), jax.ShapeDtypeStruct((m, 2 * d), F32)],
        compiler_params=_params(("arbitrary",)),
    )(q, kv, do)


def _pool_p(buf, uv, rows, w, bt):
    acc = uv
    for dd in range(1, w):
        acc = acc + buf[pl.ds(POOL_HALO - dd, bt), :]
    cnt = jnp.minimum(rows + 1, w).astype(F32)
    return acc / cnt - uv


def pool_fwd(z, w_pool, scale, *, name):
    t = z.shape[0]
    ng, gsz = w_pool.shape[0], w_pool.shape[1]
    mix = ng * gsz
    bt = _blk(512, t)

    def body(u_ref, uh_ref, w_ref, sc_ref, o_ref, buf):
        r = pl.program_id(0)
        rows = r * bt + lax.broadcasted_iota(jnp.int32, (bt, 1), 0)
        for g in range(ng):
            gs = slice(g * gsz, (g + 1) * gsz)
            uv = u_ref[:, gs]
            buf[0:POOL_HALO, :] = jnp.where(r > 0, uh_ref[:, gs], 0.0)
            buf[POOL_HALO:POOL_HALO + bt, :] = uv
            p = _pool_p(buf, uv, rows, POOL_WINDOWS[g], bt)
            y = jnp.dot(p.astype(BF16), w_ref[g], preferred_element_type=F32) * sc_ref[:, gs]
            o_ref[:, gs] = y.astype(BF16)

    hb = bt // POOL_HALO
    return pl.pallas_call(
        body, name=name, grid=(t // bt,),
        in_specs=[pl.BlockSpec((bt, mix), lambda i: (i, 0)),
                  pl.BlockSpec((POOL_HALO, mix), lambda i: (jnp.maximum(i * hb - 1, 0), 0)),
                  pl.BlockSpec((ng, gsz, gsz), lambda i: (0, 0, 0)), pl.BlockSpec((1, mix), lambda i: (0, 0))],
        out_specs=pl.BlockSpec((None, bt, mix), lambda i: (0, i, 0)),
        out_shape=jax.ShapeDtypeStruct((2, t, mix), BF16),
        scratch_shapes=[pltpu.VMEM((POOL_HALO + bt, gsz), F32)],
        compiler_params=_params(("parallel",)),
    )(z, z, w_pool, scale)


def pool_bwd(z, dcat, w_pool, scale, *, name):
    t = z.shape[0]
    ng, gsz = w_pool.shape[0], w_pool.shape[1]
    mix = ng * gsz
    bt = _blk(512, t)
    nb = t // bt
    nt_dims = (((1,), (1,)), ((), ()))
    tn_dims = (((0,), (0,)), ((), ()))

    def body(u_ref, uh_ref, dy_ref, dyh_ref, w_ref, sc_ref, du_ref, dw_ref, dsc_ref, buf, buf2):
        r = pl.program_id(0)
        first = r == 0
        rows = r * bt + lax.broadcasted_iota(jnp.int32, (bt, 1), 0)
        rows_h = (r + 1) * bt + lax.broadcasted_iota(jnp.int32, (POOL_HALO, 1), 0)
        for g in range(ng):
            w = POOL_WINDOWS[g]
            gs = slice(g * gsz, (g + 1) * gsz)
            uv = u_ref[:, gs]
            buf[0:POOL_HALO, :] = jnp.where(r > 0, uh_ref[:, gs], 0.0)
            buf[POOL_HALO:POOL_HALO + bt, :] = uv
            pb = _pool_p(buf, uv, rows, w, bt).astype(BF16)
            wg = w_ref[g]
            sc = sc_ref[:, gs]
            y0 = jnp.dot(pb, wg, preferred_element_type=F32)
            dyv = dy_ref[:, gs].astype(F32)
            dsc = jnp.sum(dyv * y0, axis=0, keepdims=True)
            dyw = (dyv * sc).astype(BF16)
            dw = lax.dot_general(pb, dyw, tn_dims, preferred_element_type=F32)

            @pl.when(first)
            def _():
                dw_ref[g] = dw
                dsc_ref[:, gs] = dsc

            @pl.when(jnp.logical_not(first))
            def _():
                dw_ref[g] += dw
                dsc_ref[:, gs] += dsc

            dp = lax.dot_general(dyw, wg, nt_dims, preferred_element_type=F32)
            dyh = (dyh_ref[:, gs].astype(F32) * sc).astype(BF16)
            dph = lax.dot_general(dyh, wg, nt_dims, preferred_element_type=F32)
            dph = jnp.where(r < nb - 1, dph, 0.0)
            buf2[0:bt, :] = dp / jnp.minimum(rows + 1, w).astype(F32)
            buf2[bt:bt + POOL_HALO, :] = dph / jnp.minimum(rows_h + 1, w).astype(F32)
            acc = buf2[pl.ds(0, bt), :]
            for dd in range(1, w):
                acc = acc + buf2[pl.ds(dd, bt), :]
            du_ref[:, gs] = (acc - dp).astype(BF16)

    hb = bt // POOL_HALO
    nhb = t // POOL_HALO
    return pl.pallas_call(
        body, name=name, grid=(nb,),
        in_specs=[pl.BlockSpec((bt, mix), lambda i: (i, 0)),
                  pl.BlockSpec((POOL_HALO, mix), lambda i: (jnp.maximum(i * hb - 1, 0), 0)),
                  pl.BlockSpec((None, bt, mix), lambda i: (0, i, 0)),
                  pl.BlockSpec((None, POOL_HALO, mix), lambda i: (0, jnp.minimum((i + 1) * hb, nhb - 1), 0)),
                  pl.BlockSpec((ng, gsz, gsz), lambda i: (0, 0, 0)), pl.BlockSpec((1, mix), lambda i: (0, 0))],
        out_specs=[pl.BlockSpec((None, bt, mix), lambda i: (4, i, 0)),
                   pl.BlockSpec((ng, gsz, gsz), lambda i: (0, 0, 0)), pl.BlockSpec((1, mix), lambda i: (0, 0))],
        out_shape=[jax.ShapeDtypeStruct((5, t, mix), BF16), jax.ShapeDtypeStruct((ng, gsz, gsz), F32),
                   jax.ShapeDtypeStruct((1, mix), F32)],
        scratch_shapes=[pltpu.VMEM((POOL_HALO + bt, gsz), F32), pltpu.VMEM((bt + POOL_HALO, gsz), F32)],
        compiler_params=_params(("arbitrary",)),
    )(z, z, dcat, dcat, w_pool, scale)


HG_HEADS_PER_STEP = 2
HG_LEVELS = ((64, 31), (32, 15), (16, 7))
HG_DIAG = (8, 3)
_NT = (((1,), (1,)), ((), ()))
_TN = (((0,), (0,)), ((), ()))
_HI = lax.Precision.HIGHEST


def _hg_masks():
    c = HG_CHUNK
    t = lax.broadcasted_iota(jnp.int32, (c, c), 0)
    s = lax.broadcasted_iota(jnp.int32, (c, c), 1)
    masks = []
    for blk, row in HG_LEVELS:
        sh = blk.bit_length() - 1
        same = (t >> sh) == (s >> sh)
        masks.append(same & ((t & (blk - 1)) > row) & ((s & (blk - 1)) <= row))
    sh = HG_DIAG[0].bit_length() - 1
    masks.append(((t >> sh) == (s >> sh)) & (s <= t))
    return t, s, masks


def _row_of_block(x, blk, row):
    c, n = x.shape
    x3 = x.reshape(c // blk, blk, n)
    return jnp.broadcast_to(x3[:, row:row + 1, :], x3.shape).reshape(c, n)


def _hg_parts(qv, flv, lb, masks, tri):
    sgf = _sigmoid(flv)
    f = lb + (1.0 - lb) * sgf
    logf = jnp.log(f)
    kk = 1.0 - f
    sgq = _sigmoid(qv)
    qf = qv * sgq * (HG_HEAD ** -0.5)
    bc = jnp.dot(tri, logf, preferred_element_type=F32, precision=_HI)
    levels = []
    a = None
    for li, (blk, row) in enumerate(HG_LEVELS + (HG_DIAG,)):
        e = bc - _row_of_block(bc, blk, row)
        if li < len(HG_LEVELS):
            eq, ek = jnp.exp(jnp.minimum(e, 0.0)), jnp.exp(jnp.minimum(-e, 0.0))
        else:
            eq, ek = jnp.exp(jnp.clip(e, -EXP_CLAMP, EXP_CLAMP)), jnp.exp(jnp.clip(-e, -EXP_CLAMP, EXP_CLAMP))
        qt, kt = qf * eq, kk * ek
        part = jnp.where(masks[li], lax.dot_general(qt.astype(BF16), kt.astype(BF16), _NT, preferred_element_type=F32), 0.0)
        a = part if a is None else a + part
        levels.append((eq, ek, qt, kt))
    return dict(sgf=sgf, f=f, kk=kk, sgq=sgq, qf=qf, bc=bc, levels=levels, a=a)


def hgrn_fwd(z, cat, lb, gain, mix_a, *, side=None, name):
    t = z.shape[0]
    mix_b = lb.shape[1]
    nh = mix_b // HG_HEAD
    bt = _blk(256, t)
    ncb = bt // HG_CHUNK
    dh = HG_HEAD

    def body(q_ref, fl_ref, i_ref, g_ref, lb_ref, gain_ref, cat_in, o_ref, st_ref, st):
        del cat_in

        @pl.when(pl.program_id(1) == 0)
        def _():
            st[...] = jnp.zeros_like(st)

        t_i, s_i, masks = _hg_masks()
        tri = (s_i <= t_i).astype(F32)
        lbv, gn = lb_ref[...], gain_ref[...]
        for c in range(ncb):
            rs = slice(c * HG_CHUNK, (c + 1) * HG_CHUNK)
            pr = _hg_parts(q_ref[rs, :], fl_ref[rs, :], lbv, masks, tri)
            vb = i_ref[rs, :].astype(BF16)
            stv = st[...]
            st_ref[c] = stv
            bc = pr["bc"]
            qt = pr["qf"] * jnp.exp(bc)
            o = (jnp.dot(pr["a"].astype(BF16), vb, preferred_element_type=F32)
                 + lax.dot_general(qt.astype(BF16), stv.astype(BF16), _NT, preferred_element_type=F32))
            bl = bc[HG_CHUNK - 1:HG_CHUNK, :]
            khat = pr["kk"] * jnp.exp(bl - bc)
            st[...] = stv * jnp.exp(bl) + lax.dot_general(vb, khat.astype(BF16), _TN, preferred_element_type=F32)
            r = lax.rsqrt(jnp.mean(o * o, axis=-1, keepdims=True) + EPS)
            gv = g_ref[rs, :]
            o_ref[rs, :] = (o * r * gn * (gv * _sigmoid(gv))).astype(BF16)

    def col(which):
        base = (mix_a + which * mix_b) // dh
        return pl.BlockSpec((bt, dh), lambda h, i: (i, base + h))

    return _call(
        body, side, name=name, grid=(nh, t // bt),
        in_specs=[col(0), col(1), col(2), col(3), pl.BlockSpec((1, dh), lambda h, i: (0, h)),
                  pl.BlockSpec((1, dh), lambda h, i: (0, 0)), _ANY],
        out_specs=[pl.BlockSpec((None, bt, dh), lambda h, i: (1, i, h)),
                   pl.BlockSpec((None, ncb, dh, dh), lambda h, i: (h, i, 0, 0))],
        out_shape=[jax.ShapeDtypeStruct(cat.shape, BF16), jax.ShapeDtypeStruct((nh, t // HG_CHUNK, dh, dh), F32)],
        scratch_shapes=[pltpu.VMEM((dh, dh), F32)], aliases={6: 0}, sem=("parallel", "arbitrary"),
        args=(z, z, z, z, lb, gain, cat))


def hgrn_bwd(z, dcat, dz5, states, lb, gain, mix_a, *, side=None, name):
    t = z.shape[0]
    mix_b = lb.shape[1]
    nh = mix_b // HG_HEAD
    bt = _blk(256, t)
    nb = t // bt
    ncb = bt // HG_CHUNK
    dh = HG_HEAD
    hp = HG_HEADS_PER_STEP if nh % HG_HEADS_PER_STEP == 0 else 1

    def body(q_ref, fl_ref, i_ref, g_ref, dy_ref, st_ref, lb_ref, gain_ref, dz_in, dz_ref, dlb_ref, dgn_ref, dst):
        del dz_in
        first = pl.program_id(1) == 0

        @pl.when(first)
        def _():
            dst[...] = jnp.zeros_like(dst)

        t_i, s_i, masks = _hg_masks()
        tri = (s_i <= t_i).astype(F32)
        triu = (s_i >= t_i).astype(F32)
        last_row = lax.broadcasted_iota(jnp.int32, (HG_CHUNK, 1), 0) == HG_CHUNK - 1
        gn = gain_ref[...]
        dlb_acc = [jnp.zeros((1, dh), F32) for _ in range(hp)]
        dgn_acc = [jnp.zeros((1, dh), F32) for _ in range(hp)]
        for c, hh in [(c, hh) for c in reversed(range(ncb)) for hh in range(hp)]:
            rs, cs = slice(c * HG_CHUNK, (c + 1) * HG_CHUNK), slice(hh * dh, (hh + 1) * dh)
            lbv = lb_ref[:, cs]
            qv, flv, gv = q_ref[rs, cs], fl_ref[rs, cs], g_ref[rs, cs]
            pr = _hg_parts(qv, flv, lbv, masks, tri)
            vb = i_ref[rs, cs].astype(BF16)
            stv = st_ref[hh, c]
            stb = stv.astype(BF16)
            dsv = dst[hh]
            dsb = dsv.astype(BF16)
            bc, kk, qf, ab = pr["bc"], pr["kk"], pr["qf"], pr["a"].astype(BF16)
            ebc = jnp.exp(bc)
            qt = qf * ebc
            qtb = qt.astype(BF16)
            o = jnp.dot(ab, vb, preferred_element_type=F32) + lax.dot_general(qtb, stb, _NT, preferred_element_type=F32)
            r = lax.rsqrt(jnp.mean(o * o, axis=-1, keepdims=True) + EPS)
            oh = o * r
            sgg = _sigmoid(gv)
            dyv = dy_ref[rs, cs].astype(F32)
            don = dyv * (gv * sgg)
            dgate = dyv * (oh * gn) * (sgg * (1.0 + gv * (1.0 - sgg)))
            dgn_acc[hh] = dgn_acc[hh] + jnp.sum(don * oh, axis=0, keepdims=True)
            doh = don * gn
            do = r * (doh - oh * jnp.mean(doh * oh, axis=-1, keepdims=True))
            dob = do.astype(BF16)
            bl = bc[HG_CHUNK - 1:HG_CHUNK, :]
            ebl = jnp.exp(bl)
            ekh = jnp.exp(bl - bc)
            khat = kk * ekh
            dv = (lax.dot_general(ab, dob, _TN, preferred_element_type=F32)
                  + lax.dot_general(khat.astype(BF16), dsb, _NT, preferred_element_type=F32))
            da = lax.dot_general(dob, vb, _NT, preferred_element_type=F32)
            dqt = jnp.dot(dob, stb, preferred_element_type=F32)
            dkh = jnp.dot(vb, dsb, preferred_element_type=F32)
            dst[hh] = dsv * ebl + lax.dot_general(dob, qtb, _TN, preferred_element_type=F32)
            dbl = jnp.sum(dsv * stv, axis=0, keepdims=True) * ebl + jnp.sum(dkh * khat, axis=0, keepdims=True)
            dqf = dqt * ebc
            dkk = dkh * ekh
            dbc = dqt * qt - dkh * khat
            for li, (eq, ek, qtl, ktl) in enumerate(pr["levels"]):
                gm = jnp.where(masks[li], da, 0.0).astype(BF16)
                qtr, ktr = qtl.astype(BF16), ktl.astype(BF16)
                dql = jnp.dot(gm, ktr, preferred_element_type=F32)
                dkl = lax.dot_general(gm, qtr, _TN, preferred_element_type=F32)
                dqf = dqf + dql * eq
                dkk = dkk + dkl * ek
                dbc = dbc + qtr.astype(F32) * dql - ktr.astype(F32) * dkl
            dbc = dbc + jnp.where(last_row, dbl, 0.0)
            dlogf = jnp.dot(triu, dbc, preferred_element_type=F32, precision=_HI)
            df = dlogf / pr["f"] - dkk
            sgf = pr["sgf"]
            dfl = df * (1.0 - lbv) * sgf * (1.0 - sgf)
            dlb_acc[hh] = dlb_acc[hh] + jnp.sum(df * (1.0 - sgf), axis=0, keepdims=True)
            sgq = pr["sgq"]
            dq = dqf * (HG_HEAD ** -0.5) * (sgq * (1.0 + qv * (1.0 - sgq)))
            dz_ref[0, rs, cs] = dq.astype(BF16)
            dz_ref[1, rs, cs] = dfl.astype(BF16)
            dz_ref[2, rs, cs] = dv.astype(BF16)
            dz_ref[3, rs, cs] = dgate.astype(BF16)

        @pl.when(first)
        def _():
            for hh in range(hp):
                dlb_ref[:, hh * dh:(hh + 1) * dh] = dlb_acc[hh]
                dgn_ref[hh] = dgn_acc[hh]

        @pl.when(jnp.logical_not(first))
        def _():
            for hh in range(hp):
                dlb_ref[:, hh * dh:(hh + 1) * dh] += dlb_acc[hh]
                dgn_ref[hh] += dgn_acc[hh]

    wd = hp * dh

    def col(which):
        base = (mix_a + which * mix_b) // wd
        return pl.BlockSpec((bt, wd), lambda h, i: (nb - 1 - i, base + h))

    return _call(
        body, side, name=name, grid=(nh // hp, nb),
        in_specs=[col(0), col(1), col(2), col(3),
                  pl.BlockSpec((None, bt, wd), lambda h, i: (1, nb - 1 - i, h)),
                  pl.BlockSpec((hp, ncb, dh, dh), lambda h, i: (h, nb - 1 - i, 0, 0)),
                  pl.BlockSpec((1, wd), lambda h, i: (0, h)), pl.BlockSpec((1, dh), lambda h, i: (0, 0)), _ANY],
        out_specs=[pl.BlockSpec((4, bt, wd), lambda h, i: (0, nb - 1 - i, h)),
                   pl.BlockSpec((1, wd), lambda h, i: (0, h)),
                   pl.BlockSpec((hp, 1, dh), lambda h, i: (h, 0, 0))],
        out_shape=[jax.ShapeDtypeStruct(dz5.shape, BF16), jax.ShapeDtypeStruct((1, mix_b), F32),
                   jax.ShapeDtypeStruct((nh, 1, dh), F32)],
        scratch_shapes=[pltpu.VMEM((hp, dh, dh), F32)], aliases={8: 0}, sem=("parallel", "arbitrary"),
        args=(z, z, z, z, dcat, states, lb, gain, dz5))


LOG2E = 1.4426950408889634


def _fox_scores(qb, kb, fk, scale, masked):
    s = lax.dot_general(qb, kb, _NT, preferred_element_type=F32) * (scale * LOG2E) - fk * LOG2E
    if masked:
        n = s.shape[0]
        row = lax.broadcasted_iota(jnp.int32, (n, n), 0)
        col = lax.broadcasted_iota(jnp.int32, (n, n), 1)
        s = jnp.where(col <= row, s, -jnp.inf)
    return s


def fox_fwd(qkv, fk, *, side=None, name):
    _, t, d = qkv.shape
    nh = d // FOX_HEAD
    b = _blk(FOX_BLK, t)
    nb = t // b
    dh = FOX_HEAD
    scale = dh ** -0.5

    def body(q_ref, k_ref, v_ref, f_ref, o_ref, lse_ref):
        qi = pl.program_id(1)
        qb = q_ref[...]

        def step(kj, carry, masked):
            m, l, acc = carry
            off = pl.multiple_of(kj * b, b)
            s = _fox_scores(qb, k_ref[pl.ds(off, b), :], f_ref[kj], scale, masked)
            m_new = jnp.maximum(m, jnp.max(s, axis=-1, keepdims=True))
            alpha = jnp.exp2(m - m_new)
            p = jnp.exp2(s - m_new)
            l = alpha * l + jnp.sum(p, axis=-1, keepdims=True)
            acc = alpha * acc + jnp.dot(p.astype(BF16), v_ref[pl.ds(off, b), :], preferred_element_type=F32)
            return m_new, l, acc

        init = (jnp.full((b, 1), -jnp.inf, F32), jnp.zeros((b, 1), F32), jnp.zeros((b, dh), F32))
        carry = lax.fori_loop(0, qi, lambda kj, c: step(kj, c, False), init)
        m, l, acc = step(qi, carry, True)
        o_ref[...] = (acc / l).astype(BF16)
        lse_ref[...] = m + jnp.log(l) * LOG2E

    return _call(
        body, side, name=name, grid=(nh, nb),
        in_specs=[pl.BlockSpec((None, b, dh), lambda h, i: (0, i, h)),
                  pl.BlockSpec((None, t, dh), lambda h, i: (1, 0, h)),
                  pl.BlockSpec((None, t, dh), lambda h, i: (2, 0, h)),
                  pl.BlockSpec((None, nb, 1, b), lambda h, i: (h, 0, 0, 0))],
        out_specs=[pl.BlockSpec((b, dh), lambda h, i: (i, h)), pl.BlockSpec((None, b, 1), lambda h, i: (h, i, 0))],
        out_shape=[jax.ShapeDtypeStruct((t, d), BF16), jax.ShapeDtypeStruct((nh, t, 1), F32)],
        sem=("parallel", "parallel"), args=(qkv, qkv, qkv, fk))


def fox_bwd_dq(qkv, fk, do, lse, *, side=None, name):
    _, t, d = qkv.shape
    nh = d // FOX_HEAD
    b = _blk(FOX_BLK, t)
    nb = t // b
    dh = FOX_HEAD
    scale = dh ** -0.5

    def body(q_ref, k_ref, v_ref, f_ref, do_ref, lse_ref, dq_ref, dl_ref, p_buf, dp_buf):
        qi = pl.program_id(1)
        qb, dob, lse_v = q_ref[...], do_ref[...], lse_ref[...]

        def first(kj, dl, masked):
            off = pl.multiple_of(kj * b, b)
            p = jnp.exp2(_fox_scores(qb, k_ref[pl.ds(off, b), :], f_ref[kj], scale, masked) - lse_v)
            dp = lax.dot_general(dob, v_ref[pl.ds(off, b), :], _NT, preferred_element_type=F32)
            p_buf[kj] = p
            dp_buf[kj] = dp
            return dl + jnp.sum(p * dp, axis=-1, keepdims=True)

        dl = lax.fori_loop(0, qi, lambda kj, c: first(kj, c, False), jnp.zeros((b, 1), F32))
        dl = first(qi, dl, True)
        dl_ref[...] = dl

        def second(kj, dq):
            off = pl.multiple_of(kj * b, b)
            dsv = p_buf[kj] * (dp_buf[kj] - dl)
            return dq + jnp.dot(dsv.astype(BF16), k_ref[pl.ds(off, b), :], preferred_element_type=F32)

        dq = lax.fori_loop(0, qi + 1, second, jnp.zeros((b, dh), F32))
        dq_ref[...] = (dq * scale).astype(BF16)

    col = pl.BlockSpec((None, b, 1), lambda h, i: (h, i, 0))
    return _call(
        body, side, name=name, grid=(nh, nb),
        in_specs=[pl.BlockSpec((None, b, dh), lambda h, i: (0, i, h)),
                  pl.BlockSpec((None, t, dh), lambda h, i: (1, 0, h)),
                  pl.BlockSpec((None, t, dh), lambda h, i: (2, 0, h)),
                  pl.BlockSpec((None, nb, 1, b), lambda h, i: (h, 0, 0, 0)),
                  pl.BlockSpec((b, dh), lambda h, i: (i, h)), col],
        out_specs=[pl.BlockSpec((None, b, dh), lambda h, i: (2, i, h)), col],
        out_shape=[jax.ShapeDtypeStruct((3, t, d), BF16), jax.ShapeDtypeStruct((nh, t, 1), F32)],
        scratch_shapes=[pltpu.VMEM((nb, b, b), F32), pltpu.VMEM((nb, b, b), F32)],
        sem=("parallel", "parallel"), args=(qkv, qkv, qkv, fk, do, lse))


def fox_bwd_dkv(qkv, fk, do, lse, delta, dqkv, *, side=None, name):
    _, t, d = qkv.shape
    nh = d // FOX_HEAD
    b = _blk(FOX_BLK, t)
    nb = t // b
    dh = FOX_HEAD
    scale = dh ** -0.5

    def body(q_ref, k_ref, v_ref, f_ref, do_ref, lse_ref, dl_ref, dz_in, dkv_ref, df_ref):
        del dz_in
        kj = pl.program_id(1)
        kb, vb, fkv = k_ref[...], v_ref[...], f_ref[...]

        def step(qi, carry, masked):
            dk, dv, df = carry
            off = pl.multiple_of(qi * b, b)
            qb, dob = q_ref[pl.ds(off, b), :], do_ref[pl.ds(off, b), :]
            p = jnp.exp2(_fox_scores(qb, kb, fkv, scale, masked) - lse_ref[pl.ds(off, b), :])
            dv = dv + lax.dot_general(p.astype(BF16), dob, _TN, preferred_element_type=F32)
            dp = lax.dot_general(dob, vb, _NT, preferred_element_type=F32)
            dsv = p * (dp - dl_ref[pl.ds(off, b), :])
            dk = dk + lax.dot_general(dsv.astype(BF16), qb, _TN, preferred_element_type=F32)
            return dk, dv, df - jnp.sum(dsv, axis=0, keepdims=True)

        init = (jnp.zeros((b, dh), F32), jnp.zeros((b, dh), F32), jnp.zeros((1, b), F32))
        carry = step(kj, init, True)
        dk, dv, df = lax.fori_loop(kj + 1, nb, lambda qi, c: step(qi, c, False), carry)
        dkv_ref[0] = (dk * scale).astype(BF16)
        dkv_ref[1] = dv.astype(BF16)
        df_ref[...] = df

    col = pl.BlockSpec((None, t, 1), lambda h, j: (h, 0, 0))
    return _call(
        body, side, name=name, grid=(nh, nb),
        in_specs=[pl.BlockSpec((None, t, dh), lambda h, j: (0, 0, h)),
                  pl.BlockSpec((None, b, dh), lambda h, j: (1, j, h)),
                  pl.BlockSpec((None, b, dh), lambda h, j: (2, j, h)),
                  pl.BlockSpec((None, None, 1, b), lambda h, j: (h, j, 0, 0)),
                  pl.BlockSpec((t, dh), lambda h, j: (0, h)), col, col, pl.BlockSpec(memory_space=pl.ANY)],
        out_specs=[pl.BlockSpec((2, b, dh), lambda h, j: (0, j, h)),
                   pl.BlockSpec((None, None, 1, b), lambda h, j: (h, j, 0, 0))],
        out_shape=[jax.ShapeDtypeStruct((3, t, d), BF16), jax.ShapeDtypeStruct((nh, nb, 1, b), F32)],
        aliases={7: 0}, sem=("parallel", "parallel"), args=(qkv, qkv, qkv, fk, do, lse, delta, dqkv))


FL_BLK = 256


def _log_sigmoid(x):
    return jnp.minimum(x, 0.0) - jnp.log(1.0 + jnp.exp(-jnp.abs(x)))


def fl_fwd(zf, bf, *, name):
    t, n = zf.shape
    bt = _blk(FL_BLK, t)

    def body(z_ref, b_ref, o_ref, carry):
        @pl.when(pl.program_id(0) == 0)
        def _():
            carry[...] = jnp.zeros_like(carry)

        ls = _log_sigmoid(z_ref[...] + b_ref[...])
        r = lax.broadcasted_iota(jnp.int32, (bt, bt), 0)
        c = lax.broadcasted_iota(jnp.int32, (bt, bt), 1)
        cs = jnp.dot((c <= r).astype(F32), ls, preferred_element_type=F32, precision=_HI) + carry[...]
        o_ref[...] = cs
        carry[...] = cs[bt - 1:bt, :]

    return pl.pallas_call(
        body, name=name, grid=(t // bt,),
        in_specs=[pl.BlockSpec((bt, n), lambda i: (i, 0)), pl.BlockSpec((1, n), lambda i: (0, 0))],
        out_specs=pl.BlockSpec((bt, n), lambda i: (i, 0)), out_shape=jax.ShapeDtypeStruct((t, n), F32),
        scratch_shapes=[pltpu.VMEM((1, n), F32)], compiler_params=_params(("arbitrary",)),
    )(zf, bf)


def fl_bwd(df, zf, bf, *, name):
    t, n = zf.shape
    bt = _blk(FL_BLK, t)
    nb = t // bt

    def body(df_ref, z_ref, b_ref, dz_ref, db_ref, carry):
        first = pl.program_id(0) == 0

        @pl.when(first)
        def _():
            carry[...] = jnp.zeros_like(carry)

        r = lax.broadcasted_iota(jnp.int32, (bt, bt), 0)
        c = lax.broadcasted_iota(jnp.int32, (bt, bt), 1)
        dls = jnp.dot((c >= r).astype(F32), df_ref[...], preferred_element_type=F32, precision=_HI) + carry[...]
        carry[...] = dls[0:1, :]
        dz = dls * (1.0 - _sigmoid(z_ref[...] + b_ref[...]))
        dz_ref[...] = dz.astype(BF16)
        part = jnp.sum(dz, axis=0, keepdims=True)

        @pl.when(first)
        def _():
            db_ref[...] = part

        @pl.when(jnp.logical_not(first))
        def _():
            db_ref[...] += part

    row = pl.BlockSpec((bt, n), lambda i: (nb - 1 - i, 0))
    vec = pl.BlockSpec((1, n), lambda i: (0, 0))
    return pl.pallas_call(
        body, name=name, grid=(nb,), in_specs=[row, row, vec], out_specs=[row, vec],
        out_shape=[jax.ShapeDtypeStruct((t, n), BF16), jax.ShapeDtypeStruct((1, n), F32)],
        scratch_shapes=[pltpu.VMEM((1, n), F32)], compiler_params=_params(("arbitrary",)),
    )(df, zf, bf)


def _adamw_math(w, g, m, v):
    m = ADAM_B1 * m + (1.0 - ADAM_B1) * g
    v = ADAM_B2 * v + (1.0 - ADAM_B2) * (g * g)
    m_hat = m / (1.0 - ADAM_B1 ** ADAM_STEP)
    v_hat = v / (1.0 - ADAM_B2 ** ADAM_STEP)
    delta = -ADAM_LR * (m_hat / (jnp.sqrt(v_hat) + ADAM_EPS) + ADAM_WD * w)
    return delta, m, v


def adamw(w, g, m, v, *, name):
    r, c = w.shape
    br = _blk(256, r)

    def body(w_ref, g_ref, m_ref, v_ref, go_ref, d_ref, mo_ref, vo_ref):
        gv = g_ref[...]
        go_ref[...] = gv
        d_ref[...], mo_ref[...], vo_ref[...] = _adamw_math(w_ref[...], gv, m_ref[...], v_ref[...])

    spec = pl.BlockSpec((br, c), lambda i: (i, 0))
    return pl.pallas_call(
        body, name=name, grid=(r // br,), in_specs=[spec] * 4, out_specs=[spec] * 4,
        out_shape=[jax.ShapeDtypeStruct((r, c), F32)] * 4, compiler_params=_params(("parallel",)),
    )(w, g, m, v)


def _f2(a):
    return a.reshape(a.shape[-2:])


def _local_step(x0, mem, tgt, sp, plan):
    t, d = x0.shape
    mix_a = sp["pool_scale"].shape[1]
    small = {}

    def row(a, l):
        return a[l:l + 1]

    def rows4(g):
        return g.reshape(N_CHIPS, -1, g.shape[-1])

    def xattn_f(l, xin):
        w = plan.weights(f"xa{l}")
        hx = rmsnorm_fwd(xin, row(sp["xa_norm"], l), name=f"xa_norm_f{l}")
        q = _f2(matmul(vm2(hx), w["wq"], "nn", out_dtype=BF16, side=plan.take_fwd(), name=f"xa_q_f{l}"))
        mn = rmsnorm_fwd(mem, row(sp["xa_mem_norm"], l), name=f"xa_memnorm_f{l}")
        kv = _f2(matmul(vm2(mn), w["wkv"], "nn", out_dtype=BF16, name=f"xa_kv_f{l}"))
        o = xattn_fwd(q, kv, name=f"xa_attn_f{l}")
        xout = _f2(matmul(vm2(o), w["wo"], "nn", out_dtype=F32, res=vm2(xin), side=plan.take_fwd(), name=f"xa_o_f{l}"))
        return xout, (xin, hx, q, mn, kv, o)

    def ffn_f(l, xin):
        w = plan.weights(f"ffn{l}")
        hf = rmsnorm_fwd(xin, row(sp["ffn_norm"], l), name=f"ffn_norm_f{l}")
        a = _f2(matmul(vm2(hf), w["gate"], "nn", out_dtype=BF16, side=plan.take_fwd(), name=f"ffn_gate_f{l}"))
        b, s = matmul(vm2(hf), w["up"], "nn", out_dtype=BF16, epi=(_swiglu_epi, [vm2(a)], 2), side=plan.take_fwd(), name=f"ffn_up_f{l}")
        b, s = _f2(b), _f2(s)
        xout = _f2(matmul(vm2(s), w["down"], "nn", out_dtype=F32, res=vm2(xin), side=plan.take_fwd(), name=f"ffn_down_f{l}"))
        return xout, (xin, hf, a, b, s)

    ev = plan.weights("ev")
    h0 = rmsnorm_fwd(x0, sp["ev_norm"], name="ev_norm_f")
    z = _f2(matmul(vm2(h0), ev["ev_in"], "nn", out_dtype=F32, side=plan.take_fwd(), name="ev_in_f"))
    cat = pool_fwd(z, ev["pool"], sp["pool_scale"], name="pool_f")
    cat, states = hgrn_fwd(z, cat, sp["lb"], sp["hg_gain"], mix_a, side=plan.take_fwd(), name="hgrn_f")
    x1 = _f2(matmul(VM(cat), ev["ev_out"], "nn", out_dtype=F32, res=vm2(x0), side=plan.take_fwd(), name="ev_out_f"))
    x2, xa0 = xattn_f(0, x1)
    x3, ff0 = ffn_f(0, x2)

    od = plan.weights("od")
    ho = rmsnorm_fwd(x3, sp["od_norm"], name="od_norm_f")
    qkv = matmul(vm2(ho), od["wqkv"], "nn", out_dtype=BF16, out_p=3, side=plan.take_fwd(), name="od_qkv_f")
    zf = _f2(matmul(vm2(ho), od["wf"], "nn", out_dtype=F32, name="od_fl_f"))
    fcum = fl_fwd(zf, sp["bf"], name="od_forget_f")
    nh = d // FOX_HEAD
    nfb = t // _blk(FOX_BLK, t)
    fk = fcum[:, :nh].T.reshape(nh, nfb, 1, t // nfb)
    of, lse = fox_fwd(qkv, fk, side=plan.take_fwd(), name="fox_f")
    x4 = _f2(matmul(vm2(of), od["od_out"], "nn", out_dtype=F32, res=vm2(x3), name="od_out_f"))
    x5, xa1 = xattn_f(1, x4)
    x6, ff1 = ffn_f(1, x5)
    loss, dx, dxb, small["final_norm"] = loss_head(x6, sp["final_norm"], tgt, name="loss_head")

    def ffn_b(l, saved, dx, dxb):
        xin, hf, a, b, s = saved
        w = plan.weights(f"ffn{l}")
        da, db = matmul(vm2(dxb), w["down"], "nt", out_dtype=BF16, epi=(_swiglu_bwd_epi, [vm2(a), vm2(b)], 2), side=plan.take_bwd(1), name=f"ffn_down_bx{l}")
        da, db = _f2(da), _f2(db)
        g_down = rows4(matmul(vm2(s), vm2(dxb), "tn", out_dtype=BF16, name=f"ffn_down_bw{l}"))
        g_gate = matmul(vm2(hf), vm2(da), "tn", out_dtype=BF16, out_p=N_CHIPS, name=f"ffn_gate_bw{l}")
        g_up = matmul(vm2(hf), vm2(db), "tn", out_dtype=BF16, out_p=N_CHIPS, name=f"ffn_up_bw{l}")
        plan.grads_done({f"down{l}": g_down, f"gate{l}": g_gate, f"up{l}": g_up})
        dh = matmul(vm2(da), w["gate"], "nt", out_dtype=F32, side=plan.take_bwd(), name=f"ffn_gate_bx{l}")
        dh = _f2(matmul(vm2(db), w["up"], "nt", out_dtype=BF16, res=VM(dh), side=plan.take_bwd(), name=f"ffn_up_bx{l}"))
        dx, dxb, dg = rmsnorm_bwd(xin, row(sp["ffn_norm"], l), dh, dx, name=f"ffn_norm_b{l}")
        return dx, dxb, dg

    def xattn_b(l, saved, dx, dxb):
        xin, hx, q, mn, kv, o = saved
        w = plan.weights(f"xa{l}")
        do = _f2(matmul(vm2(dxb), w["wo"], "nt", out_dtype=BF16, side=plan.take_bwd(), name=f"xa_o_bx{l}"))
        g_wo = rows4(matmul(vm2(o), vm2(dxb), "tn", out_dtype=BF16, name=f"xa_o_bw{l}"))
        dq, dkv = xattn_bwd(q, kv, do, name=f"xa_attn_b{l}")
        g_wq = rows4(matmul(vm2(hx), vm2(dq), "tn", out_dtype=BF16, name=f"xa_q_bw{l}"))
        dh = _f2(matmul(vm2(dq), w["wq"], "nt", out_dtype=BF16, name=f"xa_q_bx{l}"))
        dkvb = dkv.astype(BF16)
        g_wkv = matmul(vm2(mn), vm2(dkvb), "tn", out_dtype=BF16, out_p=N_CHIPS, name=f"xa_kv_bw{l}")
        plan.grads_done({f"wo{l}": g_wo, f"wq{l}": g_wq, f"wkv{l}": g_wkv})
        dmn = _f2(matmul(vm2(dkvb), w["wkv"], "nt", out_dtype=F32, side=plan.take_bwd(), name=f"xa_kv_bx{l}"))
        (dgm,) = rmsnorm_bwd(mem, row(sp["xa_mem_norm"], l), dmn, None, name=f"xa_memnorm_b{l}")
        dx, dxb, dg = rmsnorm_bwd(xin, row(sp["xa_norm"], l), dh, dx, name=f"xa_norm_b{l}")
        return dx, dxb, dg, dgm

    dg_ffn, dg_xa, dg_mem = [None, None], [None, None], [None, None]
    dx, dxb, dg_ffn[1] = ffn_b(1, ff1, dx, dxb)
    dx, dxb, dg_xa[1], dg_mem[1] = xattn_b(1, xa1, dx, dxb)

    do = _f2(matmul(vm2(dxb), od["od_out"], "nt", out_dtype=BF16, side=plan.take_bwd(), name="od_out_bx"))
    g_od_out = rows4(matmul(vm2(of), vm2(dxb), "tn", out_dtype=BF16, name="od_out_bw"))
    dz3, delta = fox_bwd_dq(qkv, fk, do, lse, side=plan.take_bwd(1), name="fox_bq")
    dz3, dfk = fox_bwd_dkv(qkv, fk, do, lse, delta, dz3, side=plan.take_bwd(1), name="fox_bkv")
    dfc = jnp.pad(dfk.reshape(nh, t).T, ((0, 0), (0, zf.shape[1] - nh)))
    dzf, dbf = fl_bwd(dfc, zf, sp["bf"], name="od_forget_b")
    dqkv = VM(dz3, "cs", pfn=lambda p: lax.rem(p + 2, 3))
    dwqkv = _f2(matmul(vm2(ho), dqkv, "tn", out_dtype=BF16, name="od_qkv_bw"))
    dwf = _f2(matmul(vm2(ho), vm2(dzf), "tn", out_dtype=BF16, name="od_fl_bw"))
    od_in_full = jnp.concatenate([dwqkv, dwf[:, :nh]], axis=1)
    plan.grads_done({"od_out": g_od_out, "od_in": od_in_full.reshape(d, N_CHIPS, -1).transpose(1, 0, 2)})
    dh = matmul(dqkv, od["wqkv"], "nt", out_dtype=F32, side=plan.take_bwd(), name="od_qkv_bx")
    dh = _f2(matmul(vm2(dzf), od["wf"], "nt", out_dtype=BF16, res=VM(dh), name="od_fl_bx"))
    dx, dxb, small["od_norm"] = rmsnorm_bwd(x3, sp["od_norm"], dh, dx, name="od_norm_b")
    small["bf"] = dbf

    dx, dxb, dg_ffn[0] = ffn_b(0, ff0, dx, dxb)
    dx, dxb, dg_xa[0], dg_mem[0] = xattn_b(0, xa0, dx, dxb)

    dcat = matmul(vm2(dxb), ev["ev_out"], "nt", out_dtype=BF16, out_p=2, side=plan.take_bwd(), name="ev_out_bx")
    g_ev_out = rows4(matmul(VM(cat), vm2(dxb), "tn", out_dtype=BF16, name="ev_out_bw"))
    dz5, g_pool, small["pool_scale"] = pool_bwd(z, dcat, ev["pool"], sp["pool_scale"], name="pool_b")
    dz5, small["lb"], dgn = hgrn_bwd(z, dcat, dz5, states, sp["lb"], sp["hg_gain"], mix_a, side=plan.take_bwd(1), name="hgrn_b")
    small["hg_gain"] = jnp.sum(dgn, axis=0)
    dzv = VM(dz5, "cs", pfn=lambda p: lax.rem(p + 4, 5))
    g_ev_in = _f2(matmul(vm2(h0), dzv, "tn", out_dtype=BF16, side=plan.take_bwd(1), name="ev_in_bw"))
    g_ev_in = g_ev_in.reshape(d, N_CHIPS, -1).transpose(1, 0, 2)
    ng, gsz = g_pool.shape[0], g_pool.shape[1]
    pool_parts = g_pool.reshape(ng, N_CHIPS, gsz // N_CHIPS, gsz).transpose(1, 0, 2, 3).reshape(N_CHIPS, gsz, gsz).astype(BF16)
    plan.grads_done({"ev_out": g_ev_out, "pool": pool_parts, "ev_in": g_ev_in}, now=True)
    dh = _f2(matmul(dzv, ev["ev_in"], "nt", out_dtype=BF16, side=plan.take_bwd(1), name="ev_in_bx"))
    dx, _, small["ev_norm"] = rmsnorm_bwd(x0, sp["ev_norm"], dh, dx, name="ev_norm_b")

    small["xa_norm"] = jnp.concatenate(dg_xa, axis=0)
    small["xa_mem_norm"] = jnp.concatenate(dg_mem, axis=0)
    small["ffn_norm"] = jnp.concatenate(dg_ffn, axis=0)
    return loss, dx, small


def gather_forward(fulls, shards, *, name):
    n = len(fulls)
    side = ForwardSide(fulls, shards)

    def body(*refs):
        ins, outs = refs[:2 * n], refs[2 * n:3 * n]
        ssem, rsem = refs[3 * n:]
        side.start(ins, outs, ssem, rsem)
        side.finish(ins, outs, ssem, rsem)

    return pl.pallas_call(
        body, name=name, in_specs=[_ANY] * (2 * n), out_specs=[_ANY] * n, out_shape=side.out_shape,
        input_output_aliases=side.aliases,
        scratch_shapes=[pltpu.SemaphoreType.DMA((n, 4)), pltpu.SemaphoreType.DMA((n, 4))],
    )(*side.inputs)


def gather_shards(shards, *, name):
    n = len(shards)

    def body(*refs):
        ins, outs = refs[:n], refs[n:2 * n]
        ssem, rsem = refs[2 * n:]
        x, y, c, chips = _me()
        mine = _chip_id((x, y))
        sibling = (x, y, 1 - c)

        def rows(w, chip_id, which):
            h = shards[w].shape[0] // 2
            return outs[w].at[chip_id, pl.ds(which * h, h)]

        def to_chip(w, j):
            h = shards[w].shape[0] // 2
            return _rcopy(ins[w].at[pl.ds(c * h, h)], rows(w, mine, c), ssem.at[w, j], rsem.at[w, j], (*chips[j], c))

        def from_chip(w, j):
            r = rows(w, _chip_id(chips[j]), c)
            return _rcopy(r, r, ssem.at[w, j], rsem.at[w, j], (*chips[j], c))

        def to_sibling(w, j):
            r = rows(w, _chip_id(chips[j]), c)
            return _rcopy(r, r, ssem.at[w, 3 + j], rsem.at[w, 3 + j], sibling)

        def from_sibling(w, j):
            r = rows(w, _chip_id(chips[j]), 1 - c)
            return _rcopy(r, r, ssem.at[w, 3 + j], rsem.at[w, 3 + j], sibling)

        def own(w):
            return _rcopy(ins[w], outs[w].at[mine], ssem.at[w, 6], rsem.at[w, 6], sibling)

        for w in range(n):
            own(w).start()
            for j in range(3):
                to_chip(w, j).start()
        for w in range(n):
            for j in range(3):
                from_chip(w, j).wait_recv()
                to_sibling(w, j).start()
        for w in range(n):
            for j in range(3):
                from_sibling(w, j).wait_recv()
        for w in range(n):
            own(w).wait()
            for j in range(3):
                to_chip(w, j).wait_send()
                to_sibling(w, j).wait_send()

    return pl.pallas_call(
        body, name=name, in_specs=[_ANY] * n, out_specs=[_ANY] * n,
        out_shape=[jax.ShapeDtypeStruct((N_CHIPS,) + s.shape, s.dtype) for s in shards],
        scratch_shapes=[pltpu.SemaphoreType.DMA((n, 7)), pltpu.SemaphoreType.DMA((n, 7))],
    )(*shards)


def _ids_spec(grid, in_specs, out_specs):
    return pltpu.PrefetchScalarGridSpec(num_scalar_prefetch=1, grid=grid, in_specs=in_specs, out_specs=out_specs)


def rs_pair(parts, *, name):
    n = len(parts)

    def body(*refs):
        ins, recv = refs[:n], refs[n:2 * n]
        ssem, rsem = refs[2 * n:]
        x, y, c, _ = _me()
        sibling = (x, y, 1 - c)

        def swap(w):
            h = parts[w].shape[1] // 2
            return _rcopy(ins[w].at[:, pl.ds((1 - c) * h, h), :], recv[w], ssem.at[w], rsem.at[w], sibling)

        for w in range(n):
            swap(w).start()
        for w in range(n):
            swap(w).wait()

    return pl.pallas_call(
        body, name=name, in_specs=[_ANY] * n, out_specs=[_ANY] * n,
        out_shape=[jax.ShapeDtypeStruct((p.shape[0], p.shape[1] // 2, p.shape[2]), p.dtype) for p in parts],
        scratch_shapes=[pltpu.SemaphoreType.DMA((n,)), pltpu.SemaphoreType.DMA((n,))],
    )(*parts)


def add_pair(part, recv, ids, *, name):
    p, h, c = recv.shape
    br = _row_blk(h, 512)
    nb = h // br

    def body(ids_ref, a_ref, b_ref, o_ref):
        del ids_ref
        o_ref[...] = (a_ref[...].astype(F32) + b_ref[...].astype(F32)).astype(o_ref.dtype)

    half = pl.BlockSpec((None, br, c), lambda k, i, ids: (k, i, 0))
    return pl.pallas_call(
        body, name=name, out_shape=jax.ShapeDtypeStruct(recv.shape, recv.dtype),
        grid_spec=_ids_spec((p, nb), [pl.BlockSpec((None, br, c), lambda k, i, ids: (k, ids[1] * nb + i, 0)), half], half),
        compiler_params=_params(("parallel", "parallel")),
    )(ids, part, recv)


def rs_chip(sums, *, name):
    n = len(sums)

    def body(*refs):
        ins, outs = refs[:n], refs[n:2 * n]
        ssem, rsem = refs[2 * n:]
        x, y, c, chips = _me()

        def swap(w, j):
            return _rcopy(ins[w].at[_chip_id(chips[j])], outs[w].at[j], ssem.at[w, j], rsem.at[w, j], (*chips[j], c))

        for w in range(n):
            for j in range(3):
                swap(w, j).start()
        for w in range(n):
            for j in range(3):
                swap(w, j).wait()

    return pl.pallas_call(
        body, name=name, in_specs=[_ANY] * n, out_specs=[_ANY] * n,
        out_shape=[jax.ShapeDtypeStruct((3,) + s.shape[1:], s.dtype) for s in sums],
        scratch_shapes=[pltpu.SemaphoreType.DMA((n, 3)), pltpu.SemaphoreType.DMA((n, 3))],
    )(*sums)


def add_chips(sums, landed, ids, group, layer, group_shape, *, name):
    _, h, c = sums.shape
    br = _row_blk(h, 256)
    nb = h // br

    def body(ids_ref, a_ref, b_ref, *rest):
        o_ref = rest[-1]
        tot = a_ref[...].astype(F32)
        for k in range(3):
            tot = tot + b_ref[k].astype(F32)
        o_ref[...] = tot

    in_specs = [pl.BlockSpec((None, br, c), lambda i, ids: (ids[0], i, 0)), pl.BlockSpec((3, br, c), lambda i, ids: (0, i, 0))]
    args = [ids, sums, landed]
    if group is not None:
        in_specs.append(_ANY)
        args.append(group)
    return pl.pallas_call(
        body, name=name, out_shape=jax.ShapeDtypeStruct(group_shape, F32),
        input_output_aliases={3: 0} if group is not None else {},
        grid_spec=_ids_spec((nb,), in_specs, pl.BlockSpec((None, br, c), lambda i, ids: (layer, ids[1] * nb + i, 0))),
        compiler_params=_params(("parallel",)),
    )(*args)


def rs_share(groups, slots, *, name):
    ng = len(groups)
    n = len(slots)

    def body(*refs):
        outs = refs[ng:2 * ng]
        ssem, rsem = refs[2 * ng:]
        x, y, c, _ = _me()
        sibling = (x, y, 1 - c)

        def rows(w, which):
            g, l = slots[w]
            h = groups[g].shape[1] // 2
            return outs[g].at[l, pl.ds(which * h, h), :]

        def swap(w):
            return _rcopy(rows(w, c), rows(w, c), ssem.at[w], rsem.at[w], sibling)

        for w in range(n):
            swap(w).start()
        for w in range(n):
            swap(w).wait_send()
            _rcopy(rows(w, 1 - c), rows(w, 1 - c), ssem.at[w], rsem.at[w], sibling).wait_recv()

    return pl.pallas_call(
        body, name=name, in_specs=[_ANY] * ng, out_specs=[_ANY] * ng,
        out_shape=[jax.ShapeDtypeStruct(g.shape, g.dtype) for g in groups],
        input_output_aliases={g: g for g in range(ng)},
        scratch_shapes=[pltpu.SemaphoreType.DMA((n,)), pltpu.SemaphoreType.DMA((n,))],
    )(*groups)


def allreduce_small(v, *, name):
    r, c = v.shape
    ndev = 2 * N_CHIPS

    def body(v_ref, o_ref, buf, ssem, rsem):
        x, y, cc, _ = _me()
        me = 4 * x + 2 * y + cc
        flips = [(a, b, d) for a in (0, 1) for b in (0, 1) for d in (0, 1)][1:]
        buf[me] = v_ref[...]
        cps = []
        for k, (a, b, d) in enumerate(flips):
            peer = (jnp.bitwise_xor(x, a), jnp.bitwise_xor(y, b), jnp.bitwise_xor(cc, d))
            cp = _rcopy(v_ref, buf.at[me], ssem.at[k], rsem.at[k], peer)
            cp.start()
            cps.append(cp)
        for k, (a, b, d) in enumerate(flips):
            peer = (jnp.bitwise_xor(x, a), jnp.bitwise_xor(y, b), jnp.bitwise_xor(cc, d))
            src = 4 * peer[0] + 2 * peer[1] + peer[2]
            _rcopy(v_ref, buf.at[src], ssem.at[k], rsem.at[k], peer).wait_recv()
        for cp in cps:
            cp.wait_send()
        tot = buf[0]
        for k in range(1, ndev):
            tot = tot + buf[k]
        o_ref[...] = tot

    vm = pl.BlockSpec(memory_space=pltpu.VMEM)
    return pl.pallas_call(
        body, name=name, in_specs=[vm], out_specs=vm, out_shape=jax.ShapeDtypeStruct((r, c), F32),
        scratch_shapes=[pltpu.VMEM((ndev, r, c), F32), pltpu.SemaphoreType.DMA((ndev - 1,)), pltpu.SemaphoreType.DMA((ndev - 1,))],
    )(v)


WEIGHTS = ["lb_table", "ev_norm", "ev_w_in", "ev_w_pool", "ev_pool_scale", "ev_hg_norm", "ev_w_out", "od_norm", "od_w_in",
           "od_b_f", "od_w_out", "xa_norm", "xa_mem_norm", "xa_wq", "xa_wkv", "xa_wo", "ffn_norm", "ffn_w_gate", "ffn_w_up",
           "ffn_w_down", "final_norm"]
BIG = ["ev_w_in", "ev_w_pool", "ev_w_out", "od_w_in", "od_w_out", "xa_wq", "xa_wkv", "xa_wo", "ffn_w_gate", "ffn_w_up", "ffn_w_down"]
SMALL_ROWS = 16


def _rows(parts, width):
    rows = [jnp.pad(p.reshape(-1, p.shape[-1]).astype(F32), ((0, 0), (0, width - p.shape[-1]))) for p in parts]
    out = jnp.concatenate(rows, axis=0)
    return jnp.pad(out, ((0, SMALL_ROWS - out.shape[0]), (0, 0)))


def _unrows(packed, like):
    out, r = [], 0
    for p in like:
        n = p.size // p.shape[-1]
        out.append(packed[r:r + n, :p.shape[-1]].reshape(p.shape))
        r += n
    return out


def _m3(a):
    return a.reshape(a.shape[0], -1, a.shape[-1])


SLOT = {"ev_in": ("ev_w_in", 0), "pool": ("ev_w_pool", 0), "ev_out": ("ev_w_out", 0), "od_in": ("od_w_in", 0),
        "od_out": ("od_w_out", 0)}
for _l in range(2):
    SLOT.update({f"wq{_l}": ("xa_wq", _l), f"wkv{_l}": ("xa_wkv", _l), f"wo{_l}": ("xa_wo", _l),
                 f"gate{_l}": ("ffn_w_gate", _l), f"up{_l}": ("ffn_w_up", _l), f"down{_l}": ("ffn_w_down", _l)})
GATHER_FIRST = ["ev_in", "ev_out", "pool", "od_norm"]
GATHER_CARRIED = [["wq0", "wo0"], ["wkv0", "gate0"], ["od_out"], ["wq1"], ["wo1"], ["up0"], ["down0"], ["od_in"], ["wkv1"],
                  ["gate1", "up1", "down1"]]


class _Lazy:
    def __init__(self, plan, group):
        self.plan, self.layer = plan, group[-1] if group[-1] in "01" else ""

    def __getitem__(self, key):
        return self.plan.w(key + self.layer if key in ("wq", "wo", "wkv", "gate", "up", "down") else key)


class _Plan:
    def __init__(self, shards, ids, group_shapes, d, nh):
        self.shards, self.ids, self.group_shapes, self.d, self.nh = shards, ids, group_shapes, d, nh
        self.full, self.cache = {}, {}
        self.queue, self.sides, self.fsides, self.forwarded = [list(u) for u in GATHER_CARRIED], [], [], set()
        self.parts, self.psides, self.sums, self.rqueue, self.rsides = [], [], {}, [], []
        got = gather_shards([shards[n] for n in GATHER_FIRST], name="gather_first")
        for n, f in zip(GATHER_FIRST, got):
            self.full[n] = f

    def take_fwd(self):
        parts = []
        ready = [(ns, s) for ns, s in self.sides if s.outs is not None and ns[0] not in self.forwarded]
        for ns, s in ready:
            fs = ForwardSide(s.outs, [self.shards[n] for n in ns])
            self.fsides.append((ns, fs))
            self.forwarded.update(ns)
            parts.append(fs)
        if self.queue:
            names = self.queue.pop(0)
            side = GatherSide([self.shards[n] for n in names])
            self.sides.append((names, side))
            parts.append(side)
        return Sides(parts) if parts else None

    def _need(self, names):
        missing = [n for n in names if n not in self.full]
        if not missing:
            return
        done = {n: a for ns, s in self.fsides if s.outs is not None for n, a in zip(ns, s.outs)}
        landed = {n: a for ns, s in self.sides if s.outs is not None for n, a in zip(ns, s.outs)}
        pre = {n: done[n] for n in missing if n in done}
        half = [n for n in missing if n not in done and n in landed]
        late = [n for n in missing if n not in done and n not in landed]
        if half:
            self.forwarded.update(half)
            pre.update(zip(half, gather_forward([landed[n] for n in half], [self.shards[n] for n in half],
                                                name=f"gather_forward_{half[0]}")))
        if late:
            self.queue = [u for u in ([n for n in u if n not in late] for u in self.queue) if u]
            pre.update(zip(late, gather_shards([self.shards[n] for n in late], name=f"gather_late_{late[0]}")))
        for n in missing:
            self.full[n] = pre[n]

    def w(self, name):
        if name in self.cache:
            return self.cache[name]
        if name in ("wqkv", "wf"):
            self._need(["od_in"])
            od_full = self.full["od_in"].transpose(1, 0, 2).reshape(self.d, -1)
            self.cache["wqkv"] = vm2(od_full[:, :3 * self.d])
            self.cache["wf"] = vm2(jnp.pad(od_full[:, 3 * self.d:], ((0, 0), (0, 128 - self.nh))))
            return self.cache[name]
        self._need([name])
        f = self.full[name]
        if name == "pool":
            rows, gsz = f.shape[1:]
            ng = rows * N_CHIPS // gsz
            out = f.reshape(N_CHIPS, ng, gsz // N_CHIPS, gsz).transpose(1, 0, 2, 3).reshape(ng, gsz, gsz)
        elif name == "ev_in":
            out = vm2(f.transpose(1, 0, 2).reshape(self.d, -1))
        else:
            out = VM(f, "cs") if name.rstrip("01") in ("wkv", "gate", "up") else vm2(f.reshape(-1, f.shape[-1]))
        self.cache[name] = out
        return out

    def weights(self, group):
        return _Lazy(self, group)

    def grads_done(self, parts, now=False):
        names = list(parts)
        if now:
            got = rs_pair([parts[n] for n in names], name=f"reduce_pair_{names[0]}")
            for n, g in zip(names, got):
                self.sums[n] = add_pair(parts[n], g, self.ids, name=f"reduce_add2_{n}")
            self.rqueue.append(names)
        else:
            self.parts.append((names, [parts[n] for n in names]))

    def _add_swapped(self):
        for names, parts, side in self.psides:
            if side.outs is not None and names[0] not in self.sums:
                for n, p, g in zip(names, parts, side.outs):
                    self.sums[n] = add_pair(p, g, self.ids, name=f"reduce_add2_{n}")
                self.rqueue.append(names)

    def take_bwd(self, units=0):
        self._add_swapped()
        sides = []
        for names, parts in self.parts:
            ps = PairSide(parts)
            self.psides.append((names, parts, ps))
            sides.append(ps)
        self.parts = []
        names = [n for u in self.rqueue[:units] for n in u]
        self.rqueue = self.rqueue[units:]
        if names:
            rs = ReduceSide([self.sums[n] for n in names])
            self.rsides.append((names, rs))
            sides.append(rs)
        return Sides(sides) if sides else None

    def finish(self):
        for names, parts in self.parts:
            self.grads_done(dict(zip(names, parts)), now=True)
        self._add_swapped()
        landed = {}
        for ns, side in self.rsides:
            landed.update(zip(ns, side.outs))
        rest = [n for u in self.rqueue for n in u]
        if rest:
            landed.update(zip(rest, rs_chip([self.sums[n] for n in rest], name="reduce_chips_rest")))
        gbig = {n: None for n in BIG}
        for n, (big, l) in SLOT.items():
            gbig[big] = add_chips(self.sums[n], landed[n], self.ids, gbig[big], l, self.group_shapes[big], name=f"reduce_add4_{n}")
        full = rs_share([gbig[n] for n in BIG], [(BIG.index(big), l) for big, l in SLOT.values()], name="reduce_share")
        return dict(zip(BIG, full))


def kernel(x, mem, lb_table, ev_norm, ev_w_in, ev_w_pool, ev_pool_scale, ev_hg_norm, ev_w_out, od_norm, od_w_in, od_b_f, od_w_out, xa_norm, xa_mem_norm, xa_wq, xa_wkv, xa_wo, ffn_norm, ffn_w_gate, ffn_w_up, ffn_w_down, final_norm, loss_target, m_lb_table, m_ev_norm, m_ev_w_in, m_ev_w_pool, m_ev_pool_scale, m_ev_hg_norm, m_ev_w_out, m_od_norm, m_od_w_in, m_od_b_f, m_od_w_out, m_xa_norm, m_xa_mem_norm, m_xa_wq, m_xa_wkv, m_xa_wo, m_ffn_norm, m_ffn_w_gate, m_ffn_w_up, m_ffn_w_down, m_final_norm, v_lb_table, v_ev_norm, v_ev_w_in, v_ev_w_pool, v_ev_pool_scale, v_ev_hg_norm, v_ev_w_out, v_od_norm, v_od_w_in, v_od_b_f, v_od_w_out, v_xa_norm, v_xa_mem_norm, v_xa_wq, v_xa_wkv, v_xa_wo, v_ffn_norm, v_ffn_w_gate, v_ffn_w_up, v_ffn_w_down, v_final_norm):
    a = dict(locals())
    w = {n: a[n] for n in WEIGHTS}
    mom = {n: a["m_" + n] for n in WEIGHTS}
    var = {n: a["v_" + n] for n in WEIGHTS}
    _, t, d = x.shape
    nh = d // FOX_HEAD
    lanes = 128
    cx, cy = lax.axis_index("x"), lax.axis_index("y")
    chip = 2 * cx + cy

    w3 = {n: _m3(w[n]) for n in BIG}
    flat = lambda v: v.reshape(-1, v.shape[-1])
    shards = {"od_norm": jnp.broadcast_to(od_norm, (16, od_norm.shape[1]))}
    for name, (big, l) in SLOT.items():
        shards[name] = w3[big][l].astype(BF16)
    ids = jnp.stack([chip, lax.axis_index("c")]).astype(jnp.int32)
    plan = _Plan(shards, ids, {n: w3[n].shape for n in BIG}, d, nh)
    od_norm_full = plan.full["od_norm"][:, 0, :].reshape(1, d)

    sm = jax.nn.softmax(lb_table, axis=0)
    sp = {
        "lb": sm[1:2], "ev_norm": ev_norm, "pool_scale": ev_pool_scale, "hg_gain": ev_hg_norm, "od_norm": od_norm_full,
        "bf": jnp.pad(od_b_f, ((0, 0), (0, lanes - nh))), "xa_norm": xa_norm, "xa_mem_norm": xa_mem_norm, "ffn_norm": ffn_norm,
        "final_norm": final_norm.reshape(1, d),
    }
    loss_l, gx, small = _local_step(x[0], mem[0], loss_target[0], sp, plan)
    loss = lax.psum(loss_l[0, 0], ("x", "y", "c"))
    gbig = plan.finish()

    raw_like = [small["lb"], small["ev_norm"], small["pool_scale"], small["hg_gain"], small["od_norm"], small["bf"],
                small["xa_norm"], small["xa_mem_norm"], small["ffn_norm"], small["final_norm"]]
    summed = _unrows(allreduce_small(_rows(raw_like, d), name="reduce_small"), raw_like)
    dlb, g_ev_norm, g_pool_scale, g_hg, g_od_norm_full, g_bf, g_xa, g_xam, g_ffn, g_final = summed
    dsm = jnp.zeros_like(sm).at[1:2].set(dlb)
    gsmall = {
        "lb_table": sm * (dsm - jnp.sum(sm * dsm, axis=0, keepdims=True)), "ev_norm": g_ev_norm, "ev_pool_scale": g_pool_scale,
        "ev_hg_norm": g_hg, "od_norm": lax.dynamic_slice_in_dim(g_od_norm_full, chip * od_norm.shape[1], od_norm.shape[1], axis=1),
        "od_b_f": g_bf[:, :nh], "xa_norm": g_xa, "xa_mem_norm": g_xam, "ffn_norm": g_ffn, "final_norm": g_final.reshape(d),
    }

    grad, delta, new_m, new_v = {}, {}, {}, {}
    for n in BIG:
        shp = w[n].shape
        res = adamw(flat(w3[n]), flat(gbig[n]), flat(_m3(mom[n])), flat(_m3(var[n])), name=f"adamw_{n}")
        grad[n], delta[n], new_m[n], new_v[n] = [r.reshape(shp) for r in res]
    snames = [n for n in WEIGHTS if n not in BIG]
    like = [w[n] for n in snames]
    res = adamw(_rows(like, d), _rows([gsmall[n] for n in snames], d), _rows([mom[n] for n in snames], d),
                _rows([var[n] for n in snames], d), name="adamw_small")
    for vals, dst in zip(res, (grad, delta, new_m, new_v)):
        dst.update(zip(snames, _unrows(vals, like)))
    return (loss, gx.reshape(x.shape), *[grad[n] for n in WEIGHTS], *[delta[n] for n in WEIGHTS],
            *[new_m[n] for n in WEIGHTS], *[new_v[n] for n in WEIGHTS])
```

```python
import functools
import math

import jax
import jax.numpy as jnp
from jax import lax
from jax.experimental import pallas as pl
from jax.experimental.pallas import tpu as pltpu

F32 = jnp.float32
BF16 = jnp.bfloat16
MESH = pl.DeviceIdType.MESH

V7X_VMEM_LIMIT_BYTES = 56 * 1024 * 1024
N_CHIPS = 4

EPS = 1e-6
POOL_WINDOWS = (2, 4, 8, 16)
POOL_HALO = 16
HG_HEAD = 128
HG_CHUNK = 64
FOX_HEAD = 128
FOX_BLK = 512
XA_HEADS = 4
ADAM_LR, ADAM_B1, ADAM_B2, ADAM_EPS, ADAM_WD, ADAM_STEP = 0.001, 0.9, 0.999, 1e-08, 0.01, 10
EXP_CLAMP = 80.0


def _params(sem=None):
    return pltpu.CompilerParams(dimension_semantics=sem, vmem_limit_bytes=V7X_VMEM_LIMIT_BYTES)


def _blk(pref, dim):
    b = min(pref, dim)
    assert dim % b == 0, (pref, dim)
    return b


class VM:
    def __init__(self, arr, kind="cs", lead=(), inner=(), pfn=None):
        self.arr, self.kind, self.lead, self.inner = arr, kind, tuple(lead), tuple(inner)
        self.pfn = pfn or (lambda p: p)
        p = arr.shape[len(self.lead)]
        r, c = arr.shape[-2:]
        assert arr.ndim == len(self.lead) + 1 + len(self.inner) + 2, (arr.shape, lead, inner)
        self.P = p
        self.shape = (r, c * p) if kind == "cs" else (r * p, c)
        self.dtype = arr.dtype

    def spec(self, br, bc, rfn, cfn):
        p = self.P
        r, c = self.arr.shape[-2:]
        assert c % bc == 0 and r % br == 0, (self.arr.shape, br, bc)
        if p == 1:
            def imap(*g):
                return (*self.lead, self.pfn(0), *self.inner, rfn(*g), cfn(*g))
        elif self.kind == "cs":
            per = c // bc

            def imap(*g):
                cb = cfn(*g)
                return (*self.lead, self.pfn(lax.div(cb, per)), *self.inner, rfn(*g), lax.rem(cb, per))
        else:
            per = r // br

            def imap(*g):
                rb = rfn(*g)
                return (*self.lead, self.pfn(lax.div(rb, per)), *self.inner, lax.rem(rb, per), cfn(*g))
        return pl.BlockSpec((None,) * (self.arr.ndim - 2) + (br, bc), imap)


def vm2(arr):
    return VM(arr.reshape((1,) + arr.shape))


def _out_struct(shape, kind, p, dtype):
    r, c = shape
    return jax.ShapeDtypeStruct((p, r, c // p) if kind == "cs" else (p, r // p, c), dtype)


_ANY = pl.BlockSpec(memory_space=pl.ANY)


def _me():
    x, y, c = lax.axis_index("x"), lax.axis_index("y"), lax.axis_index("c")
    chips = [(1 - x, y), (x, 1 - y), (1 - x, 1 - y)]
    return x, y, c, chips


def _chip_id(xy):
    return 2 * xy[0] + xy[1]


def _rcopy(src, dst, ssem, rsem, dev):
    return pltpu.make_async_remote_copy(src_ref=src, dst_ref=dst, send_sem=ssem, recv_sem=rsem, device_id=dev,
                                        device_id_type=MESH)


class GatherSide:
    def __init__(self, shards):
        self.inputs = list(shards)
        self.out_shape = [jax.ShapeDtypeStruct((N_CHIPS,) + s.shape, s.dtype) for s in shards]
        self.aliases = {}
        self.rows = len(shards)
        self.outs = None

    def _copy(self, ins, outs, ssem, rsem, w, j, receive):
        x, y, c, chips = _me()
        h = self.inputs[w].shape[0] // 2
        half = pl.ds(c * h, h)
        if receive:
            r = outs[w].at[_chip_id(chips[j]), half]
            return _rcopy(r, r, ssem.at[w, j], rsem.at[w, j], (*chips[j], c))
        return _rcopy(ins[w].at[half], outs[w].at[_chip_id((x, y)), half], ssem.at[w, j], rsem.at[w, j], (*chips[j], c))

    def start(self, ins, outs, ssem, rsem):
        for w in range(len(self.inputs)):
            for j in range(3):
                self._copy(ins, outs, ssem, rsem, w, j, False).start()

    def finish(self, ins, outs, ssem, rsem):
        for w in range(len(self.inputs)):
            for j in range(3):
                self._copy(ins, outs, ssem, rsem, w, j, True).wait_recv()
                self._copy(ins, outs, ssem, rsem, w, j, False).wait_send()


class ForwardSide:
    def __init__(self, fulls, shards):
        self.inputs = list(fulls) + list(shards)
        self.out_shape = [jax.ShapeDtypeStruct(f.shape, f.dtype) for f in fulls]
        self.aliases = {w: w for w in range(len(fulls))}
        self.rows = len(fulls)
        self.outs = None

    def _copy(self, outs, ssem, rsem, w, j, receive):
        x, y, c, chips = _me()
        h = self.inputs[w].shape[1] // 2
        r = outs[w].at[_chip_id(chips[j]), pl.ds(((1 - c) if receive else c) * h, h)]
        return _rcopy(r, r, ssem.at[w, j], rsem.at[w, j], (x, y, 1 - c))

    def _own(self, ins, outs, ssem, rsem, w):
        x, y, c, _ = _me()
        return _rcopy(ins[self.rows + w], outs[w].at[_chip_id((x, y))], ssem.at[w, 3], rsem.at[w, 3], (x, y, 1 - c))

    def start(self, ins, outs, ssem, rsem):
        for w in range(self.rows):
            self._own(ins, outs, ssem, rsem, w).start()
            for j in range(3):
                self._copy(outs, ssem, rsem, w, j, False).start()

    def finish(self, ins, outs, ssem, rsem):
        for w in range(self.rows):
            self._own(ins, outs, ssem, rsem, w).wait()
            for j in range(3):
                self._copy(outs, ssem, rsem, w, j, False).wait_send()
                self._copy(outs, ssem, rsem, w, j, True).wait_recv()


class PairSide:
    def __init__(self, parts):
        self.inputs = list(parts)
        self.out_shape = [jax.ShapeDtypeStruct((p.shape[0], p.shape[1] // 2, p.shape[2]), p.dtype) for p in parts]
        self.aliases = {}
        self.rows = len(parts)
        self.outs = None

    def _copy(self, ins, outs, ssem, rsem, w):
        x, y, c, _ = _me()
        h = self.inputs[w].shape[1] // 2
        return _rcopy(ins[w].at[:, pl.ds((1 - c) * h, h), :], outs[w], ssem.at[w, 0], rsem.at[w, 0], (x, y, 1 - c))

    def start(self, ins, outs, ssem, rsem):
        for w in range(self.rows):
            self._copy(ins, outs, ssem, rsem, w).start()

    def finish(self, ins, outs, ssem, rsem):
        for w in range(self.rows):
            self._copy(ins, outs, ssem, rsem, w).wait()


class ShareSide:
    def __init__(self, group):
        self.inputs = [group]
        self.out_shape = [jax.ShapeDtypeStruct(group.shape, group.dtype)]
        self.aliases = {0: 0}
        self.rows = group.shape[0]
        self.outs = None

    def _copy(self, outs, ssem, rsem, l, receive):
        x, y, c, _ = _me()
        h = self.inputs[0].shape[1] // 2
        r = outs[0].at[l, pl.ds(((1 - c) if receive else c) * h, h), :]
        return _rcopy(r, r, ssem.at[l, 0], rsem.at[l, 0], (x, y, 1 - c))

    def start(self, ins, outs, ssem, rsem):
        for l in range(self.rows):
            self._copy(outs, ssem, rsem, l, False).start()

    def finish(self, ins, outs, ssem, rsem):
        for l in range(self.rows):
            self._copy(outs, ssem, rsem, l, False).wait_send()
            self._copy(outs, ssem, rsem, l, True).wait_recv()


class _SemRows:
    def __init__(self, sem, off):
        self.sem, self.off = sem, off

    @property
    def at(self):
        return self

    def __getitem__(self, idx):
        return self.sem.at[self.off + idx[0], idx[1]]


class Sides:
    def __init__(self, sides):
        self.sides = list(sides)
        self.inputs = [a for s in self.sides for a in s.inputs]
        self.out_shape = [o for s in self.sides for o in s.out_shape]
        self.rows = sum(s.rows for s in self.sides)
        self.aliases, i0, o0 = {}, 0, 0
        for s in self.sides:
            self.aliases.update({i0 + i: o0 + o for i, o in s.aliases.items()})
            i0, o0 = i0 + len(s.inputs), o0 + len(s.out_shape)

    def _each(self, method, ins, outs, ssem, rsem):
        i0 = o0 = r0 = 0
        for s in self.sides:
            getattr(s, method)(ins[i0:i0 + len(s.inputs)], outs[o0:o0 + len(s.out_shape)], _SemRows(ssem, r0), _SemRows(rsem, r0))
            i0, o0, r0 = i0 + len(s.inputs), o0 + len(s.out_shape), r0 + s.rows

    def start(self, ins, outs, ssem, rsem):
        self._each("start", ins, outs, ssem, rsem)

    def finish(self, ins, outs, ssem, rsem):
        self._each("finish", ins, outs, ssem, rsem)

    @property
    def outs(self):
        return None

    @outs.setter
    def outs(self, vals):
        o0 = 0
        for s in self.sides:
            s.outs = list(vals[o0:o0 + len(s.out_shape)])
            o0 += len(s.out_shape)


class ReduceSide:
    def __init__(self, sums):
        self.inputs = list(sums)
        self.out_shape = [jax.ShapeDtypeStruct((3,) + s.shape[1:], s.dtype) for s in sums]
        self.aliases = {}
        self.rows = len(sums)
        self.outs = None

    def _copy(self, ins, outs, ssem, rsem, w, j):
        _, _, c, chips = _me()
        return _rcopy(ins[w].at[_chip_id(chips[j])], outs[w].at[j], ssem.at[w, j], rsem.at[w, j], (*chips[j], c))

    def start(self, ins, outs, ssem, rsem):
        for w in range(len(self.inputs)):
            for j in range(3):
                self._copy(ins, outs, ssem, rsem, w, j).start()

    def finish(self, ins, outs, ssem, rsem):
        for w in range(len(self.inputs)):
            for j in range(3):
                self._copy(ins, outs, ssem, rsem, w, j).wait()


def _call(body, side, *, name, grid, in_specs, out_specs, out_shape, scratch_shapes=(), sem, aliases=None, args):
    if side is None:
        return pl.pallas_call(body, name=name, grid=grid, in_specs=in_specs, out_specs=out_specs, out_shape=out_shape,
                              scratch_shapes=list(scratch_shapes), input_output_aliases=aliases or {},
                              compiler_params=_params(sem))(*args)
    single = not isinstance(out_shape, (list, tuple))
    oshape, ospecs = ([out_shape], [out_specs]) if single else (list(out_shape), list(out_specs))
    n_in, n_out, s_in, s_out = len(in_specs), len(oshape), len(side.inputs), len(side.out_shape)

    def wrapped(*refs):
        ins, sin = refs[:n_in], refs[n_in:n_in + s_in]
        outs = refs[n_in + s_in:n_in + s_in + n_out]
        souts = refs[n_in + s_in + n_out:n_in + s_in + n_out + s_out]
        rest = refs[n_in + s_in + n_out + s_out:]
        scratch, (ssem, rsem) = rest[:-2], rest[-2:]
        first = functools.reduce(jnp.logical_and, [pl.program_id(a) == 0 for a in range(len(grid))])
        last = functools.reduce(jnp.logical_and, [pl.program_id(a) == grid[a] - 1 for a in range(len(grid))])

        @pl.when(first)
        def _():
            side.start(sin, souts, ssem, rsem)

        body(*ins, *outs, *scratch)

        @pl.when(last)
        def _():
            side.finish(sin, souts, ssem, rsem)

    sems = pltpu.SemaphoreType.DMA((side.rows, 4))
    res = pl.pallas_call(
        wrapped, name=name, grid=grid, in_specs=list(in_specs) + [_ANY] * s_in, out_specs=ospecs + [_ANY] * s_out,
        out_shape=oshape + side.out_shape, scratch_shapes=list(scratch_shapes) + [sems, sems],
        input_output_aliases={**(aliases or {}), **{n_in + i: n_out + o for i, o in side.aliases.items()}},
        compiler_params=_params(("arbitrary",) * len(grid)),
    )(*args, *side.inputs)
    side.outs = list(res[n_out:])
    return res[0] if single else list(res[:n_out])


def _best(g, cap):
    if g <= cap:
        return g
    cands = [d for d in range(128, cap + 1, 128) if g % d == 0]
    assert cands, (g, cap)
    return cands[-1]


def _row_blk(n, cap):
    cands = [d for d in range(16, min(n, cap) + 1, 16) if n % d == 0]
    assert cands, (n, cap)
    return cands[-1]


def _tiles(a, b, mode, out_kind, out_p, bm, bn, bk):
    def cpiece(v):
        return v.arr.shape[-1] if v.kind == "cs" else v.shape[1]

    def rpiece(v):
        return v.arr.shape[-2] if v.kind == "rs" else v.shape[0]

    if mode == "nn":
        m, n = a.shape[0], b.shape[1]
        gm, gn, gk = rpiece(a), cpiece(b), math.gcd(cpiece(a), rpiece(b))
    elif mode == "nt":
        m, n = a.shape[0], b.shape[0]
        gm, gn, gk = rpiece(a), rpiece(b), math.gcd(cpiece(a), cpiece(b))
    else:
        m, n = a.shape[1], b.shape[1]
        gm, gn, gk = cpiece(a), cpiece(b), math.gcd(rpiece(a), rpiece(b))
    if out_kind == "cs":
        gn = math.gcd(gn, n // out_p)
    else:
        gm = math.gcd(gm, m // out_p)
    caps = {"nn": (1024, 1536, 2048), "nt": (512, 2048, 2048), "tn": (1536, 1536, 2048)}[mode]
    return (bm or _best(gm, caps[0])), (bn or _best(gn, caps[1])), (bk or _best(gk, caps[2]))


def matmul(a, b, mode, *, out_dtype, bm=None, bn=None, bk=None, out_kind="cs", out_p=1, out_pfn=None, res=None, epi=None,
           side=None, name):
    bm, bn, bk = _tiles(a, b, mode, out_kind, out_p, bm, bn, bk)
    if mode == "nn":
        (m, k), (k2, n) = a.shape, b.shape
        a_spec = a.spec(bm, bk, lambda i, j, kk: i, lambda i, j, kk: kk)
        b_spec = b.spec(bk, bn, lambda i, j, kk: kk, lambda i, j, kk: j)
        dims = (((1,), (0,)), ((), ()))
    elif mode == "nt":
        (m, k), (n, k2) = a.shape, b.shape
        a_spec = a.spec(bm, bk, lambda i, j, kk: i, lambda i, j, kk: kk)
        b_spec = b.spec(bn, bk, lambda i, j, kk: j, lambda i, j, kk: kk)
        dims = (((1,), (1,)), ((), ()))
    else:
        (k, m), (k2, n) = a.shape, b.shape
        a_spec = a.spec(bk, bm, lambda i, j, kk: kk, lambda i, j, kk: i)
        b_spec = b.spec(bk, bn, lambda i, j, kk: kk, lambda i, j, kk: j)
        dims = (((0,), (0,)), ((), ()))
    assert k == k2, (a.shape, b.shape, mode)
    assert m % bm == 0 and n % bn == 0 and k % bk == 0, (m, n, k, bm, bn, bk)
    nk = k // bk
    out_sds = _out_struct((m, n), out_kind, out_p, out_dtype)
    out_vm = VM(out_sds, out_kind, pfn=out_pfn)
    o_spec = out_vm.spec(bm, bn, lambda i, j, kk: i, lambda i, j, kk: j)
    in_specs, args = [a_spec, b_spec], [a.arr, b.arr]
    tiles = ([res] if res is not None else []) + (list(epi[1]) if epi else [])
    for v in tiles:
        assert v.shape == (m, n)
        in_specs.append(v.spec(bm, bn, lambda i, j, kk: i, lambda i, j, kk: j))
        args.append(v.arr)
    n_out = epi[2] if epi else 1

    def body(a_ref, b_ref, *rest):
        t_refs, o_refs = rest[:len(tiles)], rest[len(tiles):len(tiles) + n_out]
        part = lax.dot_general(a_ref[...], b_ref[...], dims, preferred_element_type=F32)

        def write(tot):
            if res is not None:
                tot = tot + t_refs[0][...].astype(F32)
            outs = epi[0](tot, *[r[...].astype(F32) for r in t_refs[len(tiles) - len(epi[1]):]]) if epi else (tot,)
            for o_ref, val in zip(o_refs, outs):
                o_ref[...] = val.astype(o_ref.dtype)

        if nk == 1:
            write(part)
            return
        acc = rest[-1]
        kk = pl.program_id(2)

        @pl.when(kk == 0)
        def _():
            acc[...] = part

        @pl.when(kk > 0)
        def _():
            acc[...] += part

        @pl.when(kk == nk - 1)
        def _():
            write(acc[...])

    return _call(body, side, name=name, grid=(m // bm, n // bn, nk), in_specs=in_specs,
                 out_specs=o_spec if n_out == 1 else [o_spec] * n_out, out_shape=out_sds if n_out == 1 else [out_sds] * n_out,
                 scratch_shapes=[pltpu.VMEM((bm, bn), F32)] if nk > 1 else [],
                 sem=("parallel", "parallel", "arbitrary"), args=args)


def rmsnorm_fwd(x, g, *, name):
    t, d = x.shape
    bt = _blk(512, t)

    def body(x_ref, g_ref, o_ref):
        xv = x_ref[...]
        r = lax.rsqrt(jnp.mean(xv * xv, axis=-1, keepdims=True) + EPS)
        o_ref[...] = (xv * r * g_ref[...]).astype(o_ref.dtype)

    return pl.pallas_call(
        body, name=name, grid=(t // bt,),
        in_specs=[pl.BlockSpec((bt, d), lambda i: (i, 0)), pl.BlockSpec((1, d), lambda i: (0, 0))],
        out_specs=pl.BlockSpec((bt, d), lambda i: (i, 0)), out_shape=jax.ShapeDtypeStruct((t, d), BF16),
        compiler_params=_params(("parallel",)),
    )(x, g)


def rmsnorm_bwd(x, g, dh, dres, *, name):
    t, d = x.shape
    bt = _blk(256, t)
    want_dx = dres is not None

    def body(x_ref, g_ref, dh_ref, *rest):
        if want_dx:
            dres_ref, dx_ref, dxb_ref, dg_ref = rest
        else:
            (dg_ref,) = rest
        xv = x_ref[...]
        dhv = dh_ref[...].astype(F32)
        r = lax.rsqrt(jnp.mean(xv * xv, axis=-1, keepdims=True) + EPS)
        xh = xv * r
        part = jnp.sum(dhv * xh, axis=0, keepdims=True)

        @pl.when(pl.program_id(0) == 0)
        def _():
            dg_ref[...] = part

        @pl.when(pl.program_id(0) > 0)
        def _():
            dg_ref[...] += part

        if want_dx:
            dy = dhv * g_ref[...]
            dxn = r * (dy - xh * jnp.mean(dy * xh, axis=-1, keepdims=True))
            dx = dres_ref[...] + dxn
            dx_ref[...] = dx
            dxb_ref[...] = dx.astype(BF16)

    row = pl.BlockSpec((bt, d), lambda i: (i, 0))
    vec = pl.BlockSpec((1, d), lambda i: (0, 0))
    in_specs, args = [row, vec, row], [x, g, dh]
    out_specs, out_shape = [vec], [jax.ShapeDtypeStruct((1, d), F32)]
    if want_dx:
        in_specs.append(row)
        args.append(dres)
        out_specs = [row, row] + out_specs
        out_shape = [jax.ShapeDtypeStruct((t, d), F32), jax.ShapeDtypeStruct((t, d), BF16)] + out_shape
    return pl.pallas_call(
        body, name=name, grid=(t // bt,), in_specs=in_specs, out_specs=out_specs, out_shape=out_shape,
        compiler_params=_params(("arbitrary",)),
    )(*args)


def loss_head(x, g, tgt, *, name):
    t, d = x.shape
    bt = _blk(256, t)

    def body(x_ref, g_ref, t_ref, loss_ref, dx_ref, dxb_ref, dg_ref):
        xv = x_ref[...]
        gv = g_ref[...]
        r = lax.rsqrt(jnp.mean(xv * xv, axis=-1, keepdims=True) + EPS)
        xh = xv * r
        e = xh * gv - t_ref[...]
        lpart = jnp.zeros((1, 128), F32) + jnp.sum(e * e) * (0.5 / d)
        dyv = e * (1.0 / d)
        gpart = jnp.sum(dyv * xh, axis=0, keepdims=True)

        @pl.when(pl.program_id(0) == 0)
        def _():
            loss_ref[...] = lpart
            dg_ref[...] = gpart

        @pl.when(pl.program_id(0) > 0)
        def _():
            loss_ref[...] += lpart
            dg_ref[...] += gpart

        dy = dyv * gv
        dx = r * (dy - xh * jnp.mean(dy * xh, axis=-1, keepdims=True))
        dx_ref[...] = dx
        dxb_ref[...] = dx.astype(BF16)

    row = pl.BlockSpec((bt, d), lambda i: (i, 0))
    vec = pl.BlockSpec((1, d), lambda i: (0, 0))
    return pl.pallas_call(
        body, name=name, grid=(t // bt,), in_specs=[row, vec, row],
        out_specs=[pl.BlockSpec((1, 128), lambda i: (0, 0)), row, row, vec],
        out_shape=[jax.ShapeDtypeStruct((1, 128), F32), jax.ShapeDtypeStruct((t, d), F32),
                   jax.ShapeDtypeStruct((t, d), BF16), jax.ShapeDtypeStruct((1, d), F32)],
        compiler_params=_params(("arbitrary",)),
    )(x, g, tgt)


def _sigmoid(x):
    return 1.0 / (1.0 + jnp.exp(-x))


def _swiglu_epi(b, a):
    return b, a * _sigmoid(a) * b


def _swiglu_bwd_epi(ds, a, b):
    sg = _sigmoid(a)
    return ds * b * sg * (1.0 + a * (1.0 - sg)), ds * a * sg


def _xa_probs(qh, kh, scale):
    s = lax.dot_general(qh, kh, (((1,), (1,)), ((), ())), preferred_element_type=F32) * scale
    s = s - jnp.max(s, axis=-1, keepdims=True)
    p = jnp.exp(s)
    return p / jnp.sum(p, axis=-1, keepdims=True)


def xattn_fwd(q, kv, *, name):
    t, d = q.shape
    m = kv.shape[0]
    hd = d // XA_HEADS
    bt = _blk(512, t)
    scale = hd ** -0.5

    def body(q_ref, kv_ref, o_ref):
        for h in range(XA_HEADS):
            qh = q_ref[:, h * hd:(h + 1) * hd]
            kh = kv_ref[:, h * hd:(h + 1) * hd]
            vh = kv_ref[:, d + h * hd:d + (h + 1) * hd]
            p = _xa_probs(qh, kh, scale)
            o_ref[:, h * hd:(h + 1) * hd] = jnp.dot(p.astype(BF16), vh, preferred_element_type=F32).astype(BF16)

    return pl.pallas_call(
        body, name=name, grid=(t // bt,),
        in_specs=[pl.BlockSpec((bt, d), lambda i: (i, 0)), pl.BlockSpec((m, 2 * d), lambda i: (0, 0))],
        out_specs=pl.BlockSpec((bt, d), lambda i: (i, 0)), out_shape=jax.ShapeDtypeStruct((t, d), BF16),
        compiler_params=_params(("parallel",)),
    )(q, kv)


def xattn_bwd(q, kv, do, *, name):
    t, d = q.shape
    m = kv.shape[0]
    hd = d // XA_HEADS
    bt = _blk(512, t)
    scale = hd ** -0.5

    def body(q_ref, kv_ref, do_ref, dq_ref, dkv_ref):
        first = pl.program_id(0) == 0
        for h in range(XA_HEADS):
            qs, ks, vs = slice(h * hd, (h + 1) * hd), slice(h * hd, (h + 1) * hd), slice(d + h * hd, d + (h + 1) * hd)
            qh, kh, vh, doh = q_ref[:, qs], kv_ref[:, ks], kv_ref[:, vs], do_ref[:, qs]
            p = _xa_probs(qh, kh, scale)
            dp = lax.dot_general(doh, vh, (((1,), (1,)), ((), ())), preferred_element_type=F32)
            dsv = p * (dp - jnp.sum(p * dp, axis=-1, keepdims=True)) * scale
            dsb = dsv.astype(BF16)
            dq_ref[:, qs] = jnp.dot(dsb, kh, preferred_element_type=F32).astype(BF16)
            dk = lax.dot_general(dsb, qh, (((0,), (0,)), ((), ())), preferred_element_type=F32)
            dv = lax.dot_general(p.astype(BF16), doh, (((0,), (0,)), ((), ())), preferred_element_type=F32)

            @pl.when(first)
            def _():
                dkv_ref[:, ks] = dk
                dkv_ref[:, vs] = dv

            @pl.when(jnp.logical_not(first))
            def _():
                dkv_ref[:, ks] += dk
                dkv_ref[:, vs] += dv

    row = pl.BlockSpec((bt, d), lambda i: (i, 0))
    full = pl.BlockSpec((m, 2 * d), lambda i: (0, 0))
    return pl.pallas_call(
        body, name=name, grid=(t // bt,), in_specs=[row, full, row], out_specs=[row, full],
        out_shape=[jax.ShapeDtypeStruct((t, d), BF16), jax.ShapeDtypeStruct((m, 2 * d), F32)],
        compiler_params=_params(("arbitrary",)),
    )(q, kv, do)


def _pool_p(buf, uv, rows, w, bt):
    acc = uv
    for dd in range(1, w):
        acc = acc + buf[pl.ds(POOL_HALO - dd, bt), :]
    cnt = jnp.minimum(rows + 1, w).astype(F32)
    return acc / cnt - uv


def pool_fwd(z, w_pool, scale, *, name):
    t = z.shape[0]
    ng, gsz = w_pool.shape[0], w_pool.shape[1]
    mix = ng * gsz
    bt = _blk(512, t)

    def body(u_ref, uh_ref, w_ref, sc_ref, o_ref, buf):
        r = pl.program_id(0)
        rows = r * bt + lax.broadcasted_iota(jnp.int32, (bt, 1), 0)
        for g in range(ng):
            gs = slice(g * gsz, (g + 1) * gsz)
            uv = u_ref[:, gs]
            buf[0:POOL_HALO, :] = jnp.where(r > 0, uh_ref[:, gs], 0.0)
            buf[POOL_HALO:POOL_HALO + bt, :] = uv
            p = _pool_p(buf, uv, rows, POOL_WINDOWS[g], bt)
            y = jnp.dot(p.astype(BF16), w_ref[g], preferred_element_type=F32) * sc_ref[:, gs]
            o_ref[:, gs] = y.astype(BF16)

    hb = bt // POOL_HALO
    return pl.pallas_call(
        body, name=name, grid=(t // bt,),
        in_specs=[pl.BlockSpec((bt, mix), lambda i: (i, 0)),
                  pl.BlockSpec((POOL_HALO, mix), lambda i: (jnp.maximum(i * hb - 1, 0), 0)),
                  pl.BlockSpec((ng, gsz, gsz), lambda i: (0, 0, 0)), pl.BlockSpec((1, mix), lambda i: (0, 0))],
        out_specs=pl.BlockSpec((None, bt, mix), lambda i: (0, i, 0)),
        out_shape=jax.ShapeDtypeStruct((2, t, mix), BF16),
        scratch_shapes=[pltpu.VMEM((POOL_HALO + bt, gsz), F32)],
        compiler_params=_params(("parallel",)),
    )(z, z, w_pool, scale)


def pool_bwd(z, dcat, w_pool, scale, *, name):
    t = z.shape[0]
    ng, gsz = w_pool.shape[0], w_pool.shape[1]
    mix = ng * gsz
    bt = _blk(512, t)
    nb = t // bt
    nt_dims = (((1,), (1,)), ((), ()))
    tn_dims = (((0,), (0,)), ((), ()))

    def body(u_ref, uh_ref, dy_ref, dyh_ref, w_ref, sc_ref, du_ref, dw_ref, dsc_ref, buf, buf2):
        r = pl.program_id(0)
        first = r == 0
        rows = r * bt + lax.broadcasted_iota(jnp.int32, (bt, 1), 0)
        rows_h = (r + 1) * bt + lax.broadcasted_iota(jnp.int32, (POOL_HALO, 1), 0)
        for g in range(ng):
            w = POOL_WINDOWS[g]
            gs = slice(g * gsz, (g + 1) * gsz)
            uv = u_ref[:, gs]
            buf[0:POOL_HALO, :] = jnp.where(r > 0, uh_ref[:, gs], 0.0)
            buf[POOL_HALO:POOL_HALO + bt, :] = uv
            pb = _pool_p(buf, uv, rows, w, bt).astype(BF16)
            wg = w_ref[g]
            sc = sc_ref[:, gs]
            y0 = jnp.dot(pb, wg, preferred_element_type=F32)
            dyv = dy_ref[:, gs].astype(F32)
            dsc = jnp.sum(dyv * y0, axis=0, keepdims=True)
            dyw = (dyv * sc).astype(BF16)
            dw = lax.dot_general(pb, dyw, tn_dims, preferred_element_type=F32)

            @pl.when(first)
            def _():
                dw_ref[g] = dw
                dsc_ref[:, gs] = dsc

            @pl.when(jnp.logical_not(first))
            def _():
                dw_ref[g] += dw
                dsc_ref[:, gs] += dsc

            dp = lax.dot_general(dyw, wg, nt_dims, preferred_element_type=F32)
            dyh = (dyh_ref[:, gs].astype(F32) * sc).astype(BF16)
            dph = lax.dot_general(dyh, wg, nt_dims, preferred_element_type=F32)
            dph = jnp.where(r < nb - 1, dph, 0.0)
            buf2[0:bt, :] = dp / jnp.minimum(rows + 1, w).astype(F32)
            buf2[bt:bt + POOL_HALO, :] = dph / jnp.minimum(rows_h + 1, w).astype(F32)
            acc = buf2[pl.ds(0, bt), :]
            for dd in range(1, w):
                acc = acc + buf2[pl.ds(dd, bt), :]
            du_ref[:, gs] = (acc - dp).astype(BF16)

    hb = bt // POOL_HALO
    nhb = t // POOL_HALO
    return pl.pallas_call(
        body, name=name, grid=(nb,),
        in_specs=[pl.BlockSpec((bt, mix), lambda i: (i, 0)),
                  pl.BlockSpec((POOL_HALO, mix), lambda i: (jnp.maximum(i * hb - 1, 0), 0)),
                  pl.BlockSpec((None, bt, mix), lambda i: (0, i, 0)),
                  pl.BlockSpec((None, POOL_HALO, mix), lambda i: (0, jnp.minimum((i + 1) * hb, nhb - 1), 0)),
                  pl.BlockSpec((ng, gsz, gsz), lambda i: (0, 0, 0)), pl.BlockSpec((1, mix), lambda i: (0, 0))],
        out_specs=[pl.BlockSpec((None, bt, mix), lambda i: (4, i, 0)),
                   pl.BlockSpec((ng, gsz, gsz), lambda i: (0, 0, 0)), pl.BlockSpec((1, mix), lambda i: (0, 0))],
        out_shape=[jax.ShapeDtypeStruct((5, t, mix), BF16), jax.ShapeDtypeStruct((ng, gsz, gsz), F32),
                   jax.ShapeDtypeStruct((1, mix), F32)],
        scratch_shapes=[pltpu.VMEM((POOL_HALO + bt, gsz), F32), pltpu.VMEM((bt + POOL_HALO, gsz), F32)],
        compiler_params=_params(("arbitrary",)),
    )(z, z, dcat, dcat, w_pool, scale)


HG_HEADS_PER_STEP = 2
HG_LEVELS = ((64, 31), (32, 15), (16, 7))
HG_DIAG = (8, 3)
_NT = (((1,), (1,)), ((), ()))
_TN = (((0,), (0,)), ((), ()))
_HI = lax.Precision.HIGHEST


def _hg_masks():
    c = HG_CHUNK
    t = lax.broadcasted_iota(jnp.int32, (c, c), 0)
    s = lax.broadcasted_iota(jnp.int32, (c, c), 1)
    masks = []
    for blk, row in HG_LEVELS:
        sh = blk.bit_length() - 1
        same = (t >> sh) == (s >> sh)
        masks.append(same & ((t & (blk - 1)) > row) & ((s & (blk - 1)) <= row))
    sh = HG_DIAG[0].bit_length() - 1
    masks.append(((t >> sh) == (s >> sh)) & (s <= t))
    return t, s, masks


def _row_of_block(x, blk, row):
    c, n = x.shape
    x3 = x.reshape(c // blk, blk, n)
    return jnp.broadcast_to(x3[:, row:row + 1, :], x3.shape).reshape(c, n)


def _hg_parts(qv, flv, lb, masks, tri):
    sgf = _sigmoid(flv)
    f = lb + (1.0 - lb) * sgf
    logf = jnp.log(f)
    kk = 1.0 - f
    sgq = _sigmoid(qv)
    qf = qv * sgq * (HG_HEAD ** -0.5)
    bc = jnp.dot(tri, logf, preferred_element_type=F32, precision=_HI)
    levels = []
    a = None
    for li, (blk, row) in enumerate(HG_LEVELS + (HG_DIAG,)):
        e = bc - _row_of_block(bc, blk, row)
        if li < len(HG_LEVELS):
            eq, ek = jnp.exp(jnp.minimum(e, 0.0)), jnp.exp(jnp.minimum(-e, 0.0))
        else:
            eq, ek = jnp.exp(jnp.clip(e, -EXP_CLAMP, EXP_CLAMP)), jnp.exp(jnp.clip(-e, -EXP_CLAMP, EXP_CLAMP))
        qt, kt = qf * eq, kk * ek
        part = jnp.where(masks[li], lax.dot_general(qt.astype(BF16), kt.astype(BF16), _NT, preferred_element_type=F32), 0.0)
        a = part if a is None else a + part
        levels.append((eq, ek, qt, kt))
    return dict(sgf=sgf, f=f, kk=kk, sgq=sgq, qf=qf, bc=bc, levels=levels, a=a)


def hgrn_fwd(z, cat, lb, gain, mix_a, *, side=None, name):
    t = z.shape[0]
    mix_b = lb.shape[1]
    nh = mix_b // HG_HEAD
    bt = _blk(256, t)
    ncb = bt // HG_CHUNK
    dh = HG_HEAD

    def body(q_ref, fl_ref, i_ref, g_ref, lb_ref, gain_ref, cat_in, o_ref, st_ref, st):
        del cat_in

        @pl.when(pl.program_id(1) == 0)
        def _():
            st[...] = jnp.zeros_like(st)

        t_i, s_i, masks = _hg_masks()
        tri = (s_i <= t_i).astype(F32)
        lbv, gn = lb_ref[...], gain_ref[...]
        for c in range(ncb):
            rs = slice(c * HG_CHUNK, (c + 1) * HG_CHUNK)
            pr = _hg_parts(q_ref[rs, :], fl_ref[rs, :], lbv, masks, tri)
            vb = i_ref[rs, :].astype(BF16)
            stv = st[...]
            st_ref[c] = stv
            bc = pr["bc"]
            qt = pr["qf"] * jnp.exp(bc)
            o = (jnp.dot(pr["a"].astype(BF16), vb, preferred_element_type=F32)
                 + lax.dot_general(qt.astype(BF16), stv.astype(BF16), _NT, preferred_element_type=F32))
            bl = bc[HG_CHUNK - 1:HG_CHUNK, :]
            khat = pr["kk"] * jnp.exp(bl - bc)
            st[...] = stv * jnp.exp(bl) + lax.dot_general(vb, khat.astype(BF16), _TN, preferred_element_type=F32)
            r = lax.rsqrt(jnp.mean(o * o, axis=-1, keepdims=True) + EPS)
            gv = g_ref[rs, :]
            o_ref[rs, :] = (o * r * gn * (gv * _sigmoid(gv))).astype(BF16)

    def col(which):
        base = (mix_a + which * mix_b) // dh
        return pl.BlockSpec((bt, dh), lambda h, i: (i, base + h))

    return _call(
        body, side, name=name, grid=(nh, t // bt),
        in_specs=[col(0), col(1), col(2), col(3), pl.BlockSpec((1, dh), lambda h, i: (0, h)),
                  pl.BlockSpec((1, dh), lambda h, i: (0, 0)), _ANY],
        out_specs=[pl.BlockSpec((None, bt, dh), lambda h, i: (1, i, h)),
                   pl.BlockSpec((None, ncb, dh, dh), lambda h, i: (h, i, 0, 0))],
        out_shape=[jax.ShapeDtypeStruct(cat.shape, BF16), jax.ShapeDtypeStruct((nh, t // HG_CHUNK, dh, dh), F32)],
        scratch_shapes=[pltpu.VMEM((dh, dh), F32)], aliases={6: 0}, sem=("parallel", "arbitrary"),
        args=(z, z, z, z, lb, gain, cat))


def hgrn_bwd(z, dcat, dz5, states, lb, gain, mix_a, *, side=None, name):
    t = z.shape[0]
    mix_b = lb.shape[1]
    nh = mix_b // HG_HEAD
    bt = _blk(256, t)
    nb = t // bt
    ncb = bt // HG_CHUNK
    dh = HG_HEAD
    hp = HG_HEADS_PER_STEP if nh % HG_HEADS_PER_STEP == 0 else 1

    def body(q_ref, fl_ref, i_ref, g_ref, dy_ref, st_ref, lb_ref, gain_ref, dz_in, dz_ref, dlb_ref, dgn_ref, dst):
        del dz_in
        first = pl.program_id(1) == 0

        @pl.when(first)
        def _():
            dst[...] = jnp.zeros_like(dst)

        t_i, s_i, masks = _hg_masks()
        tri = (s_i <= t_i).astype(F32)
        triu = (s_i >= t_i).astype(F32)
        last_row = lax.broadcasted_iota(jnp.int32, (HG_CHUNK, 1), 0) == HG_CHUNK - 1
        gn = gain_ref[...]
        dlb_acc = [jnp.zeros((1, dh), F32) for _ in range(hp)]
        dgn_acc = [jnp.zeros((1, dh), F32) for _ in range(hp)]
        for c, hh in [(c, hh) for c in reversed(range(ncb)) for hh in range(hp)]:
            rs, cs = slice(c * HG_CHUNK, (c + 1) * HG_CHUNK), slice(hh * dh, (hh + 1) * dh)
            lbv = lb_ref[:, cs]
            qv, flv, gv = q_ref[rs, cs], fl_ref[rs, cs], g_ref[rs, cs]
            pr = _hg_parts(qv, flv, lbv, masks, tri)
            vb = i_ref[rs, cs].astype(BF16)
            stv = st_ref[hh, c]
            stb = stv.astype(BF16)
            dsv = dst[hh]
            dsb = dsv.astype(BF16)
            bc, kk, qf, ab = pr["bc"], pr["kk"], pr["qf"], pr["a"].astype(BF16)
            ebc = jnp.exp(bc)
            qt = qf * ebc
            qtb = qt.astype(BF16)
            o = jnp.dot(ab, vb, preferred_element_type=F32) + lax.dot_general(qtb, stb, _NT, preferred_element_type=F32)
            r = lax.rsqrt(jnp.mean(o * o, axis=-1, keepdims=True) + EPS)
            oh = o * r
            sgg = _sigmoid(gv)
            dyv = dy_ref[rs, cs].astype(F32)
            don = dyv * (gv * sgg)
            dgate = dyv * (oh * gn) * (sgg * (1.0 + gv * (1.0 - sgg)))
            dgn_acc[hh] = dgn_acc[hh] + jnp.sum(don * oh, axis=0, keepdims=True)
            doh = don * gn
            do = r * (doh - oh * jnp.mean(doh * oh, axis=-1, keepdims=True))
            dob = do.astype(BF16)
            bl = bc[HG_CHUNK - 1:HG_CHUNK, :]
            ebl = jnp.exp(bl)
            ekh = jnp.exp(bl - bc)
            khat = kk * ekh
            dv = (lax.dot_general(ab, dob, _TN, preferred_element_type=F32)
                  + lax.dot_general(khat.astype(BF16), dsb, _NT, preferred_element_type=F32))
            da = lax.dot_general(dob, vb, _NT, preferred_element_type=F32)
            dqt = jnp.dot(dob, stb, preferred_element_type=F32)
            dkh = jnp.dot(vb, dsb, preferred_element_type=F32)
            dst[hh] = dsv * ebl + lax.dot_general(dob, qtb, _TN, preferred_element_type=F32)
            dbl = jnp.sum(dsv * stv, axis=0, keepdims=True) * ebl + jnp.sum(dkh * khat, axis=0, keepdims=True)
            dqf = dqt * ebc
            dkk = dkh * ekh
            dbc = dqt * qt - dkh * khat
            for li, (eq, ek, qtl, ktl) in enumerate(pr["levels"]):
                gm = jnp.where(masks[li], da, 0.0).astype(BF16)
                qtr, ktr = qtl.astype(BF16), ktl.astype(BF16)
                dql = jnp.dot(gm, ktr, preferred_element_type=F32)
                dkl = lax.dot_general(gm, qtr, _TN, preferred_element_type=F32)
                dqf = dqf + dql * eq
                dkk = dkk + dkl * ek
                dbc = dbc + qtr.astype(F32) * dql - ktr.astype(F32) * dkl
            dbc = dbc + jnp.where(last_row, dbl, 0.0)
            dlogf = jnp.dot(triu, dbc, preferred_element_type=F32, precision=_HI)
            df = dlogf / pr["f"] - dkk
            sgf = pr["sgf"]
            dfl = df * (1.0 - lbv) * sgf * (1.0 - sgf)
            dlb_acc[hh] = dlb_acc[hh] + jnp.sum(df * (1.0 - sgf), axis=0, keepdims=True)
            sgq = pr["sgq"]
            dq = dqf * (HG_HEAD ** -0.5) * (sgq * (1.0 + qv * (1.0 - sgq)))
            dz_ref[0, rs, cs] = dq.astype(BF16)
            dz_ref[1, rs, cs] = dfl.astype(BF16)
            dz_ref[2, rs, cs] = dv.astype(BF16)
            dz_ref[3, rs, cs] = dgate.astype(BF16)

        @pl.when(first)
        def _():
            for hh in range(hp):
                dlb_ref[:, hh * dh:(hh + 1) * dh] = dlb_acc[hh]
                dgn_ref[hh] = dgn_acc[hh]

        @pl.when(jnp.logical_not(first))
        def _():
            for hh in range(hp):
                dlb_ref[:, hh * dh:(hh + 1) * dh] += dlb_acc[hh]
                dgn_ref[hh] += dgn_acc[hh]

    wd = hp * dh

    def col(which):
        base = (mix_a + which * mix_b) // wd
        return pl.BlockSpec((bt, wd), lambda h, i: (nb - 1 - i, base + h))

    return _call(
        body, side, name=name, grid=(nh // hp, nb),
        in_specs=[col(0), col(1), col(2), col(3),
                  pl.BlockSpec((None, bt, wd), lambda h, i: (1, nb - 1 - i, h)),
                  pl.BlockSpec((hp, ncb, dh, dh), lambda h, i: (h, nb - 1 - i, 0, 0)),
                  pl.BlockSpec((1, wd), lambda h, i: (0, h)), pl.BlockSpec((1, dh), lambda h, i: (0, 0)), _ANY],
        out_specs=[pl.BlockSpec((4, bt, wd), lambda h, i: (0, nb - 1 - i, h)),
                   pl.BlockSpec((1, wd), lambda h, i: (0, h)),
                   pl.BlockSpec((hp, 1, dh), lambda h, i: (h, 0, 0))],
        out_shape=[jax.ShapeDtypeStruct(dz5.shape, BF16), jax.ShapeDtypeStruct((1, mix_b), F32),
                   jax.ShapeDtypeStruct((nh, 1, dh), F32)],
        scratch_shapes=[pltpu.VMEM((hp, dh, dh), F32)], aliases={8: 0}, sem=("parallel", "arbitrary"),
        args=(z, z, z, z, dcat, states, lb, gain, dz5))


LOG2E = 1.4426950408889634


def _fox_scores(qb, kb, fk, scale, masked):
    s = lax.dot_general(qb, kb, _NT, preferred_element_type=F32) * (scale * LOG2E) - fk * LOG2E
    if masked:
        n = s.shape[0]
        row = lax.broadcasted_iota(jnp.int32, (n, n), 0)
        col = lax.broadcasted_iota(jnp.int32, (n, n), 1)
        s = jnp.where(col <= row, s, -jnp.inf)
    return s


def fox_fwd(qkv, fk, *, side=None, name):
    _, t, d = qkv.shape
    nh = d // FOX_HEAD
    b = _blk(FOX_BLK, t)
    nb = t // b
    dh = FOX_HEAD
    scale = dh ** -0.5

    def body(q_ref, k_ref, v_ref, f_ref, o_ref, lse_ref):
        qi = pl.program_id(1)
        qb = q_ref[...]

        def step(kj, carry, masked):
            m, l, acc = carry
            off = pl.multiple_of(kj * b, b)
            s = _fox_scores(qb, k_ref[pl.ds(off, b), :], f_ref[kj], scale, masked)
            m_new = jnp.maximum(m, jnp.max(s, axis=-1, keepdims=True))
            alpha = jnp.exp2(m - m_new)
            p = jnp.exp2(s - m_new)
            l = alpha * l + jnp.sum(p, axis=-1, keepdims=True)
            acc = alpha * acc + jnp.dot(p.astype(BF16), v_ref[pl.ds(off, b), :], preferred_element_type=F32)
            return m_new, l, acc

        init = (jnp.full((b, 1), -jnp.inf, F32), jnp.zeros((b, 1), F32), jnp.zeros((b, dh), F32))
        carry = lax.fori_loop(0, qi, lambda kj, c: step(kj, c, False), init)
        m, l, acc = step(qi, carry, True)
        o_ref[...] = (acc / l).astype(BF16)
        lse_ref[...] = m + jnp.log(l) * LOG2E

    return _call(
        body, side, name=name, grid=(nh, nb),
        in_specs=[pl.BlockSpec((None, b, dh), lambda h, i: (0, i, h)),
                  pl.BlockSpec((None, t, dh), lambda h, i: (1, 0, h)),
                  pl.BlockSpec((None, t, dh), lambda h, i: (2, 0, h)),
                  pl.BlockSpec((None, nb, 1, b), lambda h, i: (h, 0, 0, 0))],
        out_specs=[pl.BlockSpec((b, dh), lambda h, i: (i, h)), pl.BlockSpec((None, b, 1), lambda h, i: (h, i, 0))],
        out_shape=[jax.ShapeDtypeStruct((t, d), BF16), jax.ShapeDtypeStruct((nh, t, 1), F32)],
        sem=("parallel", "parallel"), args=(qkv, qkv, qkv, fk))


def fox_bwd_dq(qkv, fk, do, lse, *, side=None, name):
    _, t, d = qkv.shape
    nh = d // FOX_HEAD
    b = _blk(FOX_BLK, t)
    nb = t // b
    dh = FOX_HEAD
    scale = dh ** -0.5

    def body(q_ref, k_ref, v_ref, f_ref, do_ref, lse_ref, dq_ref, dl_ref, p_buf, dp_buf):
        qi = pl.program_id(1)
        qb, dob, lse_v = q_ref[...], do_ref[...], lse_ref[...]

        def first(kj, dl, masked):
            off = pl.multiple_of(kj * b, b)
            p = jnp.exp2(_fox_scores(qb, k_ref[pl.ds(off, b), :], f_ref[kj], scale, masked) - lse_v)
            dp = lax.dot_general(dob, v_ref[pl.ds(off, b), :], _NT, preferred_element_type=F32)
            p_buf[kj] = p
            dp_buf[kj] = dp
            return dl + jnp.sum(p * dp, axis=-1, keepdims=True)

        dl = lax.fori_loop(0, qi, lambda kj, c: first(kj, c, False), jnp.zeros((b, 1), F32))
        dl = first(qi, dl, True)
        dl_ref[...] = dl

        def second(kj, dq):
            off = pl.multiple_of(kj * b, b)
            dsv = p_buf[kj] * (dp_buf[kj] - dl)
            return dq + jnp.dot(dsv.astype(BF16), k_ref[pl.ds(off, b), :], preferred_element_type=F32)

        dq = lax.fori_loop(0, qi + 1, second, jnp.zeros((b, dh), F32))
        dq_ref[...] = (dq * scale).astype(BF16)

    col = pl.BlockSpec((None, b, 1), lambda h, i: (h, i, 0))
    return _call(
        body, side, name=name, grid=(nh, nb),
        in_specs=[pl.BlockSpec((None, b, dh), lambda h, i: (0, i, h)),
                  pl.BlockSpec((None, t, dh), lambda h, i: (1, 0, h)),
                  pl.BlockSpec((None, t, dh), lambda h, i: (2, 0, h)),
                  pl.BlockSpec((None, nb, 1, b), lambda h, i: (h, 0, 0, 0)),
                  pl.BlockSpec((b, dh), lambda h, i: (i, h)), col],
        out_specs=[pl.BlockSpec((None, b, dh), lambda h, i: (2, i, h)), col],
        out_shape=[jax.ShapeDtypeStruct((3, t, d), BF16), jax.ShapeDtypeStruct((nh, t, 1), F32)],
        scratch_shapes=[pltpu.VMEM((nb, b, b), F32), pltpu.VMEM((nb, b, b), F32)],
        sem=("parallel", "parallel"), args=(qkv, qkv, qkv, fk, do, lse))


def fox_bwd_dkv(qkv, fk, do, lse, delta, dqkv, *, side=None, name):
    _, t, d = qkv.shape
    nh = d // FOX_HEAD
    b = _blk(FOX_BLK, t)
    nb = t // b
    dh = FOX_HEAD
    scale = dh ** -0.5

    def body(q_ref, k_ref, v_ref, f_ref, do_ref, lse_ref, dl_ref, dz_in, dkv_ref, df_ref):
        del dz_in
        kj = pl.program_id(1)
        kb, vb, fkv = k_ref[...], v_ref[...], f_ref[...]

        def step(qi, carry, masked):
            dk, dv, df = carry
            off = pl.multiple_of(qi * b, b)
            qb, dob = q_ref[pl.ds(off, b), :], do_ref[pl.ds(off, b), :]
            p = jnp.exp2(_fox_scores(qb, kb, fkv, scale, masked) - lse_ref[pl.ds(off, b), :])
            dv = dv + lax.dot_general(p.astype(BF16), dob, _TN, preferred_element_type=F32)
            dp = lax.dot_general(dob, vb, _NT, preferred_element_type=F32)
            dsv = p * (dp - dl_ref[pl.ds(off, b), :])
            dk = dk + lax.dot_general(dsv.astype(BF16), qb, _TN, preferred_element_type=F32)
            return dk, dv, df - jnp.sum(dsv, axis=0, keepdims=True)

        init = (jnp.zeros((b, dh), F32), jnp.zeros((b, dh), F32), jnp.zeros((1, b), F32))
        carry = step(kj, init, True)
        dk, dv, df = lax.fori_loop(kj + 1, nb, lambda qi, c: step(qi, c, False), carry)
        dkv_ref[0] = (dk * scale).astype(BF16)
        dkv_ref[1] = dv.astype(BF16)
        df_ref[...] = df

    col = pl.BlockSpec((None, t, 1), lambda h, j: (h, 0, 0))
    return _call(
        body, side, name=name, grid=(nh, nb),
        in_specs=[pl.BlockSpec((None, t, dh), lambda h, j: (0, 0, h)),
                  pl.BlockSpec((None, b, dh), lambda h, j: (1, j, h)),
                  pl.BlockSpec((None, b, dh), lambda h, j: (2, j, h)),
                  pl.BlockSpec((None, None, 1, b), lambda h, j: (h, j, 0, 0)),
                  pl.BlockSpec((t, dh), lambda h, j: (0, h)), col, col, pl.BlockSpec(memory_space=pl.ANY)],
        out_specs=[pl.BlockSpec((2, b, dh), lambda h, j: (0, j, h)),
                   pl.BlockSpec((None, None, 1, b), lambda h, j: (h, j, 0, 0))],
        out_shape=[jax.ShapeDtypeStruct((3, t, d), BF16), jax.ShapeDtypeStruct((nh, nb, 1, b), F32)],
        aliases={7: 0}, sem=("parallel", "parallel"), args=(qkv, qkv, qkv, fk, do, lse, delta, dqkv))


FL_BLK = 256


def _log_sigmoid(x):
    return jnp.minimum(x, 0.0) - jnp.log(1.0 + jnp.exp(-jnp.abs(x)))


def fl_fwd(zf, bf, *, name):
    t, n = zf.shape
    bt = _blk(FL_BLK, t)

    def body(z_ref, b_ref, o_ref, carry):
        @pl.when(pl.program_id(0) == 0)
        def _():
            carry[...] = jnp.zeros_like(carry)

        ls = _log_sigmoid(z_ref[...] + b_ref[...])
        r = lax.broadcasted_iota(jnp.int32, (bt, bt), 0)
        c = lax.broadcasted_iota(jnp.int32, (bt, bt), 1)
        cs = jnp.dot((c <= r).astype(F32), ls, preferred_element_type=F32, precision=_HI) + carry[...]
        o_ref[...] = cs
        carry[...] = cs[bt - 1:bt, :]

    return pl.pallas_call(
        body, name=name, grid=(t // bt,),
        in_specs=[pl.BlockSpec((bt, n), lambda i: (i, 0)), pl.BlockSpec((1, n), lambda i: (0, 0))],
        out_specs=pl.BlockSpec((bt, n), lambda i: (i, 0)), out_shape=jax.ShapeDtypeStruct((t, n), F32),
        scratch_shapes=[pltpu.VMEM((1, n), F32)], compiler_params=_params(("arbitrary",)),
    )(zf, bf)


def fl_bwd(df, zf, bf, *, name):
    t, n = zf.shape
    bt = _blk(FL_BLK, t)
    nb = t // bt

    def body(df_ref, z_ref, b_ref, dz_ref, db_ref, carry):
        first = pl.program_id(0) == 0

        @pl.when(first)
        def _():
            carry[...] = jnp.zeros_like(carry)

        r = lax.broadcasted_iota(jnp.int32, (bt, bt), 0)
        c = lax.broadcasted_iota(jnp.int32, (bt, bt), 1)
        dls = jnp.dot((c >= r).astype(F32), df_ref[...], preferred_element_type=F32, precision=_HI) + carry[...]
        carry[...] = dls[0:1, :]
        dz = dls * (1.0 - _sigmoid(z_ref[...] + b_ref[...]))
        dz_ref[...] = dz.astype(BF16)
        part = jnp.sum(dz, axis=0, keepdims=True)

        @pl.when(first)
        def _():
            db_ref[...] = part

        @pl.when(jnp.logical_not(first))
        def _():
            db_ref[...] += part

    row = pl.BlockSpec((bt, n), lambda i: (nb - 1 - i, 0))
    vec = pl.BlockSpec((1, n), lambda i: (0, 0))
    return pl.pallas_call(
        body, name=name, grid=(nb,), in_specs=[row, row, vec], out_specs=[row, vec],
        out_shape=[jax.ShapeDtypeStruct((t, n), BF16), jax.ShapeDtypeStruct((1, n), F32)],
        scratch_shapes=[pltpu.VMEM((1, n), F32)], compiler_params=_params(("arbitrary",)),
    )(df, zf, bf)


def _adamw_math(w, g, m, v):
    m = ADAM_B1 * m + (1.0 - ADAM_B1) * g
    v = ADAM_B2 * v + (1.0 - ADAM_B2) * (g * g)
    m_hat = m / (1.0 - ADAM_B1 ** ADAM_STEP)
    v_hat = v / (1.0 - ADAM_B2 ** ADAM_STEP)
    delta = -ADAM_LR * (m_hat / (jnp.sqrt(v_hat) + ADAM_EPS) + ADAM_WD * w)
    return delta, m, v


def adamw(w, g, m, v, *, side=None, name):
    nl, r, c = w.shape
    br = _row_blk(r, 256)

    def body(w_ref, g_ref, m_ref, v_ref, go_ref, d_ref, mo_ref, vo_ref):
        gv = g_ref[...]
        go_ref[...] = gv
        d_ref[...], mo_ref[...], vo_ref[...] = _adamw_math(w_ref[...], gv, m_ref[...], v_ref[...])

    spec = pl.BlockSpec((None, br, c), lambda l, i: (l, i, 0))
    return _call(body, side, name=name, grid=(nl, r // br), in_specs=[spec] * 4, out_specs=[spec] * 4,
                 out_shape=[jax.ShapeDtypeStruct((nl, r, c), F32)] * 4, sem=("parallel", "parallel"), args=(w, g, m, v))


def _f2(a):
    return a.reshape(a.shape[-2:])


def _local_step(x0, mem, tgt, sp, plan):
    t, d = x0.shape
    mix_a = sp["pool_scale"].shape[1]
    small = {}

    def row(a, l):
        return a[l:l + 1]

    def rows4(g):
        return g.reshape(N_CHIPS, -1, g.shape[-1])

    def xattn_f(l, xin):
        w = plan.weights(f"xa{l}")
        hx = rmsnorm_fwd(xin, row(sp["xa_norm"], l), name=f"xa_norm_f{l}")
        q = _f2(matmul(vm2(hx), w["wq"], "nn", out_dtype=BF16, side=plan.take_fwd(), name=f"xa_q_f{l}"))
        mn = rmsnorm_fwd(mem, row(sp["xa_mem_norm"], l), name=f"xa_memnorm_f{l}")
        kv = _f2(matmul(vm2(mn), w["wkv"], "nn", out_dtype=BF16, name=f"xa_kv_f{l}"))
        o = xattn_fwd(q, kv, name=f"xa_attn_f{l}")
        xout = _f2(matmul(vm2(o), w["wo"], "nn", out_dtype=F32, res=vm2(xin), side=plan.take_fwd(), name=f"xa_o_f{l}"))
        return xout, (xin, hx, q, mn, kv, o)

    def ffn_f(l, xin):
        w = plan.weights(f"ffn{l}")
        hf = rmsnorm_fwd(xin, row(sp["ffn_norm"], l), name=f"ffn_norm_f{l}")
        a = _f2(matmul(vm2(hf), w["gate"], "nn", out_dtype=BF16, side=plan.take_fwd(), name=f"ffn_gate_f{l}"))
        b, s = matmul(vm2(hf), w["up"], "nn", out_dtype=BF16, epi=(_swiglu_epi, [vm2(a)], 2), side=plan.take_fwd(), name=f"ffn_up_f{l}")
        b, s = _f2(b), _f2(s)
        xout = _f2(matmul(vm2(s), w["down"], "nn", out_dtype=F32, res=vm2(xin), side=plan.take_fwd(), name=f"ffn_down_f{l}"))
        return xout, (xin, hf, a, b, s)

    ev = plan.weights("ev")
    h0 = rmsnorm_fwd(x0, sp["ev_norm"], name="ev_norm_f")
    z = _f2(matmul(vm2(h0), ev["ev_in"], "nn", out_dtype=F32, side=plan.take_fwd(), name="ev_in_f"))
    cat = pool_fwd(z, ev["pool"], sp["pool_scale"], name="pool_f")
    cat, states = hgrn_fwd(z, cat, sp["lb"], sp["hg_gain"], mix_a, side=plan.take_fwd(), name="hgrn_f")
    x1 = _f2(matmul(VM(cat), ev["ev_out"], "nn", out_dtype=F32, res=vm2(x0), side=plan.take_fwd(), name="ev_out_f"))
    x2, xa0 = xattn_f(0, x1)
    x3, ff0 = ffn_f(0, x2)

    od = plan.weights("od")
    ho = rmsnorm_fwd(x3, sp["od_norm"], name="od_norm_f")
    qkv = matmul(vm2(ho), od["wqkv"], "nn", out_dtype=BF16, out_p=3, side=plan.take_fwd(), name="od_qkv_f")
    zf = _f2(matmul(vm2(ho), od["wf"], "nn", out_dtype=F32, name="od_fl_f"))
    fcum = fl_fwd(zf, sp["bf"], name="od_forget_f")
    nh = d // FOX_HEAD
    nfb = t // _blk(FOX_BLK, t)
    fk = fcum[:, :nh].T.reshape(nh, nfb, 1, t // nfb)
    of, lse = fox_fwd(qkv, fk, side=plan.take_fwd(), name="fox_f")
    x4 = _f2(matmul(vm2(of), od["od_out"], "nn", out_dtype=F32, res=vm2(x3), name="od_out_f"))
    x5, xa1 = xattn_f(1, x4)
    x6, ff1 = ffn_f(1, x5)
    loss, dx, dxb, small["final_norm"] = loss_head(x6, sp["final_norm"], tgt, name="loss_head")

    def ffn_b(l, saved, dx, dxb):
        xin, hf, a, b, s = saved
        w = plan.weights(f"ffn{l}")
        da, db = matmul(vm2(dxb), w["down"], "nt", out_dtype=BF16, epi=(_swiglu_bwd_epi, [vm2(a), vm2(b)], 2), side=plan.take_bwd(1), name=f"ffn_down_bx{l}")
        da, db = _f2(da), _f2(db)
        g_down = rows4(matmul(vm2(s), vm2(dxb), "tn", out_dtype=BF16, name=f"ffn_down_bw{l}"))
        g_gate = matmul(vm2(hf), vm2(da), "tn", out_dtype=BF16, out_p=N_CHIPS, name=f"ffn_gate_bw{l}")
        g_up = matmul(vm2(hf), vm2(db), "tn", out_dtype=BF16, out_p=N_CHIPS, name=f"ffn_up_bw{l}")
        plan.grads_done({f"down{l}": g_down, f"gate{l}": g_gate, f"up{l}": g_up})
        dh = matmul(vm2(da), w["gate"], "nt", out_dtype=F32, side=plan.take_bwd(), name=f"ffn_gate_bx{l}")
        dh = _f2(matmul(vm2(db), w["up"], "nt", out_dtype=BF16, res=VM(dh), side=plan.take_bwd(), name=f"ffn_up_bx{l}"))
        dx, dxb, dg = rmsnorm_bwd(xin, row(sp["ffn_norm"], l), dh, dx, name=f"ffn_norm_b{l}")
        return dx, dxb, dg

    def xattn_b(l, saved, dx, dxb):
        xin, hx, q, mn, kv, o = saved
        w = plan.weights(f"xa{l}")
        do = _f2(matmul(vm2(dxb), w["wo"], "nt", out_dtype=BF16, side=plan.take_bwd(), name=f"xa_o_bx{l}"))
        g_wo = rows4(matmul(vm2(o), vm2(dxb), "tn", out_dtype=BF16, name=f"xa_o_bw{l}"))
        dq, dkv = xattn_bwd(q, kv, do, name=f"xa_attn_b{l}")
        g_wq = rows4(matmul(vm2(hx), vm2(dq), "tn", out_dtype=BF16, name=f"xa_q_bw{l}"))
        dh = _f2(matmul(vm2(dq), w["wq"], "nt", out_dtype=BF16, name=f"xa_q_bx{l}"))
        dkvb = dkv.astype(BF16)
        g_wkv = matmul(vm2(mn), vm2(dkvb), "tn", out_dtype=BF16, out_p=N_CHIPS, name=f"xa_kv_bw{l}")
        plan.grads_done({f"wo{l}": g_wo, f"wq{l}": g_wq, f"wkv{l}": g_wkv})
        dmn = _f2(matmul(vm2(dkvb), w["wkv"], "nt", out_dtype=F32, side=plan.take_bwd(), name=f"xa_kv_bx{l}"))
        (dgm,) = rmsnorm_bwd(mem, row(sp["xa_mem_norm"], l), dmn, None, name=f"xa_memnorm_b{l}")
        dx, dxb, dg = rmsnorm_bwd(xin, row(sp["xa_norm"], l), dh, dx, name=f"xa_norm_b{l}")
        return dx, dxb, dg, dgm

    dg_ffn, dg_xa, dg_mem = [None, None], [None, None], [None, None]
    dx, dxb, dg_ffn[1] = ffn_b(1, ff1, dx, dxb)
    dx, dxb, dg_xa[1], dg_mem[1] = xattn_b(1, xa1, dx, dxb)

    do = _f2(matmul(vm2(dxb), od["od_out"], "nt", out_dtype=BF16, side=plan.take_bwd(), name="od_out_bx"))
    g_od_out = rows4(matmul(vm2(of), vm2(dxb), "tn", out_dtype=BF16, name="od_out_bw"))
    dz3, delta = fox_bwd_dq(qkv, fk, do, lse, side=plan.take_bwd(1), name="fox_bq")
    dz3, dfk = fox_bwd_dkv(qkv, fk, do, lse, delta, dz3, side=plan.take_bwd(1), name="fox_bkv")
    dfc = jnp.pad(dfk.reshape(nh, t).T, ((0, 0), (0, zf.shape[1] - nh)))
    dzf, dbf = fl_bwd(dfc, zf, sp["bf"], name="od_forget_b")
    dqkv = VM(dz3, "cs", pfn=lambda p: lax.rem(p + 2, 3))
    dwqkv = _f2(matmul(vm2(ho), dqkv, "tn", out_dtype=BF16, name="od_qkv_bw"))
    dwf = _f2(matmul(vm2(ho), vm2(dzf), "tn", out_dtype=BF16, name="od_fl_bw"))
    od_in_full = jnp.concatenate([dwqkv, dwf[:, :nh]], axis=1)
    plan.grads_done({"od_out": g_od_out, "od_in": od_in_full.reshape(d, N_CHIPS, -1).transpose(1, 0, 2)})
    dh = matmul(dqkv, od["wqkv"], "nt", out_dtype=F32, side=plan.take_bwd(), name="od_qkv_bx")
    dh = _f2(matmul(vm2(dzf), od["wf"], "nt", out_dtype=BF16, res=VM(dh), name="od_fl_bx"))
    dx, dxb, small["od_norm"] = rmsnorm_bwd(x3, sp["od_norm"], dh, dx, name="od_norm_b")
    small["bf"] = dbf

    dx, dxb, dg_ffn[0] = ffn_b(0, ff0, dx, dxb)
    dx, dxb, dg_xa[0], dg_mem[0] = xattn_b(0, xa0, dx, dxb)

    dcat = matmul(vm2(dxb), ev["ev_out"], "nt", out_dtype=BF16, out_p=2, side=plan.take_bwd(), name="ev_out_bx")
    g_ev_out = rows4(matmul(VM(cat), vm2(dxb), "tn", out_dtype=BF16, name="ev_out_bw"))
    dz5, g_pool, small["pool_scale"] = pool_bwd(z, dcat, ev["pool"], sp["pool_scale"], name="pool_b")
    dz5, small["lb"], dgn = hgrn_bwd(z, dcat, dz5, states, sp["lb"], sp["hg_gain"], mix_a, side=plan.take_bwd(1), name="hgrn_b")
    small["hg_gain"] = jnp.sum(dgn, axis=0)
    dzv = VM(dz5, "cs", pfn=lambda p: lax.rem(p + 4, 5))
    g_ev_in = _f2(matmul(vm2(h0), dzv, "tn", out_dtype=BF16, side=plan.take_bwd(1), name="ev_in_bw"))
    g_ev_in = g_ev_in.reshape(d, N_CHIPS, -1).transpose(1, 0, 2)
    ng, gsz = g_pool.shape[0], g_pool.shape[1]
    pool_parts = g_pool.reshape(ng, N_CHIPS, gsz // N_CHIPS, gsz).transpose(1, 0, 2, 3).reshape(N_CHIPS, gsz, gsz).astype(BF16)
    plan.grads_done({"ev_out": g_ev_out, "pool": pool_parts, "ev_in": g_ev_in}, now=True)
    dh = _f2(matmul(dzv, ev["ev_in"], "nt", out_dtype=BF16, side=plan.take_bwd(1), name="ev_in_bx"))
    dx, _, small["ev_norm"] = rmsnorm_bwd(x0, sp["ev_norm"], dh, dx, name="ev_norm_b")

    small["xa_norm"] = jnp.concatenate(dg_xa, axis=0)
    small["xa_mem_norm"] = jnp.concatenate(dg_mem, axis=0)
    small["ffn_norm"] = jnp.concatenate(dg_ffn, axis=0)
    return loss, dx, small


def gather_forward(fulls, shards, *, name):
    n = len(fulls)
    side = ForwardSide(fulls, shards)

    def body(*refs):
        ins, outs = refs[:2 * n], refs[2 * n:3 * n]
        ssem, rsem = refs[3 * n:]
        side.start(ins, outs, ssem, rsem)
        side.finish(ins, outs, ssem, rsem)

    return pl.pallas_call(
        body, name=name, in_specs=[_ANY] * (2 * n), out_specs=[_ANY] * n, out_shape=side.out_shape,
        input_output_aliases=side.aliases,
        scratch_shapes=[pltpu.SemaphoreType.DMA((n, 4)), pltpu.SemaphoreType.DMA((n, 4))],
    )(*side.inputs)


def gather_shards(shards, *, name):
    n = len(shards)

    def body(*refs):
        ins, outs = refs[:n], refs[n:2 * n]
        ssem, rsem = refs[2 * n:]
        x, y, c, chips = _me()
        mine = _chip_id((x, y))
        sibling = (x, y, 1 - c)

        def rows(w, chip_id, which):
            h = shards[w].shape[0] // 2
            return outs[w].at[chip_id, pl.ds(which * h, h)]

        def to_chip(w, j):
            h = shards[w].shape[0] // 2
            return _rcopy(ins[w].at[pl.ds(c * h, h)], rows(w, mine, c), ssem.at[w, j], rsem.at[w, j], (*chips[j], c))

        def from_chip(w, j):
            r = rows(w, _chip_id(chips[j]), c)
            return _rcopy(r, r, ssem.at[w, j], rsem.at[w, j], (*chips[j], c))

        def to_sibling(w, j):
            r = rows(w, _chip_id(chips[j]), c)
            return _rcopy(r, r, ssem.at[w, 3 + j], rsem.at[w, 3 + j], sibling)

        def from_sibling(w, j):
            r = rows(w, _chip_id(chips[j]), 1 - c)
            return _rcopy(r, r, ssem.at[w, 3 + j], rsem.at[w, 3 + j], sibling)

        def own(w):
            return _rcopy(ins[w], outs[w].at[mine], ssem.at[w, 6], rsem.at[w, 6], sibling)

        for w in range(n):
            own(w).start()
            for j in range(3):
                to_chip(w, j).start()
        for w in range(n):
            for j in range(3):
                from_chip(w, j).wait_recv()
                to_sibling(w, j).start()
        for w in range(n):
            for j in range(3):
                from_sibling(w, j).wait_recv()
        for w in range(n):
            own(w).wait()
            for j in range(3):
                to_chip(w, j).wait_send()
                to_sibling(w, j).wait_send()

    return pl.pallas_call(
        body, name=name, in_specs=[_ANY] * n, out_specs=[_ANY] * n,
        out_shape=[jax.ShapeDtypeStruct((N_CHIPS,) + s.shape, s.dtype) for s in shards],
        scratch_shapes=[pltpu.SemaphoreType.DMA((n, 7)), pltpu.SemaphoreType.DMA((n, 7))],
    )(*shards)


def _ids_spec(grid, in_specs, out_specs):
    return pltpu.PrefetchScalarGridSpec(num_scalar_prefetch=1, grid=grid, in_specs=in_specs, out_specs=out_specs)


def rs_pair(parts, *, name):
    n = len(parts)

    def body(*refs):
        ins, recv = refs[:n], refs[n:2 * n]
        ssem, rsem = refs[2 * n:]
        x, y, c, _ = _me()
        sibling = (x, y, 1 - c)

        def swap(w):
            h = parts[w].shape[1] // 2
            return _rcopy(ins[w].at[:, pl.ds((1 - c) * h, h), :], recv[w], ssem.at[w], rsem.at[w], sibling)

        for w in range(n):
            swap(w).start()
        for w in range(n):
            swap(w).wait()

    return pl.pallas_call(
        body, name=name, in_specs=[_ANY] * n, out_specs=[_ANY] * n,
        out_shape=[jax.ShapeDtypeStruct((p.shape[0], p.shape[1] // 2, p.shape[2]), p.dtype) for p in parts],
        scratch_shapes=[pltpu.SemaphoreType.DMA((n,)), pltpu.SemaphoreType.DMA((n,))],
    )(*parts)


def add_pair(part, recv, ids, *, name):
    p, h, c = recv.shape
    br = _row_blk(h, 512)
    nb = h // br

    def body(ids_ref, a_ref, b_ref, o_ref):
        del ids_ref
        o_ref[...] = (a_ref[...].astype(F32) + b_ref[...].astype(F32)).astype(o_ref.dtype)

    half = pl.BlockSpec((None, br, c), lambda k, i, ids: (k, i, 0))
    return pl.pallas_call(
        body, name=name, out_shape=jax.ShapeDtypeStruct(recv.shape, recv.dtype),
        grid_spec=_ids_spec((p, nb), [pl.BlockSpec((None, br, c), lambda k, i, ids: (k, ids[1] * nb + i, 0)), half], half),
        compiler_params=_params(("parallel", "parallel")),
    )(ids, part, recv)


def rs_chip(sums, *, name):
    n = len(sums)

    def body(*refs):
        ins, outs = refs[:n], refs[n:2 * n]
        ssem, rsem = refs[2 * n:]
        x, y, c, chips = _me()

        def swap(w, j):
            return _rcopy(ins[w].at[_chip_id(chips[j])], outs[w].at[j], ssem.at[w, j], rsem.at[w, j], (*chips[j], c))

        for w in range(n):
            for j in range(3):
                swap(w, j).start()
        for w in range(n):
            for j in range(3):
                swap(w, j).wait()

    return pl.pallas_call(
        body, name=name, in_specs=[_ANY] * n, out_specs=[_ANY] * n,
        out_shape=[jax.ShapeDtypeStruct((3,) + s.shape[1:], s.dtype) for s in sums],
        scratch_shapes=[pltpu.SemaphoreType.DMA((n, 3)), pltpu.SemaphoreType.DMA((n, 3))],
    )(*sums)


def add_chips(sums, landed, ids, group, layer, group_shape, *, name):
    _, h, c = sums.shape
    br = _row_blk(h, 256)
    nb = h // br

    def body(ids_ref, a_ref, b_ref, *rest):
        o_ref = rest[-1]
        tot = a_ref[...].astype(F32)
        for k in range(3):
            tot = tot + b_ref[k].astype(F32)
        o_ref[...] = tot

    in_specs = [pl.BlockSpec((None, br, c), lambda i, ids: (ids[0], i, 0)), pl.BlockSpec((3, br, c), lambda i, ids: (0, i, 0))]
    args = [ids, sums, landed]
    if group is not None:
        in_specs.append(_ANY)
        args.append(group)
    return pl.pallas_call(
        body, name=name, out_shape=jax.ShapeDtypeStruct(group_shape, F32),
        input_output_aliases={3: 0} if group is not None else {},
        grid_spec=_ids_spec((nb,), in_specs, pl.BlockSpec((None, br, c), lambda i, ids: (layer, ids[1] * nb + i, 0))),
        compiler_params=_params(("parallel",)),
    )(*args)


def rs_share(groups, slots, *, name):
    ng = len(groups)
    n = len(slots)

    def body(*refs):
        outs = refs[ng:2 * ng]
        ssem, rsem = refs[2 * ng:]
        x, y, c, _ = _me()
        sibling = (x, y, 1 - c)

        def rows(w, which):
            g, l = slots[w]
            h = groups[g].shape[1] // 2
            return outs[g].at[l, pl.ds(which * h, h), :]

        def swap(w):
            return _rcopy(rows(w, c), rows(w, c), ssem.at[w], rsem.at[w], sibling)

        for w in range(n):
            swap(w).start()
        for w in range(n):
            swap(w).wait_send()
            _rcopy(rows(w, 1 - c), rows(w, 1 - c), ssem.at[w], rsem.at[w], sibling).wait_recv()

    return pl.pallas_call(
        body, name=name, in_specs=[_ANY] * ng, out_specs=[_ANY] * ng,
        out_shape=[jax.ShapeDtypeStruct(g.shape, g.dtype) for g in groups],
        input_output_aliases={g: g for g in range(ng)},
        scratch_shapes=[pltpu.SemaphoreType.DMA((n,)), pltpu.SemaphoreType.DMA((n,))],
    )(*groups)


def allreduce_small(v, *, name):
    r, c = v.shape
    ndev = 2 * N_CHIPS

    def body(v_ref, o_ref, buf, ssem, rsem):
        x, y, cc, _ = _me()
        me = 4 * x + 2 * y + cc
        flips = [(a, b, d) for a in (0, 1) for b in (0, 1) for d in (0, 1)][1:]
        buf[me] = v_ref[...]
        cps = []
        for k, (a, b, d) in enumerate(flips):
            peer = (jnp.bitwise_xor(x, a), jnp.bitwise_xor(y, b), jnp.bitwise_xor(cc, d))
            cp = _rcopy(v_ref, buf.at[me], ssem.at[k], rsem.at[k], peer)
            cp.start()
            cps.append(cp)
        for k, (a, b, d) in enumerate(flips):
            peer = (jnp.bitwise_xor(x, a), jnp.bitwise_xor(y, b), jnp.bitwise_xor(cc, d))
            src = 4 * peer[0] + 2 * peer[1] + peer[2]
            _rcopy(v_ref, buf.at[src], ssem.at[k], rsem.at[k], peer).wait_recv()
        for cp in cps:
            cp.wait_send()
        tot = buf[0]
        for k in range(1, ndev):
            tot = tot + buf[k]
        o_ref[...] = tot

    vm = pl.BlockSpec(memory_space=pltpu.VMEM)
    return pl.pallas_call(
        body, name=name, in_specs=[vm], out_specs=vm, out_shape=jax.ShapeDtypeStruct((r, c), F32),
        scratch_shapes=[pltpu.VMEM((ndev, r, c), F32), pltpu.SemaphoreType.DMA((ndev - 1,)), pltpu.SemaphoreType.DMA((ndev - 1,))],
    )(v)


WEIGHTS = ["lb_table", "ev_norm", "ev_w_in", "ev_w_pool", "ev_pool_scale", "ev_hg_norm", "ev_w_out", "od_norm", "od_w_in",
           "od_b_f", "od_w_out", "xa_norm", "xa_mem_norm", "xa_wq", "xa_wkv", "xa_wo", "ffn_norm", "ffn_w_gate", "ffn_w_up",
           "ffn_w_down", "final_norm"]
BIG = ["ev_w_in", "ev_w_pool", "ev_w_out", "od_w_in", "od_w_out", "xa_wq", "xa_wkv", "xa_wo", "ffn_w_gate", "ffn_w_up", "ffn_w_down"]
SMALL_ROWS = 16


def _rows(parts, width):
    rows = [jnp.pad(p.reshape(-1, p.shape[-1]).astype(F32), ((0, 0), (0, width - p.shape[-1]))) for p in parts]
    out = jnp.concatenate(rows, axis=0)
    return jnp.pad(out, ((0, SMALL_ROWS - out.shape[0]), (0, 0)))


def _unrows(packed, like):
    out, r = [], 0
    for p in like:
        n = p.size // p.shape[-1]
        out.append(packed[r:r + n, :p.shape[-1]].reshape(p.shape))
        r += n
    return out


def _m3(a):
    return a.reshape(a.shape[0], -1, a.shape[-1])


SLOT = {"ev_in": ("ev_w_in", 0), "pool": ("ev_w_pool", 0), "ev_out": ("ev_w_out", 0), "od_in": ("od_w_in", 0),
        "od_out": ("od_w_out", 0)}
for _l in range(2):
    SLOT.update({f"wq{_l}": ("xa_wq", _l), f"wkv{_l}": ("xa_wkv", _l), f"wo{_l}": ("xa_wo", _l),
                 f"gate{_l}": ("ffn_w_gate", _l), f"up{_l}": ("ffn_w_up", _l), f"down{_l}": ("ffn_w_down", _l)})
GATHER_FIRST = ["ev_in", "ev_out", "pool", "od_norm"]
GATHER_CARRIED = [["wq0", "wo0"], ["wkv0", "gate0"], ["od_out"], ["wq1"], ["wo1"], ["up0"], ["down0"], ["od_in"], ["wkv1"],
                  ["gate1", "up1", "down1"]]


class _Lazy:
    def __init__(self, plan, group):
        self.plan, self.layer = plan, group[-1] if group[-1] in "01" else ""

    def __getitem__(self, key):
        return self.plan.w(key + self.layer if key in ("wq", "wo", "wkv", "gate", "up", "down") else key)


class _Plan:
    def __init__(self, shards, ids, group_shapes, d, nh):
        self.shards, self.ids, self.group_shapes, self.d, self.nh = shards, ids, group_shapes, d, nh
        self.full, self.cache = {}, {}
        self.queue, self.sides, self.fsides, self.forwarded = [list(u) for u in GATHER_CARRIED], [], [], set()
        self.parts, self.psides, self.sums, self.rqueue, self.rsides = [], [], {}, [], []
        got = gather_shards([shards[n] for n in GATHER_FIRST], name="gather_first")
        for n, f in zip(GATHER_FIRST, got):
            self.full[n] = f

    def take_fwd(self):
        parts = []
        ready = [(ns, s) for ns, s in self.sides if s.outs is not None and ns[0] not in self.forwarded]
        for ns, s in ready:
            fs = ForwardSide(s.outs, [self.shards[n] for n in ns])
            self.fsides.append((ns, fs))
            self.forwarded.update(ns)
            parts.append(fs)
        if self.queue:
            names = self.queue.pop(0)
            side = GatherSide([self.shards[n] for n in names])
            self.sides.append((names, side))
            parts.append(side)
        return Sides(parts) if parts else None

    def _need(self, names):
        missing = [n for n in names if n not in self.full]
        if not missing:
            return
        done = {n: a for ns, s in self.fsides if s.outs is not None for n, a in zip(ns, s.outs)}
        landed = {n: a for ns, s in self.sides if s.outs is not None for n, a in zip(ns, s.outs)}
        pre = {n: done[n] for n in missing if n in done}
        half = [n for n in missing if n not in done and n in landed]
        late = [n for n in missing if n not in done and n not in landed]
        if half:
            self.forwarded.update(half)
            pre.update(zip(half, gather_forward([landed[n] for n in half], [self.shards[n] for n in half],
                                                name=f"gather_forward_{half[0]}")))
        if late:
            self.queue = [u for u in ([n for n in u if n not in late] for u in self.queue) if u]
            pre.update(zip(late, gather_shards([self.shards[n] for n in late], name=f"gather_late_{late[0]}")))
        for n in missing:
            self.full[n] = pre[n]

    def w(self, name):
        if name in self.cache:
            return self.cache[name]
        if name in ("wqkv", "wf"):
            self._need(["od_in"])
            od_full = self.full["od_in"].transpose(1, 0, 2).reshape(self.d, -1)
            self.cache["wqkv"] = vm2(od_full[:, :3 * self.d])
            self.cache["wf"] = vm2(jnp.pad(od_full[:, 3 * self.d:], ((0, 0), (0, 128 - self.nh))))
            return self.cache[name]
        self._need([name])
        f = self.full[name]
        if name == "pool":
            rows, gsz = f.shape[1:]
            ng = rows * N_CHIPS // gsz
            out = f.reshape(N_CHIPS, ng, gsz // N_CHIPS, gsz).transpose(1, 0, 2, 3).reshape(ng, gsz, gsz)
        elif name == "ev_in":
            out = vm2(f.transpose(1, 0, 2).reshape(self.d, -1))
        else:
            out = VM(f, "cs") if name.rstrip("01") in ("wkv", "gate", "up") else vm2(f.reshape(-1, f.shape[-1]))
        self.cache[name] = out
        return out

    def weights(self, group):
        return _Lazy(self, group)

    def grads_done(self, parts, now=False):
        names = list(parts)
        if now:
            got = rs_pair([parts[n] for n in names], name=f"reduce_pair_{names[0]}")
            for n, g in zip(names, got):
                self.sums[n] = add_pair(parts[n], g, self.ids, name=f"reduce_add2_{n}")
            self.rqueue.append(names)
        else:
            self.parts.append((names, [parts[n] for n in names]))

    def _add_swapped(self):
        for names, parts, side in self.psides:
            if side.outs is not None and names[0] not in self.sums:
                for n, p, g in zip(names, parts, side.outs):
                    self.sums[n] = add_pair(p, g, self.ids, name=f"reduce_add2_{n}")
                self.rqueue.append(names)

    def take_bwd(self, units=0):
        self._add_swapped()
        sides = []
        for names, parts in self.parts:
            ps = PairSide(parts)
            self.psides.append((names, parts, ps))
            sides.append(ps)
        self.parts = []
        names = [n for u in self.rqueue[:units] for n in u]
        self.rqueue = self.rqueue[units:]
        if names:
            rs = ReduceSide([self.sums[n] for n in names])
            self.rsides.append((names, rs))
            sides.append(rs)
        return Sides(sides) if sides else None

    def finish(self):
        for names, parts in self.parts:
            self.grads_done(dict(zip(names, parts)), now=True)
        self._add_swapped()
        landed = {}
        for ns, side in self.rsides:
            landed.update(zip(ns, side.outs))
        rest = [n for u in self.rqueue for n in u]
        if rest:
            landed.update(zip(rest, rs_chip([self.sums[n] for n in rest], name="reduce_chips_rest")))
        gbig = {n: None for n in BIG}
        for n, (big, l) in SLOT.items():
            gbig[big] = add_chips(self.sums[n], landed[n], self.ids, gbig[big], l, self.group_shapes[big], name=f"reduce_add4_{n}")
        return gbig


def kernel(x, mem, lb_table, ev_norm, ev_w_in, ev_w_pool, ev_pool_scale, ev_hg_norm, ev_w_out, od_norm, od_w_in, od_b_f, od_w_out, xa_norm, xa_mem_norm, xa_wq, xa_wkv, xa_wo, ffn_norm, ffn_w_gate, ffn_w_up, ffn_w_down, final_norm, loss_target, m_lb_table, m_ev_norm, m_ev_w_in, m_ev_w_pool, m_ev_pool_scale, m_ev_hg_norm, m_ev_w_out, m_od_norm, m_od_w_in, m_od_b_f, m_od_w_out, m_xa_norm, m_xa_mem_norm, m_xa_wq, m_xa_wkv, m_xa_wo, m_ffn_norm, m_ffn_w_gate, m_ffn_w_up, m_ffn_w_down, m_final_norm, v_lb_table, v_ev_norm, v_ev_w_in, v_ev_w_pool, v_ev_pool_scale, v_ev_hg_norm, v_ev_w_out, v_od_norm, v_od_w_in, v_od_b_f, v_od_w_out, v_xa_norm, v_xa_mem_norm, v_xa_wq, v_xa_wkv, v_xa_wo, v_ffn_norm, v_ffn_w_gate, v_ffn_w_up, v_ffn_w_down, v_final_norm):
    a = dict(locals())
    w = {n: a[n] for n in WEIGHTS}
    mom = {n: a["m_" + n] for n in WEIGHTS}
    var = {n: a["v_" + n] for n in WEIGHTS}
    _, t, d = x.shape
    nh = d // FOX_HEAD
    lanes = 128
    cx, cy = lax.axis_index("x"), lax.axis_index("y")
    chip = 2 * cx + cy

    w3 = {n: _m3(w[n]) for n in BIG}
    shards = {"od_norm": jnp.broadcast_to(od_norm, (16, od_norm.shape[1]))}
    for name, (big, l) in SLOT.items():
        shards[name] = w3[big][l].astype(BF16)
    ids = jnp.stack([chip, lax.axis_index("c")]).astype(jnp.int32)
    plan = _Plan(shards, ids, {n: w3[n].shape for n in BIG}, d, nh)
    od_norm_full = plan.full["od_norm"][:, 0, :].reshape(1, d)

    sm = jax.nn.softmax(lb_table, axis=0)
    sp = {
        "lb": sm[1:2], "ev_norm": ev_norm, "pool_scale": ev_pool_scale, "hg_gain": ev_hg_norm, "od_norm": od_norm_full,
        "bf": jnp.pad(od_b_f, ((0, 0), (0, lanes - nh))), "xa_norm": xa_norm, "xa_mem_norm": xa_mem_norm, "ffn_norm": ffn_norm,
        "final_norm": final_norm.reshape(1, d),
    }
    loss_l, gx, small = _local_step(x[0], mem[0], loss_target[0], sp, plan)
    loss = lax.psum(loss_l[0, 0], ("x", "y", "c"))
    gbig = plan.finish()

    raw_like = [small["lb"], small["ev_norm"], small["pool_scale"], small["hg_gain"], small["od_norm"], small["bf"],
                small["xa_norm"], small["xa_mem_norm"], small["ffn_norm"], small["final_norm"]]
    summed = _unrows(allreduce_small(_rows(raw_like, d), name="reduce_small"), raw_like)
    dlb, g_ev_norm, g_pool_scale, g_hg, g_od_norm_full, g_bf, g_xa, g_xam, g_ffn, g_final = summed
    dsm = jnp.zeros_like(sm).at[1:2].set(dlb)
    gsmall = {
        "lb_table": sm * (dsm - jnp.sum(sm * dsm, axis=0, keepdims=True)), "ev_norm": g_ev_norm, "ev_pool_scale": g_pool_scale,
        "ev_hg_norm": g_hg, "od_norm": lax.dynamic_slice_in_dim(g_od_norm_full, chip * od_norm.shape[1], od_norm.shape[1], axis=1),
        "od_b_f": g_bf[:, :nh], "xa_norm": g_xa, "xa_mem_norm": g_xam, "ffn_norm": g_ffn, "final_norm": g_final.reshape(d),
    }

    grad, delta, new_m, new_v = {}, {}, {}, {}
    order = sorted(BIG, key=lambda n: w3[n].size)
    first = order[0]
    whole = {first: rs_share([gbig[first]], [(0, l) for l in range(w3[first].shape[0])], name="reduce_share_first")[0]}
    for k, n in enumerate(order):
        side = ShareSide(gbig[order[k + 1]]) if k + 1 < len(order) else None
        res = adamw(w3[n], whole[n], _m3(mom[n]), _m3(var[n]), side=side, name=f"adamw_{n}")
        if side is not None:
            whole[order[k + 1]] = side.outs[0]
        grad[n], delta[n], new_m[n], new_v[n] = [r.reshape(w[n].shape) for r in res]
    snames = [n for n in WEIGHTS if n not in BIG]
    like = [w[n] for n in snames]
    res = adamw(_rows(like, d)[None], _rows([gsmall[n] for n in snames], d)[None], _rows([mom[n] for n in snames], d)[None],
                _rows([var[n] for n in snames], d)[None], name="adamw_small")
    res = [r[0] for r in res]
    for vals, dst in zip(res, (grad, delta, new_m, new_v)):
        dst.update(zip(snames, _unrows(vals, like)))
    return (loss, gx.reshape(x.shape), *[grad[n] for n in WEIGHTS], *[delta[n] for n in WEIGHTS],
            *[new_m[n] for n in WEIGHTS], *[new_v[n] for n in WEIGHTS])
```

```python
import functools
import math

import jax
import jax.numpy as jnp
from jax import lax
from jax.experimental import pallas as pl
from jax.experimental.pallas import tpu as pltpu

F32 = jnp.float32
BF16 = jnp.bfloat16
MESH = pl.DeviceIdType.MESH

V7X_VMEM_LIMIT_BYTES = 56 * 1024 * 1024
N_CHIPS = 4

EPS = 1e-6
POOL_WINDOWS = (2, 4, 8, 16)
POOL_HALO = 16
HG_HEAD = 128
HG_CHUNK = 64
FOX_HEAD = 128
FOX_BLK = 512
XA_HEADS = 4
ADAM_LR, ADAM_B1, ADAM_B2, ADAM_EPS, ADAM_WD, ADAM_STEP = 0.001, 0.9, 0.999, 1e-08, 0.01, 10
EXP_CLAMP = 80.0


def _params(sem=None):
    return pltpu.CompilerParams(dimension_semantics=sem, vmem_limit_bytes=V7X_VMEM_LIMIT_BYTES)


def _blk(pref, dim):
    b = min(pref, dim)
    assert dim % b == 0, (pref, dim)
    return b


class VM:
    def __init__(self, arr, kind="cs", lead=(), inner=(), pfn=None):
        self.arr, self.kind, self.lead, self.inner = arr, kind, tuple(lead), tuple(inner)
        self.pfn = pfn or (lambda p: p)
        p = arr.shape[len(self.lead)]
        r, c = arr.shape[-2:]
        assert arr.ndim == len(self.lead) + 1 + len(self.inner) + 2, (arr.shape, lead, inner)
        self.P = p
        self.shape = (r, c * p) if kind == "cs" else (r * p, c)
        self.dtype = arr.dtype

    def spec(self, br, bc, rfn, cfn):
        p = self.P
        r, c = self.arr.shape[-2:]
        assert c % bc == 0 and r % br == 0, (self.arr.shape, br, bc)
        if p == 1:
            def imap(*g):
                return (*self.lead, self.pfn(0), *self.inner, rfn(*g), cfn(*g))
        elif self.kind == "cs":
            per = c // bc

            def imap(*g):
                cb = cfn(*g)
                return (*self.lead, self.pfn(lax.div(cb, per)), *self.inner, rfn(*g), lax.rem(cb, per))
        else:
            per = r // br

            def imap(*g):
                rb = rfn(*g)
                return (*self.lead, self.pfn(lax.div(rb, per)), *self.inner, lax.rem(rb, per), cfn(*g))
        return pl.BlockSpec((None,) * (self.arr.ndim - 2) + (br, bc), imap)


def vm2(arr):
    return VM(arr.reshape((1,) + arr.shape))


def _out_struct(shape, kind, p, dtype):
    r, c = shape
    return jax.ShapeDtypeStruct((p, r, c // p) if kind == "cs" else (p, r // p, c), dtype)


_ANY = pl.BlockSpec(memory_space=pl.ANY)


def _me():
    x, y, c = lax.axis_index("x"), lax.axis_index("y"), lax.axis_index("c")
    chips = [(1 - x, y), (x, 1 - y), (1 - x, 1 - y)]
    return x, y, c, chips


def _chip_id(xy):
    return 2 * xy[0] + xy[1]


def _rcopy(src, dst, ssem, rsem, dev):
    return pltpu.make_async_remote_copy(src_ref=src, dst_ref=dst, send_sem=ssem, recv_sem=rsem, device_id=dev,
                                        device_id_type=MESH)


class GatherSide:
    def __init__(self, shards):
        self.inputs = list(shards)
        self.out_shape = [jax.ShapeDtypeStruct((N_CHIPS,) + s.shape, s.dtype) for s in shards]
        self.aliases = {}
        self.rows = len(shards)
        self.outs = None

    def _copy(self, ins, outs, ssem, rsem, w, j, receive):
        x, y, c, chips = _me()
        h = self.inputs[w].shape[0] // 2
        half = pl.ds(c * h, h)
        if receive:
            r = outs[w].at[_chip_id(chips[j]), half]
            return _rcopy(r, r, ssem.at[w, j], rsem.at[w, j], (*chips[j], c))
        return _rcopy(ins[w].at[half], outs[w].at[_chip_id((x, y)), half], ssem.at[w, j], rsem.at[w, j], (*chips[j], c))

    def start(self, ins, outs, ssem, rsem):
        for w in range(len(self.inputs)):
            for j in range(3):
                self._copy(ins, outs, ssem, rsem, w, j, False).start()

    def finish(self, ins, outs, ssem, rsem):
        for w in range(len(self.inputs)):
            for j in range(3):
                self._copy(ins, outs, ssem, rsem, w, j, True).wait_recv()
                self._copy(ins, outs, ssem, rsem, w, j, False).wait_send()


class ForwardSide:
    def __init__(self, fulls, shards):
        self.inputs = list(fulls) + list(shards)
        self.out_shape = [jax.ShapeDtypeStruct(f.shape, f.dtype) for f in fulls]
        self.aliases = {w: w for w in range(len(fulls))}
        self.rows = len(fulls)
        self.outs = None

    def _copy(self, outs, ssem, rsem, w, j, receive):
        x, y, c, chips = _me()
        h = self.inputs[w].shape[1] // 2
        r = outs[w].at[_chip_id(chips[j]), pl.ds(((1 - c) if receive else c) * h, h)]
        return _rcopy(r, r, ssem.at[w, j], rsem.at[w, j], (x, y, 1 - c))

    def _own(self, ins, outs, ssem, rsem, w):
        x, y, c, _ = _me()
        return _rcopy(ins[self.rows + w], outs[w].at[_chip_id((x, y))], ssem.at[w, 3], rsem.at[w, 3], (x, y, 1 - c))

    def start(self, ins, outs, ssem, rsem):
        for w in range(self.rows):
            self._own(ins, outs, ssem, rsem, w).start()
            for j in range(3):
                self._copy(outs, ssem, rsem, w, j, False).start()

    def finish(self, ins, outs, ssem, rsem):
        for w in range(self.rows):
            self._own(ins, outs, ssem, rsem, w).wait()
            for j in range(3):
                self._copy(outs, ssem, rsem, w, j, False).wait_send()
                self._copy(outs, ssem, rsem, w, j, True).wait_recv()


class PairSide:
    def __init__(self, parts):
        self.inputs = list(parts)
        self.out_shape = [jax.ShapeDtypeStruct((p.shape[0], p.shape[1] // 2, p.shape[2]), p.dtype) for p in parts]
        self.aliases = {}
        self.rows = len(parts)
        self.outs = None

    def _copy(self, ins, outs, ssem, rsem, w):
        x, y, c, _ = _me()
        h = self.inputs[w].shape[1] // 2
        return _rcopy(ins[w].at[:, pl.ds((1 - c) * h, h), :], outs[w], ssem.at[w, 0], rsem.at[w, 0], (x, y, 1 - c))

    def start(self, ins, outs, ssem, rsem):
        for w in range(self.rows):
            self._copy(ins, outs, ssem, rsem, w).start()

    def finish(self, ins, outs, ssem, rsem):
        for w in range(self.rows):
            self._copy(ins, outs, ssem, rsem, w).wait()


class _SemRows:
    def __init__(self, sem, off):
        self.sem, self.off = sem, off

    @property
    def at(self):
        return self

    def __getitem__(self, idx):
        return self.sem.at[self.off + idx[0], idx[1]]


class Sides:
    def __init__(self, sides):
        self.sides = list(sides)
        self.inputs = [a for s in self.sides for a in s.inputs]
        self.out_shape = [o for s in self.sides for o in s.out_shape]
        self.rows = sum(s.rows for s in self.sides)
        self.aliases, i0, o0 = {}, 0, 0
        for s in self.sides:
            self.aliases.update({i0 + i: o0 + o for i, o in s.aliases.items()})
            i0, o0 = i0 + len(s.inputs), o0 + len(s.out_shape)

    def _each(self, method, ins, outs, ssem, rsem):
        i0 = o0 = r0 = 0
        for s in self.sides:
            getattr(s, method)(ins[i0:i0 + len(s.inputs)], outs[o0:o0 + len(s.out_shape)], _SemRows(ssem, r0), _SemRows(rsem, r0))
            i0, o0, r0 = i0 + len(s.inputs), o0 + len(s.out_shape), r0 + s.rows

    def start(self, ins, outs, ssem, rsem):
        self._each("start", ins, outs, ssem, rsem)

    def finish(self, ins, outs, ssem, rsem):
        self._each("finish", ins, outs, ssem, rsem)

    @property
    def outs(self):
        return None

    @outs.setter
    def outs(self, vals):
        o0 = 0
        for s in self.sides:
            s.outs = list(vals[o0:o0 + len(s.out_shape)])
            o0 += len(s.out_shape)


class ReduceSide:
    def __init__(self, sums):
        self.inputs = list(sums)
        self.out_shape = [jax.ShapeDtypeStruct((3,) + s.shape[1:], s.dtype) for s in sums]
        self.aliases = {}
        self.rows = len(sums)
        self.outs = None

    def _copy(self, ins, outs, ssem, rsem, w, j):
        _, _, c, chips = _me()
        return _rcopy(ins[w].at[_chip_id(chips[j])], outs[w].at[j], ssem.at[w, j], rsem.at[w, j], (*chips[j], c))

    def start(self, ins, outs, ssem, rsem):
        for w in range(len(self.inputs)):
            for j in range(3):
                self._copy(ins, outs, ssem, rsem, w, j).start()

    def finish(self, ins, outs, ssem, rsem):
        for w in range(len(self.inputs)):
            for j in range(3):
                self._copy(ins, outs, ssem, rsem, w, j).wait()


def _call(body, side, *, name, grid, in_specs, out_specs, out_shape, scratch_shapes=(), sem, aliases=None, args):
    if side is None:
        return pl.pallas_call(body, name=name, grid=grid, in_specs=in_specs, out_specs=out_specs, out_shape=out_shape,
                              scratch_shapes=list(scratch_shapes), input_output_aliases=aliases or {},
                              compiler_params=_params(sem))(*args)
    single = not isinstance(out_shape, (list, tuple))
    oshape, ospecs = ([out_shape], [out_specs]) if single else (list(out_shape), list(out_specs))
    n_in, n_out, s_in, s_out = len(in_specs), len(oshape), len(side.inputs), len(side.out_shape)

    def wrapped(*refs):
        ins, sin = refs[:n_in], refs[n_in:n_in + s_in]
        outs = refs[n_in + s_in:n_in + s_in + n_out]
        souts = refs[n_in + s_in + n_out:n_in + s_in + n_out + s_out]
        rest = refs[n_in + s_in + n_out + s_out:]
        scratch, (ssem, rsem) = rest[:-2], rest[-2:]
        first = functools.reduce(jnp.logical_and, [pl.program_id(a) == 0 for a in range(len(grid))])
        last = functools.reduce(jnp.logical_and, [pl.program_id(a) == grid[a] - 1 for a in range(len(grid))])

        @pl.when(first)
        def _():
            side.start(sin, souts, ssem, rsem)

        body(*ins, *outs, *scratch)

        @pl.when(last)
        def _():
            side.finish(sin, souts, ssem, rsem)

    sems = pltpu.SemaphoreType.DMA((side.rows, 4))
    res = pl.pallas_call(
        wrapped, name=name, grid=grid, in_specs=list(in_specs) + [_ANY] * s_in, out_specs=ospecs + [_ANY] * s_out,
        out_shape=oshape + side.out_shape, scratch_shapes=list(scratch_shapes) + [sems, sems],
        input_output_aliases={**(aliases or {}), **{n_in + i: n_out + o for i, o in side.aliases.items()}},
        compiler_params=_params(("arbitrary",) * len(grid)),
    )(*args, *side.inputs)
    side.outs = list(res[n_out:])
    return res[0] if single else list(res[:n_out])


def _best(g, cap):
    if g <= cap:
        return g
    cands = [d for d in range(128, cap + 1, 128) if g % d == 0]
    assert cands, (g, cap)
    return cands[-1]


def _row_blk(n, cap):
    cands = [d for d in range(16, min(n, cap) + 1, 16) if n % d == 0]
    assert cands, (n, cap)
    return cands[-1]


def _tiles(a, b, mode, out_kind, out_p, bm, bn, bk):
    def cpiece(v):
        return v.arr.shape[-1] if v.kind == "cs" else v.shape[1]

    def rpiece(v):
        return v.arr.shape[-2] if v.kind == "rs" else v.shape[0]

    if mode == "nn":
        m, n = a.shape[0], b.shape[1]
        gm, gn, gk = rpiece(a), cpiece(b), math.gcd(cpiece(a), rpiece(b))
    elif mode == "nt":
        m, n = a.shape[0], b.shape[0]
        gm, gn, gk = rpiece(a), rpiece(b), math.gcd(cpiece(a), cpiece(b))
    else:
        m, n = a.shape[1], b.shape[1]
        gm, gn, gk = cpiece(a), cpiece(b), math.gcd(rpiece(a), rpiece(b))
    if out_kind == "cs":
        gn = math.gcd(gn, n // out_p)
    else:
        gm = math.gcd(gm, m // out_p)
    caps = {"nn": (1024, 1536, 2048), "nt": (512, 2048, 2048), "tn": (1536, 1536, 2048)}[mode]
    return (bm or _best(gm, caps[0])), (bn or _best(gn, caps[1])), (bk or _best(gk, caps[2]))


def matmul(a, b, mode, *, out_dtype, bm=None, bn=None, bk=None, out_kind="cs", out_p=1, out_pfn=None, res=None, epi=None,
           side=None, name):
    bm, bn, bk = _tiles(a, b, mode, out_kind, out_p, bm, bn, bk)
    if mode == "nn":
        (m, k), (k2, n) = a.shape, b.shape
        a_spec = a.spec(bm, bk, lambda i, j, kk: i, lambda i, j, kk: kk)
        b_spec = b.spec(bk, bn, lambda i, j, kk: kk, lambda i, j, kk: j)
        dims = (((1,), (0,)), ((), ()))
    elif mode == "nt":
        (m, k), (n, k2) = a.shape, b.shape
        a_spec = a.spec(bm, bk, lambda i, j, kk: i, lambda i, j, kk: kk)
        b_spec = b.spec(bn, bk, lambda i, j, kk: j, lambda i, j, kk: kk)
        dims = (((1,), (1,)), ((), ()))
    else:
        (k, m), (k2, n) = a.shape, b.shape
        a_spec = a.spec(bk, bm, lambda i, j, kk: kk, lambda i, j, kk: i)
        b_spec = b.spec(bk, bn, lambda i, j, kk: kk, lambda i, j, kk: j)
        dims = (((0,), (0,)), ((), ()))
    assert k == k2, (a.shape, b.shape, mode)
    assert m % bm == 0 and n % bn == 0 and k % bk == 0, (m, n, k, bm, bn, bk)
    nk = k // bk
    out_sds = _out_struct((m, n), out_kind, out_p, out_dtype)
    out_vm = VM(out_sds, out_kind, pfn=out_pfn)
    o_spec = out_vm.spec(bm, bn, lambda i, j, kk: i, lambda i, j, kk: j)
    in_specs, args = [a_spec, b_spec], [a.arr, b.arr]
    tiles = ([res] if res is not None else []) + (list(epi[1]) if epi else [])
    for v in tiles:
        assert v.shape == (m, n)
        in_specs.append(v.spec(bm, bn, lambda i, j, kk: i, lambda i, j, kk: j))
        args.append(v.arr)
    n_out = epi[2] if epi else 1

    def body(a_ref, b_ref, *rest):
        t_refs, o_refs = rest[:len(tiles)], rest[len(tiles):len(tiles) + n_out]
        part = lax.dot_general(a_ref[...], b_ref[...], dims, preferred_element_type=F32)

        def write(tot):
            if res is not None:
                tot = tot + t_refs[0][...].astype(F32)
            outs = epi[0](tot, *[r[...].astype(F32) for r in t_refs[len(tiles) - len(epi[1]):]]) if epi else (tot,)
            for o_ref, val in zip(o_refs, outs):
                o_ref[...] = val.astype(o_ref.dtype)

        if nk == 1:
            write(part)
            return
        acc = rest[-1]
        kk = pl.program_id(2)

        @pl.when(kk == 0)
        def _():
            acc[...] = part

        @pl.when(kk > 0)
        def _():
            acc[...] += part

        @pl.when(kk == nk - 1)
        def _():
            write(acc[...])

    return _call(body, side, name=name, grid=(m // bm, n // bn, nk), in_specs=in_specs,
                 out_specs=o_spec if n_out == 1 else [o_spec] * n_out, out_shape=out_sds if n_out == 1 else [out_sds] * n_out,
                 scratch_shapes=[pltpu.VMEM((bm, bn), F32)] if nk > 1 else [],
                 sem=("parallel", "parallel", "arbitrary"), args=args)


def rmsnorm_fwd(x, g, *, name):
    t, d = x.shape
    bt = _blk(512, t)

    def body(x_ref, g_ref, o_ref):
        xv = x_ref[...]
        r = lax.rsqrt(jnp.mean(xv * xv, axis=-1, keepdims=True) + EPS)
        o_ref[...] = (xv * r * g_ref[...]).astype(o_ref.dtype)

    return pl.pallas_call(
        body, name=name, grid=(t // bt,),
        in_specs=[pl.BlockSpec((bt, d), lambda i: (i, 0)), pl.BlockSpec((1, d), lambda i: (0, 0))],
        out_specs=pl.BlockSpec((bt, d), lambda i: (i, 0)), out_shape=jax.ShapeDtypeStruct((t, d), BF16),
        compiler_params=_params(("parallel",)),
    )(x, g)


def rmsnorm_bwd(x, g, dh, dres, *, name):
    t, d = x.shape
    bt = _blk(256, t)
    want_dx = dres is not None

    def body(x_ref, g_ref, dh_ref, *rest):
        if want_dx:
            dres_ref, dx_ref, dxb_ref, dg_ref = rest
        else:
            (dg_ref,) = rest
        xv = x_ref[...]
        dhv = dh_ref[...].astype(F32)
        r = lax.rsqrt(jnp.mean(xv * xv, axis=-1, keepdims=True) + EPS)
        xh = xv * r
        part = jnp.sum(dhv * xh, axis=0, keepdims=True)

        @pl.when(pl.program_id(0) == 0)
        def _():
            dg_ref[...] = part

        @pl.when(pl.program_id(0) > 0)
        def _():
            dg_ref[...] += part

        if want_dx:
            dy = dhv * g_ref[...]
            dxn = r * (dy - xh * jnp.mean(dy * xh, axis=-1, keepdims=True))
            dx = dres_ref[...] + dxn
            dx_ref[...] = dx
            dxb_ref[...] = dx.astype(BF16)

    row = pl.BlockSpec((bt, d), lambda i: (i, 0))
    vec = pl.BlockSpec((1, d), lambda i: (0, 0))
    in_specs, args = [row, vec, row], [x, g, dh]
    out_specs, out_shape = [vec], [jax.ShapeDtypeStruct((1, d), F32)]
    if want_dx:
        in_specs.append(row)
        args.append(dres)
        out_specs = [row, row] + out_specs
        out_shape = [jax.ShapeDtypeStruct((t, d), F32), jax.ShapeDtypeStruct((t, d), BF16)] + out_shape
    return pl.pallas_call(
        body, name=name, grid=(t // bt,), in_specs=in_specs, out_specs=out_specs, out_shape=out_shape,
        compiler_params=_params(("arbitrary",)),
    )(*args)


def loss_head(x, g, tgt, *, name):
    t, d = x.shape
    bt = _blk(256, t)

    def body(x_ref, g_ref, t_ref, loss_ref, dx_ref, dxb_ref, dg_ref):
        xv = x_ref[...]
        gv = g_ref[...]
        r = lax.rsqrt(jnp.mean(xv * xv, axis=-1, keepdims=True) + EPS)
        xh = xv * r
        e = xh * gv - t_ref[...]
        lpart = jnp.zeros((1, 128), F32) + jnp.sum(e * e) * (0.5 / d)
        dyv = e * (1.0 / d)
        gpart = jnp.sum(dyv * xh, axis=0, keepdims=True)

        @pl.when(pl.program_id(0) == 0)
        def _():
            loss_ref[...] = lpart
            dg_ref[...] = gpart

        @pl.when(pl.program_id(0) > 0)
        def _():
            loss_ref[...] += lpart
            dg_ref[...] += gpart

        dy = dyv * gv
        dx = r * (dy - xh * jnp.mean(dy * xh, axis=-1, keepdims=True))
        dx_ref[...] = dx
        dxb_ref[...] = dx.astype(BF16)

    row = pl.BlockSpec((bt, d), lambda i: (i, 0))
    vec = pl.BlockSpec((1, d), lambda i: (0, 0))
    return pl.pallas_call(
        body, name=name, grid=(t // bt,), in_specs=[row, vec, row],
        out_specs=[pl.BlockSpec((1, 128), lambda i: (0, 0)), row, row, vec],
        out_shape=[jax.ShapeDtypeStruct((1, 128), F32), jax.ShapeDtypeStruct((t, d), F32),
                   jax.ShapeDtypeStruct((t, d), BF16), jax.ShapeDtypeStruct((1, d), F32)],
        compiler_params=_params(("arbitrary",)),
    )(x, g, tgt)


def _sigmoid(x):
    return 1.0 / (1.0 + jnp.exp(-x))


def _swiglu_epi(b, a):
    return b, a * _sigmoid(a) * b


def _swiglu_bwd_epi(ds, a, b):
    sg = _sigmoid(a)
    return ds * b * sg * (1.0 + a * (1.0 - sg)), ds * a * sg


def _xa_probs(qh, kh, scale):
    s = lax.dot_general(qh, kh, (((1,), (1,)), ((), ())), preferred_element_type=F32) * scale
    s = s - jnp.max(s, axis=-1, keepdims=True)
    p = jnp.exp(s)
    return p / jnp.sum(p, axis=-1, keepdims=True)


def xattn_fwd(q, kv, *, name):
    t, d = q.shape
    m = kv.shape[0]
    hd = d // XA_HEADS
    bt = _blk(512, t)
    scale = hd ** -0.5

    def body(q_ref, kv_ref, o_ref):
        for h in range(XA_HEADS):
            qh = q_ref[:, h * hd:(h + 1) * hd]
            kh = kv_ref[:, h * hd:(h + 1) * hd]
            vh = kv_ref[:, d + h * hd:d + (h + 1) * hd]
            p = _xa_probs(qh, kh, scale)
            o_ref[:, h * hd:(h + 1) * hd] = jnp.dot(p.astype(BF16), vh, preferred_element_type=F32).astype(BF16)

    return pl.pallas_call(
        body, name=name, grid=(t // bt,),
        in_specs=[pl.BlockSpec((bt, d), lambda i: (i, 0)), pl.BlockSpec((m, 2 * d), lambda i: (0, 0))],
        out_specs=pl.BlockSpec((bt, d), lambda i: (i, 0)), out_shape=jax.ShapeDtypeStruct((t, d), BF16),
        compiler_params=_params(("parallel",)),
    )(q, kv)


def xattn_bwd(q, kv, do, *, name):
    t, d = q.shape
    m = kv.shape[0]
    hd = d // XA_HEADS
    bt = _blk(512, t)
    scale = hd ** -0.5

    def body(q_ref, kv_ref, do_ref, dq_ref, dkv_ref):
        first = pl.program_id(0) == 0
        for h in range(XA_HEADS):
            qs, ks, vs = slice(h * hd, (h + 1) * hd), slice(h * hd, (h + 1) * hd), slice(d + h * hd, d + (h + 1) * hd)
            qh, kh, vh, doh = q_ref[:, qs], kv_ref[:, ks], kv_ref[:, vs], do_ref[:, qs]
            p = _xa_probs(qh, kh, scale)
            dp = lax.dot_general(doh, vh, (((1,), (1,)), ((), ())), preferred_element_type=F32)
            dsv = p * (dp - jnp.sum(p * dp, axis=-1, keepdims=True)) * scale
            dsb = dsv.astype(BF16)
            dq_ref[:, qs] = jnp.dot(dsb, kh, preferred_element_type=F32).astype(BF16)
            dk = lax.dot_general(dsb, qh, (((0,), (0,)), ((), ())), preferred_element_type=F32)
            dv = lax.dot_general(p.astype(BF16), doh, (((0,), (0,)), ((), ())), preferred_element_type=F32)

            @pl.when(first)
            def _():
                dkv_ref[:, ks] = dk
                dkv_ref[:, vs] = dv

            @pl.when(jnp.logical_not(first))
            def _():
                dkv_ref[:, ks] += dk
                dkv_ref[:, vs] += dv

    row = pl.BlockSpec((bt, d), lambda i: (i, 0))
    full = pl.BlockSpec((m, 2 * d), lambda i: (0, 0))
    return pl.pallas_call(
        body, name=name, grid=(t // bt,), in_specs=[row, full, row], out_specs=[row, full],
        out_shape=[jax.ShapeDtypeStruct((t, d), BF16), jax.ShapeDtypeStruct((m, 2 * d), F32)],
        compiler_params=_params(("arbitrary",)),
    )(q, kv, do)


def _pool_p(buf, uv, rows, w, bt):
    acc = uv
    for dd in range(1, w):
        acc = acc + buf[pl.ds(POOL_HALO - dd, bt), :]
    cnt = jnp.minimum(rows + 1, w).astype(F32)
    return acc / cnt - uv


def pool_fwd(z, w_pool, scale, *, name):
    t = z.shape[0]
    ng, gsz = w_pool.shape[0], w_pool.shape[1]
    mix = ng * gsz
    bt = _blk(512, t)

    def body(u_ref, uh_ref, w_ref, sc_ref, o_ref, buf):
        r = pl.program_id(0)
        rows = r * bt + lax.broadcasted_iota(jnp.int32, (bt, 1), 0)
        for g in range(ng):
            gs = slice(g * gsz, (g + 1) * gsz)
            uv = u_ref[:, gs]
            buf[0:POOL_HALO, :] = jnp.where(r > 0, uh_ref[:, gs], 0.0)
            buf[POOL_HALO:POOL_HALO + bt, :] = uv
            p = _pool_p(buf, uv, rows, POOL_WINDOWS[g], bt)
            y = jnp.dot(p.astype(BF16), w_ref[g], preferred_element_type=F32) * sc_ref[:, gs]
            o_ref[:, gs] = y.astype(BF16)

    hb = bt // POOL_HALO
    return pl.pallas_call(
        body, name=name, grid=(t // bt,),
        in_specs=[pl.BlockSpec((bt, mix), lambda i: (i, 0)),
                  pl.BlockSpec((POOL_HALO, mix), lambda i: (jnp.maximum(i * hb - 1, 0), 0)),
                  pl.BlockSpec((ng, gsz, gsz), lambda i: (0, 0, 0)), pl.BlockSpec((1, mix), lambda i: (0, 0))],
        out_specs=pl.BlockSpec((None, bt, mix), lambda i: (0, i, 0)),
        out_shape=jax.ShapeDtypeStruct((2, t, mix), BF16),
        scratch_shapes=[pltpu.VMEM((POOL_HALO + bt, gsz), F32)],
        compiler_params=_params(("parallel",)),
    )(z, z, w_pool, scale)


def pool_bwd(z, dcat, w_pool, scale, *, name):
    t = z.shape[0]
    ng, gsz = w_pool.shape[0], w_pool.shape[1]
    mix = ng * gsz
    bt = _blk(512, t)
    nb = t // bt
    nt_dims = (((1,), (1,)), ((), ()))
    tn_dims = (((0,), (0,)), ((), ()))

    def body(u_ref, uh_ref, dy_ref, dyh_ref, w_ref, sc_ref, du_ref, dw_ref, dsc_ref, buf, buf2):
        r = pl.program_id(0)
        first = r == 0
        rows = r * bt + lax.broadcasted_iota(jnp.int32, (bt, 1), 0)
        rows_h = (r + 1) * bt + lax.broadcasted_iota(jnp.int32, (POOL_HALO, 1), 0)
        for g in range(ng):
            w = POOL_WINDOWS[g]
            gs = slice(g * gsz, (g + 1) * gsz)
            uv = u_ref[:, gs]
            buf[0:POOL_HALO, :] = jnp.where(r > 0, uh_ref[:, gs], 0.0)
            buf[POOL_HALO:POOL_HALO + bt, :] = uv
            pb = _pool_p(buf, uv, rows, w, bt).astype(BF16)
            wg = w_ref[g]
            sc = sc_ref[:, gs]
            y0 = jnp.dot(pb, wg, preferred_element_type=F32)
            dyv = dy_ref[:, gs].astype(F32)
            dsc = jnp.sum(dyv * y0, axis=0, keepdims=True)
            dyw = (dyv * sc).astype(BF16)
            dw = lax.dot_general(pb, dyw, tn_dims, preferred_element_type=F32)

            @pl.when(first)
            def _():
                dw_ref[g] = dw
                dsc_ref[:, gs] = dsc

            @pl.when(jnp.logical_not(first))
            def _():
                dw_ref[g] += dw
                dsc_ref[:, gs] += dsc

            dp = lax.dot_general(dyw, wg, nt_dims, preferred_element_type=F32)
            dyh = (dyh_ref[:, gs].astype(F32) * sc).astype(BF16)
            dph = lax.dot_general(dyh, wg, nt_dims, preferred_element_type=F32)
            dph = jnp.where(r < nb - 1, dph, 0.0)
            buf2[0:bt, :] = dp / jnp.minimum(rows + 1, w).astype(F32)
            buf2[bt:bt + POOL_HALO, :] = dph / jnp.minimum(rows_h + 1, w).astype(F32)
            acc = buf2[pl.ds(0, bt), :]
            for dd in range(1, w):
                acc = acc + buf2[pl.ds(dd, bt), :]
            du_ref[:, gs] = (acc - dp).astype(BF16)

    hb = bt // POOL_HALO
    nhb = t // POOL_HALO
    return pl.pallas_call(
        body, name=name, grid=(nb,),
        in_specs=[pl.BlockSpec((bt, mix), lambda i: (i, 0)),
                  pl.BlockSpec((POOL_HALO, mix), lambda i: (jnp.maximum(i * hb - 1, 0), 0)),
                  pl.BlockSpec((None, bt, mix), lambda i: (0, i, 0)),
                  pl.BlockSpec((None, POOL_HALO, mix), lambda i: (0, jnp.minimum((i + 1) * hb, nhb - 1), 0)),
                  pl.BlockSpec((ng, gsz, gsz), lambda i: (0, 0, 0)), pl.BlockSpec((1, mix), lambda i: (0, 0))],
        out_specs=[pl.BlockSpec((None, bt, mix), lambda i: (4, i, 0)),
                   pl.BlockSpec((ng, gsz, gsz), lambda i: (0, 0, 0)), pl.BlockSpec((1, mix), lambda i: (0, 0))],
        out_shape=[jax.ShapeDtypeStruct((5, t, mix), BF16), jax.ShapeDtypeStruct((ng, gsz, gsz), F32),
                   jax.ShapeDtypeStruct((1, mix), F32)],
        scratch_shapes=[pltpu.VMEM((POOL_HALO + bt, gsz), F32), pltpu.VMEM((bt + POOL_HALO, gsz), F32)],
        compiler_params=_params(("arbitrary",)),
    )(z, z, dcat, dcat, w_pool, scale)


HG_HEADS_PER_STEP = 2
HG_LEVELS = ((64, 31), (32, 15), (16, 7))
HG_DIAG = (8, 3)
_NT = (((1,), (1,)), ((), ()))
_TN = (((0,), (0,)), ((), ()))
_HI = lax.Precision.HIGHEST


def _hg_masks():
    c = HG_CHUNK
    t = lax.broadcasted_iota(jnp.int32, (c, c), 0)
    s = lax.broadcasted_iota(jnp.int32, (c, c), 1)
    masks = []
    for blk, row in HG_LEVELS:
        sh = blk.bit_length() - 1
        same = (t >> sh) == (s >> sh)
        masks.append(same & ((t & (blk - 1)) > row) & ((s & (blk - 1)) <= row))
    sh = HG_DIAG[0].bit_length() - 1
    masks.append(((t >> sh) == (s >> sh)) & (s <= t))
    return t, s, masks


def _row_of_block(x, blk, row):
    c, n = x.shape
    x3 = x.reshape(c // blk, blk, n)
    return jnp.broadcast_to(x3[:, row:row + 1, :], x3.shape).reshape(c, n)


def _hg_parts(qv, flv, lb, masks, tri):
    sgf = _sigmoid(flv)
    f = lb + (1.0 - lb) * sgf
    logf = jnp.log(f)
    kk = 1.0 - f
    sgq = _sigmoid(qv)
    qf = qv * sgq * (HG_HEAD ** -0.5)
    bc = jnp.dot(tri, logf, preferred_element_type=F32, precision=_HI)
    levels = []
    a = None
    for li, (blk, row) in enumerate(HG_LEVELS + (HG_DIAG,)):
        e = bc - _row_of_block(bc, blk, row)
        if li < len(HG_LEVELS):
            eq, ek = jnp.exp(jnp.minimum(e, 0.0)), jnp.exp(jnp.minimum(-e, 0.0))
        else:
            eq, ek = jnp.exp(jnp.clip(e, -EXP_CLAMP, EXP_CLAMP)), jnp.exp(jnp.clip(-e, -EXP_CLAMP, EXP_CLAMP))
        qt, kt = qf * eq, kk * ek
        part = jnp.where(masks[li], lax.dot_general(qt.astype(BF16), kt.astype(BF16), _NT, preferred_element_type=F32), 0.0)
        a = part if a is None else a + part
        levels.append((eq, ek, qt, kt))
    return dict(sgf=sgf, f=f, kk=kk, sgq=sgq, qf=qf, bc=bc, levels=levels, a=a)


def hgrn_fwd(z, cat, lb, gain, mix_a, *, side=None, name):
    t = z.shape[0]
    mix_b = lb.shape[1]
    nh = mix_b // HG_HEAD
    bt = _blk(256, t)
    ncb = bt // HG_CHUNK
    dh = HG_HEAD

    def body(q_ref, fl_ref, i_ref, g_ref, lb_ref, gain_ref, cat_in, o_ref, st_ref, st):
        del cat_in

        @pl.when(pl.program_id(1) == 0)
        def _():
            st[...] = jnp.zeros_like(st)

        t_i, s_i, masks = _hg_masks()
        tri = (s_i <= t_i).astype(F32)
        lbv, gn = lb_ref[...], gain_ref[...]
        for c in range(ncb):
            rs = slice(c * HG_CHUNK, (c + 1) * HG_CHUNK)
            pr = _hg_parts(q_ref[rs, :], fl_ref[rs, :], lbv, masks, tri)
            vb = i_ref[rs, :].astype(BF16)
            stv = st[...]
            st_ref[c] = stv
            bc = pr["bc"]
            qt = pr["qf"] * jnp.exp(bc)
            o = (jnp.dot(pr["a"].astype(BF16), vb, preferred_element_type=F32)
                 + lax.dot_general(qt.astype(BF16), stv.astype(BF16), _NT, preferred_element_type=F32))
            bl = bc[HG_CHUNK - 1:HG_CHUNK, :]
            khat = pr["kk"] * jnp.exp(bl - bc)
            st[...] = stv * jnp.exp(bl) + lax.dot_general(vb, khat.astype(BF16), _TN, preferred_element_type=F32)
            r = lax.rsqrt(jnp.mean(o * o, axis=-1, keepdims=True) + EPS)
            gv = g_ref[rs, :]
            o_ref[rs, :] = (o * r * gn * (gv * _sigmoid(gv))).astype(BF16)

    def col(which):
        base = (mix_a + which * mix_b) // dh
        return pl.BlockSpec((bt, dh), lambda h, i: (i, base + h))

    return _call(
        body, side, name=name, grid=(nh, t // bt),
        in_specs=[col(0), col(1), col(2), col(3), pl.BlockSpec((1, dh), lambda h, i: (0, h)),
                  pl.BlockSpec((1, dh), lambda h, i: (0, 0)), _ANY],
        out_specs=[pl.BlockSpec((None, bt, dh), lambda h, i: (1, i, h)),
                   pl.BlockSpec((None, ncb, dh, dh), lambda h, i: (h, i, 0, 0))],
        out_shape=[jax.ShapeDtypeStruct(cat.shape, BF16), jax.ShapeDtypeStruct((nh, t // HG_CHUNK, dh, dh), F32)],
        scratch_shapes=[pltpu.VMEM((dh, dh), F32)], aliases={6: 0}, sem=("parallel", "arbitrary"),
        args=(z, z, z, z, lb, gain, cat))


def hgrn_bwd(z, dcat, dz5, states, lb, gain, mix_a, *, side=None, name):
    t = z.shape[0]
    mix_b = lb.shape[1]
    nh = mix_b // HG_HEAD
    bt = _blk(256, t)
    nb = t // bt
    ncb = bt // HG_CHUNK
    dh = HG_HEAD
    hp = HG_HEADS_PER_STEP if nh % HG_HEADS_PER_STEP == 0 else 1

    def body(q_ref, fl_ref, i_ref, g_ref, dy_ref, st_ref, lb_ref, gain_ref, dz_in, dz_ref, dlb_ref, dgn_ref, dst):
        del dz_in
        first = pl.program_id(1) == 0

        @pl.when(first)
        def _():
            dst[...] = jnp.zeros_like(dst)

        t_i, s_i, masks = _hg_masks()
        tri = (s_i <= t_i).astype(F32)
        triu = (s_i >= t_i).astype(F32)
        last_row = lax.broadcasted_iota(jnp.int32, (HG_CHUNK, 1), 0) == HG_CHUNK - 1
        gn = gain_ref[...]
        dlb_acc = [jnp.zeros((1, dh), F32) for _ in range(hp)]
        dgn_acc = [jnp.zeros((1, dh), F32) for _ in range(hp)]
        for c, hh in [(c, hh) for c in reversed(range(ncb)) for hh in range(hp)]:
            rs, cs = slice(c * HG_CHUNK, (c + 1) * HG_CHUNK), slice(hh * dh, (hh + 1) * dh)
            lbv = lb_ref[:, cs]
            qv, flv, gv = q_ref[rs, cs], fl_ref[rs, cs], g_ref[rs, cs]
            pr = _hg_parts(qv, flv, lbv, masks, tri)
            vb = i_ref[rs, cs].astype(BF16)
            stv = st_ref[hh, c]
            stb = stv.astype(BF16)
            dsv = dst[hh]
            dsb = dsv.astype(BF16)
            bc, kk, qf, ab = pr["bc"], pr["kk"], pr["qf"], pr["a"].astype(BF16)
            ebc = jnp.exp(bc)
            qt = qf * ebc
            qtb = qt.astype(BF16)
            o = jnp.dot(ab, vb, preferred_element_type=F32) + lax.dot_general(qtb, stb, _NT, preferred_element_type=F32)
            r = lax.rsqrt(jnp.mean(o * o, axis=-1, keepdims=True) + EPS)
            oh = o * r
            sgg = _sigmoid(gv)
            dyv = dy_ref[rs, cs].astype(F32)
            don = dyv * (gv * sgg)
            dgate = dyv * (oh * gn) * (sgg * (1.0 + gv * (1.0 - sgg)))
            dgn_acc[hh] = dgn_acc[hh] + jnp.sum(don * oh, axis=0, keepdims=True)
            doh = don * gn
            do = r * (doh - oh * jnp.mean(doh * oh, axis=-1, keepdims=True))
            dob = do.astype(BF16)
            bl = bc[HG_CHUNK - 1:HG_CHUNK, :]
            ebl = jnp.exp(bl)
            ekh = jnp.exp(bl - bc)
            khat = kk * ekh
            dv = (lax.dot_general(ab, dob, _TN, preferred_element_type=F32)
                  + lax.dot_general(khat.astype(BF16), dsb, _NT, preferred_element_type=F32))
            da = lax.dot_general(dob, vb, _NT, preferred_element_type=F32)
            dqt = jnp.dot(dob, stb, preferred_element_type=F32)
            dkh = jnp.dot(vb, dsb, preferred_element_type=F32)
            dst[hh] = dsv * ebl + lax.dot_general(dob, qtb, _TN, preferred_element_type=F32)
            dbl = jnp.sum(dsv * stv, axis=0, keepdims=True) * ebl + jnp.sum(dkh * khat, axis=0, keepdims=True)
            dqf = dqt * ebc
            dkk = dkh * ekh
            dbc = dqt * qt - dkh * khat
            for li, (eq, ek, qtl, ktl) in enumerate(pr["levels"]):
                gm = jnp.where(masks[li], da, 0.0).astype(BF16)
                qtr, ktr = qtl.astype(BF16), ktl.astype(BF16)
                dql = jnp.dot(gm, ktr, preferred_element_type=F32)
                dkl = lax.dot_general(gm, qtr, _TN, preferred_element_type=F32)
                dqf = dqf + dql * eq
                dkk = dkk + dkl * ek
                dbc = dbc + qtr.astype(F32) * dql - ktr.astype(F32) * dkl
            dbc = dbc + jnp.where(last_row, dbl, 0.0)
            dlogf = jnp.dot(triu, dbc, preferred_element_type=F32, precision=_HI)
            df = dlogf / pr["f"] - dkk
            sgf = pr["sgf"]
            dfl = df * (1.0 - lbv) * sgf * (1.0 - sgf)
            dlb_acc[hh] = dlb_acc[hh] + jnp.sum(df * (1.0 - sgf), axis=0, keepdims=True)
            sgq = pr["sgq"]
            dq = dqf * (HG_HEAD ** -0.5) * (sgq * (1.0 + qv * (1.0 - sgq)))
            dz_ref[0, rs, cs] = dq.astype(BF16)
            dz_ref[1, rs, cs] = dfl.astype(BF16)
            dz_ref[2, rs, cs] = dv.astype(BF16)
            dz_ref[3, rs, cs] = dgate.astype(BF16)

        @pl.when(first)
        def _():
            for hh in range(hp):
                dlb_ref[:, hh * dh:(hh + 1) * dh] = dlb_acc[hh]
                dgn_ref[hh] = dgn_acc[hh]

        @pl.when(jnp.logical_not(first))
        def _():
            for hh in range(hp):
                dlb_ref[:, hh * dh:(hh + 1) * dh] += dlb_acc[hh]
                dgn_ref[hh] += dgn_acc[hh]

    wd = hp * dh

    def col(which):
        base = (mix_a + which * mix_b) // wd
        return pl.BlockSpec((bt, wd), lambda h, i: (nb - 1 - i, base + h))

    return _call(
        body, side, name=name, grid=(nh // hp, nb),
        in_specs=[col(0), col(1), col(2), col(3),
                  pl.BlockSpec((None, bt, wd), lambda h, i: (1, nb - 1 - i, h)),
                  pl.BlockSpec((hp, ncb, dh, dh), lambda h, i: (h, nb - 1 - i, 0, 0)),
                  pl.BlockSpec((1, wd), lambda h, i: (0, h)), pl.BlockSpec((1, dh), lambda h, i: (0, 0)), _ANY],
        out_specs=[pl.BlockSpec((4, bt, wd), lambda h, i: (0, nb - 1 - i, h)),
                   pl.BlockSpec((1, wd), lambda h, i: (0, h)),
                   pl.BlockSpec((hp, 1, dh), lambda h, i: (h, 0, 0))],
        out_shape=[jax.ShapeDtypeStruct(dz5.shape, BF16), jax.ShapeDtypeStruct((1, mix_b), F32),
                   jax.ShapeDtypeStruct((nh, 1, dh), F32)],
        scratch_shapes=[pltpu.VMEM((hp, dh, dh), F32)], aliases={8: 0}, sem=("parallel", "arbitrary"),
        args=(z, z, z, z, dcat, states, lb, gain, dz5))


LOG2E = 1.4426950408889634


def _fox_scores(qb, kb, fk, scale, masked):
    s = lax.dot_general(qb, kb, _NT, preferred_element_type=F32) * (scale * LOG2E) - fk * LOG2E
    if masked:
        n = s.shape[0]
        row = lax.broadcasted_iota(jnp.int32, (n, n), 0)
        col = lax.broadcasted_iota(jnp.int32, (n, n), 1)
        s = jnp.where(col <= row, s, -jnp.inf)
    return s


def fox_fwd(qkv, fk, *, side=None, name):
    _, t, d = qkv.shape
    nh = d // FOX_HEAD
    b = _blk(FOX_BLK, t)
    nb = t // b
    dh = FOX_HEAD
    scale = dh ** -0.5

    def body(q_ref, k_ref, v_ref, f_ref, o_ref, lse_ref):
        qi = pl.program_id(1)
        qb = q_ref[...]

        def step(kj, carry, masked):
            m, l, acc = carry
            off = pl.multiple_of(kj * b, b)
            s = _fox_scores(qb, k_ref[pl.ds(off, b), :], f_ref[kj], scale, masked)
            m_new = jnp.maximum(m, jnp.max(s, axis=-1, keepdims=True))
            alpha = jnp.exp2(m - m_new)
            p = jnp.exp2(s - m_new)
            l = alpha * l + jnp.sum(p, axis=-1, keepdims=True)
            acc = alpha * acc + jnp.dot(p.astype(BF16), v_ref[pl.ds(off, b), :], preferred_element_type=F32)
            return m_new, l, acc

        init = (jnp.full((b, 1), -jnp.inf, F32), jnp.zeros((b, 1), F32), jnp.zeros((b, dh), F32))
        carry = lax.fori_loop(0, qi, lambda kj, c: step(kj, c, False), init)
        m, l, acc = step(qi, carry, True)
        o_ref[...] = (acc / l).astype(BF16)
        lse_ref[...] = m + jnp.log(l) * LOG2E

    return _call(
        body, side, name=name, grid=(nh, nb),
        in_specs=[pl.BlockSpec((None, b, dh), lambda h, i: (0, i, h)),
                  pl.BlockSpec((None, t, dh), lambda h, i: (1, 0, h)),
                  pl.BlockSpec((None, t, dh), lambda h, i: (2, 0, h)),
                  pl.BlockSpec((None, nb, 1, b), lambda h, i: (h, 0, 0, 0))],
        out_specs=[pl.BlockSpec((b, dh), lambda h, i: (i, h)), pl.BlockSpec((None, b, 1), lambda h, i: (h, i, 0))],
        out_shape=[jax.ShapeDtypeStruct((t, d), BF16), jax.ShapeDtypeStruct((nh, t, 1), F32)],
        sem=("parallel", "parallel"), args=(qkv, qkv, qkv, fk))


def fox_bwd_dq(qkv, fk, do, lse, *, side=None, name):
    _, t, d = qkv.shape
    nh = d // FOX_HEAD
    b = _blk(FOX_BLK, t)
    nb = t // b
    dh = FOX_HEAD
    scale = dh ** -0.5

    def body(q_ref, k_ref, v_ref, f_ref, do_ref, lse_ref, dq_ref, dl_ref, p_buf, dp_buf):
        qi = pl.program_id(1)
        qb, dob, lse_v = q_ref[...], do_ref[...], lse_ref[...]

        def first(kj, dl, masked):
            off = pl.multiple_of(kj * b, b)
            p = jnp.exp2(_fox_scores(qb, k_ref[pl.ds(off, b), :], f_ref[kj], scale, masked) - lse_v)
            dp = lax.dot_general(dob, v_ref[pl.ds(off, b), :], _NT, preferred_element_type=F32)
            p_buf[kj] = p
            dp_buf[kj] = dp
            return dl + jnp.sum(p * dp, axis=-1, keepdims=True)

        dl = lax.fori_loop(0, qi, lambda kj, c: first(kj, c, False), jnp.zeros((b, 1), F32))
        dl = first(qi, dl, True)
        dl_ref[...] = dl

        def second(kj, dq):
            off = pl.multiple_of(kj * b, b)
            dsv = p_buf[kj] * (dp_buf[kj] - dl)
            return dq + jnp.dot(dsv.astype(BF16), k_ref[pl.ds(off, b), :], preferred_element_type=F32)

        dq = lax.fori_loop(0, qi + 1, second, jnp.zeros((b, dh), F32))
        dq_ref[...] = (dq * scale).astype(BF16)

    col = pl.BlockSpec((None, b, 1), lambda h, i: (h, i, 0))
    return _call(
        body, side, name=name, grid=(nh, nb),
        in_specs=[pl.BlockSpec((None, b, dh), lambda h, i: (0, i, h)),
                  pl.BlockSpec((None, t, dh), lambda h, i: (1, 0, h)),
                  pl.BlockSpec((None, t, dh), lambda h, i: (2, 0, h)),
                  pl.BlockSpec((None, nb, 1, b), lambda h, i: (h, 0, 0, 0)),
                  pl.BlockSpec((b, dh), lambda h, i: (i, h)), col],
        out_specs=[pl.BlockSpec((None, b, dh), lambda h, i: (2, i, h)), col],
        out_shape=[jax.ShapeDtypeStruct((3, t, d), BF16), jax.ShapeDtypeStruct((nh, t, 1), F32)],
        scratch_shapes=[pltpu.VMEM((nb, b, b), F32), pltpu.VMEM((nb, b, b), F32)],
        sem=("parallel", "parallel"), args=(qkv, qkv, qkv, fk, do, lse))


def fox_bwd_dkv(qkv, fk, do, lse, delta, dqkv, *, side=None, name):
    _, t, d = qkv.shape
    nh = d // FOX_HEAD
    b = _blk(FOX_BLK, t)
    nb = t // b
    dh = FOX_HEAD
    scale = dh ** -0.5

    def body(q_ref, k_ref, v_ref, f_ref, do_ref, lse_ref, dl_ref, dz_in, dkv_ref, df_ref):
        del dz_in
        kj = pl.program_id(1)
        kb, vb, fkv = k_ref[...], v_ref[...], f_ref[...]

        def step(qi, carry, masked):
            dk, dv, df = carry
            off = pl.multiple_of(qi * b, b)
            qb, dob = q_ref[pl.ds(off, b), :], do_ref[pl.ds(off, b), :]
            p = jnp.exp2(_fox_scores(qb, kb, fkv, scale, masked) - lse_ref[pl.ds(off, b), :])
            dv = dv + lax.dot_general(p.astype(BF16), dob, _TN, preferred_element_type=F32)
            dp = lax.dot_general(dob, vb, _NT, preferred_element_type=F32)
            dsv = p * (dp - dl_ref[pl.ds(off, b), :])
            dk = dk + lax.dot_general(dsv.astype(BF16), qb, _TN, preferred_element_type=F32)
            return dk, dv, df - jnp.sum(dsv, axis=0, keepdims=True)

        init = (jnp.zeros((b, dh), F32), jnp.zeros((b, dh), F32), jnp.zeros((1, b), F32))
        carry = step(kj, init, True)
        dk, dv, df = lax.fori_loop(kj + 1, nb, lambda qi, c: step(qi, c, False), carry)
        dkv_ref[0] = (dk * scale).astype(BF16)
        dkv_ref[1] = dv.astype(BF16)
        df_ref[...] = df

    col = pl.BlockSpec((None, t, 1), lambda h, j: (h, 0, 0))
    return _call(
        body, side, name=name, grid=(nh, nb),
        in_specs=[pl.BlockSpec((None, t, dh), lambda h, j: (0, 0, h)),
                  pl.BlockSpec((None, b, dh), lambda h, j: (1, j, h)),
                  pl.BlockSpec((None, b, dh), lambda h, j: (2, j, h)),
                  pl.BlockSpec((None, None, 1, b), lambda h, j: (h, j, 0, 0)),
                  pl.BlockSpec((t, dh), lambda h, j: (0, h)), col, col, pl.BlockSpec(memory_space=pl.ANY)],
        out_specs=[pl.BlockSpec((2, b, dh), lambda h, j: (0, j, h)),
                   pl.BlockSpec((None, None, 1, b), lambda h, j: (h, j, 0, 0))],
        out_shape=[jax.ShapeDtypeStruct((3, t, d), BF16), jax.ShapeDtypeStruct((nh, nb, 1, b), F32)],
        aliases={7: 0}, sem=("parallel", "parallel"), args=(qkv, qkv, qkv, fk, do, lse, delta, dqkv))


FL_BLK = 256


def _log_sigmoid(x):
    return jnp.minimum(x, 0.0) - jnp.log(1.0 + jnp.exp(-jnp.abs(x)))


def fl_fwd(zf, bf, *, name):
    t, n = zf.shape
    bt = _blk(FL_BLK, t)

    def body(z_ref, b_ref, o_ref, carry):
        @pl.when(pl.program_id(0) == 0)
        def _():
            carry[...] = jnp.zeros_like(carry)

        ls = _log_sigmoid(z_ref[...] + b_ref[...])
        r = lax.broadcasted_iota(jnp.int32, (bt, bt), 0)
        c = lax.broadcasted_iota(jnp.int32, (bt, bt), 1)
        cs = jnp.dot((c <= r).astype(F32), ls, preferred_element_type=F32, precision=_HI) + carry[...]
        o_ref[...] = cs
        carry[...] = cs[bt - 1:bt, :]

    return pl.pallas_call(
        body, name=name, grid=(t // bt,),
        in_specs=[pl.BlockSpec((bt, n), lambda i: (i, 0)), pl.BlockSpec((1, n), lambda i: (0, 0))],
        out_specs=pl.BlockSpec((bt, n), lambda i: (i, 0)), out_shape=jax.ShapeDtypeStruct((t, n), F32),
        scratch_shapes=[pltpu.VMEM((1, n), F32)], compiler_params=_params(("arbitrary",)),
    )(zf, bf)


def fl_bwd(df, zf, bf, *, name):
    t, n = zf.shape
    bt = _blk(FL_BLK, t)
    nb = t // bt

    def body(df_ref, z_ref, b_ref, dz_ref, db_ref, carry):
        first = pl.program_id(0) == 0

        @pl.when(first)
        def _():
            carry[...] = jnp.zeros_like(carry)

        r = lax.broadcasted_iota(jnp.int32, (bt, bt), 0)
        c = lax.broadcasted_iota(jnp.int32, (bt, bt), 1)
        dls = jnp.dot((c >= r).astype(F32), df_ref[...], preferred_element_type=F32, precision=_HI) + carry[...]
        carry[...] = dls[0:1, :]
        dz = dls * (1.0 - _sigmoid(z_ref[...] + b_ref[...]))
        dz_ref[...] = dz.astype(BF16)
        part = jnp.sum(dz, axis=0, keepdims=True)

        @pl.when(first)
        def _():
            db_ref[...] = part

        @pl.when(jnp.logical_not(first))
        def _():
            db_ref[...] += part

    row = pl.BlockSpec((bt, n), lambda i: (nb - 1 - i, 0))
    vec = pl.BlockSpec((1, n), lambda i: (0, 0))
    return pl.pallas_call(
        body, name=name, grid=(nb,), in_specs=[row, row, vec], out_specs=[row, vec],
        out_shape=[jax.ShapeDtypeStruct((t, n), BF16), jax.ShapeDtypeStruct((1, n), F32)],
        scratch_shapes=[pltpu.VMEM((1, n), F32)], compiler_params=_params(("arbitrary",)),
    )(df, zf, bf)


def _adamw_math(w, g, m, v):
    m = ADAM_B1 * m + (1.0 - ADAM_B1) * g
    v = ADAM_B2 * v + (1.0 - ADAM_B2) * (g * g)
    m_hat = m / (1.0 - ADAM_B1 ** ADAM_STEP)
    v_hat = v / (1.0 - ADAM_B2 ** ADAM_STEP)
    delta = -ADAM_LR * (m_hat / (jnp.sqrt(v_hat) + ADAM_EPS) + ADAM_WD * w)
    return delta, m, v


def adamw(w, g, m, v, *, side=None, name):
    nl, r, c = w.shape
    br = _row_blk(r, 256)
    nb = r // br

    def body(w_ref, g_ref, m_ref, v_ref, go_ref, d_ref, mo_ref, vo_ref):
        gv = g_ref[...]
        go_ref[...] = gv
        d_ref[...], mo_ref[...], vo_ref[...] = _adamw_math(w_ref[...], gv, m_ref[...], v_ref[...])

    return _call(body, side, name=name, grid=(nl, nb), in_specs=[pl.BlockSpec((None, br, c), lambda l, i: (l, i, 0))] * 4,
                 out_specs=[pl.BlockSpec((br, c), lambda l, i: (l * nb + i, 0))] * 4,
                 out_shape=[jax.ShapeDtypeStruct((nl * r, c), F32)] * 4, sem=("parallel", "parallel"), args=(w, g, m, v))


def _f2(a):
    return a.reshape(a.shape[-2:])


def _local_step(x0, mem, tgt, sp, plan):
    t, d = x0.shape
    mix_a = sp["pool_scale"].shape[1]
    small = {}

    def row(a, l):
        return a[l:l + 1]

    def rows4(g):
        return g.reshape(N_CHIPS, -1, g.shape[-1])

    def xattn_f(l, xin):
        w = plan.weights(f"xa{l}")
        hx = rmsnorm_fwd(xin, row(sp["xa_norm"], l), name=f"xa_norm_f{l}")
        q = _f2(matmul(vm2(hx), w["wq"], "nn", out_dtype=BF16, side=plan.take_fwd(), name=f"xa_q_f{l}"))
        mn = rmsnorm_fwd(mem, row(sp["xa_mem_norm"], l), name=f"xa_memnorm_f{l}")
        kv = _f2(matmul(vm2(mn), w["wkv"], "nn", out_dtype=BF16, name=f"xa_kv_f{l}"))
        o = xattn_fwd(q, kv, name=f"xa_attn_f{l}")
        xout = _f2(matmul(vm2(o), w["wo"], "nn", out_dtype=F32, res=vm2(xin), side=plan.take_fwd(), name=f"xa_o_f{l}"))
        return xout, (xin, hx, q, mn, kv, o)

    def ffn_f(l, xin):
        w = plan.weights(f"ffn{l}")
        hf = rmsnorm_fwd(xin, row(sp["ffn_norm"], l), name=f"ffn_norm_f{l}")
        a = _f2(matmul(vm2(hf), w["gate"], "nn", out_dtype=BF16, side=plan.take_fwd(), name=f"ffn_gate_f{l}"))
        b, s = matmul(vm2(hf), w["up"], "nn", out_dtype=BF16, epi=(_swiglu_epi, [vm2(a)], 2), side=plan.take_fwd(), name=f"ffn_up_f{l}")
        b, s = _f2(b), _f2(s)
        xout = _f2(matmul(vm2(s), w["down"], "nn", out_dtype=F32, res=vm2(xin), side=plan.take_fwd(), name=f"ffn_down_f{l}"))
        return xout, (xin, hf, a, b, s)

    ev = plan.weights("ev")
    h0 = rmsnorm_fwd(x0, sp["ev_norm"], name="ev_norm_f")
    z = _f2(matmul(vm2(h0), ev["ev_in"], "nn", out_dtype=F32, side=plan.take_fwd(), name="ev_in_f"))
    cat = pool_fwd(z, ev["pool"], sp["pool_scale"], name="pool_f")
    cat, states = hgrn_fwd(z, cat, sp["lb"], sp["hg_gain"], mix_a, side=plan.take_fwd(), name="hgrn_f")
    x1 = _f2(matmul(VM(cat), ev["ev_out"], "nn", out_dtype=F32, res=vm2(x0), side=plan.take_fwd(), name="ev_out_f"))
    x2, xa0 = xattn_f(0, x1)
    x3, ff0 = ffn_f(0, x2)

    od = plan.weights("od")
    ho = rmsnorm_fwd(x3, sp["od_norm"], name="od_norm_f")
    qkv = matmul(vm2(ho), od["wqkv"], "nn", out_dtype=BF16, out_p=3, side=plan.take_fwd(), name="od_qkv_f")
    zf = _f2(matmul(vm2(ho), od["wf"], "nn", out_dtype=F32, name="od_fl_f"))
    fcum = fl_fwd(zf, sp["bf"], name="od_forget_f")
    nh = d // FOX_HEAD
    nfb = t // _blk(FOX_BLK, t)
    fk = fcum[:, :nh].T.reshape(nh, nfb, 1, t // nfb)
    of, lse = fox_fwd(qkv, fk, side=plan.take_fwd(), name="fox_f")
    x4 = _f2(matmul(vm2(of), od["od_out"], "nn", out_dtype=F32, res=vm2(x3), name="od_out_f"))
    x5, xa1 = xattn_f(1, x4)
    x6, ff1 = ffn_f(1, x5)
    loss, dx, dxb, small["final_norm"] = loss_head(x6, sp["final_norm"], tgt, name="loss_head")

    def ffn_b(l, saved, dx, dxb):
        xin, hf, a, b, s = saved
        w = plan.weights(f"ffn{l}")
        da, db = matmul(vm2(dxb), w["down"], "nt", out_dtype=BF16, epi=(_swiglu_bwd_epi, [vm2(a), vm2(b)], 2), side=plan.take_bwd(1), name=f"ffn_down_bx{l}")
        da, db = _f2(da), _f2(db)
        g_down = rows4(matmul(vm2(s), vm2(dxb), "tn", out_dtype=BF16, name=f"ffn_down_bw{l}"))
        g_gate = matmul(vm2(hf), vm2(da), "tn", out_dtype=BF16, out_p=N_CHIPS, name=f"ffn_gate_bw{l}")
        g_up = matmul(vm2(hf), vm2(db), "tn", out_dtype=BF16, out_p=N_CHIPS, name=f"ffn_up_bw{l}")
        plan.grads_done({f"down{l}": g_down, f"gate{l}": g_gate, f"up{l}": g_up})
        dh = matmul(vm2(da), w["gate"], "nt", out_dtype=F32, side=plan.take_bwd(), name=f"ffn_gate_bx{l}")
        dh = _f2(matmul(vm2(db), w["up"], "nt", out_dtype=BF16, res=VM(dh), side=plan.take_bwd(), name=f"ffn_up_bx{l}"))
        dx, dxb, dg = rmsnorm_bwd(xin, row(sp["ffn_norm"], l), dh, dx, name=f"ffn_norm_b{l}")
        return dx, dxb, dg

    def xattn_b(l, saved, dx, dxb):
        xin, hx, q, mn, kv, o = saved
        w = plan.weights(f"xa{l}")
        do = _f2(matmul(vm2(dxb), w["wo"], "nt", out_dtype=BF16, side=plan.take_bwd(), name=f"xa_o_bx{l}"))
        g_wo = rows4(matmul(vm2(o), vm2(dxb), "tn", out_dtype=BF16, name=f"xa_o_bw{l}"))
        dq, dkv = xattn_bwd(q, kv, do, name=f"xa_attn_b{l}")
        g_wq = rows4(matmul(vm2(hx), vm2(dq), "tn", out_dtype=BF16, name=f"xa_q_bw{l}"))
        dh = _f2(matmul(vm2(dq), w["wq"], "nt", out_dtype=BF16, name=f"xa_q_bx{l}"))
        dkvb = dkv.astype(BF16)
        g_wkv = matmul(vm2(mn), vm2(dkvb), "tn", out_dtype=BF16, out_p=N_CHIPS, name=f"xa_kv_bw{l}")
        plan.grads_done({f"wo{l}": g_wo, f"wq{l}": g_wq, f"wkv{l}": g_wkv})
        dmn = _f2(matmul(vm2(dkvb), w["wkv"], "nt", out_dtype=F32, side=plan.take_bwd(), name=f"xa_kv_bx{l}"))
        (dgm,) = rmsnorm_bwd(mem, row(sp["xa_mem_norm"], l), dmn, None, name=f"xa_memnorm_b{l}")
        dx, dxb, dg = rmsnorm_bwd(xin, row(sp["xa_norm"], l), dh, dx, name=f"xa_norm_b{l}")
        return dx, dxb, dg, dgm

    dg_ffn, dg_xa, dg_mem = [None, None], [None, None], [None, None]
    dx, dxb, dg_ffn[1] = ffn_b(1, ff1, dx, dxb)
    dx, dxb, dg_xa[1], dg_mem[1] = xattn_b(1, xa1, dx, dxb)

    do = _f2(matmul(vm2(dxb), od["od_out"], "nt", out_dtype=BF16, side=plan.take_bwd(), name="od_out_bx"))
    g_od_out = rows4(matmul(vm2(of), vm2(dxb), "tn", out_dtype=BF16, name="od_out_bw"))
    dz3, delta = fox_bwd_dq(qkv, fk, do, lse, side=plan.take_bwd(1), name="fox_bq")
    dz3, dfk = fox_bwd_dkv(qkv, fk, do, lse, delta, dz3, side=plan.take_bwd(1), name="fox_bkv")
    dfc = jnp.pad(dfk.reshape(nh, t).T, ((0, 0), (0, zf.shape[1] - nh)))
    dzf, dbf = fl_bwd(dfc, zf, sp["bf"], name="od_forget_b")
    dqkv = VM(dz3, "cs", pfn=lambda p: lax.rem(p + 2, 3))
    dwqkv = _f2(matmul(vm2(ho), dqkv, "tn", out_dtype=BF16, name="od_qkv_bw"))
    dwf = _f2(matmul(vm2(ho), vm2(dzf), "tn", out_dtype=BF16, name="od_fl_bw"))
    od_in_full = jnp.concatenate([dwqkv, dwf[:, :nh]], axis=1)
    plan.grads_done({"od_out": g_od_out, "od_in": od_in_full.reshape(d, N_CHIPS, -1).transpose(1, 0, 2)})
    dh = matmul(dqkv, od["wqkv"], "nt", out_dtype=F32, side=plan.take_bwd(), name="od_qkv_bx")
    dh = _f2(matmul(vm2(dzf), od["wf"], "nt", out_dtype=BF16, res=VM(dh), name="od_fl_bx"))
    dx, dxb, small["od_norm"] = rmsnorm_bwd(x3, sp["od_norm"], dh, dx, name="od_norm_b")
    small["bf"] = dbf

    dx, dxb, dg_ffn[0] = ffn_b(0, ff0, dx, dxb)
    dx, dxb, dg_xa[0], dg_mem[0] = xattn_b(0, xa0, dx, dxb)

    dcat = matmul(vm2(dxb), ev["ev_out"], "nt", out_dtype=BF16, out_p=2, side=plan.take_bwd(), name="ev_out_bx")
    g_ev_out = rows4(matmul(VM(cat), vm2(dxb), "tn", out_dtype=BF16, name="ev_out_bw"))
    dz5, g_pool, small["pool_scale"] = pool_bwd(z, dcat, ev["pool"], sp["pool_scale"], name="pool_b")
    dz5, small["lb"], dgn = hgrn_bwd(z, dcat, dz5, states, sp["lb"], sp["hg_gain"], mix_a, side=plan.take_bwd(1), name="hgrn_b")
    small["hg_gain"] = jnp.sum(dgn, axis=0)
    dzv = VM(dz5, "cs", pfn=lambda p: lax.rem(p + 4, 5))
    g_ev_in = _f2(matmul(vm2(h0), dzv, "tn", out_dtype=BF16, side=plan.take_bwd(1), name="ev_in_bw"))
    g_ev_in = g_ev_in.reshape(d, N_CHIPS, -1).transpose(1, 0, 2)
    ng, gsz = g_pool.shape[0], g_pool.shape[1]
    pool_parts = g_pool.reshape(ng, N_CHIPS, gsz // N_CHIPS, gsz).transpose(1, 0, 2, 3).reshape(N_CHIPS, gsz, gsz).astype(BF16)
    plan.grads_done({"ev_out": g_ev_out, "pool": pool_parts, "ev_in": g_ev_in}, now=True)
    dh = _f2(matmul(dzv, ev["ev_in"], "nt", out_dtype=BF16, side=plan.take_bwd(1), name="ev_in_bx"))
    dx, _, small["ev_norm"] = rmsnorm_bwd(x0, sp["ev_norm"], dh, dx, name="ev_norm_b")

    small["xa_norm"] = jnp.concatenate(dg_xa, axis=0)
    small["xa_mem_norm"] = jnp.concatenate(dg_mem, axis=0)
    small["ffn_norm"] = jnp.concatenate(dg_ffn, axis=0)
    return loss, dx, small


def gather_forward(fulls, shards, *, name):
    n = len(fulls)
    side = ForwardSide(fulls, shards)

    def body(*refs):
        ins, outs = refs[:2 * n], refs[2 * n:3 * n]
        ssem, rsem = refs[3 * n:]
        side.start(ins, outs, ssem, rsem)
        side.finish(ins, outs, ssem, rsem)

    return pl.pallas_call(
        body, name=name, in_specs=[_ANY] * (2 * n), out_specs=[_ANY] * n, out_shape=side.out_shape,
        input_output_aliases=side.aliases,
        scratch_shapes=[pltpu.SemaphoreType.DMA((n, 4)), pltpu.SemaphoreType.DMA((n, 4))],
    )(*side.inputs)


def gather_shards(shards, *, name):
    n = len(shards)

    def body(*refs):
        ins, outs = refs[:n], refs[n:2 * n]
        ssem, rsem = refs[2 * n:]
        x, y, c, chips = _me()
        mine = _chip_id((x, y))
        sibling = (x, y, 1 - c)

        def rows(w, chip_id, which):
            h = shards[w].shape[0] // 2
            return outs[w].at[chip_id, pl.ds(which * h, h)]

        def to_chip(w, j):
            h = shards[w].shape[0] // 2
            return _rcopy(ins[w].at[pl.ds(c * h, h)], rows(w, mine, c), ssem.at[w, j], rsem.at[w, j], (*chips[j], c))

        def from_chip(w, j):
            r = rows(w, _chip_id(chips[j]), c)
            return _rcopy(r, r, ssem.at[w, j], rsem.at[w, j], (*chips[j], c))

        def to_sibling(w, j):
            r = rows(w, _chip_id(chips[j]), c)
            return _rcopy(r, r, ssem.at[w, 3 + j], rsem.at[w, 3 + j], sibling)

        def from_sibling(w, j):
            r = rows(w, _chip_id(chips[j]), 1 - c)
            return _rcopy(r, r, ssem.at[w, 3 + j], rsem.at[w, 3 + j], sibling)

        def own(w):
            return _rcopy(ins[w], outs[w].at[mine], ssem.at[w, 6], rsem.at[w, 6], sibling)

        for w in range(n):
            own(w).start()
            for j in range(3):
                to_chip(w, j).start()
        for w in range(n):
            for j in range(3):
                from_chip(w, j).wait_recv()
                to_sibling(w, j).start()
        for w in range(n):
            for j in range(3):
                from_sibling(w, j).wait_recv()
        for w in range(n):
            own(w).wait()
            for j in range(3):
                to_chip(w, j).wait_send()
                to_sibling(w, j).wait_send()

    return pl.pallas_call(
        body, name=name, in_specs=[_ANY] * n, out_specs=[_ANY] * n,
        out_shape=[jax.ShapeDtypeStruct((N_CHIPS,) + s.shape, s.dtype) for s in shards],
        scratch_shapes=[pltpu.SemaphoreType.DMA((n, 7)), pltpu.SemaphoreType.DMA((n, 7))],
    )(*shards)


def _ids_spec(grid, in_specs, out_specs):
    return pltpu.PrefetchScalarGridSpec(num_scalar_prefetch=1, grid=grid, in_specs=in_specs, out_specs=out_specs)


def rs_pair(parts, *, name):
    n = len(parts)

    def body(*refs):
        ins, recv = refs[:n], refs[n:2 * n]
        ssem, rsem = refs[2 * n:]
        x, y, c, _ = _me()
        sibling = (x, y, 1 - c)

        def swap(w):
            h = parts[w].shape[1] // 2
            return _rcopy(ins[w].at[:, pl.ds((1 - c) * h, h), :], recv[w], ssem.at[w], rsem.at[w], sibling)

        for w in range(n):
            swap(w).start()
        for w in range(n):
            swap(w).wait()

    return pl.pallas_call(
        body, name=name, in_specs=[_ANY] * n, out_specs=[_ANY] * n,
        out_shape=[jax.ShapeDtypeStruct((p.shape[0], p.shape[1] // 2, p.shape[2]), p.dtype) for p in parts],
        scratch_shapes=[pltpu.SemaphoreType.DMA((n,)), pltpu.SemaphoreType.DMA((n,))],
    )(*parts)


def add_pair(part, recv, ids, *, name):
    p, h, c = recv.shape
    br = _row_blk(h, 512)
    nb = h // br

    def body(ids_ref, a_ref, b_ref, o_ref):
        del ids_ref
        o_ref[...] = (a_ref[...].astype(F32) + b_ref[...].astype(F32)).astype(o_ref.dtype)

    half = pl.BlockSpec((None, br, c), lambda k, i, ids: (k, i, 0))
    return pl.pallas_call(
        body, name=name, out_shape=jax.ShapeDtypeStruct(recv.shape, recv.dtype),
        grid_spec=_ids_spec((p, nb), [pl.BlockSpec((None, br, c), lambda k, i, ids: (k, ids[1] * nb + i, 0)), half], half),
        compiler_params=_params(("parallel", "parallel")),
    )(ids, part, recv)


def rs_chip(sums, *, name):
    n = len(sums)

    def body(*refs):
        ins, outs = refs[:n], refs[n:2 * n]
        ssem, rsem = refs[2 * n:]
        x, y, c, chips = _me()

        def swap(w, j):
            return _rcopy(ins[w].at[_chip_id(chips[j])], outs[w].at[j], ssem.at[w, j], rsem.at[w, j], (*chips[j], c))

        for w in range(n):
            for j in range(3):
                swap(w, j).start()
        for w in range(n):
            for j in range(3):
                swap(w, j).wait()

    return pl.pallas_call(
        body, name=name, in_specs=[_ANY] * n, out_specs=[_ANY] * n,
        out_shape=[jax.ShapeDtypeStruct((3,) + s.shape[1:], s.dtype) for s in sums],
        scratch_shapes=[pltpu.SemaphoreType.DMA((n, 3)), pltpu.SemaphoreType.DMA((n, 3))],
    )(*sums)


def add_chips(sums, landed, ids, group, layer, group_shape, *, name):
    _, h, c = sums.shape
    br = _row_blk(h, 256)
    nb = h // br

    def body(ids_ref, a_ref, b_ref, *rest):
        o_ref = rest[-1]
        tot = a_ref[...].astype(F32)
        for k in range(3):
            tot = tot + b_ref[k].astype(F32)
        o_ref[...] = tot

    in_specs = [pl.BlockSpec((None, br, c), lambda i, ids: (ids[0], i, 0)), pl.BlockSpec((3, br, c), lambda i, ids: (0, i, 0))]
    args = [ids, sums, landed]
    if group is not None:
        in_specs.append(_ANY)
        args.append(group)
    return pl.pallas_call(
        body, name=name, out_shape=jax.ShapeDtypeStruct(group_shape, F32),
        input_output_aliases={3: 0} if group is not None else {},
        grid_spec=_ids_spec((nb,), in_specs, pl.BlockSpec((None, br, c), lambda i, ids: (layer, ids[1] * nb + i, 0))),
        compiler_params=_params(("parallel",)),
    )(*args)


def rs_share(groups, slots, *, name):
    ng = len(groups)
    n = len(slots)

    def body(*refs):
        outs = refs[ng:2 * ng]
        ssem, rsem = refs[2 * ng:]
        x, y, c, _ = _me()
        sibling = (x, y, 1 - c)

        def rows(w, which):
            g, l = slots[w]
            h = groups[g].shape[1] // 2
            return outs[g].at[l, pl.ds(which * h, h), :]

        def swap(w):
            return _rcopy(rows(w, c), rows(w, c), ssem.at[w], rsem.at[w], sibling)

        for w in range(n):
            swap(w).start()
        for w in range(n):
            swap(w).wait_send()
            _rcopy(rows(w, 1 - c), rows(w, 1 - c), ssem.at[w], rsem.at[w], sibling).wait_recv()

    return pl.pallas_call(
        body, name=name, in_specs=[_ANY] * ng, out_specs=[_ANY] * ng,
        out_shape=[jax.ShapeDtypeStruct(g.shape, g.dtype) for g in groups],
        input_output_aliases={g: g for g in range(ng)},
        scratch_shapes=[pltpu.SemaphoreType.DMA((n,)), pltpu.SemaphoreType.DMA((n,))],
    )(*groups)


def allreduce_small(v, *, name):
    r, c = v.shape
    ndev = 2 * N_CHIPS

    def body(v_ref, o_ref, buf, ssem, rsem):
        x, y, cc, _ = _me()
        me = 4 * x + 2 * y + cc
        flips = [(a, b, d) for a in (0, 1) for b in (0, 1) for d in (0, 1)][1:]
        buf[me] = v_ref[...]
        cps = []
        for k, (a, b, d) in enumerate(flips):
            peer = (jnp.bitwise_xor(x, a), jnp.bitwise_xor(y, b), jnp.bitwise_xor(cc, d))
            cp = _rcopy(v_ref, buf.at[me], ssem.at[k], rsem.at[k], peer)
            cp.start()
            cps.append(cp)
        for k, (a, b, d) in enumerate(flips):
            peer = (jnp.bitwise_xor(x, a), jnp.bitwise_xor(y, b), jnp.bitwise_xor(cc, d))
            src = 4 * peer[0] + 2 * peer[1] + peer[2]
            _rcopy(v_ref, buf.at[src], ssem.at[k], rsem.at[k], peer).wait_recv()
        for cp in cps:
            cp.wait_send()
        tot = buf[0]
        for k in range(1, ndev):
            tot = tot + buf[k]
        o_ref[...] = tot

    vm = pl.BlockSpec(memory_space=pltpu.VMEM)
    return pl.pallas_call(
        body, name=name, in_specs=[vm], out_specs=vm, out_shape=jax.ShapeDtypeStruct((r, c), F32),
        scratch_shapes=[pltpu.VMEM((ndev, r, c), F32), pltpu.SemaphoreType.DMA((ndev - 1,)), pltpu.SemaphoreType.DMA((ndev - 1,))],
    )(v)


WEIGHTS = ["lb_table", "ev_norm", "ev_w_in", "ev_w_pool", "ev_pool_scale", "ev_hg_norm", "ev_w_out", "od_norm", "od_w_in",
           "od_b_f", "od_w_out", "xa_norm", "xa_mem_norm", "xa_wq", "xa_wkv", "xa_wo", "ffn_norm", "ffn_w_gate", "ffn_w_up",
           "ffn_w_down", "final_norm"]
BIG = ["ev_w_in", "ev_w_pool", "ev_w_out", "od_w_in", "od_w_out", "xa_wq", "xa_wkv", "xa_wo", "ffn_w_gate", "ffn_w_up", "ffn_w_down"]
SMALL_ROWS = 16


def _rows(parts, width):
    rows = [jnp.pad(p.reshape(-1, p.shape[-1]).astype(F32), ((0, 0), (0, width - p.shape[-1]))) for p in parts]
    out = jnp.concatenate(rows, axis=0)
    return jnp.pad(out, ((0, SMALL_ROWS - out.shape[0]), (0, 0)))


def _unrows(packed, like):
    out, r = [], 0
    for p in like:
        n = p.size // p.shape[-1]
        out.append(packed[r:r + n, :p.shape[-1]].reshape(p.shape))
        r += n
    return out


def _m3(a):
    return a.reshape(a.shape[0], -1, a.shape[-1])


SLOT = {"ev_in": ("ev_w_in", 0), "pool": ("ev_w_pool", 0), "ev_out": ("ev_w_out", 0), "od_in": ("od_w_in", 0),
        "od_out": ("od_w_out", 0)}
for _l in range(2):
    SLOT.update({f"wq{_l}": ("xa_wq", _l), f"wkv{_l}": ("xa_wkv", _l), f"wo{_l}": ("xa_wo", _l),
                 f"gate{_l}": ("ffn_w_gate", _l), f"up{_l}": ("ffn_w_up", _l), f"down{_l}": ("ffn_w_down", _l)})
GATHER_FIRST = ["ev_in", "ev_out", "pool", "od_norm"]
GATHER_CARRIED = [["wq0", "wo0"], ["wkv0", "gate0"], ["od_out"], ["wq1"], ["wo1"], ["up0"], ["down0"], ["od_in"], ["wkv1"],
                  ["gate1", "up1", "down1"]]


class _Lazy:
    def __init__(self, plan, group):
        self.plan, self.layer = plan, group[-1] if group[-1] in "01" else ""

    def __getitem__(self, key):
        return self.plan.w(key + self.layer if key in ("wq", "wo", "wkv", "gate", "up", "down") else key)


class _Plan:
    def __init__(self, shards, ids, group_shapes, d, nh):
        self.shards, self.ids, self.group_shapes, self.d, self.nh = shards, ids, group_shapes, d, nh
        self.full, self.cache = {}, {}
        self.queue, self.sides, self.fsides, self.forwarded = [list(u) for u in GATHER_CARRIED], [], [], set()
        self.parts, self.psides, self.sums, self.rqueue, self.rsides = [], [], {}, [], []
        got = gather_shards([shards[n] for n in GATHER_FIRST], name="gather_first")
        for n, f in zip(GATHER_FIRST, got):
            self.full[n] = f

    def take_fwd(self):
        parts = []
        ready = [(ns, s) for ns, s in self.sides if s.outs is not None and ns[0] not in self.forwarded]
        for ns, s in ready:
            fs = ForwardSide(s.outs, [self.shards[n] for n in ns])
            self.fsides.append((ns, fs))
            self.forwarded.update(ns)
            parts.append(fs)
        if self.queue:
            names = self.queue.pop(0)
            side = GatherSide([self.shards[n] for n in names])
            self.sides.append((names, side))
            parts.append(side)
        return Sides(parts) if parts else None

    def _need(self, names):
        missing = [n for n in names if n not in self.full]
        if not missing:
            return
        done = {n: a for ns, s in self.fsides if s.outs is not None for n, a in zip(ns, s.outs)}
        landed = {n: a for ns, s in self.sides if s.outs is not None for n, a in zip(ns, s.outs)}
        pre = {n: done[n] for n in missing if n in done}
        half = [n for n in missing if n not in done and n in landed]
        late = [n for n in missing if n not in done and n not in landed]
        if half:
            self.forwarded.update(half)
            pre.update(zip(half, gather_forward([landed[n] for n in half], [self.shards[n] for n in half],
                                                name=f"gather_forward_{half[0]}")))
        if late:
            self.queue = [u for u in ([n for n in u if n not in late] for u in self.queue) if u]
            pre.update(zip(late, gather_shards([self.shards[n] for n in late], name=f"gather_late_{late[0]}")))
        for n in missing:
            self.full[n] = pre[n]

    def w(self, name):
        if name in self.cache:
            return self.cache[name]
        if name in ("wqkv", "wf"):
            self._need(["od_in"])
            od_full = self.full["od_in"].transpose(1, 0, 2).reshape(self.d, -1)
            self.cache["wqkv"] = vm2(od_full[:, :3 * self.d])
            self.cache["wf"] = vm2(jnp.pad(od_full[:, 3 * self.d:], ((0, 0), (0, 128 - self.nh))))
            return self.cache[name]
        self._need([name])
        f = self.full[name]
        if name == "pool":
            rows, gsz = f.shape[1:]
            ng = rows * N_CHIPS // gsz
            out = f.reshape(N_CHIPS, ng, gsz // N_CHIPS, gsz).transpose(1, 0, 2, 3).reshape(ng, gsz, gsz)
        elif name == "ev_in":
            out = vm2(f.transpose(1, 0, 2).reshape(self.d, -1))
        else:
            out = VM(f, "cs") if name.rstrip("01") in ("wkv", "gate", "up") else vm2(f.reshape(-1, f.shape[-1]))
        self.cache[name] = out
        return out

    def weights(self, group):
        return _Lazy(self, group)

    def grads_done(self, parts, now=False):
        names = list(parts)
        if now:
            got = rs_pair([parts[n] for n in names], name=f"reduce_pair_{names[0]}")
            for n, g in zip(names, got):
                self.sums[n] = add_pair(parts[n], g, self.ids, name=f"reduce_add2_{n}")
            self.rqueue.append(names)
        else:
            self.parts.append((names, [parts[n] for n in names]))

    def _add_swapped(self):
        for names, parts, side in self.psides:
            if side.outs is not None and names[0] not in self.sums:
                for n, p, g in zip(names, parts, side.outs):
                    self.sums[n] = add_pair(p, g, self.ids, name=f"reduce_add2_{n}")
                self.rqueue.append(names)

    def take_bwd(self, units=0):
        self._add_swapped()
        sides = []
        for names, parts in self.parts:
            ps = PairSide(parts)
            self.psides.append((names, parts, ps))
            sides.append(ps)
        self.parts = []
        names = [n for u in self.rqueue[:units] for n in u]
        self.rqueue = self.rqueue[units:]
        if names:
            rs = ReduceSide([self.sums[n] for n in names])
            self.rsides.append((names, rs))
            sides.append(rs)
        return Sides(sides) if sides else None

    def finish(self):
        for names, parts in self.parts:
            self.grads_done(dict(zip(names, parts)), now=True)
        self._add_swapped()
        landed = {}
        for ns, side in self.rsides:
            landed.update(zip(ns, side.outs))
        rest = [n for u in self.rqueue for n in u]
        if rest:
            landed.update(zip(rest, rs_chip([self.sums[n] for n in rest], name="reduce_chips_rest")))
        gbig = {n: None for n in BIG}
        for n, (big, l) in SLOT.items():
            gbig[big] = add_chips(self.sums[n], landed[n], self.ids, gbig[big], l, self.group_shapes[big], name=f"reduce_add4_{n}")
        full = rs_share([gbig[n] for n in BIG], [(BIG.index(big), l) for big, l in SLOT.values()], name="reduce_share")
        return dict(zip(BIG, full))


def kernel(x, mem, lb_table, ev_norm, ev_w_in, ev_w_pool, ev_pool_scale, ev_hg_norm, ev_w_out, od_norm, od_w_in, od_b_f, od_w_out, xa_norm, xa_mem_norm, xa_wq, xa_wkv, xa_wo, ffn_norm, ffn_w_gate, ffn_w_up, ffn_w_down, final_norm, loss_target, m_lb_table, m_ev_norm, m_ev_w_in, m_ev_w_pool, m_ev_pool_scale, m_ev_hg_norm, m_ev_w_out, m_od_norm, m_od_w_in, m_od_b_f, m_od_w_out, m_xa_norm, m_xa_mem_norm, m_xa_wq, m_xa_wkv, m_xa_wo, m_ffn_norm, m_ffn_w_gate, m_ffn_w_up, m_ffn_w_down, m_final_norm, v_lb_table, v_ev_norm, v_ev_w_in, v_ev_w_pool, v_ev_pool_scale, v_ev_hg_norm, v_ev_w_out, v_od_norm, v_od_w_in, v_od_b_f, v_od_w_out, v_xa_norm, v_xa_mem_norm, v_xa_wq, v_xa_wkv, v_xa_wo, v_ffn_norm, v_ffn_w_gate, v_ffn_w_up, v_ffn_w_down, v_final_norm):
    a = dict(locals())
    w = {n: a[n] for n in WEIGHTS}
    mom = {n: a["m_" + n] for n in WEIGHTS}
    var = {n: a["v_" + n] for n in WEIGHTS}
    _, t, d = x.shape
    nh = d // FOX_HEAD
    lanes = 128
    cx, cy = lax.axis_index("x"), lax.axis_index("y")
    chip = 2 * cx + cy

    w3 = {n: _m3(w[n]) for n in BIG}
    shards = {"od_norm": jnp.broadcast_to(od_norm, (16, od_norm.shape[1]))}
    for name, (big, l) in SLOT.items():
        shards[name] = w3[big][l].astype(BF16)
    ids = jnp.stack([chip, lax.axis_index("c")]).astype(jnp.int32)
    plan = _Plan(shards, ids, {n: w3[n].shape for n in BIG}, d, nh)
    od_norm_full = plan.full["od_norm"][:, 0, :].reshape(1, d)

    sm = jax.nn.softmax(lb_table, axis=0)
    sp = {
        "lb": sm[1:2], "ev_norm": ev_norm, "pool_scale": ev_pool_scale, "hg_gain": ev_hg_norm, "od_norm": od_norm_full,
        "bf": jnp.pad(od_b_f, ((0, 0), (0, lanes - nh))), "xa_norm": xa_norm, "xa_mem_norm": xa_mem_norm, "ffn_norm": ffn_norm,
        "final_norm": final_norm.reshape(1, d),
    }
    loss_l, gx, small = _local_step(x[0], mem[0], loss_target[0], sp, plan)
    loss = lax.psum(loss_l[0, 0], ("x", "y", "c"))
    gbig = plan.finish()

    raw_like = [small["lb"], small["ev_norm"], small["pool_scale"], small["hg_gain"], small["od_norm"], small["bf"],
                small["xa_norm"], small["xa_mem_norm"], small["ffn_norm"], small["final_norm"]]
    summed = _unrows(allreduce_small(_rows(raw_like, d), name="reduce_small"), raw_like)
    dlb, g_ev_norm, g_pool_scale, g_hg, g_od_norm_full, g_bf, g_xa, g_xam, g_ffn, g_final = summed
    dsm = jnp.zeros_like(sm).at[1:2].set(dlb)
    gsmall = {
        "lb_table": sm * (dsm - jnp.sum(sm * dsm, axis=0, keepdims=True)), "ev_norm": g_ev_norm, "ev_pool_scale": g_pool_scale,
        "ev_hg_norm": g_hg, "od_norm": lax.dynamic_slice_in_dim(g_od_norm_full, chip * od_norm.shape[1], od_norm.shape[1], axis=1),
        "od_b_f": g_bf[:, :nh], "xa_norm": g_xa, "xa_mem_norm": g_xam, "ffn_norm": g_ffn, "final_norm": g_final.reshape(d),
    }

    grad, delta, new_m, new_v = {}, {}, {}, {}
    for n in BIG:
        res = adamw(w3[n], gbig[n], _m3(mom[n]), _m3(var[n]), name=f"adamw_{n}")
        grad[n], delta[n], new_m[n], new_v[n] = [r.reshape(w[n].shape) for r in res]
    snames = [n for n in WEIGHTS if n not in BIG]
    like = [w[n] for n in snames]
    res = adamw(_rows(like, d)[None], _rows([gsmall[n] for n in snames], d)[None], _rows([mom[n] for n in snames], d)[None],
                _rows([var[n] for n in snames], d)[None], name="adamw_small")
    for vals, dst in zip(res, (grad, delta, new_m, new_v)):
        dst.update(zip(snames, _unrows(vals, like)))
    return (loss, gx.reshape(x.shape), *[grad[n] for n in WEIGHTS], *[delta[n] for n in WEIGHTS],
            *[new_m[n] for n in WEIGHTS], *[new_v[n] for n in WEIGHTS])
```

```python
import functools
import math

import jax
import jax.numpy as jnp
from jax import lax
from jax.experimental import pallas as pl
from jax.experimental.pallas import tpu as pltpu

F32 = jnp.float32
BF16 = jnp.bfloat16
MESH = pl.DeviceIdType.MESH

V7X_VMEM_LIMIT_BYTES = 56 * 1024 * 1024
N_CHIPS = 4

EPS = 1e-6
POOL_WINDOWS = (2, 4, 8, 16)
POOL_HALO = 16
HG_HEAD = 128
HG_CHUNK = 64
FOX_HEAD = 128
FOX_BLK = 512
XA_HEADS = 4
ADAM_LR, ADAM_B1, ADAM_B2, ADAM_EPS, ADAM_WD, ADAM_STEP = 0.001, 0.9, 0.999, 1e-08, 0.01, 10
EXP_CLAMP = 80.0


def _params(sem=None):
    return pltpu.CompilerParams(dimension_semantics=sem, vmem_limit_bytes=V7X_VMEM_LIMIT_BYTES)


def _blk(pref, dim):
    b = min(pref, dim)
    assert dim % b == 0, (pref, dim)
    return b


class VM:
    def __init__(self, arr, kind="cs", lead=(), inner=(), pfn=None):
        self.arr, self.kind, self.lead, self.inner = arr, kind, tuple(lead), tuple(inner)
        self.pfn = pfn or (lambda p: p)
        p = arr.shape[len(self.lead)]
        r, c = arr.shape[-2:]
        assert arr.ndim == len(self.lead) + 1 + len(self.inner) + 2, (arr.shape, lead, inner)
        self.P = p
        self.shape = (r, c * p) if kind == "cs" else (r * p, c)
        self.dtype = arr.dtype

    def spec(self, br, bc, rfn, cfn):
        p = self.P
        r, c = self.arr.shape[-2:]
        assert c % bc == 0 and r % br == 0, (self.arr.shape, br, bc)
        if p == 1:
            def imap(*g):
                return (*self.lead, self.pfn(0), *self.inner, rfn(*g), cfn(*g))
        elif self.kind == "cs":
            per = c // bc

            def imap(*g):
                cb = cfn(*g)
                return (*self.lead, self.pfn(lax.div(cb, per)), *self.inner, rfn(*g), lax.rem(cb, per))
        else:
            per = r // br

            def imap(*g):
                rb = rfn(*g)
                return (*self.lead, self.pfn(lax.div(rb, per)), *self.inner, lax.rem(rb, per), cfn(*g))
        return pl.BlockSpec((None,) * (self.arr.ndim - 2) + (br, bc), imap)


def vm2(arr):
    return VM(arr.reshape((1,) + arr.shape))


def _out_struct(shape, kind, p, dtype):
    r, c = shape
    return jax.ShapeDtypeStruct((p, r, c // p) if kind == "cs" else (p, r // p, c), dtype)


_ANY = pl.BlockSpec(memory_space=pl.ANY)


def _me():
    x, y, c = lax.axis_index("x"), lax.axis_index("y"), lax.axis_index("c")
    chips = [(1 - x, y), (x, 1 - y), (1 - x, 1 - y)]
    return x, y, c, chips


def _chip_id(xy):
    return 2 * xy[0] + xy[1]


def _rcopy(src, dst, ssem, rsem, dev):
    return pltpu.make_async_remote_copy(src_ref=src, dst_ref=dst, send_sem=ssem, recv_sem=rsem, device_id=dev,
                                        device_id_type=MESH)


class GatherSide:
    def __init__(self, shards):
        self.inputs = list(shards)
        self.out_shape = [jax.ShapeDtypeStruct((N_CHIPS,) + s.shape, s.dtype) for s in shards]
        self.aliases = {}
        self.rows = len(shards)
        self.outs = None

    def _copy(self, ins, outs, ssem, rsem, w, j, receive):
        x, y, c, chips = _me()
        h = self.inputs[w].shape[0] // 2
        half = pl.ds(c * h, h)
        if receive:
            r = outs[w].at[_chip_id(chips[j]), half]
            return _rcopy(r, r, ssem.at[w, j], rsem.at[w, j], (*chips[j], c))
        return _rcopy(ins[w].at[half], outs[w].at[_chip_id((x, y)), half], ssem.at[w, j], rsem.at[w, j], (*chips[j], c))

    def start(self, ins, outs, ssem, rsem):
        for w in range(len(self.inputs)):
            for j in range(3):
                self._copy(ins, outs, ssem, rsem, w, j, False).start()

    def finish(self, ins, outs, ssem, rsem):
        for w in range(len(self.inputs)):
            for j in range(3):
                self._copy(ins, outs, ssem, rsem, w, j, True).wait_recv()
                self._copy(ins, outs, ssem, rsem, w, j, False).wait_send()


class ForwardSide:
    def __init__(self, fulls, shards):
        self.inputs = list(fulls) + list(shards)
        self.out_shape = [jax.ShapeDtypeStruct(f.shape, f.dtype) for f in fulls]
        self.aliases = {w: w for w in range(len(fulls))}
        self.rows = len(fulls)
        self.outs = None

    def _copy(self, outs, ssem, rsem, w, j, receive):
        x, y, c, chips = _me()
        h = self.inputs[w].shape[1] // 2
        r = outs[w].at[_chip_id(chips[j]), pl.ds(((1 - c) if receive else c) * h, h)]
        return _rcopy(r, r, ssem.at[w, j], rsem.at[w, j], (x, y, 1 - c))

    def _own(self, ins, outs, ssem, rsem, w):
        x, y, c, _ = _me()
        return _rcopy(ins[self.rows + w], outs[w].at[_chip_id((x, y))], ssem.at[w, 3], rsem.at[w, 3], (x, y, 1 - c))

    def start(self, ins, outs, ssem, rsem):
        for w in range(self.rows):
            self._own(ins, outs, ssem, rsem, w).start()
            for j in range(3):
                self._copy(outs, ssem, rsem, w, j, False).start()

    def finish(self, ins, outs, ssem, rsem):
        for w in range(self.rows):
            self._own(ins, outs, ssem, rsem, w).wait()
            for j in range(3):
                self._copy(outs, ssem, rsem, w, j, False).wait_send()
                self._copy(outs, ssem, rsem, w, j, True).wait_recv()


class PairSide:
    def __init__(self, parts):
        self.inputs = list(parts)
        self.out_shape = [jax.ShapeDtypeStruct((p.shape[0], p.shape[1] // 2, p.shape[2]), p.dtype) for p in parts]
        self.aliases = {}
        self.rows = len(parts)
        self.outs = None

    def _copy(self, ins, outs, ssem, rsem, w):
        x, y, c, _ = _me()
        h = self.inputs[w].shape[1] // 2
        return _rcopy(ins[w].at[:, pl.ds((1 - c) * h, h), :], outs[w], ssem.at[w, 0], rsem.at[w, 0], (x, y, 1 - c))

    def start(self, ins, outs, ssem, rsem):
        for w in range(self.rows):
            self._copy(ins, outs, ssem, rsem, w).start()

    def finish(self, ins, outs, ssem, rsem):
        for w in range(self.rows):
            self._copy(ins, outs, ssem, rsem, w).wait()


class _SemRows:
    def __init__(self, sem, off):
        self.sem, self.off = sem, off

    @property
    def at(self):
        return self

    def __getitem__(self, idx):
        return self.sem.at[self.off + idx[0], idx[1]]


class Sides:
    def __init__(self, sides):
        self.sides = list(sides)
        self.inputs = [a for s in self.sides for a in s.inputs]
        self.out_shape = [o for s in self.sides for o in s.out_shape]
        self.rows = sum(s.rows for s in self.sides)
        self.aliases, i0, o0 = {}, 0, 0
        for s in self.sides:
            self.aliases.update({i0 + i: o0 + o for i, o in s.aliases.items()})
            i0, o0 = i0 + len(s.inputs), o0 + len(s.out_shape)

    def _each(self, method, ins, outs, ssem, rsem):
        i0 = o0 = r0 = 0
        for s in self.sides:
            getattr(s, method)(ins[i0:i0 + len(s.inputs)], outs[o0:o0 + len(s.out_shape)], _SemRows(ssem, r0), _SemRows(rsem, r0))
            i0, o0, r0 = i0 + len(s.inputs), o0 + len(s.out_shape), r0 + s.rows

    def start(self, ins, outs, ssem, rsem):
        self._each("start", ins, outs, ssem, rsem)

    def finish(self, ins, outs, ssem, rsem):
        self._each("finish", ins, outs, ssem, rsem)

    @property
    def outs(self):
        return None

    @outs.setter
    def outs(self, vals):
        o0 = 0
        for s in self.sides:
            s.outs = list(vals[o0:o0 + len(s.out_shape)])
            o0 += len(s.out_shape)


class ReduceSide:
    def __init__(self, sums):
        self.inputs = list(sums)
        self.out_shape = [jax.ShapeDtypeStruct((3,) + s.shape[1:], s.dtype) for s in sums]
        self.aliases = {}
        self.rows = len(sums)
        self.outs = None

    def _copy(self, ins, outs, ssem, rsem, w, j):
        _, _, c, chips = _me()
        return _rcopy(ins[w].at[_chip_id(chips[j])], outs[w].at[j], ssem.at[w, j], rsem.at[w, j], (*chips[j], c))

    def start(self, ins, outs, ssem, rsem):
        for w in range(len(self.inputs)):
            for j in range(3):
                self._copy(ins, outs, ssem, rsem, w, j).start()

    def finish(self, ins, outs, ssem, rsem):
        for w in range(len(self.inputs)):
            for j in range(3):
                self._copy(ins, outs, ssem, rsem, w, j).wait()


def _call(body, side, *, name, grid, in_specs, out_specs, out_shape, scratch_shapes=(), sem, aliases=None, args):
    if side is None:
        return pl.pallas_call(body, name=name, grid=grid, in_specs=in_specs, out_specs=out_specs, out_shape=out_shape,
                              scratch_shapes=list(scratch_shapes), input_output_aliases=aliases or {},
                              compiler_params=_params(sem))(*args)
    single = not isinstance(out_shape, (list, tuple))
    oshape, ospecs = ([out_shape], [out_specs]) if single else (list(out_shape), list(out_specs))
    n_in, n_out, s_in, s_out = len(in_specs), len(oshape), len(side.inputs), len(side.out_shape)

    def wrapped(*refs):
        ins, sin = refs[:n_in], refs[n_in:n_in + s_in]
        outs = refs[n_in + s_in:n_in + s_in + n_out]
        souts = refs[n_in + s_in + n_out:n_in + s_in + n_out + s_out]
        rest = refs[n_in + s_in + n_out + s_out:]
        scratch, (ssem, rsem) = rest[:-2], rest[-2:]
        first = functools.reduce(jnp.logical_and, [pl.program_id(a) == 0 for a in range(len(grid))])
        last = functools.reduce(jnp.logical_and, [pl.program_id(a) == grid[a] - 1 for a in range(len(grid))])

        @pl.when(first)
        def _():
            side.start(sin, souts, ssem, rsem)

        body(*ins, *outs, *scratch)

        @pl.when(last)
        def _():
            side.finish(sin, souts, ssem, rsem)

    sems = pltpu.SemaphoreType.DMA((side.rows, 4))
    res = pl.pallas_call(
        wrapped, name=name, grid=grid, in_specs=list(in_specs) + [_ANY] * s_in, out_specs=ospecs + [_ANY] * s_out,
        out_shape=oshape + side.out_shape, scratch_shapes=list(scratch_shapes) + [sems, sems],
        input_output_aliases={**(aliases or {}), **{n_in + i: n_out + o for i, o in side.aliases.items()}},
        compiler_params=_params(("arbitrary",) * len(grid)),
    )(*args, *side.inputs)
    side.outs = list(res[n_out:])
    return res[0] if single else list(res[:n_out])


def _best(g, cap):
    if g <= cap:
        return g
    cands = [d for d in range(128, cap + 1, 128) if g % d == 0]
    assert cands, (g, cap)
    return cands[-1]


def _row_blk(n, cap):
    cands = [d for d in range(16, min(n, cap) + 1, 16) if n % d == 0]
    assert cands, (n, cap)
    return cands[-1]


def _tiles(a, b, mode, out_kind, out_p, bm, bn, bk):
    def cpiece(v):
        return v.arr.shape[-1] if v.kind == "cs" else v.shape[1]

    def rpiece(v):
        return v.arr.shape[-2] if v.kind == "rs" else v.shape[0]

    if mode == "nn":
        m, n = a.shape[0], b.shape[1]
        gm, gn, gk = rpiece(a), cpiece(b), math.gcd(cpiece(a), rpiece(b))
    elif mode == "nt":
        m, n = a.shape[0], b.shape[0]
        gm, gn, gk = rpiece(a), rpiece(b), math.gcd(cpiece(a), cpiece(b))
    else:
        m, n = a.shape[1], b.shape[1]
        gm, gn, gk = cpiece(a), cpiece(b), math.gcd(rpiece(a), rpiece(b))
    if out_kind == "cs":
        gn = math.gcd(gn, n // out_p)
    else:
        gm = math.gcd(gm, m // out_p)
    caps = {"nn": (1024, 1536, 2048), "nt": (512, 2048, 2048), "tn": (1536, 1536, 2048)}[mode]
    return (bm or _best(gm, caps[0])), (bn or _best(gn, caps[1])), (bk or _best(gk, caps[2]))


def matmul(a, b, mode, *, out_dtype, bm=None, bn=None, bk=None, out_kind="cs", out_p=1, out_pfn=None, res=None, epi=None,
           side=None, name):
    bm, bn, bk = _tiles(a, b, mode, out_kind, out_p, bm, bn, bk)
    if mode == "nn":
        (m, k), (k2, n) = a.shape, b.shape
        a_spec = a.spec(bm, bk, lambda i, j, kk: i, lambda i, j, kk: kk)
        b_spec = b.spec(bk, bn, lambda i, j, kk: kk, lambda i, j, kk: j)
        dims = (((1,), (0,)), ((), ()))
    elif mode == "nt":
        (m, k), (n, k2) = a.shape, b.shape
        a_spec = a.spec(bm, bk, lambda i, j, kk: i, lambda i, j, kk: kk)
        b_spec = b.spec(bn, bk, lambda i, j, kk: j, lambda i, j, kk: kk)
        dims = (((1,), (1,)), ((), ()))
    else:
        (k, m), (k2, n) = a.shape, b.shape
        a_spec = a.spec(bk, bm, lambda i, j, kk: kk, lambda i, j, kk: i)
        b_spec = b.spec(bk, bn, lambda i, j, kk: kk, lambda i, j, kk: j)
        dims = (((0,), (0,)), ((), ()))
    assert k == k2, (a.shape, b.shape, mode)
    assert m % bm == 0 and n % bn == 0 and k % bk == 0, (m, n, k, bm, bn, bk)
    nk = k // bk
    out_sds = _out_struct((m, n), out_kind, out_p, out_dtype)
    out_vm = VM(out_sds, out_kind, pfn=out_pfn)
    o_spec = out_vm.spec(bm, bn, lambda i, j, kk: i, lambda i, j, kk: j)
    in_specs, args = [a_spec, b_spec], [a.arr, b.arr]
    tiles = ([res] if res is not None else []) + (list(epi[1]) if epi else [])
    for v in tiles:
        assert v.shape == (m, n)
        in_specs.append(v.spec(bm, bn, lambda i, j, kk: i, lambda i, j, kk: j))
        args.append(v.arr)
    n_out = epi[2] if epi else 1

    def body(a_ref, b_ref, *rest):
        t_refs, o_refs = rest[:len(tiles)], rest[len(tiles):len(tiles) + n_out]
        part = lax.dot_general(a_ref[...], b_ref[...], dims, preferred_element_type=F32)

        def write(tot):
            if res is not None:
                tot = tot + t_refs[0][...].astype(F32)
            outs = epi[0](tot, *[r[...].astype(F32) for r in t_refs[len(tiles) - len(epi[1]):]]) if epi else (tot,)
            for o_ref, val in zip(o_refs, outs):
                o_ref[...] = val.astype(o_ref.dtype)

        if nk == 1:
            write(part)
            return
        acc = rest[-1]
        kk = pl.program_id(2)

        @pl.when(kk == 0)
        def _():
            acc[...] = part

        @pl.when(kk > 0)
        def _():
            acc[...] += part

        @pl.when(kk == nk - 1)
        def _():
            write(acc[...])

    return _call(body, side, name=name, grid=(m // bm, n // bn, nk), in_specs=in_specs,
                 out_specs=o_spec if n_out == 1 else [o_spec] * n_out, out_shape=out_sds if n_out == 1 else [out_sds] * n_out,
                 scratch_shapes=[pltpu.VMEM((bm, bn), F32)] if nk > 1 else [],
                 sem=("parallel", "parallel", "arbitrary"), args=args)


def rmsnorm_fwd(x, g, *, name):
    t, d = x.shape
    bt = _blk(512, t)

    def body(x_ref, g_ref, o_ref):
        xv = x_ref[...]
        r = lax.rsqrt(jnp.mean(xv * xv, axis=-1, keepdims=True) + EPS)
        o_ref[...] = (xv * r * g_ref[...]).astype(o_ref.dtype)

    return pl.pallas_call(
        body, name=name, grid=(t // bt,),
        in_specs=[pl.BlockSpec((bt, d), lambda i: (i, 0)), pl.BlockSpec((1, d), lambda i: (0, 0))],
        out_specs=pl.BlockSpec((bt, d), lambda i: (i, 0)), out_shape=jax.ShapeDtypeStruct((t, d), BF16),
        compiler_params=_params(("parallel",)),
    )(x, g)


def rmsnorm_bwd(x, g, dh, dres, *, name):
    t, d = x.shape
    bt = _blk(256, t)
    want_dx = dres is not None

    def body(x_ref, g_ref, dh_ref, *rest):
        if want_dx:
            dres_ref, dx_ref, dxb_ref, dg_ref = rest
        else:
            (dg_ref,) = rest
        xv = x_ref[...]
        dhv = dh_ref[...].astype(F32)
        r = lax.rsqrt(jnp.mean(xv * xv, axis=-1, keepdims=True) + EPS)
        xh = xv * r
        part = jnp.sum(dhv * xh, axis=0, keepdims=True)

        @pl.when(pl.program_id(0) == 0)
        def _():
            dg_ref[...] = part

        @pl.when(pl.program_id(0) > 0)
        def _():
            dg_ref[...] += part

        if want_dx:
            dy = dhv * g_ref[...]
            dxn = r * (dy - xh * jnp.mean(dy * xh, axis=-1, keepdims=True))
            dx = dres_ref[...] + dxn
            dx_ref[...] = dx
            dxb_ref[...] = dx.astype(BF16)

    row = pl.BlockSpec((bt, d), lambda i: (i, 0))
    vec = pl.BlockSpec((1, d), lambda i: (0, 0))
    in_specs, args = [row, vec, row], [x, g, dh]
    out_specs, out_shape = [vec], [jax.ShapeDtypeStruct((1, d), F32)]
    if want_dx:
        in_specs.append(row)
        args.append(dres)
        out_specs = [row, row] + out_specs
        out_shape = [jax.ShapeDtypeStruct((t, d), F32), jax.ShapeDtypeStruct((t, d), BF16)] + out_shape
    return pl.pallas_call(
        body, name=name, grid=(t // bt,), in_specs=in_specs, out_specs=out_specs, out_shape=out_shape,
        compiler_params=_params(("arbitrary",)),
    )(*args)


def loss_head(x, g, tgt, *, name):
    t, d = x.shape
    bt = _blk(256, t)

    def body(x_ref, g_ref, t_ref, loss_ref, dx_ref, dxb_ref, dg_ref):
        xv = x_ref[...]
        gv = g_ref[...]
        r = lax.rsqrt(jnp.mean(xv * xv, axis=-1, keepdims=True) + EPS)
        xh = xv * r
        e = xh * gv - t_ref[...]
        lpart = jnp.zeros((1, 128), F32) + jnp.sum(e * e) * (0.5 / d)
        dyv = e * (1.0 / d)
        gpart = jnp.sum(dyv * xh, axis=0, keepdims=True)

        @pl.when(pl.program_id(0) == 0)
        def _():
            loss_ref[...] = lpart
            dg_ref[...] = gpart

        @pl.when(pl.program_id(0) > 0)
        def _():
            loss_ref[...] += lpart
            dg_ref[...] += gpart

        dy = dyv * gv
        dx = r * (dy - xh * jnp.mean(dy * xh, axis=-1, keepdims=True))
        dx_ref[...] = dx
        dxb_ref[...] = dx.astype(BF16)

    row = pl.BlockSpec((bt, d), lambda i: (i, 0))
    vec = pl.BlockSpec((1, d), lambda i: (0, 0))
    return pl.pallas_call(
        body, name=name, grid=(t // bt,), in_specs=[row, vec, row],
        out_specs=[pl.BlockSpec((1, 128), lambda i: (0, 0)), row, row, vec],
        out_shape=[jax.ShapeDtypeStruct((1, 128), F32), jax.ShapeDtypeStruct((t, d), F32),
                   jax.ShapeDtypeStruct((t, d), BF16), jax.ShapeDtypeStruct((1, d), F32)],
        compiler_params=_params(("arbitrary",)),
    )(x, g, tgt)


def _sigmoid(x):
    return 1.0 / (1.0 + jnp.exp(-x))


def _swiglu_epi(b, a):
    return b, a * _sigmoid(a) * b


def _swiglu_bwd_epi(ds, a, b):
    sg = _sigmoid(a)
    return ds * b * sg * (1.0 + a * (1.0 - sg)), ds * a * sg


def _xa_probs(qh, kh, scale):
    s = lax.dot_general(qh, kh, (((1,), (1,)), ((), ())), preferred_element_type=F32) * scale
    s = s - jnp.max(s, axis=-1, keepdims=True)
    p = jnp.exp(s)
    return p / jnp.sum(p, axis=-1, keepdims=True)


def xattn_fwd(q, kv, *, name):
    t, d = q.shape
    m = kv.shape[0]
    hd = d // XA_HEADS
    bt = _blk(512, t)
    scale = hd ** -0.5

    def body(q_ref, kv_ref, o_ref):
        for h in range(XA_HEADS):
            qh = q_ref[:, h * hd:(h + 1) * hd]
            kh = kv_ref[:, h * hd:(h + 1) * hd]
            vh = kv_ref[:, d + h * hd:d + (h + 1) * hd]
            p = _xa_probs(qh, kh, scale)
            o_ref[:, h * hd:(h + 1) * hd] = jnp.dot(p.astype(BF16), vh, preferred_element_type=F32).astype(BF16)

    return pl.pallas_call(
        body, name=name, grid=(t // bt,),
        in_specs=[pl.BlockSpec((bt, d), lambda i: (i, 0)), pl.BlockSpec((m, 2 * d), lambda i: (0, 0))],
        out_specs=pl.BlockSpec((bt, d), lambda i: (i, 0)), out_shape=jax.ShapeDtypeStruct((t, d), BF16),
        compiler_params=_params(("parallel",)),
    )(q, kv)


def xattn_bwd(q, kv, do, *, name):
    t, d = q.shape
    m = kv.shape[0]
    hd = d // XA_HEADS
    bt = _blk(512, t)
    scale = hd ** -0.5

    def body(q_ref, kv_ref, do_ref, dq_ref, dkv_ref):
        first = pl.program_id(0) == 0
        for h in range(XA_HEADS):
            qs, ks, vs = slice(h * hd, (h + 1) * hd), slice(h * hd, (h + 1) * hd), slice(d + h * hd, d + (h + 1) * hd)
            qh, kh, vh, doh = q_ref[:, qs], kv_ref[:, ks], kv_ref[:, vs], do_ref[:, qs]
            p = _xa_probs(qh, kh, scale)
            dp = lax.dot_general(doh, vh, (((1,), (1,)), ((), ())), preferred_element_type=F32)
            dsv = p * (dp - jnp.sum(p * dp, axis=-1, keepdims=True)) * scale
            dsb = dsv.astype(BF16)
            dq_ref[:, qs] = jnp.dot(dsb, kh, preferred_element_type=F32).astype(BF16)
            dk = lax.dot_general(dsb, qh, (((0,), (0,)), ((), ())), preferred_element_type=F32)
            dv = lax.dot_general(p.astype(BF16), doh, (((0,), (0,)), ((), ())), preferred_element_type=F32)

            @pl.when(first)
            def _():
                dkv_ref[:, ks] = dk
                dkv_ref[:, vs] = dv

            @pl.when(jnp.logical_not(first))
            def _():
                dkv_ref[:, ks] += dk
                dkv_ref[:, vs] += dv

    row = pl.BlockSpec((bt, d), lambda i: (i, 0))
    full = pl.BlockSpec((m, 2 * d), lambda i: (0, 0))
    return pl.pallas_call(
        body, name=name, grid=(t // bt,), in_specs=[row, full, row], out_specs=[row, full],
        out_shape=[jax.ShapeDtypeStruct((t, d), BF16), jax.ShapeDtypeStruct((m, 2 * d), F32)],
        compiler_params=_params(("arbitrary",)),
    )(q, kv, do)


def _pool_p(buf, uv, rows, w, bt):
    acc = uv
    for dd in range(1, w):
        acc = acc + buf[pl.ds(POOL_HALO - dd, bt), :]
    cnt = jnp.minimum(rows + 1, w).astype(F32)
    return acc / cnt - uv


def pool_fwd(z, w_pool, scale, *, name):
    t = z.shape[0]
    ng, gsz = w_pool.shape[0], w_pool.shape[1]
    mix = ng * gsz
    bt = _blk(512, t)

    def body(u_ref, uh_ref, w_ref, sc_ref, o_ref, buf):
        r = pl.program_id(0)
        rows = r * bt + lax.broadcasted_iota(jnp.int32, (bt, 1), 0)
        for g in range(ng):
            gs = slice(g * gsz, (g + 1) * gsz)
            uv = u_ref[:, gs]
            buf[0:POOL_HALO, :] = jnp.where(r > 0, uh_ref[:, gs], 0.0)
            buf[POOL_HALO:POOL_HALO + bt, :] = uv
            p = _pool_p(buf, uv, rows, POOL_WINDOWS[g], bt)
            y = jnp.dot(p.astype(BF16), w_ref[g], preferred_element_type=F32) * sc_ref[:, gs]
            o_ref[:, gs] = y.astype(BF16)

    hb = bt // POOL_HALO
    return pl.pallas_call(
        body, name=name, grid=(t // bt,),
        in_specs=[pl.BlockSpec((bt, mix), lambda i: (i, 0)),
                  pl.BlockSpec((POOL_HALO, mix), lambda i: (jnp.maximum(i * hb - 1, 0), 0)),
                  pl.BlockSpec((ng, gsz, gsz), lambda i: (0, 0, 0)), pl.BlockSpec((1, mix), lambda i: (0, 0))],
        out_specs=pl.BlockSpec((None, bt, mix), lambda i: (0, i, 0)),
        out_shape=jax.ShapeDtypeStruct((2, t, mix), BF16),
        scratch_shapes=[pltpu.VMEM((POOL_HALO + bt, gsz), F32)],
        compiler_params=_params(("parallel",)),
    )(z, z, w_pool, scale)


def pool_bwd(z, dcat, w_pool, scale, *, name):
    t = z.shape[0]
    ng, gsz = w_pool.shape[0], w_pool.shape[1]
    mix = ng * gsz
    bt = _blk(512, t)
    nb = t // bt
    nt_dims = (((1,), (1,)), ((), ()))
    tn_dims = (((0,), (0,)), ((), ()))

    def body(u_ref, uh_ref, dy_ref, dyh_ref, w_ref, sc_ref, du_ref, dw_ref, dsc_ref, buf, buf2):
        r = pl.program_id(0)
        first = r == 0
        rows = r * bt + lax.broadcasted_iota(jnp.int32, (bt, 1), 0)
        rows_h = (r + 1) * bt + lax.broadcasted_iota(jnp.int32, (POOL_HALO, 1), 0)
        for g in range(ng):
            w = POOL_WINDOWS[g]
            gs = slice(g * gsz, (g + 1) * gsz)
            uv = u_ref[:, gs]
            buf[0:POOL_HALO, :] = jnp.where(r > 0, uh_ref[:, gs], 0.0)
            buf[POOL_HALO:POOL_HALO + bt, :] = uv
            pb = _pool_p(buf, uv, rows, w, bt).astype(BF16)
            wg = w_ref[g]
            sc = sc_ref[:, gs]
            y0 = jnp.dot(pb, wg, preferred_element_type=F32)
            dyv = dy_ref[:, gs].astype(F32)
            dsc = jnp.sum(dyv * y0, axis=0, keepdims=True)
            dyw = (dyv * sc).astype(BF16)
            dw = lax.dot_general(pb, dyw, tn_dims, preferred_element_type=F32)

            @pl.when(first)
            def _():
                dw_ref[g] = dw
                dsc_ref[:, gs] = dsc

            @pl.when(jnp.logical_not(first))
            def _():
                dw_ref[g] += dw
                dsc_ref[:, gs] += dsc

            dp = lax.dot_general(dyw, wg, nt_dims, preferred_element_type=F32)
            dyh = (dyh_ref[:, gs].astype(F32) * sc).astype(BF16)
            dph = lax.dot_general(dyh, wg, nt_dims, preferred_element_type=F32)
            dph = jnp.where(r < nb - 1, dph, 0.0)
            buf2[0:bt, :] = dp / jnp.minimum(rows + 1, w).astype(F32)
            buf2[bt:bt + POOL_HALO, :] = dph / jnp.minimum(rows_h + 1, w).astype(F32)
            acc = buf2[pl.ds(0, bt), :]
            for dd in range(1, w):
                acc = acc + buf2[pl.ds(dd, bt), :]
            du_ref[:, gs] = (acc - dp).astype(BF16)

    hb = bt // POOL_HALO
    nhb = t // POOL_HALO
    return pl.pallas_call(
        body, name=name, grid=(nb,),
        in_specs=[pl.BlockSpec((bt, mix), lambda i: (i, 0)),
                  pl.BlockSpec((POOL_HALO, mix), lambda i: (jnp.maximum(i * hb - 1, 0), 0)),
                  pl.BlockSpec((None, bt, mix), lambda i: (0, i, 0)),
                  pl.BlockSpec((None, POOL_HALO, mix), lambda i: (0, jnp.minimum((i + 1) * hb, nhb - 1), 0)),
                  pl.BlockSpec((ng, gsz, gsz), lambda i: (0, 0, 0)), pl.BlockSpec((1, mix), lambda i: (0, 0))],
        out_specs=[pl.BlockSpec((None, bt, mix), lambda i: (4, i, 0)),
                   pl.BlockSpec((ng, gsz, gsz), lambda i: (0, 0, 0)), pl.BlockSpec((1, mix), lambda i: (0, 0))],
        out_shape=[jax.ShapeDtypeStruct((5, t, mix), BF16), jax.ShapeDtypeStruct((ng, gsz, gsz), F32),
                   jax.ShapeDtypeStruct((1, mix), F32)],
        scratch_shapes=[pltpu.VMEM((POOL_HALO + bt, gsz), F32), pltpu.VMEM((bt + POOL_HALO, gsz), F32)],
        compiler_params=_params(("arbitrary",)),
    )(z, z, dcat, dcat, w_pool, scale)


HG_HEADS_PER_STEP = 2
HG_LEVELS = ((64, 31), (32, 15), (16, 7))
HG_DIAG = (8, 3)
_NT = (((1,), (1,)), ((), ()))
_TN = (((0,), (0,)), ((), ()))
_HI = lax.Precision.HIGHEST


def _hg_masks():
    c = HG_CHUNK
    t = lax.broadcasted_iota(jnp.int32, (c, c), 0)
    s = lax.broadcasted_iota(jnp.int32, (c, c), 1)
    masks = []
    for blk, row in HG_LEVELS:
        sh = blk.bit_length() - 1
        same = (t >> sh) == (s >> sh)
        masks.append(same & ((t & (blk - 1)) > row) & ((s & (blk - 1)) <= row))
    sh = HG_DIAG[0].bit_length() - 1
    masks.append(((t >> sh) == (s >> sh)) & (s <= t))
    return t, s, masks


def _row_of_block(x, blk, row):
    c, n = x.shape
    x3 = x.reshape(c // blk, blk, n)
    return jnp.broadcast_to(x3[:, row:row + 1, :], x3.shape).reshape(c, n)


def _hg_parts(qv, flv, lb, masks, tri):
    sgf = _sigmoid(flv)
    f = lb + (1.0 - lb) * sgf
    logf = jnp.log(f)
    kk = 1.0 - f
    sgq = _sigmoid(qv)
    qf = qv * sgq * (HG_HEAD ** -0.5)
    bc = jnp.dot(tri, logf, preferred_element_type=F32, precision=_HI)
    levels = []
    a = None
    for li, (blk, row) in enumerate(HG_LEVELS + (HG_DIAG,)):
        e = bc - _row_of_block(bc, blk, row)
        if li < len(HG_LEVELS):
            eq, ek = jnp.exp(jnp.minimum(e, 0.0)), jnp.exp(jnp.minimum(-e, 0.0))
        else:
            eq, ek = jnp.exp(jnp.clip(e, -EXP_CLAMP, EXP_CLAMP)), jnp.exp(jnp.clip(-e, -EXP_CLAMP, EXP_CLAMP))
        qt, kt = qf * eq, kk * ek
        part = jnp.where(masks[li], lax.dot_general(qt.astype(BF16), kt.astype(BF16), _NT, preferred_element_type=F32), 0.0)
        a = part if a is None else a + part
        levels.append((eq, ek, qt, kt))
    return dict(sgf=sgf, f=f, kk=kk, sgq=sgq, qf=qf, bc=bc, levels=levels, a=a)


def hgrn_fwd(z, cat, lb, gain, mix_a, *, side=None, name):
    t = z.shape[0]
    mix_b = lb.shape[1]
    nh = mix_b // HG_HEAD
    bt = _blk(256, t)
    ncb = bt // HG_CHUNK
    dh = HG_HEAD

    def body(q_ref, fl_ref, i_ref, g_ref, lb_ref, gain_ref, cat_in, o_ref, st_ref, st):
        del cat_in

        @pl.when(pl.program_id(1) == 0)
        def _():
            st[...] = jnp.zeros_like(st)

        t_i, s_i, masks = _hg_masks()
        tri = (s_i <= t_i).astype(F32)
        lbv, gn = lb_ref[...], gain_ref[...]
        for c in range(ncb):
            rs = slice(c * HG_CHUNK, (c + 1) * HG_CHUNK)
            pr = _hg_parts(q_ref[rs, :], fl_ref[rs, :], lbv, masks, tri)
            vb = i_ref[rs, :].astype(BF16)
            stv = st[...]
            st_ref[c] = stv
            bc = pr["bc"]
            qt = pr["qf"] * jnp.exp(bc)
            o = (jnp.dot(pr["a"].astype(BF16), vb, preferred_element_type=F32)
                 + lax.dot_general(qt.astype(BF16), stv.astype(BF16), _NT, preferred_element_type=F32))
            bl = bc[HG_CHUNK - 1:HG_CHUNK, :]
            khat = pr["kk"] * jnp.exp(bl - bc)
            st[...] = stv * jnp.exp(bl) + lax.dot_general(vb, khat.astype(BF16), _TN, preferred_element_type=F32)
            r = lax.rsqrt(jnp.mean(o * o, axis=-1, keepdims=True) + EPS)
            gv = g_ref[rs, :]
            o_ref[rs, :] = (o * r * gn * (gv * _sigmoid(gv))).astype(BF16)

    def col(which):
        base = (mix_a + which * mix_b) // dh
        return pl.BlockSpec((bt, dh), lambda h, i: (i, base + h))

    return _call(
        body, side, name=name, grid=(nh, t // bt),
        in_specs=[col(0), col(1), col(2), col(3), pl.BlockSpec((1, dh), lambda h, i: (0, h)),
                  pl.BlockSpec((1, dh), lambda h, i: (0, 0)), _ANY],
        out_specs=[pl.BlockSpec((None, bt, dh), lambda h, i: (1, i, h)),
                   pl.BlockSpec((None, ncb, dh, dh), lambda h, i: (h, i, 0, 0))],
        out_shape=[jax.ShapeDtypeStruct(cat.shape, BF16), jax.ShapeDtypeStruct((nh, t // HG_CHUNK, dh, dh), F32)],
        scratch_shapes=[pltpu.VMEM((dh, dh), F32)], aliases={6: 0}, sem=("parallel", "arbitrary"),
        args=(z, z, z, z, lb, gain, cat))


def hgrn_bwd(z, dcat, dz5, states, lb, gain, mix_a, *, side=None, name):
    t = z.shape[0]
    mix_b = lb.shape[1]
    nh = mix_b // HG_HEAD
    bt = _blk(256, t)
    nb = t // bt
    ncb = bt // HG_CHUNK
    dh = HG_HEAD
    hp = HG_HEADS_PER_STEP if nh % HG_HEADS_PER_STEP == 0 else 1

    def body(q_ref, fl_ref, i_ref, g_ref, dy_ref, st_ref, lb_ref, gain_ref, dz_in, dz_ref, dlb_ref, dgn_ref, dst):
        del dz_in
        first = pl.program_id(1) == 0

        @pl.when(first)
        def _():
            dst[...] = jnp.zeros_like(dst)

        t_i, s_i, masks = _hg_masks()
        tri = (s_i <= t_i).astype(F32)
        triu = (s_i >= t_i).astype(F32)
        last_row = lax.broadcasted_iota(jnp.int32, (HG_CHUNK, 1), 0) == HG_CHUNK - 1
        gn = gain_ref[...]
        dlb_acc = [jnp.zeros((1, dh), F32) for _ in range(hp)]
        dgn_acc = [jnp.zeros((1, dh), F32) for _ in range(hp)]
        for c, hh in [(c, hh) for c in reversed(range(ncb)) for hh in range(hp)]:
            rs, cs = slice(c * HG_CHUNK, (c + 1) * HG_CHUNK), slice(hh * dh, (hh + 1) * dh)
            lbv = lb_ref[:, cs]
            qv, flv, gv = q_ref[rs, cs], fl_ref[rs, cs], g_ref[rs, cs]
            pr = _hg_parts(qv, flv, lbv, masks, tri)
            vb = i_ref[rs, cs].astype(BF16)
            stv = st_ref[hh, c]
            stb = stv.astype(BF16)
            dsv = dst[hh]
            dsb = dsv.astype(BF16)
            bc, kk, qf, ab = pr["bc"], pr["kk"], pr["qf"], pr["a"].astype(BF16)
            ebc = jnp.exp(bc)
            qt = qf * ebc
            qtb = qt.astype(BF16)
            o = jnp.dot(ab, vb, preferred_element_type=F32) + lax.dot_general(qtb, stb, _NT, preferred_element_type=F32)
            r = lax.rsqrt(jnp.mean(o * o, axis=-1, keepdims=True) + EPS)
            oh = o * r
            sgg = _sigmoid(gv)
            dyv = dy_ref[rs, cs].astype(F32)
            don = dyv * (gv * sgg)
            dgate = dyv * (oh * gn) * (sgg * (1.0 + gv * (1.0 - sgg)))
            dgn_acc[hh] = dgn_acc[hh] + jnp.sum(don * oh, axis=0, keepdims=True)
            doh = don * gn
            do = r * (doh - oh * jnp.mean(doh * oh, axis=-1, keepdims=True))
            dob = do.astype(BF16)
            bl = bc[HG_CHUNK - 1:HG_CHUNK, :]
            ebl = jnp.exp(bl)
            ekh = jnp.exp(bl - bc)
            khat = kk * ekh
            dv = (lax.dot_general(ab, dob, _TN, preferred_element_type=F32)
                  + lax.dot_general(khat.astype(BF16), dsb, _NT, preferred_element_type=F32))
            da = lax.dot_general(dob, vb, _NT, preferred_element_type=F32)
            dqt = jnp.dot(dob, stb, preferred_element_type=F32)
            dkh = jnp.dot(vb, dsb, preferred_element_type=F32)
            dst[hh] = dsv * ebl + lax.dot_general(dob, qtb, _TN, preferred_element_type=F32)
            dbl = jnp.sum(dsv * stv, axis=0, keepdims=True) * ebl + jnp.sum(dkh * khat, axis=0, keepdims=True)
            dqf = dqt * ebc
            dkk = dkh * ekh
            dbc = dqt * qt - dkh * khat
            for li, (eq, ek, qtl, ktl) in enumerate(pr["levels"]):
                gm = jnp.where(masks[li], da, 0.0).astype(BF16)
                qtr, ktr = qtl.astype(BF16), ktl.astype(BF16)
                dql = jnp.dot(gm, ktr, preferred_element_type=F32)
                dkl = lax.dot_general(gm, qtr, _TN, preferred_element_type=F32)
                dqf = dqf + dql * eq
                dkk = dkk + dkl * ek
                dbc = dbc + qtr.astype(F32) * dql - ktr.astype(F32) * dkl
            dbc = dbc + jnp.where(last_row, dbl, 0.0)
            dlogf = jnp.dot(triu, dbc, preferred_element_type=F32, precision=_HI)
            df = dlogf / pr["f"] - dkk
            sgf = pr["sgf"]
            dfl = df * (1.0 - lbv) * sgf * (1.0 - sgf)
            dlb_acc[hh] = dlb_acc[hh] + jnp.sum(df * (1.0 - sgf), axis=0, keepdims=True)
            sgq = pr["sgq"]
            dq = dqf * (HG_HEAD ** -0.5) * (sgq * (1.0 + qv * (1.0 - sgq)))
            dz_ref[0, rs, cs] = dq.astype(BF16)
            dz_ref[1, rs, cs] = dfl.astype(BF16)
            dz_ref[2, rs, cs] = dv.astype(BF16)
            dz_ref[3, rs, cs] = dgate.astype(BF16)

        @pl.when(first)
        def _():
            for hh in range(hp):
                dlb_ref[:, hh * dh:(hh + 1) * dh] = dlb_acc[hh]
                dgn_ref[hh] = dgn_acc[hh]

        @pl.when(jnp.logical_not(first))
        def _():
            for hh in range(hp):
                dlb_ref[:, hh * dh:(hh + 1) * dh] += dlb_acc[hh]
                dgn_ref[hh] += dgn_acc[hh]

    wd = hp * dh

    def col(which):
        base = (mix_a + which * mix_b) // wd
        return pl.BlockSpec((bt, wd), lambda h, i: (nb - 1 - i, base + h))

    return _call(
        body, side, name=name, grid=(nh // hp, nb),
        in_specs=[col(0), col(1), col(2), col(3),
                  pl.BlockSpec((None, bt, wd), lambda h, i: (1, nb - 1 - i, h)),
                  pl.BlockSpec((hp, ncb, dh, dh), lambda h, i: (h, nb - 1 - i, 0, 0)),
                  pl.BlockSpec((1, wd), lambda h, i: (0, h)), pl.BlockSpec((1, dh), lambda h, i: (0, 0)), _ANY],
        out_specs=[pl.BlockSpec((4, bt, wd), lambda h, i: (0, nb - 1 - i, h)),
                   pl.BlockSpec((1, wd), lambda h, i: (0, h)),
                   pl.BlockSpec((hp, 1, dh), lambda h, i: (h, 0, 0))],
        out_shape=[jax.ShapeDtypeStruct(dz5.shape, BF16), jax.ShapeDtypeStruct((1, mix_b), F32),
                   jax.ShapeDtypeStruct((nh, 1, dh), F32)],
        scratch_shapes=[pltpu.VMEM((hp, dh, dh), F32)], aliases={8: 0}, sem=("parallel", "arbitrary"),
        args=(z, z, z, z, dcat, states, lb, gain, dz5))


LOG2E = 1.4426950408889634


def _fox_q(qb):
    return (qb.astype(F32) * (FOX_HEAD ** -0.5 * LOG2E)).astype(BF16)


def _fox_scores(qs, kb, fk, masked):
    s = lax.dot_general(qs, kb, _NT, preferred_element_type=F32) - fk * LOG2E
    if masked:
        n = s.shape[0]
        row = lax.broadcasted_iota(jnp.int32, (n, n), 0)
        col = lax.broadcasted_iota(jnp.int32, (n, n), 1)
        s = jnp.where(col <= row, s, -jnp.inf)
    return s


def fox_fwd(qkv, fk, *, side=None, name):
    _, t, d = qkv.shape
    nh = d // FOX_HEAD
    b = _blk(FOX_BLK, t)
    nb = t // b
    dh = FOX_HEAD

    def body(q_ref, k_ref, v_ref, f_ref, o_ref, lse_ref):
        qi = pl.program_id(1)
        qs = _fox_q(q_ref[...])

        def step(kj, carry, masked):
            m, l, acc = carry
            off = pl.multiple_of(kj * b, b)
            s = _fox_scores(qs, k_ref[pl.ds(off, b), :], f_ref[kj], masked)
            m_new = jnp.maximum(m, jnp.max(s, axis=-1, keepdims=True))
            alpha = jnp.exp2(m - m_new)
            p = jnp.exp2(s - m_new)
            l = alpha * l + jnp.sum(p, axis=-1, keepdims=True)
            acc = alpha * acc + jnp.dot(p.astype(BF16), v_ref[pl.ds(off, b), :], preferred_element_type=F32)
            return m_new, l, acc

        init = (jnp.full((b, 1), -jnp.inf, F32), jnp.zeros((b, 1), F32), jnp.zeros((b, dh), F32))
        carry = lax.fori_loop(0, qi, lambda kj, c: step(kj, c, False), init)
        m, l, acc = step(qi, carry, True)
        o_ref[...] = (acc / l).astype(BF16)
        lse_ref[...] = m + jnp.log(l) * LOG2E

    return _call(
        body, side, name=name, grid=(nh, nb),
        in_specs=[pl.BlockSpec((None, b, dh), lambda h, i: (0, i, h)),
                  pl.BlockSpec((None, t, dh), lambda h, i: (1, 0, h)),
                  pl.BlockSpec((None, t, dh), lambda h, i: (2, 0, h)),
                  pl.BlockSpec((None, nb, 1, b), lambda h, i: (h, 0, 0, 0))],
        out_specs=[pl.BlockSpec((b, dh), lambda h, i: (i, h)), pl.BlockSpec((None, b, 1), lambda h, i: (h, i, 0))],
        out_shape=[jax.ShapeDtypeStruct((t, d), BF16), jax.ShapeDtypeStruct((nh, t, 1), F32)],
        sem=("parallel", "parallel"), args=(qkv, qkv, qkv, fk))


def fox_bwd_dq(qkv, fk, do, lse, *, side=None, name):
    _, t, d = qkv.shape
    nh = d // FOX_HEAD
    b = _blk(FOX_BLK, t)
    nb = t // b
    dh = FOX_HEAD
    scale = dh ** -0.5

    def body(q_ref, k_ref, v_ref, f_ref, do_ref, lse_ref, dq_ref, dl_ref, p_buf, dp_buf):
        qi = pl.program_id(1)
        qs, dob, lse_v = _fox_q(q_ref[...]), do_ref[...], lse_ref[...]

        def first(kj, dl, masked):
            off = pl.multiple_of(kj * b, b)
            p = jnp.exp2(_fox_scores(qs, k_ref[pl.ds(off, b), :], f_ref[kj], masked) - lse_v)
            dp = lax.dot_general(dob, v_ref[pl.ds(off, b), :], _NT, preferred_element_type=F32)
            p_buf[kj] = p
            dp_buf[kj] = dp
            return dl + jnp.sum(p * dp, axis=-1, keepdims=True)

        dl = lax.fori_loop(0, qi, lambda kj, c: first(kj, c, False), jnp.zeros((b, 1), F32))
        dl = first(qi, dl, True)
        dl_ref[...] = dl

        def second(kj, dq):
            off = pl.multiple_of(kj * b, b)
            dsv = p_buf[kj] * (dp_buf[kj] - dl)
            return dq + jnp.dot(dsv.astype(BF16), k_ref[pl.ds(off, b), :], preferred_element_type=F32)

        dq = lax.fori_loop(0, qi + 1, second, jnp.zeros((b, dh), F32))
        dq_ref[...] = (dq * scale).astype(BF16)

    col = pl.BlockSpec((None, b, 1), lambda h, i: (h, i, 0))
    return _call(
        body, side, name=name, grid=(nh, nb),
        in_specs=[pl.BlockSpec((None, b, dh), lambda h, i: (0, i, h)),
                  pl.BlockSpec((None, t, dh), lambda h, i: (1, 0, h)),
                  pl.BlockSpec((None, t, dh), lambda h, i: (2, 0, h)),
                  pl.BlockSpec((None, nb, 1, b), lambda h, i: (h, 0, 0, 0)),
                  pl.BlockSpec((b, dh), lambda h, i: (i, h)), col],
        out_specs=[pl.BlockSpec((None, b, dh), lambda h, i: (2, i, h)), col],
        out_shape=[jax.ShapeDtypeStruct((3, t, d), BF16), jax.ShapeDtypeStruct((nh, t, 1), F32)],
        scratch_shapes=[pltpu.VMEM((nb, b, b), F32), pltpu.VMEM((nb, b, b), F32)],
        sem=("parallel", "parallel"), args=(qkv, qkv, qkv, fk, do, lse))


def fox_bwd_dkv(qkv, fk, do, lse, delta, dqkv, *, side=None, name):
    _, t, d = qkv.shape
    nh = d // FOX_HEAD
    b = _blk(FOX_BLK, t)
    nb = t // b
    dh = FOX_HEAD
    scale = dh ** -0.5

    def body(q_ref, k_ref, v_ref, f_ref, do_ref, lse_ref, dl_ref, dz_in, dkv_ref, df_ref):
        del dz_in
        kj = pl.program_id(1)
        kb, vb, fkv = k_ref[...], v_ref[...], f_ref[...]

        def step(qi, carry, masked):
            dk, dv, df = carry
            off = pl.multiple_of(qi * b, b)
            qb, dob = q_ref[pl.ds(off, b), :], do_ref[pl.ds(off, b), :]
            p = jnp.exp2(_fox_scores(_fox_q(qb), kb, fkv, masked) - lse_ref[pl.ds(off, b), :])
            dv = dv + lax.dot_general(p.astype(BF16), dob, _TN, preferred_element_type=F32)
            dp = lax.dot_general(dob, vb, _NT, preferred_element_type=F32)
            dsv = p * (dp - dl_ref[pl.ds(off, b), :])
            dk = dk + lax.dot_general(dsv.astype(BF16), qb, _TN, preferred_element_type=F32)
            return dk, dv, df - jnp.sum(dsv, axis=0, keepdims=True)

        init = (jnp.zeros((b, dh), F32), jnp.zeros((b, dh), F32), jnp.zeros((1, b), F32))
        carry = step(kj, init, True)
        dk, dv, df = lax.fori_loop(kj + 1, nb, lambda qi, c: step(qi, c, False), carry)
        dkv_ref[0] = (dk * scale).astype(BF16)
        dkv_ref[1] = dv.astype(BF16)
        df_ref[...] = df

    col = pl.BlockSpec((None, t, 1), lambda h, j: (h, 0, 0))
    return _call(
        body, side, name=name, grid=(nh, nb),
        in_specs=[pl.BlockSpec((None, t, dh), lambda h, j: (0, 0, h)),
                  pl.BlockSpec((None, b, dh), lambda h, j: (1, j, h)),
                  pl.BlockSpec((None, b, dh), lambda h, j: (2, j, h)),
                  pl.BlockSpec((None, None, 1, b), lambda h, j: (h, j, 0, 0)),
                  pl.BlockSpec((t, dh), lambda h, j: (0, h)), col, col, pl.BlockSpec(memory_space=pl.ANY)],
        out_specs=[pl.BlockSpec((2, b, dh), lambda h, j: (0, j, h)),
                   pl.BlockSpec((None, None, 1, b), lambda h, j: (h, j, 0, 0))],
        out_shape=[jax.ShapeDtypeStruct((3, t, d), BF16), jax.ShapeDtypeStruct((nh, nb, 1, b), F32)],
        aliases={7: 0}, sem=("parallel", "parallel"), args=(qkv, qkv, qkv, fk, do, lse, delta, dqkv))


FL_BLK = 256


def _log_sigmoid(x):
    return jnp.minimum(x, 0.0) - jnp.log(1.0 + jnp.exp(-jnp.abs(x)))


def fl_fwd(zf, bf, *, name):
    t, n = zf.shape
    bt = _blk(FL_BLK, t)

    def body(z_ref, b_ref, o_ref, carry):
        @pl.when(pl.program_id(0) == 0)
        def _():
            carry[...] = jnp.zeros_like(carry)

        ls = _log_sigmoid(z_ref[...] + b_ref[...])
        r = lax.broadcasted_iota(jnp.int32, (bt, bt), 0)
        c = lax.broadcasted_iota(jnp.int32, (bt, bt), 1)
        cs = jnp.dot((c <= r).astype(F32), ls, preferred_element_type=F32, precision=_HI) + carry[...]
        o_ref[...] = cs
        carry[...] = cs[bt - 1:bt, :]

    return pl.pallas_call(
        body, name=name, grid=(t // bt,),
        in_specs=[pl.BlockSpec((bt, n), lambda i: (i, 0)), pl.BlockSpec((1, n), lambda i: (0, 0))],
        out_specs=pl.BlockSpec((bt, n), lambda i: (i, 0)), out_shape=jax.ShapeDtypeStruct((t, n), F32),
        scratch_shapes=[pltpu.VMEM((1, n), F32)], compiler_params=_params(("arbitrary",)),
    )(zf, bf)


def fl_bwd(df, zf, bf, *, name):
    t, n = zf.shape
    bt = _blk(FL_BLK, t)
    nb = t // bt

    def body(df_ref, z_ref, b_ref, dz_ref, db_ref, carry):
        first = pl.program_id(0) == 0

        @pl.when(first)
        def _():
            carry[...] = jnp.zeros_like(carry)

        r = lax.broadcasted_iota(jnp.int32, (bt, bt), 0)
        c = lax.broadcasted_iota(jnp.int32, (bt, bt), 1)
        dls = jnp.dot((c >= r).astype(F32), df_ref[...], preferred_element_type=F32, precision=_HI) + carry[...]
        carry[...] = dls[0:1, :]
        dz = dls * (1.0 - _sigmoid(z_ref[...] + b_ref[...]))
        dz_ref[...] = dz.astype(BF16)
        part = jnp.sum(dz, axis=0, keepdims=True)

        @pl.when(first)
        def _():
            db_ref[...] = part

        @pl.when(jnp.logical_not(first))
        def _():
            db_ref[...] += part

    row = pl.BlockSpec((bt, n), lambda i: (nb - 1 - i, 0))
    vec = pl.BlockSpec((1, n), lambda i: (0, 0))
    return pl.pallas_call(
        body, name=name, grid=(nb,), in_specs=[row, row, vec], out_specs=[row, vec],
        out_shape=[jax.ShapeDtypeStruct((t, n), BF16), jax.ShapeDtypeStruct((1, n), F32)],
        scratch_shapes=[pltpu.VMEM((1, n), F32)], compiler_params=_params(("arbitrary",)),
    )(df, zf, bf)


def _adamw_math(w, g, m, v):
    m = ADAM_B1 * m + (1.0 - ADAM_B1) * g
    v = ADAM_B2 * v + (1.0 - ADAM_B2) * (g * g)
    m_hat = m / (1.0 - ADAM_B1 ** ADAM_STEP)
    v_hat = v / (1.0 - ADAM_B2 ** ADAM_STEP)
    delta = -ADAM_LR * (m_hat / (jnp.sqrt(v_hat) + ADAM_EPS) + ADAM_WD * w)
    return delta, m, v


def adamw(w, g, m, v, *, side=None, name):
    nl, r, c = w.shape
    br = _row_blk(r, 256)
    nb = r // br

    def body(w_ref, g_ref, m_ref, v_ref, go_ref, d_ref, mo_ref, vo_ref):
        gv = g_ref[...]
        go_ref[...] = gv
        d_ref[...], mo_ref[...], vo_ref[...] = _adamw_math(w_ref[...], gv, m_ref[...], v_ref[...])

    return _call(body, side, name=name, grid=(nl, nb), in_specs=[pl.BlockSpec((None, br, c), lambda l, i: (l, i, 0))] * 4,
                 out_specs=[pl.BlockSpec((br, c), lambda l, i: (l * nb + i, 0))] * 4,
                 out_shape=[jax.ShapeDtypeStruct((nl * r, c), F32)] * 4, sem=("parallel", "parallel"), args=(w, g, m, v))


def _f2(a):
    return a.reshape(a.shape[-2:])


def _local_step(x0, mem, tgt, sp, plan):
    t, d = x0.shape
    mix_a = sp["pool_scale"].shape[1]
    small = {}

    def row(a, l):
        return a[l:l + 1]

    def rows4(g):
        return g.reshape(N_CHIPS, -1, g.shape[-1])

    def xattn_f(l, xin):
        w = plan.weights(f"xa{l}")
        hx = rmsnorm_fwd(xin, row(sp["xa_norm"], l), name=f"xa_norm_f{l}")
        q = _f2(matmul(vm2(hx), w["wq"], "nn", out_dtype=BF16, side=plan.take_fwd(), name=f"xa_q_f{l}"))
        mn = rmsnorm_fwd(mem, row(sp["xa_mem_norm"], l), name=f"xa_memnorm_f{l}")
        kv = _f2(matmul(vm2(mn), w["wkv"], "nn", out_dtype=BF16, name=f"xa_kv_f{l}"))
        o = xattn_fwd(q, kv, name=f"xa_attn_f{l}")
        xout = _f2(matmul(vm2(o), w["wo"], "nn", out_dtype=F32, res=vm2(xin), side=plan.take_fwd(), name=f"xa_o_f{l}"))
        return xout, (xin, hx, q, mn, kv, o)

    def ffn_f(l, xin):
        w = plan.weights(f"ffn{l}")
        hf = rmsnorm_fwd(xin, row(sp["ffn_norm"], l), name=f"ffn_norm_f{l}")
        a = _f2(matmul(vm2(hf), w["gate"], "nn", out_dtype=BF16, side=plan.take_fwd(), name=f"ffn_gate_f{l}"))
        b, s = matmul(vm2(hf), w["up"], "nn", out_dtype=BF16, epi=(_swiglu_epi, [vm2(a)], 2), side=plan.take_fwd(), name=f"ffn_up_f{l}")
        b, s = _f2(b), _f2(s)
        xout = _f2(matmul(vm2(s), w["down"], "nn", out_dtype=F32, res=vm2(xin), side=plan.take_fwd(), name=f"ffn_down_f{l}"))
        return xout, (xin, hf, a, b, s)

    ev = plan.weights("ev")
    h0 = rmsnorm_fwd(x0, sp["ev_norm"], name="ev_norm_f")
    z = _f2(matmul(vm2(h0), ev["ev_in"], "nn", out_dtype=F32, side=plan.take_fwd(), name="ev_in_f"))
    cat = pool_fwd(z, ev["pool"], sp["pool_scale"], name="pool_f")
    cat, states = hgrn_fwd(z, cat, sp["lb"], sp["hg_gain"], mix_a, side=plan.take_fwd(), name="hgrn_f")
    x1 = _f2(matmul(VM(cat), ev["ev_out"], "nn", out_dtype=F32, res=vm2(x0), side=plan.take_fwd(), name="ev_out_f"))
    x2, xa0 = xattn_f(0, x1)
    x3, ff0 = ffn_f(0, x2)

    od = plan.weights("od")
    ho = rmsnorm_fwd(x3, sp["od_norm"], name="od_norm_f")
    qkv = matmul(vm2(ho), od["wqkv"], "nn", out_dtype=BF16, out_p=3, side=plan.take_fwd(), name="od_qkv_f")
    zf = _f2(matmul(vm2(ho), od["wf"], "nn", out_dtype=F32, name="od_fl_f"))
    fcum = fl_fwd(zf, sp["bf"], name="od_forget_f")
    nh = d // FOX_HEAD
    nfb = t // _blk(FOX_BLK, t)
    fk = fcum[:, :nh].T.reshape(nh, nfb, 1, t // nfb)
    of, lse = fox_fwd(qkv, fk, side=plan.take_fwd(), name="fox_f")
    x4 = _f2(matmul(vm2(of), od["od_out"], "nn", out_dtype=F32, res=vm2(x3), name="od_out_f"))
    x5, xa1 = xattn_f(1, x4)
    x6, ff1 = ffn_f(1, x5)
    loss, dx, dxb, small["final_norm"] = loss_head(x6, sp["final_norm"], tgt, name="loss_head")

    def ffn_b(l, saved, dx, dxb):
        xin, hf, a, b, s = saved
        w = plan.weights(f"ffn{l}")
        da, db = matmul(vm2(dxb), w["down"], "nt", out_dtype=BF16, epi=(_swiglu_bwd_epi, [vm2(a), vm2(b)], 2), side=plan.take_bwd(1), name=f"ffn_down_bx{l}")
        da, db = _f2(da), _f2(db)
        g_down = rows4(matmul(vm2(s), vm2(dxb), "tn", out_dtype=BF16, name=f"ffn_down_bw{l}"))
        g_gate = matmul(vm2(hf), vm2(da), "tn", out_dtype=BF16, out_p=N_CHIPS, name=f"ffn_gate_bw{l}")
        g_up = matmul(vm2(hf), vm2(db), "tn", out_dtype=BF16, out_p=N_CHIPS, name=f"ffn_up_bw{l}")
        plan.grads_done({f"down{l}": g_down, f"gate{l}": g_gate, f"up{l}": g_up})
        dh = matmul(vm2(da), w["gate"], "nt", out_dtype=F32, side=plan.take_bwd(1), name=f"ffn_gate_bx{l}")
        dh = _f2(matmul(vm2(db), w["up"], "nt", out_dtype=BF16, res=VM(dh), side=plan.take_bwd(), name=f"ffn_up_bx{l}"))
        dx, dxb, dg = rmsnorm_bwd(xin, row(sp["ffn_norm"], l), dh, dx, name=f"ffn_norm_b{l}")
        return dx, dxb, dg

    def xattn_b(l, saved, dx, dxb):
        xin, hx, q, mn, kv, o = saved
        w = plan.weights(f"xa{l}")
        do = _f2(matmul(vm2(dxb), w["wo"], "nt", out_dtype=BF16, side=plan.take_bwd(), name=f"xa_o_bx{l}"))
        g_wo = rows4(matmul(vm2(o), vm2(dxb), "tn", out_dtype=BF16, name=f"xa_o_bw{l}"))
        dq, dkv = xattn_bwd(q, kv, do, name=f"xa_attn_b{l}")
        g_wq = rows4(matmul(vm2(hx), vm2(dq), "tn", out_dtype=BF16, name=f"xa_q_bw{l}"))
        dh = _f2(matmul(vm2(dq), w["wq"], "nt", out_dtype=BF16, name=f"xa_q_bx{l}"))
        dkvb = dkv.astype(BF16)
        g_wkv = matmul(vm2(mn), vm2(dkvb), "tn", out_dtype=BF16, out_p=N_CHIPS, name=f"xa_kv_bw{l}")
        plan.grads_done({f"wo{l}": g_wo, f"wq{l}": g_wq, f"wkv{l}": g_wkv})
        dmn = _f2(matmul(vm2(dkvb), w["wkv"], "nt", out_dtype=F32, side=plan.take_bwd(), name=f"xa_kv_bx{l}"))
        (dgm,) = rmsnorm_bwd(mem, row(sp["xa_mem_norm"], l), dmn, None, name=f"xa_memnorm_b{l}")
        dx, dxb, dg = rmsnorm_bwd(xin, row(sp["xa_norm"], l), dh, dx, name=f"xa_norm_b{l}")
        return dx, dxb, dg, dgm

    dg_ffn, dg_xa, dg_mem = [None, None], [None, None], [None, None]
    dx, dxb, dg_ffn[1] = ffn_b(1, ff1, dx, dxb)
    dx, dxb, dg_xa[1], dg_mem[1] = xattn_b(1, xa1, dx, dxb)

    do = _f2(matmul(vm2(dxb), od["od_out"], "nt", out_dtype=BF16, side=plan.take_bwd(), name="od_out_bx"))
    g_od_out = rows4(matmul(vm2(of), vm2(dxb), "tn", out_dtype=BF16, name="od_out_bw"))
    dz3, delta = fox_bwd_dq(qkv, fk, do, lse, side=plan.take_bwd(1), name="fox_bq")
    dz3, dfk = fox_bwd_dkv(qkv, fk, do, lse, delta, dz3, side=plan.take_bwd(1), name="fox_bkv")
    dfc = jnp.pad(dfk.reshape(nh, t).T, ((0, 0), (0, zf.shape[1] - nh)))
    dzf, dbf = fl_bwd(dfc, zf, sp["bf"], name="od_forget_b")
    dqkv = VM(dz3, "cs", pfn=lambda p: lax.rem(p + 2, 3))
    dwqkv = _f2(matmul(vm2(ho), dqkv, "tn", out_dtype=BF16, name="od_qkv_bw"))
    dwf = _f2(matmul(vm2(ho), vm2(dzf), "tn", out_dtype=BF16, name="od_fl_bw"))
    od_in_full = jnp.concatenate([dwqkv, dwf[:, :nh]], axis=1)
    plan.grads_done({"od_out": g_od_out})
    plan.grads_done({"od_in": od_in_full.reshape(d, N_CHIPS, -1).transpose(1, 0, 2)})
    dh = matmul(dqkv, od["wqkv"], "nt", out_dtype=F32, side=plan.take_bwd(), name="od_qkv_bx")
    dh = _f2(matmul(vm2(dzf), od["wf"], "nt", out_dtype=BF16, res=VM(dh), name="od_fl_bx"))
    dx, dxb, small["od_norm"] = rmsnorm_bwd(x3, sp["od_norm"], dh, dx, name="od_norm_b")
    small["bf"] = dbf

    dx, dxb, dg_ffn[0] = ffn_b(0, ff0, dx, dxb)
    dx, dxb, dg_xa[0], dg_mem[0] = xattn_b(0, xa0, dx, dxb)

    dcat = matmul(vm2(dxb), ev["ev_out"], "nt", out_dtype=BF16, out_p=2, side=plan.take_bwd(), name="ev_out_bx")
    g_ev_out = rows4(matmul(VM(cat), vm2(dxb), "tn", out_dtype=BF16, name="ev_out_bw"))
    dz5, g_pool, small["pool_scale"] = pool_bwd(z, dcat, ev["pool"], sp["pool_scale"], name="pool_b")
    dz5, small["lb"], dgn = hgrn_bwd(z, dcat, dz5, states, sp["lb"], sp["hg_gain"], mix_a, side=plan.take_bwd(1), name="hgrn_b")
    small["hg_gain"] = jnp.sum(dgn, axis=0)
    dzv = VM(dz5, "cs", pfn=lambda p: lax.rem(p + 4, 5))
    g_ev_in = _f2(matmul(vm2(h0), dzv, "tn", out_dtype=BF16, side=plan.take_bwd(1), name="ev_in_bw"))
    g_ev_in = g_ev_in.reshape(d, N_CHIPS, -1).transpose(1, 0, 2)
    ng, gsz = g_pool.shape[0], g_pool.shape[1]
    pool_parts = g_pool.reshape(ng, N_CHIPS, gsz // N_CHIPS, gsz).transpose(1, 0, 2, 3).reshape(N_CHIPS, gsz, gsz).astype(BF16)
    plan.grads_done({"ev_out": g_ev_out, "pool": pool_parts, "ev_in": g_ev_in}, now=True)
    dh = _f2(matmul(dzv, ev["ev_in"], "nt", out_dtype=BF16, side=plan.take_bwd(1), name="ev_in_bx"))
    dx, _, small["ev_norm"] = rmsnorm_bwd(x0, sp["ev_norm"], dh, dx, name="ev_norm_b")

    small["xa_norm"] = jnp.concatenate(dg_xa, axis=0)
    small["xa_mem_norm"] = jnp.concatenate(dg_mem, axis=0)
    small["ffn_norm"] = jnp.concatenate(dg_ffn, axis=0)
    return loss, dx, small


def gather_forward(fulls, shards, *, name):
    n = len(fulls)
    side = ForwardSide(fulls, shards)

    def body(*refs):
        ins, outs = refs[:2 * n], refs[2 * n:3 * n]
        ssem, rsem = refs[3 * n:]
        side.start(ins, outs, ssem, rsem)
        side.finish(ins, outs, ssem, rsem)

    return pl.pallas_call(
        body, name=name, in_specs=[_ANY] * (2 * n), out_specs=[_ANY] * n, out_shape=side.out_shape,
        input_output_aliases=side.aliases,
        scratch_shapes=[pltpu.SemaphoreType.DMA((n, 4)), pltpu.SemaphoreType.DMA((n, 4))],
    )(*side.inputs)


def gather_shards(shards, *, name):
    n = len(shards)

    def body(*refs):
        ins, outs = refs[:n], refs[n:2 * n]
        ssem, rsem = refs[2 * n:]
        x, y, c, chips = _me()
        mine = _chip_id((x, y))
        sibling = (x, y, 1 - c)

        def rows(w, chip_id, which):
            h = shards[w].shape[0] // 2
            return outs[w].at[chip_id, pl.ds(which * h, h)]

        def to_chip(w, j):
            h = shards[w].shape[0] // 2
            return _rcopy(ins[w].at[pl.ds(c * h, h)], rows(w, mine, c), ssem.at[w, j], rsem.at[w, j], (*chips[j], c))

        def from_chip(w, j):
            r = rows(w, _chip_id(chips[j]), c)
            return _rcopy(r, r, ssem.at[w, j], rsem.at[w, j], (*chips[j], c))

        def to_sibling(w, j):
            r = rows(w, _chip_id(chips[j]), c)
            return _rcopy(r, r, ssem.at[w, 3 + j], rsem.at[w, 3 + j], sibling)

        def from_sibling(w, j):
            r = rows(w, _chip_id(chips[j]), 1 - c)
            return _rcopy(r, r, ssem.at[w, 3 + j], rsem.at[w, 3 + j], sibling)

        def own(w):
            return _rcopy(ins[w], outs[w].at[mine], ssem.at[w, 6], rsem.at[w, 6], sibling)

        for w in range(n):
            own(w).start()
            for j in range(3):
                to_chip(w, j).start()
        for w in range(n):
            for j in range(3):
                from_chip(w, j).wait_recv()
                to_sibling(w, j).start()
        for w in range(n):
            for j in range(3):
                from_sibling(w, j).wait_recv()
        for w in range(n):
            own(w).wait()
            for j in range(3):
                to_chip(w, j).wait_send()
                to_sibling(w, j).wait_send()

    return pl.pallas_call(
        body, name=name, in_specs=[_ANY] * n, out_specs=[_ANY] * n,
        out_shape=[jax.ShapeDtypeStruct((N_CHIPS,) + s.shape, s.dtype) for s in shards],
        scratch_shapes=[pltpu.SemaphoreType.DMA((n, 7)), pltpu.SemaphoreType.DMA((n, 7))],
    )(*shards)


def _ids_spec(grid, in_specs, out_specs):
    return pltpu.PrefetchScalarGridSpec(num_scalar_prefetch=1, grid=grid, in_specs=in_specs, out_specs=out_specs)


def rs_pair(parts, *, name):
    n = len(parts)

    def body(*refs):
        ins, recv = refs[:n], refs[n:2 * n]
        ssem, rsem = refs[2 * n:]
        x, y, c, _ = _me()
        sibling = (x, y, 1 - c)

        def swap(w):
            h = parts[w].shape[1] // 2
            return _rcopy(ins[w].at[:, pl.ds((1 - c) * h, h), :], recv[w], ssem.at[w], rsem.at[w], sibling)

        for w in range(n):
            swap(w).start()
        for w in range(n):
            swap(w).wait()

    return pl.pallas_call(
        body, name=name, in_specs=[_ANY] * n, out_specs=[_ANY] * n,
        out_shape=[jax.ShapeDtypeStruct((p.shape[0], p.shape[1] // 2, p.shape[2]), p.dtype) for p in parts],
        scratch_shapes=[pltpu.SemaphoreType.DMA((n,)), pltpu.SemaphoreType.DMA((n,))],
    )(*parts)


def add_pair(part, recv, ids, *, name):
    p, h, c = recv.shape
    br = _row_blk(h, 512)
    nb = h // br

    def body(ids_ref, a_ref, b_ref, o_ref):
        del ids_ref
        o_ref[...] = (a_ref[...].astype(F32) + b_ref[...].astype(F32)).astype(o_ref.dtype)

    half = pl.BlockSpec((None, br, c), lambda k, i, ids: (k, i, 0))
    return pl.pallas_call(
        body, name=name, out_shape=jax.ShapeDtypeStruct(recv.shape, recv.dtype),
        grid_spec=_ids_spec((p, nb), [pl.BlockSpec((None, br, c), lambda k, i, ids: (k, ids[1] * nb + i, 0)), half], half),
        compiler_params=_params(("parallel", "parallel")),
    )(ids, part, recv)


def rs_chip(sums, *, name):
    n = len(sums)

    def body(*refs):
        ins, outs = refs[:n], refs[n:2 * n]
        ssem, rsem = refs[2 * n:]
        x, y, c, chips = _me()

        def swap(w, j):
            return _rcopy(ins[w].at[_chip_id(chips[j])], outs[w].at[j], ssem.at[w, j], rsem.at[w, j], (*chips[j], c))

        for w in range(n):
            for j in range(3):
                swap(w, j).start()
        for w in range(n):
            for j in range(3):
                swap(w, j).wait()

    return pl.pallas_call(
        body, name=name, in_specs=[_ANY] * n, out_specs=[_ANY] * n,
        out_shape=[jax.ShapeDtypeStruct((3,) + s.shape[1:], s.dtype) for s in sums],
        scratch_shapes=[pltpu.SemaphoreType.DMA((n, 3)), pltpu.SemaphoreType.DMA((n, 3))],
    )(*sums)


def add_chips(sums, landed, ids, group, layer, group_shape, *, name):
    _, h, c = sums.shape
    br = _row_blk(h, 256)
    nb = h // br

    def body(ids_ref, a_ref, b_ref, *rest):
        o_ref = rest[-1]
        tot = a_ref[...].astype(F32)
        for k in range(3):
            tot = tot + b_ref[k].astype(F32)
        o_ref[...] = tot

    in_specs = [pl.BlockSpec((None, br, c), lambda i, ids: (ids[0], i, 0)), pl.BlockSpec((3, br, c), lambda i, ids: (0, i, 0))]
    args = [ids, sums, landed]
    if group is not None:
        in_specs.append(_ANY)
        args.append(group)
    return pl.pallas_call(
        body, name=name, out_shape=jax.ShapeDtypeStruct(group_shape, F32),
        input_output_aliases={3: 0} if group is not None else {},
        grid_spec=_ids_spec((nb,), in_specs, pl.BlockSpec((None, br, c), lambda i, ids: (layer, ids[1] * nb + i, 0))),
        compiler_params=_params(("parallel",)),
    )(*args)


def rs_share(groups, slots, *, name):
    ng = len(groups)
    n = len(slots)

    def body(*refs):
        outs = refs[ng:2 * ng]
        ssem, rsem = refs[2 * ng:]
        x, y, c, _ = _me()
        sibling = (x, y, 1 - c)

        def rows(w, which):
            g, l = slots[w]
            h = groups[g].shape[1] // 2
            return outs[g].at[l, pl.ds(which * h, h), :]

        def swap(w):
            return _rcopy(rows(w, c), rows(w, c), ssem.at[w], rsem.at[w], sibling)

        for w in range(n):
            swap(w).start()
        for w in range(n):
            swap(w).wait_send()
            _rcopy(rows(w, 1 - c), rows(w, 1 - c), ssem.at[w], rsem.at[w], sibling).wait_recv()

    return pl.pallas_call(
        body, name=name, in_specs=[_ANY] * ng, out_specs=[_ANY] * ng,
        out_shape=[jax.ShapeDtypeStruct(g.shape, g.dtype) for g in groups],
        input_output_aliases={g: g for g in range(ng)},
        scratch_shapes=[pltpu.SemaphoreType.DMA((n,)), pltpu.SemaphoreType.DMA((n,))],
    )(*groups)


def allreduce_small(v, *, name):
    r, c = v.shape
    ndev = 2 * N_CHIPS

    def body(v_ref, o_ref, buf, ssem, rsem):
        x, y, cc, _ = _me()
        me = 4 * x + 2 * y + cc
        flips = [(a, b, d) for a in (0, 1) for b in (0, 1) for d in (0, 1)][1:]
        buf[me] = v_ref[...]
        cps = []
        for k, (a, b, d) in enumerate(flips):
            peer = (jnp.bitwise_xor(x, a), jnp.bitwise_xor(y, b), jnp.bitwise_xor(cc, d))
            cp = _rcopy(v_ref, buf.at[me], ssem.at[k], rsem.at[k], peer)
            cp.start()
            cps.append(cp)
        for k, (a, b, d) in enumerate(flips):
            peer = (jnp.bitwise_xor(x, a), jnp.bitwise_xor(y, b), jnp.bitwise_xor(cc, d))
            src = 4 * peer[0] + 2 * peer[1] + peer[2]
            _rcopy(v_ref, buf.at[src], ssem.at[k], rsem.at[k], peer).wait_recv()
        for cp in cps:
            cp.wait_send()
        tot = buf[0]
        for k in range(1, ndev):
            tot = tot + buf[k]
        o_ref[...] = tot

    vm = pl.BlockSpec(memory_space=pltpu.VMEM)
    return pl.pallas_call(
        body, name=name, in_specs=[vm], out_specs=vm, out_shape=jax.ShapeDtypeStruct((r, c), F32),
        scratch_shapes=[pltpu.VMEM((ndev, r, c), F32), pltpu.SemaphoreType.DMA((ndev - 1,)), pltpu.SemaphoreType.DMA((ndev - 1,))],
    )(v)


WEIGHTS = ["lb_table", "ev_norm", "ev_w_in", "ev_w_pool", "ev_pool_scale", "ev_hg_norm", "ev_w_out", "od_norm", "od_w_in",
           "od_b_f", "od_w_out", "xa_norm", "xa_mem_norm", "xa_wq", "xa_wkv", "xa_wo", "ffn_norm", "ffn_w_gate", "ffn_w_up",
           "ffn_w_down", "final_norm"]
BIG = ["ev_w_in", "ev_w_pool", "ev_w_out", "od_w_in", "od_w_out", "xa_wq", "xa_wkv", "xa_wo", "ffn_w_gate", "ffn_w_up", "ffn_w_down"]
SMALL_ROWS = 16


def _rows(parts, width):
    rows = [jnp.pad(p.reshape(-1, p.shape[-1]).astype(F32), ((0, 0), (0, width - p.shape[-1]))) for p in parts]
    out = jnp.concatenate(rows, axis=0)
    return jnp.pad(out, ((0, SMALL_ROWS - out.shape[0]), (0, 0)))


def _unrows(packed, like):
    out, r = [], 0
    for p in like:
        n = p.size // p.shape[-1]
        out.append(packed[r:r + n, :p.shape[-1]].reshape(p.shape))
        r += n
    return out


def _m3(a):
    return a.reshape(a.shape[0], -1, a.shape[-1])


SLOT = {"ev_in": ("ev_w_in", 0), "pool": ("ev_w_pool", 0), "ev_out": ("ev_w_out", 0), "od_in": ("od_w_in", 0),
        "od_out": ("od_w_out", 0)}
for _l in range(2):
    SLOT.update({f"wq{_l}": ("xa_wq", _l), f"wkv{_l}": ("xa_wkv", _l), f"wo{_l}": ("xa_wo", _l),
                 f"gate{_l}": ("ffn_w_gate", _l), f"up{_l}": ("ffn_w_up", _l), f"down{_l}": ("ffn_w_down", _l)})
GATHER_FIRST = ["ev_in", "ev_out", "pool", "od_norm"]
GATHER_CARRIED = [["wq0", "wo0"], ["wkv0", "gate0"], ["od_out"], ["wq1"], ["wo1"], ["up0"], ["down0"], ["od_in"], ["wkv1"],
                  ["gate1", "up1", "down1"]]


class _Lazy:
    def __init__(self, plan, group):
        self.plan, self.layer = plan, group[-1] if group[-1] in "01" else ""

    def __getitem__(self, key):
        return self.plan.w(key + self.layer if key in ("wq", "wo", "wkv", "gate", "up", "down") else key)


class _Plan:
    def __init__(self, shards, ids, group_shapes, d, nh):
        self.shards, self.ids, self.group_shapes, self.d, self.nh = shards, ids, group_shapes, d, nh
        self.full, self.cache = {}, {}
        self.queue, self.sides, self.fsides, self.forwarded = [list(u) for u in GATHER_CARRIED], [], [], set()
        self.parts, self.psides, self.sums, self.rqueue, self.rsides = [], [], {}, [], []
        got = gather_shards([shards[n] for n in GATHER_FIRST], name="gather_first")
        for n, f in zip(GATHER_FIRST, got):
            self.full[n] = f

    def take_fwd(self):
        parts = []
        ready = [(ns, s) for ns, s in self.sides if s.outs is not None and ns[0] not in self.forwarded]
        for ns, s in ready:
            fs = ForwardSide(s.outs, [self.shards[n] for n in ns])
            self.fsides.append((ns, fs))
            self.forwarded.update(ns)
            parts.append(fs)
        if self.queue:
            names = self.queue.pop(0)
            side = GatherSide([self.shards[n] for n in names])
            self.sides.append((names, side))
            parts.append(side)
        return Sides(parts) if parts else None

    def _need(self, names):
        missing = [n for n in names if n not in self.full]
        if not missing:
            return
        done = {n: a for ns, s in self.fsides if s.outs is not None for n, a in zip(ns, s.outs)}
        landed = {n: a for ns, s in self.sides if s.outs is not None for n, a in zip(ns, s.outs)}
        pre = {n: done[n] for n in missing if n in done}
        half = [n for n in missing if n not in done and n in landed]
        late = [n for n in missing if n not in done and n not in landed]
        if half:
            self.forwarded.update(half)
            pre.update(zip(half, gather_forward([landed[n] for n in half], [self.shards[n] for n in half],
                                                name=f"gather_forward_{half[0]}")))
        if late:
            self.queue = [u for u in ([n for n in u if n not in late] for u in self.queue) if u]
            pre.update(zip(late, gather_shards([self.shards[n] for n in late], name=f"gather_late_{late[0]}")))
        for n in missing:
            self.full[n] = pre[n]

    def w(self, name):
        if name in self.cache:
            return self.cache[name]
        if name in ("wqkv", "wf"):
            self._need(["od_in"])
            od_full = self.full["od_in"].transpose(1, 0, 2).reshape(self.d, -1)
            self.cache["wqkv"] = vm2(od_full[:, :3 * self.d])
            self.cache["wf"] = vm2(jnp.pad(od_full[:, 3 * self.d:], ((0, 0), (0, 128 - self.nh))))
            return self.cache[name]
        self._need([name])
        f = self.full[name]
        if name == "pool":
            rows, gsz = f.shape[1:]
            ng = rows * N_CHIPS // gsz
            out = f.reshape(N_CHIPS, ng, gsz // N_CHIPS, gsz).transpose(1, 0, 2, 3).reshape(ng, gsz, gsz)
        elif name == "ev_in":
            out = vm2(f.transpose(1, 0, 2).reshape(self.d, -1))
        else:
            out = VM(f, "cs") if name.rstrip("01") in ("wkv", "gate", "up") else vm2(f.reshape(-1, f.shape[-1]))
        self.cache[name] = out
        return out

    def weights(self, group):
        return _Lazy(self, group)

    def grads_done(self, parts, now=False):
        names = list(parts)
        if now:
            got = rs_pair([parts[n] for n in names], name=f"reduce_pair_{names[0]}")
            for n, g in zip(names, got):
                self.sums[n] = add_pair(parts[n], g, self.ids, name=f"reduce_add2_{n}")
            self.rqueue.append(names)
        else:
            self.parts.append((names, [parts[n] for n in names]))

    def _add_swapped(self):
        for names, parts, side in self.psides:
            if side.outs is not None and names[0] not in self.sums:
                for n, p, g in zip(names, parts, side.outs):
                    self.sums[n] = add_pair(p, g, self.ids, name=f"reduce_add2_{n}")
                self.rqueue.append(names)

    def take_bwd(self, units=0):
        self._add_swapped()
        sides = []
        for names, parts in self.parts:
            ps = PairSide(parts)
            self.psides.append((names, parts, ps))
            sides.append(ps)
        self.parts = []
        names = [n for u in self.rqueue[:units] for n in u]
        self.rqueue = self.rqueue[units:]
        if names:
            rs = ReduceSide([self.sums[n] for n in names])
            self.rsides.append((names, rs))
            sides.append(rs)
        return Sides(sides) if sides else None

    def finish(self):
        for names, parts in self.parts:
            self.grads_done(dict(zip(names, parts)), now=True)
        self._add_swapped()
        landed = {}
        for ns, side in self.rsides:
            landed.update(zip(ns, side.outs))
        rest = [n for u in self.rqueue for n in u]
        if rest:
            landed.update(zip(rest, rs_chip([self.sums[n] for n in rest], name="reduce_chips_rest")))
        gbig = {n: None for n in BIG}
        for n, (big, l) in SLOT.items():
            gbig[big] = add_chips(self.sums[n], landed[n], self.ids, gbig[big], l, self.group_shapes[big], name=f"reduce_add4_{n}")
        full = rs_share([gbig[n] for n in BIG], [(BIG.index(big), l) for big, l in SLOT.values()], name="reduce_share")
        return dict(zip(BIG, full))


def kernel(x, mem, lb_table, ev_norm, ev_w_in, ev_w_pool, ev_pool_scale, ev_hg_norm, ev_w_out, od_norm, od_w_in, od_b_f, od_w_out, xa_norm, xa_mem_norm, xa_wq, xa_wkv, xa_wo, ffn_norm, ffn_w_gate, ffn_w_up, ffn_w_down, final_norm, loss_target, m_lb_table, m_ev_norm, m_ev_w_in, m_ev_w_pool, m_ev_pool_scale, m_ev_hg_norm, m_ev_w_out, m_od_norm, m_od_w_in, m_od_b_f, m_od_w_out, m_xa_norm, m_xa_mem_norm, m_xa_wq, m_xa_wkv, m_xa_wo, m_ffn_norm, m_ffn_w_gate, m_ffn_w_up, m_ffn_w_down, m_final_norm, v_lb_table, v_ev_norm, v_ev_w_in, v_ev_w_pool, v_ev_pool_scale, v_ev_hg_norm, v_ev_w_out, v_od_norm, v_od_w_in, v_od_b_f, v_od_w_out, v_xa_norm, v_xa_mem_norm, v_xa_wq, v_xa_wkv, v_xa_wo, v_ffn_norm, v_ffn_w_gate, v_ffn_w_up, v_ffn_w_down, v_final_norm):
    a = dict(locals())
    w = {n: a[n] for n in WEIGHTS}
    mom = {n: a["m_" + n] for n in WEIGHTS}
    var = {n: a["v_" + n] for n in WEIGHTS}
    _, t, d = x.shape
    nh = d // FOX_HEAD
    lanes = 128
    cx, cy = lax.axis_index("x"), lax.axis_index("y")
    chip = 2 * cx + cy

    w3 = {n: _m3(w[n]) for n in BIG}
    shards = {"od_norm": jnp.broadcast_to(od_norm, (16, od_norm.shape[1]))}
    for name, (big, l) in SLOT.items():
        shards[name] = w3[big][l].astype(BF16)
    ids = jnp.stack([chip, lax.axis_index("c")]).astype(jnp.int32)
    plan = _Plan(shards, ids, {n: w3[n].shape for n in BIG}, d, nh)
    od_norm_full = plan.full["od_norm"][:, 0, :].reshape(1, d)

    sm = jax.nn.softmax(lb_table, axis=0)
    sp = {
        "lb": sm[1:2], "ev_norm": ev_norm, "pool_scale": ev_pool_scale, "hg_gain": ev_hg_norm, "od_norm": od_norm_full,
        "bf": jnp.pad(od_b_f, ((0, 0), (0, lanes - nh))), "xa_norm": xa_norm, "xa_mem_norm": xa_mem_norm, "ffn_norm": ffn_norm,
        "final_norm": final_norm.reshape(1, d),
    }
    loss_l, gx, small = _local_step(x[0], mem[0], loss_target[0], sp, plan)
    loss = lax.psum(loss_l[0, 0], ("x", "y", "c"))
    gbig = plan.finish()

    raw_like = [small["lb"], small["ev_norm"], small["pool_scale"], small["hg_gain"], small["od_norm"], small["bf"],
                small["xa_norm"], small["xa_mem_norm"], small["ffn_norm"], small["final_norm"]]
    summed = _unrows(allreduce_small(_rows(raw_like, d), name="reduce_small"), raw_like)
    dlb, g_ev_norm, g_pool_scale, g_hg, g_od_norm_full, g_bf, g_xa, g_xam, g_ffn, g_final = summed
    dsm = jnp.zeros_like(sm).at[1:2].set(dlb)
    gsmall = {
        "lb_table": sm * (dsm - jnp.sum(sm * dsm, axis=0, keepdims=True)), "ev_norm": g_ev_norm, "ev_pool_scale": g_pool_scale,
        "ev_hg_norm": g_hg, "od_norm": lax.dynamic_slice_in_dim(g_od_norm_full, chip * od_norm.shape[1], od_norm.shape[1], axis=1),
        "od_b_f": g_bf[:, :nh], "xa_norm": g_xa, "xa_mem_norm": g_xam, "ffn_norm": g_ffn, "final_norm": g_final.reshape(d),
    }

    grad, delta, new_m, new_v = {}, {}, {}, {}
    for n in BIG:
        res = adamw(w3[n], gbig[n], _m3(mom[n]), _m3(var[n]), name=f"adamw_{n}")
        grad[n], delta[n], new_m[n], new_v[n] = [r.reshape(w[n].shape) for r in res]
    snames = [n for n in WEIGHTS if n not in BIG]
    like = [w[n] for n in snames]
    res = adamw(_rows(like, d)[None], _rows([gsmall[n] for n in snames], d)[None], _rows([mom[n] for n in snames], d)[None],
                _rows([var[n] for n in snames], d)[None], name="adamw_small")
    for vals, dst in zip(res, (grad, delta, new_m, new_v)):
        dst.update(zip(snames, _unrows(vals, like)))
    return (loss, gx.reshape(x.shape), *[grad[n] for n in WEIGHTS], *[delta[n] for n in WEIGHTS],
            *[new_m[n] for n in WEIGHTS], *[new_v[n] for n in WEIGHTS])
```

```python
import functools
import math

import jax
import jax.numpy as jnp
from jax import lax
from jax.experimental import pallas as pl
from jax.experimental.pallas import tpu as pltpu

F32 = jnp.float32
BF16 = jnp.bfloat16
MESH = pl.DeviceIdType.MESH

V7X_VMEM_LIMIT_BYTES = 56 * 1024 * 1024
N_CHIPS = 4

EPS = 1e-6
POOL_WINDOWS = (2, 4, 8, 16)
POOL_HALO = 16
HG_HEAD = 128
HG_CHUNK = 64
FOX_HEAD = 128
FOX_BLK = 512
XA_HEADS = 4
ADAM_LR, ADAM_B1, ADAM_B2, ADAM_EPS, ADAM_WD, ADAM_STEP = 0.001, 0.9, 0.999, 1e-08, 0.01, 10
EXP_CLAMP = 80.0


def _params(sem=None):
    return pltpu.CompilerParams(dimension_semantics=sem, vmem_limit_bytes=V7X_VMEM_LIMIT_BYTES)


def _blk(pref, dim):
    b = min(pref, dim)
    assert dim % b == 0, (pref, dim)
    return b


class VM:
    def __init__(self, arr, kind="cs", lead=(), inner=(), pfn=None):
        self.arr, self.kind, self.lead, self.inner = arr, kind, tuple(lead), tuple(inner)
        self.pfn = pfn or (lambda p: p)
        p = arr.shape[len(self.lead)]
        r, c = arr.shape[-2:]
        assert arr.ndim == len(self.lead) + 1 + len(self.inner) + 2, (arr.shape, lead, inner)
        self.P = p
        self.shape = (r, c * p) if kind == "cs" else (r * p, c)
        self.dtype = arr.dtype

    def spec(self, br, bc, rfn, cfn):
        p = self.P
        r, c = self.arr.shape[-2:]
        assert c % bc == 0 and r % br == 0, (self.arr.shape, br, bc)
        if p == 1:
            def imap(*g):
                return (*self.lead, self.pfn(0), *self.inner, rfn(*g), cfn(*g))
        elif self.kind == "cs":
            per = c // bc

            def imap(*g):
                cb = cfn(*g)
                return (*self.lead, self.pfn(lax.div(cb, per)), *self.inner, rfn(*g), lax.rem(cb, per))
        else:
            per = r // br

            def imap(*g):
                rb = rfn(*g)
                return (*self.lead, self.pfn(lax.div(rb, per)), *self.inner, lax.rem(rb, per), cfn(*g))
        return pl.BlockSpec((None,) * (self.arr.ndim - 2) + (br, bc), imap)


def vm2(arr):
    return VM(arr.reshape((1,) + arr.shape))


def _out_struct(shape, kind, p, dtype):
    r, c = shape
    return jax.ShapeDtypeStruct((p, r, c // p) if kind == "cs" else (p, r // p, c), dtype)


_ANY = pl.BlockSpec(memory_space=pl.ANY)


def _me():
    x, y, c = lax.axis_index("x"), lax.axis_index("y"), lax.axis_index("c")
    chips = [(1 - x, y), (x, 1 - y), (1 - x, 1 - y)]
    return x, y, c, chips


def _chip_id(xy):
    return 2 * xy[0] + xy[1]


def _rcopy(src, dst, ssem, rsem, dev):
    return pltpu.make_async_remote_copy(src_ref=src, dst_ref=dst, send_sem=ssem, recv_sem=rsem, device_id=dev,
                                        device_id_type=MESH)


class GatherSide:
    def __init__(self, shards):
        self.inputs = list(shards)
        self.out_shape = [jax.ShapeDtypeStruct((N_CHIPS,) + s.shape, s.dtype) for s in shards]
        self.aliases = {}
        self.rows = len(shards)
        self.outs = None

    def _copy(self, ins, outs, ssem, rsem, w, j, receive):
        x, y, c, chips = _me()
        h = self.inputs[w].shape[0] // 2
        half = pl.ds(c * h, h)
        if receive:
            r = outs[w].at[_chip_id(chips[j]), half]
            return _rcopy(r, r, ssem.at[w, j], rsem.at[w, j], (*chips[j], c))
        return _rcopy(ins[w].at[half], outs[w].at[_chip_id((x, y)), half], ssem.at[w, j], rsem.at[w, j], (*chips[j], c))

    def start(self, ins, outs, ssem, rsem):
        for w in range(len(self.inputs)):
            for j in range(3):
                self._copy(ins, outs, ssem, rsem, w, j, False).start()

    def finish(self, ins, outs, ssem, rsem):
        for w in range(len(self.inputs)):
            for j in range(3):
                self._copy(ins, outs, ssem, rsem, w, j, True).wait_recv()
                self._copy(ins, outs, ssem, rsem, w, j, False).wait_send()


class ForwardSide:
    def __init__(self, fulls, shards):
        self.inputs = list(fulls) + list(shards)
        self.out_shape = [jax.ShapeDtypeStruct(f.shape, f.dtype) for f in fulls]
        self.aliases = {w: w for w in range(len(fulls))}
        self.rows = len(fulls)
        self.outs = None

    def _copy(self, outs, ssem, rsem, w, j, receive):
        x, y, c, chips = _me()
        h = self.inputs[w].shape[1] // 2
        r = outs[w].at[_chip_id(chips[j]), pl.ds(((1 - c) if receive else c) * h, h)]
        return _rcopy(r, r, ssem.at[w, j], rsem.at[w, j], (x, y, 1 - c))

    def _own(self, ins, outs, ssem, rsem, w):
        x, y, c, _ = _me()
        return _rcopy(ins[self.rows + w], outs[w].at[_chip_id((x, y))], ssem.at[w, 3], rsem.at[w, 3], (x, y, 1 - c))

    def start(self, ins, outs, ssem, rsem):
        for w in range(self.rows):
            self._own(ins, outs, ssem, rsem, w).start()
            for j in range(3):
                self._copy(outs, ssem, rsem, w, j, False).start()

    def finish(self, ins, outs, ssem, rsem):
        for w in range(self.rows):
            self._own(ins, outs, ssem, rsem, w).wait()
            for j in range(3):
                self._copy(outs, ssem, rsem, w, j, False).wait_send()
                self._copy(outs, ssem, rsem, w, j, True).wait_recv()


class PairSide:
    def __init__(self, parts):
        self.inputs = list(parts)
        self.out_shape = [jax.ShapeDtypeStruct((p.shape[0], p.shape[1] // 2, p.shape[2]), p.dtype) for p in parts]
        self.aliases = {}
        self.rows = len(parts)
        self.outs = None

    def _copy(self, ins, outs, ssem, rsem, w):
        x, y, c, _ = _me()
        h = self.inputs[w].shape[1] // 2
        return _rcopy(ins[w].at[:, pl.ds((1 - c) * h, h), :], outs[w], ssem.at[w, 0], rsem.at[w, 0], (x, y, 1 - c))

    def start(self, ins, outs, ssem, rsem):
        for w in range(self.rows):
            self._copy(ins, outs, ssem, rsem, w).start()

    def finish(self, ins, outs, ssem, rsem):
        for w in range(self.rows):
            self._copy(ins, outs, ssem, rsem, w).wait()


class _SemRows:
    def __init__(self, sem, off):
        self.sem, self.off = sem, off

    @property
    def at(self):
        return self

    def __getitem__(self, idx):
        return self.sem.at[self.off + idx[0], idx[1]]


class Sides:
    def __init__(self, sides):
        self.sides = list(sides)
        self.inputs = [a for s in self.sides for a in s.inputs]
        self.out_shape = [o for s in self.sides for o in s.out_shape]
        self.rows = sum(s.rows for s in self.sides)
        self.aliases, i0, o0 = {}, 0, 0
        for s in self.sides:
            self.aliases.update({i0 + i: o0 + o for i, o in s.aliases.items()})
            i0, o0 = i0 + len(s.inputs), o0 + len(s.out_shape)

    def _each(self, method, ins, outs, ssem, rsem):
        i0 = o0 = r0 = 0
        for s in self.sides:
            getattr(s, method)(ins[i0:i0 + len(s.inputs)], outs[o0:o0 + len(s.out_shape)], _SemRows(ssem, r0), _SemRows(rsem, r0))
            i0, o0, r0 = i0 + len(s.inputs), o0 + len(s.out_shape), r0 + s.rows

    def start(self, ins, outs, ssem, rsem):
        self._each("start", ins, outs, ssem, rsem)

    def finish(self, ins, outs, ssem, rsem):
        self._each("finish", ins, outs, ssem, rsem)

    @property
    def outs(self):
        return None

    @outs.setter
    def outs(self, vals):
        o0 = 0
        for s in self.sides:
            s.outs = list(vals[o0:o0 + len(s.out_shape)])
            o0 += len(s.out_shape)


class ReduceSide:
    def __init__(self, sums):
        self.inputs = list(sums)
        self.out_shape = [jax.ShapeDtypeStruct((3,) + s.shape[1:], s.dtype) for s in sums]
        self.aliases = {}
        self.rows = len(sums)
        self.outs = None

    def _copy(self, ins, outs, ssem, rsem, w, j):
        _, _, c, chips = _me()
        return _rcopy(ins[w].at[_chip_id(chips[j])], outs[w].at[j], ssem.at[w, j], rsem.at[w, j], (*chips[j], c))

    def start(self, ins, outs, ssem, rsem):
        for w in range(len(self.inputs)):
            for j in range(3):
                self._copy(ins, outs, ssem, rsem, w, j).start()

    def finish(self, ins, outs, ssem, rsem):
        for w in range(len(self.inputs)):
            for j in range(3):
                self._copy(ins, outs, ssem, rsem, w, j).wait()


def _call(body, side, *, name, grid, in_specs, out_specs, out_shape, scratch_shapes=(), sem, aliases=None, args):
    if side is None:
        return pl.pallas_call(body, name=name, grid=grid, in_specs=in_specs, out_specs=out_specs, out_shape=out_shape,
                              scratch_shapes=list(scratch_shapes), input_output_aliases=aliases or {},
                              compiler_params=_params(sem))(*args)
    single = not isinstance(out_shape, (list, tuple))
    oshape, ospecs = ([out_shape], [out_specs]) if single else (list(out_shape), list(out_specs))
    n_in, n_out, s_in, s_out = len(in_specs), len(oshape), len(side.inputs), len(side.out_shape)

    def wrapped(*refs):
        ins, sin = refs[:n_in], refs[n_in:n_in + s_in]
        outs = refs[n_in + s_in:n_in + s_in + n_out]
        souts = refs[n_in + s_in + n_out:n_in + s_in + n_out + s_out]
        rest = refs[n_in + s_in + n_out + s_out:]
        scratch, (ssem, rsem) = rest[:-2], rest[-2:]
        first = functools.reduce(jnp.logical_and, [pl.program_id(a) == 0 for a in range(len(grid))])
        last = functools.reduce(jnp.logical_and, [pl.program_id(a) == grid[a] - 1 for a in range(len(grid))])

        @pl.when(first)
        def _():
            side.start(sin, souts, ssem, rsem)

        body(*ins, *outs, *scratch)

        @pl.when(last)
        def _():
            side.finish(sin, souts, ssem, rsem)

    sems = pltpu.SemaphoreType.DMA((side.rows, 4))
    res = pl.pallas_call(
        wrapped, name=name, grid=grid, in_specs=list(in_specs) + [_ANY] * s_in, out_specs=ospecs + [_ANY] * s_out,
        out_shape=oshape + side.out_shape, scratch_shapes=list(scratch_shapes) + [sems, sems],
        input_output_aliases={**(aliases or {}), **{n_in + i: n_out + o for i, o in side.aliases.items()}},
        compiler_params=_params(("arbitrary",) * len(grid)),
    )(*args, *side.inputs)
    side.outs = list(res[n_out:])
    return res[0] if single else list(res[:n_out])


def _best(g, cap):
    if g <= cap:
        return g
    cands = [d for d in range(128, cap + 1, 128) if g % d == 0]
    assert cands, (g, cap)
    return cands[-1]


def _row_blk(n, cap):
    cands = [d for d in range(16, min(n, cap) + 1, 16) if n % d == 0]
    assert cands, (n, cap)
    return cands[-1]


def _tiles(a, b, mode, out_kind, out_p, bm, bn, bk):
    def cpiece(v):
        return v.arr.shape[-1] if v.kind == "cs" else v.shape[1]

    def rpiece(v):
        return v.arr.shape[-2] if v.kind == "rs" else v.shape[0]

    if mode == "nn":
        m, n = a.shape[0], b.shape[1]
        gm, gn, gk = rpiece(a), cpiece(b), math.gcd(cpiece(a), rpiece(b))
    elif mode == "nt":
        m, n = a.shape[0], b.shape[0]
        gm, gn, gk = rpiece(a), rpiece(b), math.gcd(cpiece(a), cpiece(b))
    else:
        m, n = a.shape[1], b.shape[1]
        gm, gn, gk = cpiece(a), cpiece(b), math.gcd(rpiece(a), rpiece(b))
    if out_kind == "cs":
        gn = math.gcd(gn, n // out_p)
    else:
        gm = math.gcd(gm, m // out_p)
    caps = {"nn": (1024, 1536, 2048), "nt": (512, 2048, 2048), "tn": (1536, 1536, 2048)}[mode]
    return (bm or _best(gm, caps[0])), (bn or _best(gn, caps[1])), (bk or _best(gk, caps[2]))


def matmul(a, b, mode, *, out_dtype, bm=None, bn=None, bk=None, out_kind="cs", out_p=1, out_pfn=None, res=None, epi=None,
           side=None, name):
    bm, bn, bk = _tiles(a, b, mode, out_kind, out_p, bm, bn, bk)
    if mode == "nn":
        (m, k), (k2, n) = a.shape, b.shape
        a_spec = a.spec(bm, bk, lambda i, j, kk: i, lambda i, j, kk: kk)
        b_spec = b.spec(bk, bn, lambda i, j, kk: kk, lambda i, j, kk: j)
        dims = (((1,), (0,)), ((), ()))
    elif mode == "nt":
        (m, k), (n, k2) = a.shape, b.shape
        a_spec = a.spec(bm, bk, lambda i, j, kk: i, lambda i, j, kk: kk)
        b_spec = b.spec(bn, bk, lambda i, j, kk: j, lambda i, j, kk: kk)
        dims = (((1,), (1,)), ((), ()))
    else:
        (k, m), (k2, n) = a.shape, b.shape
        a_spec = a.spec(bk, bm, lambda i, j, kk: kk, lambda i, j, kk: i)
        b_spec = b.spec(bk, bn, lambda i, j, kk: kk, lambda i, j, kk: j)
        dims = (((0,), (0,)), ((), ()))
    assert k == k2, (a.shape, b.shape, mode)
    assert m % bm == 0 and n % bn == 0 and k % bk == 0, (m, n, k, bm, bn, bk)
    nk = k // bk
    out_sds = _out_struct((m, n), out_kind, out_p, out_dtype)
    out_vm = VM(out_sds, out_kind, pfn=out_pfn)
    o_spec = out_vm.spec(bm, bn, lambda i, j, kk: i, lambda i, j, kk: j)
    in_specs, args = [a_spec, b_spec], [a.arr, b.arr]
    tiles = ([res] if res is not None else []) + (list(epi[1]) if epi else [])
    for v in tiles:
        assert v.shape == (m, n)
        in_specs.append(v.spec(bm, bn, lambda i, j, kk: i, lambda i, j, kk: j))
        args.append(v.arr)
    n_out = epi[2] if epi else 1

    def body(a_ref, b_ref, *rest):
        t_refs, o_refs = rest[:len(tiles)], rest[len(tiles):len(tiles) + n_out]
        part = lax.dot_general(a_ref[...], b_ref[...], dims, preferred_element_type=F32)

        def write(tot):
            if res is not None:
                tot = tot + t_refs[0][...].astype(F32)
            outs = epi[0](tot, *[r[...].astype(F32) for r in t_refs[len(tiles) - len(epi[1]):]]) if epi else (tot,)
            for o_ref, val in zip(o_refs, outs):
                o_ref[...] = val.astype(o_ref.dtype)

        if nk == 1:
            write(part)
            return
        acc = rest[-1]
        kk = pl.program_id(2)

        @pl.when(kk == 0)
        def _():
            acc[...] = part

        @pl.when(kk > 0)
        def _():
            acc[...] += part

        @pl.when(kk == nk - 1)
        def _():
            write(acc[...])

    return _call(body, side, name=name, grid=(m // bm, n // bn, nk), in_specs=in_specs,
                 out_specs=o_spec if n_out == 1 else [o_spec] * n_out, out_shape=out_sds if n_out == 1 else [out_sds] * n_out,
                 scratch_shapes=[pltpu.VMEM((bm, bn), F32)] if nk > 1 else [],
                 sem=("parallel", "parallel", "arbitrary"), args=args)


def rmsnorm_fwd(x, g, *, name):
    t, d = x.shape
    bt = _blk(512, t)

    def body(x_ref, g_ref, o_ref):
        xv = x_ref[...]
        r = lax.rsqrt(jnp.mean(xv * xv, axis=-1, keepdims=True) + EPS)
        o_ref[...] = (xv * r * g_ref[...]).astype(o_ref.dtype)

    return pl.pallas_call(
        body, name=name, grid=(t // bt,),
        in_specs=[pl.BlockSpec((bt, d), lambda i: (i, 0)), pl.BlockSpec((1, d), lambda i: (0, 0))],
        out_specs=pl.BlockSpec((bt, d), lambda i: (i, 0)), out_shape=jax.ShapeDtypeStruct((t, d), BF16),
        compiler_params=_params(("parallel",)),
    )(x, g)


def rmsnorm_bwd(x, g, dh, dres, *, name):
    t, d = x.shape
    bt = _blk(256, t)
    want_dx = dres is not None

    def body(x_ref, g_ref, dh_ref, *rest):
        if want_dx:
            dres_ref, dx_ref, dxb_ref, dg_ref = rest
        else:
            (dg_ref,) = rest
        xv = x_ref[...]
        dhv = dh_ref[...].astype(F32)
        r = lax.rsqrt(jnp.mean(xv * xv, axis=-1, keepdims=True) + EPS)
        xh = xv * r
        part = jnp.sum(dhv * xh, axis=0, keepdims=True)

        @pl.when(pl.program_id(0) == 0)
        def _():
            dg_ref[...] = part

        @pl.when(pl.program_id(0) > 0)
        def _():
            dg_ref[...] += part

        if want_dx:
            dy = dhv * g_ref[...]
            dxn = r * (dy - xh * jnp.mean(dy * xh, axis=-1, keepdims=True))
            dx = dres_ref[...] + dxn
            dx_ref[...] = dx
            dxb_ref[...] = dx.astype(BF16)

    row = pl.BlockSpec((bt, d), lambda i: (i, 0))
    vec = pl.BlockSpec((1, d), lambda i: (0, 0))
    in_specs, args = [row, vec, row], [x, g, dh]
    out_specs, out_shape = [vec], [jax.ShapeDtypeStruct((1, d), F32)]
    if want_dx:
        in_specs.append(row)
        args.append(dres)
        out_specs = [row, row] + out_specs
        out_shape = [jax.ShapeDtypeStruct((t, d), F32), jax.ShapeDtypeStruct((t, d), BF16)] + out_shape
    return pl.pallas_call(
        body, name=name, grid=(t // bt,), in_specs=in_specs, out_specs=out_specs, out_shape=out_shape,
        compiler_params=_params(("arbitrary",)),
    )(*args)


def loss_head(x, g, tgt, *, name):
    t, d = x.shape
    bt = _blk(256, t)

    def body(x_ref, g_ref, t_ref, loss_ref, dx_ref, dxb_ref, dg_ref):
        xv = x_ref[...]
        gv = g_ref[...]
        r = lax.rsqrt(jnp.mean(xv * xv, axis=-1, keepdims=True) + EPS)
        xh = xv * r
        e = xh * gv - t_ref[...]
        lpart = jnp.zeros((1, 128), F32) + jnp.sum(e * e) * (0.5 / d)
        dyv = e * (1.0 / d)
        gpart = jnp.sum(dyv * xh, axis=0, keepdims=True)

        @pl.when(pl.program_id(0) == 0)
        def _():
            loss_ref[...] = lpart
            dg_ref[...] = gpart

        @pl.when(pl.program_id(0) > 0)
        def _():
            loss_ref[...] += lpart
            dg_ref[...] += gpart

        dy = dyv * gv
        dx = r * (dy - xh * jnp.mean(dy * xh, axis=-1, keepdims=True))
        dx_ref[...] = dx
        dxb_ref[...] = dx.astype(BF16)

    row = pl.BlockSpec((bt, d), lambda i: (i, 0))
    vec = pl.BlockSpec((1, d), lambda i: (0, 0))
    return pl.pallas_call(
        body, name=name, grid=(t // bt,), in_specs=[row, vec, row],
        out_specs=[pl.BlockSpec((1, 128), lambda i: (0, 0)), row, row, vec],
        out_shape=[jax.ShapeDtypeStruct((1, 128), F32), jax.ShapeDtypeStruct((t, d), F32),
                   jax.ShapeDtypeStruct((t, d), BF16), jax.ShapeDtypeStruct((1, d), F32)],
        compiler_params=_params(("arbitrary",)),
    )(x, g, tgt)


def _sigmoid(x):
    return 1.0 / (1.0 + jnp.exp(-x))


def _swiglu_epi(b, a):
    return b, a * _sigmoid(a) * b


def _swiglu_bwd_epi(ds, a, b):
    sg = _sigmoid(a)
    return ds * b * sg * (1.0 + a * (1.0 - sg)), ds * a * sg


def _xa_probs(qh, kh, scale):
    s = lax.dot_general(qh, kh, (((1,), (1,)), ((), ())), preferred_element_type=F32) * scale
    s = s - jnp.max(s, axis=-1, keepdims=True)
    p = jnp.exp(s)
    return p / jnp.sum(p, axis=-1, keepdims=True)


def xattn_fwd(q, kv, *, name):
    t, d = q.shape
    m = kv.shape[0]
    hd = d // XA_HEADS
    bt = _blk(512, t)
    scale = hd ** -0.5

    def body(q_ref, kv_ref, o_ref):
        for h in range(XA_HEADS):
            qh = q_ref[:, h * hd:(h + 1) * hd]
            kh = kv_ref[:, h * hd:(h + 1) * hd]
            vh = kv_ref[:, d + h * hd:d + (h + 1) * hd]
            p = _xa_probs(qh, kh, scale)
            o_ref[:, h * hd:(h + 1) * hd] = jnp.dot(p.astype(BF16), vh, preferred_element_type=F32).astype(BF16)

    return pl.pallas_call(
        body, name=name, grid=(t // bt,),
        in_specs=[pl.BlockSpec((bt, d), lambda i: (i, 0)), pl.BlockSpec((m, 2 * d), lambda i: (0, 0))],
        out_specs=pl.BlockSpec((bt, d), lambda i: (i, 0)), out_shape=jax.ShapeDtypeStruct((t, d), BF16),
        compiler_params=_params(("parallel",)),
    )(q, kv)


def xattn_bwd(q, kv, do, *, name):
    t, d = q.shape
    m = kv.shape[0]
    hd = d // XA_HEADS
    bt = _blk(512, t)
    scale = hd ** -0.5

    def body(q_ref, kv_ref, do_ref, dq_ref, dkv_ref):
        first = pl.program_id(0) == 0
        for h in range(XA_HEADS):
            qs, ks, vs = slice(h * hd, (h + 1) * hd), slice(h * hd, (h + 1) * hd), slice(d + h * hd, d + (h + 1) * hd)
            qh, kh, vh, doh = q_ref[:, qs], kv_ref[:, ks], kv_ref[:, vs], do_ref[:, qs]
            p = _xa_probs(qh, kh, scale)
            dp = lax.dot_general(doh, vh, (((1,), (1,)), ((), ())), preferred_element_type=F32)
            dsv = p * (dp - jnp.sum(p * dp, axis=-1, keepdims=True)) * scale
            dsb = dsv.astype(BF16)
            dq_ref[:, qs] = jnp.dot(dsb, kh, preferred_element_type=F32).astype(BF16)
            dk = lax.dot_general(dsb, qh, (((0,), (0,)), ((), ())), preferred_element_type=F32)
            dv = lax.dot_general(p.astype(BF16), doh, (((0,), (0,)), ((), ())), preferred_element_type=F32)

            @pl.when(first)
            def _():
                dkv_ref[:, ks] = dk
                dkv_ref[:, vs] = dv

            @pl.when(jnp.logical_not(first))
            def _():
                dkv_ref[:, ks] += dk
                dkv_ref[:, vs] += dv

    row = pl.BlockSpec((bt, d), lambda i: (i, 0))
    full = pl.BlockSpec((m, 2 * d), lambda i: (0, 0))
    return pl.pallas_call(
        body, name=name, grid=(t // bt,), in_specs=[row, full, row], out_specs=[row, full],
        out_shape=[jax.ShapeDtypeStruct((t, d), BF16), jax.ShapeDtypeStruct((m, 2 * d), F32)],
        compiler_params=_params(("arbitrary",)),
    )(q, kv, do)


def _pool_p(buf, uv, rows, w, bt):
    acc = uv
    for dd in range(1, w):
        acc = acc + buf[pl.ds(POOL_HALO - dd, bt), :]
    cnt = jnp.minimum(rows + 1, w).astype(F32)
    return acc / cnt - uv


def pool_fwd(z, w_pool, scale, *, name):
    t = z.shape[0]
    ng, gsz = w_pool.shape[0], w_pool.shape[1]
    mix = ng * gsz
    bt = _blk(512, t)

    def body(u_ref, uh_ref, w_ref, sc_ref, o_ref, buf):
        r = pl.program_id(0)
        rows = r * bt + lax.broadcasted_iota(jnp.int32, (bt, 1), 0)
        for g in range(ng):
            gs = slice(g * gsz, (g + 1) * gsz)
            uv = u_ref[:, gs]
            buf[0:POOL_HALO, :] = jnp.where(r > 0, uh_ref[:, gs], 0.0)
            buf[POOL_HALO:POOL_HALO + bt, :] = uv
            p = _pool_p(buf, uv, rows, POOL_WINDOWS[g], bt)
            y = jnp.dot(p.astype(BF16), w_ref[g], preferred_element_type=F32) * sc_ref[:, gs]
            o_ref[:, gs] = y.astype(BF16)

    hb = bt // POOL_HALO
    return pl.pallas_call(
        body, name=name, grid=(t // bt,),
        in_specs=[pl.BlockSpec((bt, mix), lambda i: (i, 0)),
                  pl.BlockSpec((POOL_HALO, mix), lambda i: (jnp.maximum(i * hb - 1, 0), 0)),
                  pl.BlockSpec((ng, gsz, gsz), lambda i: (0, 0, 0)), pl.BlockSpec((1, mix), lambda i: (0, 0))],
        out_specs=pl.BlockSpec((None, bt, mix), lambda i: (0, i, 0)),
        out_shape=jax.ShapeDtypeStruct((2, t, mix), BF16),
        scratch_shapes=[pltpu.VMEM((POOL_HALO + bt, gsz), F32)],
        compiler_params=_params(("parallel",)),
    )(z, z, w_pool, scale)


def pool_bwd(z, dcat, w_pool, scale, *, name):
    t = z.shape[0]
    ng, gsz = w_pool.shape[0], w_pool.shape[1]
    mix = ng * gsz
    bt = _blk(512, t)
    nb = t // bt
    nt_dims = (((1,), (1,)), ((), ()))
    tn_dims = (((0,), (0,)), ((), ()))

    def body(u_ref, uh_ref, dy_ref, dyh_ref, w_ref, sc_ref, du_ref, dw_ref, dsc_ref, buf, buf2):
        r = pl.program_id(0)
        first = r == 0
        rows = r * bt + lax.broadcasted_iota(jnp.int32, (bt, 1), 0)
        rows_h = (r + 1) * bt + lax.broadcasted_iota(jnp.int32, (POOL_HALO, 1), 0)
        for g in range(ng):
            w = POOL_WINDOWS[g]
            gs = slice(g * gsz, (g + 1) * gsz)
            uv = u_ref[:, gs]
            buf[0:POOL_HALO, :] = jnp.where(r > 0, uh_ref[:, gs], 0.0)
            buf[POOL_HALO:POOL_HALO + bt, :] = uv
            pb = _pool_p(buf, uv, rows, w, bt).astype(BF16)
            wg = w_ref[g]
            sc = sc_ref[:, gs]
            y0 = jnp.dot(pb, wg, preferred_element_type=F32)
            dyv = dy_ref[:, gs].astype(F32)
            dsc = jnp.sum(dyv * y0, axis=0, keepdims=True)
            dyw = (dyv * sc).astype(BF16)
            dw = lax.dot_general(pb, dyw, tn_dims, preferred_element_type=F32)

            @pl.when(first)
            def _():
                dw_ref[g] = dw
                dsc_ref[:, gs] = dsc

            @pl.when(jnp.logical_not(first))
            def _():
                dw_ref[g] += dw
                dsc_ref[:, gs] += dsc

            dp = lax.dot_general(dyw, wg, nt_dims, preferred_element_type=F32)
            dyh = (dyh_ref[:, gs].astype(F32) * sc).astype(BF16)
            dph = lax.dot_general(dyh, wg, nt_dims, preferred_element_type=F32)
            dph = jnp.where(r < nb - 1, dph, 0.0)
            buf2[0:bt, :] = dp / jnp.minimum(rows + 1, w).astype(F32)
            buf2[bt:bt + POOL_HALO, :] = dph / jnp.minimum(rows_h + 1, w).astype(F32)
            acc = buf2[pl.ds(0, bt), :]
            for dd in range(1, w):
                acc = acc + buf2[pl.ds(dd, bt), :]
            du_ref[:, gs] = (acc - dp).astype(BF16)

    hb = bt // POOL_HALO
    nhb = t // POOL_HALO
    return pl.pallas_call(
        body, name=name, grid=(nb,),
        in_specs=[pl.BlockSpec((bt, mix), lambda i: (i, 0)),
                  pl.BlockSpec((POOL_HALO, mix), lambda i: (jnp.maximum(i * hb - 1, 0), 0)),
                  pl.BlockSpec((None, bt, mix), lambda i: (0, i, 0)),
                  pl.BlockSpec((None, POOL_HALO, mix), lambda i: (0, jnp.minimum((i + 1) * hb, nhb - 1), 0)),
                  pl.BlockSpec((ng, gsz, gsz), lambda i: (0, 0, 0)), pl.BlockSpec((1, mix), lambda i: (0, 0))],
        out_specs=[pl.BlockSpec((None, bt, mix), lambda i: (4, i, 0)),
                   pl.BlockSpec((ng, gsz, gsz), lambda i: (0, 0, 0)), pl.BlockSpec((1, mix), lambda i: (0, 0))],
        out_shape=[jax.ShapeDtypeStruct((5, t, mix), BF16), jax.ShapeDtypeStruct((ng, gsz, gsz), F32),
                   jax.ShapeDtypeStruct((1, mix), F32)],
        scratch_shapes=[pltpu.VMEM((POOL_HALO + bt, gsz), F32), pltpu.VMEM((bt + POOL_HALO, gsz), F32)],
        compiler_params=_params(("arbitrary",)),
    )(z, z, dcat, dcat, w_pool, scale)


HG_HEADS_PER_STEP = 4
HG_LEVELS = ((64, 31), (32, 15), (16, 7))
HG_DIAG = (8, 3)
_NT = (((1,), (1,)), ((), ()))
_TN = (((0,), (0,)), ((), ()))
_HI = lax.Precision.HIGHEST


def _hg_masks():
    c = HG_CHUNK
    t = lax.broadcasted_iota(jnp.int32, (c, c), 0)
    s = lax.broadcasted_iota(jnp.int32, (c, c), 1)
    masks = []
    for blk, row in HG_LEVELS:
        sh = blk.bit_length() - 1
        same = (t >> sh) == (s >> sh)
        masks.append(same & ((t & (blk - 1)) > row) & ((s & (blk - 1)) <= row))
    sh = HG_DIAG[0].bit_length() - 1
    masks.append(((t >> sh) == (s >> sh)) & (s <= t))
    return t, s, masks


def _row_of_block(x, blk, row):
    c, n = x.shape
    x3 = x.reshape(c // blk, blk, n)
    return jnp.broadcast_to(x3[:, row:row + 1, :], x3.shape).reshape(c, n)


def _hg_parts(qv, flv, lb, masks, tri):
    sgf = _sigmoid(flv)
    f = lb + (1.0 - lb) * sgf
    logf = jnp.log(f)
    kk = 1.0 - f
    sgq = _sigmoid(qv)
    qf = qv * sgq * (HG_HEAD ** -0.5)
    bc = jnp.dot(tri, logf, preferred_element_type=F32, precision=_HI)
    levels = []
    a = None
    for li, (blk, row) in enumerate(HG_LEVELS + (HG_DIAG,)):
        e = bc - _row_of_block(bc, blk, row)
        if li < len(HG_LEVELS):
            eq, ek = jnp.exp(jnp.minimum(e, 0.0)), jnp.exp(jnp.minimum(-e, 0.0))
        else:
            eq, ek = jnp.exp(jnp.clip(e, -EXP_CLAMP, EXP_CLAMP)), jnp.exp(jnp.clip(-e, -EXP_CLAMP, EXP_CLAMP))
        qt, kt = qf * eq, kk * ek
        part = jnp.where(masks[li], lax.dot_general(qt.astype(BF16), kt.astype(BF16), _NT, preferred_element_type=F32), 0.0)
        a = part if a is None else a + part
        levels.append((eq, ek, qt, kt))
    return dict(sgf=sgf, f=f, kk=kk, sgq=sgq, qf=qf, bc=bc, levels=levels, a=a)


def hgrn_fwd(z, cat, lb, gain, mix_a, *, side=None, name):
    t = z.shape[0]
    mix_b = lb.shape[1]
    nh = mix_b // HG_HEAD
    bt = _blk(256, t)
    ncb = bt // HG_CHUNK
    dh = HG_HEAD

    def body(q_ref, fl_ref, i_ref, g_ref, lb_ref, gain_ref, cat_in, o_ref, st_ref, st):
        del cat_in

        @pl.when(pl.program_id(1) == 0)
        def _():
            st[...] = jnp.zeros_like(st)

        t_i, s_i, masks = _hg_masks()
        tri = (s_i <= t_i).astype(F32)
        lbv, gn = lb_ref[...], gain_ref[...]
        for c in range(ncb):
            rs = slice(c * HG_CHUNK, (c + 1) * HG_CHUNK)
            pr = _hg_parts(q_ref[rs, :], fl_ref[rs, :], lbv, masks, tri)
            vb = i_ref[rs, :].astype(BF16)
            stv = st[...]
            st_ref[c] = stv
            bc = pr["bc"]
            qt = pr["qf"] * jnp.exp(bc)
            o = (jnp.dot(pr["a"].astype(BF16), vb, preferred_element_type=F32)
                 + lax.dot_general(qt.astype(BF16), stv.astype(BF16), _NT, preferred_element_type=F32))
            bl = bc[HG_CHUNK - 1:HG_CHUNK, :]
            khat = pr["kk"] * jnp.exp(bl - bc)
            st[...] = stv * jnp.exp(bl) + lax.dot_general(vb, khat.astype(BF16), _TN, preferred_element_type=F32)
            r = lax.rsqrt(jnp.mean(o * o, axis=-1, keepdims=True) + EPS)
            gv = g_ref[rs, :]
            o_ref[rs, :] = (o * r * gn * (gv * _sigmoid(gv))).astype(BF16)

    def col(which):
        base = (mix_a + which * mix_b) // dh
        return pl.BlockSpec((bt, dh), lambda h, i: (i, base + h))

    return _call(
        body, side, name=name, grid=(nh, t // bt),
        in_specs=[col(0), col(1), col(2), col(3), pl.BlockSpec((1, dh), lambda h, i: (0, h)),
                  pl.BlockSpec((1, dh), lambda h, i: (0, 0)), _ANY],
        out_specs=[pl.BlockSpec((None, bt, dh), lambda h, i: (1, i, h)),
                   pl.BlockSpec((None, ncb, dh, dh), lambda h, i: (h, i, 0, 0))],
        out_shape=[jax.ShapeDtypeStruct(cat.shape, BF16), jax.ShapeDtypeStruct((nh, t // HG_CHUNK, dh, dh), F32)],
        scratch_shapes=[pltpu.VMEM((dh, dh), F32)], aliases={6: 0}, sem=("parallel", "arbitrary"),
        args=(z, z, z, z, lb, gain, cat))


def hgrn_bwd(z, dcat, dz5, states, lb, gain, mix_a, *, side=None, name):
    t = z.shape[0]
    mix_b = lb.shape[1]
    nh = mix_b // HG_HEAD
    bt = _blk(256, t)
    nb = t // bt
    ncb = bt // HG_CHUNK
    dh = HG_HEAD
    hp = HG_HEADS_PER_STEP if nh % HG_HEADS_PER_STEP == 0 else 1

    def body(q_ref, fl_ref, i_ref, g_ref, dy_ref, st_ref, lb_ref, gain_ref, dz_in, dz_ref, dlb_ref, dgn_ref, dst):
        del dz_in
        first = pl.program_id(1) == 0

        @pl.when(first)
        def _():
            dst[...] = jnp.zeros_like(dst)

        t_i, s_i, masks = _hg_masks()
        tri = (s_i <= t_i).astype(F32)
        triu = (s_i >= t_i).astype(F32)
        last_row = lax.broadcasted_iota(jnp.int32, (HG_CHUNK, 1), 0) == HG_CHUNK - 1
        gn = gain_ref[...]
        dlb_acc = [jnp.zeros((1, dh), F32) for _ in range(hp)]
        dgn_acc = [jnp.zeros((1, dh), F32) for _ in range(hp)]
        for c, hh in [(c, hh) for c in reversed(range(ncb)) for hh in range(hp)]:
            rs, cs = slice(c * HG_CHUNK, (c + 1) * HG_CHUNK), slice(hh * dh, (hh + 1) * dh)
            lbv = lb_ref[:, cs]
            qv, flv, gv = q_ref[rs, cs], fl_ref[rs, cs], g_ref[rs, cs]
            pr = _hg_parts(qv, flv, lbv, masks, tri)
            vb = i_ref[rs, cs].astype(BF16)
            stv = st_ref[hh, c]
            stb = stv.astype(BF16)
            dsv = dst[hh]
            dsb = dsv.astype(BF16)
            bc, kk, qf, ab = pr["bc"], pr["kk"], pr["qf"], pr["a"].astype(BF16)
            ebc = jnp.exp(bc)
            qt = qf * ebc
            qtb = qt.astype(BF16)
            o = jnp.dot(ab, vb, preferred_element_type=F32) + lax.dot_general(qtb, stb, _NT, preferred_element_type=F32)
            r = lax.rsqrt(jnp.mean(o * o, axis=-1, keepdims=True) + EPS)
            oh = o * r
            sgg = _sigmoid(gv)
            dyv = dy_ref[rs, cs].astype(F32)
            don = dyv * (gv * sgg)
            dgate = dyv * (oh * gn) * (sgg * (1.0 + gv * (1.0 - sgg)))
            dgn_acc[hh] = dgn_acc[hh] + jnp.sum(don * oh, axis=0, keepdims=True)
            doh = don * gn
            do = r * (doh - oh * jnp.mean(doh * oh, axis=-1, keepdims=True))
            dob = do.astype(BF16)
            bl = bc[HG_CHUNK - 1:HG_CHUNK, :]
            ebl = jnp.exp(bl)
            ekh = jnp.exp(bl - bc)
            khat = kk * ekh
            dv = (lax.dot_general(ab, dob, _TN, preferred_element_type=F32)
                  + lax.dot_general(khat.astype(BF16), dsb, _NT, preferred_element_type=F32))
            da = lax.dot_general(dob, vb, _NT, preferred_element_type=F32)
            dqt = jnp.dot(dob, stb, preferred_element_type=F32)
            dkh = jnp.dot(vb, dsb, preferred_element_type=F32)
            dst[hh] = dsv * ebl + lax.dot_general(dob, qtb, _TN, preferred_element_type=F32)
            dbl = jnp.sum(dsv * stv, axis=0, keepdims=True) * ebl + jnp.sum(dkh * khat, axis=0, keepdims=True)
            dqf = dqt * ebc
            dkk = dkh * ekh
            dbc = dqt * qt - dkh * khat
            for li, (eq, ek, qtl, ktl) in enumerate(pr["levels"]):
                gm = jnp.where(masks[li], da, 0.0).astype(BF16)
                qtr, ktr = qtl.astype(BF16), ktl.astype(BF16)
                dql = jnp.dot(gm, ktr, preferred_element_type=F32)
                dkl = lax.dot_general(gm, qtr, _TN, preferred_element_type=F32)
                dqf = dqf + dql * eq
                dkk = dkk + dkl * ek
                dbc = dbc + qtr.astype(F32) * dql - ktr.astype(F32) * dkl
            dbc = dbc + jnp.where(last_row, dbl, 0.0)
            dlogf = jnp.dot(triu, dbc, preferred_element_type=F32, precision=_HI)
            df = dlogf / pr["f"] - dkk
            sgf = pr["sgf"]
            dfl = df * (1.0 - lbv) * sgf * (1.0 - sgf)
            dlb_acc[hh] = dlb_acc[hh] + jnp.sum(df * (1.0 - sgf), axis=0, keepdims=True)
            sgq = pr["sgq"]
            dq = dqf * (HG_HEAD ** -0.5) * (sgq * (1.0 + qv * (1.0 - sgq)))
            dz_ref[0, rs, cs] = dq.astype(BF16)
            dz_ref[1, rs, cs] = dfl.astype(BF16)
            dz_ref[2, rs, cs] = dv.astype(BF16)
            dz_ref[3, rs, cs] = dgate.astype(BF16)

        @pl.when(first)
        def _():
            for hh in range(hp):
                dlb_ref[:, hh * dh:(hh + 1) * dh] = dlb_acc[hh]
                dgn_ref[hh] = dgn_acc[hh]

        @pl.when(jnp.logical_not(first))
        def _():
            for hh in range(hp):
                dlb_ref[:, hh * dh:(hh + 1) * dh] += dlb_acc[hh]
                dgn_ref[hh] += dgn_acc[hh]

    wd = hp * dh

    def col(which):
        base = (mix_a + which * mix_b) // wd
        return pl.BlockSpec((bt, wd), lambda h, i: (nb - 1 - i, base + h))

    return _call(
        body, side, name=name, grid=(nh // hp, nb),
        in_specs=[col(0), col(1), col(2), col(3),
                  pl.BlockSpec((None, bt, wd), lambda h, i: (1, nb - 1 - i, h)),
                  pl.BlockSpec((hp, ncb, dh, dh), lambda h, i: (h, nb - 1 - i, 0, 0)),
                  pl.BlockSpec((1, wd), lambda h, i: (0, h)), pl.BlockSpec((1, dh), lambda h, i: (0, 0)), _ANY],
        out_specs=[pl.BlockSpec((4, bt, wd), lambda h, i: (0, nb - 1 - i, h)),
                   pl.BlockSpec((1, wd), lambda h, i: (0, h)),
                   pl.BlockSpec((hp, 1, dh), lambda h, i: (h, 0, 0))],
        out_shape=[jax.ShapeDtypeStruct(dz5.shape, BF16), jax.ShapeDtypeStruct((1, mix_b), F32),
                   jax.ShapeDtypeStruct((nh, 1, dh), F32)],
        scratch_shapes=[pltpu.VMEM((hp, dh, dh), F32)], aliases={8: 0}, sem=("parallel", "arbitrary"),
        args=(z, z, z, z, dcat, states, lb, gain, dz5))


LOG2E = 1.4426950408889634


def _fox_q(qb):
    return (qb.astype(F32) * (FOX_HEAD ** -0.5 * LOG2E)).astype(BF16)


def _fox_scores(qs, kb, fk, masked):
    s = lax.dot_general(qs, kb, _NT, preferred_element_type=F32) - fk * LOG2E
    if masked:
        n = s.shape[0]
        row = lax.broadcasted_iota(jnp.int32, (n, n), 0)
        col = lax.broadcasted_iota(jnp.int32, (n, n), 1)
        s = jnp.where(col <= row, s, -jnp.inf)
    return s


def fox_fwd(qkv, fk, *, side=None, name):
    _, t, d = qkv.shape
    nh = d // FOX_HEAD
    b = _blk(FOX_BLK, t)
    nb = t // b
    dh = FOX_HEAD

    def body(q_ref, k_ref, v_ref, f_ref, o_ref, lse_ref):
        qi = pl.program_id(1)
        qs = _fox_q(q_ref[...])

        def step(kj, carry, masked):
            m, l, acc = carry
            off = pl.multiple_of(kj * b, b)
            s = _fox_scores(qs, k_ref[pl.ds(off, b), :], f_ref[kj], masked)
            m_new = jnp.maximum(m, jnp.max(s, axis=-1, keepdims=True))
            alpha = jnp.exp2(m - m_new)
            p = jnp.exp2(s - m_new)
            l = alpha * l + jnp.sum(p, axis=-1, keepdims=True)
            acc = alpha * acc + jnp.dot(p.astype(BF16), v_ref[pl.ds(off, b), :], preferred_element_type=F32)
            return m_new, l, acc

        init = (jnp.full((b, 1), -jnp.inf, F32), jnp.zeros((b, 1), F32), jnp.zeros((b, dh), F32))
        carry = lax.fori_loop(0, qi, lambda kj, c: step(kj, c, False), init)
        m, l, acc = step(qi, carry, True)
        o_ref[...] = (acc / l).astype(BF16)
        lse_ref[...] = m + jnp.log(l) * LOG2E

    return _call(
        body, side, name=name, grid=(nh, nb),
        in_specs=[pl.BlockSpec((None, b, dh), lambda h, i: (0, i, h)),
                  pl.BlockSpec((None, t, dh), lambda h, i: (1, 0, h)),
                  pl.BlockSpec((None, t, dh), lambda h, i: (2, 0, h)),
                  pl.BlockSpec((None, nb, 1, b), lambda h, i: (h, 0, 0, 0))],
        out_specs=[pl.BlockSpec((b, dh), lambda h, i: (i, h)), pl.BlockSpec((None, b, 1), lambda h, i: (h, i, 0))],
        out_shape=[jax.ShapeDtypeStruct((t, d), BF16), jax.ShapeDtypeStruct((nh, t, 1), F32)],
        sem=("parallel", "parallel"), args=(qkv, qkv, qkv, fk))


def fox_bwd_dq(qkv, fk, do, lse, *, side=None, name):
    _, t, d = qkv.shape
    nh = d // FOX_HEAD
    b = _blk(FOX_BLK, t)
    nb = t // b
    dh = FOX_HEAD
    scale = dh ** -0.5

    def body(q_ref, k_ref, v_ref, f_ref, do_ref, lse_ref, dq_ref, dl_ref, p_buf, dp_buf):
        qi = pl.program_id(1)
        qs, dob, lse_v = _fox_q(q_ref[...]), do_ref[...], lse_ref[...]

        def first(kj, dl, masked):
            off = pl.multiple_of(kj * b, b)
            p = jnp.exp2(_fox_scores(qs, k_ref[pl.ds(off, b), :], f_ref[kj], masked) - lse_v)
            dp = lax.dot_general(dob, v_ref[pl.ds(off, b), :], _NT, preferred_element_type=F32)
            p_buf[kj] = p
            dp_buf[kj] = dp
            return dl + jnp.sum(p * dp, axis=-1, keepdims=True)

        dl = lax.fori_loop(0, qi, lambda kj, c: first(kj, c, False), jnp.zeros((b, 1), F32))
        dl = first(qi, dl, True)
        dl_ref[...] = dl

        def second(kj, dq):
            off = pl.multiple_of(kj * b, b)
            dsv = p_buf[kj] * (dp_buf[kj] - dl)
            return dq + jnp.dot(dsv.astype(BF16), k_ref[pl.ds(off, b), :], preferred_element_type=F32)

        dq = lax.fori_loop(0, qi + 1, second, jnp.zeros((b, dh), F32))
        dq_ref[...] = (dq * scale).astype(BF16)

    col = pl.BlockSpec((None, b, 1), lambda h, i: (h, i, 0))
    return _call(
        body, side, name=name, grid=(nh, nb),
        in_specs=[pl.BlockSpec((None, b, dh), lambda h, i: (0, i, h)),
                  pl.BlockSpec((None, t, dh), lambda h, i: (1, 0, h)),
                  pl.BlockSpec((None, t, dh), lambda h, i: (2, 0, h)),
                  pl.BlockSpec((None, nb, 1, b), lambda h, i: (h, 0, 0, 0)),
                  pl.BlockSpec((b, dh), lambda h, i: (i, h)), col],
        out_specs=[pl.BlockSpec((None, b, dh), lambda h, i: (2, i, h)), col],
        out_shape=[jax.ShapeDtypeStruct((3, t, d), BF16), jax.ShapeDtypeStruct((nh, t, 1), F32)],
        scratch_shapes=[pltpu.VMEM((nb, b, b), F32), pltpu.VMEM((nb, b, b), F32)],
        sem=("parallel", "parallel"), args=(qkv, qkv, qkv, fk, do, lse))


def fox_bwd_dkv(qkv, fk, do, lse, delta, dqkv, *, side=None, name):
    _, t, d = qkv.shape
    nh = d // FOX_HEAD
    b = _blk(FOX_BLK, t)
    nb = t // b
    dh = FOX_HEAD
    scale = dh ** -0.5

    def body(q_ref, k_ref, v_ref, f_ref, do_ref, lse_ref, dl_ref, dz_in, dkv_ref, df_ref):
        del dz_in
        kj = pl.program_id(1)
        kb, vb, fkv = k_ref[...], v_ref[...], f_ref[...]

        def step(qi, carry, masked):
            dk, dv, df = carry
            off = pl.multiple_of(qi * b, b)
            qb, dob = q_ref[pl.ds(off, b), :], do_ref[pl.ds(off, b), :]
            p = jnp.exp2(_fox_scores(_fox_q(qb), kb, fkv, masked) - lse_ref[pl.ds(off, b), :])
            dv = dv + lax.dot_general(p.astype(BF16), dob, _TN, preferred_element_type=F32)
            dp = lax.dot_general(dob, vb, _NT, preferred_element_type=F32)
            dsv = p * (dp - dl_ref[pl.ds(off, b), :])
            dk = dk + lax.dot_general(dsv.astype(BF16), qb, _TN, preferred_element_type=F32)
            return dk, dv, df - jnp.sum(dsv, axis=0, keepdims=True)

        init = (jnp.zeros((b, dh), F32), jnp.zeros((b, dh), F32), jnp.zeros((1, b), F32))
        carry = step(kj, init, True)
        dk, dv, df = lax.fori_loop(kj + 1, nb, lambda qi, c: step(qi, c, False), carry)
        dkv_ref[0] = (dk * scale).astype(BF16)
        dkv_ref[1] = dv.astype(BF16)
        df_ref[...] = df

    col = pl.BlockSpec((None, t, 1), lambda h, j: (h, 0, 0))
    return _call(
        body, side, name=name, grid=(nh, nb),
        in_specs=[pl.BlockSpec((None, t, dh), lambda h, j: (0, 0, h)),
                  pl.BlockSpec((None, b, dh), lambda h, j: (1, j, h)),
                  pl.BlockSpec((None, b, dh), lambda h, j: (2, j, h)),
                  pl.BlockSpec((None, None, 1, b), lambda h, j: (h, j, 0, 0)),
                  pl.BlockSpec((t, dh), lambda h, j: (0, h)), col, col, pl.BlockSpec(memory_space=pl.ANY)],
        out_specs=[pl.BlockSpec((2, b, dh), lambda h, j: (0, j, h)),
                   pl.BlockSpec((None, None, 1, b), lambda h, j: (h, j, 0, 0))],
        out_shape=[jax.ShapeDtypeStruct((3, t, d), BF16), jax.ShapeDtypeStruct((nh, nb, 1, b), F32)],
        aliases={7: 0}, sem=("parallel", "parallel"), args=(qkv, qkv, qkv, fk, do, lse, delta, dqkv))


FL_BLK = 256


def _log_sigmoid(x):
    return jnp.minimum(x, 0.0) - jnp.log(1.0 + jnp.exp(-jnp.abs(x)))


def fl_fwd(zf, bf, *, name):
    t, n = zf.shape
    bt = _blk(FL_BLK, t)

    def body(z_ref, b_ref, o_ref, carry):
        @pl.when(pl.program_id(0) == 0)
        def _():
            carry[...] = jnp.zeros_like(carry)

        ls = _log_sigmoid(z_ref[...] + b_ref[...])
        r = lax.broadcasted_iota(jnp.int32, (bt, bt), 0)
        c = lax.broadcasted_iota(jnp.int32, (bt, bt), 1)
        cs = jnp.dot((c <= r).astype(F32), ls, preferred_element_type=F32, precision=_HI) + carry[...]
        o_ref[...] = cs
        carry[...] = cs[bt - 1:bt, :]

    return pl.pallas_call(
        body, name=name, grid=(t // bt,),
        in_specs=[pl.BlockSpec((bt, n), lambda i: (i, 0)), pl.BlockSpec((1, n), lambda i: (0, 0))],
        out_specs=pl.BlockSpec((bt, n), lambda i: (i, 0)), out_shape=jax.ShapeDtypeStruct((t, n), F32),
        scratch_shapes=[pltpu.VMEM((1, n), F32)], compiler_params=_params(("arbitrary",)),
    )(zf, bf)


def fl_bwd(df, zf, bf, *, name):
    t, n = zf.shape
    bt = _blk(FL_BLK, t)
    nb = t // bt

    def body(df_ref, z_ref, b_ref, dz_ref, db_ref, carry):
        first = pl.program_id(0) == 0

        @pl.when(first)
        def _():
            carry[...] = jnp.zeros_like(carry)

        r = lax.broadcasted_iota(jnp.int32, (bt, bt), 0)
        c = lax.broadcasted_iota(jnp.int32, (bt, bt), 1)
        dls = jnp.dot((c >= r).astype(F32), df_ref[...], preferred_element_type=F32, precision=_HI) + carry[...]
        carry[...] = dls[0:1, :]
        dz = dls * (1.0 - _sigmoid(z_ref[...] + b_ref[...]))
        dz_ref[...] = dz.astype(BF16)
        part = jnp.sum(dz, axis=0, keepdims=True)

        @pl.when(first)
        def _():
            db_ref[...] = part

        @pl.when(jnp.logical_not(first))
        def _():
            db_ref[...] += part

    row = pl.BlockSpec((bt, n), lambda i: (nb - 1 - i, 0))
    vec = pl.BlockSpec((1, n), lambda i: (0, 0))
    return pl.pallas_call(
        body, name=name, grid=(nb,), in_specs=[row, row, vec], out_specs=[row, vec],
        out_shape=[jax.ShapeDtypeStruct((t, n), BF16), jax.ShapeDtypeStruct((1, n), F32)],
        scratch_shapes=[pltpu.VMEM((1, n), F32)], compiler_params=_params(("arbitrary",)),
    )(df, zf, bf)


def _adamw_math(w, g, m, v):
    m = ADAM_B1 * m + (1.0 - ADAM_B1) * g
    v = ADAM_B2 * v + (1.0 - ADAM_B2) * (g * g)
    m_hat = m / (1.0 - ADAM_B1 ** ADAM_STEP)
    v_hat = v / (1.0 - ADAM_B2 ** ADAM_STEP)
    delta = -ADAM_LR * (m_hat / (jnp.sqrt(v_hat) + ADAM_EPS) + ADAM_WD * w)
    return delta, m, v


def adamw(w, g, m, v, *, side=None, name):
    nl, r, c = w.shape
    br = _row_blk(r, 256)
    nb = r // br

    def body(w_ref, g_ref, m_ref, v_ref, go_ref, d_ref, mo_ref, vo_ref):
        gv = g_ref[...]
        go_ref[...] = gv
        d_ref[...], mo_ref[...], vo_ref[...] = _adamw_math(w_ref[...], gv, m_ref[...], v_ref[...])

    return _call(body, side, name=name, grid=(nl, nb), in_specs=[pl.BlockSpec((None, br, c), lambda l, i: (l, i, 0))] * 4,
                 out_specs=[pl.BlockSpec((br, c), lambda l, i: (l * nb + i, 0))] * 4,
                 out_shape=[jax.ShapeDtypeStruct((nl * r, c), F32)] * 4, sem=("parallel", "parallel"), args=(w, g, m, v))


def _f2(a):
    return a.reshape(a.shape[-2:])


def _local_step(x0, mem, tgt, sp, plan):
    t, d = x0.shape
    mix_a = sp["pool_scale"].shape[1]
    small = {}

    def row(a, l):
        return a[l:l + 1]

    def rows4(g):
        return g.reshape(N_CHIPS, -1, g.shape[-1])

    def xattn_f(l, xin):
        w = plan.weights(f"xa{l}")
        hx = rmsnorm_fwd(xin, row(sp["xa_norm"], l), name=f"xa_norm_f{l}")
        q = _f2(matmul(vm2(hx), w["wq"], "nn", out_dtype=BF16, side=plan.take_fwd(), name=f"xa_q_f{l}"))
        mn = rmsnorm_fwd(mem, row(sp["xa_mem_norm"], l), name=f"xa_memnorm_f{l}")
        kv = _f2(matmul(vm2(mn), w["wkv"], "nn", out_dtype=BF16, name=f"xa_kv_f{l}"))
        o = xattn_fwd(q, kv, name=f"xa_attn_f{l}")
        xout = _f2(matmul(vm2(o), w["wo"], "nn", out_dtype=F32, res=vm2(xin), side=plan.take_fwd(), name=f"xa_o_f{l}"))
        return xout, (xin, hx, q, mn, kv, o)

    def ffn_f(l, xin):
        w = plan.weights(f"ffn{l}")
        hf = rmsnorm_fwd(xin, row(sp["ffn_norm"], l), name=f"ffn_norm_f{l}")
        a = _f2(matmul(vm2(hf), w["gate"], "nn", out_dtype=BF16, side=plan.take_fwd(), name=f"ffn_gate_f{l}"))
        b, s = matmul(vm2(hf), w["up"], "nn", out_dtype=BF16, epi=(_swiglu_epi, [vm2(a)], 2), side=plan.take_fwd(), name=f"ffn_up_f{l}")
        b, s = _f2(b), _f2(s)
        xout = _f2(matmul(vm2(s), w["down"], "nn", out_dtype=F32, res=vm2(xin), side=plan.take_fwd(), name=f"ffn_down_f{l}"))
        return xout, (xin, hf, a, b, s)

    ev = plan.weights("ev")
    h0 = rmsnorm_fwd(x0, sp["ev_norm"], name="ev_norm_f")
    z = _f2(matmul(vm2(h0), ev["ev_in"], "nn", out_dtype=F32, side=plan.take_fwd(), name="ev_in_f"))
    cat = pool_fwd(z, ev["pool"], sp["pool_scale"], name="pool_f")
    cat, states = hgrn_fwd(z, cat, sp["lb"], sp["hg_gain"], mix_a, side=plan.take_fwd(), name="hgrn_f")
    x1 = _f2(matmul(VM(cat), ev["ev_out"], "nn", out_dtype=F32, res=vm2(x0), side=plan.take_fwd(), name="ev_out_f"))
    x2, xa0 = xattn_f(0, x1)
    x3, ff0 = ffn_f(0, x2)

    od = plan.weights("od")
    ho = rmsnorm_fwd(x3, sp["od_norm"], name="od_norm_f")
    qkv = matmul(vm2(ho), od["wqkv"], "nn", out_dtype=BF16, out_p=3, side=plan.take_fwd(), name="od_qkv_f")
    zf = _f2(matmul(vm2(ho), od["wf"], "nn", out_dtype=F32, name="od_fl_f"))
    fcum = fl_fwd(zf, sp["bf"], name="od_forget_f")
    nh = d // FOX_HEAD
    nfb = t // _blk(FOX_BLK, t)
    fk = fcum[:, :nh].T.reshape(nh, nfb, 1, t // nfb)
    of, lse = fox_fwd(qkv, fk, side=plan.take_fwd(), name="fox_f")
    x4 = _f2(matmul(vm2(of), od["od_out"], "nn", out_dtype=F32, res=vm2(x3), name="od_out_f"))
    x5, xa1 = xattn_f(1, x4)
    x6, ff1 = ffn_f(1, x5)
    loss, dx, dxb, small["final_norm"] = loss_head(x6, sp["final_norm"], tgt, name="loss_head")

    def ffn_b(l, saved, dx, dxb):
        xin, hf, a, b, s = saved
        w = plan.weights(f"ffn{l}")
        da, db = matmul(vm2(dxb), w["down"], "nt", out_dtype=BF16, epi=(_swiglu_bwd_epi, [vm2(a), vm2(b)], 2), side=plan.take_bwd(1), name=f"ffn_down_bx{l}")
        da, db = _f2(da), _f2(db)
        g_down = rows4(matmul(vm2(s), vm2(dxb), "tn", out_dtype=BF16, name=f"ffn_down_bw{l}"))
        g_gate = matmul(vm2(hf), vm2(da), "tn", out_dtype=BF16, out_p=N_CHIPS, name=f"ffn_gate_bw{l}")
        g_up = matmul(vm2(hf), vm2(db), "tn", out_dtype=BF16, out_p=N_CHIPS, name=f"ffn_up_bw{l}")
        plan.grads_done({f"down{l}": g_down, f"gate{l}": g_gate, f"up{l}": g_up})
        dh = matmul(vm2(da), w["gate"], "nt", out_dtype=F32, side=plan.take_bwd(1), name=f"ffn_gate_bx{l}")
        dh = _f2(matmul(vm2(db), w["up"], "nt", out_dtype=BF16, res=VM(dh), side=plan.take_bwd(), name=f"ffn_up_bx{l}"))
        dx, dxb, dg = rmsnorm_bwd(xin, row(sp["ffn_norm"], l), dh, dx, name=f"ffn_norm_b{l}")
        return dx, dxb, dg

    def xattn_b(l, saved, dx, dxb):
        xin, hx, q, mn, kv, o = saved
        w = plan.weights(f"xa{l}")
        do = _f2(matmul(vm2(dxb), w["wo"], "nt", out_dtype=BF16, side=plan.take_bwd(), name=f"xa_o_bx{l}"))
        g_wo = rows4(matmul(vm2(o), vm2(dxb), "tn", out_dtype=BF16, name=f"xa_o_bw{l}"))
        dq, dkv = xattn_bwd(q, kv, do, name=f"xa_attn_b{l}")
        g_wq = rows4(matmul(vm2(hx), vm2(dq), "tn", out_dtype=BF16, name=f"xa_q_bw{l}"))
        dh = _f2(matmul(vm2(dq), w["wq"], "nt", out_dtype=BF16, name=f"xa_q_bx{l}"))
        dkvb = dkv.astype(BF16)
        g_wkv = matmul(vm2(mn), vm2(dkvb), "tn", out_dtype=BF16, out_p=N_CHIPS, name=f"xa_kv_bw{l}")
        plan.grads_done({f"wo{l}": g_wo, f"wq{l}": g_wq, f"wkv{l}": g_wkv})
        dmn = _f2(matmul(vm2(dkvb), w["wkv"], "nt", out_dtype=F32, side=plan.take_bwd(), name=f"xa_kv_bx{l}"))
        (dgm,) = rmsnorm_bwd(mem, row(sp["xa_mem_norm"], l), dmn, None, name=f"xa_memnorm_b{l}")
        dx, dxb, dg = rmsnorm_bwd(xin, row(sp["xa_norm"], l), dh, dx, name=f"xa_norm_b{l}")
        return dx, dxb, dg, dgm

    dg_ffn, dg_xa, dg_mem = [None, None], [None, None], [None, None]
    dx, dxb, dg_ffn[1] = ffn_b(1, ff1, dx, dxb)
    dx, dxb, dg_xa[1], dg_mem[1] = xattn_b(1, xa1, dx, dxb)

    do = _f2(matmul(vm2(dxb), od["od_out"], "nt", out_dtype=BF16, side=plan.take_bwd(), name="od_out_bx"))
    g_od_out = rows4(matmul(vm2(of), vm2(dxb), "tn", out_dtype=BF16, name="od_out_bw"))
    dz3, delta = fox_bwd_dq(qkv, fk, do, lse, side=plan.take_bwd(1), name="fox_bq")
    dz3, dfk = fox_bwd_dkv(qkv, fk, do, lse, delta, dz3, side=plan.take_bwd(1), name="fox_bkv")
    dfc = jnp.pad(dfk.reshape(nh, t).T, ((0, 0), (0, zf.shape[1] - nh)))
    dzf, dbf = fl_bwd(dfc, zf, sp["bf"], name="od_forget_b")
    dqkv = VM(dz3, "cs", pfn=lambda p: lax.rem(p + 2, 3))
    dwqkv = _f2(matmul(vm2(ho), dqkv, "tn", out_dtype=BF16, name="od_qkv_bw"))
    dwf = _f2(matmul(vm2(ho), vm2(dzf), "tn", out_dtype=BF16, name="od_fl_bw"))
    od_in_full = jnp.concatenate([dwqkv, dwf[:, :nh]], axis=1)
    plan.grads_done({"od_out": g_od_out})
    plan.grads_done({"od_in": od_in_full.reshape(d, N_CHIPS, -1).transpose(1, 0, 2)})
    dh = matmul(dqkv, od["wqkv"], "nt", out_dtype=F32, side=plan.take_bwd(), name="od_qkv_bx")
    dh = _f2(matmul(vm2(dzf), od["wf"], "nt", out_dtype=BF16, res=VM(dh), name="od_fl_bx"))
    dx, dxb, small["od_norm"] = rmsnorm_bwd(x3, sp["od_norm"], dh, dx, name="od_norm_b")
    small["bf"] = dbf

    dx, dxb, dg_ffn[0] = ffn_b(0, ff0, dx, dxb)
    dx, dxb, dg_xa[0], dg_mem[0] = xattn_b(0, xa0, dx, dxb)

    dcat = matmul(vm2(dxb), ev["ev_out"], "nt", out_dtype=BF16, out_p=2, side=plan.take_bwd(), name="ev_out_bx")
    g_ev_out = rows4(matmul(VM(cat), vm2(dxb), "tn", out_dtype=BF16, name="ev_out_bw"))
    dz5, g_pool, small["pool_scale"] = pool_bwd(z, dcat, ev["pool"], sp["pool_scale"], name="pool_b")
    dz5, small["lb"], dgn = hgrn_bwd(z, dcat, dz5, states, sp["lb"], sp["hg_gain"], mix_a, side=plan.take_bwd(1), name="hgrn_b")
    small["hg_gain"] = jnp.sum(dgn, axis=0)
    dzv = VM(dz5, "cs", pfn=lambda p: lax.rem(p + 4, 5))
    g_ev_in = _f2(matmul(vm2(h0), dzv, "tn", out_dtype=BF16, side=plan.take_bwd(1), name="ev_in_bw"))
    g_ev_in = g_ev_in.reshape(d, N_CHIPS, -1).transpose(1, 0, 2)
    ng, gsz = g_pool.shape[0], g_pool.shape[1]
    pool_parts = g_pool.reshape(ng, N_CHIPS, gsz // N_CHIPS, gsz).transpose(1, 0, 2, 3).reshape(N_CHIPS, gsz, gsz).astype(BF16)
    plan.grads_done({"ev_out": g_ev_out, "pool": pool_parts, "ev_in": g_ev_in}, now=True)
    dh = _f2(matmul(dzv, ev["ev_in"], "nt", out_dtype=BF16, side=plan.take_bwd(1), name="ev_in_bx"))
    dx, _, small["ev_norm"] = rmsnorm_bwd(x0, sp["ev_norm"], dh, dx, name="ev_norm_b")

    small["xa_norm"] = jnp.concatenate(dg_xa, axis=0)
    small["xa_mem_norm"] = jnp.concatenate(dg_mem, axis=0)
    small["ffn_norm"] = jnp.concatenate(dg_ffn, axis=0)
    return loss, dx, small


def gather_forward(fulls, shards, *, name):
    n = len(fulls)
    side = ForwardSide(fulls, shards)

    def body(*refs):
        ins, outs = refs[:2 * n], refs[2 * n:3 * n]
        ssem, rsem = refs[3 * n:]
        side.start(ins, outs, ssem, rsem)
        side.finish(ins, outs, ssem, rsem)

    return pl.pallas_call(
        body, name=name, in_specs=[_ANY] * (2 * n), out_specs=[_ANY] * n, out_shape=side.out_shape,
        input_output_aliases=side.aliases,
        scratch_shapes=[pltpu.SemaphoreType.DMA((n, 4)), pltpu.SemaphoreType.DMA((n, 4))],
    )(*side.inputs)


def gather_shards(shards, *, name):
    n = len(shards)

    def body(*refs):
        ins, outs = refs[:n], refs[n:2 * n]
        ssem, rsem = refs[2 * n:]
        x, y, c, chips = _me()
        mine = _chip_id((x, y))
        sibling = (x, y, 1 - c)

        def rows(w, chip_id, which):
            h = shards[w].shape[0] // 2
            return outs[w].at[chip_id, pl.ds(which * h, h)]

        def to_chip(w, j):
            h = shards[w].shape[0] // 2
            return _rcopy(ins[w].at[pl.ds(c * h, h)], rows(w, mine, c), ssem.at[w, j], rsem.at[w, j], (*chips[j], c))

        def from_chip(w, j):
            r = rows(w, _chip_id(chips[j]), c)
            return _rcopy(r, r, ssem.at[w, j], rsem.at[w, j], (*chips[j], c))

        def to_sibling(w, j):
            r = rows(w, _chip_id(chips[j]), c)
            return _rcopy(r, r, ssem.at[w, 3 + j], rsem.at[w, 3 + j], sibling)

        def from_sibling(w, j):
            r = rows(w, _chip_id(chips[j]), 1 - c)
            return _rcopy(r, r, ssem.at[w, 3 + j], rsem.at[w, 3 + j], sibling)

        def own(w):
            return _rcopy(ins[w], outs[w].at[mine], ssem.at[w, 6], rsem.at[w, 6], sibling)

        for w in range(n):
            own(w).start()
            for j in range(3):
                to_chip(w, j).start()
        for w in range(n):
            for j in range(3):
                from_chip(w, j).wait_recv()
                to_sibling(w, j).start()
        for w in range(n):
            for j in range(3):
                from_sibling(w, j).wait_recv()
        for w in range(n):
            own(w).wait()
            for j in range(3):
                to_chip(w, j).wait_send()
                to_sibling(w, j).wait_send()

    return pl.pallas_call(
        body, name=name, in_specs=[_ANY] * n, out_specs=[_ANY] * n,
        out_shape=[jax.ShapeDtypeStruct((N_CHIPS,) + s.shape, s.dtype) for s in shards],
        scratch_shapes=[pltpu.SemaphoreType.DMA((n, 7)), pltpu.SemaphoreType.DMA((n, 7))],
    )(*shards)


def _ids_spec(grid, in_specs, out_specs):
    return pltpu.PrefetchScalarGridSpec(num_scalar_prefetch=1, grid=grid, in_specs=in_specs, out_specs=out_specs)


def rs_pair(parts, *, name):
    n = len(parts)

    def body(*refs):
        ins, recv = refs[:n], refs[n:2 * n]
        ssem, rsem = refs[2 * n:]
        x, y, c, _ = _me()
        sibling = (x, y, 1 - c)

        def swap(w):
            h = parts[w].shape[1] // 2
            return _rcopy(ins[w].at[:, pl.ds((1 - c) * h, h), :], recv[w], ssem.at[w], rsem.at[w], sibling)

        for w in range(n):
            swap(w).start()
        for w in range(n):
            swap(w).wait()

    return pl.pallas_call(
        body, name=name, in_specs=[_ANY] * n, out_specs=[_ANY] * n,
        out_shape=[jax.ShapeDtypeStruct((p.shape[0], p.shape[1] // 2, p.shape[2]), p.dtype) for p in parts],
        scratch_shapes=[pltpu.SemaphoreType.DMA((n,)), pltpu.SemaphoreType.DMA((n,))],
    )(*parts)


def add_pair(part, recv, ids, *, name):
    p, h, c = recv.shape
    br = _row_blk(h, 512)
    nb = h // br

    def body(ids_ref, a_ref, b_ref, o_ref):
        del ids_ref
        o_ref[...] = (a_ref[...].astype(F32) + b_ref[...].astype(F32)).astype(o_ref.dtype)

    half = pl.BlockSpec((None, br, c), lambda k, i, ids: (k, i, 0))
    return pl.pallas_call(
        body, name=name, out_shape=jax.ShapeDtypeStruct(recv.shape, recv.dtype),
        grid_spec=_ids_spec((p, nb), [pl.BlockSpec((None, br, c), lambda k, i, ids: (k, ids[1] * nb + i, 0)), half], half),
        compiler_params=_params(("parallel", "parallel")),
    )(ids, part, recv)


def rs_chip(sums, *, name):
    n = len(sums)

    def body(*refs):
        ins, outs = refs[:n], refs[n:2 * n]
        ssem, rsem = refs[2 * n:]
        x, y, c, chips = _me()

        def swap(w, j):
            return _rcopy(ins[w].at[_chip_id(chips[j])], outs[w].at[j], ssem.at[w, j], rsem.at[w, j], (*chips[j], c))

        for w in range(n):
            for j in range(3):
                swap(w, j).start()
        for w in range(n):
            for j in range(3):
                swap(w, j).wait()

    return pl.pallas_call(
        body, name=name, in_specs=[_ANY] * n, out_specs=[_ANY] * n,
        out_shape=[jax.ShapeDtypeStruct((3,) + s.shape[1:], s.dtype) for s in sums],
        scratch_shapes=[pltpu.SemaphoreType.DMA((n, 3)), pltpu.SemaphoreType.DMA((n, 3))],
    )(*sums)


def add_chips(sums, landed, ids, group, layer, group_shape, *, name):
    _, h, c = sums.shape
    br = _row_blk(h, 256)
    nb = h // br

    def body(ids_ref, a_ref, b_ref, *rest):
        o_ref = rest[-1]
        tot = a_ref[...].astype(F32)
        for k in range(3):
            tot = tot + b_ref[k].astype(F32)
        o_ref[...] = tot

    in_specs = [pl.BlockSpec((None, br, c), lambda i, ids: (ids[0], i, 0)), pl.BlockSpec((3, br, c), lambda i, ids: (0, i, 0))]
    args = [ids, sums, landed]
    if group is not None:
        in_specs.append(_ANY)
        args.append(group)
    return pl.pallas_call(
        body, name=name, out_shape=jax.ShapeDtypeStruct(group_shape, F32),
        input_output_aliases={3: 0} if group is not None else {},
        grid_spec=_ids_spec((nb,), in_specs, pl.BlockSpec((None, br, c), lambda i, ids: (layer, ids[1] * nb + i, 0))),
        compiler_params=_params(("parallel",)),
    )(*args)


def rs_share(groups, slots, *, name):
    ng = len(groups)
    n = len(slots)

    def body(*refs):
        outs = refs[ng:2 * ng]
        ssem, rsem = refs[2 * ng:]
        x, y, c, _ = _me()
        sibling = (x, y, 1 - c)

        def rows(w, which):
            g, l = slots[w]
            h = groups[g].shape[1] // 2
            return outs[g].at[l, pl.ds(which * h, h), :]

        def swap(w):
            return _rcopy(rows(w, c), rows(w, c), ssem.at[w], rsem.at[w], sibling)

        for w in range(n):
            swap(w).start()
        for w in range(n):
            swap(w).wait_send()
            _rcopy(rows(w, 1 - c), rows(w, 1 - c), ssem.at[w], rsem.at[w], sibling).wait_recv()

    return pl.pallas_call(
        body, name=name, in_specs=[_ANY] * ng, out_specs=[_ANY] * ng,
        out_shape=[jax.ShapeDtypeStruct(g.shape, g.dtype) for g in groups],
        input_output_aliases={g: g for g in range(ng)},
        scratch_shapes=[pltpu.SemaphoreType.DMA((n,)), pltpu.SemaphoreType.DMA((n,))],
    )(*groups)


def allreduce_small(v, *, name):
    r, c = v.shape
    ndev = 2 * N_CHIPS

    def body(v_ref, o_ref, buf, ssem, rsem):
        x, y, cc, _ = _me()
        me = 4 * x + 2 * y + cc
        flips = [(a, b, d) for a in (0, 1) for b in (0, 1) for d in (0, 1)][1:]
        buf[me] = v_ref[...]
        cps = []
        for k, (a, b, d) in enumerate(flips):
            peer = (jnp.bitwise_xor(x, a), jnp.bitwise_xor(y, b), jnp.bitwise_xor(cc, d))
            cp = _rcopy(v_ref, buf.at[me], ssem.at[k], rsem.at[k], peer)
            cp.start()
            cps.append(cp)
        for k, (a, b, d) in enumerate(flips):
            peer = (jnp.bitwise_xor(x, a), jnp.bitwise_xor(y, b), jnp.bitwise_xor(cc, d))
            src = 4 * peer[0] + 2 * peer[1] + peer[2]
            _rcopy(v_ref, buf.at[src], ssem.at[k], rsem.at[k], peer).wait_recv()
        for cp in cps:
            cp.wait_send()
        tot = buf[0]
        for k in range(1, ndev):
            tot = tot + buf[k]
        o_ref[...] = tot

    vm = pl.BlockSpec(memory_space=pltpu.VMEM)
    return pl.pallas_call(
        body, name=name, in_specs=[vm], out_specs=vm, out_shape=jax.ShapeDtypeStruct((r, c), F32),
        scratch_shapes=[pltpu.VMEM((ndev, r, c), F32), pltpu.SemaphoreType.DMA((ndev - 1,)), pltpu.SemaphoreType.DMA((ndev - 1,))],
    )(v)


WEIGHTS = ["lb_table", "ev_norm", "ev_w_in", "ev_w_pool", "ev_pool_scale", "ev_hg_norm", "ev_w_out", "od_norm", "od_w_in",
           "od_b_f", "od_w_out", "xa_norm", "xa_mem_norm", "xa_wq", "xa_wkv", "xa_wo", "ffn_norm", "ffn_w_gate", "ffn_w_up",
           "ffn_w_down", "final_norm"]
BIG = ["ev_w_in", "ev_w_pool", "ev_w_out", "od_w_in", "od_w_out", "xa_wq", "xa_wkv", "xa_wo", "ffn_w_gate", "ffn_w_up", "ffn_w_down"]
SMALL_ROWS = 16


def _rows(parts, width):
    rows = [jnp.pad(p.reshape(-1, p.shape[-1]).astype(F32), ((0, 0), (0, width - p.shape[-1]))) for p in parts]
    out = jnp.concatenate(rows, axis=0)
    return jnp.pad(out, ((0, SMALL_ROWS - out.shape[0]), (0, 0)))


def _unrows(packed, like):
    out, r = [], 0
    for p in like:
        n = p.size // p.shape[-1]
        out.append(packed[r:r + n, :p.shape[-1]].reshape(p.shape))
        r += n
    return out


def _m3(a):
    return a.reshape(a.shape[0], -1, a.shape[-1])


SLOT = {"ev_in": ("ev_w_in", 0), "pool": ("ev_w_pool", 0), "ev_out": ("ev_w_out", 0), "od_in": ("od_w_in", 0),
        "od_out": ("od_w_out", 0)}
for _l in range(2):
    SLOT.update({f"wq{_l}": ("xa_wq", _l), f"wkv{_l}": ("xa_wkv", _l), f"wo{_l}": ("xa_wo", _l),
                 f"gate{_l}": ("ffn_w_gate", _l), f"up{_l}": ("ffn_w_up", _l), f"down{_l}": ("ffn_w_down", _l)})
GATHER_FIRST = ["ev_in", "ev_out", "pool", "od_norm"]
GATHER_CARRIED = [["wq0", "wo0"], ["wkv0", "gate0"], ["od_out"], ["wq1"], ["wo1"], ["up0"], ["down0"], ["od_in"], ["wkv1"],
                  ["gate1", "up1", "down1"]]


class _Lazy:
    def __init__(self, plan, group):
        self.plan, self.layer = plan, group[-1] if group[-1] in "01" else ""

    def __getitem__(self, key):
        return self.plan.w(key + self.layer if key in ("wq", "wo", "wkv", "gate", "up", "down") else key)


class _Plan:
    def __init__(self, shards, ids, group_shapes, d, nh):
        self.shards, self.ids, self.group_shapes, self.d, self.nh = shards, ids, group_shapes, d, nh
        self.full, self.cache = {}, {}
        self.queue, self.sides, self.fsides, self.forwarded = [list(u) for u in GATHER_CARRIED], [], [], set()
        self.parts, self.psides, self.sums, self.rqueue, self.rsides = [], [], {}, [], []
        got = gather_shards([shards[n] for n in GATHER_FIRST], name="gather_first")
        for n, f in zip(GATHER_FIRST, got):
            self.full[n] = f

    def take_fwd(self):
        parts = []
        ready = [(ns, s) for ns, s in self.sides if s.outs is not None and ns[0] not in self.forwarded]
        for ns, s in ready:
            fs = ForwardSide(s.outs, [self.shards[n] for n in ns])
            self.fsides.append((ns, fs))
            self.forwarded.update(ns)
            parts.append(fs)
        if self.queue:
            names = self.queue.pop(0)
            side = GatherSide([self.shards[n] for n in names])
            self.sides.append((names, side))
            parts.append(side)
        return Sides(parts) if parts else None

    def _need(self, names):
        missing = [n for n in names if n not in self.full]
        if not missing:
            return
        done = {n: a for ns, s in self.fsides if s.outs is not None for n, a in zip(ns, s.outs)}
        landed = {n: a for ns, s in self.sides if s.outs is not None for n, a in zip(ns, s.outs)}
        pre = {n: done[n] for n in missing if n in done}
        half = [n for n in missing if n not in done and n in landed]
        late = [n for n in missing if n not in done and n not in landed]
        if half:
            self.forwarded.update(half)
            pre.update(zip(half, gather_forward([landed[n] for n in half], [self.shards[n] for n in half],
                                                name=f"gather_forward_{half[0]}")))
        if late:
            self.queue = [u for u in ([n for n in u if n not in late] for u in self.queue) if u]
            pre.update(zip(late, gather_shards([self.shards[n] for n in late], name=f"gather_late_{late[0]}")))
        for n in missing:
            self.full[n] = pre[n]

    def w(self, name):
        if name in self.cache:
            return self.cache[name]
        if name in ("wqkv", "wf"):
            self._need(["od_in"])
            od_full = self.full["od_in"].transpose(1, 0, 2).reshape(self.d, -1)
            self.cache["wqkv"] = vm2(od_full[:, :3 * self.d])
            self.cache["wf"] = vm2(jnp.pad(od_full[:, 3 * self.d:], ((0, 0), (0, 128 - self.nh))))
            return self.cache[name]
        self._need([name])
        f = self.full[name]
        if name == "pool":
            rows, gsz = f.shape[1:]
            ng = rows * N_CHIPS // gsz
            out = f.reshape(N_CHIPS, ng, gsz // N_CHIPS, gsz).transpose(1, 0, 2, 3).reshape(ng, gsz, gsz)
        elif name == "ev_in":
            out = vm2(f.transpose(1, 0, 2).reshape(self.d, -1))
        else:
            out = VM(f, "cs") if name.rstrip("01") in ("wkv", "gate", "up") else vm2(f.reshape(-1, f.shape[-1]))
        self.cache[name] = out
        return out

    def weights(self, group):
        return _Lazy(self, group)

    def grads_done(self, parts, now=False):
        names = list(parts)
        if now:
            got = rs_pair([parts[n] for n in names], name=f"reduce_pair_{names[0]}")
            for n, g in zip(names, got):
                self.sums[n] = add_pair(parts[n], g, self.ids, name=f"reduce_add2_{n}")
            self.rqueue.append(names)
        else:
            self.parts.append((names, [parts[n] for n in names]))

    def _add_swapped(self):
        for names, parts, side in self.psides:
            if side.outs is not None and names[0] not in self.sums:
                for n, p, g in zip(names, parts, side.outs):
                    self.sums[n] = add_pair(p, g, self.ids, name=f"reduce_add2_{n}")
                self.rqueue.append(names)

    def take_bwd(self, units=0):
        self._add_swapped()
        sides = []
        for names, parts in self.parts:
            ps = PairSide(parts)
            self.psides.append((names, parts, ps))
            sides.append(ps)
        self.parts = []
        names = [n for u in self.rqueue[:units] for n in u]
        self.rqueue = self.rqueue[units:]
        if names:
            rs = ReduceSide([self.sums[n] for n in names])
            self.rsides.append((names, rs))
            sides.append(rs)
        return Sides(sides) if sides else None

    def finish(self):
        for names, parts in self.parts:
            self.grads_done(dict(zip(names, parts)), now=True)
        self._add_swapped()
        landed = {}
        for ns, side in self.rsides:
            landed.update(zip(ns, side.outs))
        rest = [n for u in self.rqueue for n in u]
        if rest:
            landed.update(zip(rest, rs_chip([self.sums[n] for n in rest], name="reduce_chips_rest")))
        gbig = {n: None for n in BIG}
        for n, (big, l) in SLOT.items():
            gbig[big] = add_chips(self.sums[n], landed[n], self.ids, gbig[big], l, self.group_shapes[big], name=f"reduce_add4_{n}")
        full = rs_share([gbig[n] for n in BIG], [(BIG.index(big), l) for big, l in SLOT.values()], name="reduce_share")
        return dict(zip(BIG, full))


def kernel(x, mem, lb_table, ev_norm, ev_w_in, ev_w_pool, ev_pool_scale, ev_hg_norm, ev_w_out, od_norm, od_w_in, od_b_f, od_w_out, xa_norm, xa_mem_norm, xa_wq, xa_wkv, xa_wo, ffn_norm, ffn_w_gate, ffn_w_up, ffn_w_down, final_norm, loss_target, m_lb_table, m_ev_norm, m_ev_w_in, m_ev_w_pool, m_ev_pool_scale, m_ev_hg_norm, m_ev_w_out, m_od_norm, m_od_w_in, m_od_b_f, m_od_w_out, m_xa_norm, m_xa_mem_norm, m_xa_wq, m_xa_wkv, m_xa_wo, m_ffn_norm, m_ffn_w_gate, m_ffn_w_up, m_ffn_w_down, m_final_norm, v_lb_table, v_ev_norm, v_ev_w_in, v_ev_w_pool, v_ev_pool_scale, v_ev_hg_norm, v_ev_w_out, v_od_norm, v_od_w_in, v_od_b_f, v_od_w_out, v_xa_norm, v_xa_mem_norm, v_xa_wq, v_xa_wkv, v_xa_wo, v_ffn_norm, v_ffn_w_gate, v_ffn_w_up, v_ffn_w_down, v_final_norm):
    a = dict(locals())
    w = {n: a[n] for n in WEIGHTS}
    mom = {n: a["m_" + n] for n in WEIGHTS}
    var = {n: a["v_" + n] for n in WEIGHTS}
    _, t, d = x.shape
    nh = d // FOX_HEAD
    lanes = 128
    cx, cy = lax.axis_index("x"), lax.axis_index("y")
    chip = 2 * cx + cy

    w3 = {n: _m3(w[n]) for n in BIG}
    shards = {"od_norm": jnp.broadcast_to(od_norm, (16, od_norm.shape[1]))}
    for name, (big, l) in SLOT.items():
        shards[name] = w3[big][l].astype(BF16)
    ids = jnp.stack([chip, lax.axis_index("c")]).astype(jnp.int32)
    plan = _Plan(shards, ids, {n: w3[n].shape for n in BIG}, d, nh)
    od_norm_full = plan.full["od_norm"][:, 0, :].reshape(1, d)

    sm = jax.nn.softmax(lb_table, axis=0)
    sp = {
        "lb": sm[1:2], "ev_norm": ev_norm, "pool_scale": ev_pool_scale, "hg_gain": ev_hg_norm, "od_norm": od_norm_full,
        "bf": jnp.pad(od_b_f, ((0, 0), (0, lanes - nh))), "xa_norm": xa_norm, "xa_mem_norm": xa_mem_norm, "ffn_norm": ffn_norm,
        "final_norm": final_norm.reshape(1, d),
    }
    loss_l, gx, small = _local_step(x[0], mem[0], loss_target[0], sp, plan)
    loss = lax.psum(loss_l[0, 0], ("x", "y", "c"))
    gbig = plan.finish()

    raw_like = [small["lb"], small["ev_norm"], small["pool_scale"], small["hg_gain"], small["od_norm"], small["bf"],
                small["xa_norm"], small["xa_mem_norm"], small["ffn_norm"], small["final_norm"]]
    summed = _unrows(allreduce_small(_rows(raw_like, d), name="reduce_small"), raw_like)
    dlb, g_ev_norm, g_pool_scale, g_hg, g_od_norm_full, g_bf, g_xa, g_xam, g_ffn, g_final = summed
    dsm = jnp.zeros_like(sm).at[1:2].set(dlb)
    gsmall = {
        "lb_table": sm * (dsm - jnp.sum(sm * dsm, axis=0, keepdims=True)), "ev_norm": g_ev_norm, "ev_pool_scale": g_pool_scale,
        "ev_hg_norm": g_hg, "od_norm": lax.dynamic_slice_in_dim(g_od_norm_full, chip * od_norm.shape[1], od_norm.shape[1], axis=1),
        "od_b_f": g_bf[:, :nh], "xa_norm": g_xa, "xa_mem_norm": g_xam, "ffn_norm": g_ffn, "final_norm": g_final.reshape(d),
    }

    grad, delta, new_m, new_v = {}, {}, {}, {}
    for n in BIG:
        res = adamw(w3[n], gbig[n], _m3(mom[n]), _m3(var[n]), name=f"adamw_{n}")
        grad[n], delta[n], new_m[n], new_v[n] = [r.reshape(w[n].shape) for r in res]
    snames = [n for n in WEIGHTS if n not in BIG]
    like = [w[n] for n in snames]
    res = adamw(_rows(like, d)[None], _rows([gsmall[n] for n in snames], d)[None], _rows([mom[n] for n in snames], d)[None],
                _rows([var[n] for n in snames], d)[None], name="adamw_small")
    for vals, dst in zip(res, (grad, delta, new_m, new_v)):
        dst.update(zip(snames, _unrows(vals, like)))
    return (loss, gx.reshape(x.shape), *[grad[n] for n in WEIGHTS], *[delta[n] for n in WEIGHTS],
            *[new_m[n] for n in WEIGHTS], *[new_v[n] for n in WEIGHTS])
```

```python
import functools
import math

import jax
import jax.numpy as jnp
from jax import lax
from jax.experimental import pallas as pl
from jax.experimental.pallas import tpu as pltpu

F32 = jnp.float32
BF16 = jnp.bfloat16
MESH = pl.DeviceIdType.MESH

V7X_VMEM_LIMIT_BYTES = 56 * 1024 * 1024
N_CHIPS = 4

EPS = 1e-6
POOL_WINDOWS = (2, 4, 8, 16)
POOL_HALO = 16
HG_HEAD = 128
HG_CHUNK = 64
FOX_HEAD = 128
FOX_BLK = 512
XA_HEADS = 4
ADAM_LR, ADAM_B1, ADAM_B2, ADAM_EPS, ADAM_WD, ADAM_STEP = 0.001, 0.9, 0.999, 1e-08, 0.01, 10
EXP_CLAMP = 80.0


def _params(sem=None):
    return pltpu.CompilerParams(dimension_semantics=sem, vmem_limit_bytes=V7X_VMEM_LIMIT_BYTES)


def _blk(pref, dim):
    b = min(pref, dim)
    assert dim % b == 0, (pref, dim)
    return b


class VM:
    def __init__(self, arr, kind="cs", lead=(), inner=(), pfn=None):
        self.arr, self.kind, self.lead, self.inner = arr, kind, tuple(lead), tuple(inner)
        self.pfn = pfn or (lambda p: p)
        p = arr.shape[len(self.lead)]
        r, c = arr.shape[-2:]
        assert arr.ndim == len(self.lead) + 1 + len(self.inner) + 2, (arr.shape, lead, inner)
        self.P = p
        self.shape = (r, c * p) if kind == "cs" else (r * p, c)
        self.dtype = arr.dtype

    def spec(self, br, bc, rfn, cfn):
        p = self.P
        r, c = self.arr.shape[-2:]
        assert c % bc == 0 and r % br == 0, (self.arr.shape, br, bc)
        if p == 1:
            def imap(*g):
                return (*self.lead, self.pfn(0), *self.inner, rfn(*g), cfn(*g))
        elif self.kind == "cs":
            per = c // bc

            def imap(*g):
                cb = cfn(*g)
                return (*self.lead, self.pfn(lax.div(cb, per)), *self.inner, rfn(*g), lax.rem(cb, per))
        else:
            per = r // br

            def imap(*g):
                rb = rfn(*g)
                return (*self.lead, self.pfn(lax.div(rb, per)), *self.inner, lax.rem(rb, per), cfn(*g))
        return pl.BlockSpec((None,) * (self.arr.ndim - 2) + (br, bc), imap)


def vm2(arr):
    return VM(arr.reshape((1,) + arr.shape))


def _out_struct(shape, kind, p, dtype):
    r, c = shape
    return jax.ShapeDtypeStruct((p, r, c // p) if kind == "cs" else (p, r // p, c), dtype)


_ANY = pl.BlockSpec(memory_space=pl.ANY)


def _me():
    x, y, c = lax.axis_index("x"), lax.axis_index("y"), lax.axis_index("c")
    chips = [(1 - x, y), (x, 1 - y), (1 - x, 1 - y)]
    return x, y, c, chips


def _chip_id(xy):
    return 2 * xy[0] + xy[1]


def _rcopy(src, dst, ssem, rsem, dev):
    return pltpu.make_async_remote_copy(src_ref=src, dst_ref=dst, send_sem=ssem, recv_sem=rsem, device_id=dev,
                                        device_id_type=MESH)


class GatherSide:
    def __init__(self, shards):
        self.inputs = list(shards)
        self.out_shape = [jax.ShapeDtypeStruct((N_CHIPS,) + s.shape, s.dtype) for s in shards]
        self.aliases = {}
        self.rows = len(shards)
        self.outs = None

    def _copy(self, ins, outs, ssem, rsem, w, j, receive):
        x, y, c, chips = _me()
        h = self.inputs[w].shape[0] // 2
        half = pl.ds(c * h, h)
        if receive:
            r = outs[w].at[_chip_id(chips[j]), half]
            return _rcopy(r, r, ssem.at[w, j], rsem.at[w, j], (*chips[j], c))
        return _rcopy(ins[w].at[half], outs[w].at[_chip_id((x, y)), half], ssem.at[w, j], rsem.at[w, j], (*chips[j], c))

    def start(self, ins, outs, ssem, rsem):
        for w in range(len(self.inputs)):
            for j in range(3):
                self._copy(ins, outs, ssem, rsem, w, j, False).start()

    def finish(self, ins, outs, ssem, rsem):
        for w in range(len(self.inputs)):
            for j in range(3):
                self._copy(ins, outs, ssem, rsem, w, j, True).wait_recv()
                self._copy(ins, outs, ssem, rsem, w, j, False).wait_send()


class ForwardSide:
    def __init__(self, fulls, shards):
        self.inputs = list(fulls) + list(shards)
        self.out_shape = [jax.ShapeDtypeStruct(f.shape, f.dtype) for f in fulls]
        self.aliases = {w: w for w in range(len(fulls))}
        self.rows = len(fulls)
        self.outs = None

    def _copy(self, outs, ssem, rsem, w, j, receive):
        x, y, c, chips = _me()
        h = self.inputs[w].shape[1] // 2
        r = outs[w].at[_chip_id(chips[j]), pl.ds(((1 - c) if receive else c) * h, h)]
        return _rcopy(r, r, ssem.at[w, j], rsem.at[w, j], (x, y, 1 - c))

    def _own(self, ins, outs, ssem, rsem, w):
        x, y, c, _ = _me()
        return _rcopy(ins[self.rows + w], outs[w].at[_chip_id((x, y))], ssem.at[w, 3], rsem.at[w, 3], (x, y, 1 - c))

    def start(self, ins, outs, ssem, rsem):
        for w in range(self.rows):
            self._own(ins, outs, ssem, rsem, w).start()
            for j in range(3):
                self._copy(outs, ssem, rsem, w, j, False).start()

    def finish(self, ins, outs, ssem, rsem):
        for w in range(self.rows):
            self._own(ins, outs, ssem, rsem, w).wait()
            for j in range(3):
                self._copy(outs, ssem, rsem, w, j, False).wait_send()
                self._copy(outs, ssem, rsem, w, j, True).wait_recv()


class PairSide:
    def __init__(self, parts):
        self.inputs = list(parts)
        self.out_shape = [jax.ShapeDtypeStruct((p.shape[0], p.shape[1] // 2, p.shape[2]), p.dtype) for p in parts]
        self.aliases = {}
        self.rows = len(parts)
        self.outs = None

    def _copy(self, ins, outs, ssem, rsem, w):
        x, y, c, _ = _me()
        h = self.inputs[w].shape[1] // 2
        return _rcopy(ins[w].at[:, pl.ds((1 - c) * h, h), :], outs[w], ssem.at[w, 0], rsem.at[w, 0], (x, y, 1 - c))

    def start(self, ins, outs, ssem, rsem):
        for w in range(self.rows):
            self._copy(ins, outs, ssem, rsem, w).start()

    def finish(self, ins, outs, ssem, rsem):
        for w in range(self.rows):
            self._copy(ins, outs, ssem, rsem, w).wait()


class _SemRows:
    def __init__(self, sem, off):
        self.sem, self.off = sem, off

    @property
    def at(self):
        return self

    def __getitem__(self, idx):
        return self.sem.at[self.off + idx[0], idx[1]]


class Sides:
    def __init__(self, sides):
        self.sides = list(sides)
        self.inputs = [a for s in self.sides for a in s.inputs]
        self.out_shape = [o for s in self.sides for o in s.out_shape]
        self.rows = sum(s.rows for s in self.sides)
        self.aliases, i0, o0 = {}, 0, 0
        for s in self.sides:
            self.aliases.update({i0 + i: o0 + o for i, o in s.aliases.items()})
            i0, o0 = i0 + len(s.inputs), o0 + len(s.out_shape)

    def _each(self, method, ins, outs, ssem, rsem):
        i0 = o0 = r0 = 0
        for s in self.sides:
            getattr(s, method)(ins[i0:i0 + len(s.inputs)], outs[o0:o0 + len(s.out_shape)], _SemRows(ssem, r0), _SemRows(rsem, r0))
            i0, o0, r0 = i0 + len(s.inputs), o0 + len(s.out_shape), r0 + s.rows

    def start(self, ins, outs, ssem, rsem):
        self._each("start", ins, outs, ssem, rsem)

    def finish(self, ins, outs, ssem, rsem):
        self._each("finish", ins, outs, ssem, rsem)

    @property
    def outs(self):
        return None

    @outs.setter
    def outs(self, vals):
        o0 = 0
        for s in self.sides:
            s.outs = list(vals[o0:o0 + len(s.out_shape)])
            o0 += len(s.out_shape)


class ReduceSide:
    def __init__(self, sums):
        self.inputs = list(sums)
        self.out_shape = [jax.ShapeDtypeStruct((3,) + s.shape[1:], s.dtype) for s in sums]
        self.aliases = {}
        self.rows = len(sums)
        self.outs = None

    def _copy(self, ins, outs, ssem, rsem, w, j):
        _, _, c, chips = _me()
        return _rcopy(ins[w].at[_chip_id(chips[j])], outs[w].at[j], ssem.at[w, j], rsem.at[w, j], (*chips[j], c))

    def start(self, ins, outs, ssem, rsem):
        for w in range(len(self.inputs)):
            for j in range(3):
                self._copy(ins, outs, ssem, rsem, w, j).start()

    def finish(self, ins, outs, ssem, rsem):
        for w in range(len(self.inputs)):
            for j in range(3):
                self._copy(ins, outs, ssem, rsem, w, j).wait()


def _call(body, side, *, name, grid, in_specs, out_specs, out_shape, scratch_shapes=(), sem, aliases=None, args):
    if side is None:
        return pl.pallas_call(body, name=name, grid=grid, in_specs=in_specs, out_specs=out_specs, out_shape=out_shape,
                              scratch_shapes=list(scratch_shapes), input_output_aliases=aliases or {},
                              compiler_params=_params(sem))(*args)
    single = not isinstance(out_shape, (list, tuple))
    oshape, ospecs = ([out_shape], [out_specs]) if single else (list(out_shape), list(out_specs))
    n_in, n_out, s_in, s_out = len(in_specs), len(oshape), len(side.inputs), len(side.out_shape)

    def wrapped(*refs):
        ins, sin = refs[:n_in], refs[n_in:n_in + s_in]
        outs = refs[n_in + s_in:n_in + s_in + n_out]
        souts = refs[n_in + s_in + n_out:n_in + s_in + n_out + s_out]
        rest = refs[n_in + s_in + n_out + s_out:]
        scratch, (ssem, rsem) = rest[:-2], rest[-2:]
        first = functools.reduce(jnp.logical_and, [pl.program_id(a) == 0 for a in range(len(grid))])
        last = functools.reduce(jnp.logical_and, [pl.program_id(a) == grid[a] - 1 for a in range(len(grid))])

        @pl.when(first)
        def _():
            side.start(sin, souts, ssem, rsem)

        body(*ins, *outs, *scratch)

        @pl.when(last)
        def _():
            side.finish(sin, souts, ssem, rsem)

    sems = pltpu.SemaphoreType.DMA((side.rows, 4))
    res = pl.pallas_call(
        wrapped, name=name, grid=grid, in_specs=list(in_specs) + [_ANY] * s_in, out_specs=ospecs + [_ANY] * s_out,
        out_shape=oshape + side.out_shape, scratch_shapes=list(scratch_shapes) + [sems, sems],
        input_output_aliases={**(aliases or {}), **{n_in + i: n_out + o for i, o in side.aliases.items()}},
        compiler_params=_params(("arbitrary",) * len(grid)),
    )(*args, *side.inputs)
    side.outs = list(res[n_out:])
    return res[0] if single else list(res[:n_out])


def _best(g, cap):
    if g <= cap:
        return g
    cands = [d for d in range(128, cap + 1, 128) if g % d == 0]
    assert cands, (g, cap)
    return cands[-1]


def _row_blk(n, cap):
    cands = [d for d in range(16, min(n, cap) + 1, 16) if n % d == 0]
    assert cands, (n, cap)
    return cands[-1]


def _tiles(a, b, mode, out_kind, out_p, bm, bn, bk):
    def cpiece(v):
        return v.arr.shape[-1] if v.kind == "cs" else v.shape[1]

    def rpiece(v):
        return v.arr.shape[-2] if v.kind == "rs" else v.shape[0]

    if mode == "nn":
        m, n = a.shape[0], b.shape[1]
        gm, gn, gk = rpiece(a), cpiece(b), math.gcd(cpiece(a), rpiece(b))
    elif mode == "nt":
        m, n = a.shape[0], b.shape[0]
        gm, gn, gk = rpiece(a), rpiece(b), math.gcd(cpiece(a), cpiece(b))
    else:
        m, n = a.shape[1], b.shape[1]
        gm, gn, gk = cpiece(a), cpiece(b), math.gcd(rpiece(a), rpiece(b))
    if out_kind == "cs":
        gn = math.gcd(gn, n // out_p)
    else:
        gm = math.gcd(gm, m // out_p)
    caps = {"nn": (1024, 1536, 2048), "nt": (512, 2048, 2048), "tn": (1536, 1536, 2048)}[mode]
    return (bm or _best(gm, caps[0])), (bn or _best(gn, caps[1])), (bk or _best(gk, caps[2]))


def matmul(a, b, mode, *, out_dtype, bm=None, bn=None, bk=None, out_kind="cs", out_p=1, out_pfn=None, res=None, epi=None,
           side=None, name):
    bm, bn, bk = _tiles(a, b, mode, out_kind, out_p, bm, bn, bk)
    if mode == "nn":
        (m, k), (k2, n) = a.shape, b.shape
        a_spec = a.spec(bm, bk, lambda i, j, kk: i, lambda i, j, kk: kk)
        b_spec = b.spec(bk, bn, lambda i, j, kk: kk, lambda i, j, kk: j)
        dims = (((1,), (0,)), ((), ()))
    elif mode == "nt":
        (m, k), (n, k2) = a.shape, b.shape
        a_spec = a.spec(bm, bk, lambda i, j, kk: i, lambda i, j, kk: kk)
        b_spec = b.spec(bn, bk, lambda i, j, kk: j, lambda i, j, kk: kk)
        dims = (((1,), (1,)), ((), ()))
    else:
        (k, m), (k2, n) = a.shape, b.shape
        a_spec = a.spec(bk, bm, lambda i, j, kk: kk, lambda i, j, kk: i)
        b_spec = b.spec(bk, bn, lambda i, j, kk: kk, lambda i, j, kk: j)
        dims = (((0,), (0,)), ((), ()))
    assert k == k2, (a.shape, b.shape, mode)
    assert m % bm == 0 and n % bn == 0 and k % bk == 0, (m, n, k, bm, bn, bk)
    nk = k // bk
    out_sds = _out_struct((m, n), out_kind, out_p, out_dtype)
    out_vm = VM(out_sds, out_kind, pfn=out_pfn)
    o_spec = out_vm.spec(bm, bn, lambda i, j, kk: i, lambda i, j, kk: j)
    in_specs, args = [a_spec, b_spec], [a.arr, b.arr]
    tiles = ([res] if res is not None else []) + (list(epi[1]) if epi else [])
    for v in tiles:
        assert v.shape == (m, n)
        in_specs.append(v.spec(bm, bn, lambda i, j, kk: i, lambda i, j, kk: j))
        args.append(v.arr)
    n_out = epi[2] if epi else 1

    def body(a_ref, b_ref, *rest):
        t_refs, o_refs = rest[:len(tiles)], rest[len(tiles):len(tiles) + n_out]
        part = lax.dot_general(a_ref[...], b_ref[...], dims, preferred_element_type=F32)

        def write(tot):
            if res is not None:
                tot = tot + t_refs[0][...].astype(F32)
            outs = epi[0](tot, *[r[...].astype(F32) for r in t_refs[len(tiles) - len(epi[1]):]]) if epi else (tot,)
            for o_ref, val in zip(o_refs, outs):
                o_ref[...] = val.astype(o_ref.dtype)

        if nk == 1:
            write(part)
            return
        acc = rest[-1]
        kk = pl.program_id(2)

        @pl.when(kk == 0)
        def _():
            acc[...] = part

        @pl.when(kk > 0)
        def _():
            acc[...] += part

        @pl.when(kk == nk - 1)
        def _():
            write(acc[...])

    return _call(body, side, name=name, grid=(m // bm, n // bn, nk), in_specs=in_specs,
                 out_specs=o_spec if n_out == 1 else [o_spec] * n_out, out_shape=out_sds if n_out == 1 else [out_sds] * n_out,
                 scratch_shapes=[pltpu.VMEM((bm, bn), F32)] if nk > 1 else [],
                 sem=("parallel", "parallel", "arbitrary"), args=args)


def rmsnorm_fwd(x, g, *, name):
    t, d = x.shape
    bt = _blk(512, t)

    def body(x_ref, g_ref, o_ref):
        xv = x_ref[...]
        r = lax.rsqrt(jnp.mean(xv * xv, axis=-1, keepdims=True) + EPS)
        o_ref[...] = (xv * r * g_ref[...]).astype(o_ref.dtype)

    return pl.pallas_call(
        body, name=name, grid=(t // bt,),
        in_specs=[pl.BlockSpec((bt, d), lambda i: (i, 0)), pl.BlockSpec((1, d), lambda i: (0, 0))],
        out_specs=pl.BlockSpec((bt, d), lambda i: (i, 0)), out_shape=jax.ShapeDtypeStruct((t, d), BF16),
        compiler_params=_params(("parallel",)),
    )(x, g)


def rmsnorm_bwd(x, g, dh, dres, *, name):
    t, d = x.shape
    bt = _blk(256, t)
    want_dx = dres is not None

    def body(x_ref, g_ref, dh_ref, *rest):
        if want_dx:
            dres_ref, dx_ref, dxb_ref, dg_ref = rest
        else:
            (dg_ref,) = rest
        xv = x_ref[...]
        dhv = dh_ref[...].astype(F32)
        r = lax.rsqrt(jnp.mean(xv * xv, axis=-1, keepdims=True) + EPS)
        xh = xv * r
        part = jnp.sum(dhv * xh, axis=0, keepdims=True)

        @pl.when(pl.program_id(0) == 0)
        def _():
            dg_ref[...] = part

        @pl.when(pl.program_id(0) > 0)
        def _():
            dg_ref[...] += part

        if want_dx:
            dy = dhv * g_ref[...]
            dxn = r * (dy - xh * jnp.mean(dy * xh, axis=-1, keepdims=True))
            dx = dres_ref[...] + dxn
            dx_ref[...] = dx
            dxb_ref[...] = dx.astype(BF16)

    row = pl.BlockSpec((bt, d), lambda i: (i, 0))
    vec = pl.BlockSpec((1, d), lambda i: (0, 0))
    in_specs, args = [row, vec, row], [x, g, dh]
    out_specs, out_shape = [vec], [jax.ShapeDtypeStruct((1, d), F32)]
    if want_dx:
        in_specs.append(row)
        args.append(dres)
        out_specs = [row, row] + out_specs
        out_shape = [jax.ShapeDtypeStruct((t, d), F32), jax.ShapeDtypeStruct((t, d), BF16)] + out_shape
    return pl.pallas_call(
        body, name=name, grid=(t // bt,), in_specs=in_specs, out_specs=out_specs, out_shape=out_shape,
        compiler_params=_params(("arbitrary",)),
    )(*args)


def loss_head(x, g, tgt, *, name):
    t, d = x.shape
    bt = _blk(256, t)

    def body(x_ref, g_ref, t_ref, loss_ref, dx_ref, dxb_ref, dg_ref):
        xv = x_ref[...]
        gv = g_ref[...]
        r = lax.rsqrt(jnp.mean(xv * xv, axis=-1, keepdims=True) + EPS)
        xh = xv * r
        e = xh * gv - t_ref[...]
        lpart = jnp.zeros((1, 128), F32) + jnp.sum(e * e) * (0.5 / d)
        dyv = e * (1.0 / d)
        gpart = jnp.sum(dyv * xh, axis=0, keepdims=True)

        @pl.when(pl.program_id(0) == 0)
        def _():
            loss_ref[...] = lpart
            dg_ref[...] = gpart

        @pl.when(pl.program_id(0) > 0)
        def _():
            loss_ref[...] += lpart
            dg_ref[...] += gpart

        dy = dyv * gv
        dx = r * (dy - xh * jnp.mean(dy * xh, axis=-1, keepdims=True))
        dx_ref[...] = dx
        dxb_ref[...] = dx.astype(BF16)

    row = pl.BlockSpec((bt, d), lambda i: (i, 0))
    vec = pl.BlockSpec((1, d), lambda i: (0, 0))
    return pl.pallas_call(
        body, name=name, grid=(t // bt,), in_specs=[row, vec, row],
        out_specs=[pl.BlockSpec((1, 128), lambda i: (0, 0)), row, row, vec],
        out_shape=[jax.ShapeDtypeStruct((1, 128), F32), jax.ShapeDtypeStruct((t, d), F32),
                   jax.ShapeDtypeStruct((t, d), BF16), jax.ShapeDtypeStruct((1, d), F32)],
        compiler_params=_params(("arbitrary",)),
    )(x, g, tgt)


def _sigmoid(x):
    return 1.0 / (1.0 + jnp.exp(-x))


def _swiglu_epi(b, a):
    return b, a * _sigmoid(a) * b


def _swiglu_bwd_epi(ds, a, b):
    sg = _sigmoid(a)
    return ds * b * sg * (1.0 + a * (1.0 - sg)), ds * a * sg


def _xa_probs(qh, kh, scale):
    s = lax.dot_general(qh, kh, (((1,), (1,)), ((), ())), preferred_element_type=F32) * scale
    s = s - jnp.max(s, axis=-1, keepdims=True)
    p = jnp.exp(s)
    return p / jnp.sum(p, axis=-1, keepdims=True)


def xattn_fwd(q, kv, *, name):
    t, d = q.shape
    m = kv.shape[0]
    hd = d // XA_HEADS
    bt = _blk(512, t)
    scale = hd ** -0.5

    def body(q_ref, kv_ref, o_ref):
        for h in range(XA_HEADS):
            qh = q_ref[:, h * hd:(h + 1) * hd]
            kh = kv_ref[:, h * hd:(h + 1) * hd]
            vh = kv_ref[:, d + h * hd:d + (h + 1) * hd]
            p = _xa_probs(qh, kh, scale)
            o_ref[:, h * hd:(h + 1) * hd] = jnp.dot(p.astype(BF16), vh, preferred_element_type=F32).astype(BF16)

    return pl.pallas_call(
        body, name=name, grid=(t // bt,),
        in_specs=[pl.BlockSpec((bt, d), lambda i: (i, 0)), pl.BlockSpec((m, 2 * d), lambda i: (0, 0))],
        out_specs=pl.BlockSpec((bt, d), lambda i: (i, 0)), out_shape=jax.ShapeDtypeStruct((t, d), BF16),
        compiler_params=_params(("parallel",)),
    )(q, kv)


def xattn_bwd(q, kv, do, *, name):
    t, d = q.shape
    m = kv.shape[0]
    hd = d // XA_HEADS
    bt = _blk(512, t)
    scale = hd ** -0.5

    def body(q_ref, kv_ref, do_ref, dq_ref, dkv_ref):
        first = pl.program_id(0) == 0
        for h in range(XA_HEADS):
            qs, ks, vs = slice(h * hd, (h + 1) * hd), slice(h * hd, (h + 1) * hd), slice(d + h * hd, d + (h + 1) * hd)
            qh, kh, vh, doh = q_ref[:, qs], kv_ref[:, ks], kv_ref[:, vs], do_ref[:, qs]
            p = _xa_probs(qh, kh, scale)
            dp = lax.dot_general(doh, vh, (((1,), (1,)), ((), ())), preferred_element_type=F32)
            dsv = p * (dp - jnp.sum(p * dp, axis=-1, keepdims=True)) * scale
            dsb = dsv.astype(BF16)
            dq_ref[:, qs] = jnp.dot(dsb, kh, preferred_element_type=F32).astype(BF16)
            dk = lax.dot_general(dsb, qh, (((0,), (0,)), ((), ())), preferred_element_type=F32)
            dv = lax.dot_general(p.astype(BF16), doh, (((0,), (0,)), ((), ())), preferred_element_type=F32)

            @pl.when(first)
            def _():
                dkv_ref[:, ks] = dk
                dkv_ref[:, vs] = dv

            @pl.when(jnp.logical_not(first))
            def _():
                dkv_ref[:, ks] += dk
                dkv_ref[:, vs] += dv

    row = pl.BlockSpec((bt, d), lambda i: (i, 0))
    full = pl.BlockSpec((m, 2 * d), lambda i: (0, 0))
    return pl.pallas_call(
        body, name=name, grid=(t // bt,), in_specs=[row, full, row], out_specs=[row, full],
        out_shape=[jax.ShapeDtypeStruct((t, d), BF16), jax.ShapeDtypeStruct((m, 2 * d), F32)],
        compiler_params=_params(("arbitrary",)),
    )(q, kv, do)


def _pool_p(buf, uv, rows, w, bt):
    acc = uv
    for dd in range(1, w):
        acc = acc + buf[pl.ds(POOL_HALO - dd, bt), :]
    cnt = jnp.minimum(rows + 1, w).astype(F32)
    return acc / cnt - uv


def pool_fwd(z, w_pool, scale, *, name):
    t = z.shape[0]
    ng, gsz = w_pool.shape[0], w_pool.shape[1]
    mix = ng * gsz
    bt = _blk(512, t)

    def body(u_ref, uh_ref, w_ref, sc_ref, o_ref, buf):
        r = pl.program_id(0)
        rows = r * bt + lax.broadcasted_iota(jnp.int32, (bt, 1), 0)
        for g in range(ng):
            gs = slice(g * gsz, (g + 1) * gsz)
            uv = u_ref[:, gs]
            buf[0:POOL_HALO, :] = jnp.where(r > 0, uh_ref[:, gs], 0.0)
            buf[POOL_HALO:POOL_HALO + bt, :] = uv
            p = _pool_p(buf, uv, rows, POOL_WINDOWS[g], bt)
            y = jnp.dot(p.astype(BF16), w_ref[g], preferred_element_type=F32) * sc_ref[:, gs]
            o_ref[:, gs] = y.astype(BF16)

    hb = bt // POOL_HALO
    return pl.pallas_call(
        body, name=name, grid=(t // bt,),
        in_specs=[pl.BlockSpec((bt, mix), lambda i: (i, 0)),
                  pl.BlockSpec((POOL_HALO, mix), lambda i: (jnp.maximum(i * hb - 1, 0), 0)),
                  pl.BlockSpec((ng, gsz, gsz), lambda i: (0, 0, 0)), pl.BlockSpec((1, mix), lambda i: (0, 0))],
        out_specs=pl.BlockSpec((None, bt, mix), lambda i: (0, i, 0)),
        out_shape=jax.ShapeDtypeStruct((2, t, mix), BF16),
        scratch_shapes=[pltpu.VMEM((POOL_HALO + bt, gsz), F32)],
        compiler_params=_params(("parallel",)),
    )(z, z, w_pool, scale)


def pool_bwd(z, dcat, w_pool, scale, *, name):
    t = z.shape[0]
    ng, gsz = w_pool.shape[0], w_pool.shape[1]
    mix = ng * gsz
    bt = _blk(512, t)
    nb = t // bt
    nt_dims = (((1,), (1,)), ((), ()))
    tn_dims = (((0,), (0,)), ((), ()))

    def body(u_ref, uh_ref, dy_ref, dyh_ref, w_ref, sc_ref, du_ref, dw_ref, dsc_ref, buf, buf2):
        r = pl.program_id(0)
        first = r == 0
        rows = r * bt + lax.broadcasted_iota(jnp.int32, (bt, 1), 0)
        rows_h = (r + 1) * bt + lax.broadcasted_iota(jnp.int32, (POOL_HALO, 1), 0)
        for g in range(ng):
            w = POOL_WINDOWS[g]
            gs = slice(g * gsz, (g + 1) * gsz)
            uv = u_ref[:, gs]
            buf[0:POOL_HALO, :] = jnp.where(r > 0, uh_ref[:, gs], 0.0)
            buf[POOL_HALO:POOL_HALO + bt, :] = uv
            pb = _pool_p(buf, uv, rows, w, bt).astype(BF16)
            wg = w_ref[g]
            sc = sc_ref[:, gs]
            y0 = jnp.dot(pb, wg, preferred_element_type=F32)
            dyv = dy_ref[:, gs].astype(F32)
            dsc = jnp.sum(dyv * y0, axis=0, keepdims=True)
            dyw = (dyv * sc).astype(BF16)
            dw = lax.dot_general(pb, dyw, tn_dims, preferred_element_type=F32)

            @pl.when(first)
            def _():
                dw_ref[g] = dw
                dsc_ref[:, gs] = dsc

            @pl.when(jnp.logical_not(first))
            def _():
                dw_ref[g] += dw
                dsc_ref[:, gs] += dsc

            dp = lax.dot_general(dyw, wg, nt_dims, preferred_element_type=F32)
            dyh = (dyh_ref[:, gs].astype(F32) * sc).astype(BF16)
            dph = lax.dot_general(dyh, wg, nt_dims, preferred_element_type=F32)
            dph = jnp.where(r < nb - 1, dph, 0.0)
            buf2[0:bt, :] = dp / jnp.minimum(rows + 1, w).astype(F32)
            buf2[bt:bt + POOL_HALO, :] = dph / jnp.minimum(rows_h + 1, w).astype(F32)
            acc = buf2[pl.ds(0, bt), :]
            for dd in range(1, w):
                acc = acc + buf2[pl.ds(dd, bt), :]
            du_ref[:, gs] = (acc - dp).astype(BF16)

    hb = bt // POOL_HALO
    nhb = t // POOL_HALO
    return pl.pallas_call(
        body, name=name, grid=(nb,),
        in_specs=[pl.BlockSpec((bt, mix), lambda i: (i, 0)),
                  pl.BlockSpec((POOL_HALO, mix), lambda i: (jnp.maximum(i * hb - 1, 0), 0)),
                  pl.BlockSpec((None, bt, mix), lambda i: (0, i, 0)),
                  pl.BlockSpec((None, POOL_HALO, mix), lambda i: (0, jnp.minimum((i + 1) * hb, nhb - 1), 0)),
                  pl.BlockSpec((ng, gsz, gsz), lambda i: (0, 0, 0)), pl.BlockSpec((1, mix), lambda i: (0, 0))],
        out_specs=[pl.BlockSpec((None, bt, mix), lambda i: (4, i, 0)),
                   pl.BlockSpec((ng, gsz, gsz), lambda i: (0, 0, 0)), pl.BlockSpec((1, mix), lambda i: (0, 0))],
        out_shape=[jax.ShapeDtypeStruct((5, t, mix), BF16), jax.ShapeDtypeStruct((ng, gsz, gsz), F32),
                   jax.ShapeDtypeStruct((1, mix), F32)],
        scratch_shapes=[pltpu.VMEM((POOL_HALO + bt, gsz), F32), pltpu.VMEM((bt + POOL_HALO, gsz), F32)],
        compiler_params=_params(("arbitrary",)),
    )(z, z, dcat, dcat, w_pool, scale)


HG_HEADS_PER_STEP = 8
HG_LEVELS = ((64, 31), (32, 15), (16, 7))
HG_DIAG = (8, 3)
_NT = (((1,), (1,)), ((), ()))
_TN = (((0,), (0,)), ((), ()))
_HI = lax.Precision.HIGHEST


def _hg_masks():
    c = HG_CHUNK
    t = lax.broadcasted_iota(jnp.int32, (c, c), 0)
    s = lax.broadcasted_iota(jnp.int32, (c, c), 1)
    masks = []
    for blk, row in HG_LEVELS:
        sh = blk.bit_length() - 1
        same = (t >> sh) == (s >> sh)
        masks.append(same & ((t & (blk - 1)) > row) & ((s & (blk - 1)) <= row))
    sh = HG_DIAG[0].bit_length() - 1
    masks.append(((t >> sh) == (s >> sh)) & (s <= t))
    return t, s, masks


def _row_of_block(x, blk, row):
    c, n = x.shape
    x3 = x.reshape(c // blk, blk, n)
    return jnp.broadcast_to(x3[:, row:row + 1, :], x3.shape).reshape(c, n)


def _hg_parts(qv, flv, lb, masks, tri):
    sgf = _sigmoid(flv)
    f = lb + (1.0 - lb) * sgf
    logf = jnp.log(f)
    kk = 1.0 - f
    sgq = _sigmoid(qv)
    qf = qv * sgq * (HG_HEAD ** -0.5)
    bc = jnp.dot(tri, logf, preferred_element_type=F32, precision=_HI)
    levels = []
    a = None
    for li, (blk, row) in enumerate(HG_LEVELS + (HG_DIAG,)):
        e = bc - _row_of_block(bc, blk, row)
        if li < len(HG_LEVELS):
            eq, ek = jnp.exp(jnp.minimum(e, 0.0)), jnp.exp(jnp.minimum(-e, 0.0))
        else:
            eq, ek = jnp.exp(jnp.clip(e, -EXP_CLAMP, EXP_CLAMP)), jnp.exp(jnp.clip(-e, -EXP_CLAMP, EXP_CLAMP))
        qt, kt = qf * eq, kk * ek
        part = jnp.where(masks[li], lax.dot_general(qt.astype(BF16), kt.astype(BF16), _NT, preferred_element_type=F32), 0.0)
        a = part if a is None else a + part
        levels.append((eq, ek, qt, kt))
    return dict(sgf=sgf, f=f, kk=kk, sgq=sgq, qf=qf, bc=bc, levels=levels, a=a)


def hgrn_fwd(z, cat, lb, gain, mix_a, *, side=None, name):
    t = z.shape[0]
    mix_b = lb.shape[1]
    nh = mix_b // HG_HEAD
    bt = _blk(256, t)
    ncb = bt // HG_CHUNK
    dh = HG_HEAD

    def body(q_ref, fl_ref, i_ref, g_ref, lb_ref, gain_ref, cat_in, o_ref, st_ref, st):
        del cat_in

        @pl.when(pl.program_id(1) == 0)
        def _():
            st[...] = jnp.zeros_like(st)

        t_i, s_i, masks = _hg_masks()
        tri = (s_i <= t_i).astype(F32)
        lbv, gn = lb_ref[...], gain_ref[...]
        for c in range(ncb):
            rs = slice(c * HG_CHUNK, (c + 1) * HG_CHUNK)
            pr = _hg_parts(q_ref[rs, :], fl_ref[rs, :], lbv, masks, tri)
            vb = i_ref[rs, :].astype(BF16)
            stv = st[...]
            st_ref[c] = stv
            bc = pr["bc"]
            qt = pr["qf"] * jnp.exp(bc)
            o = (jnp.dot(pr["a"].astype(BF16), vb, preferred_element_type=F32)
                 + lax.dot_general(qt.astype(BF16), stv.astype(BF16), _NT, preferred_element_type=F32))
            bl = bc[HG_CHUNK - 1:HG_CHUNK, :]
            khat = pr["kk"] * jnp.exp(bl - bc)
            st[...] = stv * jnp.exp(bl) + lax.dot_general(vb, khat.astype(BF16), _TN, preferred_element_type=F32)
            r = lax.rsqrt(jnp.mean(o * o, axis=-1, keepdims=True) + EPS)
            gv = g_ref[rs, :]
            o_ref[rs, :] = (o * r * gn * (gv * _sigmoid(gv))).astype(BF16)

    def col(which):
        base = (mix_a + which * mix_b) // dh
        return pl.BlockSpec((bt, dh), lambda h, i: (i, base + h))

    return _call(
        body, side, name=name, grid=(nh, t // bt),
        in_specs=[col(0), col(1), col(2), col(3), pl.BlockSpec((1, dh), lambda h, i: (0, h)),
                  pl.BlockSpec((1, dh), lambda h, i: (0, 0)), _ANY],
        out_specs=[pl.BlockSpec((None, bt, dh), lambda h, i: (1, i, h)),
                   pl.BlockSpec((None, ncb, dh, dh), lambda h, i: (h, i, 0, 0))],
        out_shape=[jax.ShapeDtypeStruct(cat.shape, BF16), jax.ShapeDtypeStruct((nh, t // HG_CHUNK, dh, dh), F32)],
        scratch_shapes=[pltpu.VMEM((dh, dh), F32)], aliases={6: 0}, sem=("parallel", "arbitrary"),
        args=(z, z, z, z, lb, gain, cat))


def hgrn_bwd(z, dcat, dz5, states, lb, gain, mix_a, *, side=None, name):
    t = z.shape[0]
    mix_b = lb.shape[1]
    nh = mix_b // HG_HEAD
    bt = _blk(256, t)
    nb = t // bt
    ncb = bt // HG_CHUNK
    dh = HG_HEAD
    hp = HG_HEADS_PER_STEP if nh % HG_HEADS_PER_STEP == 0 else 1

    def body(q_ref, fl_ref, i_ref, g_ref, dy_ref, st_ref, lb_ref, gain_ref, dz_in, dz_ref, dlb_ref, dgn_ref, dst):
        del dz_in
        first = pl.program_id(1) == 0

        @pl.when(first)
        def _():
            dst[...] = jnp.zeros_like(dst)

        t_i, s_i, masks = _hg_masks()
        tri = (s_i <= t_i).astype(F32)
        triu = (s_i >= t_i).astype(F32)
        last_row = lax.broadcasted_iota(jnp.int32, (HG_CHUNK, 1), 0) == HG_CHUNK - 1
        gn = gain_ref[...]
        dlb_acc = [jnp.zeros((1, dh), F32) for _ in range(hp)]
        dgn_acc = [jnp.zeros((1, dh), F32) for _ in range(hp)]
        for c, hh in [(c, hh) for c in reversed(range(ncb)) for hh in range(hp)]:
            rs, cs = slice(c * HG_CHUNK, (c + 1) * HG_CHUNK), slice(hh * dh, (hh + 1) * dh)
            lbv = lb_ref[:, cs]
            qv, flv, gv = q_ref[rs, cs], fl_ref[rs, cs], g_ref[rs, cs]
            pr = _hg_parts(qv, flv, lbv, masks, tri)
            vb = i_ref[rs, cs].astype(BF16)
            stv = st_ref[hh, c]
            stb = stv.astype(BF16)
            dsv = dst[hh]
            dsb = dsv.astype(BF16)
            bc, kk, qf, ab = pr["bc"], pr["kk"], pr["qf"], pr["a"].astype(BF16)
            ebc = jnp.exp(bc)
            qt = qf * ebc
            qtb = qt.astype(BF16)
            o = jnp.dot(ab, vb, preferred_element_type=F32) + lax.dot_general(qtb, stb, _NT, preferred_element_type=F32)
            r = lax.rsqrt(jnp.mean(o * o, axis=-1, keepdims=True) + EPS)
            oh = o * r
            sgg = _sigmoid(gv)
            dyv = dy_ref[rs, cs].astype(F32)
            don = dyv * (gv * sgg)
            dgate = dyv * (oh * gn) * (sgg * (1.0 + gv * (1.0 - sgg)))
            dgn_acc[hh] = dgn_acc[hh] + jnp.sum(don * oh, axis=0, keepdims=True)
            doh = don * gn
            do = r * (doh - oh * jnp.mean(doh * oh, axis=-1, keepdims=True))
            dob = do.astype(BF16)
            bl = bc[HG_CHUNK - 1:HG_CHUNK, :]
            ebl = jnp.exp(bl)
            ekh = jnp.exp(bl - bc)
            khat = kk * ekh
            dv = (lax.dot_general(ab, dob, _TN, preferred_element_type=F32)
                  + lax.dot_general(khat.astype(BF16), dsb, _NT, preferred_element_type=F32))
            da = lax.dot_general(dob, vb, _NT, preferred_element_type=F32)
            dqt = jnp.dot(dob, stb, preferred_element_type=F32)
            dkh = jnp.dot(vb, dsb, preferred_element_type=F32)
            dst[hh] = dsv * ebl + lax.dot_general(dob, qtb, _TN, preferred_element_type=F32)
            dbl = jnp.sum(dsv * stv, axis=0, keepdims=True) * ebl + jnp.sum(dkh * khat, axis=0, keepdims=True)
            dqf = dqt * ebc
            dkk = dkh * ekh
            dbc = dqt * qt - dkh * khat
            for li, (eq, ek, qtl, ktl) in enumerate(pr["levels"]):
                gm = jnp.where(masks[li], da, 0.0).astype(BF16)
                qtr, ktr = qtl.astype(BF16), ktl.astype(BF16)
                dql = jnp.dot(gm, ktr, preferred_element_type=F32)
                dkl = lax.dot_general(gm, qtr, _TN, preferred_element_type=F32)
                dqf = dqf + dql * eq
                dkk = dkk + dkl * ek
                dbc = dbc + qtr.astype(F32) * dql - ktr.astype(F32) * dkl
            dbc = dbc + jnp.where(last_row, dbl, 0.0)
            dlogf = jnp.dot(triu, dbc, preferred_element_type=F32, precision=_HI)
            df = dlogf / pr["f"] - dkk
            sgf = pr["sgf"]
            dfl = df * (1.0 - lbv) * sgf * (1.0 - sgf)
            dlb_acc[hh] = dlb_acc[hh] + jnp.sum(df * (1.0 - sgf), axis=0, keepdims=True)
            sgq = pr["sgq"]
            dq = dqf * (HG_HEAD ** -0.5) * (sgq * (1.0 + qv * (1.0 - sgq)))
            dz_ref[0, rs, cs] = dq.astype(BF16)
            dz_ref[1, rs, cs] = dfl.astype(BF16)
            dz_ref[2, rs, cs] = dv.astype(BF16)
            dz_ref[3, rs, cs] = dgate.astype(BF16)

        @pl.when(first)
        def _():
            for hh in range(hp):
                dlb_ref[:, hh * dh:(hh + 1) * dh] = dlb_acc[hh]
                dgn_ref[hh] = dgn_acc[hh]

        @pl.when(jnp.logical_not(first))
        def _():
            for hh in range(hp):
                dlb_ref[:, hh * dh:(hh + 1) * dh] += dlb_acc[hh]
                dgn_ref[hh] += dgn_acc[hh]

    wd = hp * dh

    def col(which):
        base = (mix_a + which * mix_b) // wd
        return pl.BlockSpec((bt, wd), lambda h, i: (nb - 1 - i, base + h))

    return _call(
        body, side, name=name, grid=(nh // hp, nb),
        in_specs=[col(0), col(1), col(2), col(3),
                  pl.BlockSpec((None, bt, wd), lambda h, i: (1, nb - 1 - i, h)),
                  pl.BlockSpec((hp, ncb, dh, dh), lambda h, i: (h, nb - 1 - i, 0, 0)),
                  pl.BlockSpec((1, wd), lambda h, i: (0, h)), pl.BlockSpec((1, dh), lambda h, i: (0, 0)), _ANY],
        out_specs=[pl.BlockSpec((4, bt, wd), lambda h, i: (0, nb - 1 - i, h)),
                   pl.BlockSpec((1, wd), lambda h, i: (0, h)),
                   pl.BlockSpec((hp, 1, dh), lambda h, i: (h, 0, 0))],
        out_shape=[jax.ShapeDtypeStruct(dz5.shape, BF16), jax.ShapeDtypeStruct((1, mix_b), F32),
                   jax.ShapeDtypeStruct((nh, 1, dh), F32)],
        scratch_shapes=[pltpu.VMEM((hp, dh, dh), F32)], aliases={8: 0}, sem=("parallel", "arbitrary"),
        args=(z, z, z, z, dcat, states, lb, gain, dz5))


LOG2E = 1.4426950408889634


def _fox_q(qb):
    return (qb.astype(F32) * (FOX_HEAD ** -0.5 * LOG2E)).astype(BF16)


def _fox_scores(qs, kb, fk, masked):
    s = lax.dot_general(qs, kb, _NT, preferred_element_type=F32) - fk * LOG2E
    if masked:
        n = s.shape[0]
        row = lax.broadcasted_iota(jnp.int32, (n, n), 0)
        col = lax.broadcasted_iota(jnp.int32, (n, n), 1)
        s = jnp.where(col <= row, s, -jnp.inf)
    return s


def fox_fwd(qkv, fk, *, side=None, name):
    _, t, d = qkv.shape
    nh = d // FOX_HEAD
    b = _blk(FOX_BLK, t)
    nb = t // b
    dh = FOX_HEAD

    def body(q_ref, k_ref, v_ref, f_ref, o_ref, lse_ref):
        qi = pl.program_id(1)
        qs = _fox_q(q_ref[...])

        def step(kj, carry, masked):
            m, l, acc = carry
            off = pl.multiple_of(kj * b, b)
            s = _fox_scores(qs, k_ref[pl.ds(off, b), :], f_ref[kj], masked)
            m_new = jnp.maximum(m, jnp.max(s, axis=-1, keepdims=True))
            alpha = jnp.exp2(m - m_new)
            p = jnp.exp2(s - m_new)
            l = alpha * l + jnp.sum(p, axis=-1, keepdims=True)
            acc = alpha * acc + jnp.dot(p.astype(BF16), v_ref[pl.ds(off, b), :], preferred_element_type=F32)
            return m_new, l, acc

        init = (jnp.full((b, 1), -jnp.inf, F32), jnp.zeros((b, 1), F32), jnp.zeros((b, dh), F32))
        carry = lax.fori_loop(0, qi, lambda kj, c: step(kj, c, False), init)
        m, l, acc = step(qi, carry, True)
        o_ref[...] = (acc / l).astype(BF16)
        lse_ref[...] = m + jnp.log(l) * LOG2E

    return _call(
        body, side, name=name, grid=(nh, nb),
        in_specs=[pl.BlockSpec((None, b, dh), lambda h, i: (0, i, h)),
                  pl.BlockSpec((None, t, dh), lambda h, i: (1, 0, h)),
                  pl.BlockSpec((None, t, dh), lambda h, i: (2, 0, h)),
                  pl.BlockSpec((None, nb, 1, b), lambda h, i: (h, 0, 0, 0))],
        out_specs=[pl.BlockSpec((b, dh), lambda h, i: (i, h)), pl.BlockSpec((None, b, 1), lambda h, i: (h, i, 0))],
        out_shape=[jax.ShapeDtypeStruct((t, d), BF16), jax.ShapeDtypeStruct((nh, t, 1), F32)],
        sem=("parallel", "parallel"), args=(qkv, qkv, qkv, fk))


def fox_bwd_dq(qkv, fk, do, lse, *, side=None, name):
    _, t, d = qkv.shape
    nh = d // FOX_HEAD
    b = _blk(FOX_BLK, t)
    nb = t // b
    dh = FOX_HEAD
    scale = dh ** -0.5

    def body(q_ref, k_ref, v_ref, f_ref, do_ref, lse_ref, dq_ref, dl_ref, p_buf, dp_buf):
        qi = pl.program_id(1)
        qs, dob, lse_v = _fox_q(q_ref[...]), do_ref[...], lse_ref[...]

        def first(kj, dl, masked):
            off = pl.multiple_of(kj * b, b)
            p = jnp.exp2(_fox_scores(qs, k_ref[pl.ds(off, b), :], f_ref[kj], masked) - lse_v)
            dp = lax.dot_general(dob, v_ref[pl.ds(off, b), :], _NT, preferred_element_type=F32)
            p_buf[kj] = p
            dp_buf[kj] = dp
            return dl + jnp.sum(p * dp, axis=-1, keepdims=True)

        dl = lax.fori_loop(0, qi, lambda kj, c: first(kj, c, False), jnp.zeros((b, 1), F32))
        dl = first(qi, dl, True)
        dl_ref[...] = dl

        def second(kj, dq):
            off = pl.multiple_of(kj * b, b)
            dsv = p_buf[kj] * (dp_buf[kj] - dl)
            return dq + jnp.dot(dsv.astype(BF16), k_ref[pl.ds(off, b), :], preferred_element_type=F32)

        dq = lax.fori_loop(0, qi + 1, second, jnp.zeros((b, dh), F32))
        dq_ref[...] = (dq * scale).astype(BF16)

    col = pl.BlockSpec((None, b, 1), lambda h, i: (h, i, 0))
    return _call(
        body, side, name=name, grid=(nh, nb),
        in_specs=[pl.BlockSpec((None, b, dh), lambda h, i: (0, i, h)),
                  pl.BlockSpec((None, t, dh), lambda h, i: (1, 0, h)),
                  pl.BlockSpec((None, t, dh), lambda h, i: (2, 0, h)),
                  pl.BlockSpec((None, nb, 1, b), lambda h, i: (h, 0, 0, 0)),
                  pl.BlockSpec((b, dh), lambda h, i: (i, h)), col],
        out_specs=[pl.BlockSpec((None, b, dh), lambda h, i: (2, i, h)), col],
        out_shape=[jax.ShapeDtypeStruct((3, t, d), BF16), jax.ShapeDtypeStruct((nh, t, 1), F32)],
        scratch_shapes=[pltpu.VMEM((nb, b, b), F32), pltpu.VMEM((nb, b, b), F32)],
        sem=("parallel", "parallel"), args=(qkv, qkv, qkv, fk, do, lse))


def fox_bwd_dkv(qkv, fk, do, lse, delta, dqkv, *, side=None, name):
    _, t, d = qkv.shape
    nh = d // FOX_HEAD
    b = _blk(FOX_BLK, t)
    nb = t // b
    dh = FOX_HEAD
    scale = dh ** -0.5

    def body(q_ref, k_ref, v_ref, f_ref, do_ref, lse_ref, dl_ref, dz_in, dkv_ref, df_ref):
        del dz_in
        kj = pl.program_id(1)
        kb, vb, fkv = k_ref[...], v_ref[...], f_ref[...]

        def step(qi, carry, masked):
            dk, dv, df = carry
            off = pl.multiple_of(qi * b, b)
            qb, dob = q_ref[pl.ds(off, b), :], do_ref[pl.ds(off, b), :]
            p = jnp.exp2(_fox_scores(_fox_q(qb), kb, fkv, masked) - lse_ref[pl.ds(off, b), :])
            dv = dv + lax.dot_general(p.astype(BF16), dob, _TN, preferred_element_type=F32)
            dp = lax.dot_general(dob, vb, _NT, preferred_element_type=F32)
            dsv = p * (dp - dl_ref[pl.ds(off, b), :])
            dk = dk + lax.dot_general(dsv.astype(BF16), qb, _TN, preferred_element_type=F32)
            return dk, dv, df - jnp.sum(dsv, axis=0, keepdims=True)

        init = (jnp.zeros((b, dh), F32), jnp.zeros((b, dh), F32), jnp.zeros((1, b), F32))
        carry = step(kj, init, True)
        dk, dv, df = lax.fori_loop(kj + 1, nb, lambda qi, c: step(qi, c, False), carry)
        dkv_ref[0] = (dk * scale).astype(BF16)
        dkv_ref[1] = dv.astype(BF16)
        df_ref[...] = df

    col = pl.BlockSpec((None, t, 1), lambda h, j: (h, 0, 0))
    return _call(
        body, side, name=name, grid=(nh, nb),
        in_specs=[pl.BlockSpec((None, t, dh), lambda h, j: (0, 0, h)),
                  pl.BlockSpec((None, b, dh), lambda h, j: (1, j, h)),
                  pl.BlockSpec((None, b, dh), lambda h, j: (2, j, h)),
                  pl.BlockSpec((None, None, 1, b), lambda h, j: (h, j, 0, 0)),
                  pl.BlockSpec((t, dh), lambda h, j: (0, h)), col, col, pl.BlockSpec(memory_space=pl.ANY)],
        out_specs=[pl.BlockSpec((2, b, dh), lambda h, j: (0, j, h)),
                   pl.BlockSpec((None, None, 1, b), lambda h, j: (h, j, 0, 0))],
        out_shape=[jax.ShapeDtypeStruct((3, t, d), BF16), jax.ShapeDtypeStruct((nh, nb, 1, b), F32)],
        aliases={7: 0}, sem=("parallel", "parallel"), args=(qkv, qkv, qkv, fk, do, lse, delta, dqkv))


FL_BLK = 256


def _log_sigmoid(x):
    return jnp.minimum(x, 0.0) - jnp.log(1.0 + jnp.exp(-jnp.abs(x)))


def fl_fwd(zf, bf, *, name):
    t, n = zf.shape
    bt = _blk(FL_BLK, t)

    def body(z_ref, b_ref, o_ref, carry):
        @pl.when(pl.program_id(0) == 0)
        def _():
            carry[...] = jnp.zeros_like(carry)

        ls = _log_sigmoid(z_ref[...] + b_ref[...])
        r = lax.broadcasted_iota(jnp.int32, (bt, bt), 0)
        c = lax.broadcasted_iota(jnp.int32, (bt, bt), 1)
        cs = jnp.dot((c <= r).astype(F32), ls, preferred_element_type=F32, precision=_HI) + carry[...]
        o_ref[...] = cs
        carry[...] = cs[bt - 1:bt, :]

    return pl.pallas_call(
        body, name=name, grid=(t // bt,),
        in_specs=[pl.BlockSpec((bt, n), lambda i: (i, 0)), pl.BlockSpec((1, n), lambda i: (0, 0))],
        out_specs=pl.BlockSpec((bt, n), lambda i: (i, 0)), out_shape=jax.ShapeDtypeStruct((t, n), F32),
        scratch_shapes=[pltpu.VMEM((1, n), F32)], compiler_params=_params(("arbitrary",)),
    )(zf, bf)


def fl_bwd(df, zf, bf, *, name):
    t, n = zf.shape
    bt = _blk(FL_BLK, t)
    nb = t // bt

    def body(df_ref, z_ref, b_ref, dz_ref, db_ref, carry):
        first = pl.program_id(0) == 0

        @pl.when(first)
        def _():
            carry[...] = jnp.zeros_like(carry)

        r = lax.broadcasted_iota(jnp.int32, (bt, bt), 0)
        c = lax.broadcasted_iota(jnp.int32, (bt, bt), 1)
        dls = jnp.dot((c >= r).astype(F32), df_ref[...], preferred_element_type=F32, precision=_HI) + carry[...]
        carry[...] = dls[0:1, :]
        dz = dls * (1.0 - _sigmoid(z_ref[...] + b_ref[...]))
        dz_ref[...] = dz.astype(BF16)
        part = jnp.sum(dz, axis=0, keepdims=True)

        @pl.when(first)
        def _():
            db_ref[...] = part

        @pl.when(jnp.logical_not(first))
        def _():
            db_ref[...] += part

    row = pl.BlockSpec((bt, n), lambda i: (nb - 1 - i, 0))
    vec = pl.BlockSpec((1, n), lambda i: (0, 0))
    return pl.pallas_call(
        body, name=name, grid=(nb,), in_specs=[row, row, vec], out_specs=[row, vec],
        out_shape=[jax.ShapeDtypeStruct((t, n), BF16), jax.ShapeDtypeStruct((1, n), F32)],
        scratch_shapes=[pltpu.VMEM((1, n), F32)], compiler_params=_params(("arbitrary",)),
    )(df, zf, bf)


def _adamw_math(w, g, m, v):
    m = ADAM_B1 * m + (1.0 - ADAM_B1) * g
    v = ADAM_B2 * v + (1.0 - ADAM_B2) * (g * g)
    m_hat = m / (1.0 - ADAM_B1 ** ADAM_STEP)
    v_hat = v / (1.0 - ADAM_B2 ** ADAM_STEP)
    delta = -ADAM_LR * (m_hat / (jnp.sqrt(v_hat) + ADAM_EPS) + ADAM_WD * w)
    return delta, m, v


def adamw(w, g, m, v, *, side=None, name):
    nl, r, c = w.shape
    br = _row_blk(r, 256)
    nb = r // br

    def body(w_ref, g_ref, m_ref, v_ref, go_ref, d_ref, mo_ref, vo_ref):
        gv = g_ref[...]
        go_ref[...] = gv
        d_ref[...], mo_ref[...], vo_ref[...] = _adamw_math(w_ref[...], gv, m_ref[...], v_ref[...])

    return _call(body, side, name=name, grid=(nl, nb), in_specs=[pl.BlockSpec((None, br, c), lambda l, i: (l, i, 0))] * 4,
                 out_specs=[pl.BlockSpec((br, c), lambda l, i: (l * nb + i, 0))] * 4,
                 out_shape=[jax.ShapeDtypeStruct((nl * r, c), F32)] * 4, sem=("parallel", "parallel"), args=(w, g, m, v))


def _f2(a):
    return a.reshape(a.shape[-2:])


def _local_step(x0, mem, tgt, sp, plan):
    t, d = x0.shape
    mix_a = sp["pool_scale"].shape[1]
    small = {}

    def row(a, l):
        return a[l:l + 1]

    def rows4(g):
        return g.reshape(N_CHIPS, -1, g.shape[-1])

    def xattn_f(l, xin):
        w = plan.weights(f"xa{l}")
        hx = rmsnorm_fwd(xin, row(sp["xa_norm"], l), name=f"xa_norm_f{l}")
        q = _f2(matmul(vm2(hx), w["wq"], "nn", out_dtype=BF16, side=plan.take_fwd(), name=f"xa_q_f{l}"))
        mn = rmsnorm_fwd(mem, row(sp["xa_mem_norm"], l), name=f"xa_memnorm_f{l}")
        kv = _f2(matmul(vm2(mn), w["wkv"], "nn", out_dtype=BF16, name=f"xa_kv_f{l}"))
        o = xattn_fwd(q, kv, name=f"xa_attn_f{l}")
        xout = _f2(matmul(vm2(o), w["wo"], "nn", out_dtype=F32, res=vm2(xin), side=plan.take_fwd(), name=f"xa_o_f{l}"))
        return xout, (xin, hx, q, mn, kv, o)

    def ffn_f(l, xin):
        w = plan.weights(f"ffn{l}")
        hf = rmsnorm_fwd(xin, row(sp["ffn_norm"], l), name=f"ffn_norm_f{l}")
        a = _f2(matmul(vm2(hf), w["gate"], "nn", out_dtype=BF16, side=plan.take_fwd(), name=f"ffn_gate_f{l}"))
        b, s = matmul(vm2(hf), w["up"], "nn", out_dtype=BF16, epi=(_swiglu_epi, [vm2(a)], 2), side=plan.take_fwd(), name=f"ffn_up_f{l}")
        b, s = _f2(b), _f2(s)
        xout = _f2(matmul(vm2(s), w["down"], "nn", out_dtype=F32, res=vm2(xin), side=plan.take_fwd(), name=f"ffn_down_f{l}"))
        return xout, (xin, hf, a, b, s)

    ev = plan.weights("ev")
    h0 = rmsnorm_fwd(x0, sp["ev_norm"], name="ev_norm_f")
    z = _f2(matmul(vm2(h0), ev["ev_in"], "nn", out_dtype=F32, side=plan.take_fwd(), name="ev_in_f"))
    cat = pool_fwd(z, ev["pool"], sp["pool_scale"], name="pool_f")
    cat, states = hgrn_fwd(z, cat, sp["lb"], sp["hg_gain"], mix_a, side=plan.take_fwd(), name="hgrn_f")
    x1 = _f2(matmul(VM(cat), ev["ev_out"], "nn", out_dtype=F32, res=vm2(x0), side=plan.take_fwd(), name="ev_out_f"))
    x2, xa0 = xattn_f(0, x1)
    x3, ff0 = ffn_f(0, x2)

    od = plan.weights("od")
    ho = rmsnorm_fwd(x3, sp["od_norm"], name="od_norm_f")
    qkv = matmul(vm2(ho), od["wqkv"], "nn", out_dtype=BF16, out_p=3, side=plan.take_fwd(), name="od_qkv_f")
    zf = _f2(matmul(vm2(ho), od["wf"], "nn", out_dtype=F32, name="od_fl_f"))
    fcum = fl_fwd(zf, sp["bf"], name="od_forget_f")
    nh = d // FOX_HEAD
    nfb = t // _blk(FOX_BLK, t)
    fk = fcum[:, :nh].T.reshape(nh, nfb, 1, t // nfb)
    of, lse = fox_fwd(qkv, fk, side=plan.take_fwd(), name="fox_f")
    x4 = _f2(matmul(vm2(of), od["od_out"], "nn", out_dtype=F32, res=vm2(x3), name="od_out_f"))
    x5, xa1 = xattn_f(1, x4)
    x6, ff1 = ffn_f(1, x5)
    loss, dx, dxb, small["final_norm"] = loss_head(x6, sp["final_norm"], tgt, name="loss_head")

    def ffn_b(l, saved, dx, dxb):
        xin, hf, a, b, s = saved
        w = plan.weights(f"ffn{l}")
        da, db = matmul(vm2(dxb), w["down"], "nt", out_dtype=BF16, epi=(_swiglu_bwd_epi, [vm2(a), vm2(b)], 2), side=plan.take_bwd(1), name=f"ffn_down_bx{l}")
        da, db = _f2(da), _f2(db)
        g_down = rows4(matmul(vm2(s), vm2(dxb), "tn", out_dtype=BF16, name=f"ffn_down_bw{l}"))
        g_gate = matmul(vm2(hf), vm2(da), "tn", out_dtype=BF16, out_p=N_CHIPS, name=f"ffn_gate_bw{l}")
        g_up = matmul(vm2(hf), vm2(db), "tn", out_dtype=BF16, out_p=N_CHIPS, name=f"ffn_up_bw{l}")
        plan.grads_done({f"down{l}": g_down, f"gate{l}": g_gate, f"up{l}": g_up})
        dh = matmul(vm2(da), w["gate"], "nt", out_dtype=F32, side=plan.take_bwd(1), name=f"ffn_gate_bx{l}")
        dh = _f2(matmul(vm2(db), w["up"], "nt", out_dtype=BF16, res=VM(dh), side=plan.take_bwd(), name=f"ffn_up_bx{l}"))
        dx, dxb, dg = rmsnorm_bwd(xin, row(sp["ffn_norm"], l), dh, dx, name=f"ffn_norm_b{l}")
        return dx, dxb, dg

    def xattn_b(l, saved, dx, dxb):
        xin, hx, q, mn, kv, o = saved
        w = plan.weights(f"xa{l}")
        do = _f2(matmul(vm2(dxb), w["wo"], "nt", out_dtype=BF16, side=plan.take_bwd(), name=f"xa_o_bx{l}"))
        g_wo = rows4(matmul(vm2(o), vm2(dxb), "tn", out_dtype=BF16, name=f"xa_o_bw{l}"))
        dq, dkv = xattn_bwd(q, kv, do, name=f"xa_attn_b{l}")
        g_wq = rows4(matmul(vm2(hx), vm2(dq), "tn", out_dtype=BF16, name=f"xa_q_bw{l}"))
        dh = _f2(matmul(vm2(dq), w["wq"], "nt", out_dtype=BF16, name=f"xa_q_bx{l}"))
        dkvb = dkv.astype(BF16)
        g_wkv = matmul(vm2(mn), vm2(dkvb), "tn", out_dtype=BF16, out_p=N_CHIPS, name=f"xa_kv_bw{l}")
        plan.grads_done({f"wo{l}": g_wo, f"wq{l}": g_wq, f"wkv{l}": g_wkv})
        dmn = _f2(matmul(vm2(dkvb), w["wkv"], "nt", out_dtype=F32, side=plan.take_bwd(), name=f"xa_kv_bx{l}"))
        (dgm,) = rmsnorm_bwd(mem, row(sp["xa_mem_norm"], l), dmn, None, name=f"xa_memnorm_b{l}")
        dx, dxb, dg = rmsnorm_bwd(xin, row(sp["xa_norm"], l), dh, dx, name=f"xa_norm_b{l}")
        return dx, dxb, dg, dgm

    dg_ffn, dg_xa, dg_mem = [None, None], [None, None], [None, None]
    dx, dxb, dg_ffn[1] = ffn_b(1, ff1, dx, dxb)
    dx, dxb, dg_xa[1], dg_mem[1] = xattn_b(1, xa1, dx, dxb)

    do = _f2(matmul(vm2(dxb), od["od_out"], "nt", out_dtype=BF16, side=plan.take_bwd(), name="od_out_bx"))
    g_od_out = rows4(matmul(vm2(of), vm2(dxb), "tn", out_dtype=BF16, name="od_out_bw"))
    dz3, delta = fox_bwd_dq(qkv, fk, do, lse, side=plan.take_bwd(1), name="fox_bq")
    dz3, dfk = fox_bwd_dkv(qkv, fk, do, lse, delta, dz3, side=plan.take_bwd(1), name="fox_bkv")
    dfc = jnp.pad(dfk.reshape(nh, t).T, ((0, 0), (0, zf.shape[1] - nh)))
    dzf, dbf = fl_bwd(dfc, zf, sp["bf"], name="od_forget_b")
    dqkv = VM(dz3, "cs", pfn=lambda p: lax.rem(p + 2, 3))
    dwqkv = _f2(matmul(vm2(ho), dqkv, "tn", out_dtype=BF16, name="od_qkv_bw"))
    dwf = _f2(matmul(vm2(ho), vm2(dzf), "tn", out_dtype=BF16, name="od_fl_bw"))
    od_in_full = jnp.concatenate([dwqkv, dwf[:, :nh]], axis=1)
    plan.grads_done({"od_out": g_od_out})
    plan.grads_done({"od_in": od_in_full.reshape(d, N_CHIPS, -1).transpose(1, 0, 2)})
    dh = matmul(dqkv, od["wqkv"], "nt", out_dtype=F32, side=plan.take_bwd(), name="od_qkv_bx")
    dh = _f2(matmul(vm2(dzf), od["wf"], "nt", out_dtype=BF16, res=VM(dh), name="od_fl_bx"))
    dx, dxb, small["od_norm"] = rmsnorm_bwd(x3, sp["od_norm"], dh, dx, name="od_norm_b")
    small["bf"] = dbf

    dx, dxb, dg_ffn[0] = ffn_b(0, ff0, dx, dxb)
    dx, dxb, dg_xa[0], dg_mem[0] = xattn_b(0, xa0, dx, dxb)

    dcat = matmul(vm2(dxb), ev["ev_out"], "nt", out_dtype=BF16, out_p=2, side=plan.take_bwd(), name="ev_out_bx")
    g_ev_out = rows4(matmul(VM(cat), vm2(dxb), "tn", out_dtype=BF16, name="ev_out_bw"))
    dz5, g_pool, small["pool_scale"] = pool_bwd(z, dcat, ev["pool"], sp["pool_scale"], name="pool_b")
    dz5, small["lb"], dgn = hgrn_bwd(z, dcat, dz5, states, sp["lb"], sp["hg_gain"], mix_a, side=plan.take_bwd(1), name="hgrn_b")
    small["hg_gain"] = jnp.sum(dgn, axis=0)
    dzv = VM(dz5, "cs", pfn=lambda p: lax.rem(p + 4, 5))
    g_ev_in = _f2(matmul(vm2(h0), dzv, "tn", out_dtype=BF16, side=plan.take_bwd(1), name="ev_in_bw"))
    g_ev_in = g_ev_in.reshape(d, N_CHIPS, -1).transpose(1, 0, 2)
    ng, gsz = g_pool.shape[0], g_pool.shape[1]
    pool_parts = g_pool.reshape(ng, N_CHIPS, gsz // N_CHIPS, gsz).transpose(1, 0, 2, 3).reshape(N_CHIPS, gsz, gsz).astype(BF16)
    plan.grads_done({"ev_out": g_ev_out, "pool": pool_parts, "ev_in": g_ev_in}, now=True)
    dh = _f2(matmul(dzv, ev["ev_in"], "nt", out_dtype=BF16, side=plan.take_bwd(1), name="ev_in_bx"))
    dx, _, small["ev_norm"] = rmsnorm_bwd(x0, sp["ev_norm"], dh, dx, name="ev_norm_b")

    small["xa_norm"] = jnp.concatenate(dg_xa, axis=0)
    small["xa_mem_norm"] = jnp.concatenate(dg_mem, axis=0)
    small["ffn_norm"] = jnp.concatenate(dg_ffn, axis=0)
    return loss, dx, small


def gather_forward(fulls, shards, *, name):
    n = len(fulls)
    side = ForwardSide(fulls, shards)

    def body(*refs):
        ins, outs = refs[:2 * n], refs[2 * n:3 * n]
        ssem, rsem = refs[3 * n:]
        side.start(ins, outs, ssem, rsem)
        side.finish(ins, outs, ssem, rsem)

    return pl.pallas_call(
        body, name=name, in_specs=[_ANY] * (2 * n), out_specs=[_ANY] * n, out_shape=side.out_shape,
        input_output_aliases=side.aliases,
        scratch_shapes=[pltpu.SemaphoreType.DMA((n, 4)), pltpu.SemaphoreType.DMA((n, 4))],
    )(*side.inputs)


def gather_shards(shards, *, name):
    n = len(shards)

    def body(*refs):
        ins, outs = refs[:n], refs[n:2 * n]
        ssem, rsem = refs[2 * n:]
        x, y, c, chips = _me()
        mine = _chip_id((x, y))
        sibling = (x, y, 1 - c)

        def rows(w, chip_id, which):
            h = shards[w].shape[0] // 2
            return outs[w].at[chip_id, pl.ds(which * h, h)]

        def to_chip(w, j):
            h = shards[w].shape[0] // 2
            return _rcopy(ins[w].at[pl.ds(c * h, h)], rows(w, mine, c), ssem.at[w, j], rsem.at[w, j], (*chips[j], c))

        def from_chip(w, j):
            r = rows(w, _chip_id(chips[j]), c)
            return _rcopy(r, r, ssem.at[w, j], rsem.at[w, j], (*chips[j], c))

        def to_sibling(w, j):
            r = rows(w, _chip_id(chips[j]), c)
            return _rcopy(r, r, ssem.at[w, 3 + j], rsem.at[w, 3 + j], sibling)

        def from_sibling(w, j):
            r = rows(w, _chip_id(chips[j]), 1 - c)
            return _rcopy(r, r, ssem.at[w, 3 + j], rsem.at[w, 3 + j], sibling)

        def own(w):
            return _rcopy(ins[w], outs[w].at[mine], ssem.at[w, 6], rsem.at[w, 6], sibling)

        for w in range(n):
            own(w).start()
            for j in range(3):
                to_chip(w, j).start()
        for w in range(n):
            for j in range(3):
                from_chip(w, j).wait_recv()
                to_sibling(w, j).start()
        for w in range(n):
            for j in range(3):
                from_sibling(w, j).wait_recv()
        for w in range(n):
            own(w).wait()
            for j in range(3):
                to_chip(w, j).wait_send()
                to_sibling(w, j).wait_send()

    return pl.pallas_call(
        body, name=name, in_specs=[_ANY] * n, out_specs=[_ANY] * n,
        out_shape=[jax.ShapeDtypeStruct((N_CHIPS,) + s.shape, s.dtype) for s in shards],
        scratch_shapes=[pltpu.SemaphoreType.DMA((n, 7)), pltpu.SemaphoreType.DMA((n, 7))],
    )(*shards)


def _ids_spec(grid, in_specs, out_specs):
    return pltpu.PrefetchScalarGridSpec(num_scalar_prefetch=1, grid=grid, in_specs=in_specs, out_specs=out_specs)


def rs_pair(parts, *, name):
    n = len(parts)

    def body(*refs):
        ins, recv = refs[:n], refs[n:2 * n]
        ssem, rsem = refs[2 * n:]
        x, y, c, _ = _me()
        sibling = (x, y, 1 - c)

        def swap(w):
            h = parts[w].shape[1] // 2
            return _rcopy(ins[w].at[:, pl.ds((1 - c) * h, h), :], recv[w], ssem.at[w], rsem.at[w], sibling)

        for w in range(n):
            swap(w).start()
        for w in range(n):
            swap(w).wait()

    return pl.pallas_call(
        body, name=name, in_specs=[_ANY] * n, out_specs=[_ANY] * n,
        out_shape=[jax.ShapeDtypeStruct((p.shape[0], p.shape[1] // 2, p.shape[2]), p.dtype) for p in parts],
        scratch_shapes=[pltpu.SemaphoreType.DMA((n,)), pltpu.SemaphoreType.DMA((n,))],
    )(*parts)


def add_pair(part, recv, ids, *, name):
    p, h, c = recv.shape
    br = _row_blk(h, 512)
    nb = h // br

    def body(ids_ref, a_ref, b_ref, o_ref):
        del ids_ref
        o_ref[...] = (a_ref[...].astype(F32) + b_ref[...].astype(F32)).astype(o_ref.dtype)

    half = pl.BlockSpec((None, br, c), lambda k, i, ids: (k, i, 0))
    return pl.pallas_call(
        body, name=name, out_shape=jax.ShapeDtypeStruct(recv.shape, recv.dtype),
        grid_spec=_ids_spec((p, nb), [pl.BlockSpec((None, br, c), lambda k, i, ids: (k, ids[1] * nb + i, 0)), half], half),
        compiler_params=_params(("parallel", "parallel")),
    )(ids, part, recv)


def rs_chip(sums, *, name):
    n = len(sums)

    def body(*refs):
        ins, outs = refs[:n], refs[n:2 * n]
        ssem, rsem = refs[2 * n:]
        x, y, c, chips = _me()

        def swap(w, j):
            return _rcopy(ins[w].at[_chip_id(chips[j])], outs[w].at[j], ssem.at[w, j], rsem.at[w, j], (*chips[j], c))

        for w in range(n):
            for j in range(3):
                swap(w, j).start()
        for w in range(n):
            for j in range(3):
                swap(w, j).wait()

    return pl.pallas_call(
        body, name=name, in_specs=[_ANY] * n, out_specs=[_ANY] * n,
        out_shape=[jax.ShapeDtypeStruct((3,) + s.shape[1:], s.dtype) for s in sums],
        scratch_shapes=[pltpu.SemaphoreType.DMA((n, 3)), pltpu.SemaphoreType.DMA((n, 3))],
    )(*sums)


def add_chips(sums, landed, ids, group, layer, group_shape, *, name):
    _, h, c = sums.shape
    br = _row_blk(h, 256)
    nb = h // br

    def body(ids_ref, a_ref, b_ref, *rest):
        o_ref = rest[-1]
        tot = a_ref[...].astype(F32)
        for k in range(3):
            tot = tot + b_ref[k].astype(F32)
        o_ref[...] = tot

    in_specs = [pl.BlockSpec((None, br, c), lambda i, ids: (ids[0], i, 0)), pl.BlockSpec((3, br, c), lambda i, ids: (0, i, 0))]
    args = [ids, sums, landed]
    if group is not None:
        in_specs.append(_ANY)
        args.append(group)
    return pl.pallas_call(
        body, name=name, out_shape=jax.ShapeDtypeStruct(group_shape, F32),
        input_output_aliases={3: 0} if group is not None else {},
        grid_spec=_ids_spec((nb,), in_specs, pl.BlockSpec((None, br, c), lambda i, ids: (layer, ids[1] * nb + i, 0))),
        compiler_params=_params(("parallel",)),
    )(*args)


def rs_share(groups, slots, *, name):
    ng = len(groups)
    n = len(slots)

    def body(*refs):
        outs = refs[ng:2 * ng]
        ssem, rsem = refs[2 * ng:]
        x, y, c, _ = _me()
        sibling = (x, y, 1 - c)

        def rows(w, which):
            g, l = slots[w]
            h = groups[g].shape[1] // 2
            return outs[g].at[l, pl.ds(which * h, h), :]

        def swap(w):
            return _rcopy(rows(w, c), rows(w, c), ssem.at[w], rsem.at[w], sibling)

        for w in range(n):
            swap(w).start()
        for w in range(n):
            swap(w).wait_send()
            _rcopy(rows(w, 1 - c), rows(w, 1 - c), ssem.at[w], rsem.at[w], sibling).wait_recv()

    return pl.pallas_call(
        body, name=name, in_specs=[_ANY] * ng, out_specs=[_ANY] * ng,
        out_shape=[jax.ShapeDtypeStruct(g.shape, g.dtype) for g in groups],
        input_output_aliases={g: g for g in range(ng)},
        scratch_shapes=[pltpu.SemaphoreType.DMA((n,)), pltpu.SemaphoreType.DMA((n,))],
    )(*groups)


def allreduce_small(v, *, name):
    r, c = v.shape
    ndev = 2 * N_CHIPS

    def body(v_ref, o_ref, buf, ssem, rsem):
        x, y, cc, _ = _me()
        me = 4 * x + 2 * y + cc
        flips = [(a, b, d) for a in (0, 1) for b in (0, 1) for d in (0, 1)][1:]
        buf[me] = v_ref[...]
        cps = []
        for k, (a, b, d) in enumerate(flips):
            peer = (jnp.bitwise_xor(x, a), jnp.bitwise_xor(y, b), jnp.bitwise_xor(cc, d))
            cp = _rcopy(v_ref, buf.at[me], ssem.at[k], rsem.at[k], peer)
            cp.start()
            cps.append(cp)
        for k, (a, b, d) in enumerate(flips):
            peer = (jnp.bitwise_xor(x, a), jnp.bitwise_xor(y, b), jnp.bitwise_xor(cc, d))
            src = 4 * peer[0] + 2 * peer[1] + peer[2]
            _rcopy(v_ref, buf.at[src], ssem.at[k], rsem.at[k], peer).wait_recv()
        for cp in cps:
            cp.wait_send()
        tot = buf[0]
        for k in range(1, ndev):
            tot = tot + buf[k]
        o_ref[...] = tot

    vm = pl.BlockSpec(memory_space=pltpu.VMEM)
    return pl.pallas_call(
        body, name=name, in_specs=[vm], out_specs=vm, out_shape=jax.ShapeDtypeStruct((r, c), F32),
        scratch_shapes=[pltpu.VMEM((ndev, r, c), F32), pltpu.SemaphoreType.DMA((ndev - 1,)), pltpu.SemaphoreType.DMA((ndev - 1,))],
    )(v)


WEIGHTS = ["lb_table", "ev_norm", "ev_w_in", "ev_w_pool", "ev_pool_scale", "ev_hg_norm", "ev_w_out", "od_norm", "od_w_in",
           "od_b_f", "od_w_out", "xa_norm", "xa_mem_norm", "xa_wq", "xa_wkv", "xa_wo", "ffn_norm", "ffn_w_gate", "ffn_w_up",
           "ffn_w_down", "final_norm"]
BIG = ["ev_w_in", "ev_w_pool", "ev_w_out", "od_w_in", "od_w_out", "xa_wq", "xa_wkv", "xa_wo", "ffn_w_gate", "ffn_w_up", "ffn_w_down"]
SMALL_ROWS = 16


def _rows(parts, width):
    rows = [jnp.pad(p.reshape(-1, p.shape[-1]).astype(F32), ((0, 0), (0, width - p.shape[-1]))) for p in parts]
    out = jnp.concatenate(rows, axis=0)
    return jnp.pad(out, ((0, SMALL_ROWS - out.shape[0]), (0, 0)))


def _unrows(packed, like):
    out, r = [], 0
    for p in like:
        n = p.size // p.shape[-1]
        out.append(packed[r:r + n, :p.shape[-1]].reshape(p.shape))
        r += n
    return out


def _m3(a):
    return a.reshape(a.shape[0], -1, a.shape[-1])


SLOT = {"ev_in": ("ev_w_in", 0), "pool": ("ev_w_pool", 0), "ev_out": ("ev_w_out", 0), "od_in": ("od_w_in", 0),
        "od_out": ("od_w_out", 0)}
for _l in range(2):
    SLOT.update({f"wq{_l}": ("xa_wq", _l), f"wkv{_l}": ("xa_wkv", _l), f"wo{_l}": ("xa_wo", _l),
                 f"gate{_l}": ("ffn_w_gate", _l), f"up{_l}": ("ffn_w_up", _l), f"down{_l}": ("ffn_w_down", _l)})
GATHER_FIRST = ["ev_in", "ev_out", "pool", "od_norm"]
GATHER_CARRIED = [["wq0", "wo0"], ["wkv0", "gate0"], ["od_out"], ["wq1"], ["wo1"], ["up0"], ["down0"], ["od_in"], ["wkv1"],
                  ["gate1", "up1", "down1"]]


class _Lazy:
    def __init__(self, plan, group):
        self.plan, self.layer = plan, group[-1] if group[-1] in "01" else ""

    def __getitem__(self, key):
        return self.plan.w(key + self.layer if key in ("wq", "wo", "wkv", "gate", "up", "down") else key)


class _Plan:
    def __init__(self, shards, ids, group_shapes, d, nh):
        self.shards, self.ids, self.group_shapes, self.d, self.nh = shards, ids, group_shapes, d, nh
        self.full, self.cache = {}, {}
        self.queue, self.sides, self.fsides, self.forwarded = [list(u) for u in GATHER_CARRIED], [], [], set()
        self.parts, self.psides, self.sums, self.rqueue, self.rsides = [], [], {}, [], []
        got = gather_shards([shards[n] for n in GATHER_FIRST], name="gather_first")
        for n, f in zip(GATHER_FIRST, got):
            self.full[n] = f

    def take_fwd(self):
        parts = []
        ready = [(ns, s) for ns, s in self.sides if s.outs is not None and ns[0] not in self.forwarded]
        for ns, s in ready:
            fs = ForwardSide(s.outs, [self.shards[n] for n in ns])
            self.fsides.append((ns, fs))
            self.forwarded.update(ns)
            parts.append(fs)
        if self.queue:
            names = self.queue.pop(0)
            side = GatherSide([self.shards[n] for n in names])
            self.sides.append((names, side))
            parts.append(side)
        return Sides(parts) if parts else None

    def _need(self, names):
        missing = [n for n in names if n not in self.full]
        if not missing:
            return
        done = {n: a for ns, s in self.fsides if s.outs is not None for n, a in zip(ns, s.outs)}
        landed = {n: a for ns, s in self.sides if s.outs is not None for n, a in zip(ns, s.outs)}
        pre = {n: done[n] for n in missing if n in done}
        half = [n for n in missing if n not in done and n in landed]
        late = [n for n in missing if n not in done and n not in landed]
        if half:
            self.forwarded.update(half)
            pre.update(zip(half, gather_forward([landed[n] for n in half], [self.shards[n] for n in half],
                                                name=f"gather_forward_{half[0]}")))
        if late:
            self.queue = [u for u in ([n for n in u if n not in late] for u in self.queue) if u]
            pre.update(zip(late, gather_shards([self.shards[n] for n in late], name=f"gather_late_{late[0]}")))
        for n in missing:
            self.full[n] = pre[n]

    def w(self, name):
        if name in self.cache:
            return self.cache[name]
        if name in ("wqkv", "wf"):
            self._need(["od_in"])
            od_full = self.full["od_in"].transpose(1, 0, 2).reshape(self.d, -1)
            self.cache["wqkv"] = vm2(od_full[:, :3 * self.d])
            self.cache["wf"] = vm2(jnp.pad(od_full[:, 3 * self.d:], ((0, 0), (0, 128 - self.nh))))
            return self.cache[name]
        self._need([name])
        f = self.full[name]
        if name == "pool":
            rows, gsz = f.shape[1:]
            ng = rows * N_CHIPS // gsz
            out = f.reshape(N_CHIPS, ng, gsz // N_CHIPS, gsz).transpose(1, 0, 2, 3).reshape(ng, gsz, gsz)
        elif name == "ev_in":
            out = vm2(f.transpose(1, 0, 2).reshape(self.d, -1))
        else:
            out = VM(f, "cs") if name.rstrip("01") in ("wkv", "gate", "up") else vm2(f.reshape(-1, f.shape[-1]))
        self.cache[name] = out
        return out

    def weights(self, group):
        return _Lazy(self, group)

    def grads_done(self, parts, now=False):
        names = list(parts)
        if now:
            got = rs_pair([parts[n] for n in names], name=f"reduce_pair_{names[0]}")
            for n, g in zip(names, got):
                self.sums[n] = add_pair(parts[n], g, self.ids, name=f"reduce_add2_{n}")
            self.rqueue.append(names)
        else:
            self.parts.append((names, [parts[n] for n in names]))

    def _add_swapped(self):
        for names, parts, side in self.psides:
            if side.outs is not None and names[0] not in self.sums:
                for n, p, g in zip(names, parts, side.outs):
                    self.sums[n] = add_pair(p, g, self.ids, name=f"reduce_add2_{n}")
                self.rqueue.append(names)

    def take_bwd(self, units=0):
        self._add_swapped()
        sides = []
        for names, parts in self.parts:
            ps = PairSide(parts)
            self.psides.append((names, parts, ps))
            sides.append(ps)
        self.parts = []
        names = [n for u in self.rqueue[:units] for n in u]
        self.rqueue = self.rqueue[units:]
        if names:
            rs = ReduceSide([self.sums[n] for n in names])
            self.rsides.append((names, rs))
            sides.append(rs)
        return Sides(sides) if sides else None

    def finish(self):
        for names, parts in self.parts:
            self.grads_done(dict(zip(names, parts)), now=True)
        self._add_swapped()
        landed = {}
        for ns, side in self.rsides:
            landed.update(zip(ns, side.outs))
        rest = [n for u in self.rqueue for n in u]
        if rest:
            landed.update(zip(rest, rs_chip([self.sums[n] for n in rest], name="reduce_chips_rest")))
        gbig = {n: None for n in BIG}
        for n, (big, l) in SLOT.items():
            gbig[big] = add_chips(self.sums[n], landed[n], self.ids, gbig[big], l, self.group_shapes[big], name=f"reduce_add4_{n}")
        full = rs_share([gbig[n] for n in BIG], [(BIG.index(big), l) for big, l in SLOT.values()], name="reduce_share")
        return dict(zip(BIG, full))


def kernel(x, mem, lb_table, ev_norm, ev_w_in, ev_w_pool, ev_pool_scale, ev_hg_norm, ev_w_out, od_norm, od_w_in, od_b_f, od_w_out, xa_norm, xa_mem_norm, xa_wq, xa_wkv, xa_wo, ffn_norm, ffn_w_gate, ffn_w_up, ffn_w_down, final_norm, loss_target, m_lb_table, m_ev_norm, m_ev_w_in, m_ev_w_pool, m_ev_pool_scale, m_ev_hg_norm, m_ev_w_out, m_od_norm, m_od_w_in, m_od_b_f, m_od_w_out, m_xa_norm, m_xa_mem_norm, m_xa_wq, m_xa_wkv, m_xa_wo, m_ffn_norm, m_ffn_w_gate, m_ffn_w_up, m_ffn_w_down, m_final_norm, v_lb_table, v_ev_norm, v_ev_w_in, v_ev_w_pool, v_ev_pool_scale, v_ev_hg_norm, v_ev_w_out, v_od_norm, v_od_w_in, v_od_b_f, v_od_w_out, v_xa_norm, v_xa_mem_norm, v_xa_wq, v_xa_wkv, v_xa_wo, v_ffn_norm, v_ffn_w_gate, v_ffn_w_up, v_ffn_w_down, v_final_norm):
    a = dict(locals())
    w = {n: a[n] for n in WEIGHTS}
    mom = {n: a["m_" + n] for n in WEIGHTS}
    var = {n: a["v_" + n] for n in WEIGHTS}
    _, t, d = x.shape
    nh = d // FOX_HEAD
    lanes = 128
    cx, cy = lax.axis_index("x"), lax.axis_index("y")
    chip = 2 * cx + cy

    w3 = {n: _m3(w[n]) for n in BIG}
    shards = {"od_norm": jnp.broadcast_to(od_norm, (16, od_norm.shape[1]))}
    for name, (big, l) in SLOT.items():
        shards[name] = w3[big][l].astype(BF16)
    ids = jnp.stack([chip, lax.axis_index("c")]).astype(jnp.int32)
    plan = _Plan(shards, ids, {n: w3[n].shape for n in BIG}, d, nh)
    od_norm_full = plan.full["od_norm"][:, 0, :].reshape(1, d)

    sm = jax.nn.softmax(lb_table, axis=0)
    sp = {
        "lb": sm[1:2], "ev_norm": ev_norm, "pool_scale": ev_pool_scale, "hg_gain": ev_hg_norm, "od_norm": od_norm_full,
        "bf": jnp.pad(od_b_f, ((0, 0), (0, lanes - nh))), "xa_norm": xa_norm, "xa_mem_norm": xa_mem_norm, "ffn_norm": ffn_norm,
        "final_norm": final_norm.reshape(1, d),
    }
    loss_l, gx, small = _local_step(x[0], mem[0], loss_target[0], sp, plan)
    loss = lax.psum(loss_l[0, 0], ("x", "y", "c"))
    gbig = plan.finish()

    raw_like = [small["lb"], small["ev_norm"], small["pool_scale"], small["hg_gain"], small["od_norm"], small["bf"],
                small["xa_norm"], small["xa_mem_norm"], small["ffn_norm"], small["final_norm"]]
    summed = _unrows(allreduce_small(_rows(raw_like, d), name="reduce_small"), raw_like)
    dlb, g_ev_norm, g_pool_scale, g_hg, g_od_norm_full, g_bf, g_xa, g_xam, g_ffn, g_final = summed
    dsm = jnp.zeros_like(sm).at[1:2].set(dlb)
    gsmall = {
        "lb_table": sm * (dsm - jnp.sum(sm * dsm, axis=0, keepdims=True)), "ev_norm": g_ev_norm, "ev_pool_scale": g_pool_scale,
        "ev_hg_norm": g_hg, "od_norm": lax.dynamic_slice_in_dim(g_od_norm_full, chip * od_norm.shape[1], od_norm.shape[1], axis=1),
        "od_b_f": g_bf[:, :nh], "xa_norm": g_xa, "xa_mem_norm": g_xam, "ffn_norm": g_ffn, "final_norm": g_final.reshape(d),
    }

    grad, delta, new_m, new_v = {}, {}, {}, {}
    for n in BIG:
        res = adamw(w3[n], gbig[n], _m3(mom[n]), _m3(var[n]), name=f"adamw_{n}")
        grad[n], delta[n], new_m[n], new_v[n] = [r.reshape(w[n].shape) for r in res]
    snames = [n for n in WEIGHTS if n not in BIG]
    like = [w[n] for n in snames]
    res = adamw(_rows(like, d)[None], _rows([gsmall[n] for n in snames], d)[None], _rows([mom[n] for n in snames], d)[None],
                _rows([var[n] for n in snames], d)[None], name="adamw_small")
    for vals, dst in zip(res, (grad, delta, new_m, new_v)):
        dst.update(zip(snames, _unrows(vals, like)))
    return (loss, gx.reshape(x.shape), *[grad[n] for n in WEIGHTS], *[delta[n] for n in WEIGHTS],
            *[new_m[n] for n in WEIGHTS], *[new_v[n] for n in WEIGHTS])
```
